```python
import jax, jax.numpy as jnp
from jax import lax
import numpy as np

D_MODEL = 1024
BATCH = 16
SEQ = 256
DEPTH = 2
DEC_BATCH = 2
DEC_SEQ = 2048
PAST_LEN = 256

GRID_W = 64
EPS = 1e-6
N_AB = (DEPTH + 1) // 2
N_C = DEPTH // 2
MLA_HEADS = 8
Q_RANK = 256
KV_RANK = 256
NOPE_DIM = 64
ROPE_DIM = 32
V_DIM = 64
ROPE_THETA = 10000.0
Q_BLOCK = 128
SSD_HEADS = 8
SSD_GROUPS = 2
SSD_HPG = SSD_HEADS // SSD_GROUPS
SSD_HEAD_DIM = 64
SSD_STATE = 128
D_SSD = SSD_HEADS * SSD_HEAD_DIM
CONV_W = 5
CONV_CH = D_SSD + 2 * SSD_GROUPS * SSD_STATE
CHUNK = 128
IN_SPLITS = (Q_RANK, KV_RANK, ROPE_DIM, D_SSD, CONV_CH, SSD_HEADS, SSD_HEADS)
IN_AB = sum(IN_SPLITS)
OUT_AB = MLA_HEADS * V_DIM + D_SSD
POOL_WINDOWS = (2, 4, 8, 16)
POOL_GC = D_MODEL // len(POOL_WINDOWS)
D_FF = ((8 * D_MODEL + 3 * 256 - 1) // (3 * 256)) * 256

kernel_name = "hybrid_mla_ssd_pool_diffusion_step"


def rmsnorm(x, g):
    xf = x.astype(jnp.float32)
    y = xf * lax.rsqrt(jnp.mean(xf * xf, axis=-1, keepdims=True) + EPS)
    return (y * g.astype(jnp.float32)).astype(x.dtype)


def split_last(x, sizes):
    idx = np.cumsum(np.array(sizes))[:-1].tolist()
    return jnp.split(x, idx, axis=-1)


def modulation(cvec, w_mod, b_mod):
    m = jnp.expand_dims(jax.nn.silu(cvec) @ w_mod + b_mod, -2)
    return split_last(m, (D_MODEL,) * 6)


def modulate(x, g, shift, scale):
    return rmsnorm(x, g) * (1 + scale) + shift


def axial_rope_tables(n_tokens):
    rows = n_tokens // GRID_W
    row = jnp.repeat(jnp.arange(rows, dtype=jnp.float32), GRID_W)
    col = jnp.tile(jnp.arange(GRID_W, dtype=jnp.float32), rows)
    half = ROPE_DIM // 2
    inv_freq = jnp.power(ROPE_THETA, -jnp.arange(0, half, 2, dtype=jnp.float32) / half)
    ang = jnp.concatenate([row[:, None] * inv_freq, col[:, None] * inv_freq], axis=-1)
    return jnp.cos(ang), jnp.sin(ang)


def apply_rope(x, cos, sin):
    xf = x.astype(jnp.float32)
    x1, x2 = xf[..., : ROPE_DIM // 2], xf[..., ROPE_DIM // 2:]
    return jnp.concatenate([x1 * cos - x2 * sin, x1 * sin + x2 * cos], axis=-1).astype(x.dtype)


def ab_project(h, w_in, q_norm, w_uq, kv_norm):
    b, L, _ = h.shape
    cq, ckv, k_pe, z, xbc, dt_f, dt_b = split_last(h @ w_in, IN_SPLITS)
    q = (rmsnorm(cq, q_norm) @ w_uq).reshape(b, L, MLA_HEADS, NOPE_DIM + ROPE_DIM)
    return q[..., :NOPE_DIM], q[..., NOPE_DIM:], rmsnorm(ckv, kv_norm), k_pe, z, xbc, dt_f, dt_b


def mla_expand_kv(ckv_n, w_ukv):
    b, L, _ = ckv_n.shape
    kv = (ckv_n @ w_ukv).reshape(b, L, MLA_HEADS, NOPE_DIM + V_DIM)
    return kv[..., :NOPE_DIM], kv[..., NOPE_DIM:]


def mla_attend(q_nope, q_pe, k_nope, k_pe, v):
    b, lq, h, _ = q_nope.shape
    nb = lq // Q_BLOCK
    scale = (NOPE_DIM + ROPE_DIM) ** -0.5

    def block(qs):
        qn, qp = qs
        s = jnp.einsum('bqhd,bkhd->bhqk', qn, k_nope) + jnp.einsum('bqhr,bkr->bhqk', qp, k_pe)
        p = jax.nn.softmax(s.astype(jnp.float32) * scale, axis=-1)
        return jnp.einsum('bhqk,bkhd->bqhd', p.astype(v.dtype), v)

    qn_b = q_nope.reshape(b, nb, Q_BLOCK, h, NOPE_DIM).transpose(1, 0, 2, 3, 4)
    qp_b = q_pe.reshape(b, nb, Q_BLOCK, h, ROPE_DIM).transpose(1, 0, 2, 3, 4)
    out = lax.map(block, (qn_b, qp_b))
    return out.transpose(1, 0, 2, 3, 4).reshape(b, lq, h * V_DIM)


def dwconv_centred(x, w, bias):
    y = lax.conv_general_dilated(x, w[:, None, :], window_strides=(1,),
                                 padding=[(CONV_W // 2, CONV_W // 2)],
                                 dimension_numbers=('NWC', 'WIO', 'NWC'),
                                 feature_group_count=x.shape[-1])
    return y + bias


def ssd_chunked(x, dt, A, Bm, Cm, h0):
    b, L, g, hg, p = x.shape
    n = Bm.shape[-1]
    nc = L // CHUNK
    f32 = jnp.float32
    dtf = dt.astype(f32)
    xdt = (x.astype(f32) * dtf[..., None]).reshape(b, nc, CHUNK, g, hg, p)
    a = (dtf * A.astype(f32)).reshape(b, nc, CHUNK, g, hg)
    Bc = Bm.astype(f32).reshape(b, nc, CHUNK, g, n)
    Cc = Cm.astype(f32).reshape(b, nc, CHUNK, g, n)
    acum = jnp.cumsum(a, axis=2)
    seg = acum[:, :, :, None] - acum[:, :, None, :]
    lower = jnp.tril(jnp.ones((CHUNK, CHUNK), dtype=bool))[:, :, None, None]
    decay = jnp.where(lower, jnp.exp(jnp.where(lower, seg, 0.0)), 0.0)
    cb = jnp.einsum('bcign,bcjgn->bcijg', Cc, Bc)
    y_diag = jnp.einsum('bcijg,bcijgh,bcjghp->bcighp', cb, decay, xdt)
    decay_end = jnp.exp(acum[:, :, -1:] - acum)
    chunk_states = jnp.einsum('bcjgn,bcjgh,bcjghp->bcghpn', Bc, decay_end, xdt)
    chunk_decay = jnp.exp(acum[:, :, -1])

    def step(h, inp):
        s, d = inp
        return d[..., None, None] * h + s, h

    h_final, h_prev = lax.scan(step, h0.astype(f32),
                               (jnp.moveaxis(chunk_states, 1, 0), jnp.moveaxis(chunk_decay, 1, 0)))
    h_prev = jnp.moveaxis(h_prev, 0, 1)
    y_off = jnp.einsum('bcign,bcghpn,bcigh->bcighp', Cc, h_prev, jnp.exp(acum))
    y = (y_diag + y_off).reshape(b, L, g, hg, p)
    return y.astype(x.dtype), h_final.astype(x.dtype)


def ssd_mixer(z, xbc, dt_f, dt_b, conv_w, conv_b, dt_bias_f, dt_bias_b, a_log_f, a_log_b,
              d_skip, norm_g, h0_f, h0_b):
    b, L, _ = z.shape
    xbc = jax.nn.silu(dwconv_centred(xbc, conv_w, conv_b))
    xs, Bm, Cm = split_last(xbc, (D_SSD, SSD_GROUPS * SSD_STATE, SSD_GROUPS * SSD_STATE))
    xs = xs.reshape(b, L, SSD_GROUPS, SSD_HPG, SSD_HEAD_DIM)
    Bm = Bm.reshape(b, L, SSD_GROUPS, SSD_STATE)
    Cm = Cm.reshape(b, L, SSD_GROUPS, SSD_STATE)

    def run_dir(dt_raw, dt_bias, a_log, h0, reverse):
        dt = jax.nn.softplus((dt_raw + dt_bias).astype(jnp.float32)).reshape(b, L, SSD_GROUPS, SSD_HPG)
        A = -jnp.exp(a_log.astype(jnp.float32)).reshape(SSD_GROUPS, SSD_HPG)
        xd, dd, Bd, Cd = xs, dt, Bm, Cm
        if reverse:
            xd, dd, Bd, Cd = jnp.flip(xd, 1), jnp.flip(dd, 1), jnp.flip(Bd, 1), jnp.flip(Cd, 1)
        y, hN = ssd_chunked(xd, dd, A, Bd, Cd,
                            h0.reshape(b, SSD_GROUPS, SSD_HPG, SSD_HEAD_DIM, SSD_STATE))
        if reverse:
            y = jnp.flip(y, 1)
        return y, hN.reshape(b, SSD_HEADS, SSD_HEAD_DIM, SSD_STATE)

    y_f, h_f = run_dir(dt_f, dt_bias_f, a_log_f, h0_f, False)
    y_b, h_b = run_dir(dt_b, dt_bias_b, a_log_b, h0_b, True)
    y = y_f + y_b + d_skip.reshape(SSD_GROUPS, SSD_HPG)[..., None] * xs
    y = y.reshape(b, L, D_SSD) * jax.nn.silu(z)
    y = rmsnorm(y.reshape(b, L, SSD_GROUPS, D_SSD // SSD_GROUPS),
                norm_g.reshape(SSD_GROUPS, D_SSD // SSD_GROUPS)).reshape(b, L, D_SSD)
    return y, h_f, h_b


def pool_mixer(h, w_pool, pool_scale):
    b, L, d = h.shape
    hf = h.astype(jnp.float32)
    cs = jnp.concatenate([jnp.zeros((b, 1, d), jnp.float32), jnp.cumsum(hf, axis=1)], axis=1)
    t = np.arange(L)
    outs = []
    for gi, w in enumerate(POOL_WINDOWS):
        lo = np.clip(t - w // 2, 0, L)
        hi = np.clip(t + w // 2, 0, L)
        cnt = jnp.asarray((hi - lo).astype(np.float32))[None, :, None]
        csg = cs[..., gi * POOL_GC:(gi + 1) * POOL_GC]
        mean = (jnp.take(csg, jnp.asarray(hi), axis=1) - jnp.take(csg, jnp.asarray(lo), axis=1)) / cnt
        outs.append(mean - hf[..., gi * POOL_GC:(gi + 1) * POOL_GC])
    pooled = jnp.stack(outs, axis=2).astype(h.dtype)
    out = jnp.einsum('blgc,gcd->blgd', pooled, w_pool).reshape(b, L, d)
    return out * pool_scale


def swiglu(h, w_gate, w_up, w_down):
    return (jax.nn.silu(h @ w_gate) * (h @ w_up)) @ w_down


def setup_inputs(seed: int = 0) -> dict:
    key = jax.random.key(seed)
    ks = jax.random.split(key, 40)
    nrm = jax.random.normal
    D = D_MODEL
    dt0 = jnp.exp(jax.random.uniform(ks[14], (N_AB, SSD_HEADS), minval=np.log(1e-3), maxval=np.log(1e-1)))
    dt1 = jnp.exp(jax.random.uniform(ks[15], (N_AB, SSD_HEADS), minval=np.log(1e-3), maxval=np.log(1e-1)))
    return {
        "x_prompt": nrm(ks[0], (BATCH, SEQ, D), jnp.float32),
        "x_sample": nrm(ks[1], (DEC_BATCH, DEC_SEQ, D), jnp.float32),
        "c": nrm(ks[2], (DEC_BATCH, D), jnp.float32),
        "cache_mla_ckv": nrm(ks[3], (DEC_BATCH, N_AB, PAST_LEN, KV_RANK), jnp.float32),
        "cache_mla_krope": nrm(ks[4], (DEC_BATCH, N_AB, PAST_LEN, ROPE_DIM), jnp.float32),
        "state_ssd_fwd": 0.1 * nrm(ks[5], (DEC_BATCH, N_AB, SSD_HEADS, SSD_HEAD_DIM, SSD_STATE), jnp.float32),
        "state_ssd_bwd": 0.1 * nrm(ks[6], (DEC_BATCH, N_AB, SSD_HEADS, SSD_HEAD_DIM, SSD_STATE), jnp.float32),
        "c_ctx": nrm(ks[7], (D,), jnp.float32),
        "w_mod": 0.5 * D ** -0.5 * nrm(ks[8], (DEPTH, D, 6 * D), jnp.float32),
        "b_mod": 0.01 * nrm(ks[9], (DEPTH, 6 * D), jnp.float32),
        "norm_pre_mix": 1.0 + 0.05 * nrm(ks[10], (DEPTH, D), jnp.float32),
        "norm_post_mix": 1.0 + 0.05 * nrm(ks[11], (DEPTH, D), jnp.float32),
        "norm_pre_ffn": 1.0 + 0.05 * nrm(ks[12], (DEPTH, D), jnp.float32),
        "norm_post_ffn": 1.0 + 0.05 * nrm(ks[13], (DEPTH, D), jnp.float32),
        "w_in_ab": D ** -0.5 * nrm(ks[16], (N_AB, D, IN_AB), jnp.float32),
        "q_norm": 1.0 + 0.05 * nrm(ks[17], (N_AB, Q_RANK), jnp.float32),
        "w_uq": Q_RANK ** -0.5 * nrm(ks[18], (N_AB, Q_RANK, MLA_HEADS * (NOPE_DIM + ROPE_DIM)), jnp.float32),
        "kv_norm": 1.0 + 0.05 * nrm(ks[19], (N_AB, KV_RANK), jnp.float32),
        "w_ukv": KV_RANK ** -0.5 * nrm(ks[20], (N_AB, KV_RANK, MLA_HEADS * (NOPE_DIM + V_DIM)), jnp.float32),
        "ssd_conv_w": CONV_W ** -0.5 * nrm(ks[21], (N_AB, CONV_W, CONV_CH), jnp.float32),
        "ssd_conv_b": 0.01 * nrm(ks[22], (N_AB, CONV_CH), jnp.float32),
        "ssd_dt_bias_fwd": jnp.log(jnp.expm1(dt0)),
        "ssd_dt_bias_bwd": jnp.log(jnp.expm1(dt1)),
        "ssd_a_log_fwd": jnp.log(jax.random.uniform(ks[23], (N_AB, SSD_HEADS), minval=1.0, maxval=16.0)),
        "ssd_a_log_bwd": jnp.log(jax.random.uniform(ks[24], (N_AB, SSD_HEADS), minval=1.0, maxval=16.0)),
        "ssd_d": 1.0 + 0.1 * nrm(ks[25], (N_AB, SSD_HEADS), jnp.float32),
        "ssd_norm": 1.0 + 0.05 * nrm(ks[26], (N_AB, D_SSD), jnp.float32),
        "w_out_ab": OUT_AB ** -0.5 * nrm(ks[27], (N_AB, OUT_AB, D), jnp.float32),
        "pool_w": POOL_GC ** -0.5 * nrm(ks[28], (N_C, len(POOL_WINDOWS), POOL_GC, POOL_GC), jnp.float32),
        "pool_scale": 1.0 + 0.05 * nrm(ks[29], (N_C, D), jnp.float32),
        "ffn_w_gate": D ** -0.5 * nrm(ks[30], (DEPTH, D, D_FF), jnp.float32),
        "ffn_w_up": D ** -0.5 * nrm(ks[31], (DEPTH, D, D_FF), jnp.float32),
        "ffn_w_down": D_FF ** -0.5 * nrm(ks[32], (DEPTH, D_FF, D), jnp.float32),
    }


def reference(x_prompt, x_sample, c, cache_mla_ckv, cache_mla_krope, state_ssd_fwd, state_ssd_bwd, c_ctx,
              w_mod, b_mod, norm_pre_mix, norm_post_mix, norm_pre_ffn, norm_post_ffn,
              w_in_ab, q_norm, w_uq, kv_norm, w_ukv, ssd_conv_w, ssd_conv_b,
              ssd_dt_bias_fwd, ssd_dt_bias_bwd, ssd_a_log_fwd, ssd_a_log_bwd, ssd_d, ssd_norm, w_out_ab,
              pool_w, pool_scale, ffn_w_gate, ffn_w_up, ffn_w_down):
    xp, xs = x_prompt, x_sample
    cos, sin = axial_rope_tables(x_sample.shape[1])
    new_ckv, new_kpe, new_hf, new_hb = [], [], [], []
    for l in range(DEPTH):
        sh_p, sc_p, g_p, shf_p, scf_p, gf_p = modulation(c_ctx, w_mod[l], b_mod[l])
        sh_s, sc_s, g_s, shf_s, scf_s, gf_s = modulation(c, w_mod[l], b_mod[l])
        hp = modulate(xp, norm_pre_mix[l], sh_p, sc_p)
        hs = modulate(xs, norm_pre_mix[l], sh_s, sc_s)
        if l % 2 == 0:
            i = l // 2
            ssd_args = (ssd_conv_w[i], ssd_conv_b[i], ssd_dt_bias_fwd[i], ssd_dt_bias_bwd[i],
                        ssd_a_log_fwd[i], ssd_a_log_bwd[i], ssd_d[i], ssd_norm[i])
            qn, qpe, ckv_n, kpe, z, xbc, dtf, dtb = ab_project(hp, w_in_ab[i], q_norm[i], w_uq[i], kv_norm[i])
            kn, v = mla_expand_kv(ckv_n, w_ukv[i])
            att_p = mla_attend(qn, qpe, kn, kpe, v)
            h_zero = jnp.zeros((xp.shape[0], SSD_HEADS, SSD_HEAD_DIM, SSD_STATE), xp.dtype)
            ssd_p, hf, hb = ssd_mixer(z, xbc, dtf, dtb, *ssd_args, h_zero, h_zero)
            mix_p = jnp.concatenate([att_p, ssd_p], axis=-1) @ w_out_ab[i]
            new_ckv.append(ckv_n)
            new_kpe.append(kpe)
            new_hf.append(hf)
            new_hb.append(hb)
            qn, qpe, ckv_s, kpe_s, z, xbc, dtf, dtb = ab_project(hs, w_in_ab[i], q_norm[i], w_uq[i], kv_norm[i])
            qpe = apply_rope(qpe, cos[:, None, :], sin[:, None, :])
            kpe_s = apply_rope(kpe_s, cos, sin)
            ckv_all = jnp.concatenate([cache_mla_ckv[:, i].astype(ckv_s.dtype), ckv_s], axis=1)
            kpe_all = jnp.concatenate([cache_mla_krope[:, i].astype(kpe_s.dtype), kpe_s], axis=1)
            kn, v = mla_expand_kv(ckv_all, w_ukv[i])
            att_s = mla_attend(qn, qpe, kn, kpe_all, v)
            ssd_s, _, _ = ssd_mixer(z, xbc, dtf, dtb, *ssd_args, state_ssd_fwd[:, i], state_ssd_bwd[:, i])
            mix_s = jnp.concatenate([att_s, ssd_s], axis=-1) @ w_out_ab[i]
        else:
            j = l // 2
            mix_p = pool_mixer(hp, pool_w[j], pool_scale[j])
            mix_s = pool_mixer(hs, pool_w[j], pool_scale[j])
        xp = xp + g_p * rmsnorm(mix_p, norm_post_mix[l])
        xs = xs + g_s * rmsnorm(mix_s, norm_post_mix[l])
        fp = swiglu(modulate(xp, norm_pre_ffn[l], shf_p, scf_p), ffn_w_gate[l], ffn_w_up[l], ffn_w_down[l])
        fs = swiglu(modulate(xs, norm_pre_ffn[l], shf_s, scf_s), ffn_w_gate[l], ffn_w_up[l], ffn_w_down[l])
        xp = xp + gf_p * rmsnorm(fp, norm_post_ffn[l])
        xs = xs + gf_s * rmsnorm(fs, norm_post_ffn[l])
    new_mla_ckv = jnp.stack(new_ckv, axis=1)
    new_mla_krope = jnp.stack(new_kpe, axis=1)
    new_ssd_fwd = jnp.stack(new_hf, axis=1)
    new_ssd_bwd = jnp.stack(new_hb, axis=1)
    return (xp, xs, new_mla_ckv, new_mla_krope, new_ssd_fwd, new_ssd_bwd)
```

```python
import functools

import numpy as np
import jax
import jax.numpy as jnp
from jax import lax
from jax.experimental import pallas as pl
from jax.experimental.pallas import tpu as pltpu

D_MODEL = 1024
BATCH = 16
SEQ = 256
DEPTH = 2
DEC_BATCH = 2
DEC_SEQ = 2048
PAST_LEN = 256
GRID_W = 64
EPS = 1e-6
MLA_HEADS = 8
Q_RANK = 256
KV_RANK = 256
NOPE_DIM = 64
ROPE_DIM = 32
V_DIM = 64
ROPE_THETA = 10000.0
SSD_HEADS = 8
SSD_GROUPS = 2
SSD_HPG = SSD_HEADS // SSD_GROUPS
SSD_HEAD_DIM = 64
SSD_STATE = 128
D_SSD = SSD_HEADS * SSD_HEAD_DIM
CONV_W = 5
CONV_CH = D_SSD + 2 * SSD_GROUPS * SSD_STATE
POOL_WINDOWS = (2, 4, 8, 16)
POOL_GC = D_MODEL // len(POOL_WINDOWS)
D_FF = ((8 * D_MODEL + 3 * 256 - 1) // (3 * 256)) * 256

SUBLANES = 8
LANES = 128

N_PROMPT = BATCH * SEQ
N_SAMPLE = DEC_BATCH * DEC_SEQ
N_TOK = N_PROMPT + N_SAMPLE
N_MODVEC = 1 + DEC_BATCH
TM = 256
TQ = 256
CHUNK = 128
HALO = SUBLANES
HEAD_SLAB = LANES
IN_COLS = Q_RANK + KV_RANK + D_SSD + CONV_CH + 2 * LANES
KPE_LANE0 = NOPE_DIM
VMEM_LIMIT = 56 * 1024 * 1024

assert SEQ == TM and DEC_SEQ % TM == 0 and N_PROMPT % DEC_SEQ == 0


def _rms(x, g):
    return x * lax.rsqrt(jnp.mean(x * x, axis=-1, keepdims=True) + EPS) * g


def _silu(x):
    return x * jax.nn.sigmoid(x)


def _softplus(x):
    return jnp.maximum(x, 0.0) + jnp.log1p(jnp.exp(-jnp.abs(x)))


def _dot(a, b):
    return jnp.dot(a, b, preferred_element_type=jnp.float32)


def _dot_nt(a, b):
    return lax.dot_general(a, b, (((1,), (1,)), ((), ())), preferred_element_type=jnp.float32)


def _dot_tn(a, b):
    return lax.dot_general(a, b, (((0,), (0,)), ((), ())), preferred_element_type=jnp.float32)


def _mod_row(i):
    return jnp.where(i < N_PROMPT // TM, 0, 1 + (i - N_PROMPT // TM) // (DEC_SEQ // TM))


def _const_spec(shape):
    nd = len(shape)
    return pl.BlockSpec(shape, lambda *_: (0,) * nd, pipeline_mode=pl.Buffered(1))


def _mod_kernel(ct_ref, w_ref, b_ref, o_ref):
    s = _silu(ct_ref[...])
    w = w_ref[0]
    b = b_ref[0]
    rows = [jnp.sum(s[:, v:v + 1] * w, axis=0, keepdims=True) + b for v in range(N_MODVEC)]
    rows.append(jnp.zeros((SUBLANES - N_MODVEC, w.shape[1]), jnp.float32))
    o_ref[0] = jnp.concatenate(rows, axis=0)


def _modulation(cvecs_t, w_mod, b_mod):
    tn = 1024
    nt = 6 * D_MODEL // tn
    return pl.pallas_call(
        _mod_kernel,
        grid=(DEPTH, nt),
        in_specs=[
            pl.BlockSpec((D_MODEL, SUBLANES), lambda l, j: (0, 0)),
            pl.BlockSpec((1, D_MODEL, tn), lambda l, j: (l, 0, j)),
            pl.BlockSpec((1, 1, tn), lambda l, j: (l, 0, j)),
        ],
        out_specs=pl.BlockSpec((1, SUBLANES, tn), lambda l, j: (l, 0, j)),
        out_shape=jax.ShapeDtypeStruct((DEPTH, SUBLANES, 6 * D_MODEL), jnp.float32),
        compiler_params=pltpu.CompilerParams(dimension_semantics=("arbitrary", "arbitrary")),
        name="modulation",
    )(cvecs_t, w_mod, b_mod.reshape(DEPTH, 1, 6 * D_MODEL))


def _inproj_kernel(x_ref, mod_ref, cos_ref, s1_ref, s2_ref, w_in_ref, w_uq_ref, qn_ref, kvn_ref, npm_ref,
                   q_ref, ckv_ref, kpe_ref, z_ref, xbc_ref, dt_ref):
    i = pl.program_id(0)
    is_latent = i >= N_PROMPT // TM
    sh = mod_ref[0, :, 0:D_MODEL]
    sc = mod_ref[0, :, D_MODEL:2 * D_MODEL]
    h = (_rms(x_ref[...], npm_ref[...]) * (1.0 + sc) + sh).astype(jnp.bfloat16)
    p = _dot(h, w_in_ref[...])
    o = 0
    cq = p[:, o:o + Q_RANK]; o += Q_RANK
    ckv = p[:, o:o + KV_RANK]; o += KV_RANK
    z_ref[...] = p[:, o:o + D_SSD]; o += D_SSD
    xbc_ref[...] = p[:, o:o + CONV_CH]; o += CONV_CH
    kpe = p[:, o:o + LANES]; o += LANES
    dt_ref[...] = p[:, o:o + LANES]

    cos = cos_ref[...]
    s1 = s1_ref[...]
    s2 = s2_ref[...]

    def rope(slab):
        r = slab * cos + pltpu.roll(slab, LANES - ROPE_DIM // 2, 1) * s1 + pltpu.roll(slab, ROPE_DIM // 2, 1) * s2
        return jnp.where(is_latent, r, slab)

    ckv_ref[...] = _rms(ckv, kvn_ref[...])
    kpe_ref[...] = rope(kpe)
    scale = (NOPE_DIM + ROPE_DIM) ** -0.5
    q = _dot(_rms(cq, qn_ref[...]).astype(jnp.bfloat16), w_uq_ref[...]) * scale
    for hd in range(MLA_HEADS):
        sl = slice(hd * HEAD_SLAB, (hd + 1) * HEAD_SLAB)
        q_ref[:, sl] = rope(q[:, sl]).astype(jnp.bfloat16)


def _inproj(xa, mod_l, tabs, w_in, w_uq, q_norm, kv_norm, npm):
    nt = N_TOK // TM
    tab_spec = pl.BlockSpec((TM, LANES), lambda i: (jnp.maximum(i - N_PROMPT // TM, 0) % (DEC_SEQ // TM), 0))
    row = lambda w: pl.BlockSpec((TM, w), lambda i: (i, 0))
    return pl.pallas_call(
        _inproj_kernel,
        grid=(nt,),
        in_specs=[
            row(D_MODEL),
            pl.BlockSpec((1, 1, 6 * D_MODEL), lambda i: (_mod_row(i), 0, 0)),
            tab_spec, tab_spec, tab_spec,
            _const_spec((D_MODEL, IN_COLS)),
            _const_spec((Q_RANK, MLA_HEADS * HEAD_SLAB)),
            _const_spec((1, Q_RANK)),
            _const_spec((1, KV_RANK)),
            _const_spec((1, D_MODEL)),
        ],
        out_specs=[row(MLA_HEADS * HEAD_SLAB), row(KV_RANK), row(LANES), row(D_SSD), row(CONV_CH), row(LANES)],
        out_shape=[
            jax.ShapeDtypeStruct((N_TOK, MLA_HEADS * HEAD_SLAB), jnp.bfloat16),
            jax.ShapeDtypeStruct((N_TOK, KV_RANK), jnp.float32),
            jax.ShapeDtypeStruct((N_TOK, LANES), jnp.float32),
            jax.ShapeDtypeStruct((N_TOK, D_SSD), jnp.float32),
            jax.ShapeDtypeStruct((N_TOK, CONV_CH), jnp.float32),
            jax.ShapeDtypeStruct((N_TOK, LANES), jnp.float32),
        ],
        compiler_params=pltpu.CompilerParams(dimension_semantics=("arbitrary",), vmem_limit_bytes=VMEM_LIMIT),
        name="inproj",
    )(xa, mod_l, *tabs, w_in, w_uq, q_norm, kv_norm, npm)


def _attn_kernel(*refs, lk_cache, lk_new):
    if lk_cache:
        q_ref, ckv_ref, kpe_ref, ckvc_ref, kpec_ref, w_ref, o_ref, k_scr, v_scr = refs
    else:
        q_ref, ckv_ref, kpe_ref, w_ref, o_ref, k_scr, v_scr = refs
    n_pair = MLA_HEADS // 2
    kcols = MLA_HEADS * HEAD_SLAB

    @pl.when(pl.program_id(1) == 0)
    def _expand_kv():
        def expand(ckv, kpe, r0):
            kv = _dot(ckv.astype(jnp.bfloat16), w_ref[...])
            rows = slice(r0, r0 + ckv.shape[0])
            for hd in range(MLA_HEADS):
                k_scr[hd, rows, :] = (kv[:, hd * HEAD_SLAB:(hd + 1) * HEAD_SLAB] + kpe).astype(jnp.bfloat16)
            for j in range(n_pair):
                v_scr[j, rows, :] = kv[:, kcols + j * LANES:kcols + (j + 1) * LANES].astype(jnp.bfloat16)

        step = 256
        for r0 in range(0, lk_cache, step):
            expand(ckvc_ref[0, r0:r0 + step, :], kpec_ref[0, r0:r0 + step, :], r0)
        for r0 in range(0, lk_new, step):
            expand(ckv_ref[r0:r0 + step, :], kpe_ref[r0:r0 + step, :], lk_cache + r0)

    low_half = lax.broadcasted_iota(jnp.int32, (TQ, LANES), 1) < V_DIM
    for j in range(n_pair):
        outs = []
        for hd in (2 * j, 2 * j + 1):
            s = _dot_nt(q_ref[:, hd * HEAD_SLAB:(hd + 1) * HEAD_SLAB], k_scr[hd])
            p = jnp.exp(s - jnp.max(s, axis=-1, keepdims=True))
            den = jnp.sum(p, axis=-1, keepdims=True)
            outs.append(_dot(p.astype(jnp.bfloat16), v_scr[j]) / den)
        o_ref[:, j * LANES:(j + 1) * LANES] = jnp.where(low_half, outs[0], outs[1]).astype(jnp.bfloat16)


def _attention(q, ckv_n, kpe, w_ukv, row_off, n_batch, seq, cache=None):
    nq = seq // TQ
    lk_cache = 0 if cache is None else cache[0].shape[1]
    lk = lk_cache + seq
    qblk = lambda b, qi: (row_off // TQ + b * nq + qi, 0)
    sblk = lambda b, qi: (row_off // seq + b, 0)
    in_specs = [
        pl.BlockSpec((TQ, MLA_HEADS * HEAD_SLAB), qblk),
        pl.BlockSpec((seq, KV_RANK), sblk),
        pl.BlockSpec((seq, LANES), sblk),
    ]
    args = [q, ckv_n, kpe]
    if cache is not None:
        in_specs += [pl.BlockSpec((1, lk_cache, KV_RANK), lambda b, qi: (b, 0, 0)),
                     pl.BlockSpec((1, lk_cache, LANES), lambda b, qi: (b, 0, 0))]
        args += list(cache)
    in_specs.append(_const_spec(w_ukv.shape))
    args.append(w_ukv)
    return pl.pallas_call(
        functools.partial(_attn_kernel, lk_cache=lk_cache, lk_new=seq),
        grid=(n_batch, nq),
        in_specs=in_specs,
        out_specs=pl.BlockSpec((TQ, MLA_HEADS * V_DIM), lambda b, qi: (b * nq + qi, 0)),
        out_shape=jax.ShapeDtypeStruct((n_batch * seq, MLA_HEADS * V_DIM), jnp.bfloat16),
        scratch_shapes=[pltpu.VMEM((MLA_HEADS, lk, HEAD_SLAB), jnp.bfloat16),
                        pltpu.VMEM((MLA_HEADS // 2, lk, LANES), jnp.bfloat16)],
        compiler_params=pltpu.CompilerParams(dimension_semantics=("arbitrary", "arbitrary"),
                                             vmem_limit_bytes=VMEM_LIMIT),
        name=f"attention_{seq}",
    )(*args)


def _expand_heads(x, lane0):
    rows = x.shape[0]
    low_half = lax.broadcasted_iota(jnp.int32, (rows, LANES), 1) < SSD_HEAD_DIM
    slabs = []
    for j in range(SSD_HEADS // 2):
        a = jnp.broadcast_to(x[:, lane0 + 2 * j:lane0 + 2 * j + 1], (rows, LANES))
        b = jnp.broadcast_to(x[:, lane0 + 2 * j + 1:lane0 + 2 * j + 2], (rows, LANES))
        slabs.append(jnp.where(low_half, a, b))
    return jnp.concatenate(slabs, axis=1)


def _exact_tri_dot(tri, a):
    a_hi = a.astype(jnp.bfloat16)
    r1 = a - a_hi.astype(jnp.float32)
    a_mid = r1.astype(jnp.bfloat16)
    a_lo = (r1 - a_mid.astype(jnp.float32)).astype(jnp.bfloat16)
    return _dot(tri, a_hi) + _dot(tri, a_mid) + _dot(tri, a_lo)


def _ssd_kernel(xbc_ref, dt_ref, h0f_ref, h0b_ref, cw_ref, cb_ref, dtb_ref, alog_ref, dsk_ref,
                y_ref, hf_ref, hb_ref, u_scr, stf_scr, stb_scr, *, seq):
    nc = seq // CHUNK
    gs = SSD_GROUPS * SSD_STATE

    def conv_chunk(c, carry):
        r0 = pl.multiple_of(c * CHUNK, CHUNK)
        cur = xbc_ref[pl.ds(r0, CHUNK), :]
        prev = xbc_ref[pl.ds(pl.multiple_of(jnp.maximum(r0 - HALO, 0), HALO), HALO), :]
        nxt = xbc_ref[pl.ds(pl.multiple_of(jnp.minimum(r0 + CHUNK, seq - HALO), HALO), HALO), :]
        prev = jnp.where(c > 0, prev, 0.0)
        nxt = jnp.where(c < nc - 1, nxt, 0.0)
        win = jnp.concatenate([prev, cur, nxt], axis=0)
        acc = jnp.broadcast_to(cb_ref[...], (CHUNK, CONV_CH))
        for k in range(CONV_W):
            lo = HALO - CONV_W // 2 + k
            acc = acc + cw_ref[k:k + 1, :] * win[lo:lo + CHUNK, :]
        u = _silu(acc)
        u_scr[pl.ds(r0, CHUNK), :] = u
        y_ref[pl.ds(r0, CHUNK), :] = dsk_ref[...] * u[:, :D_SSD]
        return carry

    lax.fori_loop(0, nc, conv_chunk, 0)

    stf_scr[...] = h0f_ref[0].T
    stb_scr[...] = h0b_ref[0].T

    row = lax.broadcasted_iota(jnp.int32, (CHUNK, CHUNK), 0)
    col = lax.broadcasted_iota(jnp.int32, (CHUNK, CHUNK), 1)
    low_half = col < SSD_HEAD_DIM
    a_neg = -jnp.exp(alog_ref[...])

    def scan_chunk(ci, st_scr, reverse):
        lane0 = SSD_HEADS if reverse else 0
        causal = (row <= col) if reverse else (row >= col)
        r0 = pl.multiple_of(ci * CHUNK, CHUNK)
        u = u_scr[pl.ds(r0, CHUNK), :]
        xs = u[:, :D_SSD]
        dts = _softplus(dt_ref[pl.ds(r0, CHUNK), :] + dtb_ref[...])
        cum = _exact_tri_dot(causal.astype(jnp.bfloat16), dts * a_neg)
        cum_t = cum.T
        last = 0 if reverse else CHUNK - 1
        tot = cum[last:last + 1, :]
        ea_x = _expand_heads(jnp.exp(cum), lane0)
        de_x = _expand_heads(jnp.exp(tot - cum), lane0)
        cd_x = ea_x[last:last + 1, :]
        xdt = xs * _expand_heads(dts, lane0)
        xdt_b = xdt.astype(jnp.bfloat16)
        xde_b = (xdt * de_x).astype(jnp.bfloat16)
        st = st_scr[...]
        y_parts = []
        st_parts = []
        for g in range(SSD_GROUPS):
            bg = u[:, D_SSD + g * SSD_STATE:D_SSD + (g + 1) * SSD_STATE].astype(jnp.bfloat16)
            cg = u[:, D_SSD + gs + g * SSD_STATE:D_SSD + gs + (g + 1) * SSD_STATE].astype(jnp.bfloat16)
            cbm = _dot_nt(cg, bg)
            gcols = slice(g * SSD_HPG * SSD_HEAD_DIM, (g + 1) * SSD_HPG * SSD_HEAD_DIM)
            y_off = _dot(cg, st[:, gcols].astype(jnp.bfloat16)) * ea_x[:, gcols]
            for jj in range(SSD_HPG // 2):
                pair = g * (SSD_HPG // 2) + jj
                xp = xdt_b[:, pair * LANES:(pair + 1) * LANES]
                outs = []
                for hd in (2 * pair, 2 * pair + 1):
                    seg = cum[:, lane0 + hd:lane0 + hd + 1] - cum_t[lane0 + hd:lane0 + hd + 1, :]
                    m = (cbm * jnp.exp(jnp.where(causal, seg, -jnp.inf))).astype(jnp.bfloat16)
                    outs.append(_dot(m, xp))
                y_parts.append(jnp.where(low_half, outs[0], outs[1])
                               + y_off[:, jj * LANES:(jj + 1) * LANES])
            st_parts.append(_dot_tn(bg, xde_b[:, gcols]))
        y_ref[pl.ds(r0, CHUNK), :] += jnp.concatenate(y_parts, axis=1)
        st_scr[...] = cd_x * st + jnp.concatenate(st_parts, axis=1)

    def both(c, carry):
        scan_chunk(c, stf_scr, False)
        scan_chunk(nc - 1 - c, stb_scr, True)
        return carry

    lax.fori_loop(0, nc, both, 0)
    hf_ref[0] = stf_scr[...].T
    hb_ref[0] = stb_scr[...].T


def _ssd(xbc, dt, h0f, h0b, conv_w, conv_b, dt_bias, a_log, d_skip, row_off, n_batch, seq):
    hp = SSD_HEADS * SSD_HEAD_DIM
    sblk = lambda w: pl.BlockSpec((seq, w), lambda b: (row_off // seq + b, 0))
    st_spec = pl.BlockSpec((1, hp, SSD_STATE), lambda b: (b, 0, 0))
    st_shape = jax.ShapeDtypeStruct((n_batch, hp, SSD_STATE), jnp.float32)
    return pl.pallas_call(
        functools.partial(_ssd_kernel, seq=seq),
        grid=(n_batch,),
        in_specs=[sblk(CONV_CH), sblk(LANES), st_spec, st_spec,
                  _const_spec((SUBLANES, CONV_CH)), _const_spec((1, CONV_CH)),
                  _const_spec((1, LANES)), _const_spec((1, LANES)), _const_spec((1, D_SSD))],
        out_specs=[pl.BlockSpec((seq, D_SSD), lambda b: (b, 0)), st_spec, st_spec],
        out_shape=[jax.ShapeDtypeStruct((n_batch * seq, D_SSD), jnp.float32), st_shape, st_shape],
        scratch_shapes=[pltpu.VMEM((seq, CONV_CH), jnp.float32),
                        pltpu.VMEM((SSD_STATE, hp), jnp.float32),
                        pltpu.VMEM((SSD_STATE, hp), jnp.float32)],
        compiler_params=pltpu.CompilerParams(dimension_semantics=("arbitrary",), vmem_limit_bytes=VMEM_LIMIT),
        name=f"ssd_{seq}",
    )(xbc, dt, h0f, h0b, conv_w, conv_b, dt_bias, a_log, d_skip)


def _post_ffn(x, mix, mod_ref, npost_ref, npre_ref, nffn_ref, wg_ref, wu_ref, wd_ref):
    d = D_MODEL
    gate_mix = mod_ref[0, :, 2 * d:3 * d]
    shf = mod_ref[0, :, 3 * d:4 * d]
    scf = mod_ref[0, :, 4 * d:5 * d]
    gate_ffn = mod_ref[0, :, 5 * d:6 * d]
    x1 = x + gate_mix * _rms(mix, npost_ref[...])
    h = (_rms(x1, npre_ref[...]) * (1.0 + scf) + shf).astype(jnp.bfloat16)
    hid = (_silu(_dot(h, wg_ref[...])) * _dot(h, wu_ref[...])).astype(jnp.bfloat16)
    f = _dot(hid, wd_ref[...])
    return x1 + gate_ffn * _rms(f, nffn_ref[...])


def _ffn_specs():
    return [_const_spec((1, D_MODEL)), _const_spec((1, D_MODEL)), _const_spec((1, D_MODEL)),
            _const_spec((D_MODEL, D_FF)), _const_spec((D_MODEL, D_FF)), _const_spec((D_FF, D_MODEL))]


def _outproj_ffn_kernel(x_ref, att_ref, y_ref, z_ref, mod_ref, sn_ref, wo_ref,
                        npost_ref, npre_ref, nffn_ref, wg_ref, wu_ref, wd_ref, o_ref):
    yg = y_ref[...] * _silu(z_ref[...])
    gw = D_SSD // SSD_GROUPS
    parts = [att_ref[...]]
    for g in range(SSD_GROUPS):
        parts.append(_rms(yg[:, g * gw:(g + 1) * gw], sn_ref[:, g * gw:(g + 1) * gw]).astype(jnp.bfloat16))
    mix = _dot(jnp.concatenate(parts, axis=1), wo_ref[...])
    o_ref[...] = _post_ffn(x_ref[...], mix, mod_ref, npost_ref, npre_ref, nffn_ref, wg_ref, wu_ref, wd_ref)


def _outproj_ffn(xa, att, y, z, mod_l, ssd_norm, w_out, npost, npre, nffn, wg, wu, wd):
    row = lambda w: pl.BlockSpec((TM, w), lambda i: (i, 0))
    return pl.pallas_call(
        _outproj_ffn_kernel,
        grid=(N_TOK // TM,),
        in_specs=[row(D_MODEL), row(MLA_HEADS * V_DIM), row(D_SSD), row(D_SSD),
                  pl.BlockSpec((1, 1, 6 * D_MODEL), lambda i: (_mod_row(i), 0, 0)),
                  _const_spec((1, D_SSD)), _const_spec((MLA_HEADS * V_DIM + D_SSD, D_MODEL))] + _ffn_specs(),
        out_specs=row(D_MODEL),
        out_shape=jax.ShapeDtypeStruct((N_TOK, D_MODEL), jnp.float32),
        compiler_params=pltpu.CompilerParams(dimension_semantics=("arbitrary",), vmem_limit_bytes=VMEM_LIMIT),
        name="outproj_ffn",
    )(xa, att, y, z, mod_l, ssd_norm, w_out, npost, npre, nffn, wg, wu, wd)


def _pool_ffn_kernel(x_ref, xp_ref, xn_ref, mod_ref, nmix_ref, pw_ref, ps_ref,
                     npost_ref, npre_ref, nffn_ref, wg_ref, wu_ref, wd_ref, o_ref):
    i = pl.program_id(0)
    seq = jnp.where(i < N_PROMPT // TM, SEQ, DEC_SEQ)
    pos0 = (i * TM) % seq
    sh = mod_ref[0, :, 0:D_MODEL]
    sc = mod_ref[0, :, D_MODEL:2 * D_MODEL]
    hmod = lambda v: _rms(v, nmix_ref[...]) * (1.0 + sc) + sh
    x = x_ref[...]
    h = hmod(x)
    h_prev = jnp.where(pos0 > 0, hmod(xp_ref[...]), 0.0)
    h_next = jnp.where(pos0 + TM < seq, hmod(xn_ref[...]), 0.0)
    padded = jnp.concatenate([h_prev, h, h_next], axis=0)
    n_rows = TM + 2 * HALO
    pos = pos0 + lax.broadcasted_iota(jnp.int32, (TM, 1), 0)

    def shifted(v, s):
        return pltpu.roll(v, n_rows - s, 0)

    outs = []
    for gi, w in enumerate(POOL_WINDOWS):
        cols = slice(gi * POOL_GC, (gi + 1) * POOL_GC)
        t = padded[:, cols]
        span = 1
        while span < w:
            t = t + shifted(t, span)
            span *= 2
        lead = HALO - w // 2
        win_sum = (shifted(t, lead) if lead else t)[:TM, :]
        cnt = (jnp.minimum(pos + w // 2, seq) - jnp.maximum(pos - w // 2, 0)).astype(jnp.float32)
        pooled = (win_sum / cnt - h[:, cols]).astype(jnp.bfloat16)
        outs.append(_dot(pooled, pw_ref[gi]))
    mix = jnp.concatenate(outs, axis=1) * ps_ref[...]
    o_ref[...] = _post_ffn(x, mix, mod_ref, npost_ref, npre_ref, nffn_ref, wg_ref, wu_ref, wd_ref)


def _pool_ffn(xa, mod_l, nmix, pool_w, pool_scale, npost, npre, nffn, wg, wu, wd):
    hb = TM // HALO
    nh = N_TOK // HALO
    return pl.pallas_call(
        _pool_ffn_kernel,
        grid=(N_TOK // TM,),
        in_specs=[pl.BlockSpec((TM, D_MODEL), lambda i: (i, 0)),
                  pl.BlockSpec((HALO, D_MODEL), lambda i: (jnp.maximum(i * hb - 1, 0), 0)),
                  pl.BlockSpec((HALO, D_MODEL), lambda i: (jnp.minimum((i + 1) * hb, nh - 1), 0)),
                  pl.BlockSpec((1, 1, 6 * D_MODEL), lambda i: (_mod_row(i), 0, 0)),
                  _const_spec((1, D_MODEL)),
                  _const_spec((len(POOL_WINDOWS), POOL_GC, POOL_GC)),
                  _const_spec((1, D_MODEL))] + _ffn_specs(),
        out_specs=pl.BlockSpec((TM, D_MODEL), lambda i: (i, 0)),
        out_shape=jax.ShapeDtypeStruct((N_TOK, D_MODEL), jnp.float32),
        compiler_params=pltpu.CompilerParams(dimension_semantics=("arbitrary",), vmem_limit_bytes=VMEM_LIMIT),
        name="pool_ffn",
    )(xa, xa, xa, mod_l, nmix, pool_w, pool_scale, npost, npre, nffn, wg, wu, wd)


def _rope_tables():
    rows = DEC_SEQ // GRID_W
    r = jnp.repeat(jnp.arange(rows, dtype=jnp.float32), GRID_W)
    c = jnp.tile(jnp.arange(GRID_W, dtype=jnp.float32), rows)
    half = ROPE_DIM // 2
    inv_freq = jnp.power(ROPE_THETA, -jnp.arange(0, half, 2, dtype=jnp.float32) / half)
    ang = jnp.concatenate([r[:, None] * inv_freq, c[:, None] * inv_freq], axis=-1)
    cos, sin = jnp.cos(ang), jnp.sin(ang)
    zl = jnp.zeros((DEC_SEQ, KPE_LANE0), jnp.float32)
    zr = jnp.zeros((DEC_SEQ, LANES - KPE_LANE0 - ROPE_DIM), jnp.float32)
    zh = jnp.zeros((DEC_SEQ, half), jnp.float32)
    cos_t = jnp.concatenate([zl + 1.0, cos, cos, zr + 1.0], axis=1)
    s1_t = jnp.concatenate([zl, -sin, zh, zr], axis=1)
    s2_t = jnp.concatenate([zl, zh, sin, zr], axis=1)
    return cos_t, s1_t, s2_t


def _kpe_slab(k):
    pad = [(0, 0)] * (k.ndim - 1) + [(KPE_LANE0, LANES - KPE_LANE0 - ROPE_DIM)]
    return jnp.pad(k, pad)


def _layout_in_proj(w):
    o = np.cumsum((0, Q_RANK, KV_RANK, ROPE_DIM, D_SSD, CONV_CH, SSD_HEADS, SSD_HEADS))
    cq, ckv, kpe, z, xbc, dtf, dtb = (w[:, o[k]:o[k + 1]] for k in range(7))
    dt = jnp.pad(jnp.concatenate([dtf, dtb], axis=1), ((0, 0), (0, LANES - 2 * SSD_HEADS)))
    return jnp.concatenate([cq, ckv, z, xbc, _kpe_slab(kpe), dt], axis=1).astype(jnp.bfloat16)


def _layout_uq(w):
    w = w.reshape(Q_RANK, MLA_HEADS, NOPE_DIM + ROPE_DIM)
    w = jnp.pad(w, ((0, 0), (0, 0), (0, HEAD_SLAB - NOPE_DIM - ROPE_DIM)))
    return w.reshape(Q_RANK, MLA_HEADS * HEAD_SLAB).astype(jnp.bfloat16)


def _layout_ukv(w):
    w = w.reshape(KV_RANK, MLA_HEADS, NOPE_DIM + V_DIM)
    kn = jnp.pad(w[:, :, :NOPE_DIM], ((0, 0), (0, 0), (0, HEAD_SLAB - NOPE_DIM)))
    v = w[:, :, NOPE_DIM:]
    return jnp.concatenate([kn.reshape(KV_RANK, -1), v.reshape(KV_RANK, -1)], axis=1).astype(jnp.bfloat16)


def _lane_row(fwd, bwd):
    return jnp.pad(jnp.concatenate([fwd, bwd]), (0, LANES - 2 * SSD_HEADS)).reshape(1, LANES)


def kernel(x_prompt, x_sample, c, cache_mla_ckv, cache_mla_krope, state_ssd_fwd, state_ssd_bwd, c_ctx, w_mod, b_mod, norm_pre_mix, norm_post_mix, norm_pre_ffn, norm_post_ffn, w_in_ab, q_norm, w_uq, kv_norm, w_ukv, ssd_conv_w, ssd_conv_b, ssd_dt_bias_fwd, ssd_dt_bias_bwd, ssd_a_log_fwd, ssd_a_log_bwd, ssd_d, ssd_norm, w_out_ab, pool_w, pool_scale, ffn_w_gate, ffn_w_up, ffn_w_down):
    f32, bf16 = jnp.float32, jnp.bfloat16
    xa = jnp.concatenate([x_prompt.reshape(N_PROMPT, D_MODEL), x_sample.reshape(N_SAMPLE, D_MODEL)], axis=0)
    cvecs = jnp.concatenate([c_ctx[None, :], c, jnp.zeros((SUBLANES - N_MODVEC, D_MODEL), f32)], axis=0)
    mod = _modulation(cvecs.T, w_mod, b_mod).reshape(DEPTH, SUBLANES, 1, 6 * D_MODEL)
    tabs = _rope_tables()
    hp = SSD_HEADS * SSD_HEAD_DIM
    row = lambda v: v.reshape(1, -1)
    new_ckv, new_kpe, new_hf, new_hb = [], [], [], []

    for l in range(DEPTH):
        ffn = (row(norm_post_mix[l]), row(norm_pre_ffn[l]), row(norm_post_ffn[l]),
               ffn_w_gate[l].astype(bf16), ffn_w_up[l].astype(bf16), ffn_w_down[l].astype(bf16))
        if l % 2 == 0:
            i = l // 2
            q, ckv_n, kpe, z, xbc, dt = _inproj(
                xa, mod[l], tabs, _layout_in_proj(w_in_ab[i]), _layout_uq(w_uq[i]),
                row(q_norm[i]), row(kv_norm[i]), row(norm_pre_mix[l]))
            w_kv = _layout_ukv(w_ukv[i])
            att_p = _attention(q, ckv_n, kpe, w_kv, 0, BATCH, SEQ)
            att_s = _attention(q, ckv_n, kpe, w_kv, N_PROMPT, DEC_BATCH, DEC_SEQ,
                               cache=(cache_mla_ckv[:, i], _kpe_slab(cache_mla_krope[:, i])))
            ssd_args = (jnp.pad(ssd_conv_w[i], ((0, SUBLANES - CONV_W), (0, 0))), row(ssd_conv_b[i]),
                        _lane_row(ssd_dt_bias_fwd[i], ssd_dt_bias_bwd[i]),
                        _lane_row(ssd_a_log_fwd[i], ssd_a_log_bwd[i]),
                        row(jnp.repeat(ssd_d[i], SSD_HEAD_DIM)))
            h_zero = jnp.zeros((BATCH, hp, SSD_STATE), f32)
            y_p, hf, hb = _ssd(xbc, dt, h_zero, h_zero, *ssd_args, 0, BATCH, SEQ)
            y_s, _, _ = _ssd(xbc, dt, state_ssd_fwd[:, i].reshape(DEC_BATCH, hp, SSD_STATE),
                             state_ssd_bwd[:, i].reshape(DEC_BATCH, hp, SSD_STATE),
                             *ssd_args, N_PROMPT, DEC_BATCH, DEC_SEQ)
            xa = _outproj_ffn(xa, jnp.concatenate([att_p, att_s], axis=0), jnp.concatenate([y_p, y_s], axis=0), z,
                              mod[l], row(ssd_norm[i]), w_out_ab[i].astype(bf16), *ffn)
            new_ckv.append(ckv_n[:N_PROMPT].reshape(BATCH, SEQ, KV_RANK))
            new_kpe.append(kpe[:N_PROMPT, KPE_LANE0:KPE_LANE0 + ROPE_DIM].reshape(BATCH, SEQ, ROPE_DIM))
            new_hf.append(hf.reshape(BATCH, SSD_HEADS, SSD_HEAD_DIM, SSD_STATE))
            new_hb.append(hb.reshape(BATCH, SSD_HEADS, SSD_HEAD_DIM, SSD_STATE))
        else:
            j = l // 2
            xa = _pool_ffn(xa, mod[l], row(norm_pre_mix[l]), pool_w[j].astype(bf16), row(pool_scale[j]), *ffn)

    return (xa[:N_PROMPT].reshape(BATCH, SEQ, D_MODEL), xa[N_PROMPT:].reshape(DEC_BATCH, DEC_SEQ, D_MODEL),
            jnp.stack(new_ckv, axis=1), jnp.stack(new_kpe, axis=1),
            jnp.stack(new_hf, axis=1), jnp.stack(new_hb, axis=1))
```

```python
import functools

import numpy as np
import jax
import jax.numpy as jnp
from jax import lax
from jax.experimental import pallas as pl
from jax.experimental.pallas import tpu as pltpu

D_MODEL = 1024
BATCH = 16
SEQ = 256
DEPTH = 2
DEC_BATCH = 2
DEC_SEQ = 2048
PAST_LEN = 256
GRID_W = 64
EPS = 1e-6
MLA_HEADS = 8
Q_RANK = 256
KV_RANK = 256
NOPE_DIM = 64
ROPE_DIM = 32
V_DIM = 64
ROPE_THETA = 10000.0
SSD_HEADS = 8
SSD_GROUPS = 2
SSD_HPG = SSD_HEADS // SSD_GROUPS
SSD_HEAD_DIM = 64
SSD_STATE = 128
D_SSD = SSD_HEADS * SSD_HEAD_DIM
CONV_W = 5
CONV_CH = D_SSD + 2 * SSD_GROUPS * SSD_STATE
POOL_WINDOWS = (2, 4, 8, 16)
POOL_GC = D_MODEL // len(POOL_WINDOWS)
D_FF = ((8 * D_MODEL + 3 * 256 - 1) // (3 * 256)) * 256

SUBLANES = 8
LANES = 128

N_PROMPT = BATCH * SEQ
N_SAMPLE = DEC_BATCH * DEC_SEQ
N_TOK = N_PROMPT + N_SAMPLE
N_MODVEC = 1 + DEC_BATCH
TM = 512
TQ = 256
CHUNK = 128
HALO = SUBLANES
HEAD_SLAB = LANES
IN_COLS = Q_RANK + KV_RANK + D_SSD + CONV_CH + 2 * LANES
KPE_LANE0 = NOPE_DIM
VMEM_LIMIT = 56 * 1024 * 1024

POOL_SUB = min(SEQ, DEC_SEQ)

assert TM % POOL_SUB == 0 and SEQ % POOL_SUB == 0 and DEC_SEQ % TM == 0 and N_PROMPT % DEC_SEQ == 0


def _rms(x, g):
    return x * lax.rsqrt(jnp.mean(x * x, axis=-1, keepdims=True) + EPS) * g


def _silu(x):
    return x * jax.nn.sigmoid(x)


def _softplus(x):
    return jnp.maximum(x, 0.0) + jnp.log1p(jnp.exp(-jnp.abs(x)))


def _dot(a, b):
    return jnp.dot(a, b, preferred_element_type=jnp.float32)


def _dot_nt(a, b):
    return lax.dot_general(a, b, (((1,), (1,)), ((), ())), preferred_element_type=jnp.float32)


def _mod_row(i):
    return jnp.where(i < N_PROMPT // TM, 0, 1 + (i - N_PROMPT // TM) // (DEC_SEQ // TM))


def _const_spec(shape):
    nd = len(shape)
    return pl.BlockSpec(shape, lambda *_: (0,) * nd, pipeline_mode=pl.Buffered(1))


def _layer_spec(shape, l):
    nd = len(shape)
    return pl.BlockSpec((1,) + tuple(shape), lambda *_: (l,) + (0,) * nd, pipeline_mode=pl.Buffered(1))


N_PT = N_PROMPT // TM


def _group_specs(width):
    return [pl.BlockSpec((TM, width), lambda i: (jnp.minimum(i, N_PT - 1), 0)),
            pl.BlockSpec((TM, width), lambda i: (jnp.maximum(i - N_PT, 0), 0))]


def _pick_group(i, p_ref, s_ref):
    return jnp.where(i < N_PT, p_ref[...], s_ref[...])


def _mod_kernel(ct_ref, w_ref, b_ref, o_ref):
    s = _silu(ct_ref[...])
    w = w_ref[0]
    b = b_ref[0]
    rows = [jnp.sum(s[:, v:v + 1] * w, axis=0, keepdims=True) + b for v in range(N_MODVEC)]
    rows.append(jnp.zeros((SUBLANES - N_MODVEC, w.shape[1]), jnp.float32))
    o_ref[0] = jnp.concatenate(rows, axis=0)


def _modulation(cvecs_t, w_mod, b_mod):
    tn = 1024
    nt = 6 * D_MODEL // tn
    return pl.pallas_call(
        _mod_kernel,
        grid=(DEPTH, nt),
        in_specs=[
            pl.BlockSpec((D_MODEL, SUBLANES), lambda l, j: (0, 0)),
            pl.BlockSpec((1, D_MODEL, tn), lambda l, j: (l, 0, j)),
            pl.BlockSpec((1, 1, tn), lambda l, j: (l, 0, j)),
        ],
        out_specs=pl.BlockSpec((1, SUBLANES, tn), lambda l, j: (l, 0, j)),
        out_shape=jax.ShapeDtypeStruct((DEPTH, SUBLANES, 6 * D_MODEL), jnp.float32),
        compiler_params=pltpu.CompilerParams(dimension_semantics=("arbitrary", "arbitrary")),
        name="modulation",
    )(cvecs_t, w_mod, b_mod.reshape(DEPTH, 1, 6 * D_MODEL))


def _inproj_kernel(xp_ref, xs_ref, mod_ref, cos_ref, s1_ref, s2_ref, w_in_ref, w_uq_ref, qn_ref, kvn_ref, npm_ref,
                   q_ref, ckv_ref, kpe_ref, z_ref, xbc_ref, dt_ref):
    i = pl.program_id(0)
    is_latent = i >= N_PT
    sh = mod_ref[0, :, 0:D_MODEL]
    sc = mod_ref[0, :, D_MODEL:2 * D_MODEL]
    h = (_rms(_pick_group(i, xp_ref, xs_ref), npm_ref[...]) * (1.0 + sc) + sh).astype(jnp.bfloat16)
    p = _dot(h, w_in_ref[...])
    o = 0
    cq = p[:, o:o + Q_RANK]; o += Q_RANK
    ckv = p[:, o:o + KV_RANK]; o += KV_RANK
    z_ref[...] = p[:, o:o + D_SSD]; o += D_SSD
    xbc_ref[...] = p[:, o:o + CONV_CH]; o += CONV_CH
    kpe = p[:, o:o + LANES]; o += LANES
    dt_ref[...] = p[:, o:o + LANES]

    cos = cos_ref[...]
    s1 = s1_ref[...]
    s2 = s2_ref[...]

    def rope(slab):
        r = slab * cos + pltpu.roll(slab, LANES - ROPE_DIM // 2, 1) * s1 + pltpu.roll(slab, ROPE_DIM // 2, 1) * s2
        return jnp.where(is_latent, r, slab)

    ckv_ref[...] = _rms(ckv, kvn_ref[...])
    kpe_ref[...] = rope(kpe)
    scale = (NOPE_DIM + ROPE_DIM) ** -0.5
    q = _dot(_rms(cq, qn_ref[...]).astype(jnp.bfloat16), w_uq_ref[...]) * scale
    for hd in range(MLA_HEADS):
        sl = slice(hd * HEAD_SLAB, (hd + 1) * HEAD_SLAB)
        q_ref[:, sl] = rope(q[:, sl]).astype(jnp.bfloat16)


def _inproj(xp, xs, mod_l, tabs, w_in, w_uq, q_norm, kv_norm, npm):
    nt = N_TOK // TM
    tab_spec = pl.BlockSpec((TM, LANES), lambda i: (jnp.maximum(i - N_PT, 0) % (DEC_SEQ // TM), 0))
    row = lambda w: pl.BlockSpec((TM, w), lambda i: (i, 0))
    return pl.pallas_call(
        _inproj_kernel,
        grid=(nt,),
        in_specs=_group_specs(D_MODEL) + [
            pl.BlockSpec((1, 1, 6 * D_MODEL), lambda i: (_mod_row(i), 0, 0)),
            tab_spec, tab_spec, tab_spec,
            _const_spec((D_MODEL, IN_COLS)),
            _const_spec((Q_RANK, MLA_HEADS * HEAD_SLAB)),
            _const_spec((1, Q_RANK)),
            _const_spec((1, KV_RANK)),
            _const_spec((1, D_MODEL)),
        ],
        out_specs=[row(MLA_HEADS * HEAD_SLAB), row(KV_RANK), row(LANES), row(D_SSD), row(CONV_CH), row(LANES)],
        out_shape=[
            jax.ShapeDtypeStruct((N_TOK, MLA_HEADS * HEAD_SLAB), jnp.bfloat16),
            jax.ShapeDtypeStruct((N_TOK, KV_RANK), jnp.float32),
            jax.ShapeDtypeStruct((N_TOK, LANES), jnp.float32),
            jax.ShapeDtypeStruct((N_TOK, D_SSD), jnp.float32),
            jax.ShapeDtypeStruct((N_TOK, CONV_CH), jnp.float32),
            jax.ShapeDtypeStruct((N_TOK, LANES), jnp.float32),
        ],
        compiler_params=pltpu.CompilerParams(dimension_semantics=("arbitrary",), vmem_limit_bytes=VMEM_LIMIT),
        name="inproj",
    )(xp, xs, mod_l, *tabs, w_in, w_uq, q_norm, kv_norm, npm)


def _attn_kernel(*refs, lk_cache, lk_new):
    if lk_cache:
        q_ref, ckv_ref, kpe_ref, ckvc_ref, kpec_ref, w_ref, o_ref, k_scr, v_scr = refs
    else:
        q_ref, ckv_ref, kpe_ref, w_ref, o_ref, k_scr, v_scr = refs
    n_pair = MLA_HEADS // 2
    kcols = MLA_HEADS * HEAD_SLAB

    @pl.when(pl.program_id(1) == 0)
    def _expand_kv():
        def expand(ckv, kpe, r0):
            kv = _dot(ckv.astype(jnp.bfloat16), w_ref[...])
            rows = slice(r0, r0 + ckv.shape[0])
            for hd in range(MLA_HEADS):
                k_scr[hd, rows, :] = (kv[:, hd * HEAD_SLAB:(hd + 1) * HEAD_SLAB] + kpe).astype(jnp.bfloat16)
            for j in range(n_pair):
                v_scr[j, rows, :] = kv[:, kcols + j * LANES:kcols + (j + 1) * LANES].astype(jnp.bfloat16)

        step = 256
        for r0 in range(0, lk_cache, step):
            expand(ckvc_ref[0, r0:r0 + step, :], kpec_ref[0, r0:r0 + step, :], r0)
        for r0 in range(0, lk_new, step):
            expand(ckv_ref[r0:r0 + step, :], kpe_ref[r0:r0 + step, :], lk_cache + r0)

    low_half = lax.broadcasted_iota(jnp.int32, (TQ, LANES), 1) < V_DIM
    for j in range(n_pair):
        outs = []
        for hd in (2 * j, 2 * j + 1):
            s = _dot_nt(q_ref[:, hd * HEAD_SLAB:(hd + 1) * HEAD_SLAB], k_scr[hd])
            p = jnp.exp(s - jnp.max(s, axis=-1, keepdims=True))
            den = jnp.sum(p, axis=-1, keepdims=True)
            outs.append(_dot(p.astype(jnp.bfloat16), v_scr[j]) / den)
        o_ref[:, j * LANES:(j + 1) * LANES] = jnp.where(low_half, outs[0], outs[1]).astype(jnp.bfloat16)


def _attention(q, ckv_n, kpe, w_ukv, row_off, n_batch, seq, cache=None):
    nq = seq // TQ
    lk_cache = 0 if cache is None else cache[0].shape[1]
    lk = lk_cache + seq
    qblk = lambda b, qi: (row_off // TQ + b * nq + qi, 0)
    sblk = lambda b, qi: (row_off // seq + b, 0)
    in_specs = [
        pl.BlockSpec((TQ, MLA_HEADS * HEAD_SLAB), qblk),
        pl.BlockSpec((seq, KV_RANK), sblk),
        pl.BlockSpec((seq, LANES), sblk),
    ]
    args = [q, ckv_n, kpe]
    if cache is not None:
        in_specs += [pl.BlockSpec((1, lk_cache, KV_RANK), lambda b, qi: (b, 0, 0)),
                     pl.BlockSpec((1, lk_cache, LANES), lambda b, qi: (b, 0, 0))]
        args += list(cache)
    in_specs.append(_const_spec(w_ukv.shape))
    args.append(w_ukv)
    return pl.pallas_call(
        functools.partial(_attn_kernel, lk_cache=lk_cache, lk_new=seq),
        grid=(n_batch, nq),
        in_specs=in_specs,
        out_specs=pl.BlockSpec((TQ, MLA_HEADS * V_DIM), lambda b, qi: (b * nq + qi, 0)),
        out_shape=jax.ShapeDtypeStruct((n_batch * seq, MLA_HEADS * V_DIM), jnp.bfloat16),
        scratch_shapes=[pltpu.VMEM((MLA_HEADS, lk, HEAD_SLAB), jnp.bfloat16),
                        pltpu.VMEM((MLA_HEADS // 2, lk, LANES), jnp.bfloat16)],
        compiler_params=pltpu.CompilerParams(dimension_semantics=("arbitrary", "arbitrary"),
                                             vmem_limit_bytes=VMEM_LIMIT),
        name=f"attention_{seq}",
    )(*args)


def _split3(a):
    a_hi = a.astype(jnp.bfloat16)
    r1 = a - a_hi.astype(jnp.float32)
    a_mid = r1.astype(jnp.bfloat16)
    a_lo = (r1 - a_mid.astype(jnp.float32)).astype(jnp.bfloat16)
    return a_hi, a_mid, a_lo


def _ssd_kernel(xbc_ref, dt_ref, h0f_ref, h0b_ref, cw_ref, cb_ref, dtb_ref, alog_ref, dtbc_ref, alogc_ref, dsk_ref,
                y_ref, hf_ref, hb_ref, xs_scr, c_scr, bt_scr, dtt_scr, stf_scr, stb_scr, *, seq):
    nc = seq // CHUNK
    gs = SSD_GROUPS * SSD_STATE
    nh2 = 2 * SSD_HEADS

    def conv_chunk(c, carry):
        r0 = pl.multiple_of(c * CHUNK, CHUNK)
        cur = xbc_ref[pl.ds(r0, CHUNK), :]
        prev = xbc_ref[pl.ds(pl.multiple_of(jnp.maximum(r0 - HALO, 0), HALO), HALO), :]
        nxt = xbc_ref[pl.ds(pl.multiple_of(jnp.minimum(r0 + CHUNK, seq - HALO), HALO), HALO), :]
        prev = jnp.where(c > 0, prev, 0.0)
        nxt = jnp.where(c < nc - 1, nxt, 0.0)
        win = jnp.concatenate([prev, cur, nxt], axis=0)
        acc = jnp.broadcast_to(cb_ref[...], (CHUNK, CONV_CH))
        for k in range(CONV_W):
            lo = HALO - CONV_W // 2 + k
            acc = acc + cw_ref[k:k + 1, :] * win[lo:lo + CHUNK, :]
        u = _silu(acc)
        xs = u[:, :D_SSD]
        y_ref[pl.ds(r0, CHUNK), :] = dsk_ref[...] * xs
        xs_scr[pl.ds(r0, CHUNK), :] = xs.astype(jnp.bfloat16)
        c_scr[pl.ds(r0, CHUNK), :] = u[:, D_SSD + gs:].astype(jnp.bfloat16)
        for g in range(SSD_GROUPS):
            b0 = pl.multiple_of(c * gs + g * SSD_STATE, SSD_STATE)
            bt_scr[pl.ds(b0, SSD_STATE), :] = u[:, D_SSD + g * SSD_STATE:D_SSD + (g + 1) * SSD_STATE].T
        dtt_scr[pl.ds(pl.multiple_of(c * nh2, nh2), nh2), :] = dt_ref[pl.ds(r0, CHUNK), :].T[:nh2, :]
        return carry

    lax.fori_loop(0, nc, conv_chunk, 0)

    stf_scr[...] = h0f_ref[0].T
    stb_scr[...] = h0b_ref[0].T

    row = lax.broadcasted_iota(jnp.int32, (CHUNK, CHUNK), 0)
    col = lax.broadcasted_iota(jnp.int32, (CHUNK, CHUNK), 1)
    low_half = col < SSD_HEAD_DIM
    a_neg = -jnp.exp(alog_ref[...])
    a_neg_c = -jnp.exp(alogc_ref[:nh2, :])
    dtb_c = dtbc_ref[:nh2, :]

    def scan_chunk(ci, st_scr, reverse):
        lane0 = SSD_HEADS if reverse else 0
        causal = (row <= col) if reverse else (row >= col)
        tri = causal.astype(jnp.bfloat16)
        tri_t = ((row >= col) if reverse else (row <= col)).astype(jnp.bfloat16)
        last = 0 if reverse else CHUNK - 1
        r0 = pl.multiple_of(ci * CHUNK, CHUNK)
        rows = pl.ds(r0, CHUNK)
        xs_b = xs_scr[rows, :]
        c_b = c_scr[rows, :]
        dts = _softplus(dt_ref[rows, :] + dtb_ref[...])
        cum = sum(_dot(tri, piece) for piece in _split3(dts * a_neg))
        dts_t = _softplus(dtt_scr[pl.ds(pl.multiple_of(ci * nh2, nh2), nh2), :] + dtb_c)
        cum_t = sum(_dot(piece, tri_t) for piece in _split3(dts_t * a_neg_c))
        w_t = dts_t * jnp.exp(cum_t[:, last:last + 1] - cum_t)
        st = st_scr[...]
        for g in range(SSD_GROUPS):
            cg = c_b[:, g * SSD_STATE:(g + 1) * SSD_STATE]
            bt = bt_scr[pl.ds(pl.multiple_of(ci * gs + g * SSD_STATE, SSD_STATE), SSD_STATE), :]
            cbm = _dot(cg, bt.astype(jnp.bfloat16))
            gcols = slice(g * SSD_HPG * SSD_HEAD_DIM, (g + 1) * SSD_HPG * SSD_HEAD_DIM)
            z = _dot(cg, st[:, gcols].astype(jnp.bfloat16))
            for jj in range(SSD_HPG // 2):
                pair = g * (SSD_HPG // 2) + jj
                pcols = slice(pair * LANES, (pair + 1) * LANES)
                lhs_y, lhs_s, entry = [], [], []
                for hd in (2 * pair, 2 * pair + 1):
                    ln = lane0 + hd
                    cum_i = jnp.broadcast_to(cum[:, ln:ln + 1], (CHUNK, CHUNK))
                    dec = jnp.exp(jnp.where(causal, cum_i - cum_t[ln:ln + 1, :], -jnp.inf))
                    lhs_y.append((cbm * dec * dts_t[ln:ln + 1, :]).astype(jnp.bfloat16))
                    lhs_s.append((bt * w_t[ln:ln + 1, :]).astype(jnp.bfloat16))
                    entry.append(jnp.exp(cum_i))
                out = _dot(jnp.concatenate(lhs_y + lhs_s, axis=0), xs_b[:, pcols])
                ea = jnp.where(low_half, entry[0], entry[1])
                y_ref[rows, pcols] += (jnp.where(low_half, out[:CHUNK], out[CHUNK:2 * CHUNK])
                                       + z[:, jj * LANES:(jj + 1) * LANES] * ea)
                st_scr[:, pcols] = (ea[last:last + 1, :] * st[:, pcols]
                                    + jnp.where(low_half, out[2 * CHUNK:3 * CHUNK], out[3 * CHUNK:]))

    def both(c, carry):
        scan_chunk(c, stf_scr, False)
        scan_chunk(nc - 1 - c, stb_scr, True)
        return carry

    lax.fori_loop(0, nc, both, 0)
    hf_ref[0] = stf_scr[...].T
    hb_ref[0] = stb_scr[...].T


def _ssd(xbc, dt, h0f, h0b, conv_w, conv_b, dt_bias, a_log, d_skip, row_off, n_batch, seq):
    hp = SSD_HEADS * SSD_HEAD_DIM
    gs = SSD_GROUPS * SSD_STATE
    nc = seq // CHUNK
    sblk = lambda w: pl.BlockSpec((seq, w), lambda b: (row_off // seq + b, 0))
    st_spec = pl.BlockSpec((1, hp, SSD_STATE), lambda b: (b, 0, 0))
    st_shape = jax.ShapeDtypeStruct((n_batch, hp, SSD_STATE), jnp.float32)
    return pl.pallas_call(
        functools.partial(_ssd_kernel, seq=seq),
        grid=(n_batch,),
        in_specs=[sblk(CONV_CH), sblk(LANES), st_spec, st_spec,
                  _const_spec((SUBLANES, CONV_CH)), _const_spec((1, CONV_CH)),
                  _const_spec((1, LANES)), _const_spec((1, LANES)),
                  _const_spec((LANES, 1)), _const_spec((LANES, 1)), _const_spec((1, D_SSD))],
        out_specs=[pl.BlockSpec((seq, D_SSD), lambda b: (b, 0)), st_spec, st_spec],
        out_shape=[jax.ShapeDtypeStruct((n_batch * seq, D_SSD), jnp.float32), st_shape, st_shape],
        scratch_shapes=[pltpu.VMEM((seq, D_SSD), jnp.bfloat16),
                        pltpu.VMEM((seq, gs), jnp.bfloat16),
                        pltpu.VMEM((nc * gs, CHUNK), jnp.float32),
                        pltpu.VMEM((nc * 2 * SSD_HEADS, CHUNK), jnp.float32),
                        pltpu.VMEM((SSD_STATE, hp), jnp.float32),
                        pltpu.VMEM((SSD_STATE, hp), jnp.float32)],
        compiler_params=pltpu.CompilerParams(dimension_semantics=("arbitrary",), vmem_limit_bytes=VMEM_LIMIT),
        name=f"ssd_{seq}",
    )(xbc, dt, h0f, h0b, conv_w, conv_b, dt_bias, a_log, dt_bias.T, a_log.T, d_skip)


def _post_ffn(x, mix, mod_ref, npost_ref, npre_ref, nffn_ref, wg_ref, wu_ref, wd_ref):
    d = D_MODEL
    gate_mix = mod_ref[0, :, 2 * d:3 * d]
    shf = mod_ref[0, :, 3 * d:4 * d]
    scf = mod_ref[0, :, 4 * d:5 * d]
    gate_ffn = mod_ref[0, :, 5 * d:6 * d]
    x1 = x + gate_mix * _rms(mix, npost_ref[...])
    h = (_rms(x1, npre_ref[...]) * (1.0 + scf) + shf).astype(jnp.bfloat16)
    hid = (_silu(_dot(h, wg_ref[0])) * _dot(h, wu_ref[0])).astype(jnp.bfloat16)
    f = _dot(hid, wd_ref[0])
    return x1 + gate_ffn * _rms(f, nffn_ref[...])


def _ffn_specs(l):
    return [_const_spec((1, D_MODEL)), _const_spec((1, D_MODEL)), _const_spec((1, D_MODEL)),
            _layer_spec((D_MODEL, D_FF), l), _layer_spec((D_MODEL, D_FF), l), _layer_spec((D_FF, D_MODEL), l)]


def _outproj_ffn_kernel(xp_ref, xs_ref, attp_ref, atts_ref, yp_ref, ys_ref, z_ref, mod_ref, sn_ref, wo_ref,
                        npost_ref, npre_ref, nffn_ref, wg_ref, wu_ref, wd_ref, o_ref):
    i = pl.program_id(0)
    yg = _pick_group(i, yp_ref, ys_ref) * _silu(z_ref[...])
    gw = D_SSD // SSD_GROUPS
    parts = [_pick_group(i, attp_ref, atts_ref)]
    for g in range(SSD_GROUPS):
        parts.append(_rms(yg[:, g * gw:(g + 1) * gw], sn_ref[:, g * gw:(g + 1) * gw]).astype(jnp.bfloat16))
    mix = _dot(jnp.concatenate(parts, axis=1), wo_ref[...])
    o_ref[...] = _post_ffn(_pick_group(i, xp_ref, xs_ref), mix, mod_ref, npost_ref, npre_ref, nffn_ref,
                           wg_ref, wu_ref, wd_ref)


def _outproj_ffn(l, xp, xs, att_p, att_s, y_p, y_s, z, mod_l, ssd_norm, w_out, npost, npre, nffn, wg, wu, wd):
    row = lambda w: pl.BlockSpec((TM, w), lambda i: (i, 0))
    return pl.pallas_call(
        _outproj_ffn_kernel,
        grid=(N_TOK // TM,),
        in_specs=_group_specs(D_MODEL) + _group_specs(MLA_HEADS * V_DIM) + _group_specs(D_SSD) + [
            row(D_SSD),
            pl.BlockSpec((1, 1, 6 * D_MODEL), lambda i: (_mod_row(i), 0, 0)),
            _const_spec((1, D_SSD)), _const_spec((MLA_HEADS * V_DIM + D_SSD, D_MODEL))] + _ffn_specs(l),
        out_specs=row(D_MODEL),
        out_shape=jax.ShapeDtypeStruct((N_TOK, D_MODEL), jnp.float32),
        compiler_params=pltpu.CompilerParams(dimension_semantics=("arbitrary",), vmem_limit_bytes=VMEM_LIMIT),
        name="outproj_ffn",
    )(xp, xs, att_p, att_s, y_p, y_s, z, mod_l, ssd_norm, w_out, npost, npre, nffn, wg, wu, wd)


def _pool_ffn_kernel(x_ref, xp_ref, xn_ref, mod_ref, nmix_ref, pw_ref, ps_ref,
                     npost_ref, npre_ref, nffn_ref, wg_ref, wu_ref, wd_ref, op_ref, os_ref):
    i = pl.program_id(0)
    seq = jnp.where(i < N_PROMPT // TM, SEQ, DEC_SEQ)
    pos0 = (i * TM) % seq
    sh = mod_ref[0, :, 0:D_MODEL]
    sc = mod_ref[0, :, D_MODEL:2 * D_MODEL]
    hmod = lambda v: _rms(v, nmix_ref[...]) * (1.0 + sc) + sh
    x = x_ref[...]
    h = hmod(x)
    h_prev = hmod(xp_ref[...])
    h_next = hmod(xn_ref[...])
    n_rows = POOL_SUB + 2 * HALO

    def shifted(v, s):
        return pltpu.roll(v, n_rows - s, 0)

    pooled = [[] for _ in POOL_WINDOWS]
    for s in range(TM // POOL_SUB):
        lo, hi = s * POOL_SUB, (s + 1) * POOL_SUB
        pos_s = (pos0 + lo) % seq
        before = h_prev if s == 0 else h[lo - HALO:lo, :]
        after = h_next if hi == TM else h[hi:hi + HALO, :]
        before = jnp.where(pos_s > 0, before, 0.0)
        after = jnp.where(pos_s + POOL_SUB < seq, after, 0.0)
        padded = jnp.concatenate([before, h[lo:hi, :], after], axis=0)
        pos = pos_s + lax.broadcasted_iota(jnp.int32, (POOL_SUB, 1), 0)
        for gi, w in enumerate(POOL_WINDOWS):
            cols = slice(gi * POOL_GC, (gi + 1) * POOL_GC)
            t = padded[:, cols]
            span = 1
            while span < w:
                t = t + shifted(t, span)
                span *= 2
            lead = HALO - w // 2
            win_sum = (shifted(t, lead) if lead else t)[:POOL_SUB, :]
            cnt = (jnp.minimum(pos + w // 2, seq) - jnp.maximum(pos - w // 2, 0)).astype(jnp.float32)
            pooled[gi].append((win_sum / cnt - h[lo:hi, cols]).astype(jnp.bfloat16))
    outs = [_dot(jnp.concatenate(pooled[gi], axis=0), pw_ref[gi]) for gi in range(len(POOL_WINDOWS))]
    mix = jnp.concatenate(outs, axis=1) * ps_ref[...]
    res = _post_ffn(x, mix, mod_ref, npost_ref, npre_ref, nffn_ref, wg_ref, wu_ref, wd_ref)

    @pl.when(i == 0)
    def _():
        os_ref[...] = jnp.zeros_like(os_ref)

    @pl.when(i < N_PT)
    def _():
        op_ref[...] = res

    @pl.when(i >= N_PT)
    def _():
        os_ref[...] = res


def _pool_ffn(l, xa, mod_l, nmix, pool_w, pool_scale, npost, npre, nffn, wg, wu, wd):
    hb = TM // HALO
    nh = N_TOK // HALO
    return pl.pallas_call(
        _pool_ffn_kernel,
        grid=(N_TOK // TM,),
        in_specs=[pl.BlockSpec((TM, D_MODEL), lambda i: (i, 0)),
                  pl.BlockSpec((HALO, D_MODEL), lambda i: (jnp.maximum(i * hb - 1, 0), 0)),
                  pl.BlockSpec((HALO, D_MODEL), lambda i: (jnp.minimum((i + 1) * hb, nh - 1), 0)),
                  pl.BlockSpec((1, 1, 6 * D_MODEL), lambda i: (_mod_row(i), 0, 0)),
                  _const_spec((1, D_MODEL)),
                  _const_spec((len(POOL_WINDOWS), POOL_GC, POOL_GC)),
                  _const_spec((1, D_MODEL))] + _ffn_specs(l),
        out_specs=_group_specs(D_MODEL),
        out_shape=[jax.ShapeDtypeStruct((N_PROMPT, D_MODEL), jnp.float32),
                   jax.ShapeDtypeStruct((N_SAMPLE, D_MODEL), jnp.float32)],
        compiler_params=pltpu.CompilerParams(dimension_semantics=("arbitrary",), vmem_limit_bytes=VMEM_LIMIT),
        name="pool_ffn",
    )(xa, xa, xa, mod_l, nmix, pool_w, pool_scale, npost, npre, nffn, wg, wu, wd)


def _rope_tables():
    rows = DEC_SEQ // GRID_W
    r = jnp.repeat(jnp.arange(rows, dtype=jnp.float32), GRID_W)
    c = jnp.tile(jnp.arange(GRID_W, dtype=jnp.float32), rows)
    half = ROPE_DIM // 2
    inv_freq = jnp.power(ROPE_THETA, -jnp.arange(0, half, 2, dtype=jnp.float32) / half)
    ang = jnp.concatenate([r[:, None] * inv_freq, c[:, None] * inv_freq], axis=-1)
    cos, sin = jnp.cos(ang), jnp.sin(ang)
    zl = jnp.zeros((DEC_SEQ, KPE_LANE0), jnp.float32)
    zr = jnp.zeros((DEC_SEQ, LANES - KPE_LANE0 - ROPE_DIM), jnp.float32)
    zh = jnp.zeros((DEC_SEQ, half), jnp.float32)
    cos_t = jnp.concatenate([zl + 1.0, cos, cos, zr + 1.0], axis=1)
    s1_t = jnp.concatenate([zl, -sin, zh, zr], axis=1)
    s2_t = jnp.concatenate([zl, zh, sin, zr], axis=1)
    return cos_t, s1_t, s2_t


def _kpe_slab(k):
    pad = [(0, 0)] * (k.ndim - 1) + [(KPE_LANE0, LANES - KPE_LANE0 - ROPE_DIM)]
    return jnp.pad(k, pad)


def _layout_in_proj(w):
    o = np.cumsum((0, Q_RANK, KV_RANK, ROPE_DIM, D_SSD, CONV_CH, SSD_HEADS, SSD_HEADS))
    cq, ckv, kpe, z, xbc, dtf, dtb = (w[:, o[k]:o[k + 1]] for k in range(7))
    dt = jnp.pad(jnp.concatenate([dtf, dtb], axis=1), ((0, 0), (0, LANES - 2 * SSD_HEADS)))
    return jnp.concatenate([cq, ckv, z, xbc, _kpe_slab(kpe), dt], axis=1).astype(jnp.bfloat16)


def _layout_uq(w):
    w = w.reshape(Q_RANK, MLA_HEADS, NOPE_DIM + ROPE_DIM)
    w = jnp.pad(w, ((0, 0), (0, 0), (0, HEAD_SLAB - NOPE_DIM - ROPE_DIM)))
    return w.reshape(Q_RANK, MLA_HEADS * HEAD_SLAB).astype(jnp.bfloat16)


def _layout_ukv(w):
    w = w.reshape(KV_RANK, MLA_HEADS, NOPE_DIM + V_DIM)
    kn = jnp.pad(w[:, :, :NOPE_DIM], ((0, 0), (0, 0), (0, HEAD_SLAB - NOPE_DIM)))
    v = w[:, :, NOPE_DIM:]
    return jnp.concatenate([kn.reshape(KV_RANK, -1), v.reshape(KV_RANK, -1)], axis=1).astype(jnp.bfloat16)


def _lane_row(fwd, bwd):
    return jnp.pad(jnp.concatenate([fwd, bwd]), (0, LANES - 2 * SSD_HEADS)).reshape(1, LANES)


def kernel(x_prompt, x_sample, c, cache_mla_ckv, cache_mla_krope, state_ssd_fwd, state_ssd_bwd, c_ctx, w_mod, b_mod, norm_pre_mix, norm_post_mix, norm_pre_ffn, norm_post_ffn, w_in_ab, q_norm, w_uq, kv_norm, w_ukv, ssd_conv_w, ssd_conv_b, ssd_dt_bias_fwd, ssd_dt_bias_bwd, ssd_a_log_fwd, ssd_a_log_bwd, ssd_d, ssd_norm, w_out_ab, pool_w, pool_scale, ffn_w_gate, ffn_w_up, ffn_w_down):
    f32, bf16 = jnp.float32, jnp.bfloat16
    assert DEPTH == 2
    xp = x_prompt.reshape(N_PROMPT, D_MODEL)
    xs = x_sample.reshape(N_SAMPLE, D_MODEL)
    cvecs = jnp.concatenate([c_ctx[None, :], c, jnp.zeros((SUBLANES - N_MODVEC, D_MODEL), f32)], axis=0)
    mod = _modulation(cvecs.T, w_mod, b_mod).reshape(DEPTH, SUBLANES, 1, 6 * D_MODEL)
    tabs = _rope_tables()
    hp = SSD_HEADS * SSD_HEAD_DIM
    row = lambda v: v.reshape(1, -1)
    wg, wu, wd = ffn_w_gate.astype(bf16), ffn_w_up.astype(bf16), ffn_w_down.astype(bf16)
    new_ckv, new_kpe, new_hf, new_hb = [], [], [], []

    for l in range(DEPTH):
        ffn = (row(norm_post_mix[l]), row(norm_pre_ffn[l]), row(norm_post_ffn[l]), wg, wu, wd)
        if l % 2 == 0:
            i = l // 2
            q, ckv_n, kpe, z, xbc, dt = _inproj(
                xp, xs, mod[l], tabs, _layout_in_proj(w_in_ab[i]), _layout_uq(w_uq[i]),
                row(q_norm[i]), row(kv_norm[i]), row(norm_pre_mix[l]))
            w_kv = _layout_ukv(w_ukv[i])
            att_p = _attention(q, ckv_n, kpe, w_kv, 0, BATCH, SEQ)
            att_s = _attention(q, ckv_n, kpe, w_kv, N_PROMPT, DEC_BATCH, DEC_SEQ,
                               cache=(cache_mla_ckv[:, i], _kpe_slab(cache_mla_krope[:, i])))
            ssd_args = (jnp.pad(ssd_conv_w[i], ((0, SUBLANES - CONV_W), (0, 0))), row(ssd_conv_b[i]),
                        _lane_row(ssd_dt_bias_fwd[i], ssd_dt_bias_bwd[i]),
                        _lane_row(ssd_a_log_fwd[i], ssd_a_log_bwd[i]),
                        row(jnp.repeat(ssd_d[i], SSD_HEAD_DIM)))
            h_zero = jnp.zeros((BATCH, hp, SSD_STATE), f32)
            y_p, hf, hb = _ssd(xbc, dt, h_zero, h_zero, *ssd_args, 0, BATCH, SEQ)
            y_s, _, _ = _ssd(xbc, dt, state_ssd_fwd[:, i].reshape(DEC_BATCH, hp, SSD_STATE),
                             state_ssd_bwd[:, i].reshape(DEC_BATCH, hp, SSD_STATE),
                             *ssd_args, N_PROMPT, DEC_BATCH, DEC_SEQ)
            xa = _outproj_ffn(l, xp, xs, att_p, att_s, y_p, y_s, z,
                              mod[l], row(ssd_norm[i]), w_out_ab[i].astype(bf16), *ffn)
            new_ckv.append(ckv_n[:N_PROMPT].reshape(BATCH, SEQ, KV_RANK))
            new_kpe.append(kpe[:N_PROMPT, KPE_LANE0:KPE_LANE0 + ROPE_DIM].reshape(BATCH, SEQ, ROPE_DIM))
            new_hf.append(hf.reshape(BATCH, SSD_HEADS, SSD_HEAD_DIM, SSD_STATE))
            new_hb.append(hb.reshape(BATCH, SSD_HEADS, SSD_HEAD_DIM, SSD_STATE))
        else:
            j = l // 2
            yp, ys = _pool_ffn(l, xa, mod[l], row(norm_pre_mix[l]), pool_w[j].astype(bf16), row(pool_scale[j]), *ffn)

    return (yp.reshape(BATCH, SEQ, D_MODEL), ys.reshape(DEC_BATCH, DEC_SEQ, D_MODEL),
            jnp.stack(new_ckv, axis=1), jnp.stack(new_kpe, axis=1),
            jnp.stack(new_hf, axis=1), jnp.stack(new_hb, axis=1))
```

```python
import functools

import numpy as np
import jax
import jax.numpy as jnp
from jax import lax
from jax.experimental import pallas as pl
from jax.experimental.pallas import tpu as pltpu

D_MODEL = 1024
BATCH = 16
SEQ = 256
DEPTH = 2
DEC_BATCH = 2
DEC_SEQ = 2048
PAST_LEN = 256
GRID_W = 64
EPS = 1e-6
MLA_HEADS = 8
Q_RANK = 256
KV_RANK = 256
NOPE_DIM = 64
ROPE_DIM = 32
V_DIM = 64
ROPE_THETA = 10000.0
SSD_HEADS = 8
SSD_GROUPS = 2
SSD_HPG = SSD_HEADS // SSD_GROUPS
SSD_HEAD_DIM = 64
SSD_STATE = 128
D_SSD = SSD_HEADS * SSD_HEAD_DIM
CONV_W = 5
CONV_CH = D_SSD + 2 * SSD_GROUPS * SSD_STATE
POOL_WINDOWS = (2, 4, 8, 16)
POOL_GC = D_MODEL // len(POOL_WINDOWS)
D_FF = ((8 * D_MODEL + 3 * 256 - 1) // (3 * 256)) * 256

SUBLANES = 8
LANES = 128

N_PROMPT = BATCH * SEQ
N_SAMPLE = DEC_BATCH * DEC_SEQ
N_TOK = N_PROMPT + N_SAMPLE
N_MODVEC = 1 + DEC_BATCH
TM = 512
TQ = 256
CHUNK = 128
HALO = SUBLANES
HEAD_SLAB = LANES
IN_COLS = Q_RANK + KV_RANK + D_SSD + CONV_CH + 2 * LANES
KPE_LANE0 = NOPE_DIM
VMEM_LIMIT = 56 * 1024 * 1024

POOL_SUB = min(SEQ, DEC_SEQ)

assert TM % POOL_SUB == 0 and SEQ % POOL_SUB == 0 and DEC_SEQ % TM == 0 and N_PROMPT % DEC_SEQ == 0


def _rms(x, g):
    return x * lax.rsqrt(jnp.mean(x * x, axis=-1, keepdims=True) + EPS) * g


def _silu(x):
    return x * jax.nn.sigmoid(x)


def _softplus(x):
    return jnp.maximum(x, 0.0) + jnp.log1p(jnp.exp(-jnp.abs(x)))


def _dot(a, b):
    return jnp.dot(a, b, preferred_element_type=jnp.float32)


def _dot_nt(a, b):
    return lax.dot_general(a, b, (((1,), (1,)), ((), ())), preferred_element_type=jnp.float32)


def _mod_row(i):
    return jnp.where(i < N_PROMPT // TM, 0, 1 + (i - N_PROMPT // TM) // (DEC_SEQ // TM))


def _const_spec(shape):
    nd = len(shape)
    return pl.BlockSpec(shape, lambda *_: (0,) * nd, pipeline_mode=pl.Buffered(1))


def _layer_spec(shape, l):
    nd = len(shape)
    return pl.BlockSpec((1,) + tuple(shape), lambda *_: (l,) + (0,) * nd, pipeline_mode=pl.Buffered(1))


N_PT = N_PROMPT // TM


def _group_specs(width):
    return [pl.BlockSpec((TM, width), lambda i: (jnp.minimum(i, N_PT - 1), 0)),
            pl.BlockSpec((TM, width), lambda i: (jnp.maximum(i - N_PT, 0), 0))]


def _pick_group(i, p_ref, s_ref, rows=slice(None)):
    return jnp.where(i < N_PT, p_ref[rows, :], s_ref[rows, :])


MOD_TN = 2048
SUB_ROWS = 256


def _mod_kernel(ct_ref, w_ref, b_ref, o_ref):
    s = _silu(ct_ref[...])
    w = w_ref[0]
    b = b_ref[0]
    rows = [jnp.sum(s[:, v:v + 1] * w, axis=0, keepdims=True) + b for v in range(N_MODVEC)]
    rows.append(jnp.zeros((SUBLANES - N_MODVEC, w.shape[1]), jnp.float32))
    o_ref[0] = jnp.concatenate(rows, axis=0)


def _modulation(cvecs_t, w_mod, b_mod):
    nt = 6 * D_MODEL // MOD_TN
    return pl.pallas_call(
        _mod_kernel,
        grid=(DEPTH, nt),
        in_specs=[
            pl.BlockSpec((D_MODEL, SUBLANES), lambda l, j: (0, 0)),
            pl.BlockSpec((1, D_MODEL, MOD_TN), lambda l, j: (l, 0, j)),
            pl.BlockSpec((1, 1, MOD_TN), lambda l, j: (l, 0, j)),
        ],
        out_specs=pl.BlockSpec((1, SUBLANES, MOD_TN), lambda l, j: (l, 0, j)),
        out_shape=jax.ShapeDtypeStruct((DEPTH, SUBLANES, 6 * D_MODEL), jnp.float32),
        compiler_params=pltpu.CompilerParams(dimension_semantics=("arbitrary", "arbitrary"),
                                             vmem_limit_bytes=VMEM_LIMIT),
        name="modulation",
    )(cvecs_t, w_mod, b_mod.reshape(DEPTH, 1, 6 * D_MODEL))


def _inproj_kernel(xp_ref, xs_ref, mod_ref, cos_ref, s1_ref, s2_ref, w_in_ref, w_uq_ref, qn_ref, kvn_ref, npm_ref,
                   q_ref, ckv_ref, kpe_ref, z_ref, xbc_ref, dt_ref):
    i = pl.program_id(0)
    is_latent = i >= N_PT
    sh = mod_ref[0, :, 0:D_MODEL]
    sc = mod_ref[0, :, D_MODEL:2 * D_MODEL]
    scale = (NOPE_DIM + ROPE_DIM) ** -0.5
    for r in range(TM // SUB_ROWS):
        rs = slice(r * SUB_ROWS, (r + 1) * SUB_ROWS)
        h = (_rms(_pick_group(i, xp_ref, xs_ref, rs), npm_ref[...]) * (1.0 + sc) + sh).astype(jnp.bfloat16)
        p = _dot(h, w_in_ref[...])
        o = 0
        cq = p[:, o:o + Q_RANK]; o += Q_RANK
        ckv = p[:, o:o + KV_RANK]; o += KV_RANK
        z_ref[rs, :] = p[:, o:o + D_SSD]; o += D_SSD
        xbc_ref[rs, :] = p[:, o:o + CONV_CH]; o += CONV_CH
        kpe = p[:, o:o + LANES]; o += LANES
        dt_ref[rs, :] = p[:, o:o + LANES]

        cos = cos_ref[rs, :]
        s1 = s1_ref[rs, :]
        s2 = s2_ref[rs, :]

        def rope(slab):
            rot = (slab * cos + pltpu.roll(slab, LANES - ROPE_DIM // 2, 1) * s1
                   + pltpu.roll(slab, ROPE_DIM // 2, 1) * s2)
            return jnp.where(is_latent, rot, slab)

        ckv_ref[rs, :] = _rms(ckv, kvn_ref[...])
        kpe_ref[rs, :] = rope(kpe)
        q = _dot(_rms(cq, qn_ref[...]).astype(jnp.bfloat16), w_uq_ref[...]) * scale
        for hd in range(MLA_HEADS):
            sl = slice(hd * HEAD_SLAB, (hd + 1) * HEAD_SLAB)
            q_ref[rs, sl] = rope(q[:, sl]).astype(jnp.bfloat16)


def _inproj(xp, xs, mod_l, tabs, w_in, w_uq, q_norm, kv_norm, npm):
    nt = N_TOK // TM
    tab_spec = pl.BlockSpec((TM, LANES), lambda i: (jnp.maximum(i - N_PT, 0) % (DEC_SEQ // TM), 0))
    row = lambda w: pl.BlockSpec((TM, w), lambda i: (i, 0))
    return pl.pallas_call(
        _inproj_kernel,
        grid=(nt,),
        in_specs=_group_specs(D_MODEL) + [
            pl.BlockSpec((1, 1, 6 * D_MODEL), lambda i: (_mod_row(i), 0, 0)),
            tab_spec, tab_spec, tab_spec,
            _const_spec((D_MODEL, IN_COLS)),
            _const_spec((Q_RANK, MLA_HEADS * HEAD_SLAB)),
            _const_spec((1, Q_RANK)),
            _const_spec((1, KV_RANK)),
            _const_spec((1, D_MODEL)),
        ],
        out_specs=[row(MLA_HEADS * HEAD_SLAB), row(KV_RANK), row(LANES), row(D_SSD), row(CONV_CH), row(LANES)],
        out_shape=[
            jax.ShapeDtypeStruct((N_TOK, MLA_HEADS * HEAD_SLAB), jnp.bfloat16),
            jax.ShapeDtypeStruct((N_TOK, KV_RANK), jnp.float32),
            jax.ShapeDtypeStruct((N_TOK, LANES), jnp.float32),
            jax.ShapeDtypeStruct((N_TOK, D_SSD), jnp.float32),
            jax.ShapeDtypeStruct((N_TOK, CONV_CH), jnp.float32),
            jax.ShapeDtypeStruct((N_TOK, LANES), jnp.float32),
        ],
        compiler_params=pltpu.CompilerParams(dimension_semantics=("arbitrary",), vmem_limit_bytes=VMEM_LIMIT),
        name="inproj",
    )(xp, xs, mod_l, *tabs, w_in, w_uq, q_norm, kv_norm, npm)


def _attn_kernel(*refs, lk_cache, lk_new):
    if lk_cache:
        q_ref, ckv_ref, kpe_ref, ckvc_ref, kpec_ref, w_ref, o_ref, k_scr, v_scr = refs
    else:
        q_ref, ckv_ref, kpe_ref, w_ref, o_ref, k_scr, v_scr = refs
    n_pair = MLA_HEADS // 2
    kcols = MLA_HEADS * HEAD_SLAB

    @pl.when(pl.program_id(1) == 0)
    def _expand_kv():
        def expand(ckv, kpe, r0):
            kv = _dot(ckv.astype(jnp.bfloat16), w_ref[...])
            rows = slice(r0, r0 + ckv.shape[0])
            for hd in range(MLA_HEADS):
                k_scr[hd, rows, :] = (kv[:, hd * HEAD_SLAB:(hd + 1) * HEAD_SLAB] + kpe).astype(jnp.bfloat16)
            for j in range(n_pair):
                v_scr[j, rows, :] = kv[:, kcols + j * LANES:kcols + (j + 1) * LANES].astype(jnp.bfloat16)

        step = 256
        for r0 in range(0, lk_cache, step):
            expand(ckvc_ref[0, r0:r0 + step, :], kpec_ref[0, r0:r0 + step, :], r0)
        for r0 in range(0, lk_new, step):
            expand(ckv_ref[r0:r0 + step, :], kpe_ref[r0:r0 + step, :], lk_cache + r0)

    low_half = lax.broadcasted_iota(jnp.int32, (TQ, LANES), 1) < V_DIM
    for j in range(n_pair):
        outs = []
        for hd in (2 * j, 2 * j + 1):
            s = _dot_nt(q_ref[:, hd * HEAD_SLAB:(hd + 1) * HEAD_SLAB], k_scr[hd])
            p = jnp.exp(s - jnp.max(s, axis=-1, keepdims=True))
            den = jnp.sum(p, axis=-1, keepdims=True)
            outs.append(_dot(p.astype(jnp.bfloat16), v_scr[j]) / den)
        o_ref[:, j * LANES:(j + 1) * LANES] = jnp.where(low_half, outs[0], outs[1]).astype(jnp.bfloat16)


def _attention(q, ckv_n, kpe, w_ukv, row_off, n_batch, seq, cache=None):
    nq = seq // TQ
    lk_cache = 0 if cache is None else cache[0].shape[1]
    lk = lk_cache + seq
    qblk = lambda b, qi: (row_off // TQ + b * nq + qi, 0)
    sblk = lambda b, qi: (row_off // seq + b, 0)
    in_specs = [
        pl.BlockSpec((TQ, MLA_HEADS * HEAD_SLAB), qblk),
        pl.BlockSpec((seq, KV_RANK), sblk),
        pl.BlockSpec((seq, LANES), sblk),
    ]
    args = [q, ckv_n, kpe]
    if cache is not None:
        in_specs += [pl.BlockSpec((1, lk_cache, KV_RANK), lambda b, qi: (b, 0, 0)),
                     pl.BlockSpec((1, lk_cache, LANES), lambda b, qi: (b, 0, 0))]
        args += list(cache)
    in_specs.append(_const_spec(w_ukv.shape))
    args.append(w_ukv)
    return pl.pallas_call(
        functools.partial(_attn_kernel, lk_cache=lk_cache, lk_new=seq),
        grid=(n_batch, nq),
        in_specs=in_specs,
        out_specs=pl.BlockSpec((TQ, MLA_HEADS * V_DIM), lambda b, qi: (b * nq + qi, 0)),
        out_shape=jax.ShapeDtypeStruct((n_batch * seq, MLA_HEADS * V_DIM), jnp.bfloat16),
        scratch_shapes=[pltpu.VMEM((MLA_HEADS, lk, HEAD_SLAB), jnp.bfloat16),
                        pltpu.VMEM((MLA_HEADS // 2, lk, LANES), jnp.bfloat16)],
        compiler_params=pltpu.CompilerParams(dimension_semantics=("arbitrary", "arbitrary"),
                                             vmem_limit_bytes=VMEM_LIMIT),
        name=f"attention_{seq}",
    )(*args)


def _split3(a):
    a_hi = a.astype(jnp.bfloat16)
    r1 = a - a_hi.astype(jnp.float32)
    a_mid = r1.astype(jnp.bfloat16)
    a_lo = (r1 - a_mid.astype(jnp.float32)).astype(jnp.bfloat16)
    return a_hi, a_mid, a_lo


def _ssd_kernel(xbc_ref, dt_ref, h0f_ref, h0b_ref, cw_ref, cb_ref, dtbc_ref, alogc_ref, dsk_ref,
                y_ref, hf_ref, hb_ref, xs_scr, c_scr, bt_scr, dts_scr, cumt_scr, wt_scr, cum_scr,
                stf_scr, stb_scr, *, seq):
    nc = seq // CHUNK
    gs = SSD_GROUPS * SSD_STATE
    nh2 = 2 * SSD_HEADS

    row = lax.broadcasted_iota(jnp.int32, (CHUNK, CHUNK), 0)
    col = lax.broadcasted_iota(jnp.int32, (CHUNK, CHUNK), 1)
    low_half = col < SSD_HEAD_DIM
    lower = row >= col
    upper = row <= col
    lower_b = lower.astype(jnp.bfloat16)
    upper_b = upper.astype(jnp.bfloat16)
    fwd_rows = lax.broadcasted_iota(jnp.int32, (nh2, CHUNK), 0) < SSD_HEADS
    fwd_cols = lax.broadcasted_iota(jnp.int32, (CHUNK, nh2), 1) < SSD_HEADS
    a_neg_c = -jnp.exp(alogc_ref[:nh2, :])
    dtb_c = dtbc_ref[:nh2, :]

    def head_terms(c):
        r0 = pl.multiple_of(c * CHUNK, CHUNK)
        dts_t = _softplus(dt_ref[pl.ds(r0, CHUNK), :].T[:nh2, :] + dtb_c)
        pieces = _split3(dts_t * a_neg_c)
        cum_t = jnp.where(fwd_rows, sum(_dot(p, upper_b) for p in pieces),
                          sum(_dot(p, lower_b) for p in pieces))
        cum = jnp.where(fwd_cols, sum(_dot_nt(lower_b, p) for p in pieces),
                        sum(_dot_nt(upper_b, p) for p in pieces))
        tot = jnp.where(fwd_rows[:, :1], cum_t[:, CHUNK - 1:], cum_t[:, :1])
        hrows = pl.ds(pl.multiple_of(c * nh2, nh2), nh2)
        dts_scr[hrows, :] = dts_t
        cumt_scr[hrows, :] = cum_t
        wt_scr[hrows, :] = dts_t * jnp.exp(tot - cum_t)
        cum_scr[pl.ds(r0, CHUNK), :] = cum

    def prep_chunk(c, carry):
        r0 = pl.multiple_of(c * CHUNK, CHUNK)
        rows = pl.ds(r0, CHUNK)
        rows_prev = pl.ds(pl.multiple_of(jnp.maximum(r0 - HALO, 0), HALO), HALO)
        rows_next = pl.ds(pl.multiple_of(jnp.minimum(r0 + CHUNK, seq - HALO), HALO), HALO)

        def conv_tile(cs):
            prev = jnp.where(c > 0, xbc_ref[rows_prev, cs], 0.0)
            nxt = jnp.where(c < nc - 1, xbc_ref[rows_next, cs], 0.0)
            win = jnp.concatenate([prev, xbc_ref[rows, cs], nxt], axis=0)
            acc = jnp.broadcast_to(cb_ref[:, cs], (CHUNK, LANES))
            for k in range(CONV_W):
                lo = HALO - CONV_W // 2 + k
                acc = acc + cw_ref[k:k + 1, cs] * win[lo:lo + CHUNK, :]
            return _silu(acc)

        def x_tile(j, carry):
            cs = pl.ds(pl.multiple_of(j * LANES, LANES), LANES)
            u = conv_tile(cs)
            y_ref[rows, cs] = dsk_ref[:, cs] * u
            xs_scr[rows, cs] = u.astype(jnp.bfloat16)
            return carry

        lax.fori_loop(0, D_SSD // LANES, x_tile, 0)
        for g in range(SSD_GROUPS):
            b0 = pl.multiple_of(c * gs + g * SSD_STATE, SSD_STATE)
            bt_scr[pl.ds(b0, SSD_STATE), :] = conv_tile(slice(D_SSD + g * SSD_STATE, D_SSD + (g + 1) * SSD_STATE)).T
            c_scr[rows, g * SSD_STATE:(g + 1) * SSD_STATE] = conv_tile(
                slice(D_SSD + gs + g * SSD_STATE, D_SSD + gs + (g + 1) * SSD_STATE)).astype(jnp.bfloat16)
        head_terms(c)
        return carry

    lax.fori_loop(0, nc, prep_chunk, 0)

    stf_scr[...] = h0f_ref[0].T
    stb_scr[...] = h0b_ref[0].T

    def scan_chunk(ci, st_scr, reverse):
        lane0 = SSD_HEADS if reverse else 0
        causal = upper if reverse else lower
        last = 0 if reverse else CHUNK - 1
        r0 = pl.multiple_of(ci * CHUNK, CHUNK)
        rows = pl.ds(r0, CHUNK)
        hrows = pl.ds(pl.multiple_of(ci * nh2, nh2), nh2)
        xs_b = xs_scr[rows, :]
        c_b = c_scr[rows, :]
        dts_t = dts_scr[hrows, :]
        cum_t = cumt_scr[hrows, :]
        w_t = wt_scr[hrows, :]
        cum = cum_scr[rows, :]
        st = st_scr[...]
        for g in range(SSD_GROUPS):
            cg = c_b[:, g * SSD_STATE:(g + 1) * SSD_STATE]
            bt = bt_scr[pl.ds(pl.multiple_of(ci * gs + g * SSD_STATE, SSD_STATE), SSD_STATE), :]
            cbm = _dot(cg, bt.astype(jnp.bfloat16))
            gcols = slice(g * SSD_HPG * SSD_HEAD_DIM, (g + 1) * SSD_HPG * SSD_HEAD_DIM)
            z = _dot(cg, st[:, gcols].astype(jnp.bfloat16))
            for jj in range(SSD_HPG // 2):
                pair = g * (SSD_HPG // 2) + jj
                pcols = slice(pair * LANES, (pair + 1) * LANES)
                lhs_y, lhs_s, entry = [], [], []
                for hd in (2 * pair, 2 * pair + 1):
                    ln = lane0 + hd
                    cum_i = jnp.broadcast_to(cum[:, ln:ln + 1], (CHUNK, CHUNK))
                    dec = jnp.exp(jnp.where(causal, cum_i - cum_t[ln:ln + 1, :], -jnp.inf))
                    lhs_y.append((cbm * dec * dts_t[ln:ln + 1, :]).astype(jnp.bfloat16))
                    lhs_s.append((bt * w_t[ln:ln + 1, :]).astype(jnp.bfloat16))
                    entry.append(jnp.exp(cum_i))
                out = _dot(jnp.concatenate(lhs_y + lhs_s, axis=0), xs_b[:, pcols])
                ea = jnp.where(low_half, entry[0], entry[1])
                y_ref[rows, pcols] += (jnp.where(low_half, out[:CHUNK], out[CHUNK:2 * CHUNK])
                                       + z[:, jj * LANES:(jj + 1) * LANES] * ea)
                st_scr[:, pcols] = (ea[last:last + 1, :] * st[:, pcols]
                                    + jnp.where(low_half, out[2 * CHUNK:3 * CHUNK], out[3 * CHUNK:]))

    def both(c, carry):
        scan_chunk(c, stf_scr, False)
        scan_chunk(nc - 1 - c, stb_scr, True)
        return carry

    lax.fori_loop(0, nc, both, 0)
    hf_ref[0] = stf_scr[...].T
    hb_ref[0] = stb_scr[...].T


def _ssd(xbc, dt, h0f, h0b, conv_w, conv_b, dt_bias, a_log, d_skip, row_off, n_batch, seq):
    hp = SSD_HEADS * SSD_HEAD_DIM
    gs = SSD_GROUPS * SSD_STATE
    nc = seq // CHUNK
    sblk = lambda w: pl.BlockSpec((seq, w), lambda b: (row_off // seq + b, 0))
    st_spec = pl.BlockSpec((1, hp, SSD_STATE), lambda b: (b, 0, 0))
    st_shape = jax.ShapeDtypeStruct((n_batch, hp, SSD_STATE), jnp.float32)
    return pl.pallas_call(
        functools.partial(_ssd_kernel, seq=seq),
        grid=(n_batch,),
        in_specs=[sblk(CONV_CH), sblk(LANES), st_spec, st_spec,
                  _const_spec((SUBLANES, CONV_CH)), _const_spec((1, CONV_CH)),
                  _const_spec((LANES, 1)), _const_spec((LANES, 1)), _const_spec((1, D_SSD))],
        out_specs=[pl.BlockSpec((seq, D_SSD), lambda b: (b, 0)), st_spec, st_spec],
        out_shape=[jax.ShapeDtypeStruct((n_batch * seq, D_SSD), jnp.float32), st_shape, st_shape],
        scratch_shapes=[pltpu.VMEM((seq, D_SSD), jnp.bfloat16),
                        pltpu.VMEM((seq, gs), jnp.bfloat16),
                        pltpu.VMEM((nc * gs, CHUNK), jnp.float32),
                        pltpu.VMEM((nc * 2 * SSD_HEADS, CHUNK), jnp.float32),
                        pltpu.VMEM((nc * 2 * SSD_HEADS, CHUNK), jnp.float32),
                        pltpu.VMEM((nc * 2 * SSD_HEADS, CHUNK), jnp.float32),
                        pltpu.VMEM((seq, 2 * SSD_HEADS), jnp.float32),
                        pltpu.VMEM((SSD_STATE, hp), jnp.float32),
                        pltpu.VMEM((SSD_STATE, hp), jnp.float32)],
        compiler_params=pltpu.CompilerParams(dimension_semantics=("arbitrary",), vmem_limit_bytes=VMEM_LIMIT),
        name=f"ssd_{seq}",
    )(xbc, dt, h0f, h0b, conv_w, conv_b, dt_bias.T, a_log.T, d_skip)


def _post_mix(x, mix, mod_ref, npost_ref, npre_ref):
    d = D_MODEL
    gate_mix = mod_ref[0, :, 2 * d:3 * d]
    shf = mod_ref[0, :, 3 * d:4 * d]
    scf = mod_ref[0, :, 4 * d:5 * d]
    x1 = x + gate_mix * _rms(mix, npost_ref[...])
    return x1, (_rms(x1, npre_ref[...]) * (1.0 + scf) + shf).astype(jnp.bfloat16)


def _subtile_pipeline(n_sub, mixer_pre, mixer_dots, mod_ref, nffn_ref, wg_ref, wu_ref, wd_ref, interleave):
    gate_ffn = mod_ref[0, :, 5 * D_MODEL:6 * D_MODEL]
    ffn_up = lambda h: (_silu(_dot(h, wg_ref[0])) * _dot(h, wu_ref[0])).astype(jnp.bfloat16)
    ffn_down = lambda x1, hid: x1 + gate_ffn * _rms(_dot(hid, wd_ref[0]), nffn_ref[...])
    if not interleave:
        staged = [mixer_dots(r, mixer_pre(r)) for r in range(n_sub)]
        return [ffn_down(x1, ffn_up(h)) for x1, h in staged]
    outs = []
    x1, h = mixer_dots(0, mixer_pre(0))
    for r in range(n_sub):
        nxt_pre = mixer_pre(r + 1) if r + 1 < n_sub else None
        hid = ffn_up(h)
        nxt = mixer_dots(r + 1, nxt_pre) if r + 1 < n_sub else None
        outs.append(ffn_down(x1, hid))
        if nxt is not None:
            x1, h = nxt
    return outs


def _ffn_specs(l):
    return [_const_spec((1, D_MODEL)), _const_spec((1, D_MODEL)), _const_spec((1, D_MODEL)),
            _layer_spec((D_MODEL, D_FF), l), _layer_spec((D_MODEL, D_FF), l), _layer_spec((D_FF, D_MODEL), l)]


def _outproj_ffn_kernel(xp_ref, xs_ref, attp_ref, atts_ref, yp_ref, ys_ref, z_ref, mod_ref, sn_ref, wo_ref,
                        npost_ref, npre_ref, nffn_ref, wg_ref, wu_ref, wd_ref, o_ref):
    i = pl.program_id(0)
    gw = D_SSD // SSD_GROUPS
    sub = lambda r: slice(r * SUB_ROWS, (r + 1) * SUB_ROWS)

    def mixer_pre(r):
        yg = _pick_group(i, yp_ref, ys_ref, sub(r)) * _silu(z_ref[sub(r), :])
        parts = [_pick_group(i, attp_ref, atts_ref, sub(r))]
        for g in range(SSD_GROUPS):
            parts.append(_rms(yg[:, g * gw:(g + 1) * gw], sn_ref[:, g * gw:(g + 1) * gw]).astype(jnp.bfloat16))
        return jnp.concatenate(parts, axis=1)

    def mixer_dots(r, cat):
        return _post_mix(_pick_group(i, xp_ref, xs_ref, sub(r)), _dot(cat, wo_ref[...]), mod_ref, npost_ref, npre_ref)

    outs = _subtile_pipeline(TM // SUB_ROWS, mixer_pre, mixer_dots, mod_ref, nffn_ref, wg_ref, wu_ref, wd_ref,
                             interleave=False)
    for r, res in enumerate(outs):
        o_ref[sub(r), :] = res


def _outproj_ffn(l, xp, xs, att_p, att_s, y_p, y_s, z, mod_l, ssd_norm, w_out, npost, npre, nffn, wg, wu, wd):
    row = lambda w: pl.BlockSpec((TM, w), lambda i: (i, 0))
    return pl.pallas_call(
        _outproj_ffn_kernel,
        grid=(N_TOK // TM,),
        in_specs=_group_specs(D_MODEL) + _group_specs(MLA_HEADS * V_DIM) + _group_specs(D_SSD) + [
            row(D_SSD),
            pl.BlockSpec((1, 1, 6 * D_MODEL), lambda i: (_mod_row(i), 0, 0)),
            _const_spec((1, D_SSD)), _const_spec((MLA_HEADS * V_DIM + D_SSD, D_MODEL))] + _ffn_specs(l),
        out_specs=row(D_MODEL),
        out_shape=jax.ShapeDtypeStruct((N_TOK, D_MODEL), jnp.float32),
        compiler_params=pltpu.CompilerParams(dimension_semantics=("arbitrary",), vmem_limit_bytes=VMEM_LIMIT),
        name="outproj_ffn",
    )(xp, xs, att_p, att_s, y_p, y_s, z, mod_l, ssd_norm, w_out, npost, npre, nffn, wg, wu, wd)


def _pool_ffn_kernel(x_ref, xp_ref, xn_ref, mod_ref, nmix_ref, pw_ref, ps_ref,
                     npost_ref, npre_ref, nffn_ref, wg_ref, wu_ref, wd_ref, op_ref, os_ref):
    i = pl.program_id(0)
    seq = jnp.where(i < N_PROMPT // TM, SEQ, DEC_SEQ)
    pos0 = (i * TM) % seq
    sh = mod_ref[0, :, 0:D_MODEL]
    sc = mod_ref[0, :, D_MODEL:2 * D_MODEL]
    hmod = lambda v: _rms(v, nmix_ref[...]) * (1.0 + sc) + sh
    n_rows = POOL_SUB + 2 * HALO

    def shifted(v, s):
        return pltpu.roll(v, n_rows - s, 0)

    def mixer_pre(s):
        lo, hi = s * POOL_SUB, (s + 1) * POOL_SUB
        pos_s = (pos0 + lo) % seq
        h = hmod(x_ref[lo:hi, :])
        before = hmod(xp_ref[...] if s == 0 else x_ref[lo - HALO:lo, :])
        after = hmod(xn_ref[...] if hi == TM else x_ref[hi:hi + HALO, :])
        before = jnp.where(pos_s > 0, before, 0.0)
        after = jnp.where(pos_s + POOL_SUB < seq, after, 0.0)
        padded = jnp.concatenate([before, h, after], axis=0)
        pos = pos_s + lax.broadcasted_iota(jnp.int32, (POOL_SUB, 1), 0)
        pooled = []
        for gi, w in enumerate(POOL_WINDOWS):
            cols = slice(gi * POOL_GC, (gi + 1) * POOL_GC)
            t = padded[:, cols]
            span = 1
            while span < w:
                t = t + shifted(t, span)
                span *= 2
            lead = HALO - w // 2
            win_sum = (shifted(t, lead) if lead else t)[:POOL_SUB, :]
            cnt = (jnp.minimum(pos + w // 2, seq) - jnp.maximum(pos - w // 2, 0)).astype(jnp.float32)
            pooled.append((win_sum / cnt - h[:, cols]).astype(jnp.bfloat16))
        return pooled

    def mixer_dots(s, pooled):
        mix = jnp.concatenate([_dot(p, pw_ref[gi]) for gi, p in enumerate(pooled)], axis=1) * ps_ref[...]
        return _post_mix(x_ref[s * POOL_SUB:(s + 1) * POOL_SUB, :], mix, mod_ref, npost_ref, npre_ref)

    res = jnp.concatenate(_subtile_pipeline(TM // POOL_SUB, mixer_pre, mixer_dots, mod_ref, nffn_ref,
                                            wg_ref, wu_ref, wd_ref, interleave=True), axis=0)

    @pl.when(i == 0)
    def _():
        os_ref[...] = jnp.zeros_like(os_ref)

    @pl.when(i < N_PT)
    def _():
        op_ref[...] = res

    @pl.when(i >= N_PT)
    def _():
        os_ref[...] = res


def _pool_ffn(l, xa, mod_l, nmix, pool_w, pool_scale, npost, npre, nffn, wg, wu, wd):
    hb = TM // HALO
    nh = N_TOK // HALO
    return pl.pallas_call(
        _pool_ffn_kernel,
        grid=(N_TOK // TM,),
        in_specs=[pl.BlockSpec((TM, D_MODEL), lambda i: (i, 0)),
                  pl.BlockSpec((HALO, D_MODEL), lambda i: (jnp.maximum(i * hb - 1, 0), 0)),
                  pl.BlockSpec((HALO, D_MODEL), lambda i: (jnp.minimum((i + 1) * hb, nh - 1), 0)),
                  pl.BlockSpec((1, 1, 6 * D_MODEL), lambda i: (_mod_row(i), 0, 0)),
                  _const_spec((1, D_MODEL)),
                  _const_spec((len(POOL_WINDOWS), POOL_GC, POOL_GC)),
                  _const_spec((1, D_MODEL))] + _ffn_specs(l),
        out_specs=_group_specs(D_MODEL),
        out_shape=[jax.ShapeDtypeStruct((N_PROMPT, D_MODEL), jnp.float32),
                   jax.ShapeDtypeStruct((N_SAMPLE, D_MODEL), jnp.float32)],
        compiler_params=pltpu.CompilerParams(dimension_semantics=("arbitrary",), vmem_limit_bytes=VMEM_LIMIT),
        name="pool_ffn",
    )(xa, xa, xa, mod_l, nmix, pool_w, pool_scale, npost, npre, nffn, wg, wu, wd)


def _rope_tables():
    rows = DEC_SEQ // GRID_W
    r = jnp.repeat(jnp.arange(rows, dtype=jnp.float32), GRID_W)
    c = jnp.tile(jnp.arange(GRID_W, dtype=jnp.float32), rows)
    half = ROPE_DIM // 2
    inv_freq = jnp.power(ROPE_THETA, -jnp.arange(0, half, 2, dtype=jnp.float32) / half)
    ang = jnp.concatenate([r[:, None] * inv_freq, c[:, None] * inv_freq], axis=-1)
    cos, sin = jnp.cos(ang), jnp.sin(ang)
    zl = jnp.zeros((DEC_SEQ, KPE_LANE0), jnp.float32)
    zr = jnp.zeros((DEC_SEQ, LANES - KPE_LANE0 - ROPE_DIM), jnp.float32)
    zh = jnp.zeros((DEC_SEQ, half), jnp.float32)
    cos_t = jnp.concatenate([zl + 1.0, cos, cos, zr + 1.0], axis=1)
    s1_t = jnp.concatenate([zl, -sin, zh, zr], axis=1)
    s2_t = jnp.concatenate([zl, zh, sin, zr], axis=1)
    return cos_t, s1_t, s2_t


def _kpe_slab(k):
    pad = [(0, 0)] * (k.ndim - 1) + [(KPE_LANE0, LANES - KPE_LANE0 - ROPE_DIM)]
    return jnp.pad(k, pad)


def _layout_in_proj(w):
    o = np.cumsum((0, Q_RANK, KV_RANK, ROPE_DIM, D_SSD, CONV_CH, SSD_HEADS, SSD_HEADS))
    cq, ckv, kpe, z, xbc, dtf, dtb = (w[:, o[k]:o[k + 1]] for k in range(7))
    dt = jnp.pad(jnp.concatenate([dtf, dtb], axis=1), ((0, 0), (0, LANES - 2 * SSD_HEADS)))
    return jnp.concatenate([cq, ckv, z, xbc, _kpe_slab(kpe), dt], axis=1).astype(jnp.bfloat16)


def _layout_uq(w):
    w = w.reshape(Q_RANK, MLA_HEADS, NOPE_DIM + ROPE_DIM)
    w = jnp.pad(w, ((0, 0), (0, 0), (0, HEAD_SLAB - NOPE_DIM - ROPE_DIM)))
    return w.reshape(Q_RANK, MLA_HEADS * HEAD_SLAB).astype(jnp.bfloat16)


def _layout_ukv(w):
    w = w.reshape(KV_RANK, MLA_HEADS, NOPE_DIM + V_DIM)
    kn = jnp.pad(w[:, :, :NOPE_DIM], ((0, 0), (0, 0), (0, HEAD_SLAB - NOPE_DIM)))
    v = w[:, :, NOPE_DIM:]
    return jnp.concatenate([kn.reshape(KV_RANK, -1), v.reshape(KV_RANK, -1)], axis=1).astype(jnp.bfloat16)


def _lane_row(fwd, bwd):
    return jnp.pad(jnp.concatenate([fwd, bwd]), (0, LANES - 2 * SSD_HEADS)).reshape(1, LANES)


def kernel(x_prompt, x_sample, c, cache_mla_ckv, cache_mla_krope, state_ssd_fwd, state_ssd_bwd, c_ctx, w_mod, b_mod, norm_pre_mix, norm_post_mix, norm_pre_ffn, norm_post_ffn, w_in_ab, q_norm, w_uq, kv_norm, w_ukv, ssd_conv_w, ssd_conv_b, ssd_dt_bias_fwd, ssd_dt_bias_bwd, ssd_a_log_fwd, ssd_a_log_bwd, ssd_d, ssd_norm, w_out_ab, pool_w, pool_scale, ffn_w_gate, ffn_w_up, ffn_w_down):
    f32, bf16 = jnp.float32, jnp.bfloat16
    assert DEPTH == 2
    xp = x_prompt.reshape(N_PROMPT, D_MODEL)
    xs = x_sample.reshape(N_SAMPLE, D_MODEL)
    cvecs = jnp.concatenate([c_ctx[None, :], c, jnp.zeros((SUBLANES - N_MODVEC, D_MODEL), f32)], axis=0)
    mod = _modulation(cvecs.T, w_mod, b_mod).reshape(DEPTH, SUBLANES, 1, 6 * D_MODEL)
    tabs = _rope_tables()
    hp = SSD_HEADS * SSD_HEAD_DIM
    row = lambda v: v.reshape(1, -1)
    wg, wu, wd = ffn_w_gate.astype(bf16), ffn_w_up.astype(bf16), ffn_w_down.astype(bf16)
    new_ckv, new_kpe, new_hf, new_hb = [], [], [], []

    for l in range(DEPTH):
        ffn = (row(norm_post_mix[l]), row(norm_pre_ffn[l]), row(norm_post_ffn[l]), wg, wu, wd)
        if l % 2 == 0:
            i = l // 2
            q, ckv_n, kpe, z, xbc, dt = _inproj(
                xp, xs, mod[l], tabs, _layout_in_proj(w_in_ab[i]), _layout_uq(w_uq[i]),
                row(q_norm[i]), row(kv_norm[i]), row(norm_pre_mix[l]))
            w_kv = _layout_ukv(w_ukv[i])
            att_p = _attention(q, ckv_n, kpe, w_kv, 0, BATCH, SEQ)
            att_s = _attention(q, ckv_n, kpe, w_kv, N_PROMPT, DEC_BATCH, DEC_SEQ,
                               cache=(cache_mla_ckv[:, i], _kpe_slab(cache_mla_krope[:, i])))
            ssd_args = (jnp.pad(ssd_conv_w[i], ((0, SUBLANES - CONV_W), (0, 0))), row(ssd_conv_b[i]),
                        _lane_row(ssd_dt_bias_fwd[i], ssd_dt_bias_bwd[i]),
                        _lane_row(ssd_a_log_fwd[i], ssd_a_log_bwd[i]),
                        row(jnp.repeat(ssd_d[i], SSD_HEAD_DIM)))
            h_zero = jnp.zeros((BATCH, hp, SSD_STATE), f32)
            y_p, hf, hb = _ssd(xbc, dt, h_zero, h_zero, *ssd_args, 0, BATCH, SEQ)
            y_s, _, _ = _ssd(xbc, dt, state_ssd_fwd[:, i].reshape(DEC_BATCH, hp, SSD_STATE),
                             state_ssd_bwd[:, i].reshape(DEC_BATCH, hp, SSD_STATE),
                             *ssd_args, N_PROMPT, DEC_BATCH, DEC_SEQ)
            xa = _outproj_ffn(l, xp, xs, att_p, att_s, y_p, y_s, z,
                              mod[l], row(ssd_norm[i]), w_out_ab[i].astype(bf16), *ffn)
            new_ckv.append(ckv_n[:N_PROMPT].reshape(BATCH, SEQ, KV_RANK))
            new_kpe.append(kpe[:N_PROMPT, KPE_LANE0:KPE_LANE0 + ROPE_DIM].reshape(BATCH, SEQ, ROPE_DIM))
            new_hf.append(hf.reshape(BATCH, SSD_HEADS, SSD_HEAD_DIM, SSD_STATE))
            new_hb.append(hb.reshape(BATCH, SSD_HEADS, SSD_HEAD_DIM, SSD_STATE))
        else:
            j = l // 2
            yp, ys = _pool_ffn(l, xa, mod[l], row(norm_pre_mix[l]), pool_w[j].astype(bf16), row(pool_scale[j]), *ffn)

    return (yp.reshape(BATCH, SEQ, D_MODEL), ys.reshape(DEC_BATCH, DEC_SEQ, D_MODEL),
            jnp.stack(new_ckv, axis=1), jnp.stack(new_kpe, axis=1),
            jnp.stack(new_hf, axis=1), jnp.stack(new_hb, axis=1))
```

```python
import functools

import numpy as np
import jax
import jax.numpy as jnp
from jax import lax
from jax.experimental import pallas as pl
from jax.experimental.pallas import tpu as pltpu

D_MODEL = 1024
BATCH = 16
SEQ = 256
DEPTH = 2
DEC_BATCH = 2
DEC_SEQ = 2048
PAST_LEN = 256
GRID_W = 64
EPS = 1e-6
MLA_HEADS = 8
Q_RANK = 256
KV_RANK = 256
NOPE_DIM = 64
ROPE_DIM = 32
V_DIM = 64
ROPE_THETA = 10000.0
SSD_HEADS = 8
SSD_GROUPS = 2
SSD_HPG = SSD_HEADS // SSD_GROUPS
SSD_HEAD_DIM = 64
SSD_STATE = 128
D_SSD = SSD_HEADS * SSD_HEAD_DIM
CONV_W = 5
CONV_CH = D_SSD + 2 * SSD_GROUPS * SSD_STATE
POOL_WINDOWS = (2, 4, 8, 16)
POOL_GC = D_MODEL // len(POOL_WINDOWS)
D_FF = ((8 * D_MODEL + 3 * 256 - 1) // (3 * 256)) * 256

SUBLANES = 8
LANES = 128

N_PROMPT = BATCH * SEQ
N_SAMPLE = DEC_BATCH * DEC_SEQ
N_TOK = N_PROMPT + N_SAMPLE
N_MODVEC = 1 + DEC_BATCH
TM = 512
TQ = 256
CHUNK = 128
HALO = SUBLANES
HEAD_SLAB = LANES
IN_COLS = Q_RANK + KV_RANK + D_SSD + CONV_CH + 2 * LANES
KPE_LANE0 = NOPE_DIM
VMEM_LIMIT = 56 * 1024 * 1024

POOL_SUB = min(SEQ, DEC_SEQ)

assert TM % POOL_SUB == 0 and SEQ % POOL_SUB == 0 and DEC_SEQ % TM == 0 and N_PROMPT % DEC_SEQ == 0


def _rms(x, g):
    return x * lax.rsqrt(jnp.mean(x * x, axis=-1, keepdims=True) + EPS) * g


def _silu(x):
    return x * jax.nn.sigmoid(x)


def _softplus(x):
    return jnp.maximum(x, 0.0) + jnp.log1p(jnp.exp(-jnp.abs(x)))


def _dot(a, b):
    return jnp.dot(a, b, preferred_element_type=jnp.float32)


def _dot_nt(a, b):
    return lax.dot_general(a, b, (((1,), (1,)), ((), ())), preferred_element_type=jnp.float32)


def _mod_row(i):
    return jnp.where(i < N_PROMPT // TM, 0, 1 + (i - N_PROMPT // TM) // (DEC_SEQ // TM))


def _const_spec(shape):
    nd = len(shape)
    return pl.BlockSpec(shape, lambda *_: (0,) * nd, pipeline_mode=pl.Buffered(1))


def _layer_spec(shape, l):
    nd = len(shape)
    return pl.BlockSpec((1,) + tuple(shape), lambda *_: (l,) + (0,) * nd, pipeline_mode=pl.Buffered(1))


N_PT = N_PROMPT // TM


def _group_specs(width):
    return [pl.BlockSpec((TM, width), lambda i: (jnp.minimum(i, N_PT - 1), 0)),
            pl.BlockSpec((TM, width), lambda i: (jnp.maximum(i - N_PT, 0), 0))]


def _pick_group(i, p_ref, s_ref, rows=slice(None)):
    return jnp.where(i < N_PT, p_ref[rows, :], s_ref[rows, :])


MOD_TN = 2048
SUB_ROWS = 256


def _mod_kernel(ct_ref, w_ref, b_ref, o_ref):
    s = _silu(ct_ref[...])
    w = w_ref[0]
    b = b_ref[0]
    rows = [jnp.sum(s[:, v:v + 1] * w, axis=0, keepdims=True) + b for v in range(N_MODVEC)]
    rows.append(jnp.zeros((SUBLANES - N_MODVEC, w.shape[1]), jnp.float32))
    o_ref[0] = jnp.concatenate(rows, axis=0)


def _modulation(cvecs_t, w_mod, b_mod):
    nt = 6 * D_MODEL // MOD_TN
    return pl.pallas_call(
        _mod_kernel,
        grid=(DEPTH, nt),
        in_specs=[
            pl.BlockSpec((D_MODEL, SUBLANES), lambda l, j: (0, 0)),
            pl.BlockSpec((1, D_MODEL, MOD_TN), lambda l, j: (l, 0, j)),
            pl.BlockSpec((1, 1, MOD_TN), lambda l, j: (l, 0, j)),
        ],
        out_specs=pl.BlockSpec((1, SUBLANES, MOD_TN), lambda l, j: (l, 0, j)),
        out_shape=jax.ShapeDtypeStruct((DEPTH, SUBLANES, 6 * D_MODEL), jnp.float32),
        compiler_params=pltpu.CompilerParams(dimension_semantics=("arbitrary", "arbitrary"),
                                             vmem_limit_bytes=VMEM_LIMIT),
        name="modulation",
    )(cvecs_t, w_mod, b_mod.reshape(DEPTH, 1, 6 * D_MODEL))


def _inproj_kernel(xp_ref, xs_ref, mod_ref, cos_ref, s1_ref, s2_ref, w_in_ref, w_uq_ref, qn_ref, kvn_ref, npm_ref,
                   q_ref, ckv_ref, kpe_ref, z_ref, xbc_ref, dt_ref):
    i = pl.program_id(0)
    is_latent = i >= N_PT
    sh = mod_ref[0, :, 0:D_MODEL]
    sc = mod_ref[0, :, D_MODEL:2 * D_MODEL]
    scale = (NOPE_DIM + ROPE_DIM) ** -0.5 * np.log2(np.e)
    for r in range(TM // SUB_ROWS):
        rs = slice(r * SUB_ROWS, (r + 1) * SUB_ROWS)
        h = (_rms(_pick_group(i, xp_ref, xs_ref, rs), npm_ref[...]) * (1.0 + sc) + sh).astype(jnp.bfloat16)
        p = _dot(h, w_in_ref[...])
        o = 0
        cq = p[:, o:o + Q_RANK]; o += Q_RANK
        ckv = p[:, o:o + KV_RANK]; o += KV_RANK
        z_ref[rs, :] = p[:, o:o + D_SSD]; o += D_SSD
        xbc_ref[rs, :] = p[:, o:o + CONV_CH]; o += CONV_CH
        kpe = p[:, o:o + LANES]; o += LANES
        dt_ref[rs, :] = p[:, o:o + LANES]

        cos = cos_ref[rs, :]
        s1 = s1_ref[rs, :]
        s2 = s2_ref[rs, :]

        def rope(slab):
            rot = (slab * cos + pltpu.roll(slab, LANES - ROPE_DIM // 2, 1) * s1
                   + pltpu.roll(slab, ROPE_DIM // 2, 1) * s2)
            return jnp.where(is_latent, rot, slab)

        ckv_ref[rs, :] = _rms(ckv, kvn_ref[...])
        kpe_ref[rs, :] = rope(kpe)
        q = _dot(_rms(cq, qn_ref[...]).astype(jnp.bfloat16), w_uq_ref[...]) * scale
        for hd in range(MLA_HEADS):
            sl = slice(hd * HEAD_SLAB, (hd + 1) * HEAD_SLAB)
            q_ref[rs, sl] = rope(q[:, sl]).astype(jnp.bfloat16)


def _inproj(xp, xs, mod_l, tabs, w_in, w_uq, q_norm, kv_norm, npm):
    nt = N_TOK // TM
    tab_spec = pl.BlockSpec((TM, LANES), lambda i: (jnp.maximum(i - N_PT, 0) % (DEC_SEQ // TM), 0))
    row = lambda w: pl.BlockSpec((TM, w), lambda i: (i, 0))
    return pl.pallas_call(
        _inproj_kernel,
        grid=(nt,),
        in_specs=_group_specs(D_MODEL) + [
            pl.BlockSpec((1, 1, 6 * D_MODEL), lambda i: (_mod_row(i), 0, 0)),
            tab_spec, tab_spec, tab_spec,
            _const_spec((D_MODEL, IN_COLS)),
            _const_spec((Q_RANK, MLA_HEADS * HEAD_SLAB)),
            _const_spec((1, Q_RANK)),
            _const_spec((1, KV_RANK)),
            _const_spec((1, D_MODEL)),
        ],
        out_specs=[row(MLA_HEADS * HEAD_SLAB), row(KV_RANK), row(LANES), row(D_SSD), row(CONV_CH), row(LANES)],
        out_shape=[
            jax.ShapeDtypeStruct((N_TOK, MLA_HEADS * HEAD_SLAB), jnp.bfloat16),
            jax.ShapeDtypeStruct((N_TOK, KV_RANK), jnp.float32),
            jax.ShapeDtypeStruct((N_TOK, LANES), jnp.float32),
            jax.ShapeDtypeStruct((N_TOK, D_SSD), jnp.float32),
            jax.ShapeDtypeStruct((N_TOK, CONV_CH), jnp.float32),
            jax.ShapeDtypeStruct((N_TOK, LANES), jnp.float32),
        ],
        compiler_params=pltpu.CompilerParams(dimension_semantics=("arbitrary",), vmem_limit_bytes=VMEM_LIMIT),
        name="inproj",
    )(xp, xs, mod_l, *tabs, w_in, w_uq, q_norm, kv_norm, npm)


def _attn_kernel(*refs, lk_cache, lk_new):
    if lk_cache:
        q_ref, ckv_ref, kpe_ref, ckvc_ref, kpec_ref, wk_ref, wvt_ref, o_ref, k_scr, vt_scr = refs
    else:
        q_ref, ckv_ref, kpe_ref, wk_ref, wvt_ref, o_ref, k_scr, vt_scr = refs

    @pl.when(pl.program_id(1) == 0)
    def _expand_kv():
        def expand(ckv, kpe, r0):
            ckv_b = ckv.astype(jnp.bfloat16)
            kn = _dot(ckv_b, wk_ref[...])
            rows = slice(r0, r0 + ckv.shape[0])
            for hd in range(MLA_HEADS):
                k_scr[hd, rows, :] = (kn[:, hd * HEAD_SLAB:(hd + 1) * HEAD_SLAB] + kpe).astype(jnp.bfloat16)
            vt_scr[:, rows] = _dot_nt(wvt_ref[...], ckv_b).astype(jnp.bfloat16)

        step = 256
        for r0 in range(0, lk_cache, step):
            expand(ckvc_ref[0, r0:r0 + step, :], kpec_ref[0, r0:r0 + step, :], r0)
        for r0 in range(0, lk_new, step):
            expand(ckv_ref[r0:r0 + step, :], kpe_ref[r0:r0 + step, :], lk_cache + r0)

    scores = [_dot_nt(k_scr[hd], q_ref[:, hd * HEAD_SLAB:(hd + 1) * HEAD_SLAB]) for hd in range(MLA_HEADS)]
    outs = []
    for hd, s_t in enumerate(scores):
        p_t = jnp.exp2(s_t - jnp.max(s_t, axis=0, keepdims=True))
        den = jnp.sum(p_t, axis=0, keepdims=True)
        outs.append(_dot(vt_scr[hd * V_DIM:(hd + 1) * V_DIM, :], p_t.astype(jnp.bfloat16)) / den)
    o_ref[...] = jnp.concatenate(outs, axis=0).T.astype(jnp.bfloat16)


def _attention(q, ckv_n, kpe, w_ukv, row_off, n_batch, seq, cache=None):
    nq = seq // TQ
    lk_cache = 0 if cache is None else cache[0].shape[1]
    lk = lk_cache + seq
    qblk = lambda b, qi: (row_off // TQ + b * nq + qi, 0)
    sblk = lambda b, qi: (row_off // seq + b, 0)
    in_specs = [
        pl.BlockSpec((TQ, MLA_HEADS * HEAD_SLAB), qblk),
        pl.BlockSpec((seq, KV_RANK), sblk),
        pl.BlockSpec((seq, LANES), sblk),
    ]
    args = [q, ckv_n, kpe]
    if cache is not None:
        in_specs += [pl.BlockSpec((1, lk_cache, KV_RANK), lambda b, qi: (b, 0, 0)),
                     pl.BlockSpec((1, lk_cache, LANES), lambda b, qi: (b, 0, 0))]
        args += list(cache)
    in_specs += [_const_spec(w.shape) for w in w_ukv]
    args += list(w_ukv)
    return pl.pallas_call(
        functools.partial(_attn_kernel, lk_cache=lk_cache, lk_new=seq),
        grid=(n_batch, nq),
        in_specs=in_specs,
        out_specs=pl.BlockSpec((TQ, MLA_HEADS * V_DIM), lambda b, qi: (b * nq + qi, 0)),
        out_shape=jax.ShapeDtypeStruct((n_batch * seq, MLA_HEADS * V_DIM), jnp.bfloat16),
        scratch_shapes=[pltpu.VMEM((MLA_HEADS, lk, HEAD_SLAB), jnp.bfloat16),
                        pltpu.VMEM((MLA_HEADS * V_DIM, lk), jnp.bfloat16)],
        compiler_params=pltpu.CompilerParams(dimension_semantics=("arbitrary", "arbitrary"),
                                             vmem_limit_bytes=VMEM_LIMIT),
        name=f"attention_{seq}",
    )(*args)


def _split3(a):
    a_hi = a.astype(jnp.bfloat16)
    r1 = a - a_hi.astype(jnp.float32)
    a_mid = r1.astype(jnp.bfloat16)
    a_lo = (r1 - a_mid.astype(jnp.float32)).astype(jnp.bfloat16)
    return a_hi, a_mid, a_lo


def _ssd_kernel(xbc_ref, dt_ref, h0f_ref, h0b_ref, cw_ref, cb_ref, dtbc_ref, alogc_ref, dsk_ref,
                y_ref, hf_ref, hb_ref, xs_scr, c_scr, bt_scr, dts_scr, cumt_scr, wt_scr, cum_scr,
                stf_scr, stb_scr, *, seq):
    nc = seq // CHUNK
    gs = SSD_GROUPS * SSD_STATE
    nh2 = 2 * SSD_HEADS

    row = lax.broadcasted_iota(jnp.int32, (CHUNK, CHUNK), 0)
    col = lax.broadcasted_iota(jnp.int32, (CHUNK, CHUNK), 1)
    low_half = col < SSD_HEAD_DIM
    lower = row >= col
    upper = row <= col
    lower_b = lower.astype(jnp.bfloat16)
    upper_b = upper.astype(jnp.bfloat16)
    fwd_rows = lax.broadcasted_iota(jnp.int32, (nh2, CHUNK), 0) < SSD_HEADS
    fwd_cols = lax.broadcasted_iota(jnp.int32, (CHUNK, nh2), 1) < SSD_HEADS
    a_neg_c = -jnp.exp(alogc_ref[:nh2, :])
    dtb_c = dtbc_ref[:nh2, :]

    def head_terms(c):
        r0 = pl.multiple_of(c * CHUNK, CHUNK)
        dts_t = _softplus(dt_ref[pl.ds(r0, CHUNK), :].T[:nh2, :] + dtb_c)
        pieces = _split3(dts_t * a_neg_c)
        cum_t = jnp.where(fwd_rows, sum(_dot(p, upper_b) for p in pieces),
                          sum(_dot(p, lower_b) for p in pieces))
        cum = jnp.where(fwd_cols, sum(_dot_nt(lower_b, p) for p in pieces),
                        sum(_dot_nt(upper_b, p) for p in pieces))
        tot = jnp.where(fwd_rows[:, :1], cum_t[:, CHUNK - 1:], cum_t[:, :1])
        hrows = pl.ds(pl.multiple_of(c * nh2, nh2), nh2)
        dts_scr[hrows, :] = dts_t
        cumt_scr[hrows, :] = cum_t
        wt_scr[hrows, :] = dts_t * jnp.exp(tot - cum_t)
        cum_scr[pl.ds(r0, CHUNK), :] = cum

    def prep_chunk(c, carry):
        r0 = pl.multiple_of(c * CHUNK, CHUNK)
        rows = pl.ds(r0, CHUNK)
        rows_prev = pl.ds(pl.multiple_of(jnp.maximum(r0 - HALO, 0), HALO), HALO)
        rows_next = pl.ds(pl.multiple_of(jnp.minimum(r0 + CHUNK, seq - HALO), HALO), HALO)

        def conv_tile(cs):
            prev = jnp.where(c > 0, xbc_ref[rows_prev, cs], 0.0)
            nxt = jnp.where(c < nc - 1, xbc_ref[rows_next, cs], 0.0)
            win = jnp.concatenate([prev, xbc_ref[rows, cs], nxt], axis=0)
            acc = jnp.broadcast_to(cb_ref[:, cs], (CHUNK, LANES))
            for k in range(CONV_W):
                lo = HALO - CONV_W // 2 + k
                acc = acc + cw_ref[k:k + 1, cs] * win[lo:lo + CHUNK, :]
            return _silu(acc)

        def x_tile(j, carry):
            cs = pl.ds(pl.multiple_of(j * LANES, LANES), LANES)
            u = conv_tile(cs)
            y_ref[rows, cs] = dsk_ref[:, cs] * u
            xs_scr[rows, cs] = u.astype(jnp.bfloat16)
            return carry

        lax.fori_loop(0, D_SSD // LANES, x_tile, 0)
        for g in range(SSD_GROUPS):
            b0 = pl.multiple_of(c * gs + g * SSD_STATE, SSD_STATE)
            bt_scr[pl.ds(b0, SSD_STATE), :] = conv_tile(slice(D_SSD + g * SSD_STATE, D_SSD + (g + 1) * SSD_STATE)).T
            c_scr[rows, g * SSD_STATE:(g + 1) * SSD_STATE] = conv_tile(
                slice(D_SSD + gs + g * SSD_STATE, D_SSD + gs + (g + 1) * SSD_STATE)).astype(jnp.bfloat16)
        head_terms(c)
        return carry

    lax.fori_loop(0, nc, prep_chunk, 0)

    stf_scr[...] = h0f_ref[0].T
    stb_scr[...] = h0b_ref[0].T

    def scan_open(ci, st_scr):
        rows = pl.ds(pl.multiple_of(ci * CHUNK, CHUNK), CHUNK)
        c_b = c_scr[rows, :]
        st = st_scr[...]
        bts, cbms, zs = [], [], []
        for g in range(SSD_GROUPS):
            cg = c_b[:, g * SSD_STATE:(g + 1) * SSD_STATE]
            bt = bt_scr[pl.ds(pl.multiple_of(ci * gs + g * SSD_STATE, SSD_STATE), SSD_STATE), :]
            gcols = slice(g * SSD_HPG * SSD_HEAD_DIM, (g + 1) * SSD_HPG * SSD_HEAD_DIM)
            bts.append(bt)
            cbms.append(_dot(cg, bt.astype(jnp.bfloat16)))
            zs.append(_dot(cg, st[:, gcols].astype(jnp.bfloat16)))
        return st, bts, cbms, zs

    def scan_pairs(ci, st_scr, reverse, opened):
        st, bts, cbms, zs = opened
        lane0 = SSD_HEADS if reverse else 0
        causal = upper if reverse else lower
        last = 0 if reverse else CHUNK - 1
        rows = pl.ds(pl.multiple_of(ci * CHUNK, CHUNK), CHUNK)
        hrows = pl.ds(pl.multiple_of(ci * nh2, nh2), nh2)
        xs_b = xs_scr[rows, :]
        dts_t = dts_scr[hrows, :]
        cum_t = cumt_scr[hrows, :]
        w_t = wt_scr[hrows, :]
        cum = cum_scr[rows, :]
        for pair in range(SSD_HEADS // 2):
            g, jj = divmod(pair, SSD_HPG // 2)
            pcols = slice(pair * LANES, (pair + 1) * LANES)
            lhs_y, lhs_s, entry = [], [], []
            for hd in (2 * pair, 2 * pair + 1):
                ln = lane0 + hd
                cum_i = jnp.broadcast_to(cum[:, ln:ln + 1], (CHUNK, CHUNK))
                dec = jnp.exp(jnp.where(causal, cum_i - cum_t[ln:ln + 1, :], -jnp.inf))
                lhs_y.append((cbms[g] * dec * dts_t[ln:ln + 1, :]).astype(jnp.bfloat16))
                lhs_s.append((bts[g] * w_t[ln:ln + 1, :]).astype(jnp.bfloat16))
                entry.append(jnp.exp(cum_i))
            out = _dot(jnp.concatenate(lhs_y + lhs_s, axis=0), xs_b[:, pcols])
            ea = jnp.where(low_half, entry[0], entry[1])
            y_ref[rows, pcols] += (jnp.where(low_half, out[:CHUNK], out[CHUNK:2 * CHUNK])
                                   + zs[g][:, jj * LANES:(jj + 1) * LANES] * ea)
            st_scr[:, pcols] = (ea[last:last + 1, :] * st[:, pcols]
                                + jnp.where(low_half, out[2 * CHUNK:3 * CHUNK], out[3 * CHUNK:]))

    def both(c, carry):
        opened_f = scan_open(c, stf_scr)
        opened_b = scan_open(nc - 1 - c, stb_scr)
        scan_pairs(c, stf_scr, False, opened_f)
        scan_pairs(nc - 1 - c, stb_scr, True, opened_b)
        return carry

    lax.fori_loop(0, nc, both, 0)
    hf_ref[0] = stf_scr[...].T
    hb_ref[0] = stb_scr[...].T


def _ssd(xbc, dt, h0f, h0b, conv_w, conv_b, dt_bias, a_log, d_skip, row_off, n_batch, seq):
    hp = SSD_HEADS * SSD_HEAD_DIM
    gs = SSD_GROUPS * SSD_STATE
    nc = seq // CHUNK
    sblk = lambda w: pl.BlockSpec((seq, w), lambda b: (row_off // seq + b, 0))
    st_spec = pl.BlockSpec((1, hp, SSD_STATE), lambda b: (b, 0, 0))
    st_shape = jax.ShapeDtypeStruct((n_batch, hp, SSD_STATE), jnp.float32)
    return pl.pallas_call(
        functools.partial(_ssd_kernel, seq=seq),
        grid=(n_batch,),
        in_specs=[sblk(CONV_CH), sblk(LANES), st_spec, st_spec,
                  _const_spec((SUBLANES, CONV_CH)), _const_spec((1, CONV_CH)),
                  _const_spec((LANES, 1)), _const_spec((LANES, 1)), _const_spec((1, D_SSD))],
        out_specs=[pl.BlockSpec((seq, D_SSD), lambda b: (b, 0)), st_spec, st_spec],
        out_shape=[jax.ShapeDtypeStruct((n_batch * seq, D_SSD), jnp.float32), st_shape, st_shape],
        scratch_shapes=[pltpu.VMEM((seq, D_SSD), jnp.bfloat16),
                        pltpu.VMEM((seq, gs), jnp.bfloat16),
                        pltpu.VMEM((nc * gs, CHUNK), jnp.float32),
                        pltpu.VMEM((nc * 2 * SSD_HEADS, CHUNK), jnp.float32),
                        pltpu.VMEM((nc * 2 * SSD_HEADS, CHUNK), jnp.float32),
                        pltpu.VMEM((nc * 2 * SSD_HEADS, CHUNK), jnp.float32),
                        pltpu.VMEM((seq, 2 * SSD_HEADS), jnp.float32),
                        pltpu.VMEM((SSD_STATE, hp), jnp.float32),
                        pltpu.VMEM((SSD_STATE, hp), jnp.float32)],
        compiler_params=pltpu.CompilerParams(dimension_semantics=("arbitrary",), vmem_limit_bytes=VMEM_LIMIT),
        name=f"ssd_{seq}",
    )(xbc, dt, h0f, h0b, conv_w, conv_b, dt_bias.T, a_log.T, d_skip)


def _post_mix(x, mix, mod_ref, npost_ref, npre_ref):
    d = D_MODEL
    gate_mix = mod_ref[0, :, 2 * d:3 * d]
    shf = mod_ref[0, :, 3 * d:4 * d]
    scf = mod_ref[0, :, 4 * d:5 * d]
    x1 = x + gate_mix * _rms(mix, npost_ref[...])
    return x1, (_rms(x1, npre_ref[...]) * (1.0 + scf) + shf).astype(jnp.bfloat16)


def _subtile_pipeline(n_sub, mixer_pre, mixer_dots, mod_ref, nffn_ref, wg_ref, wu_ref, wd_ref, interleave):
    gate_ffn = mod_ref[0, :, 5 * D_MODEL:6 * D_MODEL]
    ffn_up = lambda h: (_silu(_dot(h, wg_ref[0])) * _dot(h, wu_ref[0])).astype(jnp.bfloat16)
    ffn_down = lambda x1, hid: x1 + gate_ffn * _rms(_dot(hid, wd_ref[0]), nffn_ref[...])
    if not interleave:
        staged = [mixer_dots(r, mixer_pre(r)) for r in range(n_sub)]
        return [ffn_down(x1, ffn_up(h)) for x1, h in staged]
    outs = []
    x1, h = mixer_dots(0, mixer_pre(0))
    for r in range(n_sub):
        nxt_pre = mixer_pre(r + 1) if r + 1 < n_sub else None
        hid = ffn_up(h)
        nxt = mixer_dots(r + 1, nxt_pre) if r + 1 < n_sub else None
        outs.append(ffn_down(x1, hid))
        if nxt is not None:
            x1, h = nxt
    return outs


def _ffn_specs(l):
    return [_const_spec((1, D_MODEL)), _const_spec((1, D_MODEL)), _const_spec((1, D_MODEL)),
            _layer_spec((D_MODEL, D_FF), l), _layer_spec((D_MODEL, D_FF), l), _layer_spec((D_FF, D_MODEL), l)]


def _outproj_ffn_kernel(xp_ref, xs_ref, attp_ref, atts_ref, yp_ref, ys_ref, z_ref, mod_ref, sn_ref, wo_ref,
                        npost_ref, npre_ref, nffn_ref, wg_ref, wu_ref, wd_ref, o_ref):
    i = pl.program_id(0)
    gw = D_SSD // SSD_GROUPS
    sub = lambda r: slice(r * SUB_ROWS, (r + 1) * SUB_ROWS)

    def mixer_pre(r):
        yg = _pick_group(i, yp_ref, ys_ref, sub(r)) * _silu(z_ref[sub(r), :])
        parts = [_pick_group(i, attp_ref, atts_ref, sub(r))]
        for g in range(SSD_GROUPS):
            parts.append(_rms(yg[:, g * gw:(g + 1) * gw], sn_ref[:, g * gw:(g + 1) * gw]).astype(jnp.bfloat16))
        return jnp.concatenate(parts, axis=1)

    def mixer_dots(r, cat):
        return _post_mix(_pick_group(i, xp_ref, xs_ref, sub(r)), _dot(cat, wo_ref[...]), mod_ref, npost_ref, npre_ref)

    outs = _subtile_pipeline(TM // SUB_ROWS, mixer_pre, mixer_dots, mod_ref, nffn_ref, wg_ref, wu_ref, wd_ref,
                             interleave=False)
    for r, res in enumerate(outs):
        o_ref[sub(r), :] = res


def _outproj_ffn(l, xp, xs, att_p, att_s, y_p, y_s, z, mod_l, ssd_norm, w_out, npost, npre, nffn, wg, wu, wd):
    row = lambda w: pl.BlockSpec((TM, w), lambda i: (i, 0))
    return pl.pallas_call(
        _outproj_ffn_kernel,
        grid=(N_TOK // TM,),
        in_specs=_group_specs(D_MODEL) + _group_specs(MLA_HEADS * V_DIM) + _group_specs(D_SSD) + [
            row(D_SSD),
            pl.BlockSpec((1, 1, 6 * D_MODEL), lambda i: (_mod_row(i), 0, 0)),
            _const_spec((1, D_SSD)), _const_spec((MLA_HEADS * V_DIM + D_SSD, D_MODEL))] + _ffn_specs(l),
        out_specs=row(D_MODEL),
        out_shape=jax.ShapeDtypeStruct((N_TOK, D_MODEL), jnp.float32),
        compiler_params=pltpu.CompilerParams(dimension_semantics=("arbitrary",), vmem_limit_bytes=VMEM_LIMIT),
        name="outproj_ffn",
    )(xp, xs, att_p, att_s, y_p, y_s, z, mod_l, ssd_norm, w_out, npost, npre, nffn, wg, wu, wd)


def _pool_ffn_kernel(x_ref, xp_ref, xn_ref, mod_ref, nmix_ref, pw_ref, ps_ref,
                     npost_ref, npre_ref, nffn_ref, wg_ref, wu_ref, wd_ref, op_ref, os_ref):
    i = pl.program_id(0)
    seq = jnp.where(i < N_PROMPT // TM, SEQ, DEC_SEQ)
    pos0 = (i * TM) % seq
    sh = mod_ref[0, :, 0:D_MODEL]
    sc = mod_ref[0, :, D_MODEL:2 * D_MODEL]
    hmod = lambda v: _rms(v, nmix_ref[...]) * (1.0 + sc) + sh
    n_rows = POOL_SUB + 2 * HALO

    def shifted(v, s):
        return pltpu.roll(v, n_rows - s, 0)

    def mixer_pre(s):
        lo, hi = s * POOL_SUB, (s + 1) * POOL_SUB
        pos_s = (pos0 + lo) % seq
        h = hmod(x_ref[lo:hi, :])
        before = hmod(xp_ref[...] if s == 0 else x_ref[lo - HALO:lo, :])
        after = hmod(xn_ref[...] if hi == TM else x_ref[hi:hi + HALO, :])
        before = jnp.where(pos_s > 0, before, 0.0)
        after = jnp.where(pos_s + POOL_SUB < seq, after, 0.0)
        padded = jnp.concatenate([before, h, after], axis=0)
        pos = pos_s + lax.broadcasted_iota(jnp.int32, (POOL_SUB, 1), 0)
        pooled = []
        for gi, w in enumerate(POOL_WINDOWS):
            cols = slice(gi * POOL_GC, (gi + 1) * POOL_GC)
            t = padded[:, cols]
            span = 1
            while span < w:
                t = t + shifted(t, span)
                span *= 2
            lead = HALO - w // 2
            win_sum = (shifted(t, lead) if lead else t)[:POOL_SUB, :]
            cnt = (jnp.minimum(pos + w // 2, seq) - jnp.maximum(pos - w // 2, 0)).astype(jnp.float32)
            pooled.append((win_sum / cnt - h[:, cols]).astype(jnp.bfloat16))
        return pooled

    def mixer_dots(s, pooled):
        mix = jnp.concatenate([_dot(p, pw_ref[gi]) for gi, p in enumerate(pooled)], axis=1) * ps_ref[...]
        return _post_mix(x_ref[s * POOL_SUB:(s + 1) * POOL_SUB, :], mix, mod_ref, npost_ref, npre_ref)

    res = jnp.concatenate(_subtile_pipeline(TM // POOL_SUB, mixer_pre, mixer_dots, mod_ref, nffn_ref,
                                            wg_ref, wu_ref, wd_ref, interleave=True), axis=0)

    @pl.when(i == 0)
    def _():
        os_ref[...] = jnp.zeros_like(os_ref)

    @pl.when(i < N_PT)
    def _():
        op_ref[...] = res

    @pl.when(i >= N_PT)
    def _():
        os_ref[...] = res


def _pool_ffn(l, xa, mod_l, nmix, pool_w, pool_scale, npost, npre, nffn, wg, wu, wd):
    hb = TM // HALO
    nh = N_TOK // HALO
    return pl.pallas_call(
        _pool_ffn_kernel,
        grid=(N_TOK // TM,),
        in_specs=[pl.BlockSpec((TM, D_MODEL), lambda i: (i, 0)),
                  pl.BlockSpec((HALO, D_MODEL), lambda i: (jnp.maximum(i * hb - 1, 0), 0)),
                  pl.BlockSpec((HALO, D_MODEL), lambda i: (jnp.minimum((i + 1) * hb, nh - 1), 0)),
                  pl.BlockSpec((1, 1, 6 * D_MODEL), lambda i: (_mod_row(i), 0, 0)),
                  _const_spec((1, D_MODEL)),
                  _const_spec((len(POOL_WINDOWS), POOL_GC, POOL_GC)),
                  _const_spec((1, D_MODEL))] + _ffn_specs(l),
        out_specs=_group_specs(D_MODEL),
        out_shape=[jax.ShapeDtypeStruct((N_PROMPT, D_MODEL), jnp.float32),
                   jax.ShapeDtypeStruct((N_SAMPLE, D_MODEL), jnp.float32)],
        compiler_params=pltpu.CompilerParams(dimension_semantics=("arbitrary",), vmem_limit_bytes=VMEM_LIMIT),
        name="pool_ffn",
    )(xa, xa, xa, mod_l, nmix, pool_w, pool_scale, npost, npre, nffn, wg, wu, wd)


def _rope_tables():
    rows = DEC_SEQ // GRID_W
    r = jnp.repeat(jnp.arange(rows, dtype=jnp.float32), GRID_W)
    c = jnp.tile(jnp.arange(GRID_W, dtype=jnp.float32), rows)
    half = ROPE_DIM // 2
    inv_freq = jnp.power(ROPE_THETA, -jnp.arange(0, half, 2, dtype=jnp.float32) / half)
    ang = jnp.concatenate([r[:, None] * inv_freq, c[:, None] * inv_freq], axis=-1)
    cos, sin = jnp.cos(ang), jnp.sin(ang)
    zl = jnp.zeros((DEC_SEQ, KPE_LANE0), jnp.float32)
    zr = jnp.zeros((DEC_SEQ, LANES - KPE_LANE0 - ROPE_DIM), jnp.float32)
    zh = jnp.zeros((DEC_SEQ, half), jnp.float32)
    cos_t = jnp.concatenate([zl + 1.0, cos, cos, zr + 1.0], axis=1)
    s1_t = jnp.concatenate([zl, -sin, zh, zr], axis=1)
    s2_t = jnp.concatenate([zl, zh, sin, zr], axis=1)
    return cos_t, s1_t, s2_t


def _kpe_slab(k):
    pad = [(0, 0)] * (k.ndim - 1) + [(KPE_LANE0, LANES - KPE_LANE0 - ROPE_DIM)]
    return jnp.pad(k, pad)


def _layout_in_proj(w):
    o = np.cumsum((0, Q_RANK, KV_RANK, ROPE_DIM, D_SSD, CONV_CH, SSD_HEADS, SSD_HEADS))
    cq, ckv, kpe, z, xbc, dtf, dtb = (w[:, o[k]:o[k + 1]] for k in range(7))
    dt = jnp.pad(jnp.concatenate([dtf, dtb], axis=1), ((0, 0), (0, LANES - 2 * SSD_HEADS)))
    return jnp.concatenate([cq, ckv, z, xbc, _kpe_slab(kpe), dt], axis=1).astype(jnp.bfloat16)


def _layout_uq(w):
    w = w.reshape(Q_RANK, MLA_HEADS, NOPE_DIM + ROPE_DIM)
    w = jnp.pad(w, ((0, 0), (0, 0), (0, HEAD_SLAB - NOPE_DIM - ROPE_DIM)))
    return w.reshape(Q_RANK, MLA_HEADS * HEAD_SLAB).astype(jnp.bfloat16)


def _layout_ukv(w):
    w = w.reshape(KV_RANK, MLA_HEADS, NOPE_DIM + V_DIM)
    kn = jnp.pad(w[:, :, :NOPE_DIM], ((0, 0), (0, 0), (0, HEAD_SLAB - NOPE_DIM)))
    v = w[:, :, NOPE_DIM:]
    return kn.reshape(KV_RANK, -1).astype(jnp.bfloat16), v.reshape(KV_RANK, -1).T.astype(jnp.bfloat16)


def _lane_row(fwd, bwd):
    return jnp.pad(jnp.concatenate([fwd, bwd]), (0, LANES - 2 * SSD_HEADS)).reshape(1, LANES)


def kernel(x_prompt, x_sample, c, cache_mla_ckv, cache_mla_krope, state_ssd_fwd, state_ssd_bwd, c_ctx, w_mod, b_mod, norm_pre_mix, norm_post_mix, norm_pre_ffn, norm_post_ffn, w_in_ab, q_norm, w_uq, kv_norm, w_ukv, ssd_conv_w, ssd_conv_b, ssd_dt_bias_fwd, ssd_dt_bias_bwd, ssd_a_log_fwd, ssd_a_log_bwd, ssd_d, ssd_norm, w_out_ab, pool_w, pool_scale, ffn_w_gate, ffn_w_up, ffn_w_down):
    f32, bf16 = jnp.float32, jnp.bfloat16
    assert DEPTH == 2
    xp = x_prompt.reshape(N_PROMPT, D_MODEL)
    xs = x_sample.reshape(N_SAMPLE, D_MODEL)
    cvecs = jnp.concatenate([c_ctx[None, :], c, jnp.zeros((SUBLANES - N_MODVEC, D_MODEL), f32)], axis=0)
    mod = _modulation(cvecs.T, w_mod, b_mod).reshape(DEPTH, SUBLANES, 1, 6 * D_MODEL)
    tabs = _rope_tables()
    hp = SSD_HEADS * SSD_HEAD_DIM
    row = lambda v: v.reshape(1, -1)
    wg, wu, wd = ffn_w_gate.astype(bf16), ffn_w_up.astype(bf16), ffn_w_down.astype(bf16)
    new_ckv, new_kpe, new_hf, new_hb = [], [], [], []

    for l in range(DEPTH):
        ffn = (row(norm_post_mix[l]), row(norm_pre_ffn[l]), row(norm_post_ffn[l]), wg, wu, wd)
        if l % 2 == 0:
            i = l // 2
            q, ckv_n, kpe, z, xbc, dt = _inproj(
                xp, xs, mod[l], tabs, _layout_in_proj(w_in_ab[i]), _layout_uq(w_uq[i]),
                row(q_norm[i]), row(kv_norm[i]), row(norm_pre_mix[l]))
            w_kv = _layout_ukv(w_ukv[i])
            att_p = _attention(q, ckv_n, kpe, w_kv, 0, BATCH, SEQ)
            att_s = _attention(q, ckv_n, kpe, w_kv, N_PROMPT, DEC_BATCH, DEC_SEQ,
                               cache=(cache_mla_ckv[:, i], _kpe_slab(cache_mla_krope[:, i])))
            ssd_args = (jnp.pad(ssd_conv_w[i], ((0, SUBLANES - CONV_W), (0, 0))), row(ssd_conv_b[i]),
                        _lane_row(ssd_dt_bias_fwd[i], ssd_dt_bias_bwd[i]),
                        _lane_row(ssd_a_log_fwd[i], ssd_a_log_bwd[i]),
                        row(jnp.repeat(ssd_d[i], SSD_HEAD_DIM)))
            h_zero = jnp.zeros((BATCH, hp, SSD_STATE), f32)
            y_p, hf, hb = _ssd(xbc, dt, h_zero, h_zero, *ssd_args, 0, BATCH, SEQ)
            y_s, _, _ = _ssd(xbc, dt, state_ssd_fwd[:, i].reshape(DEC_BATCH, hp, SSD_STATE),
                             state_ssd_bwd[:, i].reshape(DEC_BATCH, hp, SSD_STATE),
                             *ssd_args, N_PROMPT, DEC_BATCH, DEC_SEQ)
            xa = _outproj_ffn(l, xp, xs, att_p, att_s, y_p, y_s, z,
                              mod[l], row(ssd_norm[i]), w_out_ab[i].astype(bf16), *ffn)
            new_ckv.append(ckv_n[:N_PROMPT].reshape(BATCH, SEQ, KV_RANK))
            new_kpe.append(kpe[:N_PROMPT, KPE_LANE0:KPE_LANE0 + ROPE_DIM].reshape(BATCH, SEQ, ROPE_DIM))
            new_hf.append(hf.reshape(BATCH, SSD_HEADS, SSD_HEAD_DIM, SSD_STATE))
            new_hb.append(hb.reshape(BATCH, SSD_HEADS, SSD_HEAD_DIM, SSD_STATE))
        else:
            j = l // 2
            yp, ys = _pool_ffn(l, xa, mod[l], row(norm_pre_mix[l]), pool_w[j].astype(bf16), row(pool_scale[j]), *ffn)

    return (yp.reshape(BATCH, SEQ, D_MODEL), ys.reshape(DEC_BATCH, DEC_SEQ, D_MODEL),
            jnp.stack(new_ckv, axis=1), jnp.stack(new_kpe, axis=1),
            jnp.stack(new_hf, axis=1), jnp.stack(new_hb, axis=1))
```

```python
import functools

import numpy as np
import jax
import jax.numpy as jnp
from jax import lax
from jax.experimental import pallas as pl
from jax.experimental.pallas import tpu as pltpu

D_MODEL = 1024
BATCH = 16
SEQ = 256
DEPTH = 2
DEC_BATCH = 2
DEC_SEQ = 2048
PAST_LEN = 256
GRID_W = 64
EPS = 1e-6
MLA_HEADS = 8
Q_RANK = 256
KV_RANK = 256
NOPE_DIM = 64
ROPE_DIM = 32
V_DIM = 64
ROPE_THETA = 10000.0
SSD_HEADS = 8
SSD_GROUPS = 2
SSD_HPG = SSD_HEADS // SSD_GROUPS
SSD_HEAD_DIM = 64
SSD_STATE = 128
D_SSD = SSD_HEADS * SSD_HEAD_DIM
CONV_W = 5
CONV_CH = D_SSD + 2 * SSD_GROUPS * SSD_STATE
POOL_WINDOWS = (2, 4, 8, 16)
POOL_GC = D_MODEL // len(POOL_WINDOWS)
D_FF = ((8 * D_MODEL + 3 * 256 - 1) // (3 * 256)) * 256

SUBLANES = 8
LANES = 128

N_PROMPT = BATCH * SEQ
N_SAMPLE = DEC_BATCH * DEC_SEQ
N_TOK = N_PROMPT + N_SAMPLE
N_MODVEC = 1 + DEC_BATCH
TM = 512
TQ = 256
CHUNK = 128
HALO = SUBLANES
HEAD_SLAB = LANES
IN_COLS = Q_RANK + KV_RANK + D_SSD + CONV_CH + 2 * LANES
KPE_LANE0 = NOPE_DIM
VMEM_LIMIT = 56 * 1024 * 1024

POOL_SUB = min(SEQ, DEC_SEQ)

assert TM % POOL_SUB == 0 and SEQ % POOL_SUB == 0 and DEC_SEQ % TM == 0 and N_PROMPT % DEC_SEQ == 0


def _rms(x, g):
    return x * lax.rsqrt(jnp.mean(x * x, axis=-1, keepdims=True) + EPS) * g


def _silu(x):
    return x * jax.nn.sigmoid(x)


def _softplus(x):
    return jnp.maximum(x, 0.0) + jnp.log1p(jnp.exp(-jnp.abs(x)))


def _dot(a, b):
    return jnp.dot(a, b, preferred_element_type=jnp.float32)


def _dot_nt(a, b):
    return lax.dot_general(a, b, (((1,), (1,)), ((), ())), preferred_element_type=jnp.float32)


def _mod_row(i):
    return jnp.where(i < N_PROMPT // TM, 0, 1 + (i - N_PROMPT // TM) // (DEC_SEQ // TM))


def _const_spec(shape):
    nd = len(shape)
    return pl.BlockSpec(shape, lambda *_: (0,) * nd, pipeline_mode=pl.Buffered(1))


N_PT = N_PROMPT // TM
N_WCHUNK = 16


def _tile(step, lead):
    return jnp.maximum(step - lead, 0)


def _row_spec(width, lead=0):
    return pl.BlockSpec((TM, width), lambda s: (_tile(s, lead), 0))


def _mod_spec(lead=0):
    return pl.BlockSpec((1, 1, 6 * D_MODEL), lambda s: (_mod_row(_tile(s, lead)), 0, 0))


def _group_specs(width, lead=0):
    return [pl.BlockSpec((TM, width), lambda s: (jnp.minimum(_tile(s, lead), N_PT - 1), 0)),
            pl.BlockSpec((TM, width), lambda s: (jnp.maximum(_tile(s, lead) - N_PT, 0), 0))]


def _wchunk_spec(rows, cols, layer=None):
    ck = rows // N_WCHUNK
    if layer is None:
        return pl.BlockSpec((ck, cols), lambda s: (jnp.minimum(s, N_WCHUNK - 1), 0))
    return pl.BlockSpec((1, ck, cols), lambda s: (layer, jnp.minimum(s, N_WCHUNK - 1), 0))


def _stage_weight(step, chunk, dst_ref):
    ck = chunk.shape[0]
    dst_ref[pl.ds(pl.multiple_of(step * ck, ck), ck), :] = chunk.astype(jnp.bfloat16)


def _pick_group(i, p_ref, s_ref, rows=slice(None)):
    return jnp.where(i < N_PT, p_ref[rows, :], s_ref[rows, :])


MOD_TN = 2048
SUB_ROWS = 256


def _mod_kernel(ct_ref, w_ref, b_ref, o_ref):
    s = _silu(ct_ref[...])
    w = w_ref[0]
    b = b_ref[0]
    rows = [jnp.sum(s[:, v:v + 1] * w, axis=0, keepdims=True) + b for v in range(N_MODVEC)]
    rows.append(jnp.zeros((SUBLANES - N_MODVEC, w.shape[1]), jnp.float32))
    o_ref[0] = jnp.concatenate(rows, axis=0)


def _modulation(cvecs_t, w_mod, b_mod):
    nt = 6 * D_MODEL // MOD_TN
    return pl.pallas_call(
        _mod_kernel,
        grid=(DEPTH, nt),
        in_specs=[
            pl.BlockSpec((D_MODEL, SUBLANES), lambda l, j: (0, 0)),
            pl.BlockSpec((1, D_MODEL, MOD_TN), lambda l, j: (l, 0, j)),
            pl.BlockSpec((1, 1, MOD_TN), lambda l, j: (l, 0, j)),
        ],
        out_specs=pl.BlockSpec((1, SUBLANES, MOD_TN), lambda l, j: (l, 0, j)),
        out_shape=jax.ShapeDtypeStruct((DEPTH, SUBLANES, 6 * D_MODEL), jnp.float32),
        compiler_params=pltpu.CompilerParams(dimension_semantics=("arbitrary", "arbitrary"),
                                             vmem_limit_bytes=VMEM_LIMIT),
        name="modulation",
    )(cvecs_t, w_mod, b_mod.reshape(DEPTH, 1, 6 * D_MODEL))


def _inproj_kernel(xp_ref, xs_ref, mod_ref, cos_ref, s1_ref, s2_ref, w_in_ref, w_uq_ref, qn_ref, kvn_ref, npm_ref,
                   q_ref, ckv_ref, kpe_ref, z_ref, xbc_ref, dt_ref):
    i = pl.program_id(0)
    is_latent = i >= N_PT
    sh = mod_ref[0, :, 0:D_MODEL]
    sc = mod_ref[0, :, D_MODEL:2 * D_MODEL]
    scale = (NOPE_DIM + ROPE_DIM) ** -0.5 * np.log2(np.e)
    for r in range(TM // SUB_ROWS):
        rs = slice(r * SUB_ROWS, (r + 1) * SUB_ROWS)
        h = (_rms(_pick_group(i, xp_ref, xs_ref, rs), npm_ref[...]) * (1.0 + sc) + sh).astype(jnp.bfloat16)
        p = _dot(h, w_in_ref[...])
        o = 0
        cq = p[:, o:o + Q_RANK]; o += Q_RANK
        ckv = p[:, o:o + KV_RANK]; o += KV_RANK
        z_ref[rs, :] = p[:, o:o + D_SSD]; o += D_SSD
        xbc_ref[rs, :] = p[:, o:o + CONV_CH]; o += CONV_CH
        kpe = p[:, o:o + LANES]; o += LANES
        dt_ref[rs, :] = p[:, o:o + LANES]

        cos = cos_ref[rs, :]
        s1 = s1_ref[rs, :]
        s2 = s2_ref[rs, :]

        def rope(slab):
            rot = (slab * cos + pltpu.roll(slab, LANES - ROPE_DIM // 2, 1) * s1
                   + pltpu.roll(slab, ROPE_DIM // 2, 1) * s2)
            return jnp.where(is_latent, rot, slab)

        ckv_ref[rs, :] = _rms(ckv, kvn_ref[...])
        kpe_ref[rs, :] = rope(kpe)
        q = _dot(_rms(cq, qn_ref[...]).astype(jnp.bfloat16), w_uq_ref[...]) * scale
        for hd in range(MLA_HEADS):
            sl = slice(hd * HEAD_SLAB, (hd + 1) * HEAD_SLAB)
            q_ref[rs, sl] = rope(q[:, sl]).astype(jnp.bfloat16)


def _inproj(xp, xs, mod_l, tabs, w_in, w_uq, q_norm, kv_norm, npm):
    nt = N_TOK // TM
    tab_spec = pl.BlockSpec((TM, LANES), lambda i: (jnp.maximum(i - N_PT, 0) % (DEC_SEQ // TM), 0))
    row = _row_spec
    return pl.pallas_call(
        _inproj_kernel,
        grid=(nt,),
        in_specs=_group_specs(D_MODEL) + [
            _mod_spec(),
            tab_spec, tab_spec, tab_spec,
            _const_spec((D_MODEL, IN_COLS)),
            _const_spec((Q_RANK, MLA_HEADS * HEAD_SLAB)),
            _const_spec((1, Q_RANK)),
            _const_spec((1, KV_RANK)),
            _const_spec((1, D_MODEL)),
        ],
        out_specs=[row(MLA_HEADS * HEAD_SLAB), row(KV_RANK), row(LANES), row(D_SSD), row(CONV_CH), row(LANES)],
        out_shape=[
            jax.ShapeDtypeStruct((N_TOK, MLA_HEADS * HEAD_SLAB), jnp.bfloat16),
            jax.ShapeDtypeStruct((N_TOK, KV_RANK), jnp.float32),
            jax.ShapeDtypeStruct((N_TOK, LANES), jnp.float32),
            jax.ShapeDtypeStruct((N_TOK, D_SSD), jnp.float32),
            jax.ShapeDtypeStruct((N_TOK, CONV_CH), jnp.float32),
            jax.ShapeDtypeStruct((N_TOK, LANES), jnp.float32),
        ],
        compiler_params=pltpu.CompilerParams(dimension_semantics=("arbitrary",), vmem_limit_bytes=VMEM_LIMIT),
        name="inproj",
    )(xp, xs, mod_l, *tabs, w_in, w_uq, q_norm, kv_norm, npm)


def _attn_kernel(*refs, lk_cache, lk_new):
    if lk_cache:
        q_ref, ckv_ref, kpe_ref, ckvc_ref, kpec_ref, wk_ref, wvt_ref, o_ref, k_scr, vt_scr = refs
    else:
        q_ref, ckv_ref, kpe_ref, wk_ref, wvt_ref, o_ref, k_scr, vt_scr = refs

    @pl.when(pl.program_id(1) == 0)
    def _expand_kv():
        def expand(ckv, kpe, r0):
            ckv_b = ckv.astype(jnp.bfloat16)
            kn = _dot(ckv_b, wk_ref[...])
            rows = slice(r0, r0 + ckv.shape[0])
            for hd in range(MLA_HEADS):
                k_scr[hd, rows, :] = (kn[:, hd * HEAD_SLAB:(hd + 1) * HEAD_SLAB] + kpe).astype(jnp.bfloat16)
            vt_scr[:, rows] = _dot_nt(wvt_ref[...], ckv_b).astype(jnp.bfloat16)

        step = 256
        for r0 in range(0, lk_cache, step):
            expand(ckvc_ref[0, r0:r0 + step, :], kpec_ref[0, r0:r0 + step, :], r0)
        for r0 in range(0, lk_new, step):
            expand(ckv_ref[r0:r0 + step, :], kpe_ref[r0:r0 + step, :], lk_cache + r0)

    scores = [_dot_nt(k_scr[hd], q_ref[:, hd * HEAD_SLAB:(hd + 1) * HEAD_SLAB]) for hd in range(MLA_HEADS)]
    outs = []
    for hd, s_t in enumerate(scores):
        p_t = jnp.exp2(s_t - jnp.max(s_t, axis=0, keepdims=True))
        den = jnp.sum(p_t, axis=0, keepdims=True)
        outs.append(_dot(vt_scr[hd * V_DIM:(hd + 1) * V_DIM, :], p_t.astype(jnp.bfloat16)) / den)
    o_ref[...] = jnp.concatenate(outs, axis=0).T.astype(jnp.bfloat16)


def _attention(q, ckv_n, kpe, w_ukv, row_off, n_batch, seq, cache=None):
    nq = seq // TQ
    lk_cache = 0 if cache is None else cache[0].shape[1]
    lk = lk_cache + seq
    qblk = lambda b, qi: (row_off // TQ + b * nq + qi, 0)
    sblk = lambda b, qi: (row_off // seq + b, 0)
    in_specs = [
        pl.BlockSpec((TQ, MLA_HEADS * HEAD_SLAB), qblk),
        pl.BlockSpec((seq, KV_RANK), sblk),
        pl.BlockSpec((seq, LANES), sblk),
    ]
    args = [q, ckv_n, kpe]
    if cache is not None:
        in_specs += [pl.BlockSpec((1, lk_cache, KV_RANK), lambda b, qi: (b, 0, 0)),
                     pl.BlockSpec((1, lk_cache, LANES), lambda b, qi: (b, 0, 0))]
        args += list(cache)
    in_specs += [_const_spec(w.shape) for w in w_ukv]
    args += list(w_ukv)
    return pl.pallas_call(
        functools.partial(_attn_kernel, lk_cache=lk_cache, lk_new=seq),
        grid=(n_batch, nq),
        in_specs=in_specs,
        out_specs=pl.BlockSpec((TQ, MLA_HEADS * V_DIM), lambda b, qi: (b * nq + qi, 0)),
        out_shape=jax.ShapeDtypeStruct((n_batch * seq, MLA_HEADS * V_DIM), jnp.bfloat16),
        scratch_shapes=[pltpu.VMEM((MLA_HEADS, lk, HEAD_SLAB), jnp.bfloat16),
                        pltpu.VMEM((MLA_HEADS * V_DIM, lk), jnp.bfloat16)],
        compiler_params=pltpu.CompilerParams(dimension_semantics=("arbitrary", "arbitrary"),
                                             vmem_limit_bytes=VMEM_LIMIT),
        name=f"attention_{seq}",
    )(*args)


def _split3(a):
    a_hi = a.astype(jnp.bfloat16)
    r1 = a - a_hi.astype(jnp.float32)
    a_mid = r1.astype(jnp.bfloat16)
    a_lo = (r1 - a_mid.astype(jnp.float32)).astype(jnp.bfloat16)
    return a_hi, a_mid, a_lo


def _ssd_kernel(xbc_ref, dt_ref, h0f_ref, h0b_ref, cw_ref, cb_ref, dtbc_ref, alogc_ref, dsk_ref,
                y_ref, hf_ref, hb_ref, xs_scr, c_scr, bt_scr, dts_scr, cumt_scr, wt_scr, cum_scr,
                stf_scr, stb_scr, *, seq):
    nc = seq // CHUNK
    gs = SSD_GROUPS * SSD_STATE
    nh2 = 2 * SSD_HEADS

    row = lax.broadcasted_iota(jnp.int32, (CHUNK, CHUNK), 0)
    col = lax.broadcasted_iota(jnp.int32, (CHUNK, CHUNK), 1)
    low_half = col < SSD_HEAD_DIM
    lower = row >= col
    upper = row <= col
    lower_b = lower.astype(jnp.bfloat16)
    upper_b = upper.astype(jnp.bfloat16)
    fwd_rows = lax.broadcasted_iota(jnp.int32, (nh2, CHUNK), 0) < SSD_HEADS
    fwd_cols = lax.broadcasted_iota(jnp.int32, (CHUNK, nh2), 1) < SSD_HEADS
    a_neg_c = -jnp.exp(alogc_ref[:nh2, :])
    dtb_c = dtbc_ref[:nh2, :]

    def head_terms(c):
        r0 = pl.multiple_of(c * CHUNK, CHUNK)
        dts_t = _softplus(dt_ref[pl.ds(r0, CHUNK), :].T[:nh2, :] + dtb_c)
        pieces = _split3(dts_t * a_neg_c)
        cum_t = jnp.where(fwd_rows, sum(_dot(p, upper_b) for p in pieces),
                          sum(_dot(p, lower_b) for p in pieces))
        cum = jnp.where(fwd_cols, sum(_dot_nt(lower_b, p) for p in pieces),
                        sum(_dot_nt(upper_b, p) for p in pieces))
        tot = jnp.where(fwd_rows[:, :1], cum_t[:, CHUNK - 1:], cum_t[:, :1])
        hrows = pl.ds(pl.multiple_of(c * nh2, nh2), nh2)
        dts_scr[hrows, :] = dts_t
        cumt_scr[hrows, :] = cum_t
        wt_scr[hrows, :] = dts_t * jnp.exp(tot - cum_t)
        cum_scr[pl.ds(r0, CHUNK), :] = cum

    def prep_chunk(c, carry):
        r0 = pl.multiple_of(c * CHUNK, CHUNK)
        rows = pl.ds(r0, CHUNK)
        rows_prev = pl.ds(pl.multiple_of(jnp.maximum(r0 - HALO, 0), HALO), HALO)
        rows_next = pl.ds(pl.multiple_of(jnp.minimum(r0 + CHUNK, seq - HALO), HALO), HALO)

        def conv_tile(cs):
            prev = jnp.where(c > 0, xbc_ref[rows_prev, cs], 0.0)
            nxt = jnp.where(c < nc - 1, xbc_ref[rows_next, cs], 0.0)
            win = jnp.concatenate([prev, xbc_ref[rows, cs], nxt], axis=0)
            acc = jnp.broadcast_to(cb_ref[:, cs], (CHUNK, LANES))
            for k in range(CONV_W):
                lo = HALO - CONV_W // 2 + k
                acc = acc + cw_ref[k:k + 1, cs] * win[lo:lo + CHUNK, :]
            return _silu(acc)

        def x_tile(j, carry):
            cs = pl.ds(pl.multiple_of(j * LANES, LANES), LANES)
            u = conv_tile(cs)
            y_ref[rows, cs] = dsk_ref[:, cs] * u
            xs_scr[rows, cs] = u.astype(jnp.bfloat16)
            return carry

        lax.fori_loop(0, D_SSD // LANES, x_tile, 0)
        for g in range(SSD_GROUPS):
            b0 = pl.multiple_of(c * gs + g * SSD_STATE, SSD_STATE)
            bt_scr[pl.ds(b0, SSD_STATE), :] = conv_tile(slice(D_SSD + g * SSD_STATE, D_SSD + (g + 1) * SSD_STATE)).T
            c_scr[rows, g * SSD_STATE:(g + 1) * SSD_STATE] = conv_tile(
                slice(D_SSD + gs + g * SSD_STATE, D_SSD + gs + (g + 1) * SSD_STATE)).astype(jnp.bfloat16)
        head_terms(c)
        return carry

    lax.fori_loop(0, nc, prep_chunk, 0)

    stf_scr[...] = h0f_ref[0].T
    stb_scr[...] = h0b_ref[0].T

    def scan_open(ci, st_scr):
        rows = pl.ds(pl.multiple_of(ci * CHUNK, CHUNK), CHUNK)
        c_b = c_scr[rows, :]
        st = st_scr[...]
        bts, cbms, zs = [], [], []
        for g in range(SSD_GROUPS):
            cg = c_b[:, g * SSD_STATE:(g + 1) * SSD_STATE]
            bt = bt_scr[pl.ds(pl.multiple_of(ci * gs + g * SSD_STATE, SSD_STATE), SSD_STATE), :]
            gcols = slice(g * SSD_HPG * SSD_HEAD_DIM, (g + 1) * SSD_HPG * SSD_HEAD_DIM)
            bts.append(bt)
            cbms.append(_dot(cg, bt.astype(jnp.bfloat16)))
            zs.append(_dot(cg, st[:, gcols].astype(jnp.bfloat16)))
        return st, bts, cbms, zs

    def scan_pairs(ci, st_scr, reverse, opened):
        st, bts, cbms, zs = opened
        lane0 = SSD_HEADS if reverse else 0
        causal = upper if reverse else lower
        last = 0 if reverse else CHUNK - 1
        rows = pl.ds(pl.multiple_of(ci * CHUNK, CHUNK), CHUNK)
        hrows = pl.ds(pl.multiple_of(ci * nh2, nh2), nh2)
        xs_b = xs_scr[rows, :]
        dts_t = dts_scr[hrows, :]
        cum_t = cumt_scr[hrows, :]
        w_t = wt_scr[hrows, :]
        cum = cum_scr[rows, :]
        for pair in range(SSD_HEADS // 2):
            g, jj = divmod(pair, SSD_HPG // 2)
            pcols = slice(pair * LANES, (pair + 1) * LANES)
            lhs_y, lhs_s, entry = [], [], []
            for hd in (2 * pair, 2 * pair + 1):
                ln = lane0 + hd
                cum_i = jnp.broadcast_to(cum[:, ln:ln + 1], (CHUNK, CHUNK))
                dec = jnp.exp(jnp.where(causal, cum_i - cum_t[ln:ln + 1, :], -jnp.inf))
                lhs_y.append((cbms[g] * dec * dts_t[ln:ln + 1, :]).astype(jnp.bfloat16))
                lhs_s.append((bts[g] * w_t[ln:ln + 1, :]).astype(jnp.bfloat16))
                entry.append(jnp.exp(cum_i))
            out = _dot(jnp.concatenate(lhs_y + lhs_s, axis=0), xs_b[:, pcols])
            ea = jnp.where(low_half, entry[0], entry[1])
            y_ref[rows, pcols] += (jnp.where(low_half, out[:CHUNK], out[CHUNK:2 * CHUNK])
                                   + zs[g][:, jj * LANES:(jj + 1) * LANES] * ea)
            st_scr[:, pcols] = (ea[last:last + 1, :] * st[:, pcols]
                                + jnp.where(low_half, out[2 * CHUNK:3 * CHUNK], out[3 * CHUNK:]))

    def both(c, carry):
        opened_f = scan_open(c, stf_scr)
        opened_b = scan_open(nc - 1 - c, stb_scr)
        scan_pairs(c, stf_scr, False, opened_f)
        scan_pairs(nc - 1 - c, stb_scr, True, opened_b)
        return carry

    lax.fori_loop(0, nc, both, 0)
    hf_ref[0] = stf_scr[...].T
    hb_ref[0] = stb_scr[...].T


def _ssd(xbc, dt, h0f, h0b, conv_w, conv_b, dt_bias, a_log, d_skip, row_off, n_batch, seq):
    hp = SSD_HEADS * SSD_HEAD_DIM
    gs = SSD_GROUPS * SSD_STATE
    nc = seq // CHUNK
    sblk = lambda w: pl.BlockSpec((seq, w), lambda b: (row_off // seq + b, 0))
    st_spec = pl.BlockSpec((1, hp, SSD_STATE), lambda b: (b, 0, 0))
    st_shape = jax.ShapeDtypeStruct((n_batch, hp, SSD_STATE), jnp.float32)
    return pl.pallas_call(
        functools.partial(_ssd_kernel, seq=seq),
        grid=(n_batch,),
        in_specs=[sblk(CONV_CH), sblk(LANES), st_spec, st_spec,
                  _const_spec((SUBLANES, CONV_CH)), _const_spec((1, CONV_CH)),
                  _const_spec((LANES, 1)), _const_spec((LANES, 1)), _const_spec((1, D_SSD))],
        out_specs=[pl.BlockSpec((seq, D_SSD), lambda b: (b, 0)), st_spec, st_spec],
        out_shape=[jax.ShapeDtypeStruct((n_batch * seq, D_SSD), jnp.float32), st_shape, st_shape],
        scratch_shapes=[pltpu.VMEM((seq, D_SSD), jnp.bfloat16),
                        pltpu.VMEM((seq, gs), jnp.bfloat16),
                        pltpu.VMEM((nc * gs, CHUNK), jnp.float32),
                        pltpu.VMEM((nc * 2 * SSD_HEADS, CHUNK), jnp.float32),
                        pltpu.VMEM((nc * 2 * SSD_HEADS, CHUNK), jnp.float32),
                        pltpu.VMEM((nc * 2 * SSD_HEADS, CHUNK), jnp.float32),
                        pltpu.VMEM((seq, 2 * SSD_HEADS), jnp.float32),
                        pltpu.VMEM((SSD_STATE, hp), jnp.float32),
                        pltpu.VMEM((SSD_STATE, hp), jnp.float32)],
        compiler_params=pltpu.CompilerParams(dimension_semantics=("arbitrary",), vmem_limit_bytes=VMEM_LIMIT),
        name=f"ssd_{seq}",
    )(xbc, dt, h0f, h0b, conv_w, conv_b, dt_bias.T, a_log.T, d_skip)


def _post_mix(x, mix, mod_ref, npost_ref, npre_ref):
    d = D_MODEL
    gate_mix = mod_ref[0, :, 2 * d:3 * d]
    shf = mod_ref[0, :, 3 * d:4 * d]
    scf = mod_ref[0, :, 4 * d:5 * d]
    x1 = x + gate_mix * _rms(mix, npost_ref[...])
    return x1, (_rms(x1, npre_ref[...]) * (1.0 + scf) + shf).astype(jnp.bfloat16)


def _subtile_pipeline(n_sub, mixer_pre, mixer_dots, mod_ref, nffn_ref, wg_ref, wu_ref, wd_ref, interleave):
    gate_ffn = mod_ref[0, :, 5 * D_MODEL:6 * D_MODEL]
    ffn_up = lambda h: (_silu(_dot(h, wg_ref[...])) * _dot(h, wu_ref[...])).astype(jnp.bfloat16)
    ffn_down = lambda x1, hid: x1 + gate_ffn * _rms(_dot(hid, wd_ref[...]), nffn_ref[...])
    if not interleave:
        staged = [mixer_dots(r, mixer_pre(r)) for r in range(n_sub)]
        return [ffn_down(x1, ffn_up(h)) for x1, h in staged]
    outs = []
    x1, h = mixer_dots(0, mixer_pre(0))
    for r in range(n_sub):
        nxt_pre = mixer_pre(r + 1) if r + 1 < n_sub else None
        hid = ffn_up(h)
        nxt = mixer_dots(r + 1, nxt_pre) if r + 1 < n_sub else None
        outs.append(ffn_down(x1, hid))
        if nxt is not None:
            x1, h = nxt
    return outs


def _ffn_specs(l):
    return [_const_spec((1, D_MODEL)), _const_spec((1, D_MODEL)), _const_spec((1, D_MODEL)),
            _wchunk_spec(D_MODEL, D_FF, l), _wchunk_spec(D_MODEL, D_FF, l), _wchunk_spec(D_FF, D_MODEL, l)]


def _ffn_scratch():
    return [pltpu.VMEM((D_MODEL, D_FF), jnp.bfloat16), pltpu.VMEM((D_MODEL, D_FF), jnp.bfloat16),
            pltpu.VMEM((D_FF, D_MODEL), jnp.bfloat16)]


def _outproj_ffn_kernel(xp_ref, xs_ref, attp_ref, atts_ref, yp_ref, ys_ref, z_ref, mod_ref, sn_ref, wo_ref,
                        npost_ref, npre_ref, nffn_ref, wg_ref, wu_ref, wd_ref, o_ref,
                        wo_scr, wg_scr, wu_scr, wd_scr):
    step = pl.program_id(0)

    @pl.when(step < N_WCHUNK)
    def _stage():
        _stage_weight(step, wo_ref[...], wo_scr)
        _stage_weight(step, wg_ref[0], wg_scr)
        _stage_weight(step, wu_ref[0], wu_scr)
        _stage_weight(step, wd_ref[0], wd_scr)

    @pl.when(step == 0)
    def _():
        o_ref[...] = jnp.zeros_like(o_ref)

    @pl.when(step >= N_WCHUNK)
    def _tile_step():
        i = step - N_WCHUNK
        gw = D_SSD // SSD_GROUPS
        sub = lambda r: slice(r * SUB_ROWS, (r + 1) * SUB_ROWS)

        def mixer_pre(r):
            yg = _pick_group(i, yp_ref, ys_ref, sub(r)) * _silu(z_ref[sub(r), :])
            parts = [_pick_group(i, attp_ref, atts_ref, sub(r))]
            for g in range(SSD_GROUPS):
                parts.append(_rms(yg[:, g * gw:(g + 1) * gw], sn_ref[:, g * gw:(g + 1) * gw]).astype(jnp.bfloat16))
            return jnp.concatenate(parts, axis=1)

        def mixer_dots(r, cat):
            return _post_mix(_pick_group(i, xp_ref, xs_ref, sub(r)), _dot(cat, wo_scr[...]),
                             mod_ref, npost_ref, npre_ref)

        outs = _subtile_pipeline(TM // SUB_ROWS, mixer_pre, mixer_dots, mod_ref, nffn_ref, wg_scr, wu_scr, wd_scr,
                                 interleave=False)
        for r, res in enumerate(outs):
            o_ref[sub(r), :] = res


def _outproj_ffn(l, xp, xs, att_p, att_s, y_p, y_s, z, mod_l, ssd_norm, w_out, npost, npre, nffn, wg, wu, wd):
    lead = N_WCHUNK
    d_cat = MLA_HEADS * V_DIM + D_SSD
    return pl.pallas_call(
        _outproj_ffn_kernel,
        grid=(lead + N_TOK // TM,),
        in_specs=(_group_specs(D_MODEL, lead) + _group_specs(MLA_HEADS * V_DIM, lead) + _group_specs(D_SSD, lead)
                  + [_row_spec(D_SSD, lead), _mod_spec(lead), _const_spec((1, D_SSD)),
                     _wchunk_spec(d_cat, D_MODEL)] + _ffn_specs(l)),
        out_specs=_row_spec(D_MODEL, lead),
        out_shape=jax.ShapeDtypeStruct((N_TOK, D_MODEL), jnp.float32),
        scratch_shapes=[pltpu.VMEM((d_cat, D_MODEL), jnp.bfloat16)] + _ffn_scratch(),
        compiler_params=pltpu.CompilerParams(dimension_semantics=("arbitrary",), vmem_limit_bytes=VMEM_LIMIT),
        name="outproj_ffn",
    )(xp, xs, att_p, att_s, y_p, y_s, z, mod_l, ssd_norm, w_out, npost, npre, nffn, wg, wu, wd)


def _pool_ffn_kernel(x_ref, xp_ref, xn_ref, mod_ref, nmix_ref, pw_ref, ps_ref,
                     npost_ref, npre_ref, nffn_ref, wg_ref, wu_ref, wd_ref, op_ref, os_ref,
                     wg_scr, wu_scr, wd_scr):
    step = pl.program_id(0)

    @pl.when(step < N_WCHUNK)
    def _stage():
        _stage_weight(step, wg_ref[0], wg_scr)
        _stage_weight(step, wu_ref[0], wu_scr)
        _stage_weight(step, wd_ref[0], wd_scr)

    @pl.when(step == 0)
    def _():
        op_ref[...] = jnp.zeros_like(op_ref)
        os_ref[...] = jnp.zeros_like(os_ref)

    @pl.when(step >= N_WCHUNK)
    def _tile_step():
        _pool_ffn_tile(step - N_WCHUNK, x_ref, xp_ref, xn_ref, mod_ref, nmix_ref, pw_ref, ps_ref,
                       npost_ref, npre_ref, nffn_ref, wg_scr, wu_scr, wd_scr, op_ref, os_ref)


def _pool_ffn_tile(i, x_ref, xp_ref, xn_ref, mod_ref, nmix_ref, pw_ref, ps_ref,
                   npost_ref, npre_ref, nffn_ref, wg_ref, wu_ref, wd_ref, op_ref, os_ref):
    seq = jnp.where(i < N_PROMPT // TM, SEQ, DEC_SEQ)
    pos0 = (i * TM) % seq
    sh = mod_ref[0, :, 0:D_MODEL]
    sc = mod_ref[0, :, D_MODEL:2 * D_MODEL]
    hmod = lambda v: _rms(v, nmix_ref[...]) * (1.0 + sc) + sh
    n_rows = POOL_SUB + 2 * HALO

    def shifted(v, s):
        return pltpu.roll(v, n_rows - s, 0)

    def mixer_pre(s):
        lo, hi = s * POOL_SUB, (s + 1) * POOL_SUB
        pos_s = (pos0 + lo) % seq
        h = hmod(x_ref[lo:hi, :])
        before = hmod(xp_ref[...] if s == 0 else x_ref[lo - HALO:lo, :])
        after = hmod(xn_ref[...] if hi == TM else x_ref[hi:hi + HALO, :])
        before = jnp.where(pos_s > 0, before, 0.0)
        after = jnp.where(pos_s + POOL_SUB < seq, after, 0.0)
        padded = jnp.concatenate([before, h, after], axis=0)
        pos = pos_s + lax.broadcasted_iota(jnp.int32, (POOL_SUB, 1), 0)
        pooled = []
        for gi, w in enumerate(POOL_WINDOWS):
            cols = slice(gi * POOL_GC, (gi + 1) * POOL_GC)
            t = padded[:, cols]
            span = 1
            while span < w:
                t = t + shifted(t, span)
                span *= 2
            lead = HALO - w // 2
            win_sum = (shifted(t, lead) if lead else t)[:POOL_SUB, :]
            cnt = (jnp.minimum(pos + w // 2, seq) - jnp.maximum(pos - w // 2, 0)).astype(jnp.float32)
            pooled.append((win_sum / cnt - h[:, cols]).astype(jnp.bfloat16))
        return pooled

    def mixer_dots(s, pooled):
        mix = jnp.concatenate([_dot(p, pw_ref[gi]) for gi, p in enumerate(pooled)], axis=1) * ps_ref[...]
        return _post_mix(x_ref[s * POOL_SUB:(s + 1) * POOL_SUB, :], mix, mod_ref, npost_ref, npre_ref)

    res = jnp.concatenate(_subtile_pipeline(TM // POOL_SUB, mixer_pre, mixer_dots, mod_ref, nffn_ref,
                                            wg_ref, wu_ref, wd_ref, interleave=True), axis=0)

    @pl.when(i < N_PT)
    def _():
        op_ref[...] = res

    @pl.when(i >= N_PT)
    def _():
        os_ref[...] = res


def _pool_ffn(l, xa, mod_l, nmix, pool_w, pool_scale, npost, npre, nffn, wg, wu, wd):
    lead = N_WCHUNK
    hb = TM // HALO
    nh = N_TOK // HALO
    return pl.pallas_call(
        _pool_ffn_kernel,
        grid=(lead + N_TOK // TM,),
        in_specs=[_row_spec(D_MODEL, lead),
                  pl.BlockSpec((HALO, D_MODEL), lambda s: (jnp.maximum(_tile(s, lead) * hb - 1, 0), 0)),
                  pl.BlockSpec((HALO, D_MODEL), lambda s: (jnp.minimum((_tile(s, lead) + 1) * hb, nh - 1), 0)),
                  _mod_spec(lead),
                  _const_spec((1, D_MODEL)),
                  _const_spec((len(POOL_WINDOWS), POOL_GC, POOL_GC)),
                  _const_spec((1, D_MODEL))] + _ffn_specs(l),
        out_specs=_group_specs(D_MODEL, lead),
        out_shape=[jax.ShapeDtypeStruct((N_PROMPT, D_MODEL), jnp.float32),
                   jax.ShapeDtypeStruct((N_SAMPLE, D_MODEL), jnp.float32)],
        scratch_shapes=_ffn_scratch(),
        compiler_params=pltpu.CompilerParams(dimension_semantics=("arbitrary",), vmem_limit_bytes=VMEM_LIMIT),
        name="pool_ffn",
    )(xa, xa, xa, mod_l, nmix, pool_w, pool_scale, npost, npre, nffn, wg, wu, wd)


def _rope_tables():
    f32 = np.float32
    rows = DEC_SEQ // GRID_W
    r = np.repeat(np.arange(rows, dtype=f32), GRID_W)
    c = np.tile(np.arange(GRID_W, dtype=f32), rows)
    half = ROPE_DIM // 2
    inv_freq = np.power(f32(ROPE_THETA), -np.arange(0, half, 2, dtype=f32) / f32(half)).astype(f32)
    ang = np.concatenate([r[:, None] * inv_freq, c[:, None] * inv_freq], axis=-1).astype(f32)
    cos, sin = np.cos(ang).astype(f32), np.sin(ang).astype(f32)
    zl = np.zeros((DEC_SEQ, KPE_LANE0), f32)
    zr = np.zeros((DEC_SEQ, LANES - KPE_LANE0 - ROPE_DIM), f32)
    zh = np.zeros((DEC_SEQ, half), f32)
    cos_t = np.concatenate([zl + 1, cos, cos, zr + 1], axis=1)
    s1_t = np.concatenate([zl, -sin, zh, zr], axis=1)
    s2_t = np.concatenate([zl, zh, sin, zr], axis=1)
    return jnp.asarray(cos_t), jnp.asarray(s1_t), jnp.asarray(s2_t)


def _kpe_slab(k):
    pad = [(0, 0)] * (k.ndim - 1) + [(KPE_LANE0, LANES - KPE_LANE0 - ROPE_DIM)]
    return jnp.pad(k, pad)


def _layout_in_proj(w):
    o = np.cumsum((0, Q_RANK, KV_RANK, ROPE_DIM, D_SSD, CONV_CH, SSD_HEADS, SSD_HEADS))
    cq, ckv, kpe, z, xbc, dtf, dtb = (w[:, o[k]:o[k + 1]] for k in range(7))
    dt = jnp.pad(jnp.concatenate([dtf, dtb], axis=1), ((0, 0), (0, LANES - 2 * SSD_HEADS)))
    return jnp.concatenate([cq, ckv, z, xbc, _kpe_slab(kpe), dt], axis=1).astype(jnp.bfloat16)


def _layout_uq(w):
    w = w.reshape(Q_RANK, MLA_HEADS, NOPE_DIM + ROPE_DIM)
    w = jnp.pad(w, ((0, 0), (0, 0), (0, HEAD_SLAB - NOPE_DIM - ROPE_DIM)))
    return w.reshape(Q_RANK, MLA_HEADS * HEAD_SLAB).astype(jnp.bfloat16)


def _layout_ukv(w):
    w = w.reshape(KV_RANK, MLA_HEADS, NOPE_DIM + V_DIM)
    kn = jnp.pad(w[:, :, :NOPE_DIM], ((0, 0), (0, 0), (0, HEAD_SLAB - NOPE_DIM)))
    v = w[:, :, NOPE_DIM:]
    return kn.reshape(KV_RANK, -1).astype(jnp.bfloat16), v.reshape(KV_RANK, -1).T.astype(jnp.bfloat16)


def _lane_row(fwd, bwd):
    return jnp.pad(jnp.concatenate([fwd, bwd]), (0, LANES - 2 * SSD_HEADS)).reshape(1, LANES)


def kernel(x_prompt, x_sample, c, cache_mla_ckv, cache_mla_krope, state_ssd_fwd, state_ssd_bwd, c_ctx, w_mod, b_mod, norm_pre_mix, norm_post_mix, norm_pre_ffn, norm_post_ffn, w_in_ab, q_norm, w_uq, kv_norm, w_ukv, ssd_conv_w, ssd_conv_b, ssd_dt_bias_fwd, ssd_dt_bias_bwd, ssd_a_log_fwd, ssd_a_log_bwd, ssd_d, ssd_norm, w_out_ab, pool_w, pool_scale, ffn_w_gate, ffn_w_up, ffn_w_down):
    f32, bf16 = jnp.float32, jnp.bfloat16
    assert DEPTH == 2
    xp = x_prompt.reshape(N_PROMPT, D_MODEL)
    xs = x_sample.reshape(N_SAMPLE, D_MODEL)
    cvecs = jnp.concatenate([c_ctx[None, :], c, jnp.zeros((SUBLANES - N_MODVEC, D_MODEL), f32)], axis=0)
    mod = _modulation(cvecs.T, w_mod, b_mod).reshape(DEPTH, SUBLANES, 1, 6 * D_MODEL)
    tabs = _rope_tables()
    hp = SSD_HEADS * SSD_HEAD_DIM
    row = lambda v: v.reshape(1, -1)
    new_ckv, new_kpe, new_hf, new_hb = [], [], [], []

    for l in range(DEPTH):
        ffn = (row(norm_post_mix[l]), row(norm_pre_ffn[l]), row(norm_post_ffn[l]), ffn_w_gate, ffn_w_up, ffn_w_down)
        if l % 2 == 0:
            i = l // 2
            q, ckv_n, kpe, z, xbc, dt = _inproj(
                xp, xs, mod[l], tabs, _layout_in_proj(w_in_ab[i]), _layout_uq(w_uq[i]),
                row(q_norm[i]), row(kv_norm[i]), row(norm_pre_mix[l]))
            w_kv = _layout_ukv(w_ukv[i])
            att_p = _attention(q, ckv_n, kpe, w_kv, 0, BATCH, SEQ)
            att_s = _attention(q, ckv_n, kpe, w_kv, N_PROMPT, DEC_BATCH, DEC_SEQ,
                               cache=(cache_mla_ckv[:, i], _kpe_slab(cache_mla_krope[:, i])))
            ssd_args = (jnp.pad(ssd_conv_w[i], ((0, SUBLANES - CONV_W), (0, 0))), row(ssd_conv_b[i]),
                        _lane_row(ssd_dt_bias_fwd[i], ssd_dt_bias_bwd[i]),
                        _lane_row(ssd_a_log_fwd[i], ssd_a_log_bwd[i]),
                        row(jnp.repeat(ssd_d[i], SSD_HEAD_DIM)))
            h_zero = jnp.zeros((BATCH, hp, SSD_STATE), f32)
            y_p, hf, hb = _ssd(xbc, dt, h_zero, h_zero, *ssd_args, 0, BATCH, SEQ)
            y_s, _, _ = _ssd(xbc, dt, state_ssd_fwd[:, i].reshape(DEC_BATCH, hp, SSD_STATE),
                             state_ssd_bwd[:, i].reshape(DEC_BATCH, hp, SSD_STATE),
                             *ssd_args, N_PROMPT, DEC_BATCH, DEC_SEQ)
            xa = _outproj_ffn(l, xp, xs, att_p, att_s, y_p, y_s, z,
                              mod[l], row(ssd_norm[i]), w_out_ab[i], *ffn)
            new_ckv.append(ckv_n[:N_PROMPT].reshape(BATCH, SEQ, KV_RANK))
            new_kpe.append(kpe[:N_PROMPT, KPE_LANE0:KPE_LANE0 + ROPE_DIM].reshape(BATCH, SEQ, ROPE_DIM))
            new_hf.append(hf.reshape(BATCH, SSD_HEADS, SSD_HEAD_DIM, SSD_STATE))
            new_hb.append(hb.reshape(BATCH, SSD_HEADS, SSD_HEAD_DIM, SSD_STATE))
        else:
            j = l // 2
            yp, ys = _pool_ffn(l, xa, mod[l], row(norm_pre_mix[l]), pool_w[j].astype(bf16), row(pool_scale[j]), *ffn)

    return (yp.reshape(BATCH, SEQ, D_MODEL), ys.reshape(DEC_BATCH, DEC_SEQ, D_MODEL),
            jnp.stack(new_ckv, axis=1), jnp.stack(new_kpe, axis=1),
            jnp.stack(new_hf, axis=1), jnp.stack(new_hb, axis=1))
```

```python
import functools

import numpy as np
import jax
import jax.numpy as jnp
from jax import lax
from jax.experimental import pallas as pl
from jax.experimental.pallas import tpu as pltpu

D_MODEL = 1024
BATCH = 16
SEQ = 256
DEPTH = 2
DEC_BATCH = 2
DEC_SEQ = 2048
PAST_LEN = 256
GRID_W = 64
EPS = 1e-6
MLA_HEADS = 8
Q_RANK = 256
KV_RANK = 256
NOPE_DIM = 64
ROPE_DIM = 32
V_DIM = 64
ROPE_THETA = 10000.0
SSD_HEADS = 8
SSD_GROUPS = 2
SSD_HPG = SSD_HEADS // SSD_GROUPS
SSD_HEAD_DIM = 64
SSD_STATE = 128
D_SSD = SSD_HEADS * SSD_HEAD_DIM
CONV_W = 5
CONV_CH = D_SSD + 2 * SSD_GROUPS * SSD_STATE
POOL_WINDOWS = (2, 4, 8, 16)
POOL_GC = D_MODEL // len(POOL_WINDOWS)
D_FF = ((8 * D_MODEL + 3 * 256 - 1) // (3 * 256)) * 256

SUBLANES = 8
LANES = 128

N_PROMPT = BATCH * SEQ
N_SAMPLE = DEC_BATCH * DEC_SEQ
N_TOK = N_PROMPT + N_SAMPLE
N_MODVEC = 1 + DEC_BATCH
TM = 512
TQ = 256
CHUNK = 128
HALO = SUBLANES
HEAD_SLAB = LANES
IN_COLS = Q_RANK + KV_RANK + D_SSD + CONV_CH + 2 * LANES
KPE_LANE0 = NOPE_DIM
VMEM_LIMIT = 56 * 1024 * 1024

POOL_SUB = min(SEQ, DEC_SEQ)

assert TM % POOL_SUB == 0 and SEQ % POOL_SUB == 0 and DEC_SEQ % TM == 0 and N_PROMPT % DEC_SEQ == 0


def _rms(x, g):
    return x * lax.rsqrt(jnp.mean(x * x, axis=-1, keepdims=True) + EPS) * g


def _silu(x):
    return x * jax.nn.sigmoid(x)


def _softplus(x):
    return jnp.maximum(x, 0.0) + jnp.log1p(jnp.exp(-jnp.abs(x)))


def _dot(a, b):
    return jnp.dot(a, b, preferred_element_type=jnp.float32)


def _dot_nt(a, b):
    return lax.dot_general(a, b, (((1,), (1,)), ((), ())), preferred_element_type=jnp.float32)


def _mod_row(i):
    return jnp.where(i < N_PROMPT // TM, 0, 1 + (i - N_PROMPT // TM) // (DEC_SEQ // TM))


def _const_spec(shape):
    nd = len(shape)
    return pl.BlockSpec(shape, lambda *_: (0,) * nd, pipeline_mode=pl.Buffered(1))


N_PT = N_PROMPT // TM
N_WCHUNK = 16


def _tile(step, lead):
    return jnp.maximum(step - lead, 0)


def _row_spec(width, lead=0):
    return pl.BlockSpec((TM, width), lambda s: (_tile(s, lead), 0))


def _mod_spec(lead=0):
    return pl.BlockSpec((1, 1, 6 * D_MODEL), lambda s: (_mod_row(_tile(s, lead)), 0, 0))


def _group_specs(width, lead=0):
    return [pl.BlockSpec((TM, width), lambda s: (jnp.minimum(_tile(s, lead), N_PT - 1), 0)),
            pl.BlockSpec((TM, width), lambda s: (jnp.maximum(_tile(s, lead) - N_PT, 0), 0))]


def _wchunk_spec(rows, cols, layer=None):
    ck = rows // N_WCHUNK
    if layer is None:
        return pl.BlockSpec((ck, cols), lambda s: (jnp.minimum(s, N_WCHUNK - 1), 0))
    return pl.BlockSpec((1, ck, cols), lambda s: (layer, jnp.minimum(s, N_WCHUNK - 1), 0))


def _stage_weight(step, chunk, dst_ref):
    ck = chunk.shape[0]
    dst_ref[pl.ds(pl.multiple_of(step * ck, ck), ck), :] = chunk.astype(jnp.bfloat16)


def _pick_group(i, p_ref, s_ref, rows=slice(None)):
    return jnp.where(i < N_PT, p_ref[rows, :], s_ref[rows, :])


MOD_TN = 2048
SUB_ROWS = 256


def _mod_kernel(ct_ref, w_ref, b_ref, o_ref):
    s = _silu(ct_ref[...])
    w = w_ref[0]
    b = b_ref[0]
    rows = [jnp.sum(s[:, v:v + 1] * w, axis=0, keepdims=True) + b for v in range(N_MODVEC)]
    rows.append(jnp.zeros((SUBLANES - N_MODVEC, w.shape[1]), jnp.float32))
    o_ref[0] = jnp.concatenate(rows, axis=0)


def _modulation(cvecs_t, w_mod, b_mod):
    nt = 6 * D_MODEL // MOD_TN
    return pl.pallas_call(
        _mod_kernel,
        grid=(DEPTH, nt),
        in_specs=[
            pl.BlockSpec((D_MODEL, SUBLANES), lambda l, j: (0, 0)),
            pl.BlockSpec((1, D_MODEL, MOD_TN), lambda l, j: (l, 0, j)),
            pl.BlockSpec((1, 1, MOD_TN), lambda l, j: (l, 0, j)),
        ],
        out_specs=pl.BlockSpec((1, SUBLANES, MOD_TN), lambda l, j: (l, 0, j)),
        out_shape=jax.ShapeDtypeStruct((DEPTH, SUBLANES, 6 * D_MODEL), jnp.float32),
        compiler_params=pltpu.CompilerParams(dimension_semantics=("arbitrary", "arbitrary"),
                                             vmem_limit_bytes=VMEM_LIMIT),
        name="modulation",
    )(cvecs_t, w_mod, b_mod.reshape(DEPTH, 1, 6 * D_MODEL))


def _split3(a):
    a_hi = a.astype(jnp.bfloat16)
    r1 = a - a_hi.astype(jnp.float32)
    a_mid = r1.astype(jnp.bfloat16)
    a_lo = (r1 - a_mid.astype(jnp.float32)).astype(jnp.bfloat16)
    return a_hi, a_mid, a_lo


def _ssd_head_terms(dt_raw, dtb_c, a_neg_c):
    nh2 = 2 * SSD_HEADS
    row = lax.broadcasted_iota(jnp.int32, (CHUNK, CHUNK), 0)
    col = lax.broadcasted_iota(jnp.int32, (CHUNK, CHUNK), 1)
    upper_b = (row <= col).astype(jnp.bfloat16)
    lower_b = (row >= col).astype(jnp.bfloat16)
    fwd_rows = lax.broadcasted_iota(jnp.int32, (nh2, CHUNK), 0) < SSD_HEADS
    dts_t = _softplus(dt_raw.T[:nh2, :] + dtb_c)
    pieces = _split3(dts_t * a_neg_c)
    cum_t = jnp.where(fwd_rows, sum(_dot(p, upper_b) for p in pieces), sum(_dot(p, lower_b) for p in pieces))
    tot = jnp.where(fwd_rows[:, :1], cum_t[:, CHUNK - 1:], cum_t[:, :1])
    return dts_t, cum_t, dts_t * jnp.exp(tot - cum_t)


def _inproj_kernel(xp_ref, xs_ref, mod_ref, cos_ref, s1_ref, s2_ref, w_in_ref, w_uq_ref, qn_ref, kvn_ref, npm_ref,
                   dtbc_ref, alogc_ref, q_ref, ckv_ref, kpe_ref, z_ref, xbc_ref, dts_ref, cumt_ref, wt_ref):
    i = pl.program_id(0)
    is_latent = i >= N_PT
    sh = mod_ref[0, :, 0:D_MODEL]
    sc = mod_ref[0, :, D_MODEL:2 * D_MODEL]
    scale = (NOPE_DIM + ROPE_DIM) ** -0.5 * np.log2(np.e)
    nh2 = 2 * SSD_HEADS
    dtb_c = dtbc_ref[:nh2, :]
    a_neg_c = -jnp.exp(alogc_ref[:nh2, :])
    dt_raw = []
    for r in range(TM // SUB_ROWS):
        rs = slice(r * SUB_ROWS, (r + 1) * SUB_ROWS)
        h = (_rms(_pick_group(i, xp_ref, xs_ref, rs), npm_ref[...]) * (1.0 + sc) + sh).astype(jnp.bfloat16)
        p = _dot(h, w_in_ref[...])
        o = 0
        cq = p[:, o:o + Q_RANK]; o += Q_RANK
        ckv = p[:, o:o + KV_RANK]; o += KV_RANK
        z_ref[rs, :] = p[:, o:o + D_SSD]; o += D_SSD
        xbc_ref[rs, :] = p[:, o:o + CONV_CH]; o += CONV_CH
        kpe = p[:, o:o + LANES]; o += LANES
        dt_raw.append(p[:, o:o + LANES])

        cos = cos_ref[rs, :]
        s1 = s1_ref[rs, :]
        s2 = s2_ref[rs, :]

        def rope(slab):
            rot = (slab * cos + pltpu.roll(slab, LANES - ROPE_DIM // 2, 1) * s1
                   + pltpu.roll(slab, ROPE_DIM // 2, 1) * s2)
            return jnp.where(is_latent, rot, slab)

        ckv_ref[rs, :] = _rms(ckv, kvn_ref[...])
        kpe_ref[rs, :] = rope(kpe)
        q = _dot(_rms(cq, qn_ref[...]).astype(jnp.bfloat16), w_uq_ref[...]) * scale
        for hd in range(MLA_HEADS):
            sl = slice(hd * HEAD_SLAB, (hd + 1) * HEAD_SLAB)
            q_ref[rs, sl] = rope(q[:, sl]).astype(jnp.bfloat16)

    dt_all = jnp.concatenate(dt_raw, axis=0)
    for ck in range(TM // CHUNK):
        terms = _ssd_head_terms(dt_all[ck * CHUNK:(ck + 1) * CHUNK, :], dtb_c, a_neg_c)
        for ref, val in zip((dts_ref, cumt_ref, wt_ref), terms):
            ref[ck * nh2:(ck + 1) * nh2, :] = val


def _inproj(xp, xs, mod_l, tabs, w_in, w_uq, q_norm, kv_norm, npm, dt_bias_c, a_log_c):
    nt = N_TOK // TM
    tab_spec = pl.BlockSpec((TM, LANES), lambda i: (jnp.maximum(i - N_PT, 0) % (DEC_SEQ // TM), 0))
    row = _row_spec
    hrows = TM // CHUNK * 2 * SSD_HEADS
    head_spec = pl.BlockSpec((hrows, CHUNK), lambda i: (i, 0))
    head_shape = jax.ShapeDtypeStruct((nt * hrows, CHUNK), jnp.float32)
    return pl.pallas_call(
        _inproj_kernel,
        grid=(nt,),
        in_specs=_group_specs(D_MODEL) + [
            _mod_spec(),
            tab_spec, tab_spec, tab_spec,
            _const_spec((D_MODEL, IN_COLS)),
            _const_spec((Q_RANK, MLA_HEADS * HEAD_SLAB)),
            _const_spec((1, Q_RANK)),
            _const_spec((1, KV_RANK)),
            _const_spec((1, D_MODEL)),
            _const_spec((LANES, 1)), _const_spec((LANES, 1)),
        ],
        out_specs=[row(MLA_HEADS * HEAD_SLAB), row(KV_RANK), row(LANES), row(D_SSD), row(CONV_CH),
                   head_spec, head_spec, head_spec],
        out_shape=[
            jax.ShapeDtypeStruct((N_TOK, MLA_HEADS * HEAD_SLAB), jnp.bfloat16),
            jax.ShapeDtypeStruct((N_TOK, KV_RANK), jnp.float32),
            jax.ShapeDtypeStruct((N_TOK, LANES), jnp.float32),
            jax.ShapeDtypeStruct((N_TOK, D_SSD), jnp.float32),
            jax.ShapeDtypeStruct((N_TOK, CONV_CH), jnp.float32),
            head_shape, head_shape, head_shape,
        ],
        compiler_params=pltpu.CompilerParams(dimension_semantics=("arbitrary",), vmem_limit_bytes=VMEM_LIMIT),
        name="inproj",
    )(xp, xs, mod_l, *tabs, w_in, w_uq, q_norm, kv_norm, npm, dt_bias_c, a_log_c)


def _attn_kernel(*refs, lk_cache, lk_new):
    if lk_cache:
        q_ref, ckv_ref, kpe_ref, ckvc_ref, kpec_ref, wk_ref, wvt_ref, o_ref, k_scr, vt_scr = refs
    else:
        q_ref, ckv_ref, kpe_ref, wk_ref, wvt_ref, o_ref, k_scr, vt_scr = refs

    @pl.when(pl.program_id(1) == 0)
    def _expand_kv():
        def expand(ckv, kpe, r0):
            ckv_b = ckv.astype(jnp.bfloat16)
            kn = _dot(ckv_b, wk_ref[...])
            rows = slice(r0, r0 + ckv.shape[0])
            for hd in range(MLA_HEADS):
                k_scr[hd, rows, :] = (kn[:, hd * HEAD_SLAB:(hd + 1) * HEAD_SLAB] + kpe).astype(jnp.bfloat16)
            vt_scr[:, rows] = _dot_nt(wvt_ref[...], ckv_b).astype(jnp.bfloat16)

        step = 256
        for r0 in range(0, lk_cache, step):
            expand(ckvc_ref[0, r0:r0 + step, :], kpec_ref[0, r0:r0 + step, :], r0)
        for r0 in range(0, lk_new, step):
            expand(ckv_ref[r0:r0 + step, :], kpe_ref[r0:r0 + step, :], lk_cache + r0)

    scores = [_dot_nt(k_scr[hd], q_ref[:, hd * HEAD_SLAB:(hd + 1) * HEAD_SLAB]) for hd in range(MLA_HEADS)]
    outs = []
    for hd, s_t in enumerate(scores):
        p_t = jnp.exp2(s_t - jnp.max(s_t, axis=0, keepdims=True))
        den = jnp.sum(p_t, axis=0, keepdims=True)
        outs.append(_dot(vt_scr[hd * V_DIM:(hd + 1) * V_DIM, :], p_t.astype(jnp.bfloat16)) / den)
    o_ref[...] = jnp.concatenate(outs, axis=0).T.astype(jnp.bfloat16)


def _attention(q, ckv_n, kpe, w_ukv, row_off, n_batch, seq, cache=None):
    nq = seq // TQ
    lk_cache = 0 if cache is None else cache[0].shape[1]
    lk = lk_cache + seq
    qblk = lambda b, qi: (row_off // TQ + b * nq + qi, 0)
    sblk = lambda b, qi: (row_off // seq + b, 0)
    in_specs = [
        pl.BlockSpec((TQ, MLA_HEADS * HEAD_SLAB), qblk),
        pl.BlockSpec((seq, KV_RANK), sblk),
        pl.BlockSpec((seq, LANES), sblk),
    ]
    args = [q, ckv_n, kpe]
    if cache is not None:
        in_specs += [pl.BlockSpec((1, lk_cache, KV_RANK), lambda b, qi: (b, 0, 0)),
                     pl.BlockSpec((1, lk_cache, LANES), lambda b, qi: (b, 0, 0))]
        args += list(cache)
    in_specs += [_const_spec(w.shape) for w in w_ukv]
    args += list(w_ukv)
    return pl.pallas_call(
        functools.partial(_attn_kernel, lk_cache=lk_cache, lk_new=seq),
        grid=(n_batch, nq),
        in_specs=in_specs,
        out_specs=pl.BlockSpec((TQ, MLA_HEADS * V_DIM), lambda b, qi: (b * nq + qi, 0)),
        out_shape=jax.ShapeDtypeStruct((n_batch * seq, MLA_HEADS * V_DIM), jnp.bfloat16),
        scratch_shapes=[pltpu.VMEM((MLA_HEADS, lk, HEAD_SLAB), jnp.bfloat16),
                        pltpu.VMEM((MLA_HEADS * V_DIM, lk), jnp.bfloat16)],
        compiler_params=pltpu.CompilerParams(dimension_semantics=("arbitrary", "arbitrary"),
                                             vmem_limit_bytes=VMEM_LIMIT),
        name=f"attention_{seq}",
    )(*args)


def _ssd_kernel(xbc_ref, dts_ref, cumt_ref, wt_ref, h0f_ref, h0b_ref, cw_ref, cb_ref, dsk_ref,
                y_ref, hf_ref, hb_ref, xs_scr, c_scr, bt_scr, cum_scr, stf_scr, stb_scr, *, seq):
    nc = seq // CHUNK
    gs = SSD_GROUPS * SSD_STATE
    nh2 = 2 * SSD_HEADS

    row = lax.broadcasted_iota(jnp.int32, (CHUNK, CHUNK), 0)
    col = lax.broadcasted_iota(jnp.int32, (CHUNK, CHUNK), 1)
    low_half = col < SSD_HEAD_DIM
    lower = row >= col
    upper = row <= col

    def prep_chunk(c, carry):
        r0 = pl.multiple_of(c * CHUNK, CHUNK)
        rows = pl.ds(r0, CHUNK)
        rows_prev = pl.ds(pl.multiple_of(jnp.maximum(r0 - HALO, 0), HALO), HALO)
        rows_next = pl.ds(pl.multiple_of(jnp.minimum(r0 + CHUNK, seq - HALO), HALO), HALO)

        def conv_tile(cs):
            prev = jnp.where(c > 0, xbc_ref[rows_prev, cs], 0.0)
            nxt = jnp.where(c < nc - 1, xbc_ref[rows_next, cs], 0.0)
            win = jnp.concatenate([prev, xbc_ref[rows, cs], nxt], axis=0)
            acc = jnp.broadcast_to(cb_ref[:, cs], (CHUNK, LANES))
            for k in range(CONV_W):
                lo = HALO - CONV_W // 2 + k
                acc = acc + cw_ref[k:k + 1, cs] * win[lo:lo + CHUNK, :]
            return _silu(acc)

        def x_tile(j, carry):
            cs = pl.ds(pl.multiple_of(j * LANES, LANES), LANES)
            u = conv_tile(cs)
            y_ref[rows, cs] = dsk_ref[:, cs] * u
            xs_scr[rows, cs] = u.astype(jnp.bfloat16)
            return carry

        lax.fori_loop(0, D_SSD // LANES, x_tile, 0)
        cum_t = cumt_ref[pl.ds(pl.multiple_of(c * nh2, nh2), nh2), :]
        cum_scr[rows, :] = jnp.concatenate(
            [cum_t, jnp.zeros((CHUNK - nh2, CHUNK), jnp.float32)], axis=0).T[:, :nh2]
        for g in range(SSD_GROUPS):
            b0 = pl.multiple_of(c * gs + g * SSD_STATE, SSD_STATE)
            bt_scr[pl.ds(b0, SSD_STATE), :] = conv_tile(slice(D_SSD + g * SSD_STATE, D_SSD + (g + 1) * SSD_STATE)).T
            c_scr[rows, g * SSD_STATE:(g + 1) * SSD_STATE] = conv_tile(
                slice(D_SSD + gs + g * SSD_STATE, D_SSD + gs + (g + 1) * SSD_STATE)).astype(jnp.bfloat16)
        return carry

    lax.fori_loop(0, nc, prep_chunk, 0)

    stf_scr[...] = h0f_ref[0].T
    stb_scr[...] = h0b_ref[0].T

    def scan_open(ci, st_scr):
        rows = pl.ds(pl.multiple_of(ci * CHUNK, CHUNK), CHUNK)
        c_b = c_scr[rows, :]
        st = st_scr[...]
        bts, cbms, zs = [], [], []
        for g in range(SSD_GROUPS):
            cg = c_b[:, g * SSD_STATE:(g + 1) * SSD_STATE]
            bt = bt_scr[pl.ds(pl.multiple_of(ci * gs + g * SSD_STATE, SSD_STATE), SSD_STATE), :]
            gcols = slice(g * SSD_HPG * SSD_HEAD_DIM, (g + 1) * SSD_HPG * SSD_HEAD_DIM)
            bts.append(bt)
            cbms.append(_dot(cg, bt.astype(jnp.bfloat16)))
            zs.append(_dot(cg, st[:, gcols].astype(jnp.bfloat16)))
        return st, bts, cbms, zs

    def scan_pairs(ci, st_scr, reverse, opened):
        st, bts, cbms, zs = opened
        lane0 = SSD_HEADS if reverse else 0
        causal = upper if reverse else lower
        last = 0 if reverse else CHUNK - 1
        rows = pl.ds(pl.multiple_of(ci * CHUNK, CHUNK), CHUNK)
        hrows = pl.ds(pl.multiple_of(ci * nh2, nh2), nh2)
        xs_b = xs_scr[rows, :]
        dts_t = dts_ref[hrows, :]
        cum_t = cumt_ref[hrows, :]
        w_t = wt_ref[hrows, :]
        cum = cum_scr[rows, :]
        for pair in range(SSD_HEADS // 2):
            g, jj = divmod(pair, SSD_HPG // 2)
            pcols = slice(pair * LANES, (pair + 1) * LANES)
            lhs_y, lhs_s, entry = [], [], []
            for hd in (2 * pair, 2 * pair + 1):
                ln = lane0 + hd
                cum_i = jnp.broadcast_to(cum[:, ln:ln + 1], (CHUNK, CHUNK))
                dec = jnp.exp(jnp.where(causal, cum_i - cum_t[ln:ln + 1, :], -jnp.inf))
                lhs_y.append((cbms[g] * dec * dts_t[ln:ln + 1, :]).astype(jnp.bfloat16))
                lhs_s.append((bts[g] * w_t[ln:ln + 1, :]).astype(jnp.bfloat16))
                entry.append(jnp.exp(cum_i))
            out = _dot(jnp.concatenate(lhs_y + lhs_s, axis=0), xs_b[:, pcols])
            ea = jnp.where(low_half, entry[0], entry[1])
            y_ref[rows, pcols] += (jnp.where(low_half, out[:CHUNK], out[CHUNK:2 * CHUNK])
                                   + zs[g][:, jj * LANES:(jj + 1) * LANES] * ea)
            st_scr[:, pcols] = (ea[last:last + 1, :] * st[:, pcols]
                                + jnp.where(low_half, out[2 * CHUNK:3 * CHUNK], out[3 * CHUNK:]))

    def both(c, carry):
        opened_f = scan_open(c, stf_scr)
        opened_b = scan_open(nc - 1 - c, stb_scr)
        scan_pairs(c, stf_scr, False, opened_f)
        scan_pairs(nc - 1 - c, stb_scr, True, opened_b)
        return carry

    lax.fori_loop(0, nc, both, 0)
    hf_ref[0] = stf_scr[...].T
    hb_ref[0] = stb_scr[...].T


def _ssd(xbc, head_terms, h0f, h0b, conv_w, conv_b, d_skip, row_off, n_batch, seq):
    hp = SSD_HEADS * SSD_HEAD_DIM
    gs = SSD_GROUPS * SSD_STATE
    nc = seq // CHUNK
    seq_blk = lambda b: (row_off // seq + b, 0)
    head_spec = pl.BlockSpec((nc * 2 * SSD_HEADS, CHUNK), seq_blk)
    st_spec = pl.BlockSpec((1, hp, SSD_STATE), lambda b: (b, 0, 0))
    st_shape = jax.ShapeDtypeStruct((n_batch, hp, SSD_STATE), jnp.float32)
    return pl.pallas_call(
        functools.partial(_ssd_kernel, seq=seq),
        grid=(n_batch,),
        in_specs=[pl.BlockSpec((seq, CONV_CH), seq_blk), head_spec, head_spec, head_spec, st_spec, st_spec,
                  _const_spec((SUBLANES, CONV_CH)), _const_spec((1, CONV_CH)), _const_spec((1, D_SSD))],
        out_specs=[pl.BlockSpec((seq, D_SSD), lambda b: (b, 0)), st_spec, st_spec],
        out_shape=[jax.ShapeDtypeStruct((n_batch * seq, D_SSD), jnp.float32), st_shape, st_shape],
        scratch_shapes=[pltpu.VMEM((seq, D_SSD), jnp.bfloat16),
                        pltpu.VMEM((seq, gs), jnp.bfloat16),
                        pltpu.VMEM((nc * gs, CHUNK), jnp.float32),
                        pltpu.VMEM((seq, 2 * SSD_HEADS), jnp.float32),
                        pltpu.VMEM((SSD_STATE, hp), jnp.float32),
                        pltpu.VMEM((SSD_STATE, hp), jnp.float32)],
        compiler_params=pltpu.CompilerParams(dimension_semantics=("arbitrary",), vmem_limit_bytes=VMEM_LIMIT),
        name=f"ssd_{seq}",
    )(xbc, *head_terms, h0f, h0b, conv_w, conv_b, d_skip)


def _post_mix(x, mix, mod_ref, npost_ref, npre_ref):
    d = D_MODEL
    gate_mix = mod_ref[0, :, 2 * d:3 * d]
    shf = mod_ref[0, :, 3 * d:4 * d]
    scf = mod_ref[0, :, 4 * d:5 * d]
    x1 = x + gate_mix * _rms(mix, npost_ref[...])
    return x1, (_rms(x1, npre_ref[...]) * (1.0 + scf) + shf).astype(jnp.bfloat16)


def _subtile_pipeline(n_sub, mixer_pre, mixer_dots, mod_ref, nffn_ref, wg_ref, wu_ref, wd_ref, interleave):
    gate_ffn = mod_ref[0, :, 5 * D_MODEL:6 * D_MODEL]
    ffn_up = lambda h: (_silu(_dot(h, wg_ref[...])) * _dot(h, wu_ref[...])).astype(jnp.bfloat16)
    ffn_down = lambda x1, hid: x1 + gate_ffn * _rms(_dot(hid, wd_ref[...]), nffn_ref[...])
    if not interleave:
        staged = [mixer_dots(r, mixer_pre(r)) for r in range(n_sub)]
        return [ffn_down(x1, ffn_up(h)) for x1, h in staged]
    outs = []
    x1, h = mixer_dots(0, mixer_pre(0))
    for r in range(n_sub):
        nxt_pre = mixer_pre(r + 1) if r + 1 < n_sub else None
        hid = ffn_up(h)
        nxt = mixer_dots(r + 1, nxt_pre) if r + 1 < n_sub else None
        outs.append(ffn_down(x1, hid))
        if nxt is not None:
            x1, h = nxt
    return outs


def _ffn_specs(l):
    return [_const_spec((1, D_MODEL)), _const_spec((1, D_MODEL)), _const_spec((1, D_MODEL)),
            _wchunk_spec(D_MODEL, D_FF, l), _wchunk_spec(D_MODEL, D_FF, l), _wchunk_spec(D_FF, D_MODEL, l)]


def _ffn_scratch():
    return [pltpu.VMEM((D_MODEL, D_FF), jnp.bfloat16), pltpu.VMEM((D_MODEL, D_FF), jnp.bfloat16),
            pltpu.VMEM((D_FF, D_MODEL), jnp.bfloat16)]


def _outproj_ffn_kernel(xp_ref, xs_ref, attp_ref, atts_ref, yp_ref, ys_ref, z_ref, mod_ref, sn_ref, wo_ref,
                        npost_ref, npre_ref, nffn_ref, wg_ref, wu_ref, wd_ref, o_ref,
                        wo_scr, wg_scr, wu_scr, wd_scr):
    step = pl.program_id(0)

    @pl.when(step < N_WCHUNK)
    def _stage():
        _stage_weight(step, wo_ref[...], wo_scr)
        _stage_weight(step, wg_ref[0], wg_scr)
        _stage_weight(step, wu_ref[0], wu_scr)
        _stage_weight(step, wd_ref[0], wd_scr)

    @pl.when(step == 0)
    def _():
        o_ref[...] = jnp.zeros_like(o_ref)

    @pl.when(step >= N_WCHUNK)
    def _tile_step():
        i = step - N_WCHUNK
        gw = D_SSD // SSD_GROUPS
        sub = lambda r: slice(r * SUB_ROWS, (r + 1) * SUB_ROWS)

        def mixer_pre(r):
            yg = _pick_group(i, yp_ref, ys_ref, sub(r)) * _silu(z_ref[sub(r), :])
            parts = [_pick_group(i, attp_ref, atts_ref, sub(r))]
            for g in range(SSD_GROUPS):
                parts.append(_rms(yg[:, g * gw:(g + 1) * gw], sn_ref[:, g * gw:(g + 1) * gw]).astype(jnp.bfloat16))
            return jnp.concatenate(parts, axis=1)

        def mixer_dots(r, cat):
            return _post_mix(_pick_group(i, xp_ref, xs_ref, sub(r)), _dot(cat, wo_scr[...]),
                             mod_ref, npost_ref, npre_ref)

        outs = _subtile_pipeline(TM // SUB_ROWS, mixer_pre, mixer_dots, mod_ref, nffn_ref, wg_scr, wu_scr, wd_scr,
                                 interleave=False)
        for r, res in enumerate(outs):
            o_ref[sub(r), :] = res


def _outproj_ffn(l, xp, xs, att_p, att_s, y_p, y_s, z, mod_l, ssd_norm, w_out, npost, npre, nffn, wg, wu, wd):
    lead = N_WCHUNK
    d_cat = MLA_HEADS * V_DIM + D_SSD
    return pl.pallas_call(
        _outproj_ffn_kernel,
        grid=(lead + N_TOK // TM,),
        in_specs=(_group_specs(D_MODEL, lead) + _group_specs(MLA_HEADS * V_DIM, lead) + _group_specs(D_SSD, lead)
                  + [_row_spec(D_SSD, lead), _mod_spec(lead), _const_spec((1, D_SSD)),
                     _wchunk_spec(d_cat, D_MODEL)] + _ffn_specs(l)),
        out_specs=_row_spec(D_MODEL, lead),
        out_shape=jax.ShapeDtypeStruct((N_TOK, D_MODEL), jnp.float32),
        scratch_shapes=[pltpu.VMEM((d_cat, D_MODEL), jnp.bfloat16)] + _ffn_scratch(),
        compiler_params=pltpu.CompilerParams(dimension_semantics=("arbitrary",), vmem_limit_bytes=VMEM_LIMIT),
        name="outproj_ffn",
    )(xp, xs, att_p, att_s, y_p, y_s, z, mod_l, ssd_norm, w_out, npost, npre, nffn, wg, wu, wd)


def _pool_ffn_kernel(x_ref, xp_ref, xn_ref, mod_ref, nmix_ref, pw_ref, ps_ref,
                     npost_ref, npre_ref, nffn_ref, wg_ref, wu_ref, wd_ref, op_ref, os_ref,
                     wg_scr, wu_scr, wd_scr):
    step = pl.program_id(0)

    @pl.when(step < N_WCHUNK)
    def _stage():
        _stage_weight(step, wg_ref[0], wg_scr)
        _stage_weight(step, wu_ref[0], wu_scr)
        _stage_weight(step, wd_ref[0], wd_scr)

    @pl.when(step == 0)
    def _():
        op_ref[...] = jnp.zeros_like(op_ref)
        os_ref[...] = jnp.zeros_like(os_ref)

    @pl.when(step >= N_WCHUNK)
    def _tile_step():
        _pool_ffn_tile(step - N_WCHUNK, x_ref, xp_ref, xn_ref, mod_ref, nmix_ref, pw_ref, ps_ref,
                       npost_ref, npre_ref, nffn_ref, wg_scr, wu_scr, wd_scr, op_ref, os_ref)


def _pool_ffn_tile(i, x_ref, xp_ref, xn_ref, mod_ref, nmix_ref, pw_ref, ps_ref,
                   npost_ref, npre_ref, nffn_ref, wg_ref, wu_ref, wd_ref, op_ref, os_ref):
    seq = jnp.where(i < N_PROMPT // TM, SEQ, DEC_SEQ)
    pos0 = (i * TM) % seq
    sh = mod_ref[0, :, 0:D_MODEL]
    sc = mod_ref[0, :, D_MODEL:2 * D_MODEL]
    hmod = lambda v: _rms(v, nmix_ref[...]) * (1.0 + sc) + sh
    n_rows = POOL_SUB + 2 * HALO

    def shifted(v, s):
        return pltpu.roll(v, n_rows - s, 0)

    def mixer_pre(s):
        lo, hi = s * POOL_SUB, (s + 1) * POOL_SUB
        pos_s = (pos0 + lo) % seq
        h = hmod(x_ref[lo:hi, :])
        before = hmod(xp_ref[...] if s == 0 else x_ref[lo - HALO:lo, :])
        after = hmod(xn_ref[...] if hi == TM else x_ref[hi:hi + HALO, :])
        before = jnp.where(pos_s > 0, before, 0.0)
        after = jnp.where(pos_s + POOL_SUB < seq, after, 0.0)
        padded = jnp.concatenate([before, h, after], axis=0)
        pos = pos_s + lax.broadcasted_iota(jnp.int32, (POOL_SUB, 1), 0)
        pooled = []
        for gi, w in enumerate(POOL_WINDOWS):
            cols = slice(gi * POOL_GC, (gi + 1) * POOL_GC)
            t = padded[:, cols]
            span = 1
            while span < w:
                t = t + shifted(t, span)
                span *= 2
            lead = HALO - w // 2
            win_sum = (shifted(t, lead) if lead else t)[:POOL_SUB, :]
            cnt = (jnp.minimum(pos + w // 2, seq) - jnp.maximum(pos - w // 2, 0)).astype(jnp.float32)
            pooled.append((win_sum / cnt - h[:, cols]).astype(jnp.bfloat16))
        return pooled

    def mixer_dots(s, pooled):
        mix = jnp.concatenate([_dot(p, pw_ref[gi]) for gi, p in enumerate(pooled)], axis=1) * ps_ref[...]
        return _post_mix(x_ref[s * POOL_SUB:(s + 1) * POOL_SUB, :], mix, mod_ref, npost_ref, npre_ref)

    res = jnp.concatenate(_subtile_pipeline(TM // POOL_SUB, mixer_pre, mixer_dots, mod_ref, nffn_ref,
                                            wg_ref, wu_ref, wd_ref, interleave=True), axis=0)

    @pl.when(i < N_PT)
    def _():
        op_ref[...] = res

    @pl.when(i >= N_PT)
    def _():
        os_ref[...] = res


def _pool_ffn(l, xa, mod_l, nmix, pool_w, pool_scale, npost, npre, nffn, wg, wu, wd):
    lead = N_WCHUNK
    hb = TM // HALO
    nh = N_TOK // HALO
    return pl.pallas_call(
        _pool_ffn_kernel,
        grid=(lead + N_TOK // TM,),
        in_specs=[_row_spec(D_MODEL, lead),
                  pl.BlockSpec((HALO, D_MODEL), lambda s: (jnp.maximum(_tile(s, lead) * hb - 1, 0), 0)),
                  pl.BlockSpec((HALO, D_MODEL), lambda s: (jnp.minimum((_tile(s, lead) + 1) * hb, nh - 1), 0)),
                  _mod_spec(lead),
                  _const_spec((1, D_MODEL)),
                  _const_spec((len(POOL_WINDOWS), POOL_GC, POOL_GC)),
                  _const_spec((1, D_MODEL))] + _ffn_specs(l),
        out_specs=_group_specs(D_MODEL, lead),
        out_shape=[jax.ShapeDtypeStruct((N_PROMPT, D_MODEL), jnp.float32),
                   jax.ShapeDtypeStruct((N_SAMPLE, D_MODEL), jnp.float32)],
        scratch_shapes=_ffn_scratch(),
        compiler_params=pltpu.CompilerParams(dimension_semantics=("arbitrary",), vmem_limit_bytes=VMEM_LIMIT),
        name="pool_ffn",
    )(xa, xa, xa, mod_l, nmix, pool_w, pool_scale, npost, npre, nffn, wg, wu, wd)


def _rope_tables():
    f32 = np.float32
    rows = DEC_SEQ // GRID_W
    r = np.repeat(np.arange(rows, dtype=f32), GRID_W)
    c = np.tile(np.arange(GRID_W, dtype=f32), rows)
    half = ROPE_DIM // 2
    inv_freq = np.power(f32(ROPE_THETA), -np.arange(0, half, 2, dtype=f32) / f32(half)).astype(f32)
    ang = np.concatenate([r[:, None] * inv_freq, c[:, None] * inv_freq], axis=-1).astype(f32)
    cos, sin = np.cos(ang).astype(f32), np.sin(ang).astype(f32)
    zl = np.zeros((DEC_SEQ, KPE_LANE0), f32)
    zr = np.zeros((DEC_SEQ, LANES - KPE_LANE0 - ROPE_DIM), f32)
    zh = np.zeros((DEC_SEQ, half), f32)
    cos_t = np.concatenate([zl + 1, cos, cos, zr + 1], axis=1)
    s1_t = np.concatenate([zl, -sin, zh, zr], axis=1)
    s2_t = np.concatenate([zl, zh, sin, zr], axis=1)
    return jnp.asarray(cos_t), jnp.asarray(s1_t), jnp.asarray(s2_t)


def _kpe_slab(k):
    pad = [(0, 0)] * (k.ndim - 1) + [(KPE_LANE0, LANES - KPE_LANE0 - ROPE_DIM)]
    return jnp.pad(k, pad)


def _layout_in_proj(w):
    o = np.cumsum((0, Q_RANK, KV_RANK, ROPE_DIM, D_SSD, CONV_CH, SSD_HEADS, SSD_HEADS))
    cq, ckv, kpe, z, xbc, dtf, dtb = (w[:, o[k]:o[k + 1]] for k in range(7))
    dt = jnp.pad(jnp.concatenate([dtf, dtb], axis=1), ((0, 0), (0, LANES - 2 * SSD_HEADS)))
    return jnp.concatenate([cq, ckv, z, xbc, _kpe_slab(kpe), dt], axis=1).astype(jnp.bfloat16)


def _layout_uq(w):
    w = w.reshape(Q_RANK, MLA_HEADS, NOPE_DIM + ROPE_DIM)
    w = jnp.pad(w, ((0, 0), (0, 0), (0, HEAD_SLAB - NOPE_DIM - ROPE_DIM)))
    return w.reshape(Q_RANK, MLA_HEADS * HEAD_SLAB).astype(jnp.bfloat16)


def _layout_ukv(w):
    w = w.reshape(KV_RANK, MLA_HEADS, NOPE_DIM + V_DIM)
    kn = jnp.pad(w[:, :, :NOPE_DIM], ((0, 0), (0, 0), (0, HEAD_SLAB - NOPE_DIM)))
    v = w[:, :, NOPE_DIM:]
    return kn.reshape(KV_RANK, -1).astype(jnp.bfloat16), v.reshape(KV_RANK, -1).T.astype(jnp.bfloat16)


def _lane_row(fwd, bwd):
    return jnp.pad(jnp.concatenate([fwd, bwd]), (0, LANES - 2 * SSD_HEADS)).reshape(1, LANES)


def kernel(x_prompt, x_sample, c, cache_mla_ckv, cache_mla_krope, state_ssd_fwd, state_ssd_bwd, c_ctx, w_mod, b_mod, norm_pre_mix, norm_post_mix, norm_pre_ffn, norm_post_ffn, w_in_ab, q_norm, w_uq, kv_norm, w_ukv, ssd_conv_w, ssd_conv_b, ssd_dt_bias_fwd, ssd_dt_bias_bwd, ssd_a_log_fwd, ssd_a_log_bwd, ssd_d, ssd_norm, w_out_ab, pool_w, pool_scale, ffn_w_gate, ffn_w_up, ffn_w_down):
    f32, bf16 = jnp.float32, jnp.bfloat16
    assert DEPTH == 2
    xp = x_prompt.reshape(N_PROMPT, D_MODEL)
    xs = x_sample.reshape(N_SAMPLE, D_MODEL)
    cvecs = jnp.concatenate([c_ctx[None, :], c, jnp.zeros((SUBLANES - N_MODVEC, D_MODEL), f32)], axis=0)
    mod = _modulation(cvecs.T, w_mod, b_mod).reshape(DEPTH, SUBLANES, 1, 6 * D_MODEL)
    tabs = _rope_tables()
    hp = SSD_HEADS * SSD_HEAD_DIM
    row = lambda v: v.reshape(1, -1)
    new_ckv, new_kpe, new_hf, new_hb = [], [], [], []

    for l in range(DEPTH):
        ffn = (row(norm_post_mix[l]), row(norm_pre_ffn[l]), row(norm_post_ffn[l]), ffn_w_gate, ffn_w_up, ffn_w_down)
        if l % 2 == 0:
            i = l // 2
            q, ckv_n, kpe, z, xbc, *heads = _inproj(
                xp, xs, mod[l], tabs, _layout_in_proj(w_in_ab[i]), _layout_uq(w_uq[i]),
                row(q_norm[i]), row(kv_norm[i]), row(norm_pre_mix[l]),
                _lane_row(ssd_dt_bias_fwd[i], ssd_dt_bias_bwd[i]).T, _lane_row(ssd_a_log_fwd[i], ssd_a_log_bwd[i]).T)
            w_kv = _layout_ukv(w_ukv[i])
            att_p = _attention(q, ckv_n, kpe, w_kv, 0, BATCH, SEQ)
            att_s = _attention(q, ckv_n, kpe, w_kv, N_PROMPT, DEC_BATCH, DEC_SEQ,
                               cache=(cache_mla_ckv[:, i], _kpe_slab(cache_mla_krope[:, i])))
            ssd_args = (jnp.pad(ssd_conv_w[i], ((0, SUBLANES - CONV_W), (0, 0))), row(ssd_conv_b[i]),
                        row(jnp.repeat(ssd_d[i], SSD_HEAD_DIM)))
            h_zero = jnp.zeros((BATCH, hp, SSD_STATE), f32)
            y_p, hf, hb = _ssd(xbc, heads, h_zero, h_zero, *ssd_args, 0, BATCH, SEQ)
            y_s, _, _ = _ssd(xbc, heads, state_ssd_fwd[:, i].reshape(DEC_BATCH, hp, SSD_STATE),
                             state_ssd_bwd[:, i].reshape(DEC_BATCH, hp, SSD_STATE),
                             *ssd_args, N_PROMPT, DEC_BATCH, DEC_SEQ)
            xa = _outproj_ffn(l, xp, xs, att_p, att_s, y_p, y_s, z,
                              mod[l], row(ssd_norm[i]), w_out_ab[i], *ffn)
            new_ckv.append(ckv_n[:N_PROMPT].reshape(BATCH, SEQ, KV_RANK))
            new_kpe.append(kpe[:N_PROMPT, KPE_LANE0:KPE_LANE0 + ROPE_DIM].reshape(BATCH, SEQ, ROPE_DIM))
            new_hf.append(hf.reshape(BATCH, SSD_HEADS, SSD_HEAD_DIM, SSD_STATE))
            new_hb.append(hb.reshape(BATCH, SSD_HEADS, SSD_HEAD_DIM, SSD_STATE))
        else:
            j = l // 2
            yp, ys = _pool_ffn(l, xa, mod[l], row(norm_pre_mix[l]), pool_w[j].astype(bf16), row(pool_scale[j]), *ffn)

    return (yp.reshape(BATCH, SEQ, D_MODEL), ys.reshape(DEC_BATCH, DEC_SEQ, D_MODEL),
            jnp.stack(new_ckv, axis=1), jnp.stack(new_kpe, axis=1),
            jnp.stack(new_hf, axis=1), jnp.stack(new_hb, axis=1))
```

```python
import functools

import numpy as np
import jax
import jax.numpy as jnp
from jax import lax
from jax.experimental import pallas as pl
from jax.experimental.pallas import tpu as pltpu

D_MODEL = 1024
BATCH = 16
SEQ = 256
DEPTH = 2
DEC_BATCH = 2
DEC_SEQ = 2048
PAST_LEN = 256
GRID_W = 64
EPS = 1e-6
MLA_HEADS = 8
Q_RANK = 256
KV_RANK = 256
NOPE_DIM = 64
ROPE_DIM = 32
V_DIM = 64
ROPE_THETA = 10000.0
SSD_HEADS = 8
SSD_GROUPS = 2
SSD_HPG = SSD_HEADS // SSD_GROUPS
SSD_HEAD_DIM = 64
SSD_STATE = 128
D_SSD = SSD_HEADS * SSD_HEAD_DIM
CONV_W = 5
CONV_CH = D_SSD + 2 * SSD_GROUPS * SSD_STATE
POOL_WINDOWS = (2, 4, 8, 16)
POOL_GC = D_MODEL // len(POOL_WINDOWS)
D_FF = ((8 * D_MODEL + 3 * 256 - 1) // (3 * 256)) * 256

SUBLANES = 8
LANES = 128

N_PROMPT = BATCH * SEQ
N_SAMPLE = DEC_BATCH * DEC_SEQ
N_TOK = N_PROMPT + N_SAMPLE
N_MODVEC = 1 + DEC_BATCH
TM = 512
TQ = 256
CHUNK = 128
HALO = SUBLANES
HEAD_SLAB = LANES
IN_COLS = Q_RANK + KV_RANK + D_SSD + CONV_CH + 2 * LANES
KPE_LANE0 = NOPE_DIM
VMEM_LIMIT = 56 * 1024 * 1024

POOL_SUB = min(SEQ, DEC_SEQ)

assert TM % POOL_SUB == 0 and SEQ % POOL_SUB == 0 and DEC_SEQ % TM == 0 and N_PROMPT % DEC_SEQ == 0


def _rms(x, g):
    return x * lax.rsqrt(jnp.mean(x * x, axis=-1, keepdims=True) + EPS) * g


def _silu(x):
    return x * jax.nn.sigmoid(x)


def _softplus(x):
    return jnp.maximum(x, 0.0) + jnp.log1p(jnp.exp(-jnp.abs(x)))


def _dot(a, b):
    return jnp.dot(a, b, preferred_element_type=jnp.float32)


def _dot_nt(a, b):
    return lax.dot_general(a, b, (((1,), (1,)), ((), ())), preferred_element_type=jnp.float32)


def _mod_row(i):
    return jnp.where(i < N_PROMPT // TM, 0, 1 + (i - N_PROMPT // TM) // (DEC_SEQ // TM))


def _const_spec(shape):
    nd = len(shape)
    return pl.BlockSpec(shape, lambda *_: (0,) * nd, pipeline_mode=pl.Buffered(1))


N_PT = N_PROMPT // TM
N_WCHUNK = 16


def _tile(step, lead):
    return jnp.maximum(step - lead, 0)


def _row_spec(width, lead=0):
    return pl.BlockSpec((TM, width), lambda s: (_tile(s, lead), 0))


def _mod_spec(l, lead=0):
    return pl.BlockSpec((None, 1, 1, 6 * D_MODEL), lambda s: (l, _mod_row(_tile(s, lead)), 0, 0))


def _vec_spec(width, l):
    return pl.BlockSpec((None, 1, width), lambda *_: (l, 0, 0), pipeline_mode=pl.Buffered(1))


def _as_rows(a):
    return a.reshape(a.shape[0], 1, a.shape[1])


def _group_specs(width, lead=0):
    return [pl.BlockSpec((TM, width), lambda s: (jnp.minimum(_tile(s, lead), N_PT - 1), 0)),
            pl.BlockSpec((TM, width), lambda s: (jnp.maximum(_tile(s, lead) - N_PT, 0), 0))]


def _wchunk_spec(rows, cols, layer=None):
    ck = rows // N_WCHUNK
    if layer is None:
        return pl.BlockSpec((ck, cols), lambda s: (jnp.minimum(s, N_WCHUNK - 1), 0))
    return pl.BlockSpec((1, ck, cols), lambda s: (layer, jnp.minimum(s, N_WCHUNK - 1), 0))


def _stage_weight(step, chunk, dst_ref):
    ck = chunk.shape[0]
    dst_ref[pl.ds(pl.multiple_of(step * ck, ck), ck), :] = chunk.astype(jnp.bfloat16)


def _pick_group(i, p_ref, s_ref, rows=slice(None)):
    return jnp.where(i < N_PT, p_ref[rows, :], s_ref[rows, :])


MOD_TN = 1024
MOD_STREAMS = 2
SUB_ROWS = 256


def _mod_kernel(ct_ref, *refs):
    w_refs, b_ref, o_ref = refs[:MOD_STREAMS], refs[MOD_STREAMS], refs[MOD_STREAMS + 1]
    s = _silu(ct_ref[...])
    for k, w_ref in enumerate(w_refs):
        cols = slice(k * MOD_TN, (k + 1) * MOD_TN)
        w = w_ref[0]
        b = b_ref[0, :, cols]
        rows = [jnp.sum(s[:, v:v + 1] * w, axis=0, keepdims=True) + b for v in range(N_MODVEC)]
        rows.append(jnp.zeros((SUBLANES - N_MODVEC, MOD_TN), jnp.float32))
        o_ref[0, :, cols] = jnp.concatenate(rows, axis=0)


def _modulation(cvecs_t, w_mod, b_mod):
    step_cols = MOD_STREAMS * MOD_TN
    nt = 6 * D_MODEL // step_cols
    w_spec = lambda k: pl.BlockSpec((1, D_MODEL, MOD_TN), lambda l, j: (l, 0, j * MOD_STREAMS + k))
    return pl.pallas_call(
        _mod_kernel,
        grid=(DEPTH, nt),
        in_specs=[pl.BlockSpec((D_MODEL, SUBLANES), lambda l, j: (0, 0))]
        + [w_spec(k) for k in range(MOD_STREAMS)]
        + [pl.BlockSpec((1, 1, step_cols), lambda l, j: (l, 0, j))],
        out_specs=pl.BlockSpec((1, SUBLANES, step_cols), lambda l, j: (l, 0, j)),
        out_shape=jax.ShapeDtypeStruct((DEPTH, SUBLANES, 6 * D_MODEL), jnp.float32),
        compiler_params=pltpu.CompilerParams(dimension_semantics=("arbitrary", "arbitrary"),
                                             vmem_limit_bytes=VMEM_LIMIT),
        name="modulation",
    )(cvecs_t, *([w_mod] * MOD_STREAMS), b_mod.reshape(DEPTH, 1, 6 * D_MODEL))


def _split3(a):
    a_hi = a.astype(jnp.bfloat16)
    r1 = a - a_hi.astype(jnp.float32)
    a_mid = r1.astype(jnp.bfloat16)
    a_lo = (r1 - a_mid.astype(jnp.float32)).astype(jnp.bfloat16)
    return a_hi, a_mid, a_lo


def _ssd_head_terms(dt_raw, dtb_c, a_neg_c):
    nh2 = 2 * SSD_HEADS
    row = lax.broadcasted_iota(jnp.int32, (CHUNK, CHUNK), 0)
    col = lax.broadcasted_iota(jnp.int32, (CHUNK, CHUNK), 1)
    upper_b = (row <= col).astype(jnp.bfloat16)
    lower_b = (row >= col).astype(jnp.bfloat16)
    fwd_rows = lax.broadcasted_iota(jnp.int32, (nh2, CHUNK), 0) < SSD_HEADS
    dts_t = _softplus(dt_raw.T[:nh2, :] + dtb_c)
    pieces = _split3(dts_t * a_neg_c)
    cum_t = jnp.where(fwd_rows, sum(_dot(p, upper_b) for p in pieces), sum(_dot(p, lower_b) for p in pieces))
    tot = jnp.where(fwd_rows[:, :1], cum_t[:, CHUNK - 1:], cum_t[:, :1])
    return dts_t, cum_t, dts_t * jnp.exp(tot - cum_t)


def _inproj_kernel(xp_ref, xs_ref, mod_ref, cos_ref, s1_ref, s2_ref, w_in_ref, w_uq_ref, qn_ref, kvn_ref, npm_ref,
                   dtbc_ref, alogc_ref, q_ref, ckv_ref, kpe_ref, z_ref, xbc_ref, dts_ref, cumt_ref, wt_ref):
    i = pl.program_id(0)
    is_latent = i >= N_PT
    sh = mod_ref[0, :, 0:D_MODEL]
    sc = mod_ref[0, :, D_MODEL:2 * D_MODEL]
    scale = (NOPE_DIM + ROPE_DIM) ** -0.5 * np.log2(np.e)
    nh2 = 2 * SSD_HEADS
    dtb_c = dtbc_ref[:nh2, :]
    a_neg_c = -jnp.exp(alogc_ref[:nh2, :])
    dt_raw = []
    for r in range(TM // SUB_ROWS):
        rs = slice(r * SUB_ROWS, (r + 1) * SUB_ROWS)
        h = (_rms(_pick_group(i, xp_ref, xs_ref, rs), npm_ref[...]) * (1.0 + sc) + sh).astype(jnp.bfloat16)
        p = _dot(h, w_in_ref[...])
        o = 0
        cq = p[:, o:o + Q_RANK]; o += Q_RANK
        ckv = p[:, o:o + KV_RANK]; o += KV_RANK
        z_ref[rs, :] = p[:, o:o + D_SSD]; o += D_SSD
        xbc_ref[rs, :] = p[:, o:o + CONV_CH]; o += CONV_CH
        kpe = p[:, o:o + LANES]; o += LANES
        dt_raw.append(p[:, o:o + LANES])

        cos = cos_ref[rs, :]
        s1 = s1_ref[rs, :]
        s2 = s2_ref[rs, :]

        def rope(slab):
            rot = (slab * cos + pltpu.roll(slab, LANES - ROPE_DIM // 2, 1) * s1
                   + pltpu.roll(slab, ROPE_DIM // 2, 1) * s2)
            return jnp.where(is_latent, rot, slab)

        ckv_ref[rs, :] = _rms(ckv, kvn_ref[...])
        kpe_ref[rs, :] = rope(kpe)
        q = _dot(_rms(cq, qn_ref[...]).astype(jnp.bfloat16), w_uq_ref[...]) * scale
        for hd in range(MLA_HEADS):
            sl = slice(hd * HEAD_SLAB, (hd + 1) * HEAD_SLAB)
            q_ref[rs, sl] = rope(q[:, sl]).astype(jnp.bfloat16)

    dt_all = jnp.concatenate(dt_raw, axis=0)
    for ck in range(TM // CHUNK):
        terms = _ssd_head_terms(dt_all[ck * CHUNK:(ck + 1) * CHUNK, :], dtb_c, a_neg_c)
        for ref, val in zip((dts_ref, cumt_ref, wt_ref), terms):
            ref[ck * nh2:(ck + 1) * nh2, :] = val


def _inproj(l, i_ab, xp, xs, mod, tabs, w_in, w_uq, q_norm, kv_norm, npm, dt_bias_c, a_log_c):
    nt = N_TOK // TM
    tab_spec = pl.BlockSpec((TM, LANES), lambda i: (jnp.maximum(i - N_PT, 0) % (DEC_SEQ // TM), 0))
    row = _row_spec
    hrows = TM // CHUNK * 2 * SSD_HEADS
    head_spec = pl.BlockSpec((hrows, CHUNK), lambda i: (i, 0))
    head_shape = jax.ShapeDtypeStruct((nt * hrows, CHUNK), jnp.float32)
    return pl.pallas_call(
        _inproj_kernel,
        grid=(nt,),
        in_specs=_group_specs(D_MODEL) + [
            _mod_spec(l),
            tab_spec, tab_spec, tab_spec,
            _const_spec((D_MODEL, IN_COLS)),
            _const_spec((Q_RANK, MLA_HEADS * HEAD_SLAB)),
            _vec_spec(Q_RANK, i_ab),
            _vec_spec(KV_RANK, i_ab),
            _vec_spec(D_MODEL, l),
            _const_spec((LANES, 1)), _const_spec((LANES, 1)),
        ],
        out_specs=[row(MLA_HEADS * HEAD_SLAB), row(KV_RANK), row(LANES), row(D_SSD), row(CONV_CH),
                   head_spec, head_spec, head_spec],
        out_shape=[
            jax.ShapeDtypeStruct((N_TOK, MLA_HEADS * HEAD_SLAB), jnp.bfloat16),
            jax.ShapeDtypeStruct((N_TOK, KV_RANK), jnp.float32),
            jax.ShapeDtypeStruct((N_TOK, LANES), jnp.float32),
            jax.ShapeDtypeStruct((N_TOK, D_SSD), jnp.float32),
            jax.ShapeDtypeStruct((N_TOK, CONV_CH), jnp.float32),
            head_shape, head_shape, head_shape,
        ],
        compiler_params=pltpu.CompilerParams(dimension_semantics=("arbitrary",), vmem_limit_bytes=VMEM_LIMIT),
        name="inproj",
    )(xp, xs, mod, *tabs, w_in, w_uq, q_norm, kv_norm, npm, dt_bias_c, a_log_c)


def _attn_kernel(*refs, lk_cache, lk_new):
    if lk_cache:
        q_ref, ckv_ref, kpe_ref, ckvc_ref, kpec_ref, wk_ref, wvt_ref, o_ref, k_scr, vt_scr = refs
    else:
        q_ref, ckv_ref, kpe_ref, wk_ref, wvt_ref, o_ref, k_scr, vt_scr = refs

    @pl.when(pl.program_id(1) == 0)
    def _expand_kv():
        def expand(ckv, kpe, r0):
            ckv_b = ckv.astype(jnp.bfloat16)
            kn = _dot(ckv_b, wk_ref[...])
            rows = slice(r0, r0 + ckv.shape[0])
            for hd in range(MLA_HEADS):
                k_scr[hd, rows, :] = (kn[:, hd * HEAD_SLAB:(hd + 1) * HEAD_SLAB] + kpe).astype(jnp.bfloat16)
            vt_scr[:, rows] = _dot_nt(wvt_ref[...], ckv_b).astype(jnp.bfloat16)

        step = 256
        for r0 in range(0, lk_cache, step):
            expand(ckvc_ref[0, r0:r0 + step, :], kpec_ref[0, r0:r0 + step, :], r0)
        for r0 in range(0, lk_new, step):
            expand(ckv_ref[r0:r0 + step, :], kpe_ref[r0:r0 + step, :], lk_cache + r0)

    scores = [_dot_nt(k_scr[hd], q_ref[:, hd * HEAD_SLAB:(hd + 1) * HEAD_SLAB]) for hd in range(MLA_HEADS)]
    outs = []
    for hd, s_t in enumerate(scores):
        p_t = jnp.exp2(s_t - jnp.max(s_t, axis=0, keepdims=True))
        den = jnp.sum(p_t, axis=0, keepdims=True)
        outs.append(_dot(vt_scr[hd * V_DIM:(hd + 1) * V_DIM, :], p_t.astype(jnp.bfloat16)) / den)
    o_ref[...] = jnp.concatenate(outs, axis=0).T.astype(jnp.bfloat16)


def _attention(q, ckv_n, kpe, w_ukv, row_off, n_batch, seq, cache=None):
    nq = seq // TQ
    lk_cache = 0 if cache is None else cache[0].shape[1]
    lk = lk_cache + seq
    qblk = lambda b, qi: (row_off // TQ + b * nq + qi, 0)
    sblk = lambda b, qi: (row_off // seq + b, 0)
    in_specs = [
        pl.BlockSpec((TQ, MLA_HEADS * HEAD_SLAB), qblk),
        pl.BlockSpec((seq, KV_RANK), sblk),
        pl.BlockSpec((seq, LANES), sblk),
    ]
    args = [q, ckv_n, kpe]
    if cache is not None:
        in_specs += [pl.BlockSpec((1, lk_cache, KV_RANK), lambda b, qi: (b, 0, 0)),
                     pl.BlockSpec((1, lk_cache, LANES), lambda b, qi: (b, 0, 0))]
        args += list(cache)
    in_specs += [_const_spec(w.shape) for w in w_ukv]
    args += list(w_ukv)
    return pl.pallas_call(
        functools.partial(_attn_kernel, lk_cache=lk_cache, lk_new=seq),
        grid=(n_batch, nq),
        in_specs=in_specs,
        out_specs=pl.BlockSpec((TQ, MLA_HEADS * V_DIM), lambda b, qi: (b * nq + qi, 0)),
        out_shape=jax.ShapeDtypeStruct((n_batch * seq, MLA_HEADS * V_DIM), jnp.bfloat16),
        scratch_shapes=[pltpu.VMEM((MLA_HEADS, lk, HEAD_SLAB), jnp.bfloat16),
                        pltpu.VMEM((MLA_HEADS * V_DIM, lk), jnp.bfloat16)],
        compiler_params=pltpu.CompilerParams(dimension_semantics=("arbitrary", "arbitrary"),
                                             vmem_limit_bytes=VMEM_LIMIT),
        name=f"attention_{seq}",
    )(*args)


def _ssd_kernel(*refs, seq, zero_init):
    if zero_init:
        (xbc_ref, dts_ref, cumt_ref, wt_ref, cw_ref, cb_ref, dsk_ref,
         y_ref, hf_ref, hb_ref, xs_scr, c_scr, bt_scr, cum_scr, stf_scr, stb_scr) = refs
    else:
        (xbc_ref, dts_ref, cumt_ref, wt_ref, h0f_ref, h0b_ref, cw_ref, cb_ref, dsk_ref,
         y_ref, hf_ref, hb_ref, xs_scr, c_scr, bt_scr, cum_scr, stf_scr, stb_scr) = refs
    nc = seq // CHUNK
    gs = SSD_GROUPS * SSD_STATE
    nh2 = 2 * SSD_HEADS

    row = lax.broadcasted_iota(jnp.int32, (CHUNK, CHUNK), 0)
    col = lax.broadcasted_iota(jnp.int32, (CHUNK, CHUNK), 1)
    low_half = col < SSD_HEAD_DIM
    lower = row >= col
    upper = row <= col

    def prep_chunk(c, carry):
        r0 = pl.multiple_of(c * CHUNK, CHUNK)
        rows = pl.ds(r0, CHUNK)
        rows_prev = pl.ds(pl.multiple_of(jnp.maximum(r0 - HALO, 0), HALO), HALO)
        rows_next = pl.ds(pl.multiple_of(jnp.minimum(r0 + CHUNK, seq - HALO), HALO), HALO)

        def conv_tile(cs):
            prev = jnp.where(c > 0, xbc_ref[rows_prev, cs], 0.0)
            nxt = jnp.where(c < nc - 1, xbc_ref[rows_next, cs], 0.0)
            win = jnp.concatenate([prev, xbc_ref[rows, cs], nxt], axis=0)
            acc = jnp.broadcast_to(cb_ref[:, cs], (CHUNK, LANES))
            for k in range(CONV_W):
                lo = HALO - CONV_W // 2 + k
                acc = acc + cw_ref[k:k + 1, cs] * win[lo:lo + CHUNK, :]
            return _silu(acc)

        def x_tile(j, carry):
            cs = pl.ds(pl.multiple_of(j * LANES, LANES), LANES)
            u = conv_tile(cs)
            y_ref[rows, cs] = dsk_ref[:, cs] * u
            xs_scr[rows, cs] = u.astype(jnp.bfloat16)
            return carry

        lax.fori_loop(0, D_SSD // LANES, x_tile, 0)
        cum_t = cumt_ref[pl.ds(pl.multiple_of(c * nh2, nh2), nh2), :]
        cum_scr[rows, :] = jnp.concatenate(
            [cum_t, jnp.zeros((CHUNK - nh2, CHUNK), jnp.float32)], axis=0).T[:, :nh2]
        for g in range(SSD_GROUPS):
            b0 = pl.multiple_of(c * gs + g * SSD_STATE, SSD_STATE)
            bt_scr[pl.ds(b0, SSD_STATE), :] = conv_tile(slice(D_SSD + g * SSD_STATE, D_SSD + (g + 1) * SSD_STATE)).T
            c_scr[rows, g * SSD_STATE:(g + 1) * SSD_STATE] = conv_tile(
                slice(D_SSD + gs + g * SSD_STATE, D_SSD + gs + (g + 1) * SSD_STATE)).astype(jnp.bfloat16)
        return carry

    lax.fori_loop(0, nc, prep_chunk, 0)

    if zero_init:
        stf_scr[...] = jnp.zeros_like(stf_scr)
        stb_scr[...] = jnp.zeros_like(stb_scr)
    else:
        stf_scr[...] = h0f_ref[0].T
        stb_scr[...] = h0b_ref[0].T

    def scan_open(ci, st_scr):
        rows = pl.ds(pl.multiple_of(ci * CHUNK, CHUNK), CHUNK)
        c_b = c_scr[rows, :]
        st = st_scr[...]
        bts, cbms, zs = [], [], []
        for g in range(SSD_GROUPS):
            cg = c_b[:, g * SSD_STATE:(g + 1) * SSD_STATE]
            bt = bt_scr[pl.ds(pl.multiple_of(ci * gs + g * SSD_STATE, SSD_STATE), SSD_STATE), :]
            gcols = slice(g * SSD_HPG * SSD_HEAD_DIM, (g + 1) * SSD_HPG * SSD_HEAD_DIM)
            bts.append(bt)
            cbms.append(_dot(cg, bt.astype(jnp.bfloat16)))
            zs.append(_dot(cg, st[:, gcols].astype(jnp.bfloat16)))
        return st, bts, cbms, zs

    def scan_pairs(ci, st_scr, reverse, opened):
        st, bts, cbms, zs = opened
        lane0 = SSD_HEADS if reverse else 0
        causal = upper if reverse else lower
        last = 0 if reverse else CHUNK - 1
        rows = pl.ds(pl.multiple_of(ci * CHUNK, CHUNK), CHUNK)
        hrows = pl.ds(pl.multiple_of(ci * nh2, nh2), nh2)
        xs_b = xs_scr[rows, :]
        dts_t = dts_ref[hrows, :]
        cum_t = cumt_ref[hrows, :]
        w_t = wt_ref[hrows, :]
        cum = cum_scr[rows, :]
        for pair in range(SSD_HEADS // 2):
            g, jj = divmod(pair, SSD_HPG // 2)
            pcols = slice(pair * LANES, (pair + 1) * LANES)
            lhs_y, lhs_s, entry = [], [], []
            for hd in (2 * pair, 2 * pair + 1):
                ln = lane0 + hd
                cum_i = jnp.broadcast_to(cum[:, ln:ln + 1], (CHUNK, CHUNK))
                dec = jnp.exp(jnp.where(causal, cum_i - cum_t[ln:ln + 1, :], -jnp.inf))
                lhs_y.append((cbms[g] * dec * dts_t[ln:ln + 1, :]).astype(jnp.bfloat16))
                lhs_s.append((bts[g] * w_t[ln:ln + 1, :]).astype(jnp.bfloat16))
                entry.append(jnp.exp(cum_i))
            out = _dot(jnp.concatenate(lhs_y + lhs_s, axis=0), xs_b[:, pcols])
            ea = jnp.where(low_half, entry[0], entry[1])
            y_ref[rows, pcols] += (jnp.where(low_half, out[:CHUNK], out[CHUNK:2 * CHUNK])
                                   + zs[g][:, jj * LANES:(jj + 1) * LANES] * ea)
            st_scr[:, pcols] = (ea[last:last + 1, :] * st[:, pcols]
                                + jnp.where(low_half, out[2 * CHUNK:3 * CHUNK], out[3 * CHUNK:]))

    def both(c, carry):
        opened_f = scan_open(c, stf_scr)
        opened_b = scan_open(nc - 1 - c, stb_scr)
        scan_pairs(c, stf_scr, False, opened_f)
        scan_pairs(nc - 1 - c, stb_scr, True, opened_b)
        return carry

    lax.fori_loop(0, nc, both, 0)
    hf_ref[0] = stf_scr[...].T
    hb_ref[0] = stb_scr[...].T


def _ssd(i_ab, xbc, head_terms, h0, conv_w, conv_b, d_skip, row_off, n_batch, seq):
    hp = SSD_HEADS * SSD_HEAD_DIM
    gs = SSD_GROUPS * SSD_STATE
    nc = seq // CHUNK
    seq_blk = lambda b: (row_off // seq + b, 0)
    head_spec = pl.BlockSpec((nc * 2 * SSD_HEADS, CHUNK), seq_blk)
    st_spec = pl.BlockSpec((1, hp, SSD_STATE), lambda b: (b, 0, 0))
    st_shape = jax.ShapeDtypeStruct((n_batch, hp, SSD_STATE), jnp.float32)
    h0 = () if h0 is None else tuple(h0)
    return pl.pallas_call(
        functools.partial(_ssd_kernel, seq=seq, zero_init=not h0),
        grid=(n_batch,),
        in_specs=[pl.BlockSpec((seq, CONV_CH), seq_blk), head_spec, head_spec, head_spec] + [st_spec] * len(h0) + [
            pl.BlockSpec((None, CONV_W, CONV_CH), lambda b: (i_ab, 0, 0), pipeline_mode=pl.Buffered(1)),
            _vec_spec(CONV_CH, i_ab), _const_spec((1, D_SSD))],
        out_specs=[pl.BlockSpec((seq, D_SSD), lambda b: (b, 0)), st_spec, st_spec],
        out_shape=[jax.ShapeDtypeStruct((n_batch * seq, D_SSD), jnp.float32), st_shape, st_shape],
        scratch_shapes=[pltpu.VMEM((seq, D_SSD), jnp.bfloat16),
                        pltpu.VMEM((seq, gs), jnp.bfloat16),
                        pltpu.VMEM((nc * gs, CHUNK), jnp.float32),
                        pltpu.VMEM((seq, 2 * SSD_HEADS), jnp.float32),
                        pltpu.VMEM((SSD_STATE, hp), jnp.float32),
                        pltpu.VMEM((SSD_STATE, hp), jnp.float32)],
        compiler_params=pltpu.CompilerParams(dimension_semantics=("arbitrary",), vmem_limit_bytes=VMEM_LIMIT),
        name=f"ssd_{seq}",
    )(xbc, *head_terms, *h0, conv_w, conv_b, d_skip)


def _post_mix(x, mix, mod_ref, npost_ref, npre_ref):
    d = D_MODEL
    gate_mix = mod_ref[0, :, 2 * d:3 * d]
    shf = mod_ref[0, :, 3 * d:4 * d]
    scf = mod_ref[0, :, 4 * d:5 * d]
    x1 = x + gate_mix * _rms(mix, npost_ref[...])
    return x1, (_rms(x1, npre_ref[...]) * (1.0 + scf) + shf).astype(jnp.bfloat16)


def _subtile_pipeline(n_sub, mixer_pre, mixer_dots, mod_ref, nffn_ref, wg_ref, wu_ref, wd_ref, interleave):
    gate_ffn = mod_ref[0, :, 5 * D_MODEL:6 * D_MODEL]
    ffn_up = lambda h: (_silu(_dot(h, wg_ref[...])) * _dot(h, wu_ref[...])).astype(jnp.bfloat16)
    ffn_down = lambda x1, hid: x1 + gate_ffn * _rms(_dot(hid, wd_ref[...]), nffn_ref[...])
    if not interleave:
        staged = [mixer_dots(r, mixer_pre(r)) for r in range(n_sub)]
        return [ffn_down(x1, ffn_up(h)) for x1, h in staged]
    outs = []
    x1, h = mixer_dots(0, mixer_pre(0))
    for r in range(n_sub):
        nxt_pre = mixer_pre(r + 1) if r + 1 < n_sub else None
        hid = ffn_up(h)
        nxt = mixer_dots(r + 1, nxt_pre) if r + 1 < n_sub else None
        outs.append(ffn_down(x1, hid))
        if nxt is not None:
            x1, h = nxt
    return outs


def _ffn_specs(l):
    return [_vec_spec(D_MODEL, l), _vec_spec(D_MODEL, l), _vec_spec(D_MODEL, l),
            _wchunk_spec(D_MODEL, D_FF, l), _wchunk_spec(D_MODEL, D_FF, l), _wchunk_spec(D_FF, D_MODEL, l)]


def _ffn_scratch():
    return [pltpu.VMEM((D_MODEL, D_FF), jnp.bfloat16), pltpu.VMEM((D_MODEL, D_FF), jnp.bfloat16),
            pltpu.VMEM((D_FF, D_MODEL), jnp.bfloat16)]


def _outproj_ffn_kernel(xp_ref, xs_ref, attp_ref, atts_ref, yp_ref, ys_ref, z_ref, mod_ref, sn_ref, wo_ref,
                        npost_ref, npre_ref, nffn_ref, wg_ref, wu_ref, wd_ref, o_ref,
                        wo_scr, wg_scr, wu_scr, wd_scr):
    step = pl.program_id(0)

    @pl.when(step < N_WCHUNK)
    def _stage():
        _stage_weight(step, wo_ref[0], wo_scr)
        _stage_weight(step, wg_ref[0], wg_scr)
        _stage_weight(step, wu_ref[0], wu_scr)
        _stage_weight(step, wd_ref[0], wd_scr)

    @pl.when(step == 0)
    def _():
        o_ref[...] = jnp.zeros_like(o_ref)

    @pl.when(step >= N_WCHUNK)
    def _tile_step():
        i = step - N_WCHUNK
        gw = D_SSD // SSD_GROUPS
        sub = lambda r: slice(r * SUB_ROWS, (r + 1) * SUB_ROWS)

        def mixer_pre(r):
            yg = _pick_group(i, yp_ref, ys_ref, sub(r)) * _silu(z_ref[sub(r), :])
            parts = [_pick_group(i, attp_ref, atts_ref, sub(r))]
            for g in range(SSD_GROUPS):
                parts.append(_rms(yg[:, g * gw:(g + 1) * gw], sn_ref[:, g * gw:(g + 1) * gw]).astype(jnp.bfloat16))
            return jnp.concatenate(parts, axis=1)

        def mixer_dots(r, cat):
            return _post_mix(_pick_group(i, xp_ref, xs_ref, sub(r)), _dot(cat, wo_scr[...]),
                             mod_ref, npost_ref, npre_ref)

        outs = _subtile_pipeline(TM // SUB_ROWS, mixer_pre, mixer_dots, mod_ref, nffn_ref, wg_scr, wu_scr, wd_scr,
                                 interleave=False)
        for r, res in enumerate(outs):
            o_ref[sub(r), :] = res


def _outproj_ffn(l, i_ab, xp, xs, att_p, att_s, y_p, y_s, z, mod, ssd_norm, w_out, npost, npre, nffn, wg, wu, wd):
    lead = N_WCHUNK
    d_cat = MLA_HEADS * V_DIM + D_SSD
    return pl.pallas_call(
        _outproj_ffn_kernel,
        grid=(lead + N_TOK // TM,),
        in_specs=(_group_specs(D_MODEL, lead) + _group_specs(MLA_HEADS * V_DIM, lead) + _group_specs(D_SSD, lead)
                  + [_row_spec(D_SSD, lead), _mod_spec(l, lead), _vec_spec(D_SSD, i_ab),
                     _wchunk_spec(d_cat, D_MODEL, i_ab)] + _ffn_specs(l)),
        out_specs=_row_spec(D_MODEL, lead),
        out_shape=jax.ShapeDtypeStruct((N_TOK, D_MODEL), jnp.float32),
        scratch_shapes=[pltpu.VMEM((d_cat, D_MODEL), jnp.bfloat16)] + _ffn_scratch(),
        compiler_params=pltpu.CompilerParams(dimension_semantics=("arbitrary",), vmem_limit_bytes=VMEM_LIMIT),
        name="outproj_ffn",
    )(xp, xs, att_p, att_s, y_p, y_s, z, mod, ssd_norm, w_out, npost, npre, nffn, wg, wu, wd)


def _pool_ffn_kernel(x_ref, xp_ref, xn_ref, mod_ref, nmix_ref, pw_ref, ps_ref,
                     npost_ref, npre_ref, nffn_ref, wg_ref, wu_ref, wd_ref, op_ref, os_ref,
                     wg_scr, wu_scr, wd_scr):
    step = pl.program_id(0)

    @pl.when(step < N_WCHUNK)
    def _stage():
        _stage_weight(step, wg_ref[0], wg_scr)
        _stage_weight(step, wu_ref[0], wu_scr)
        _stage_weight(step, wd_ref[0], wd_scr)

    @pl.when(step == 0)
    def _():
        op_ref[...] = jnp.zeros_like(op_ref)
        os_ref[...] = jnp.zeros_like(os_ref)

    @pl.when(step >= N_WCHUNK)
    def _tile_step():
        _pool_ffn_tile(step - N_WCHUNK, x_ref, xp_ref, xn_ref, mod_ref, nmix_ref, pw_ref, ps_ref,
                       npost_ref, npre_ref, nffn_ref, wg_scr, wu_scr, wd_scr, op_ref, os_ref)


def _pool_ffn_tile(i, x_ref, xp_ref, xn_ref, mod_ref, nmix_ref, pw_ref, ps_ref,
                   npost_ref, npre_ref, nffn_ref, wg_ref, wu_ref, wd_ref, op_ref, os_ref):
    seq = jnp.where(i < N_PROMPT // TM, SEQ, DEC_SEQ)
    pos0 = (i * TM) % seq
    sh = mod_ref[0, :, 0:D_MODEL]
    sc = mod_ref[0, :, D_MODEL:2 * D_MODEL]
    hmod = lambda v: _rms(v, nmix_ref[...]) * (1.0 + sc) + sh
    n_rows = POOL_SUB + 2 * HALO

    def shifted(v, s):
        return pltpu.roll(v, n_rows - s, 0)

    def mixer_pre(s):
        lo, hi = s * POOL_SUB, (s + 1) * POOL_SUB
        pos_s = (pos0 + lo) % seq
        h = hmod(x_ref[lo:hi, :])
        before = hmod(xp_ref[...] if s == 0 else x_ref[lo - HALO:lo, :])
        after = hmod(xn_ref[...] if hi == TM else x_ref[hi:hi + HALO, :])
        before = jnp.where(pos_s > 0, before, 0.0)
        after = jnp.where(pos_s + POOL_SUB < seq, after, 0.0)
        padded = jnp.concatenate([before, h, after], axis=0)
        pos = pos_s + lax.broadcasted_iota(jnp.int32, (POOL_SUB, 1), 0)
        pooled = []
        for gi, w in enumerate(POOL_WINDOWS):
            cols = slice(gi * POOL_GC, (gi + 1) * POOL_GC)
            t = padded[:, cols]
            span = 1
            while span < w:
                t = t + shifted(t, span)
                span *= 2
            lead = HALO - w // 2
            win_sum = (shifted(t, lead) if lead else t)[:POOL_SUB, :]
            cnt = (jnp.minimum(pos + w // 2, seq) - jnp.maximum(pos - w // 2, 0)).astype(jnp.float32)
            pooled.append((win_sum / cnt - h[:, cols]).astype(jnp.bfloat16))
        return pooled

    def mixer_dots(s, pooled):
        mix = jnp.concatenate([_dot(p, pw_ref[gi]) for gi, p in enumerate(pooled)], axis=1) * ps_ref[...]
        return _post_mix(x_ref[s * POOL_SUB:(s + 1) * POOL_SUB, :], mix, mod_ref, npost_ref, npre_ref)

    res = jnp.concatenate(_subtile_pipeline(TM // POOL_SUB, mixer_pre, mixer_dots, mod_ref, nffn_ref,
                                            wg_ref, wu_ref, wd_ref, interleave=True), axis=0)

    @pl.when(i < N_PT)
    def _():
        op_ref[...] = res

    @pl.when(i >= N_PT)
    def _():
        os_ref[...] = res


def _pool_ffn(l, j_c, xa, mod, nmix, pool_w, pool_scale, npost, npre, nffn, wg, wu, wd):
    lead = N_WCHUNK
    hb = TM // HALO
    nh = N_TOK // HALO
    return pl.pallas_call(
        _pool_ffn_kernel,
        grid=(lead + N_TOK // TM,),
        in_specs=[_row_spec(D_MODEL, lead),
                  pl.BlockSpec((HALO, D_MODEL), lambda s: (jnp.maximum(_tile(s, lead) * hb - 1, 0), 0)),
                  pl.BlockSpec((HALO, D_MODEL), lambda s: (jnp.minimum((_tile(s, lead) + 1) * hb, nh - 1), 0)),
                  _mod_spec(l, lead),
                  _vec_spec(D_MODEL, l),
                  _const_spec((len(POOL_WINDOWS), POOL_GC, POOL_GC)),
                  _vec_spec(D_MODEL, j_c)] + _ffn_specs(l),
        out_specs=_group_specs(D_MODEL, lead),
        out_shape=[jax.ShapeDtypeStruct((N_PROMPT, D_MODEL), jnp.float32),
                   jax.ShapeDtypeStruct((N_SAMPLE, D_MODEL), jnp.float32)],
        scratch_shapes=_ffn_scratch(),
        compiler_params=pltpu.CompilerParams(dimension_semantics=("arbitrary",), vmem_limit_bytes=VMEM_LIMIT),
        name="pool_ffn",
    )(xa, xa, xa, mod, nmix, pool_w, pool_scale, npost, npre, nffn, wg, wu, wd)


def _rope_tables():
    f32 = np.float32
    rows = DEC_SEQ // GRID_W
    r = np.repeat(np.arange(rows, dtype=f32), GRID_W)
    c = np.tile(np.arange(GRID_W, dtype=f32), rows)
    half = ROPE_DIM // 2
    inv_freq = np.power(f32(ROPE_THETA), -np.arange(0, half, 2, dtype=f32) / f32(half)).astype(f32)
    ang = np.concatenate([r[:, None] * inv_freq, c[:, None] * inv_freq], axis=-1).astype(f32)
    cos, sin = np.cos(ang).astype(f32), np.sin(ang).astype(f32)
    zl = np.zeros((DEC_SEQ, KPE_LANE0), f32)
    zr = np.zeros((DEC_SEQ, LANES - KPE_LANE0 - ROPE_DIM), f32)
    zh = np.zeros((DEC_SEQ, half), f32)
    cos_t = np.concatenate([zl + 1, cos, cos, zr + 1], axis=1)
    s1_t = np.concatenate([zl, -sin, zh, zr], axis=1)
    s2_t = np.concatenate([zl, zh, sin, zr], axis=1)
    return jnp.asarray(cos_t), jnp.asarray(s1_t), jnp.asarray(s2_t)


def _kpe_slab(k):
    pad = [(0, 0)] * (k.ndim - 1) + [(KPE_LANE0, LANES - KPE_LANE0 - ROPE_DIM)]
    return jnp.pad(k, pad)


def _layout_in_proj(w):
    o = np.cumsum((0, Q_RANK, KV_RANK, ROPE_DIM, D_SSD, CONV_CH, SSD_HEADS, SSD_HEADS))
    cq, ckv, kpe, z, xbc, dtf, dtb = (w[:, o[k]:o[k + 1]] for k in range(7))
    dt = jnp.pad(jnp.concatenate([dtf, dtb], axis=1), ((0, 0), (0, LANES - 2 * SSD_HEADS)))
    return jnp.concatenate([cq, ckv, z, xbc, _kpe_slab(kpe), dt], axis=1).astype(jnp.bfloat16)


def _layout_uq(w):
    w = w.reshape(Q_RANK, MLA_HEADS, NOPE_DIM + ROPE_DIM)
    w = jnp.pad(w, ((0, 0), (0, 0), (0, HEAD_SLAB - NOPE_DIM - ROPE_DIM)))
    return w.reshape(Q_RANK, MLA_HEADS * HEAD_SLAB).astype(jnp.bfloat16)


def _layout_ukv(w):
    w = w.reshape(KV_RANK, MLA_HEADS, NOPE_DIM + V_DIM)
    kn = jnp.pad(w[:, :, :NOPE_DIM], ((0, 0), (0, 0), (0, HEAD_SLAB - NOPE_DIM)))
    v = w[:, :, NOPE_DIM:]
    return kn.reshape(KV_RANK, -1).astype(jnp.bfloat16), v.reshape(KV_RANK, -1).T.astype(jnp.bfloat16)


def _lane_row(fwd, bwd):
    return jnp.pad(jnp.concatenate([fwd, bwd]), (0, LANES - 2 * SSD_HEADS)).reshape(1, LANES)


def kernel(x_prompt, x_sample, c, cache_mla_ckv, cache_mla_krope, state_ssd_fwd, state_ssd_bwd, c_ctx, w_mod, b_mod, norm_pre_mix, norm_post_mix, norm_pre_ffn, norm_post_ffn, w_in_ab, q_norm, w_uq, kv_norm, w_ukv, ssd_conv_w, ssd_conv_b, ssd_dt_bias_fwd, ssd_dt_bias_bwd, ssd_a_log_fwd, ssd_a_log_bwd, ssd_d, ssd_norm, w_out_ab, pool_w, pool_scale, ffn_w_gate, ffn_w_up, ffn_w_down):
    f32, bf16 = jnp.float32, jnp.bfloat16
    assert DEPTH == 2
    xp = x_prompt.reshape(N_PROMPT, D_MODEL)
    xs = x_sample.reshape(N_SAMPLE, D_MODEL)
    cvecs = jnp.concatenate([c_ctx[None, :], c, jnp.zeros((SUBLANES - N_MODVEC, D_MODEL), f32)], axis=0)
    mod = _modulation(cvecs.T, w_mod, b_mod).reshape(DEPTH, SUBLANES, 1, 6 * D_MODEL)
    tabs = _rope_tables()
    hp = SSD_HEADS * SSD_HEAD_DIM
    row = lambda v: v.reshape(1, -1)
    new_ckv, new_kpe, new_hf, new_hb = [], [], [], []

    npre_mix, npost_mix, npre_ffn, npost_ffn = map(_as_rows, (norm_pre_mix, norm_post_mix, norm_pre_ffn, norm_post_ffn))
    ffn = (npost_mix, npre_ffn, npost_ffn, ffn_w_gate, ffn_w_up, ffn_w_down)
    for l in range(DEPTH):
        if l % 2 == 0:
            i = l // 2
            q, ckv_n, kpe, z, xbc, *heads = _inproj(
                l, i, xp, xs, mod, tabs, _layout_in_proj(w_in_ab[i]), _layout_uq(w_uq[i]),
                _as_rows(q_norm), _as_rows(kv_norm), npre_mix,
                _lane_row(ssd_dt_bias_fwd[i], ssd_dt_bias_bwd[i]).T, _lane_row(ssd_a_log_fwd[i], ssd_a_log_bwd[i]).T)
            w_kv = _layout_ukv(w_ukv[i])
            att_p = _attention(q, ckv_n, kpe, w_kv, 0, BATCH, SEQ)
            att_s = _attention(q, ckv_n, kpe, w_kv, N_PROMPT, DEC_BATCH, DEC_SEQ,
                               cache=(cache_mla_ckv[:, i], _kpe_slab(cache_mla_krope[:, i])))
            ssd_args = (ssd_conv_w, _as_rows(ssd_conv_b), row(jnp.repeat(ssd_d[i], SSD_HEAD_DIM)))
            y_p, hf, hb = _ssd(i, xbc, heads, None, *ssd_args, 0, BATCH, SEQ)
            y_s, _, _ = _ssd(i, xbc, heads, (state_ssd_fwd[:, i].reshape(DEC_BATCH, hp, SSD_STATE),
                                             state_ssd_bwd[:, i].reshape(DEC_BATCH, hp, SSD_STATE)),
                             *ssd_args, N_PROMPT, DEC_BATCH, DEC_SEQ)
            xa = _outproj_ffn(l, i, xp, xs, att_p, att_s, y_p, y_s, z, mod, _as_rows(ssd_norm), w_out_ab, *ffn)
            new_ckv.append(ckv_n[:N_PROMPT].reshape(BATCH, SEQ, KV_RANK))
            new_kpe.append(kpe[:N_PROMPT, KPE_LANE0:KPE_LANE0 + ROPE_DIM].reshape(BATCH, SEQ, ROPE_DIM))
            new_hf.append(hf.reshape(BATCH, SSD_HEADS, SSD_HEAD_DIM, SSD_STATE))
            new_hb.append(hb.reshape(BATCH, SSD_HEADS, SSD_HEAD_DIM, SSD_STATE))
        else:
            j = l // 2
            yp, ys = _pool_ffn(l, j, xa, mod, npre_mix, pool_w[j].astype(bf16), _as_rows(pool_scale), *ffn)

    return (yp.reshape(BATCH, SEQ, D_MODEL), ys.reshape(DEC_BATCH, DEC_SEQ, D_MODEL),
            jnp.stack(new_ckv, axis=1), jnp.stack(new_kpe, axis=1),
            jnp.stack(new_hf, axis=1), jnp.stack(new_hb, axis=1))
```

```python
import functools

import numpy as np
import jax
import jax.numpy as jnp
from jax import lax
from jax.experimental import pallas as pl
from jax.experimental.pallas import tpu as pltpu

D_MODEL = 1024
BATCH = 16
SEQ = 256
DEPTH = 2
DEC_BATCH = 2
DEC_SEQ = 2048
PAST_LEN = 256
GRID_W = 64
EPS = 1e-6
MLA_HEADS = 8
Q_RANK = 256
KV_RANK = 256
NOPE_DIM = 64
ROPE_DIM = 32
V_DIM = 64
ROPE_THETA = 10000.0
SSD_HEADS = 8
SSD_GROUPS = 2
SSD_HPG = SSD_HEADS // SSD_GROUPS
SSD_HEAD_DIM = 64
SSD_STATE = 128
D_SSD = SSD_HEADS * SSD_HEAD_DIM
CONV_W = 5
CONV_CH = D_SSD + 2 * SSD_GROUPS * SSD_STATE
POOL_WINDOWS = (2, 4, 8, 16)
POOL_GC = D_MODEL // len(POOL_WINDOWS)
D_FF = ((8 * D_MODEL + 3 * 256 - 1) // (3 * 256)) * 256

SUBLANES = 8
LANES = 128

N_PROMPT = BATCH * SEQ
N_SAMPLE = DEC_BATCH * DEC_SEQ
N_TOK = N_PROMPT + N_SAMPLE
N_MODVEC = 1 + DEC_BATCH
TM = 512
TQ = 256
CHUNK = 128
HALO = SUBLANES
HEAD_SLAB = LANES
IN_COLS = Q_RANK + KV_RANK + D_SSD + CONV_CH + 2 * LANES
KPE_LANE0 = NOPE_DIM
VMEM_LIMIT = 56 * 1024 * 1024

POOL_SUB = min(SEQ, DEC_SEQ)

assert TM % POOL_SUB == 0 and SEQ % POOL_SUB == 0 and DEC_SEQ % TM == 0 and N_PROMPT % DEC_SEQ == 0


def _rms(x, g):
    return x * lax.rsqrt(jnp.mean(x * x, axis=-1, keepdims=True) + EPS) * g


def _silu(x):
    return x * jax.nn.sigmoid(x)


def _softplus(x):
    return jnp.maximum(x, 0.0) + jnp.log1p(jnp.exp(-jnp.abs(x)))


def _dot(a, b):
    return jnp.dot(a, b, preferred_element_type=jnp.float32)


def _dot_nt(a, b):
    return lax.dot_general(a, b, (((1,), (1,)), ((), ())), preferred_element_type=jnp.float32)


def _mod_row(i):
    return jnp.where(i < N_PROMPT // TM, 0, 1 + (i - N_PROMPT // TM) // (DEC_SEQ // TM))


def _const_spec(shape):
    nd = len(shape)
    return pl.BlockSpec(shape, lambda *_: (0,) * nd, pipeline_mode=pl.Buffered(1))


N_PT = N_PROMPT // TM
N_WCHUNK = 16


def _tile(step, lead):
    return jnp.maximum(step - lead, 0)


def _row_spec(width, lead=0):
    return pl.BlockSpec((TM, width), lambda s: (_tile(s, lead), 0))


def _mod_spec(l, lead=0):
    return pl.BlockSpec((None, 1, 1, 6 * D_MODEL), lambda s: (l, _mod_row(_tile(s, lead)), 0, 0))


def _vec_spec(row):
    return pl.BlockSpec((None, 1, D_MODEL), lambda *_: (row, 0, 0), pipeline_mode=pl.Buffered(1))


def _pack_vectors(**params):
    first_row, blocks, n = {}, [], 0
    for name, a in params.items():
        first_row[name] = n
        n += a.shape[0]
        blocks.append(jnp.pad(a, ((0, 0), (0, D_MODEL - a.shape[1]))))
    return jnp.concatenate(blocks, axis=0).reshape(n, 1, D_MODEL), first_row


def _group_specs(width, lead=0):
    return [pl.BlockSpec((TM, width), lambda s: (jnp.minimum(_tile(s, lead), N_PT - 1), 0)),
            pl.BlockSpec((TM, width), lambda s: (jnp.maximum(_tile(s, lead) - N_PT, 0), 0))]


def _wchunk_spec(rows, cols, layer=None):
    ck = rows // N_WCHUNK
    if layer is None:
        return pl.BlockSpec((ck, cols), lambda s: (jnp.minimum(s, N_WCHUNK - 1), 0))
    return pl.BlockSpec((1, ck, cols), lambda s: (layer, jnp.minimum(s, N_WCHUNK - 1), 0))


def _stage_weight(step, chunk, dst_ref):
    ck = chunk.shape[0]
    dst_ref[pl.ds(pl.multiple_of(step * ck, ck), ck), :] = chunk.astype(jnp.bfloat16)


def _pick_group(i, p_ref, s_ref, rows=slice(None)):
    return jnp.where(i < N_PT, p_ref[rows, :], s_ref[rows, :])


MOD_TN = 1024
MOD_STREAMS = 2
SUB_ROWS = 256


def _mod_kernel(ct_ref, *refs):
    w_refs, b_ref, o_ref = refs[:MOD_STREAMS], refs[MOD_STREAMS], refs[MOD_STREAMS + 1]
    s = _silu(ct_ref[...])
    for k, w_ref in enumerate(w_refs):
        cols = slice(k * MOD_TN, (k + 1) * MOD_TN)
        w = w_ref[0]
        b = b_ref[0, :, cols]
        rows = [jnp.sum(s[:, v:v + 1] * w, axis=0, keepdims=True) + b for v in range(N_MODVEC)]
        rows.append(jnp.zeros((SUBLANES - N_MODVEC, MOD_TN), jnp.float32))
        o_ref[0, :, cols] = jnp.concatenate(rows, axis=0)


def _modulation(cvecs_t, w_mod, b_mod):
    step_cols = MOD_STREAMS * MOD_TN
    nt = 6 * D_MODEL // step_cols
    w_spec = lambda k: pl.BlockSpec((1, D_MODEL, MOD_TN), lambda l, j: (l, 0, j * MOD_STREAMS + k))
    return pl.pallas_call(
        _mod_kernel,
        grid=(DEPTH, nt),
        in_specs=[pl.BlockSpec((D_MODEL, SUBLANES), lambda l, j: (0, 0))]
        + [w_spec(k) for k in range(MOD_STREAMS)]
        + [pl.BlockSpec((1, 1, step_cols), lambda l, j: (l, 0, j))],
        out_specs=pl.BlockSpec((1, SUBLANES, step_cols), lambda l, j: (l, 0, j)),
        out_shape=jax.ShapeDtypeStruct((DEPTH, SUBLANES, 6 * D_MODEL), jnp.float32),
        compiler_params=pltpu.CompilerParams(dimension_semantics=("arbitrary", "arbitrary"),
                                             vmem_limit_bytes=VMEM_LIMIT),
        name="modulation",
    )(cvecs_t, *([w_mod] * MOD_STREAMS), b_mod.reshape(DEPTH, 1, 6 * D_MODEL))


def _split3(a):
    a_hi = a.astype(jnp.bfloat16)
    r1 = a - a_hi.astype(jnp.float32)
    a_mid = r1.astype(jnp.bfloat16)
    a_lo = (r1 - a_mid.astype(jnp.float32)).astype(jnp.bfloat16)
    return a_hi, a_mid, a_lo


def _ssd_head_terms(dt_raw, dtb_c, a_neg_c):
    nh2 = 2 * SSD_HEADS
    row = lax.broadcasted_iota(jnp.int32, (CHUNK, CHUNK), 0)
    col = lax.broadcasted_iota(jnp.int32, (CHUNK, CHUNK), 1)
    upper_b = (row <= col).astype(jnp.bfloat16)
    lower_b = (row >= col).astype(jnp.bfloat16)
    fwd_rows = lax.broadcasted_iota(jnp.int32, (nh2, CHUNK), 0) < SSD_HEADS
    dts_t = _softplus(dt_raw.T[:nh2, :] + dtb_c)
    pieces = _split3(dts_t * a_neg_c)
    cum_t = jnp.where(fwd_rows, sum(_dot(p, upper_b) for p in pieces), sum(_dot(p, lower_b) for p in pieces))
    cum_t = cum_t * np.float32(np.log2(np.e))
    tot = jnp.where(fwd_rows[:, :1], cum_t[:, CHUNK - 1:], cum_t[:, :1])
    return dts_t, cum_t, dts_t * jnp.exp2(tot - cum_t)


def _inproj_kernel(xp_ref, xs_ref, mod_ref, cos_ref, s1_ref, s2_ref, w_in_ref, w_uq_ref, qn_ref, kvn_ref, npm_ref,
                   dtbc_ref, alogc_ref, q_ref, ckv_ref, kpe_ref, z_ref, xbc_ref, dts_ref, cumt_ref, wt_ref,
                   new_ckv_ref, new_kpe_ref):
    i = pl.program_id(0)
    is_latent = i >= N_PT
    sh = mod_ref[0, :, 0:D_MODEL]
    sc = mod_ref[0, :, D_MODEL:2 * D_MODEL]
    scale = (NOPE_DIM + ROPE_DIM) ** -0.5 * np.log2(np.e)
    nh2 = 2 * SSD_HEADS
    dtb_c = dtbc_ref[:nh2, :]
    a_neg_c = -jnp.exp(alogc_ref[:nh2, :])
    dt_raw = []
    for r in range(TM // SUB_ROWS):
        rs = slice(r * SUB_ROWS, (r + 1) * SUB_ROWS)
        h = (_rms(_pick_group(i, xp_ref, xs_ref, rs), npm_ref[...]) * (1.0 + sc) + sh).astype(jnp.bfloat16)
        p = _dot(h, w_in_ref[...])
        o = 0
        cq = p[:, o:o + Q_RANK]; o += Q_RANK
        ckv = p[:, o:o + KV_RANK]; o += KV_RANK
        z_ref[rs, :] = p[:, o:o + D_SSD]; o += D_SSD
        xbc_ref[rs, :] = p[:, o:o + CONV_CH]; o += CONV_CH
        kpe = p[:, o:o + LANES]; o += LANES
        dt_raw.append(p[:, o:o + LANES])

        cos = cos_ref[rs, :]
        s1 = s1_ref[rs, :]
        s2 = s2_ref[rs, :]

        def rope(slab):
            rot = (slab * cos + pltpu.roll(slab, LANES - ROPE_DIM // 2, 1) * s1
                   + pltpu.roll(slab, ROPE_DIM // 2, 1) * s2)
            return jnp.where(is_latent, rot, slab)

        ckv_ref[rs, :] = _rms(ckv, kvn_ref[:, :KV_RANK])
        kpe_ref[rs, :] = rope(kpe)
        q = _dot(_rms(cq, qn_ref[:, :Q_RANK]).astype(jnp.bfloat16), w_uq_ref[...]) * scale
        for hd in range(MLA_HEADS):
            sl = slice(hd * HEAD_SLAB, (hd + 1) * HEAD_SLAB)
            q_ref[rs, sl] = rope(q[:, sl]).astype(jnp.bfloat16)

    dt_all = jnp.concatenate(dt_raw, axis=0)
    for ck in range(TM // CHUNK):
        terms = _ssd_head_terms(dt_all[ck * CHUNK:(ck + 1) * CHUNK, :], dtb_c, a_neg_c)
        for ref, val in zip((dts_ref, cumt_ref, wt_ref), terms):
            ref[ck * nh2:(ck + 1) * nh2, :] = val

    @pl.when(i < N_PT)
    def _():
        new_ckv_ref[...] = ckv_ref[...]
        new_kpe_ref[...] = kpe_ref[:, KPE_LANE0:KPE_LANE0 + ROPE_DIM]


def _inproj(l, vec_table, vec_rows, xp, xs, mod, tabs, w_in, w_uq, dt_bias_c, a_log_c):
    nt = N_TOK // TM
    prompt_blk = lambda i: (jnp.minimum(i, N_PT - 1), 0)
    tab_spec = pl.BlockSpec((TM, LANES), lambda i: (jnp.maximum(i - N_PT, 0) % (DEC_SEQ // TM), 0))
    row = _row_spec
    hrows = TM // CHUNK * 2 * SSD_HEADS
    head_spec = pl.BlockSpec((hrows, CHUNK), lambda i: (i, 0))
    head_shape = jax.ShapeDtypeStruct((nt * hrows, CHUNK), jnp.float32)
    return pl.pallas_call(
        _inproj_kernel,
        grid=(nt,),
        in_specs=_group_specs(D_MODEL) + [
            _mod_spec(l),
            tab_spec, tab_spec, tab_spec,
            _const_spec((D_MODEL, IN_COLS)),
            _const_spec((Q_RANK, MLA_HEADS * HEAD_SLAB)),
            _vec_spec(vec_rows["q_norm"] + l // 2),
            _vec_spec(vec_rows["kv_norm"] + l // 2),
            _vec_spec(vec_rows["norm_pre_mix"] + l),
            _const_spec((LANES, 1)), _const_spec((LANES, 1)),
        ],
        out_specs=[row(MLA_HEADS * HEAD_SLAB), row(KV_RANK), row(LANES), row(D_SSD), row(CONV_CH),
                   head_spec, head_spec, head_spec,
                   pl.BlockSpec((TM, KV_RANK), prompt_blk), pl.BlockSpec((TM, ROPE_DIM), prompt_blk)],
        out_shape=[
            jax.ShapeDtypeStruct((N_TOK, MLA_HEADS * HEAD_SLAB), jnp.bfloat16),
            jax.ShapeDtypeStruct((N_TOK, KV_RANK), jnp.float32),
            jax.ShapeDtypeStruct((N_TOK, LANES), jnp.float32),
            jax.ShapeDtypeStruct((N_TOK, D_SSD), jnp.float32),
            jax.ShapeDtypeStruct((N_TOK, CONV_CH), jnp.float32),
            head_shape, head_shape, head_shape,
            jax.ShapeDtypeStruct((N_PROMPT, KV_RANK), jnp.float32),
            jax.ShapeDtypeStruct((N_PROMPT, ROPE_DIM), jnp.float32),
        ],
        compiler_params=pltpu.CompilerParams(dimension_semantics=("arbitrary",), vmem_limit_bytes=VMEM_LIMIT),
        name="inproj",
    )(xp, xs, mod, *tabs, w_in, w_uq, vec_table, vec_table, vec_table, dt_bias_c, a_log_c)


def _attn_kernel(*refs, lk_cache, lk_new):
    if lk_cache:
        q_ref, ckv_ref, kpe_ref, ckvc_ref, kpec_ref, wk_ref, wvt_ref, o_ref, k_scr, vt_scr = refs
    else:
        q_ref, ckv_ref, kpe_ref, wk_ref, wvt_ref, o_ref, k_scr, vt_scr = refs

    @pl.when(pl.program_id(1) == 0)
    def _expand_kv():
        def expand(ckv, kpe, r0):
            ckv_b = ckv.astype(jnp.bfloat16)
            kn = _dot(ckv_b, wk_ref[...])
            rows = slice(r0, r0 + ckv.shape[0])
            for hd in range(MLA_HEADS):
                k_scr[hd, rows, :] = (kn[:, hd * HEAD_SLAB:(hd + 1) * HEAD_SLAB] + kpe).astype(jnp.bfloat16)
            vt_scr[:, rows] = _dot_nt(wvt_ref[...], ckv_b).astype(jnp.bfloat16)

        step = 256
        for r0 in range(0, lk_cache, step):
            expand(ckvc_ref[0, r0:r0 + step, :], kpec_ref[0, r0:r0 + step, :], r0)
        for r0 in range(0, lk_new, step):
            expand(ckv_ref[r0:r0 + step, :], kpe_ref[r0:r0 + step, :], lk_cache + r0)

    scores = [_dot_nt(k_scr[hd], q_ref[:, hd * HEAD_SLAB:(hd + 1) * HEAD_SLAB]) for hd in range(MLA_HEADS)]
    outs = []
    for hd, s_t in enumerate(scores):
        p_t = jnp.exp2(s_t - jnp.max(s_t, axis=0, keepdims=True))
        den = jnp.sum(p_t, axis=0, keepdims=True)
        outs.append(_dot(vt_scr[hd * V_DIM:(hd + 1) * V_DIM, :], p_t.astype(jnp.bfloat16)) / den)
    o_ref[...] = jnp.concatenate(outs, axis=0).T.astype(jnp.bfloat16)


def _attention(q, ckv_n, kpe, w_ukv, row_off, n_batch, seq, cache=None):
    nq = seq // TQ
    lk_cache = 0 if cache is None else cache[0].shape[1]
    lk = lk_cache + seq
    qblk = lambda b, qi: (row_off // TQ + b * nq + qi, 0)
    sblk = lambda b, qi: (row_off // seq + b, 0)
    in_specs = [
        pl.BlockSpec((TQ, MLA_HEADS * HEAD_SLAB), qblk),
        pl.BlockSpec((seq, KV_RANK), sblk),
        pl.BlockSpec((seq, LANES), sblk),
    ]
    args = [q, ckv_n, kpe]
    if cache is not None:
        in_specs += [pl.BlockSpec((1, lk_cache, KV_RANK), lambda b, qi: (b, 0, 0)),
                     pl.BlockSpec((1, lk_cache, LANES), lambda b, qi: (b, 0, 0))]
        args += list(cache)
    in_specs += [_const_spec(w.shape) for w in w_ukv]
    args += list(w_ukv)
    return pl.pallas_call(
        functools.partial(_attn_kernel, lk_cache=lk_cache, lk_new=seq),
        grid=(n_batch, nq),
        in_specs=in_specs,
        out_specs=pl.BlockSpec((TQ, MLA_HEADS * V_DIM), lambda b, qi: (b * nq + qi, 0)),
        out_shape=jax.ShapeDtypeStruct((n_batch * seq, MLA_HEADS * V_DIM), jnp.bfloat16),
        scratch_shapes=[pltpu.VMEM((MLA_HEADS, lk, HEAD_SLAB), jnp.bfloat16),
                        pltpu.VMEM((MLA_HEADS * V_DIM, lk), jnp.bfloat16)],
        compiler_params=pltpu.CompilerParams(dimension_semantics=("arbitrary", "arbitrary"),
                                             vmem_limit_bytes=VMEM_LIMIT),
        name=f"attention_{seq}",
    )(*args)


def _ssd_kernel(*refs, seq, zero_init):
    if zero_init:
        (xbc_ref, dts_ref, cumt_ref, wt_ref, cw_ref, cb_ref, dsk_ref,
         y_ref, hf_ref, hb_ref, xs_scr, c_scr, bt_scr, cum_scr, stf_scr, stb_scr) = refs
    else:
        (xbc_ref, dts_ref, cumt_ref, wt_ref, h0f_ref, h0b_ref, cw_ref, cb_ref, dsk_ref,
         y_ref, hf_ref, hb_ref, xs_scr, c_scr, bt_scr, cum_scr, stf_scr, stb_scr) = refs
    nc = seq // CHUNK
    gs = SSD_GROUPS * SSD_STATE
    nh2 = 2 * SSD_HEADS

    row = lax.broadcasted_iota(jnp.int32, (CHUNK, CHUNK), 0)
    col = lax.broadcasted_iota(jnp.int32, (CHUNK, CHUNK), 1)
    low_half = col < SSD_HEAD_DIM
    lower = row >= col
    upper = row <= col

    def prep_chunk(c, carry):
        r0 = pl.multiple_of(c * CHUNK, CHUNK)
        rows = pl.ds(r0, CHUNK)
        rows_prev = pl.ds(pl.multiple_of(jnp.maximum(r0 - HALO, 0), HALO), HALO)
        rows_next = pl.ds(pl.multiple_of(jnp.minimum(r0 + CHUNK, seq - HALO), HALO), HALO)

        def conv_tile(cs):
            prev = jnp.where(c > 0, xbc_ref[rows_prev, cs], 0.0)
            nxt = jnp.where(c < nc - 1, xbc_ref[rows_next, cs], 0.0)
            win = jnp.concatenate([prev, xbc_ref[rows, cs], nxt], axis=0)
            acc = jnp.broadcast_to(cb_ref[:, cs], (CHUNK, LANES))
            for k in range(CONV_W):
                lo = HALO - CONV_W // 2 + k
                acc = acc + cw_ref[k:k + 1, cs] * win[lo:lo + CHUNK, :]
            return _silu(acc)

        def x_tile(j, carry):
            cs = pl.ds(pl.multiple_of(j * LANES, LANES), LANES)
            u = conv_tile(cs)
            y_ref[rows, cs] = dsk_ref[:, cs] * u
            xs_scr[rows, cs] = u.astype(jnp.bfloat16)
            return carry

        lax.fori_loop(0, D_SSD // LANES, x_tile, 0)
        cum_t = cumt_ref[pl.ds(pl.multiple_of(c * nh2, nh2), nh2), :]
        cum_scr[rows, :] = jnp.concatenate(
            [cum_t, jnp.zeros((CHUNK - nh2, CHUNK), jnp.float32)], axis=0).T[:, :nh2]
        for g in range(SSD_GROUPS):
            b0 = pl.multiple_of(c * gs + g * SSD_STATE, SSD_STATE)
            bt_scr[pl.ds(b0, SSD_STATE), :] = conv_tile(slice(D_SSD + g * SSD_STATE, D_SSD + (g + 1) * SSD_STATE)).T
            c_scr[rows, g * SSD_STATE:(g + 1) * SSD_STATE] = conv_tile(
                slice(D_SSD + gs + g * SSD_STATE, D_SSD + gs + (g + 1) * SSD_STATE)).astype(jnp.bfloat16)
        return carry

    lax.fori_loop(0, nc, prep_chunk, 0)

    if zero_init:
        stf_scr[...] = jnp.zeros_like(stf_scr)
        stb_scr[...] = jnp.zeros_like(stb_scr)
    else:
        stf_scr[...] = h0f_ref[0].T
        stb_scr[...] = h0b_ref[0].T

    def scan_open(ci, st_scr):
        rows = pl.ds(pl.multiple_of(ci * CHUNK, CHUNK), CHUNK)
        c_b = c_scr[rows, :]
        st = st_scr[...]
        bts, cbms, zs = [], [], []
        for g in range(SSD_GROUPS):
            cg = c_b[:, g * SSD_STATE:(g + 1) * SSD_STATE]
            bt = bt_scr[pl.ds(pl.multiple_of(ci * gs + g * SSD_STATE, SSD_STATE), SSD_STATE), :]
            gcols = slice(g * SSD_HPG * SSD_HEAD_DIM, (g + 1) * SSD_HPG * SSD_HEAD_DIM)
            bts.append(bt)
            cbms.append(_dot(cg, bt.astype(jnp.bfloat16)))
            zs.append(_dot(cg, st[:, gcols].astype(jnp.bfloat16)))
        return st, bts, cbms, zs

    def scan_pairs(ci, st_scr, reverse, opened):
        st, bts, cbms, zs = opened
        lane0 = SSD_HEADS if reverse else 0
        causal = upper if reverse else lower
        last = 0 if reverse else CHUNK - 1
        rows = pl.ds(pl.multiple_of(ci * CHUNK, CHUNK), CHUNK)
        hrows = pl.ds(pl.multiple_of(ci * nh2, nh2), nh2)
        xs_b = xs_scr[rows, :]
        dts_t = dts_ref[hrows, :]
        cum_t = cumt_ref[hrows, :]
        w_t = wt_ref[hrows, :]
        cum = cum_scr[rows, :]
        for pair in range(SSD_HEADS // 2):
            g, jj = divmod(pair, SSD_HPG // 2)
            pcols = slice(pair * LANES, (pair + 1) * LANES)
            lhs_y, lhs_s, entry = [], [], []
            for hd in (2 * pair, 2 * pair + 1):
                ln = lane0 + hd
                cum_i = jnp.broadcast_to(cum[:, ln:ln + 1], (CHUNK, CHUNK))
                dec = jnp.exp2(jnp.where(causal, cum_i - cum_t[ln:ln + 1, :], -jnp.inf))
                lhs_y.append((cbms[g] * dec * dts_t[ln:ln + 1, :]).astype(jnp.bfloat16))
                lhs_s.append((bts[g] * w_t[ln:ln + 1, :]).astype(jnp.bfloat16))
                entry.append(jnp.exp2(cum_i))
            out = _dot(jnp.concatenate(lhs_y + lhs_s, axis=0), xs_b[:, pcols])
            ea = jnp.where(low_half, entry[0], entry[1])
            y_ref[rows, pcols] += (jnp.where(low_half, out[:CHUNK], out[CHUNK:2 * CHUNK])
                                   + zs[g][:, jj * LANES:(jj + 1) * LANES] * ea)
            st_scr[:, pcols] = (ea[last:last + 1, :] * st[:, pcols]
                                + jnp.where(low_half, out[2 * CHUNK:3 * CHUNK], out[3 * CHUNK:]))

    def both(c, carry):
        opened_f = scan_open(c, stf_scr)
        opened_b = scan_open(nc - 1 - c, stb_scr)
        scan_pairs(c, stf_scr, False, opened_f)
        scan_pairs(nc - 1 - c, stb_scr, True, opened_b)
        return carry

    lax.fori_loop(0, nc, both, 0)
    hf_ref[0] = stf_scr[...].T
    hb_ref[0] = stb_scr[...].T


def _ssd(i_ab, xbc, head_terms, h0, conv_w, vec_table, conv_b_row, d_skip, row_off, n_batch, seq):
    assert CONV_CH == D_MODEL
    hp = SSD_HEADS * SSD_HEAD_DIM
    gs = SSD_GROUPS * SSD_STATE
    nc = seq // CHUNK
    seq_blk = lambda b: (row_off // seq + b, 0)
    head_spec = pl.BlockSpec((nc * 2 * SSD_HEADS, CHUNK), seq_blk)
    st_spec = pl.BlockSpec((1, hp, SSD_STATE), lambda b: (b, 0, 0))
    st_shape = jax.ShapeDtypeStruct((n_batch, hp, SSD_STATE), jnp.float32)
    h0 = () if h0 is None else tuple(h0)
    return pl.pallas_call(
        functools.partial(_ssd_kernel, seq=seq, zero_init=not h0),
        grid=(n_batch,),
        in_specs=[pl.BlockSpec((seq, CONV_CH), seq_blk), head_spec, head_spec, head_spec] + [st_spec] * len(h0) + [
            pl.BlockSpec((None, CONV_W, CONV_CH), lambda b: (i_ab, 0, 0), pipeline_mode=pl.Buffered(1)),
            _vec_spec(conv_b_row + i_ab), _const_spec((1, D_SSD))],
        out_specs=[pl.BlockSpec((seq, D_SSD), lambda b: (b, 0)), st_spec, st_spec],
        out_shape=[jax.ShapeDtypeStruct((n_batch * seq, D_SSD), jnp.float32), st_shape, st_shape],
        scratch_shapes=[pltpu.VMEM((seq, D_SSD), jnp.bfloat16),
                        pltpu.VMEM((seq, gs), jnp.bfloat16),
                        pltpu.VMEM((nc * gs, CHUNK), jnp.float32),
                        pltpu.VMEM((seq, 2 * SSD_HEADS), jnp.float32),
                        pltpu.VMEM((SSD_STATE, hp), jnp.float32),
                        pltpu.VMEM((SSD_STATE, hp), jnp.float32)],
        compiler_params=pltpu.CompilerParams(dimension_semantics=("arbitrary",), vmem_limit_bytes=VMEM_LIMIT),
        name=f"ssd_{seq}",
    )(xbc, *head_terms, *h0, conv_w, vec_table, d_skip)


def _post_mix(x, mix, mod_ref, npost_ref, npre_ref):
    d = D_MODEL
    gate_mix = mod_ref[0, :, 2 * d:3 * d]
    shf = mod_ref[0, :, 3 * d:4 * d]
    scf = mod_ref[0, :, 4 * d:5 * d]
    x1 = x + gate_mix * _rms(mix, npost_ref[...])
    return x1, (_rms(x1, npre_ref[...]) * (1.0 + scf) + shf).astype(jnp.bfloat16)


def _subtile_pipeline(n_sub, mixer_pre, mixer_dots, mod_ref, nffn_ref, wg_ref, wu_ref, wd_ref, interleave):
    gate_ffn = mod_ref[0, :, 5 * D_MODEL:6 * D_MODEL]
    ffn_up = lambda h: (_silu(_dot(h, wg_ref[...])) * _dot(h, wu_ref[...])).astype(jnp.bfloat16)
    ffn_down = lambda x1, hid: x1 + gate_ffn * _rms(_dot(hid, wd_ref[...]), nffn_ref[...])
    if not interleave:
        staged = [mixer_dots(r, mixer_pre(r)) for r in range(n_sub)]
        return [ffn_down(x1, ffn_up(h)) for x1, h in staged]
    outs = []
    x1, h = mixer_dots(0, mixer_pre(0))
    for r in range(n_sub):
        nxt_pre = mixer_pre(r + 1) if r + 1 < n_sub else None
        hid = ffn_up(h)
        nxt = mixer_dots(r + 1, nxt_pre) if r + 1 < n_sub else None
        outs.append(ffn_down(x1, hid))
        if nxt is not None:
            x1, h = nxt
    return outs


def _ffn_specs(l, vec_rows):
    return [_vec_spec(vec_rows[name] + l) for name in ("norm_post_mix", "norm_pre_ffn", "norm_post_ffn")] + [
        _wchunk_spec(D_MODEL, D_FF, l), _wchunk_spec(D_MODEL, D_FF, l), _wchunk_spec(D_FF, D_MODEL, l)]


def _ffn_scratch():
    return [pltpu.VMEM((D_MODEL, D_FF), jnp.bfloat16), pltpu.VMEM((D_MODEL, D_FF), jnp.bfloat16),
            pltpu.VMEM((D_FF, D_MODEL), jnp.bfloat16)]


def _outproj_ffn_kernel(xp_ref, xs_ref, attp_ref, atts_ref, yp_ref, ys_ref, z_ref, mod_ref, sn_ref, wo_ref,
                        npost_ref, npre_ref, nffn_ref, wg_ref, wu_ref, wd_ref, o_ref,
                        wo_scr, wg_scr, wu_scr, wd_scr):
    step = pl.program_id(0)

    @pl.when(step < N_WCHUNK)
    def _stage():
        _stage_weight(step, wo_ref[0], wo_scr)
        _stage_weight(step, wg_ref[0], wg_scr)
        _stage_weight(step, wu_ref[0], wu_scr)
        _stage_weight(step, wd_ref[0], wd_scr)

    @pl.when(step == 0)
    def _():
        o_ref[...] = jnp.zeros_like(o_ref)

    @pl.when(step >= N_WCHUNK)
    def _tile_step():
        i = step - N_WCHUNK
        gw = D_SSD // SSD_GROUPS
        sub = lambda r: slice(r * SUB_ROWS, (r + 1) * SUB_ROWS)

        def mixer_pre(r):
            yg = _pick_group(i, yp_ref, ys_ref, sub(r)) * _silu(z_ref[sub(r), :])
            parts = [_pick_group(i, attp_ref, atts_ref, sub(r))]
            for g in range(SSD_GROUPS):
                parts.append(_rms(yg[:, g * gw:(g + 1) * gw], sn_ref[:, g * gw:(g + 1) * gw]).astype(jnp.bfloat16))
            return jnp.concatenate(parts, axis=1)

        def mixer_dots(r, cat):
            return _post_mix(_pick_group(i, xp_ref, xs_ref, sub(r)), _dot(cat, wo_scr[...]),
                             mod_ref, npost_ref, npre_ref)

        outs = _subtile_pipeline(TM // SUB_ROWS, mixer_pre, mixer_dots, mod_ref, nffn_ref, wg_scr, wu_scr, wd_scr,
                                 interleave=False)
        for r, res in enumerate(outs):
            o_ref[sub(r), :] = res


def _outproj_ffn(l, i_ab, vec_table, vec_rows, xp, xs, att_p, att_s, y_p, y_s, z, mod, w_out, wg, wu, wd):
    lead = N_WCHUNK
    d_cat = MLA_HEADS * V_DIM + D_SSD
    return pl.pallas_call(
        _outproj_ffn_kernel,
        grid=(lead + N_TOK // TM,),
        in_specs=(_group_specs(D_MODEL, lead) + _group_specs(MLA_HEADS * V_DIM, lead) + _group_specs(D_SSD, lead)
                  + [_row_spec(D_SSD, lead), _mod_spec(l, lead), _vec_spec(vec_rows["ssd_norm"] + i_ab),
                     _wchunk_spec(d_cat, D_MODEL, i_ab)] + _ffn_specs(l, vec_rows)),
        out_specs=_row_spec(D_MODEL, lead),
        out_shape=jax.ShapeDtypeStruct((N_TOK, D_MODEL), jnp.float32),
        scratch_shapes=[pltpu.VMEM((d_cat, D_MODEL), jnp.bfloat16)] + _ffn_scratch(),
        compiler_params=pltpu.CompilerParams(dimension_semantics=("arbitrary",), vmem_limit_bytes=VMEM_LIMIT),
        name="outproj_ffn",
    )(xp, xs, att_p, att_s, y_p, y_s, z, mod, vec_table, w_out, vec_table, vec_table, vec_table, wg, wu, wd)


def _pool_ffn_kernel(x_ref, xp_ref, xn_ref, mod_ref, nmix_ref, pw_ref, ps_ref,
                     npost_ref, npre_ref, nffn_ref, wg_ref, wu_ref, wd_ref, op_ref, os_ref,
                     wg_scr, wu_scr, wd_scr):
    step = pl.program_id(0)

    @pl.when(step < N_WCHUNK)
    def _stage():
        _stage_weight(step, wg_ref[0], wg_scr)
        _stage_weight(step, wu_ref[0], wu_scr)
        _stage_weight(step, wd_ref[0], wd_scr)

    @pl.when(step == 0)
    def _():
        op_ref[...] = jnp.zeros_like(op_ref)
        os_ref[...] = jnp.zeros_like(os_ref)

    @pl.when(step >= N_WCHUNK)
    def _tile_step():
        _pool_ffn_tile(step - N_WCHUNK, x_ref, xp_ref, xn_ref, mod_ref, nmix_ref, pw_ref, ps_ref,
                       npost_ref, npre_ref, nffn_ref, wg_scr, wu_scr, wd_scr, op_ref, os_ref)


def _pool_ffn_tile(i, x_ref, xp_ref, xn_ref, mod_ref, nmix_ref, pw_ref, ps_ref,
                   npost_ref, npre_ref, nffn_ref, wg_ref, wu_ref, wd_ref, op_ref, os_ref):
    seq = jnp.where(i < N_PROMPT // TM, SEQ, DEC_SEQ)
    pos0 = (i * TM) % seq
    sh = mod_ref[0, :, 0:D_MODEL]
    sc = mod_ref[0, :, D_MODEL:2 * D_MODEL]
    hmod = lambda v: _rms(v, nmix_ref[...]) * (1.0 + sc) + sh
    n_rows = POOL_SUB + 2 * HALO

    def shifted(v, s):
        return pltpu.roll(v, n_rows - s, 0)

    def mixer_pre(s):
        lo, hi = s * POOL_SUB, (s + 1) * POOL_SUB
        pos_s = (pos0 + lo) % seq
        h = hmod(x_ref[lo:hi, :])
        before = hmod(xp_ref[...] if s == 0 else x_ref[lo - HALO:lo, :])
        after = hmod(xn_ref[...] if hi == TM else x_ref[hi:hi + HALO, :])
        before = jnp.where(pos_s > 0, before, 0.0)
        after = jnp.where(pos_s + POOL_SUB < seq, after, 0.0)
        padded = jnp.concatenate([before, h, after], axis=0)
        pos = pos_s + lax.broadcasted_iota(jnp.int32, (POOL_SUB, 1), 0)
        pooled = []
        for gi, w in enumerate(POOL_WINDOWS):
            cols = slice(gi * POOL_GC, (gi + 1) * POOL_GC)
            t = padded[:, cols]
            span = 1
            while span < w:
                t = t + shifted(t, span)
                span *= 2
            lead = HALO - w // 2
            win_sum = (shifted(t, lead) if lead else t)[:POOL_SUB, :]
            cnt = (jnp.minimum(pos + w // 2, seq) - jnp.maximum(pos - w // 2, 0)).astype(jnp.float32)
            pooled.append((win_sum / cnt - h[:, cols]).astype(jnp.bfloat16))
        return pooled

    def mixer_dots(s, pooled):
        mix = jnp.concatenate([_dot(p, pw_ref[gi]) for gi, p in enumerate(pooled)], axis=1) * ps_ref[...]
        return _post_mix(x_ref[s * POOL_SUB:(s + 1) * POOL_SUB, :], mix, mod_ref, npost_ref, npre_ref)

    res = jnp.concatenate(_subtile_pipeline(TM // POOL_SUB, mixer_pre, mixer_dots, mod_ref, nffn_ref,
                                            wg_ref, wu_ref, wd_ref, interleave=True), axis=0)

    @pl.when(i < N_PT)
    def _():
        op_ref[...] = res

    @pl.when(i >= N_PT)
    def _():
        os_ref[...] = res


def _pool_ffn(l, j_c, vec_table, vec_rows, xa, mod, pool_w, wg, wu, wd):
    lead = N_WCHUNK
    hb = TM // HALO
    nh = N_TOK // HALO
    return pl.pallas_call(
        _pool_ffn_kernel,
        grid=(lead + N_TOK // TM,),
        in_specs=[_row_spec(D_MODEL, lead),
                  pl.BlockSpec((HALO, D_MODEL), lambda s: (jnp.maximum(_tile(s, lead) * hb - 1, 0), 0)),
                  pl.BlockSpec((HALO, D_MODEL), lambda s: (jnp.minimum((_tile(s, lead) + 1) * hb, nh - 1), 0)),
                  _mod_spec(l, lead),
                  _vec_spec(vec_rows["norm_pre_mix"] + l),
                  _const_spec((len(POOL_WINDOWS), POOL_GC, POOL_GC)),
                  _vec_spec(vec_rows["pool_scale"] + j_c)] + _ffn_specs(l, vec_rows),
        out_specs=_group_specs(D_MODEL, lead),
        out_shape=[jax.ShapeDtypeStruct((N_PROMPT, D_MODEL), jnp.float32),
                   jax.ShapeDtypeStruct((N_SAMPLE, D_MODEL), jnp.float32)],
        scratch_shapes=_ffn_scratch(),
        compiler_params=pltpu.CompilerParams(dimension_semantics=("arbitrary",), vmem_limit_bytes=VMEM_LIMIT),
        name="pool_ffn",
    )(xa, xa, xa, mod, vec_table, pool_w, vec_table, vec_table, vec_table, vec_table, wg, wu, wd)


def _rope_tables():
    f32 = np.float32
    rows = DEC_SEQ // GRID_W
    r = np.repeat(np.arange(rows, dtype=f32), GRID_W)
    c = np.tile(np.arange(GRID_W, dtype=f32), rows)
    half = ROPE_DIM // 2
    inv_freq = np.power(f32(ROPE_THETA), -np.arange(0, half, 2, dtype=f32) / f32(half)).astype(f32)
    ang = np.concatenate([r[:, None] * inv_freq, c[:, None] * inv_freq], axis=-1).astype(f32)
    cos, sin = np.cos(ang).astype(f32), np.sin(ang).astype(f32)
    zl = np.zeros((DEC_SEQ, KPE_LANE0), f32)
    zr = np.zeros((DEC_SEQ, LANES - KPE_LANE0 - ROPE_DIM), f32)
    zh = np.zeros((DEC_SEQ, half), f32)
    cos_t = np.concatenate([zl + 1, cos, cos, zr + 1], axis=1)
    s1_t = np.concatenate([zl, -sin, zh, zr], axis=1)
    s2_t = np.concatenate([zl, zh, sin, zr], axis=1)
    return jnp.asarray(cos_t), jnp.asarray(s1_t), jnp.asarray(s2_t)


def _kpe_slab(k):
    pad = [(0, 0)] * (k.ndim - 1) + [(KPE_LANE0, LANES - KPE_LANE0 - ROPE_DIM)]
    return jnp.pad(k, pad)


def _layout_in_proj(w):
    o = np.cumsum((0, Q_RANK, KV_RANK, ROPE_DIM, D_SSD, CONV_CH, SSD_HEADS, SSD_HEADS))
    cq, ckv, kpe, z, xbc, dtf, dtb = (w[:, o[k]:o[k + 1]] for k in range(7))
    dt = jnp.pad(jnp.concatenate([dtf, dtb], axis=1), ((0, 0), (0, LANES - 2 * SSD_HEADS)))
    return jnp.concatenate([cq, ckv, z, xbc, _kpe_slab(kpe), dt], axis=1).astype(jnp.bfloat16)


def _layout_uq(w):
    w = w.reshape(Q_RANK, MLA_HEADS, NOPE_DIM + ROPE_DIM)
    w = jnp.pad(w, ((0, 0), (0, 0), (0, HEAD_SLAB - NOPE_DIM - ROPE_DIM)))
    return w.reshape(Q_RANK, MLA_HEADS * HEAD_SLAB).astype(jnp.bfloat16)


def _layout_ukv(w):
    w = w.reshape(KV_RANK, MLA_HEADS, NOPE_DIM + V_DIM)
    kn = jnp.pad(w[:, :, :NOPE_DIM], ((0, 0), (0, 0), (0, HEAD_SLAB - NOPE_DIM)))
    v = w[:, :, NOPE_DIM:]
    return kn.reshape(KV_RANK, -1).astype(jnp.bfloat16), v.reshape(KV_RANK, -1).T.astype(jnp.bfloat16)


def _lane_row(fwd, bwd):
    return jnp.pad(jnp.concatenate([fwd, bwd]), (0, LANES - 2 * SSD_HEADS)).reshape(1, LANES)


def kernel(x_prompt, x_sample, c, cache_mla_ckv, cache_mla_krope, state_ssd_fwd, state_ssd_bwd, c_ctx, w_mod, b_mod, norm_pre_mix, norm_post_mix, norm_pre_ffn, norm_post_ffn, w_in_ab, q_norm, w_uq, kv_norm, w_ukv, ssd_conv_w, ssd_conv_b, ssd_dt_bias_fwd, ssd_dt_bias_bwd, ssd_a_log_fwd, ssd_a_log_bwd, ssd_d, ssd_norm, w_out_ab, pool_w, pool_scale, ffn_w_gate, ffn_w_up, ffn_w_down):
    f32, bf16 = jnp.float32, jnp.bfloat16
    assert DEPTH == 2
    xp = x_prompt.reshape(N_PROMPT, D_MODEL)
    xs = x_sample.reshape(N_SAMPLE, D_MODEL)
    cvecs = jnp.concatenate([c_ctx[None, :], c, jnp.zeros((SUBLANES - N_MODVEC, D_MODEL), f32)], axis=0)
    mod = _modulation(cvecs.T, w_mod, b_mod).reshape(DEPTH, SUBLANES, 1, 6 * D_MODEL)
    tabs = _rope_tables()
    hp = SSD_HEADS * SSD_HEAD_DIM
    row = lambda v: v.reshape(1, -1)
    new_ckv, new_kpe, new_hf, new_hb = [], [], [], []

    vec_table, vec_rows = _pack_vectors(
        norm_pre_mix=norm_pre_mix, norm_post_mix=norm_post_mix, norm_pre_ffn=norm_pre_ffn, norm_post_ffn=norm_post_ffn,
        q_norm=q_norm, kv_norm=kv_norm, ssd_norm=ssd_norm, ssd_conv_b=ssd_conv_b, pool_scale=pool_scale)
    ffn_w = (ffn_w_gate, ffn_w_up, ffn_w_down)
    for l in range(DEPTH):
        if l % 2 == 0:
            i = l // 2
            q, ckv_n, kpe, z, xbc, dts_t, cum_t, w_t, ckv_prompt, kpe_prompt = _inproj(
                l, vec_table, vec_rows, xp, xs, mod, tabs, _layout_in_proj(w_in_ab[i]), _layout_uq(w_uq[i]),
                _lane_row(ssd_dt_bias_fwd[i], ssd_dt_bias_bwd[i]).T, _lane_row(ssd_a_log_fwd[i], ssd_a_log_bwd[i]).T)
            heads = (dts_t, cum_t, w_t)
            w_kv = _layout_ukv(w_ukv[i])
            att_p = _attention(q, ckv_n, kpe, w_kv, 0, BATCH, SEQ)
            att_s = _attention(q, ckv_n, kpe, w_kv, N_PROMPT, DEC_BATCH, DEC_SEQ,
                               cache=(cache_mla_ckv[:, i], _kpe_slab(cache_mla_krope[:, i])))
            ssd_args = (ssd_conv_w, vec_table, vec_rows["ssd_conv_b"], row(jnp.repeat(ssd_d[i], SSD_HEAD_DIM)))
            y_p, hf, hb = _ssd(i, xbc, heads, None, *ssd_args, 0, BATCH, SEQ)
            y_s, _, _ = _ssd(i, xbc, heads, (state_ssd_fwd[:, i].reshape(DEC_BATCH, hp, SSD_STATE),
                                             state_ssd_bwd[:, i].reshape(DEC_BATCH, hp, SSD_STATE)),
                             *ssd_args, N_PROMPT, DEC_BATCH, DEC_SEQ)
            xa = _outproj_ffn(l, i, vec_table, vec_rows, xp, xs, att_p, att_s, y_p, y_s, z, mod, w_out_ab, *ffn_w)
            new_ckv.append(ckv_prompt.reshape(BATCH, SEQ, KV_RANK))
            new_kpe.append(kpe_prompt.reshape(BATCH, SEQ, ROPE_DIM))
            new_hf.append(hf.reshape(BATCH, SSD_HEADS, SSD_HEAD_DIM, SSD_STATE))
            new_hb.append(hb.reshape(BATCH, SSD_HEADS, SSD_HEAD_DIM, SSD_STATE))
        else:
            j = l // 2
            yp, ys = _pool_ffn(l, j, vec_table, vec_rows, xa, mod, pool_w[j].astype(bf16), *ffn_w)

    return (yp.reshape(BATCH, SEQ, D_MODEL), ys.reshape(DEC_BATCH, DEC_SEQ, D_MODEL),
            jnp.stack(new_ckv, axis=1), jnp.stack(new_kpe, axis=1),
            jnp.stack(new_hf, axis=1), jnp.stack(new_hb, axis=1))
```

```python
import functools

import numpy as np
import jax
import jax.numpy as jnp
from jax import lax
from jax.experimental import pallas as pl
from jax.experimental.pallas import tpu as pltpu

D_MODEL = 1024
BATCH = 16
SEQ = 256
DEPTH = 2
DEC_BATCH = 2
DEC_SEQ = 2048
PAST_LEN = 256
GRID_W = 64
EPS = 1e-6
MLA_HEADS = 8
Q_RANK = 256
KV_RANK = 256
NOPE_DIM = 64
ROPE_DIM = 32
V_DIM = 64
ROPE_THETA = 10000.0
SSD_HEADS = 8
SSD_GROUPS = 2
SSD_HPG = SSD_HEADS // SSD_GROUPS
SSD_HEAD_DIM = 64
SSD_STATE = 128
D_SSD = SSD_HEADS * SSD_HEAD_DIM
CONV_W = 5
CONV_CH = D_SSD + 2 * SSD_GROUPS * SSD_STATE
POOL_WINDOWS = (2, 4, 8, 16)
POOL_GC = D_MODEL // len(POOL_WINDOWS)
D_FF = ((8 * D_MODEL + 3 * 256 - 1) // (3 * 256)) * 256

SUBLANES = 8
LANES = 128

N_PROMPT = BATCH * SEQ
N_SAMPLE = DEC_BATCH * DEC_SEQ
N_TOK = N_PROMPT + N_SAMPLE
N_MODVEC = 1 + DEC_BATCH
TM = 512
TQ = 256
CHUNK = 128
HALO = SUBLANES
HEAD_SLAB = LANES
IN_COLS = Q_RANK + KV_RANK + D_SSD + CONV_CH + 2 * LANES
KPE_LANE0 = NOPE_DIM
VMEM_LIMIT = 56 * 1024 * 1024

POOL_SUB = min(SEQ, DEC_SEQ)

assert TM % POOL_SUB == 0 and SEQ % POOL_SUB == 0 and DEC_SEQ % TM == 0 and N_PROMPT % DEC_SEQ == 0


def _rms(x, g):
    return x * lax.rsqrt(jnp.mean(x * x, axis=-1, keepdims=True) + EPS) * g


def _silu(x):
    return x * jax.nn.sigmoid(x)


def _softplus(x):
    return jnp.maximum(x, 0.0) + jnp.log1p(jnp.exp(-jnp.abs(x)))


def _dot(a, b):
    return jnp.dot(a, b, preferred_element_type=jnp.float32)


def _dot_nt(a, b):
    return lax.dot_general(a, b, (((1,), (1,)), ((), ())), preferred_element_type=jnp.float32)


def _mod_row(i):
    return jnp.where(i < N_PROMPT // TM, 0, 1 + (i - N_PROMPT // TM) // (DEC_SEQ // TM))


def _const_spec(shape):
    nd = len(shape)
    return pl.BlockSpec(shape, lambda *_: (0,) * nd, pipeline_mode=pl.Buffered(1))


N_PT = N_PROMPT // TM
N_WCHUNK = 16


def _tile(step, lead):
    return jnp.maximum(step - lead, 0)


def _row_spec(width, lead=0):
    return pl.BlockSpec((TM, width), lambda s: (_tile(s, lead), 0))


def _mod_spec(l, lead=0):
    return pl.BlockSpec((None, 1, 1, 6 * D_MODEL), lambda s: (l, _mod_row(_tile(s, lead)), 0, 0))


def _vec_spec(row):
    return pl.BlockSpec((None, 1, D_MODEL), lambda *_: (row, 0, 0), pipeline_mode=pl.Buffered(1))


def _pack_vectors(**params):
    first_row, blocks, n = {}, [], 0
    for name, a in params.items():
        first_row[name] = n
        n += a.shape[0]
        blocks.append(jnp.pad(a, ((0, 0), (0, D_MODEL - a.shape[1]))))
    return jnp.concatenate(blocks, axis=0).reshape(n, 1, D_MODEL), first_row


def _group_specs(width, lead=0):
    return [pl.BlockSpec((TM, width), lambda s: (jnp.minimum(_tile(s, lead), N_PT - 1), 0)),
            pl.BlockSpec((TM, width), lambda s: (jnp.maximum(_tile(s, lead) - N_PT, 0), 0))]


def _wchunk_spec(rows, cols, layer=None):
    ck = rows // N_WCHUNK
    if layer is None:
        return pl.BlockSpec((ck, cols), lambda s: (jnp.minimum(s, N_WCHUNK - 1), 0))
    return pl.BlockSpec((1, ck, cols), lambda s: (layer, jnp.minimum(s, N_WCHUNK - 1), 0))


def _stage_weight(step, chunk, dst_ref):
    ck = chunk.shape[0]
    dst_ref[pl.ds(pl.multiple_of(step * ck, ck), ck), :] = chunk.astype(jnp.bfloat16)


def _pick_group(i, p_ref, s_ref, rows=slice(None)):
    return jnp.where(i < N_PT, p_ref[rows, :], s_ref[rows, :])


MOD_TK = 256
MOD_ROWS = 2 * SUBLANES
SUB_ROWS = 256


def _split3(a):
    a_hi = a.astype(jnp.bfloat16)
    r1 = a - a_hi.astype(jnp.float32)
    a_mid = r1.astype(jnp.bfloat16)
    a_lo = (r1 - a_mid.astype(jnp.float32)).astype(jnp.bfloat16)
    return a_hi, a_mid, a_lo


def _mod_kernel(c_ref, w_ref, b_ref, o_ref):
    k = pl.program_id(1)
    s_hi, s_mid, s_lo = _split3(_silu(c_ref[...]))
    w = w_ref[0]
    w_hi = w.astype(jnp.bfloat16)
    w_lo = (w - w_hi.astype(jnp.float32)).astype(jnp.bfloat16)
    top = _dot(jnp.concatenate([s_hi, s_mid, s_lo], axis=0), w_hi)
    low = _dot(jnp.concatenate([s_hi, s_mid], axis=0), w_lo)
    part = (top[:MOD_ROWS] + top[MOD_ROWS:2 * MOD_ROWS] + top[2 * MOD_ROWS:] + low[:MOD_ROWS] + low[MOD_ROWS:])
    part = part[:SUBLANES]

    @pl.when(k == 0)
    def _():
        o_ref[0] = part + b_ref[0]

    @pl.when(k > 0)
    def _():
        o_ref[0] += part


def _modulation(cvecs, w_mod, b_mod):
    n = 6 * D_MODEL
    return pl.pallas_call(
        _mod_kernel,
        grid=(DEPTH, D_MODEL // MOD_TK),
        in_specs=[pl.BlockSpec((MOD_ROWS, MOD_TK), lambda l, k: (0, k)),
                  pl.BlockSpec((1, MOD_TK, n), lambda l, k: (l, k, 0)),
                  pl.BlockSpec((1, 1, n), lambda l, k: (l, 0, 0))],
        out_specs=pl.BlockSpec((1, SUBLANES, n), lambda l, k: (l, 0, 0)),
        out_shape=jax.ShapeDtypeStruct((DEPTH, SUBLANES, n), jnp.float32),
        compiler_params=pltpu.CompilerParams(dimension_semantics=("arbitrary", "arbitrary"),
                                             vmem_limit_bytes=VMEM_LIMIT),
        name="modulation",
    )(cvecs, w_mod, b_mod.reshape(DEPTH, 1, n))


def _ssd_head_terms(dt_raw, dtb_c, a_neg_c):
    nh2 = 2 * SSD_HEADS
    row = lax.broadcasted_iota(jnp.int32, (CHUNK, CHUNK), 0)
    col = lax.broadcasted_iota(jnp.int32, (CHUNK, CHUNK), 1)
    upper_b = (row <= col).astype(jnp.bfloat16)
    lower_b = (row >= col).astype(jnp.bfloat16)
    fwd_rows = lax.broadcasted_iota(jnp.int32, (nh2, CHUNK), 0) < SSD_HEADS
    dts_t = _softplus(dt_raw.T[:nh2, :] + dtb_c)
    pieces = _split3(dts_t * a_neg_c)
    cum_t = jnp.where(fwd_rows, sum(_dot(p, upper_b) for p in pieces), sum(_dot(p, lower_b) for p in pieces))
    cum_t = cum_t * np.float32(np.log2(np.e))
    tot = jnp.where(fwd_rows[:, :1], cum_t[:, CHUNK - 1:], cum_t[:, :1])
    return dts_t, cum_t, dts_t * jnp.exp2(tot - cum_t)


def _inproj_kernel(xp_ref, xs_ref, mod_ref, cos_ref, s1_ref, s2_ref, w_in_ref, w_uq_ref, qn_ref, kvn_ref, npm_ref,
                   dtbc_ref, alogc_ref, q_ref, ckv_ref, kpe_ref, z_ref, xbc_ref, dts_ref, cumt_ref, wt_ref,
                   new_ckv_ref, new_kpe_ref):
    i = pl.program_id(0)
    is_latent = i >= N_PT
    sh = mod_ref[0, :, 0:D_MODEL]
    sc = mod_ref[0, :, D_MODEL:2 * D_MODEL]
    scale = (NOPE_DIM + ROPE_DIM) ** -0.5 * np.log2(np.e)
    nh2 = 2 * SSD_HEADS
    dtb_c = dtbc_ref[:nh2, :]
    a_neg_c = -jnp.exp(alogc_ref[:nh2, :])
    dt_raw = []
    for r in range(TM // SUB_ROWS):
        rs = slice(r * SUB_ROWS, (r + 1) * SUB_ROWS)
        h = (_rms(_pick_group(i, xp_ref, xs_ref, rs), npm_ref[...]) * (1.0 + sc) + sh).astype(jnp.bfloat16)
        p = _dot(h, w_in_ref[...])
        o = 0
        cq = p[:, o:o + Q_RANK]; o += Q_RANK
        ckv = p[:, o:o + KV_RANK]; o += KV_RANK
        z_ref[rs, :] = p[:, o:o + D_SSD]; o += D_SSD
        xbc_ref[rs, :] = p[:, o:o + CONV_CH]; o += CONV_CH
        kpe = p[:, o:o + LANES]; o += LANES
        dt_raw.append(p[:, o:o + LANES])

        cos = cos_ref[rs, :]
        s1 = s1_ref[rs, :]
        s2 = s2_ref[rs, :]

        def rope(slab):
            rot = (slab * cos + pltpu.roll(slab, LANES - ROPE_DIM // 2, 1) * s1
                   + pltpu.roll(slab, ROPE_DIM // 2, 1) * s2)
            return jnp.where(is_latent, rot, slab)

        ckv_ref[rs, :] = _rms(ckv, kvn_ref[:, :KV_RANK])
        kpe_ref[rs, :] = rope(kpe)
        q = _dot(_rms(cq, qn_ref[:, :Q_RANK]).astype(jnp.bfloat16), w_uq_ref[...]) * scale
        for hd in range(MLA_HEADS):
            sl = slice(hd * HEAD_SLAB, (hd + 1) * HEAD_SLAB)
            q_ref[rs, sl] = rope(q[:, sl]).astype(jnp.bfloat16)

    dt_all = jnp.concatenate(dt_raw, axis=0)
    for ck in range(TM // CHUNK):
        terms = _ssd_head_terms(dt_all[ck * CHUNK:(ck + 1) * CHUNK, :], dtb_c, a_neg_c)
        for ref, val in zip((dts_ref, cumt_ref, wt_ref), terms):
            ref[ck * nh2:(ck + 1) * nh2, :] = val

    @pl.when(i < N_PT)
    def _():
        new_ckv_ref[...] = ckv_ref[...]
        new_kpe_ref[...] = kpe_ref[:, KPE_LANE0:KPE_LANE0 + ROPE_DIM]


def _inproj(l, vec_table, vec_rows, xp, xs, mod, tabs, w_in, w_uq, dt_bias_c, a_log_c):
    nt = N_TOK // TM
    prompt_blk = lambda i: (jnp.minimum(i, N_PT - 1), 0)
    tab_spec = pl.BlockSpec((TM, LANES), lambda i: (jnp.maximum(i - N_PT, 0) % (DEC_SEQ // TM), 0))
    row = _row_spec
    hrows = TM // CHUNK * 2 * SSD_HEADS
    head_spec = pl.BlockSpec((hrows, CHUNK), lambda i: (i, 0))
    head_shape = jax.ShapeDtypeStruct((nt * hrows, CHUNK), jnp.float32)
    return pl.pallas_call(
        _inproj_kernel,
        grid=(nt,),
        in_specs=_group_specs(D_MODEL) + [
            _mod_spec(l),
            tab_spec, tab_spec, tab_spec,
            _const_spec((D_MODEL, IN_COLS)),
            _const_spec((Q_RANK, MLA_HEADS * HEAD_SLAB)),
            _vec_spec(vec_rows["q_norm"] + l // 2),
            _vec_spec(vec_rows["kv_norm"] + l // 2),
            _vec_spec(vec_rows["norm_pre_mix"] + l),
            _const_spec((LANES, 1)), _const_spec((LANES, 1)),
        ],
        out_specs=[row(MLA_HEADS * HEAD_SLAB), row(KV_RANK), row(LANES), row(D_SSD), row(CONV_CH),
                   head_spec, head_spec, head_spec,
                   pl.BlockSpec((TM, KV_RANK), prompt_blk), pl.BlockSpec((TM, ROPE_DIM), prompt_blk)],
        out_shape=[
            jax.ShapeDtypeStruct((N_TOK, MLA_HEADS * HEAD_SLAB), jnp.bfloat16),
            jax.ShapeDtypeStruct((N_TOK, KV_RANK), jnp.float32),
            jax.ShapeDtypeStruct((N_TOK, LANES), jnp.float32),
            jax.ShapeDtypeStruct((N_TOK, D_SSD), jnp.float32),
            jax.ShapeDtypeStruct((N_TOK, CONV_CH), jnp.float32),
            head_shape, head_shape, head_shape,
            jax.ShapeDtypeStruct((N_PROMPT, KV_RANK), jnp.float32),
            jax.ShapeDtypeStruct((N_PROMPT, ROPE_DIM), jnp.float32),
        ],
        compiler_params=pltpu.CompilerParams(dimension_semantics=("arbitrary",), vmem_limit_bytes=VMEM_LIMIT),
        name="inproj",
    )(xp, xs, mod, *tabs, w_in, w_uq, vec_table, vec_table, vec_table, dt_bias_c, a_log_c)


def _attn_kernel(*refs, lk_cache, lk_new):
    if lk_cache:
        q_ref, ckv_ref, kpe_ref, ckvc_ref, kpec_ref, wk_ref, wvt_ref, o_ref, k_scr, vt_scr = refs
    else:
        q_ref, ckv_ref, kpe_ref, wk_ref, wvt_ref, o_ref, k_scr, vt_scr = refs

    @pl.when(pl.program_id(1) == 0)
    def _expand_kv():
        def expand(ckv, kpe, r0):
            ckv_b = ckv.astype(jnp.bfloat16)
            kn = _dot(ckv_b, wk_ref[...])
            rows = slice(r0, r0 + ckv.shape[0])
            for hd in range(MLA_HEADS):
                k_scr[hd, rows, :] = (kn[:, hd * HEAD_SLAB:(hd + 1) * HEAD_SLAB] + kpe).astype(jnp.bfloat16)
            vt_scr[:, rows] = _dot_nt(wvt_ref[...], ckv_b).astype(jnp.bfloat16)

        step = 256
        for r0 in range(0, lk_cache, step):
            expand(ckvc_ref[0, r0:r0 + step, :], kpec_ref[0, r0:r0 + step, :], r0)
        for r0 in range(0, lk_new, step):
            expand(ckv_ref[r0:r0 + step, :], kpe_ref[r0:r0 + step, :], lk_cache + r0)

    scores = [_dot_nt(k_scr[hd], q_ref[:, hd * HEAD_SLAB:(hd + 1) * HEAD_SLAB]) for hd in range(MLA_HEADS)]
    outs = []
    for hd, s_t in enumerate(scores):
        p_t = jnp.exp2(s_t - jnp.max(s_t, axis=0, keepdims=True))
        den = jnp.sum(p_t, axis=0, keepdims=True)
        outs.append(_dot(vt_scr[hd * V_DIM:(hd + 1) * V_DIM, :], p_t.astype(jnp.bfloat16)) / den)
    o_ref[...] = jnp.concatenate(outs, axis=0).T.astype(jnp.bfloat16)


def _attention(q, ckv_n, kpe, w_ukv, row_off, n_batch, seq, cache=None):
    nq = seq // TQ
    lk_cache = 0 if cache is None else cache[0].shape[1]
    lk = lk_cache + seq
    qblk = lambda b, qi: (row_off // TQ + b * nq + qi, 0)
    sblk = lambda b, qi: (row_off // seq + b, 0)
    in_specs = [
        pl.BlockSpec((TQ, MLA_HEADS * HEAD_SLAB), qblk),
        pl.BlockSpec((seq, KV_RANK), sblk),
        pl.BlockSpec((seq, LANES), sblk),
    ]
    args = [q, ckv_n, kpe]
    if cache is not None:
        in_specs += [pl.BlockSpec((1, lk_cache, KV_RANK), lambda b, qi: (b, 0, 0)),
                     pl.BlockSpec((1, lk_cache, LANES), lambda b, qi: (b, 0, 0))]
        args += list(cache)
    in_specs += [_const_spec(w.shape) for w in w_ukv]
    args += list(w_ukv)
    return pl.pallas_call(
        functools.partial(_attn_kernel, lk_cache=lk_cache, lk_new=seq),
        grid=(n_batch, nq),
        in_specs=in_specs,
        out_specs=pl.BlockSpec((TQ, MLA_HEADS * V_DIM), lambda b, qi: (b * nq + qi, 0)),
        out_shape=jax.ShapeDtypeStruct((n_batch * seq, MLA_HEADS * V_DIM), jnp.bfloat16),
        scratch_shapes=[pltpu.VMEM((MLA_HEADS, lk, HEAD_SLAB), jnp.bfloat16),
                        pltpu.VMEM((MLA_HEADS * V_DIM, lk), jnp.bfloat16)],
        compiler_params=pltpu.CompilerParams(dimension_semantics=("arbitrary", "arbitrary"),
                                             vmem_limit_bytes=VMEM_LIMIT),
        name=f"attention_{seq}",
    )(*args)


def _ssd_kernel(*refs, seq, zero_init):
    if zero_init:
        (xbc_ref, dts_ref, cumt_ref, wt_ref, cw_ref, cb_ref, dsk_ref,
         y_ref, hf_ref, hb_ref, xs_scr, c_scr, bt_scr, cum_scr, stf_scr, stb_scr) = refs
    else:
        (xbc_ref, dts_ref, cumt_ref, wt_ref, h0f_ref, h0b_ref, cw_ref, cb_ref, dsk_ref,
         y_ref, hf_ref, hb_ref, xs_scr, c_scr, bt_scr, cum_scr, stf_scr, stb_scr) = refs
    nc = seq // CHUNK
    gs = SSD_GROUPS * SSD_STATE
    nh2 = 2 * SSD_HEADS

    row = lax.broadcasted_iota(jnp.int32, (CHUNK, CHUNK), 0)
    col = lax.broadcasted_iota(jnp.int32, (CHUNK, CHUNK), 1)
    low_half = col < SSD_HEAD_DIM
    lower = row >= col
    upper = row <= col

    def prep_chunk(c, carry):
        r0 = pl.multiple_of(c * CHUNK, CHUNK)
        rows = pl.ds(r0, CHUNK)
        rows_prev = pl.ds(pl.multiple_of(jnp.maximum(r0 - HALO, 0), HALO), HALO)
        rows_next = pl.ds(pl.multiple_of(jnp.minimum(r0 + CHUNK, seq - HALO), HALO), HALO)

        def conv_tile(cs):
            prev = jnp.where(c > 0, xbc_ref[rows_prev, cs], 0.0)
            nxt = jnp.where(c < nc - 1, xbc_ref[rows_next, cs], 0.0)
            win = jnp.concatenate([prev, xbc_ref[rows, cs], nxt], axis=0)
            acc = jnp.broadcast_to(cb_ref[:, cs], (CHUNK, LANES))
            for k in range(CONV_W):
                lo = HALO - CONV_W // 2 + k
                acc = acc + cw_ref[k:k + 1, cs] * win[lo:lo + CHUNK, :]
            return _silu(acc)

        def x_tile(j, carry):
            cs = pl.ds(pl.multiple_of(j * LANES, LANES), LANES)
            u = conv_tile(cs)
            y_ref[rows, cs] = dsk_ref[:, cs] * u
            xs_scr[rows, cs] = u.astype(jnp.bfloat16)
            return carry

        lax.fori_loop(0, D_SSD // LANES, x_tile, 0)
        cum_t = cumt_ref[pl.ds(pl.multiple_of(c * nh2, nh2), nh2), :]
        cum_scr[rows, :] = jnp.concatenate(
            [cum_t, jnp.zeros((CHUNK - nh2, CHUNK), jnp.float32)], axis=0).T[:, :nh2]
        for g in range(SSD_GROUPS):
            b0 = pl.multiple_of(c * gs + g * SSD_STATE, SSD_STATE)
            bt_scr[pl.ds(b0, SSD_STATE), :] = conv_tile(slice(D_SSD + g * SSD_STATE, D_SSD + (g + 1) * SSD_STATE)).T
            c_scr[rows, g * SSD_STATE:(g + 1) * SSD_STATE] = conv_tile(
                slice(D_SSD + gs + g * SSD_STATE, D_SSD + gs + (g + 1) * SSD_STATE)).astype(jnp.bfloat16)
        return carry

    lax.fori_loop(0, nc, prep_chunk, 0)

    if zero_init:
        stf_scr[...] = jnp.zeros_like(stf_scr)
        stb_scr[...] = jnp.zeros_like(stb_scr)
    else:
        stf_scr[...] = h0f_ref[0].T
        stb_scr[...] = h0b_ref[0].T

    def scan_open(ci, st_scr):
        rows = pl.ds(pl.multiple_of(ci * CHUNK, CHUNK), CHUNK)
        c_b = c_scr[rows, :]
        st = st_scr[...]
        bts, cbms, zs = [], [], []
        for g in range(SSD_GROUPS):
            cg = c_b[:, g * SSD_STATE:(g + 1) * SSD_STATE]
            bt = bt_scr[pl.ds(pl.multiple_of(ci * gs + g * SSD_STATE, SSD_STATE), SSD_STATE), :]
            gcols = slice(g * SSD_HPG * SSD_HEAD_DIM, (g + 1) * SSD_HPG * SSD_HEAD_DIM)
            bts.append(bt)
            cbms.append(_dot(cg, bt.astype(jnp.bfloat16)))
            zs.append(_dot(cg, st[:, gcols].astype(jnp.bfloat16)))
        return st, bts, cbms, zs

    def scan_pairs(ci, st_scr, reverse, opened):
        st, bts, cbms, zs = opened
        lane0 = SSD_HEADS if reverse else 0
        causal = upper if reverse else lower
        last = 0 if reverse else CHUNK - 1
        rows = pl.ds(pl.multiple_of(ci * CHUNK, CHUNK), CHUNK)
        hrows = pl.ds(pl.multiple_of(ci * nh2, nh2), nh2)
        xs_b = xs_scr[rows, :]
        dts_t = dts_ref[hrows, :]
        cum_t = cumt_ref[hrows, :]
        w_t = wt_ref[hrows, :]
        cum = cum_scr[rows, :]
        for pair in range(SSD_HEADS // 2):
            g, jj = divmod(pair, SSD_HPG // 2)
            pcols = slice(pair * LANES, (pair + 1) * LANES)
            lhs_y, lhs_s, entry = [], [], []
            for hd in (2 * pair, 2 * pair + 1):
                ln = lane0 + hd
                cum_i = jnp.broadcast_to(cum[:, ln:ln + 1], (CHUNK, CHUNK))
                dec = jnp.exp2(jnp.where(causal, cum_i - cum_t[ln:ln + 1, :], -jnp.inf))
                lhs_y.append((cbms[g] * dec * dts_t[ln:ln + 1, :]).astype(jnp.bfloat16))
                lhs_s.append((bts[g] * w_t[ln:ln + 1, :]).astype(jnp.bfloat16))
                entry.append(jnp.exp2(cum_i))
            out = _dot(jnp.concatenate(lhs_y + lhs_s, axis=0), xs_b[:, pcols])
            ea = jnp.where(low_half, entry[0], entry[1])
            y_ref[rows, pcols] += (jnp.where(low_half, out[:CHUNK], out[CHUNK:2 * CHUNK])
                                   + zs[g][:, jj * LANES:(jj + 1) * LANES] * ea)
            st_scr[:, pcols] = (ea[last:last + 1, :] * st[:, pcols]
                                + jnp.where(low_half, out[2 * CHUNK:3 * CHUNK], out[3 * CHUNK:]))

    def both(c, carry):
        opened_f = scan_open(c, stf_scr)
        opened_b = scan_open(nc - 1 - c, stb_scr)
        scan_pairs(c, stf_scr, False, opened_f)
        scan_pairs(nc - 1 - c, stb_scr, True, opened_b)
        return carry

    lax.fori_loop(0, nc, both, 0)
    hf_ref[0] = stf_scr[...].T
    hb_ref[0] = stb_scr[...].T


def _ssd(i_ab, xbc, head_terms, h0, conv_w, vec_table, conv_b_row, d_skip, row_off, n_batch, seq):
    assert CONV_CH == D_MODEL
    hp = SSD_HEADS * SSD_HEAD_DIM
    gs = SSD_GROUPS * SSD_STATE
    nc = seq // CHUNK
    seq_blk = lambda b: (row_off // seq + b, 0)
    head_spec = pl.BlockSpec((nc * 2 * SSD_HEADS, CHUNK), seq_blk)
    st_spec = pl.BlockSpec((1, hp, SSD_STATE), lambda b: (b, 0, 0))
    st_shape = jax.ShapeDtypeStruct((n_batch, hp, SSD_STATE), jnp.float32)
    h0 = () if h0 is None else tuple(h0)
    return pl.pallas_call(
        functools.partial(_ssd_kernel, seq=seq, zero_init=not h0),
        grid=(n_batch,),
        in_specs=[pl.BlockSpec((seq, CONV_CH), seq_blk), head_spec, head_spec, head_spec] + [st_spec] * len(h0) + [
            pl.BlockSpec((None, CONV_W, CONV_CH), lambda b: (i_ab, 0, 0), pipeline_mode=pl.Buffered(1)),
            _vec_spec(conv_b_row + i_ab), _const_spec((1, D_SSD))],
        out_specs=[pl.BlockSpec((seq, D_SSD), lambda b: (b, 0)), st_spec, st_spec],
        out_shape=[jax.ShapeDtypeStruct((n_batch * seq, D_SSD), jnp.float32), st_shape, st_shape],
        scratch_shapes=[pltpu.VMEM((seq, D_SSD), jnp.bfloat16),
                        pltpu.VMEM((seq, gs), jnp.bfloat16),
                        pltpu.VMEM((nc * gs, CHUNK), jnp.float32),
                        pltpu.VMEM((seq, 2 * SSD_HEADS), jnp.float32),
                        pltpu.VMEM((SSD_STATE, hp), jnp.float32),
                        pltpu.VMEM((SSD_STATE, hp), jnp.float32)],
        compiler_params=pltpu.CompilerParams(dimension_semantics=("arbitrary",), vmem_limit_bytes=VMEM_LIMIT),
        name=f"ssd_{seq}",
    )(xbc, *head_terms, *h0, conv_w, vec_table, d_skip)


def _post_mix(x, mix, mod_ref, npost_ref, npre_ref):
    d = D_MODEL
    gate_mix = mod_ref[0, :, 2 * d:3 * d]
    shf = mod_ref[0, :, 3 * d:4 * d]
    scf = mod_ref[0, :, 4 * d:5 * d]
    x1 = x + gate_mix * _rms(mix, npost_ref[...])
    return x1, (_rms(x1, npre_ref[...]) * (1.0 + scf) + shf).astype(jnp.bfloat16)


def _subtile_pipeline(n_sub, mixer_pre, mixer_dots, mod_ref, nffn_ref, wg_ref, wu_ref, wd_ref, interleave):
    gate_ffn = mod_ref[0, :, 5 * D_MODEL:6 * D_MODEL]
    ffn_up = lambda h: (_silu(_dot(h, wg_ref[...])) * _dot(h, wu_ref[...])).astype(jnp.bfloat16)
    ffn_down = lambda x1, hid: x1 + gate_ffn * _rms(_dot(hid, wd_ref[...]), nffn_ref[...])
    if not interleave:
        staged = [mixer_dots(r, mixer_pre(r)) for r in range(n_sub)]
        return [ffn_down(x1, ffn_up(h)) for x1, h in staged]
    outs = []
    x1, h = mixer_dots(0, mixer_pre(0))
    for r in range(n_sub):
        nxt_pre = mixer_pre(r + 1) if r + 1 < n_sub else None
        hid = ffn_up(h)
        nxt = mixer_dots(r + 1, nxt_pre) if r + 1 < n_sub else None
        outs.append(ffn_down(x1, hid))
        if nxt is not None:
            x1, h = nxt
    return outs


def _ffn_specs(l, vec_rows):
    return [_vec_spec(vec_rows[name] + l) for name in ("norm_post_mix", "norm_pre_ffn", "norm_post_ffn")] + [
        _wchunk_spec(D_MODEL, D_FF, l), _wchunk_spec(D_MODEL, D_FF, l), _wchunk_spec(D_FF, D_MODEL, l)]


def _ffn_scratch():
    return [pltpu.VMEM((D_MODEL, D_FF), jnp.bfloat16), pltpu.VMEM((D_MODEL, D_FF), jnp.bfloat16),
            pltpu.VMEM((D_FF, D_MODEL), jnp.bfloat16)]


def _outproj_ffn_kernel(xp_ref, xs_ref, attp_ref, atts_ref, yp_ref, ys_ref, z_ref, mod_ref, sn_ref, wo_ref,
                        npost_ref, npre_ref, nffn_ref, wg_ref, wu_ref, wd_ref, o_ref,
                        wo_scr, wg_scr, wu_scr, wd_scr):
    step = pl.program_id(0)

    @pl.when(step < N_WCHUNK)
    def _stage():
        _stage_weight(step, wo_ref[0], wo_scr)
        _stage_weight(step, wg_ref[0], wg_scr)
        _stage_weight(step, wu_ref[0], wu_scr)
        _stage_weight(step, wd_ref[0], wd_scr)

    @pl.when(step == 0)
    def _():
        o_ref[...] = jnp.zeros_like(o_ref)

    @pl.when(step >= N_WCHUNK)
    def _tile_step():
        i = step - N_WCHUNK
        gw = D_SSD // SSD_GROUPS
        sub = lambda r: slice(r * SUB_ROWS, (r + 1) * SUB_ROWS)

        def mixer_pre(r):
            yg = _pick_group(i, yp_ref, ys_ref, sub(r)) * _silu(z_ref[sub(r), :])
            parts = [_pick_group(i, attp_ref, atts_ref, sub(r))]
            for g in range(SSD_GROUPS):
                parts.append(_rms(yg[:, g * gw:(g + 1) * gw], sn_ref[:, g * gw:(g + 1) * gw]).astype(jnp.bfloat16))
            return jnp.concatenate(parts, axis=1)

        def mixer_dots(r, cat):
            return _post_mix(_pick_group(i, xp_ref, xs_ref, sub(r)), _dot(cat, wo_scr[...]),
                             mod_ref, npost_ref, npre_ref)

        outs = _subtile_pipeline(TM // SUB_ROWS, mixer_pre, mixer_dots, mod_ref, nffn_ref, wg_scr, wu_scr, wd_scr,
                                 interleave=False)
        for r, res in enumerate(outs):
            o_ref[sub(r), :] = res


def _outproj_ffn(l, i_ab, vec_table, vec_rows, xp, xs, att_p, att_s, y_p, y_s, z, mod, w_out, wg, wu, wd):
    lead = N_WCHUNK
    d_cat = MLA_HEADS * V_DIM + D_SSD
    return pl.pallas_call(
        _outproj_ffn_kernel,
        grid=(lead + N_TOK // TM,),
        in_specs=(_group_specs(D_MODEL, lead) + _group_specs(MLA_HEADS * V_DIM, lead) + _group_specs(D_SSD, lead)
                  + [_row_spec(D_SSD, lead), _mod_spec(l, lead), _vec_spec(vec_rows["ssd_norm"] + i_ab),
                     _wchunk_spec(d_cat, D_MODEL, i_ab)] + _ffn_specs(l, vec_rows)),
        out_specs=_row_spec(D_MODEL, lead),
        out_shape=jax.ShapeDtypeStruct((N_TOK, D_MODEL), jnp.float32),
        scratch_shapes=[pltpu.VMEM((d_cat, D_MODEL), jnp.bfloat16)] + _ffn_scratch(),
        compiler_params=pltpu.CompilerParams(dimension_semantics=("arbitrary",), vmem_limit_bytes=VMEM_LIMIT),
        name="outproj_ffn",
    )(xp, xs, att_p, att_s, y_p, y_s, z, mod, vec_table, w_out, vec_table, vec_table, vec_table, wg, wu, wd)


def _pool_ffn_kernel(x_ref, xp_ref, xn_ref, mod_ref, nmix_ref, pw_ref, ps_ref,
                     npost_ref, npre_ref, nffn_ref, wg_ref, wu_ref, wd_ref, op_ref, os_ref,
                     wg_scr, wu_scr, wd_scr):
    step = pl.program_id(0)

    @pl.when(step < N_WCHUNK)
    def _stage():
        _stage_weight(step, wg_ref[0], wg_scr)
        _stage_weight(step, wu_ref[0], wu_scr)
        _stage_weight(step, wd_ref[0], wd_scr)

    @pl.when(step == 0)
    def _():
        op_ref[...] = jnp.zeros_like(op_ref)
        os_ref[...] = jnp.zeros_like(os_ref)

    @pl.when(step >= N_WCHUNK)
    def _tile_step():
        _pool_ffn_tile(step - N_WCHUNK, x_ref, xp_ref, xn_ref, mod_ref, nmix_ref, pw_ref, ps_ref,
                       npost_ref, npre_ref, nffn_ref, wg_scr, wu_scr, wd_scr, op_ref, os_ref)


def _pool_ffn_tile(i, x_ref, xp_ref, xn_ref, mod_ref, nmix_ref, pw_ref, ps_ref,
                   npost_ref, npre_ref, nffn_ref, wg_ref, wu_ref, wd_ref, op_ref, os_ref):
    seq = jnp.where(i < N_PROMPT // TM, SEQ, DEC_SEQ)
    pos0 = (i * TM) % seq
    sh = mod_ref[0, :, 0:D_MODEL]
    sc = mod_ref[0, :, D_MODEL:2 * D_MODEL]
    hmod = lambda v: _rms(v, nmix_ref[...]) * (1.0 + sc) + sh
    n_rows = POOL_SUB + 2 * HALO

    def shifted(v, s):
        return pltpu.roll(v, n_rows - s, 0)

    def mixer_pre(s):
        lo, hi = s * POOL_SUB, (s + 1) * POOL_SUB
        pos_s = (pos0 + lo) % seq
        h = hmod(x_ref[lo:hi, :])
        before = hmod(xp_ref[...] if s == 0 else x_ref[lo - HALO:lo, :])
        after = hmod(xn_ref[...] if hi == TM else x_ref[hi:hi + HALO, :])
        before = jnp.where(pos_s > 0, before, 0.0)
        after = jnp.where(pos_s + POOL_SUB < seq, after, 0.0)
        padded = jnp.concatenate([before, h, after], axis=0)
        pos = pos_s + lax.broadcasted_iota(jnp.int32, (POOL_SUB, 1), 0)
        pooled = []
        for gi, w in enumerate(POOL_WINDOWS):
            cols = slice(gi * POOL_GC, (gi + 1) * POOL_GC)
            t = padded[:, cols]
            span = 1
            while span < w:
                t = t + shifted(t, span)
                span *= 2
            lead = HALO - w // 2
            win_sum = (shifted(t, lead) if lead else t)[:POOL_SUB, :]
            cnt = (jnp.minimum(pos + w // 2, seq) - jnp.maximum(pos - w // 2, 0)).astype(jnp.float32)
            pooled.append((win_sum / cnt - h[:, cols]).astype(jnp.bfloat16))
        return pooled

    def mixer_dots(s, pooled):
        mix = jnp.concatenate([_dot(p, pw_ref[gi]) for gi, p in enumerate(pooled)], axis=1) * ps_ref[...]
        return _post_mix(x_ref[s * POOL_SUB:(s + 1) * POOL_SUB, :], mix, mod_ref, npost_ref, npre_ref)

    res = jnp.concatenate(_subtile_pipeline(TM // POOL_SUB, mixer_pre, mixer_dots, mod_ref, nffn_ref,
                                            wg_ref, wu_ref, wd_ref, interleave=True), axis=0)

    @pl.when(i < N_PT)
    def _():
        op_ref[...] = res

    @pl.when(i >= N_PT)
    def _():
        os_ref[...] = res


def _pool_ffn(l, j_c, vec_table, vec_rows, xa, mod, pool_w, wg, wu, wd):
    lead = N_WCHUNK
    hb = TM // HALO
    nh = N_TOK // HALO
    return pl.pallas_call(
        _pool_ffn_kernel,
        grid=(lead + N_TOK // TM,),
        in_specs=[_row_spec(D_MODEL, lead),
                  pl.BlockSpec((HALO, D_MODEL), lambda s: (jnp.maximum(_tile(s, lead) * hb - 1, 0), 0)),
                  pl.BlockSpec((HALO, D_MODEL), lambda s: (jnp.minimum((_tile(s, lead) + 1) * hb, nh - 1), 0)),
                  _mod_spec(l, lead),
                  _vec_spec(vec_rows["norm_pre_mix"] + l),
                  _const_spec((len(POOL_WINDOWS), POOL_GC, POOL_GC)),
                  _vec_spec(vec_rows["pool_scale"] + j_c)] + _ffn_specs(l, vec_rows),
        out_specs=_group_specs(D_MODEL, lead),
        out_shape=[jax.ShapeDtypeStruct((N_PROMPT, D_MODEL), jnp.float32),
                   jax.ShapeDtypeStruct((N_SAMPLE, D_MODEL), jnp.float32)],
        scratch_shapes=_ffn_scratch(),
        compiler_params=pltpu.CompilerParams(dimension_semantics=("arbitrary",), vmem_limit_bytes=VMEM_LIMIT),
        name="pool_ffn",
    )(xa, xa, xa, mod, vec_table, pool_w, vec_table, vec_table, vec_table, vec_table, wg, wu, wd)


def _rope_tables():
    f32 = np.float32
    rows = DEC_SEQ // GRID_W
    r = np.repeat(np.arange(rows, dtype=f32), GRID_W)
    c = np.tile(np.arange(GRID_W, dtype=f32), rows)
    half = ROPE_DIM // 2
    inv_freq = np.power(f32(ROPE_THETA), -np.arange(0, half, 2, dtype=f32) / f32(half)).astype(f32)
    ang = np.concatenate([r[:, None] * inv_freq, c[:, None] * inv_freq], axis=-1).astype(f32)
    cos, sin = np.cos(ang).astype(f32), np.sin(ang).astype(f32)
    zl = np.zeros((DEC_SEQ, KPE_LANE0), f32)
    zr = np.zeros((DEC_SEQ, LANES - KPE_LANE0 - ROPE_DIM), f32)
    zh = np.zeros((DEC_SEQ, half), f32)
    cos_t = np.concatenate([zl + 1, cos, cos, zr + 1], axis=1)
    s1_t = np.concatenate([zl, -sin, zh, zr], axis=1)
    s2_t = np.concatenate([zl, zh, sin, zr], axis=1)
    return jnp.asarray(cos_t), jnp.asarray(s1_t), jnp.asarray(s2_t)


def _kpe_slab(k):
    pad = [(0, 0)] * (k.ndim - 1) + [(KPE_LANE0, LANES - KPE_LANE0 - ROPE_DIM)]
    return jnp.pad(k, pad)


def _layout_in_proj(w):
    o = np.cumsum((0, Q_RANK, KV_RANK, ROPE_DIM, D_SSD, CONV_CH, SSD_HEADS, SSD_HEADS))
    cq, ckv, kpe, z, xbc, dtf, dtb = (w[:, o[k]:o[k + 1]] for k in range(7))
    dt = jnp.pad(jnp.concatenate([dtf, dtb], axis=1), ((0, 0), (0, LANES - 2 * SSD_HEADS)))
    return jnp.concatenate([cq, ckv, z, xbc, _kpe_slab(kpe), dt], axis=1).astype(jnp.bfloat16)


def _layout_uq(w):
    w = w.reshape(Q_RANK, MLA_HEADS, NOPE_DIM + ROPE_DIM)
    w = jnp.pad(w, ((0, 0), (0, 0), (0, HEAD_SLAB - NOPE_DIM - ROPE_DIM)))
    return w.reshape(Q_RANK, MLA_HEADS * HEAD_SLAB).astype(jnp.bfloat16)


def _layout_ukv(w):
    w = w.reshape(KV_RANK, MLA_HEADS, NOPE_DIM + V_DIM)
    kn = jnp.pad(w[:, :, :NOPE_DIM], ((0, 0), (0, 0), (0, HEAD_SLAB - NOPE_DIM)))
    v = w[:, :, NOPE_DIM:]
    return kn.reshape(KV_RANK, -1).astype(jnp.bfloat16), v.reshape(KV_RANK, -1).T.astype(jnp.bfloat16)


def _lane_row(fwd, bwd):
    return jnp.pad(jnp.concatenate([fwd, bwd]), (0, LANES - 2 * SSD_HEADS)).reshape(1, LANES)


def kernel(x_prompt, x_sample, c, cache_mla_ckv, cache_mla_krope, state_ssd_fwd, state_ssd_bwd, c_ctx, w_mod, b_mod, norm_pre_mix, norm_post_mix, norm_pre_ffn, norm_post_ffn, w_in_ab, q_norm, w_uq, kv_norm, w_ukv, ssd_conv_w, ssd_conv_b, ssd_dt_bias_fwd, ssd_dt_bias_bwd, ssd_a_log_fwd, ssd_a_log_bwd, ssd_d, ssd_norm, w_out_ab, pool_w, pool_scale, ffn_w_gate, ffn_w_up, ffn_w_down):
    f32, bf16 = jnp.float32, jnp.bfloat16
    assert DEPTH == 2
    xp = x_prompt.reshape(N_PROMPT, D_MODEL)
    xs = x_sample.reshape(N_SAMPLE, D_MODEL)
    cvecs = jnp.concatenate([c_ctx[None, :], c, jnp.zeros((MOD_ROWS - N_MODVEC, D_MODEL), f32)], axis=0)
    mod = _modulation(cvecs, w_mod, b_mod).reshape(DEPTH, SUBLANES, 1, 6 * D_MODEL)
    tabs = _rope_tables()
    hp = SSD_HEADS * SSD_HEAD_DIM
    row = lambda v: v.reshape(1, -1)
    new_ckv, new_kpe, new_hf, new_hb = [], [], [], []

    vec_table, vec_rows = _pack_vectors(
        norm_pre_mix=norm_pre_mix, norm_post_mix=norm_post_mix, norm_pre_ffn=norm_pre_ffn, norm_post_ffn=norm_post_ffn,
        q_norm=q_norm, kv_norm=kv_norm, ssd_norm=ssd_norm, ssd_conv_b=ssd_conv_b, pool_scale=pool_scale)
    ffn_w = (ffn_w_gate, ffn_w_up, ffn_w_down)
    for l in range(DEPTH):
        if l % 2 == 0:
            i = l // 2
            q, ckv_n, kpe, z, xbc, dts_t, cum_t, w_t, ckv_prompt, kpe_prompt = _inproj(
                l, vec_table, vec_rows, xp, xs, mod, tabs, _layout_in_proj(w_in_ab[i]), _layout_uq(w_uq[i]),
                _lane_row(ssd_dt_bias_fwd[i], ssd_dt_bias_bwd[i]).T, _lane_row(ssd_a_log_fwd[i], ssd_a_log_bwd[i]).T)
            heads = (dts_t, cum_t, w_t)
            w_kv = _layout_ukv(w_ukv[i])
            att_p = _attention(q, ckv_n, kpe, w_kv, 0, BATCH, SEQ)
            att_s = _attention(q, ckv_n, kpe, w_kv, N_PROMPT, DEC_BATCH, DEC_SEQ,
                               cache=(cache_mla_ckv[:, i], _kpe_slab(cache_mla_krope[:, i])))
            ssd_args = (ssd_conv_w, vec_table, vec_rows["ssd_conv_b"], row(jnp.repeat(ssd_d[i], SSD_HEAD_DIM)))
            y_p, hf, hb = _ssd(i, xbc, heads, None, *ssd_args, 0, BATCH, SEQ)
            y_s, _, _ = _ssd(i, xbc, heads, (state_ssd_fwd[:, i].reshape(DEC_BATCH, hp, SSD_STATE),
                                             state_ssd_bwd[:, i].reshape(DEC_BATCH, hp, SSD_STATE)),
                             *ssd_args, N_PROMPT, DEC_BATCH, DEC_SEQ)
            xa = _outproj_ffn(l, i, vec_table, vec_rows, xp, xs, att_p, att_s, y_p, y_s, z, mod, w_out_ab, *ffn_w)
            new_ckv.append(ckv_prompt.reshape(BATCH, SEQ, KV_RANK))
            new_kpe.append(kpe_prompt.reshape(BATCH, SEQ, ROPE_DIM))
            new_hf.append(hf.reshape(BATCH, SSD_HEADS, SSD_HEAD_DIM, SSD_STATE))
            new_hb.append(hb.reshape(BATCH, SSD_HEADS, SSD_HEAD_DIM, SSD_STATE))
        else:
            j = l // 2
            yp, ys = _pool_ffn(l, j, vec_table, vec_rows, xa, mod, pool_w[j].astype(bf16), *ffn_w)

    return (yp.reshape(BATCH, SEQ, D_MODEL), ys.reshape(DEC_BATCH, DEC_SEQ, D_MODEL),
            jnp.stack(new_ckv, axis=1), jnp.stack(new_kpe, axis=1),
            jnp.stack(new_hf, axis=1), jnp.stack(new_hb, axis=1))
```

```python
import functools

import numpy as np
import jax
import jax.numpy as jnp
from jax import lax
from jax.experimental import pallas as pl
from jax.experimental.pallas import tpu as pltpu

D_MODEL = 1024
BATCH = 16
SEQ = 256
DEPTH = 2
DEC_BATCH = 2
DEC_SEQ = 2048
PAST_LEN = 256
GRID_W = 64
EPS = 1e-6
MLA_HEADS = 8
Q_RANK = 256
KV_RANK = 256
NOPE_DIM = 64
ROPE_DIM = 32
V_DIM = 64
ROPE_THETA = 10000.0
SSD_HEADS = 8
SSD_GROUPS = 2
SSD_HPG = SSD_HEADS // SSD_GROUPS
SSD_HEAD_DIM = 64
SSD_STATE = 128
D_SSD = SSD_HEADS * SSD_HEAD_DIM
CONV_W = 5
CONV_CH = D_SSD + 2 * SSD_GROUPS * SSD_STATE
POOL_WINDOWS = (2, 4, 8, 16)
POOL_GC = D_MODEL // len(POOL_WINDOWS)
D_FF = ((8 * D_MODEL + 3 * 256 - 1) // (3 * 256)) * 256

SUBLANES = 8
LANES = 128

N_PROMPT = BATCH * SEQ
N_SAMPLE = DEC_BATCH * DEC_SEQ
N_TOK = N_PROMPT + N_SAMPLE
N_MODVEC = 1 + DEC_BATCH
TM = 512
TQ = 256
CHUNK = 128
HALO = SUBLANES
HEAD_SLAB = LANES
IN_COLS = Q_RANK + KV_RANK + D_SSD + CONV_CH + 2 * LANES
KPE_LANE0 = NOPE_DIM
VMEM_LIMIT = 56 * 1024 * 1024

POOL_SUB = min(SEQ, DEC_SEQ)

assert TM % POOL_SUB == 0 and SEQ % POOL_SUB == 0 and DEC_SEQ % TM == 0 and N_PROMPT % DEC_SEQ == 0


def _rms(x, g):
    return x * lax.rsqrt(jnp.mean(x * x, axis=-1, keepdims=True) + EPS) * g


def _silu(x):
    return x * jax.nn.sigmoid(x)


def _softplus(x):
    return jnp.maximum(x, 0.0) + jnp.log1p(jnp.exp(-jnp.abs(x)))


def _dot(a, b):
    return jnp.dot(a, b, preferred_element_type=jnp.float32)


def _dot_nt(a, b):
    return lax.dot_general(a, b, (((1,), (1,)), ((), ())), preferred_element_type=jnp.float32)


def _mod_row(i):
    return jnp.where(i < N_PROMPT // TM, 0, 1 + (i - N_PROMPT // TM) // (DEC_SEQ // TM))


def _const_spec(shape):
    nd = len(shape)
    return pl.BlockSpec(shape, lambda *_: (0,) * nd, pipeline_mode=pl.Buffered(1))


N_PT = N_PROMPT // TM
N_WCHUNK = 16


def _tile(step, lead):
    return jnp.maximum(step - lead, 0)


def _row_spec(width, lead=0):
    return pl.BlockSpec((TM, width), lambda s: (_tile(s, lead), 0))


def _mod_spec(l, lead=0):
    return pl.BlockSpec((None, 1, 1, 6 * D_MODEL), lambda s: (l, _mod_row(_tile(s, lead)), 0, 0))


def _vec_spec(row):
    return pl.BlockSpec((None, 1, D_MODEL), lambda *_: (row, 0, 0), pipeline_mode=pl.Buffered(1))


def _pack_vectors(**params):
    first_row, blocks, n = {}, [], 0
    for name, a in params.items():
        first_row[name] = n
        n += a.shape[0]
        blocks.append(jnp.pad(a, ((0, 0), (0, D_MODEL - a.shape[1]))))
    return jnp.concatenate(blocks, axis=0).reshape(n, 1, D_MODEL), first_row


def _group_specs(width, lead=0):
    return [pl.BlockSpec((TM, width), lambda s: (jnp.minimum(_tile(s, lead), N_PT - 1), 0)),
            pl.BlockSpec((TM, width), lambda s: (jnp.maximum(_tile(s, lead) - N_PT, 0), 0))]


def _wchunk_spec(rows, cols, layer=None):
    ck = rows // N_WCHUNK
    if layer is None:
        return pl.BlockSpec((ck, cols), lambda s: (jnp.minimum(s, N_WCHUNK - 1), 0))
    return pl.BlockSpec((1, ck, cols), lambda s: (layer, jnp.minimum(s, N_WCHUNK - 1), 0))


def _stage_weight(step, chunk, dst_ref):
    ck = chunk.shape[0]
    dst_ref[pl.ds(pl.multiple_of(step * ck, ck), ck), :] = chunk.astype(jnp.bfloat16)


def _pick_group(i, p_ref, s_ref, rows=slice(None)):
    return jnp.where(i < N_PT, p_ref[rows, :], s_ref[rows, :])


MOD_TK = 256
MOD_STREAMS = 4
MOD_ROWS = 2 * SUBLANES
SUB_ROWS = 256


def _split3(a):
    a_hi = a.astype(jnp.bfloat16)
    r1 = a - a_hi.astype(jnp.float32)
    a_mid = r1.astype(jnp.bfloat16)
    a_lo = (r1 - a_mid.astype(jnp.float32)).astype(jnp.bfloat16)
    return a_hi, a_mid, a_lo


def _mod_kernel(c_ref, *refs):
    w_refs, b_ref, o_ref = refs[:MOD_STREAMS], refs[MOD_STREAMS], refs[MOD_STREAMS + 1]
    k = pl.program_id(1)
    rows = MOD_TK // MOD_STREAMS
    s_all = _silu(c_ref[...])
    part = jnp.zeros((MOD_ROWS, 6 * D_MODEL), jnp.float32)
    for j, w_ref in enumerate(w_refs):
        s_hi, s_mid, s_lo = _split3(s_all[:, j * rows:(j + 1) * rows])
        w = w_ref[0]
        w_hi = w.astype(jnp.bfloat16)
        w_lo = (w - w_hi.astype(jnp.float32)).astype(jnp.bfloat16)
        top = _dot(jnp.concatenate([s_hi, s_mid, s_lo], axis=0), w_hi)
        low = _dot(jnp.concatenate([s_hi, s_mid], axis=0), w_lo)
        part = part + (top[:MOD_ROWS] + top[MOD_ROWS:2 * MOD_ROWS] + top[2 * MOD_ROWS:]
                       + low[:MOD_ROWS] + low[MOD_ROWS:])
    part = part[:SUBLANES]

    @pl.when(k == 0)
    def _():
        o_ref[0] = part + b_ref[0]

    @pl.when(k > 0)
    def _():
        o_ref[0] += part


def _modulation(cvecs, w_mod, b_mod):
    n = 6 * D_MODEL
    w_spec = lambda j: pl.BlockSpec((1, MOD_TK // MOD_STREAMS, n), lambda l, k: (l, k * MOD_STREAMS + j, 0))
    return pl.pallas_call(
        _mod_kernel,
        grid=(DEPTH, D_MODEL // MOD_TK),
        in_specs=[pl.BlockSpec((MOD_ROWS, MOD_TK), lambda l, k: (0, k))]
        + [w_spec(j) for j in range(MOD_STREAMS)]
        + [pl.BlockSpec((1, 1, n), lambda l, k: (l, 0, 0))],
        out_specs=pl.BlockSpec((1, SUBLANES, n), lambda l, k: (l, 0, 0)),
        out_shape=jax.ShapeDtypeStruct((DEPTH, SUBLANES, n), jnp.float32),
        compiler_params=pltpu.CompilerParams(dimension_semantics=("arbitrary", "arbitrary"),
                                             vmem_limit_bytes=VMEM_LIMIT),
        name="modulation",
    )(cvecs, *([w_mod] * MOD_STREAMS), b_mod.reshape(DEPTH, 1, n))


def _ssd_head_terms(dt_raw, dtb_c, a_neg_c):
    nh2 = 2 * SSD_HEADS
    row = lax.broadcasted_iota(jnp.int32, (CHUNK, CHUNK), 0)
    col = lax.broadcasted_iota(jnp.int32, (CHUNK, CHUNK), 1)
    upper_b = (row <= col).astype(jnp.bfloat16)
    lower_b = (row >= col).astype(jnp.bfloat16)
    fwd_rows = lax.broadcasted_iota(jnp.int32, (nh2, CHUNK), 0) < SSD_HEADS
    dts_t = _softplus(dt_raw.T[:nh2, :] + dtb_c)
    pieces = _split3(dts_t * a_neg_c)
    cum_t = jnp.where(fwd_rows, sum(_dot(p, upper_b) for p in pieces), sum(_dot(p, lower_b) for p in pieces))
    cum_t = cum_t * np.float32(np.log2(np.e))
    tot = jnp.where(fwd_rows[:, :1], cum_t[:, CHUNK - 1:], cum_t[:, :1])
    return dts_t, cum_t, dts_t * jnp.exp2(tot - cum_t)


def _inproj_kernel(xp_ref, xs_ref, mod_ref, cos_ref, sin_ref, w_in_ref, w_uq_ref, qn_ref, kvn_ref, npm_ref,
                   dtbc_ref, alogc_ref, q_ref, ckv_ref, kpe_ref, z_ref, xbc_ref, dts_ref, cumt_ref, wt_ref,
                   new_ckv_ref, new_kpe_ref):
    i = pl.program_id(0)
    sh = mod_ref[0, :, 0:D_MODEL]
    sc = mod_ref[0, :, D_MODEL:2 * D_MODEL]
    scale = (NOPE_DIM + ROPE_DIM) ** -0.5 * np.log2(np.e)
    nh2 = 2 * SSD_HEADS
    dtb_c = dtbc_ref[:nh2, :]
    a_neg_c = -jnp.exp(alogc_ref[:nh2, :])
    dt_raw = []
    for r in range(TM // SUB_ROWS):
        rs = slice(r * SUB_ROWS, (r + 1) * SUB_ROWS)
        h = (_rms(_pick_group(i, xp_ref, xs_ref, rs), npm_ref[...]) * (1.0 + sc) + sh).astype(jnp.bfloat16)
        p = _dot(h, w_in_ref[...])
        o = 0
        cq = p[:, o:o + Q_RANK]; o += Q_RANK
        ckv = p[:, o:o + KV_RANK]; o += KV_RANK
        z_ref[rs, :] = p[:, o:o + D_SSD]; o += D_SSD
        xbc_ref[rs, :] = p[:, o:o + CONV_CH]; o += CONV_CH
        kpe = p[:, o:o + LANES]; o += LANES
        dt_raw.append(p[:, o:o + LANES])

        cos = cos_ref[rs, :]
        sin = sin_ref[rs, :]

        def rope(slab):
            return slab * cos + pltpu.roll(slab, LANES - ROPE_DIM, 1) * sin

        ckv_ref[rs, :] = _rms(ckv, kvn_ref[:, :KV_RANK])
        kpe_ref[rs, :] = rope(kpe)
        q = _dot(_rms(cq, qn_ref[:, :Q_RANK]).astype(jnp.bfloat16), w_uq_ref[...]) * scale
        for hd in range(MLA_HEADS):
            sl = slice(hd * HEAD_SLAB, (hd + 1) * HEAD_SLAB)
            q_ref[rs, sl] = rope(q[:, sl]).astype(jnp.bfloat16)

    dt_all = jnp.concatenate(dt_raw, axis=0)
    for ck in range(TM // CHUNK):
        terms = _ssd_head_terms(dt_all[ck * CHUNK:(ck + 1) * CHUNK, :], dtb_c, a_neg_c)
        for ref, val in zip((dts_ref, cumt_ref, wt_ref), terms):
            ref[ck * nh2:(ck + 1) * nh2, :] = val

    @pl.when(i < N_PT)
    def _():
        new_ckv_ref[...] = ckv_ref[...]
        new_kpe_ref[...] = kpe_ref[:, KPE_LANE0:KPE_LANE0 + ROPE_DIM]


def _inproj(l, vec_table, vec_rows, xp, xs, mod, tabs, w_in, w_uq, dt_bias_c, a_log_c):
    nt = N_TOK // TM
    prompt_blk = lambda i: (jnp.minimum(i, N_PT - 1), 0)
    lat_tiles = DEC_SEQ // TM
    tab_spec = pl.BlockSpec((TM, LANES), lambda i: (jnp.where(i < N_PT, lat_tiles, (i - N_PT) % lat_tiles), 0))
    row = _row_spec
    hrows = TM // CHUNK * 2 * SSD_HEADS
    head_spec = pl.BlockSpec((hrows, CHUNK), lambda i: (i, 0))
    head_shape = jax.ShapeDtypeStruct((nt * hrows, CHUNK), jnp.float32)
    return pl.pallas_call(
        _inproj_kernel,
        grid=(nt,),
        in_specs=_group_specs(D_MODEL) + [
            _mod_spec(l),
            tab_spec, tab_spec,
            _const_spec((D_MODEL, IN_COLS)),
            _const_spec((Q_RANK, MLA_HEADS * HEAD_SLAB)),
            _vec_spec(vec_rows["q_norm"] + l // 2),
            _vec_spec(vec_rows["kv_norm"] + l // 2),
            _vec_spec(vec_rows["norm_pre_mix"] + l),
            _const_spec((LANES, 1)), _const_spec((LANES, 1)),
        ],
        out_specs=[row(MLA_HEADS * HEAD_SLAB), row(KV_RANK), row(LANES), row(D_SSD), row(CONV_CH),
                   head_spec, head_spec, head_spec,
                   pl.BlockSpec((TM, KV_RANK), prompt_blk), pl.BlockSpec((TM, ROPE_DIM), prompt_blk)],
        out_shape=[
            jax.ShapeDtypeStruct((N_TOK, MLA_HEADS * HEAD_SLAB), jnp.bfloat16),
            jax.ShapeDtypeStruct((N_TOK, KV_RANK), jnp.float32),
            jax.ShapeDtypeStruct((N_TOK, LANES), jnp.float32),
            jax.ShapeDtypeStruct((N_TOK, D_SSD), jnp.float32),
            jax.ShapeDtypeStruct((N_TOK, CONV_CH), jnp.float32),
            head_shape, head_shape, head_shape,
            jax.ShapeDtypeStruct((N_PROMPT, KV_RANK), jnp.float32),
            jax.ShapeDtypeStruct((N_PROMPT, ROPE_DIM), jnp.float32),
        ],
        compiler_params=pltpu.CompilerParams(dimension_semantics=("arbitrary",), vmem_limit_bytes=VMEM_LIMIT),
        name="inproj",
    )(xp, xs, mod, *tabs, w_in, w_uq, vec_table, vec_table, vec_table, dt_bias_c, a_log_c)


def _attn_kernel(*refs, lk_cache, lk_new):
    if lk_cache:
        q_ref, ckv_ref, kpe_ref, ckvc_ref, kpec_ref, wk_ref, wvt_ref, o_ref, k_scr, vt_scr = refs
    else:
        q_ref, ckv_ref, kpe_ref, wk_ref, wvt_ref, o_ref, k_scr, vt_scr = refs

    @pl.when(pl.program_id(1) == 0)
    def _expand_kv():
        def expand(ckv, kpe, r0):
            ckv_b = ckv.astype(jnp.bfloat16)
            kn = _dot(ckv_b, wk_ref[...])
            rows = slice(r0, r0 + ckv.shape[0])
            for hd in range(MLA_HEADS):
                k_scr[hd, rows, :] = (kn[:, hd * HEAD_SLAB:(hd + 1) * HEAD_SLAB] + kpe).astype(jnp.bfloat16)
            vt_scr[:, rows] = _dot_nt(wvt_ref[...], ckv_b).astype(jnp.bfloat16)

        step = 256
        for r0 in range(0, lk_cache, step):
            expand(ckvc_ref[0, r0:r0 + step, :], kpec_ref[0, r0:r0 + step, :], r0)
        for r0 in range(0, lk_new, step):
            expand(ckv_ref[r0:r0 + step, :], kpe_ref[r0:r0 + step, :], lk_cache + r0)

    scores = [_dot_nt(k_scr[hd], q_ref[:, hd * HEAD_SLAB:(hd + 1) * HEAD_SLAB]) for hd in range(MLA_HEADS)]
    outs = []
    for hd, s_t in enumerate(scores):
        p_t = jnp.exp2(s_t - jnp.max(s_t, axis=0, keepdims=True))
        den = jnp.sum(p_t, axis=0, keepdims=True)
        outs.append(_dot(vt_scr[hd * V_DIM:(hd + 1) * V_DIM, :], p_t.astype(jnp.bfloat16)) / den)
    o_ref[...] = jnp.concatenate(outs, axis=0).T.astype(jnp.bfloat16)


def _attention(q, ckv_n, kpe, w_ukv, row_off, n_batch, seq, cache=None):
    nq = seq // TQ
    lk_cache = 0 if cache is None else cache[0].shape[1]
    lk = lk_cache + seq
    qblk = lambda b, qi: (row_off // TQ + b * nq + qi, 0)
    sblk = lambda b, qi: (row_off // seq + b, 0)
    in_specs = [
        pl.BlockSpec((TQ, MLA_HEADS * HEAD_SLAB), qblk),
        pl.BlockSpec((seq, KV_RANK), sblk),
        pl.BlockSpec((seq, LANES), sblk),
    ]
    args = [q, ckv_n, kpe]
    if cache is not None:
        in_specs += [pl.BlockSpec((1, lk_cache, KV_RANK), lambda b, qi: (b, 0, 0)),
                     pl.BlockSpec((1, lk_cache, LANES), lambda b, qi: (b, 0, 0))]
        args += list(cache)
    in_specs += [_const_spec(w.shape) for w in w_ukv]
    args += list(w_ukv)
    return pl.pallas_call(
        functools.partial(_attn_kernel, lk_cache=lk_cache, lk_new=seq),
        grid=(n_batch, nq),
        in_specs=in_specs,
        out_specs=pl.BlockSpec((TQ, MLA_HEADS * V_DIM), lambda b, qi: (b * nq + qi, 0)),
        out_shape=jax.ShapeDtypeStruct((n_batch * seq, MLA_HEADS * V_DIM), jnp.bfloat16),
        scratch_shapes=[pltpu.VMEM((MLA_HEADS, lk, HEAD_SLAB), jnp.bfloat16),
                        pltpu.VMEM((MLA_HEADS * V_DIM, lk), jnp.bfloat16)],
        compiler_params=pltpu.CompilerParams(dimension_semantics=("arbitrary", "arbitrary"),
                                             vmem_limit_bytes=VMEM_LIMIT),
        name=f"attention_{seq}",
    )(*args)


def _ssd_kernel(*refs, seq, zero_init):
    if zero_init:
        (xbc_ref, dts_ref, cumt_ref, wt_ref, cw_ref, cb_ref, dsk_ref,
         y_ref, hf_ref, hb_ref, xs_scr, c_scr, bt_scr, cum_scr, stf_scr, stb_scr) = refs
    else:
        (xbc_ref, dts_ref, cumt_ref, wt_ref, h0f_ref, h0b_ref, cw_ref, cb_ref, dsk_ref,
         y_ref, hf_ref, hb_ref, xs_scr, c_scr, bt_scr, cum_scr, stf_scr, stb_scr) = refs
    nc = seq // CHUNK
    gs = SSD_GROUPS * SSD_STATE
    nh2 = 2 * SSD_HEADS

    row = lax.broadcasted_iota(jnp.int32, (CHUNK, CHUNK), 0)
    col = lax.broadcasted_iota(jnp.int32, (CHUNK, CHUNK), 1)
    low_half = col < SSD_HEAD_DIM
    lower = row >= col
    upper = row <= col

    def prep_chunk(c, carry):
        r0 = pl.multiple_of(c * CHUNK, CHUNK)
        rows = pl.ds(r0, CHUNK)
        rows_prev = pl.ds(pl.multiple_of(jnp.maximum(r0 - HALO, 0), HALO), HALO)
        rows_next = pl.ds(pl.multiple_of(jnp.minimum(r0 + CHUNK, seq - HALO), HALO), HALO)

        def conv_tile(cs):
            prev = jnp.where(c > 0, xbc_ref[rows_prev, cs], 0.0)
            nxt = jnp.where(c < nc - 1, xbc_ref[rows_next, cs], 0.0)
            win = jnp.concatenate([prev, xbc_ref[rows, cs], nxt], axis=0)
            acc = jnp.broadcast_to(cb_ref[:, cs], (CHUNK, LANES))
            for k in range(CONV_W):
                lo = HALO - CONV_W // 2 + k
                acc = acc + cw_ref[k:k + 1, cs] * win[lo:lo + CHUNK, :]
            return _silu(acc)

        def x_tile(j, carry):
            cs = pl.ds(pl.multiple_of(j * LANES, LANES), LANES)
            u = conv_tile(cs)
            y_ref[rows, cs] = dsk_ref[:, cs] * u
            xs_scr[rows, cs] = u.astype(jnp.bfloat16)
            return carry

        lax.fori_loop(0, D_SSD // LANES, x_tile, 0)
        cum_t = cumt_ref[pl.ds(pl.multiple_of(c * nh2, nh2), nh2), :]
        cum_scr[rows, :] = jnp.concatenate(
            [cum_t, jnp.zeros((CHUNK - nh2, CHUNK), jnp.float32)], axis=0).T[:, :nh2]
        for g in range(SSD_GROUPS):
            b0 = pl.multiple_of(c * gs + g * SSD_STATE, SSD_STATE)
            bt_scr[pl.ds(b0, SSD_STATE), :] = conv_tile(slice(D_SSD + g * SSD_STATE, D_SSD + (g + 1) * SSD_STATE)).T
            c_scr[rows, g * SSD_STATE:(g + 1) * SSD_STATE] = conv_tile(
                slice(D_SSD + gs + g * SSD_STATE, D_SSD + gs + (g + 1) * SSD_STATE)).astype(jnp.bfloat16)
        return carry

    lax.fori_loop(0, nc, prep_chunk, 0)

    if zero_init:
        stf_scr[...] = jnp.zeros_like(stf_scr)
        stb_scr[...] = jnp.zeros_like(stb_scr)
    else:
        stf_scr[...] = h0f_ref[0].T
        stb_scr[...] = h0b_ref[0].T

    def scan_open(ci, st_scr):
        rows = pl.ds(pl.multiple_of(ci * CHUNK, CHUNK), CHUNK)
        c_b = c_scr[rows, :]
        st = st_scr[...]
        bts, cbms, zs = [], [], []
        for g in range(SSD_GROUPS):
            cg = c_b[:, g * SSD_STATE:(g + 1) * SSD_STATE]
            bt = bt_scr[pl.ds(pl.multiple_of(ci * gs + g * SSD_STATE, SSD_STATE), SSD_STATE), :]
            gcols = slice(g * SSD_HPG * SSD_HEAD_DIM, (g + 1) * SSD_HPG * SSD_HEAD_DIM)
            bts.append(bt)
            cbms.append(_dot(cg, bt.astype(jnp.bfloat16)))
            zs.append(_dot(cg, st[:, gcols].astype(jnp.bfloat16)))
        return st, bts, cbms, zs

    def scan_pairs(ci, st_scr, reverse, opened):
        st, bts, cbms, zs = opened
        lane0 = SSD_HEADS if reverse else 0
        causal = upper if reverse else lower
        last = 0 if reverse else CHUNK - 1
        rows = pl.ds(pl.multiple_of(ci * CHUNK, CHUNK), CHUNK)
        hrows = pl.ds(pl.multiple_of(ci * nh2, nh2), nh2)
        xs_b = xs_scr[rows, :]
        dts_t = dts_ref[hrows, :]
        cum_t = cumt_ref[hrows, :]
        w_t = wt_ref[hrows, :]
        cum = cum_scr[rows, :]
        for pair in range(SSD_HEADS // 2):
            g, jj = divmod(pair, SSD_HPG // 2)
            pcols = slice(pair * LANES, (pair + 1) * LANES)
            lhs_y, lhs_s, entry = [], [], []
            for hd in (2 * pair, 2 * pair + 1):
                ln = lane0 + hd
                cum_i = jnp.broadcast_to(cum[:, ln:ln + 1], (CHUNK, CHUNK))
                dec = jnp.exp2(jnp.where(causal, cum_i - cum_t[ln:ln + 1, :], -jnp.inf))
                lhs_y.append((cbms[g] * dec * dts_t[ln:ln + 1, :]).astype(jnp.bfloat16))
                lhs_s.append((bts[g] * w_t[ln:ln + 1, :]).astype(jnp.bfloat16))
                entry.append(jnp.exp2(cum_i))
            out = _dot(jnp.concatenate(lhs_y + lhs_s, axis=0), xs_b[:, pcols])
            ea = jnp.where(low_half, entry[0], entry[1])
            y_ref[rows, pcols] += (jnp.where(low_half, out[:CHUNK], out[CHUNK:2 * CHUNK])
                                   + zs[g][:, jj * LANES:(jj + 1) * LANES] * ea)
            st_scr[:, pcols] = (ea[last:last + 1, :] * st[:, pcols]
                                + jnp.where(low_half, out[2 * CHUNK:3 * CHUNK], out[3 * CHUNK:]))

    def both(c, carry):
        opened_f = scan_open(c, stf_scr)
        opened_b = scan_open(nc - 1 - c, stb_scr)
        scan_pairs(c, stf_scr, False, opened_f)
        scan_pairs(nc - 1 - c, stb_scr, True, opened_b)
        return carry

    lax.fori_loop(0, nc, both, 0)
    hf_ref[0] = stf_scr[...].T
    hb_ref[0] = stb_scr[...].T


def _ssd(i_ab, xbc, head_terms, h0, conv_w, vec_table, conv_b_row, d_skip, row_off, n_batch, seq):
    assert CONV_CH == D_MODEL
    hp = SSD_HEADS * SSD_HEAD_DIM
    gs = SSD_GROUPS * SSD_STATE
    nc = seq // CHUNK
    seq_blk = lambda b: (row_off // seq + b, 0)
    head_spec = pl.BlockSpec((nc * 2 * SSD_HEADS, CHUNK), seq_blk)
    st_spec = pl.BlockSpec((1, hp, SSD_STATE), lambda b: (b, 0, 0))
    st_shape = jax.ShapeDtypeStruct((n_batch, hp, SSD_STATE), jnp.float32)
    h0 = () if h0 is None else tuple(h0)
    return pl.pallas_call(
        functools.partial(_ssd_kernel, seq=seq, zero_init=not h0),
        grid=(n_batch,),
        in_specs=[pl.BlockSpec((seq, CONV_CH), seq_blk), head_spec, head_spec, head_spec] + [st_spec] * len(h0) + [
            pl.BlockSpec((None, CONV_W, CONV_CH), lambda b: (i_ab, 0, 0), pipeline_mode=pl.Buffered(1)),
            _vec_spec(conv_b_row + i_ab), _const_spec((1, D_SSD))],
        out_specs=[pl.BlockSpec((seq, D_SSD), lambda b: (b, 0)), st_spec, st_spec],
        out_shape=[jax.ShapeDtypeStruct((n_batch * seq, D_SSD), jnp.float32), st_shape, st_shape],
        scratch_shapes=[pltpu.VMEM((seq, D_SSD), jnp.bfloat16),
                        pltpu.VMEM((seq, gs), jnp.bfloat16),
                        pltpu.VMEM((nc * gs, CHUNK), jnp.float32),
                        pltpu.VMEM((seq, 2 * SSD_HEADS), jnp.float32),
                        pltpu.VMEM((SSD_STATE, hp), jnp.float32),
                        pltpu.VMEM((SSD_STATE, hp), jnp.float32)],
        compiler_params=pltpu.CompilerParams(dimension_semantics=("arbitrary",), vmem_limit_bytes=VMEM_LIMIT),
        name=f"ssd_{seq}",
    )(xbc, *head_terms, *h0, conv_w, vec_table, d_skip)


def _post_mix(x, mix, mod_ref, npost_ref, npre_ref):
    d = D_MODEL
    gate_mix = mod_ref[0, :, 2 * d:3 * d]
    shf = mod_ref[0, :, 3 * d:4 * d]
    scf = mod_ref[0, :, 4 * d:5 * d]
    x1 = x + gate_mix * _rms(mix, npost_ref[...])
    return x1, (_rms(x1, npre_ref[...]) * (1.0 + scf) + shf).astype(jnp.bfloat16)


def _subtile_pipeline(n_sub, mixer_pre, mixer_dots, mod_ref, nffn_ref, wg_ref, wu_ref, wd_ref, interleave):
    gate_ffn = mod_ref[0, :, 5 * D_MODEL:6 * D_MODEL]
    ffn_up = lambda h: (_silu(_dot(h, wg_ref[...])) * _dot(h, wu_ref[...])).astype(jnp.bfloat16)
    ffn_down = lambda x1, hid: x1 + gate_ffn * _rms(_dot(hid, wd_ref[...]), nffn_ref[...])
    if not interleave:
        staged = [mixer_dots(r, mixer_pre(r)) for r in range(n_sub)]
        return [ffn_down(x1, ffn_up(h)) for x1, h in staged]
    outs = []
    x1, h = mixer_dots(0, mixer_pre(0))
    for r in range(n_sub):
        nxt_pre = mixer_pre(r + 1) if r + 1 < n_sub else None
        hid = ffn_up(h)
        nxt = mixer_dots(r + 1, nxt_pre) if r + 1 < n_sub else None
        outs.append(ffn_down(x1, hid))
        if nxt is not None:
            x1, h = nxt
    return outs


def _ffn_specs(l, vec_rows):
    return [_vec_spec(vec_rows[name] + l) for name in ("norm_post_mix", "norm_pre_ffn", "norm_post_ffn")] + [
        _wchunk_spec(D_MODEL, D_FF, l), _wchunk_spec(D_MODEL, D_FF, l), _wchunk_spec(D_FF, D_MODEL, l)]


def _ffn_scratch():
    return [pltpu.VMEM((D_MODEL, D_FF), jnp.bfloat16), pltpu.VMEM((D_MODEL, D_FF), jnp.bfloat16),
            pltpu.VMEM((D_FF, D_MODEL), jnp.bfloat16)]


def _outproj_ffn_kernel(xp_ref, xs_ref, attp_ref, atts_ref, yp_ref, ys_ref, z_ref, mod_ref, sn_ref, wo_ref,
                        npost_ref, npre_ref, nffn_ref, wg_ref, wu_ref, wd_ref, o_ref,
                        wo_scr, wg_scr, wu_scr, wd_scr):
    step = pl.program_id(0)

    @pl.when(step < N_WCHUNK)
    def _stage():
        _stage_weight(step, wo_ref[0], wo_scr)
        _stage_weight(step, wg_ref[0], wg_scr)
        _stage_weight(step, wu_ref[0], wu_scr)
        _stage_weight(step, wd_ref[0], wd_scr)

    @pl.when(step == 0)
    def _():
        o_ref[...] = jnp.zeros_like(o_ref)

    @pl.when(step >= N_WCHUNK)
    def _tile_step():
        i = step - N_WCHUNK
        gw = D_SSD // SSD_GROUPS
        sub = lambda r: slice(r * SUB_ROWS, (r + 1) * SUB_ROWS)

        def mixer_pre(r):
            yg = _pick_group(i, yp_ref, ys_ref, sub(r)) * _silu(z_ref[sub(r), :])
            parts = [_pick_group(i, attp_ref, atts_ref, sub(r))]
            for g in range(SSD_GROUPS):
                parts.append(_rms(yg[:, g * gw:(g + 1) * gw], sn_ref[:, g * gw:(g + 1) * gw]).astype(jnp.bfloat16))
            return jnp.concatenate(parts, axis=1)

        def mixer_dots(r, cat):
            return _post_mix(_pick_group(i, xp_ref, xs_ref, sub(r)), _dot(cat, wo_scr[...]),
                             mod_ref, npost_ref, npre_ref)

        outs = _subtile_pipeline(TM // SUB_ROWS, mixer_pre, mixer_dots, mod_ref, nffn_ref, wg_scr, wu_scr, wd_scr,
                                 interleave=False)
        for r, res in enumerate(outs):
            o_ref[sub(r), :] = res


def _outproj_ffn(l, i_ab, vec_table, vec_rows, xp, xs, att_p, att_s, y_p, y_s, z, mod, w_out, wg, wu, wd):
    lead = N_WCHUNK
    d_cat = MLA_HEADS * V_DIM + D_SSD
    return pl.pallas_call(
        _outproj_ffn_kernel,
        grid=(lead + N_TOK // TM,),
        in_specs=(_group_specs(D_MODEL, lead) + _group_specs(MLA_HEADS * V_DIM, lead) + _group_specs(D_SSD, lead)
                  + [_row_spec(D_SSD, lead), _mod_spec(l, lead), _vec_spec(vec_rows["ssd_norm"] + i_ab),
                     _wchunk_spec(d_cat, D_MODEL, i_ab)] + _ffn_specs(l, vec_rows)),
        out_specs=_row_spec(D_MODEL, lead),
        out_shape=jax.ShapeDtypeStruct((N_TOK, D_MODEL), jnp.float32),
        scratch_shapes=[pltpu.VMEM((d_cat, D_MODEL), jnp.bfloat16)] + _ffn_scratch(),
        compiler_params=pltpu.CompilerParams(dimension_semantics=("arbitrary",), vmem_limit_bytes=VMEM_LIMIT),
        name="outproj_ffn",
    )(xp, xs, att_p, att_s, y_p, y_s, z, mod, vec_table, w_out, vec_table, vec_table, vec_table, wg, wu, wd)


def _pool_ffn_kernel(x_ref, xp_ref, xn_ref, mod_ref, nmix_ref, pw_ref, ps_ref,
                     npost_ref, npre_ref, nffn_ref, wg_ref, wu_ref, wd_ref, op_ref, os_ref,
                     wg_scr, wu_scr, wd_scr):
    step = pl.program_id(0)

    @pl.when(step < N_WCHUNK)
    def _stage():
        _stage_weight(step, wg_ref[0], wg_scr)
        _stage_weight(step, wu_ref[0], wu_scr)
        _stage_weight(step, wd_ref[0], wd_scr)

    @pl.when(step == 0)
    def _():
        op_ref[...] = jnp.zeros_like(op_ref)
        os_ref[...] = jnp.zeros_like(os_ref)

    @pl.when(step >= N_WCHUNK)
    def _tile_step():
        _pool_ffn_tile(step - N_WCHUNK, x_ref, xp_ref, xn_ref, mod_ref, nmix_ref, pw_ref, ps_ref,
                       npost_ref, npre_ref, nffn_ref, wg_scr, wu_scr, wd_scr, op_ref, os_ref)


def _pool_ffn_tile(i, x_ref, xp_ref, xn_ref, mod_ref, nmix_ref, pw_ref, ps_ref,
                   npost_ref, npre_ref, nffn_ref, wg_ref, wu_ref, wd_ref, op_ref, os_ref):
    seq = jnp.where(i < N_PROMPT // TM, SEQ, DEC_SEQ)
    pos0 = (i * TM) % seq
    sh = mod_ref[0, :, 0:D_MODEL]
    sc = mod_ref[0, :, D_MODEL:2 * D_MODEL]
    hmod = lambda v: _rms(v, nmix_ref[...]) * (1.0 + sc) + sh
    n_rows = POOL_SUB + 2 * HALO

    def shifted(v, s):
        return pltpu.roll(v, n_rows - s, 0)

    def mixer_pre(s):
        lo, hi = s * POOL_SUB, (s + 1) * POOL_SUB
        pos_s = (pos0 + lo) % seq
        h = hmod(x_ref[lo:hi, :])
        before = hmod(xp_ref[...] if s == 0 else x_ref[lo - HALO:lo, :])
        after = hmod(xn_ref[...] if hi == TM else x_ref[hi:hi + HALO, :])
        before = jnp.where(pos_s > 0, before, 0.0)
        after = jnp.where(pos_s + POOL_SUB < seq, after, 0.0)
        padded = jnp.concatenate([before, h, after], axis=0)
        pos = pos_s + lax.broadcasted_iota(jnp.int32, (POOL_SUB, 1), 0)
        pooled = []
        for gi, w in enumerate(POOL_WINDOWS):
            cols = slice(gi * POOL_GC, (gi + 1) * POOL_GC)
            t = padded[:, cols]
            span = 1
            while span < w:
                t = t + shifted(t, span)
                span *= 2
            lead = HALO - w // 2
            win_sum = (shifted(t, lead) if lead else t)[:POOL_SUB, :]
            cnt = (jnp.minimum(pos + w // 2, seq) - jnp.maximum(pos - w // 2, 0)).astype(jnp.float32)
            pooled.append((win_sum / cnt - h[:, cols]).astype(jnp.bfloat16))
        return pooled

    def mixer_dots(s, pooled):
        mix = jnp.concatenate([_dot(p, pw_ref[gi]) for gi, p in enumerate(pooled)], axis=1) * ps_ref[...]
        return _post_mix(x_ref[s * POOL_SUB:(s + 1) * POOL_SUB, :], mix, mod_ref, npost_ref, npre_ref)

    res = jnp.concatenate(_subtile_pipeline(TM // POOL_SUB, mixer_pre, mixer_dots, mod_ref, nffn_ref,
                                            wg_ref, wu_ref, wd_ref, interleave=True), axis=0)

    @pl.when(i < N_PT)
    def _():
        op_ref[...] = res

    @pl.when(i >= N_PT)
    def _():
        os_ref[...] = res


def _pool_ffn(l, j_c, vec_table, vec_rows, xa, mod, pool_w, wg, wu, wd):
    lead = N_WCHUNK
    hb = TM // HALO
    nh = N_TOK // HALO
    return pl.pallas_call(
        _pool_ffn_kernel,
        grid=(lead + N_TOK // TM,),
        in_specs=[_row_spec(D_MODEL, lead),
                  pl.BlockSpec((HALO, D_MODEL), lambda s: (jnp.maximum(_tile(s, lead) * hb - 1, 0), 0)),
                  pl.BlockSpec((HALO, D_MODEL), lambda s: (jnp.minimum((_tile(s, lead) + 1) * hb, nh - 1), 0)),
                  _mod_spec(l, lead),
                  _vec_spec(vec_rows["norm_pre_mix"] + l),
                  _const_spec((len(POOL_WINDOWS), POOL_GC, POOL_GC)),
                  _vec_spec(vec_rows["pool_scale"] + j_c)] + _ffn_specs(l, vec_rows),
        out_specs=_group_specs(D_MODEL, lead),
        out_shape=[jax.ShapeDtypeStruct((N_PROMPT, D_MODEL), jnp.float32),
                   jax.ShapeDtypeStruct((N_SAMPLE, D_MODEL), jnp.float32)],
        scratch_shapes=_ffn_scratch(),
        compiler_params=pltpu.CompilerParams(dimension_semantics=("arbitrary",), vmem_limit_bytes=VMEM_LIMIT),
        name="pool_ffn",
    )(xa, xa, xa, mod, vec_table, pool_w, vec_table, vec_table, vec_table, vec_table, wg, wu, wd)


def _rope_tables():
    f32 = np.float32
    rows = DEC_SEQ // GRID_W
    r = np.repeat(np.arange(rows, dtype=f32), GRID_W)
    c = np.tile(np.arange(GRID_W, dtype=f32), rows)
    half = ROPE_DIM // 2
    inv_freq = np.power(f32(ROPE_THETA), -np.arange(0, half, 2, dtype=f32) / f32(half)).astype(f32)
    ang = np.concatenate([r[:, None] * inv_freq, c[:, None] * inv_freq], axis=-1).astype(f32)
    cos, sin = np.cos(ang).astype(f32), np.sin(ang).astype(f32)
    ones = np.ones((DEC_SEQ, KPE_LANE0), f32)
    zl = np.zeros((DEC_SEQ, KPE_LANE0), f32)
    zr = np.zeros((DEC_SEQ, LANES - KPE_LANE0 - ROPE_DIM), f32)
    cos_t = np.concatenate([ones, cos, cos, zr], axis=1)
    sin_t = np.concatenate([zl, -sin, sin, zr], axis=1)
    keep = np.concatenate([np.ones((TM, KPE_LANE0 + ROPE_DIM), f32), np.zeros((TM, LANES - KPE_LANE0 - ROPE_DIM), f32)],
                          axis=1)
    return (jnp.asarray(np.concatenate([cos_t, keep], axis=0)),
            jnp.asarray(np.concatenate([sin_t, np.zeros((TM, LANES), f32)], axis=0)))


def _kpe_slab(k):
    pad = [(0, 0)] * (k.ndim - 1) + [(KPE_LANE0, LANES - KPE_LANE0 - ROPE_DIM)]
    return jnp.pad(k, pad)


def _with_swapped_pair(w_pe):
    half = ROPE_DIM // 2
    return jnp.concatenate([w_pe, w_pe[..., half:], w_pe[..., :half]], axis=-1)


def _layout_in_proj(w):
    o = np.cumsum((0, Q_RANK, KV_RANK, ROPE_DIM, D_SSD, CONV_CH, SSD_HEADS, SSD_HEADS))
    cq, ckv, kpe, z, xbc, dtf, dtb = (w[:, o[k]:o[k + 1]] for k in range(7))
    dt = jnp.pad(jnp.concatenate([dtf, dtb], axis=1), ((0, 0), (0, LANES - 2 * SSD_HEADS)))
    kpe_slab = jnp.pad(_with_swapped_pair(kpe), ((0, 0), (KPE_LANE0, LANES - KPE_LANE0 - 2 * ROPE_DIM)))
    return jnp.concatenate([cq, ckv, z, xbc, kpe_slab, dt], axis=1).astype(jnp.bfloat16)


def _layout_uq(w):
    w = w.reshape(Q_RANK, MLA_HEADS, NOPE_DIM + ROPE_DIM)
    w = jnp.concatenate([w[:, :, :NOPE_DIM], _with_swapped_pair(w[:, :, NOPE_DIM:])], axis=-1)
    assert w.shape[-1] == HEAD_SLAB
    return w.reshape(Q_RANK, MLA_HEADS * HEAD_SLAB).astype(jnp.bfloat16)


def _layout_ukv(w):
    w = w.reshape(KV_RANK, MLA_HEADS, NOPE_DIM + V_DIM)
    kn = jnp.pad(w[:, :, :NOPE_DIM], ((0, 0), (0, 0), (0, HEAD_SLAB - NOPE_DIM)))
    v = w[:, :, NOPE_DIM:]
    return kn.reshape(KV_RANK, -1).astype(jnp.bfloat16), v.reshape(KV_RANK, -1).T.astype(jnp.bfloat16)


def _lane_row(fwd, bwd):
    return jnp.pad(jnp.concatenate([fwd, bwd]), (0, LANES - 2 * SSD_HEADS)).reshape(1, LANES)


def kernel(x_prompt, x_sample, c, cache_mla_ckv, cache_mla_krope, state_ssd_fwd, state_ssd_bwd, c_ctx, w_mod, b_mod, norm_pre_mix, norm_post_mix, norm_pre_ffn, norm_post_ffn, w_in_ab, q_norm, w_uq, kv_norm, w_ukv, ssd_conv_w, ssd_conv_b, ssd_dt_bias_fwd, ssd_dt_bias_bwd, ssd_a_log_fwd, ssd_a_log_bwd, ssd_d, ssd_norm, w_out_ab, pool_w, pool_scale, ffn_w_gate, ffn_w_up, ffn_w_down):
    f32, bf16 = jnp.float32, jnp.bfloat16
    assert DEPTH == 2
    xp = x_prompt.reshape(N_PROMPT, D_MODEL)
    xs = x_sample.reshape(N_SAMPLE, D_MODEL)
    cvecs = jnp.concatenate([c_ctx[None, :], c, jnp.zeros((MOD_ROWS - N_MODVEC, D_MODEL), f32)], axis=0)
    mod = _modulation(cvecs, w_mod, b_mod).reshape(DEPTH, SUBLANES, 1, 6 * D_MODEL)
    tabs = _rope_tables()
    hp = SSD_HEADS * SSD_HEAD_DIM
    row = lambda v: v.reshape(1, -1)
    new_ckv, new_kpe, new_hf, new_hb = [], [], [], []

    vec_table, vec_rows = _pack_vectors(
        norm_pre_mix=norm_pre_mix, norm_post_mix=norm_post_mix, norm_pre_ffn=norm_pre_ffn, norm_post_ffn=norm_post_ffn,
        q_norm=q_norm, kv_norm=kv_norm, ssd_norm=ssd_norm, ssd_conv_b=ssd_conv_b, pool_scale=pool_scale)
    ffn_w = (ffn_w_gate, ffn_w_up, ffn_w_down)
    for l in range(DEPTH):
        if l % 2 == 0:
            i = l // 2
            q, ckv_n, kpe, z, xbc, dts_t, cum_t, w_t, ckv_prompt, kpe_prompt = _inproj(
                l, vec_table, vec_rows, xp, xs, mod, tabs, _layout_in_proj(w_in_ab[i]), _layout_uq(w_uq[i]),
                _lane_row(ssd_dt_bias_fwd[i], ssd_dt_bias_bwd[i]).T, _lane_row(ssd_a_log_fwd[i], ssd_a_log_bwd[i]).T)
            heads = (dts_t, cum_t, w_t)
            w_kv = _layout_ukv(w_ukv[i])
            att_p = _attention(q, ckv_n, kpe, w_kv, 0, BATCH, SEQ)
            att_s = _attention(q, ckv_n, kpe, w_kv, N_PROMPT, DEC_BATCH, DEC_SEQ,
                               cache=(cache_mla_ckv[:, i], _kpe_slab(cache_mla_krope[:, i])))
            ssd_args = (ssd_conv_w, vec_table, vec_rows["ssd_conv_b"], row(jnp.repeat(ssd_d[i], SSD_HEAD_DIM)))
            y_p, hf, hb = _ssd(i, xbc, heads, None, *ssd_args, 0, BATCH, SEQ)
            y_s, _, _ = _ssd(i, xbc, heads, (state_ssd_fwd[:, i].reshape(DEC_BATCH, hp, SSD_STATE),
                                             state_ssd_bwd[:, i].reshape(DEC_BATCH, hp, SSD_STATE)),
                             *ssd_args, N_PROMPT, DEC_BATCH, DEC_SEQ)
            xa = _outproj_ffn(l, i, vec_table, vec_rows, xp, xs, att_p, att_s, y_p, y_s, z, mod, w_out_ab, *ffn_w)
            new_ckv.append(ckv_prompt.reshape(BATCH, SEQ, KV_RANK))
            new_kpe.append(kpe_prompt.reshape(BATCH, SEQ, ROPE_DIM))
            new_hf.append(hf.reshape(BATCH, SSD_HEADS, SSD_HEAD_DIM, SSD_STATE))
            new_hb.append(hb.reshape(BATCH, SSD_HEADS, SSD_HEAD_DIM, SSD_STATE))
        else:
            j = l // 2
            yp, ys = _pool_ffn(l, j, vec_table, vec_rows, xa, mod, pool_w[j].astype(bf16), *ffn_w)

    return (yp.reshape(BATCH, SEQ, D_MODEL), ys.reshape(DEC_BATCH, DEC_SEQ, D_MODEL),
            jnp.stack(new_ckv, axis=1), jnp.stack(new_kpe, axis=1),
            jnp.stack(new_hf, axis=1), jnp.stack(new_hb, axis=1))
```

```python
import functools

import numpy as np
import jax
import jax.numpy as jnp
from jax import lax
from jax.experimental import pallas as pl
from jax.experimental.pallas import tpu as pltpu

D_MODEL = 1024
BATCH = 16
SEQ = 256
DEPTH = 2
DEC_BATCH = 2
DEC_SEQ = 2048
PAST_LEN = 256
GRID_W = 64
EPS = 1e-6
MLA_HEADS = 8
Q_RANK = 256
KV_RANK = 256
NOPE_DIM = 64
ROPE_DIM = 32
V_DIM = 64
ROPE_THETA = 10000.0
SSD_HEADS = 8
SSD_GROUPS = 2
SSD_HPG = SSD_HEADS // SSD_GROUPS
SSD_HEAD_DIM = 64
SSD_STATE = 128
D_SSD = SSD_HEADS * SSD_HEAD_DIM
CONV_W = 5
CONV_CH = D_SSD + 2 * SSD_GROUPS * SSD_STATE
POOL_WINDOWS = (2, 4, 8, 16)
POOL_GC = D_MODEL // len(POOL_WINDOWS)
D_FF = ((8 * D_MODEL + 3 * 256 - 1) // (3 * 256)) * 256

SUBLANES = 8
LANES = 128

N_PROMPT = BATCH * SEQ
N_SAMPLE = DEC_BATCH * DEC_SEQ
N_TOK = N_PROMPT + N_SAMPLE
N_MODVEC = 1 + DEC_BATCH
TM = 512
TQ = 256
CHUNK = 128
HALO = SUBLANES
HEAD_SLAB = LANES
IN_COLS = Q_RANK + KV_RANK + D_SSD + CONV_CH + 2 * LANES
KPE_LANE0 = NOPE_DIM
VMEM_LIMIT = 56 * 1024 * 1024

POOL_SUB = min(SEQ, DEC_SEQ)

assert TM % POOL_SUB == 0 and SEQ % POOL_SUB == 0 and DEC_SEQ % TM == 0 and N_PROMPT % DEC_SEQ == 0


def _rms(x, g):
    return x * lax.rsqrt(jnp.mean(x * x, axis=-1, keepdims=True) + EPS) * g


def _silu(x):
    return x * jax.nn.sigmoid(x)


def _softplus(x):
    return jnp.maximum(x, 0.0) + jnp.log1p(jnp.exp(-jnp.abs(x)))


def _dot(a, b):
    return jnp.dot(a, b, preferred_element_type=jnp.float32)


def _dot_nt(a, b):
    return lax.dot_general(a, b, (((1,), (1,)), ((), ())), preferred_element_type=jnp.float32)


def _mod_row(i):
    return jnp.where(i < N_PROMPT // TM, 0, 1 + (i - N_PROMPT // TM) // (DEC_SEQ // TM))


def _const_spec(shape):
    nd = len(shape)
    return pl.BlockSpec(shape, lambda *_: (0,) * nd, pipeline_mode=pl.Buffered(1))


N_PT = N_PROMPT // TM
N_WCHUNK = 16


def _tile(step, lead):
    return jnp.maximum(step - lead, 0)


def _row_spec(width, lead=0):
    return pl.BlockSpec((TM, width), lambda s: (_tile(s, lead), 0))


def _mod_spec(l, lead=0):
    return pl.BlockSpec((None, 1, 1, 6 * D_MODEL), lambda s: (l, _mod_row(_tile(s, lead)), 0, 0))


def _vec_spec(row):
    return pl.BlockSpec((None, 1, D_MODEL), lambda *_: (row, 0, 0), pipeline_mode=pl.Buffered(1))


def _pack_vectors(**params):
    first_row, blocks, n = {}, [], 0
    for name, a in params.items():
        first_row[name] = n
        n += a.shape[0]
        blocks.append(jnp.pad(a, ((0, 0), (0, D_MODEL - a.shape[1]))))
    return jnp.concatenate(blocks, axis=0).reshape(n, 1, D_MODEL), first_row


def _group_specs(width, lead=0):
    return [pl.BlockSpec((TM, width), lambda s: (jnp.minimum(_tile(s, lead), N_PT - 1), 0)),
            pl.BlockSpec((TM, width), lambda s: (jnp.maximum(_tile(s, lead) - N_PT, 0), 0))]


def _wchunk_spec(rows, cols, layer=None):
    ck = rows // N_WCHUNK
    if layer is None:
        return pl.BlockSpec((ck, cols), lambda s: (jnp.minimum(s, N_WCHUNK - 1), 0))
    return pl.BlockSpec((1, ck, cols), lambda s: (layer, jnp.minimum(s, N_WCHUNK - 1), 0))


def _stage_weight(step, chunk, dst_ref):
    ck = chunk.shape[0]
    dst_ref[pl.ds(pl.multiple_of(step * ck, ck), ck), :] = chunk.astype(jnp.bfloat16)


def _pick_group(i, p_ref, s_ref, rows=slice(None)):
    return jnp.where(i < N_PT, p_ref[rows, :], s_ref[rows, :])


MOD_TK = 256
MOD_STREAMS = 2
PROMPT_PACK = 4
MOD_ROWS = 2 * SUBLANES
SUB_ROWS = 256


def _split3(a):
    a_hi = a.astype(jnp.bfloat16)
    r1 = a - a_hi.astype(jnp.float32)
    a_mid = r1.astype(jnp.bfloat16)
    a_lo = (r1 - a_mid.astype(jnp.float32)).astype(jnp.bfloat16)
    return a_hi, a_mid, a_lo


def _mod_kernel(c_ref, *refs):
    w_refs, b_ref, o_ref = refs[:MOD_STREAMS], refs[MOD_STREAMS], refs[MOD_STREAMS + 1]
    k = pl.program_id(1)
    rows = MOD_TK // MOD_STREAMS
    s_all = _silu(c_ref[...])
    part = jnp.zeros((MOD_ROWS, 6 * D_MODEL), jnp.float32)
    for j, w_ref in enumerate(w_refs):
        s_hi, s_mid, s_lo = _split3(s_all[:, j * rows:(j + 1) * rows])
        w = w_ref[0]
        w_hi = w.astype(jnp.bfloat16)
        w_lo = (w - w_hi.astype(jnp.float32)).astype(jnp.bfloat16)
        top = _dot(jnp.concatenate([s_hi, s_mid, s_lo], axis=0), w_hi)
        low = _dot(jnp.concatenate([s_hi, s_mid], axis=0), w_lo)
        part = part + (top[:MOD_ROWS] + top[MOD_ROWS:2 * MOD_ROWS] + top[2 * MOD_ROWS:]
                       + low[:MOD_ROWS] + low[MOD_ROWS:])
    part = part[:SUBLANES]

    @pl.when(k == 0)
    def _():
        o_ref[0] = part + b_ref[0]

    @pl.when(k > 0)
    def _():
        o_ref[0] += part


def _modulation(cvecs, w_mod, b_mod):
    n = 6 * D_MODEL
    w_spec = lambda j: pl.BlockSpec((1, MOD_TK // MOD_STREAMS, n), lambda l, k: (l, k * MOD_STREAMS + j, 0))
    return pl.pallas_call(
        _mod_kernel,
        grid=(DEPTH, D_MODEL // MOD_TK),
        in_specs=[pl.BlockSpec((MOD_ROWS, MOD_TK), lambda l, k: (0, k))]
        + [w_spec(j) for j in range(MOD_STREAMS)]
        + [pl.BlockSpec((1, 1, n), lambda l, k: (l, 0, 0))],
        out_specs=pl.BlockSpec((1, SUBLANES, n), lambda l, k: (l, 0, 0)),
        out_shape=jax.ShapeDtypeStruct((DEPTH, SUBLANES, n), jnp.float32),
        compiler_params=pltpu.CompilerParams(dimension_semantics=("arbitrary", "arbitrary"),
                                             vmem_limit_bytes=VMEM_LIMIT),
        name="modulation",
    )(cvecs, *([w_mod] * MOD_STREAMS), b_mod.reshape(DEPTH, 1, n))


def _ssd_head_terms(dt_raw, dtb_c, a_neg_c):
    nh2 = 2 * SSD_HEADS
    row = lax.broadcasted_iota(jnp.int32, (CHUNK, CHUNK), 0)
    col = lax.broadcasted_iota(jnp.int32, (CHUNK, CHUNK), 1)
    upper_b = (row <= col).astype(jnp.bfloat16)
    lower_b = (row >= col).astype(jnp.bfloat16)
    fwd_rows = lax.broadcasted_iota(jnp.int32, (nh2, CHUNK), 0) < SSD_HEADS
    dts_t = _softplus(dt_raw.T[:nh2, :] + dtb_c)
    pieces = _split3(dts_t * a_neg_c)
    cum_t = jnp.where(fwd_rows, sum(_dot(p, upper_b) for p in pieces), sum(_dot(p, lower_b) for p in pieces))
    cum_t = cum_t * np.float32(np.log2(np.e))
    tot = jnp.where(fwd_rows[:, :1], cum_t[:, CHUNK - 1:], cum_t[:, :1])
    return dts_t, cum_t, dts_t * jnp.exp2(tot - cum_t)


def _inproj_kernel(xp_ref, xs_ref, mod_ref, cos_ref, sin_ref, w_in_ref, w_uq_ref, qn_ref, kvn_ref, npm_ref,
                   dtbc_ref, alogc_ref, q_ref, ckv_ref, kpe_ref, z_ref, xbc_ref, dts_ref, cumt_ref, wt_ref,
                   new_ckv_ref, new_kpe_ref):
    i = pl.program_id(0)
    sh = mod_ref[0, :, 0:D_MODEL]
    sc = mod_ref[0, :, D_MODEL:2 * D_MODEL]
    scale = (NOPE_DIM + ROPE_DIM) ** -0.5 * np.log2(np.e)
    nh2 = 2 * SSD_HEADS
    dtb_c = dtbc_ref[:nh2, :]
    a_neg_c = -jnp.exp(alogc_ref[:nh2, :])
    dt_raw = []
    for r in range(TM // SUB_ROWS):
        rs = slice(r * SUB_ROWS, (r + 1) * SUB_ROWS)
        h = (_rms(_pick_group(i, xp_ref, xs_ref, rs), npm_ref[...]) * (1.0 + sc) + sh).astype(jnp.bfloat16)
        p = _dot(h, w_in_ref[...])
        o = 0
        cq = p[:, o:o + Q_RANK]; o += Q_RANK
        ckv = p[:, o:o + KV_RANK]; o += KV_RANK
        z_ref[rs, :] = p[:, o:o + D_SSD]; o += D_SSD
        xbc_ref[rs, :] = p[:, o:o + CONV_CH]; o += CONV_CH
        kpe = p[:, o:o + LANES]; o += LANES
        dt_raw.append(p[:, o:o + LANES])

        cos = cos_ref[rs, :]
        sin = sin_ref[rs, :]

        def rope(slab):
            return slab * cos + pltpu.roll(slab, LANES - ROPE_DIM, 1) * sin

        ckv_ref[rs, :] = _rms(ckv, kvn_ref[:, :KV_RANK])
        kpe_ref[rs, :] = rope(kpe)
        q = _dot(_rms(cq, qn_ref[:, :Q_RANK]).astype(jnp.bfloat16), w_uq_ref[...]) * scale
        for hd in range(MLA_HEADS):
            sl = slice(hd * HEAD_SLAB, (hd + 1) * HEAD_SLAB)
            q_ref[rs, sl] = rope(q[:, sl]).astype(jnp.bfloat16)

    dt_all = jnp.concatenate(dt_raw, axis=0)
    for ck in range(TM // CHUNK):
        terms = _ssd_head_terms(dt_all[ck * CHUNK:(ck + 1) * CHUNK, :], dtb_c, a_neg_c)
        for ref, val in zip((dts_ref, cumt_ref, wt_ref), terms):
            ref[ck * nh2:(ck + 1) * nh2, :] = val

    @pl.when(i < N_PT)
    def _():
        new_ckv_ref[...] = ckv_ref[...]
        new_kpe_ref[...] = kpe_ref[:, KPE_LANE0:KPE_LANE0 + ROPE_DIM]


def _inproj(l, vec_table, vec_rows, xp, xs, mod, tabs, w_in, w_uq, dt_bias_c, a_log_c):
    nt = N_TOK // TM
    prompt_blk = lambda i: (jnp.minimum(i, N_PT - 1), 0)
    lat_tiles = DEC_SEQ // TM
    tab_spec = pl.BlockSpec((TM, LANES), lambda i: (jnp.where(i < N_PT, lat_tiles, (i - N_PT) % lat_tiles), 0))
    row = _row_spec
    hrows = TM // CHUNK * 2 * SSD_HEADS
    head_spec = pl.BlockSpec((hrows, CHUNK), lambda i: (i, 0))
    head_shape = jax.ShapeDtypeStruct((nt * hrows, CHUNK), jnp.float32)
    return pl.pallas_call(
        _inproj_kernel,
        grid=(nt,),
        in_specs=_group_specs(D_MODEL) + [
            _mod_spec(l),
            tab_spec, tab_spec,
            _const_spec((D_MODEL, IN_COLS)),
            _const_spec((Q_RANK, MLA_HEADS * HEAD_SLAB)),
            _vec_spec(vec_rows["q_norm"] + l // 2),
            _vec_spec(vec_rows["kv_norm"] + l // 2),
            _vec_spec(vec_rows["norm_pre_mix"] + l),
            _const_spec((LANES, 1)), _const_spec((LANES, 1)),
        ],
        out_specs=[row(MLA_HEADS * HEAD_SLAB), row(KV_RANK), row(LANES), row(D_SSD), row(CONV_CH),
                   head_spec, head_spec, head_spec,
                   pl.BlockSpec((TM, KV_RANK), prompt_blk), pl.BlockSpec((TM, ROPE_DIM), prompt_blk)],
        out_shape=[
            jax.ShapeDtypeStruct((N_TOK, MLA_HEADS * HEAD_SLAB), jnp.bfloat16),
            jax.ShapeDtypeStruct((N_TOK, KV_RANK), jnp.float32),
            jax.ShapeDtypeStruct((N_TOK, LANES), jnp.float32),
            jax.ShapeDtypeStruct((N_TOK, D_SSD), jnp.float32),
            jax.ShapeDtypeStruct((N_TOK, CONV_CH), jnp.float32),
            head_shape, head_shape, head_shape,
            jax.ShapeDtypeStruct((N_PROMPT, KV_RANK), jnp.float32),
            jax.ShapeDtypeStruct((N_PROMPT, ROPE_DIM), jnp.float32),
        ],
        compiler_params=pltpu.CompilerParams(dimension_semantics=("arbitrary",), vmem_limit_bytes=VMEM_LIMIT),
        name="inproj",
    )(xp, xs, mod, *tabs, w_in, w_uq, vec_table, vec_table, vec_table, dt_bias_c, a_log_c)


def _attn_kernel(*refs, lk_cache, lk_new, n_pack):
    if lk_cache:
        q_ref, ckv_ref, kpe_ref, ckvc_ref, kpec_ref, wk_ref, wvt_ref, o_ref, k_scr, vt_scr = refs
    else:
        q_ref, ckv_ref, kpe_ref, wk_ref, wvt_ref, o_ref, k_scr, vt_scr = refs
    lk = lk_cache + lk_new

    @pl.when(pl.program_id(1) == 0)
    def _expand_kv():
        def expand(ckv, kpe, r0):
            ckv_b = ckv.astype(jnp.bfloat16)
            kn = _dot(ckv_b, wk_ref[...])
            rows = slice(r0, r0 + ckv.shape[0])
            for hd in range(MLA_HEADS):
                k_scr[hd, rows, :] = (kn[:, hd * HEAD_SLAB:(hd + 1) * HEAD_SLAB] + kpe).astype(jnp.bfloat16)
            vt_scr[:, rows] = _dot_nt(wvt_ref[...], ckv_b).astype(jnp.bfloat16)

        step = 256
        for r0 in range(0, lk_cache, step):
            expand(ckvc_ref[0, r0:r0 + step, :], kpec_ref[0, r0:r0 + step, :], r0)
        for r0 in range(0, n_pack * lk_new, step):
            expand(ckv_ref[r0:r0 + step, :], kpe_ref[r0:r0 + step, :], lk_cache + r0)

    work = [(s, hd) for s in range(n_pack) for hd in range(MLA_HEADS)]
    scores = [_dot_nt(k_scr[hd, s * lk:(s + 1) * lk, :], q_ref[s * TQ:(s + 1) * TQ, hd * HEAD_SLAB:(hd + 1) * HEAD_SLAB])
              for s, hd in work]
    outs = []
    for (s, hd), s_t in zip(work, scores):
        p_t = jnp.exp2(s_t - jnp.max(s_t, axis=0, keepdims=True))
        den = jnp.sum(p_t, axis=0, keepdims=True)
        v_t = vt_scr[hd * V_DIM:(hd + 1) * V_DIM, s * lk:(s + 1) * lk]
        outs.append(_dot(v_t, p_t.astype(jnp.bfloat16)) / den)
    for s in range(n_pack):
        o_ref[s * TQ:(s + 1) * TQ, :] = jnp.concatenate(
            outs[s * MLA_HEADS:(s + 1) * MLA_HEADS], axis=0).T.astype(jnp.bfloat16)


def _attention(q, ckv_n, kpe, w_ukv, row_off, n_batch, seq, cache=None, n_pack=1):
    nq = seq // TQ
    assert n_pack == 1 or (nq == 1 and cache is None and n_batch % n_pack == 0)
    n_batch //= n_pack
    lk_cache = 0 if cache is None else cache[0].shape[1]
    lk = n_pack * (lk_cache + seq)
    qblk = lambda b, qi: (row_off // (n_pack * TQ) + b * nq + qi, 0)
    sblk = lambda b, qi: (row_off // (n_pack * seq) + b, 0)
    in_specs = [
        pl.BlockSpec((n_pack * TQ, MLA_HEADS * HEAD_SLAB), qblk),
        pl.BlockSpec((n_pack * seq, KV_RANK), sblk),
        pl.BlockSpec((n_pack * seq, LANES), sblk),
    ]
    args = [q, ckv_n, kpe]
    if cache is not None:
        in_specs += [pl.BlockSpec((1, lk_cache, KV_RANK), lambda b, qi: (b, 0, 0)),
                     pl.BlockSpec((1, lk_cache, LANES), lambda b, qi: (b, 0, 0))]
        args += list(cache)
    in_specs += [_const_spec(w.shape) for w in w_ukv]
    args += list(w_ukv)
    return pl.pallas_call(
        functools.partial(_attn_kernel, lk_cache=lk_cache, lk_new=seq, n_pack=n_pack),
        grid=(n_batch, nq),
        in_specs=in_specs,
        out_specs=pl.BlockSpec((n_pack * TQ, MLA_HEADS * V_DIM), lambda b, qi: (b * nq + qi, 0)),
        out_shape=jax.ShapeDtypeStruct((n_batch * n_pack * seq, MLA_HEADS * V_DIM), jnp.bfloat16),
        scratch_shapes=[pltpu.VMEM((MLA_HEADS, lk, HEAD_SLAB), jnp.bfloat16),
                        pltpu.VMEM((MLA_HEADS * V_DIM, lk), jnp.bfloat16)],
        compiler_params=pltpu.CompilerParams(dimension_semantics=("arbitrary", "arbitrary"),
                                             vmem_limit_bytes=VMEM_LIMIT),
        name=f"attention_{seq}",
    )(*args)


def _ssd_kernel(*refs, seq, zero_init):
    if zero_init:
        (xbc_ref, dts_ref, cumt_ref, wt_ref, cw_ref, cb_ref, dsk_ref,
         y_ref, hf_ref, hb_ref, xs_scr, c_scr, bt_scr, cum_scr, stf_scr, stb_scr) = refs
    else:
        (xbc_ref, dts_ref, cumt_ref, wt_ref, h0f_ref, h0b_ref, cw_ref, cb_ref, dsk_ref,
         y_ref, hf_ref, hb_ref, xs_scr, c_scr, bt_scr, cum_scr, stf_scr, stb_scr) = refs
    nc = seq // CHUNK
    gs = SSD_GROUPS * SSD_STATE
    nh2 = 2 * SSD_HEADS

    row = lax.broadcasted_iota(jnp.int32, (CHUNK, CHUNK), 0)
    col = lax.broadcasted_iota(jnp.int32, (CHUNK, CHUNK), 1)
    low_half = col < SSD_HEAD_DIM
    lower = row >= col
    upper = row <= col

    def prep_chunk(c, carry):
        r0 = pl.multiple_of(c * CHUNK, CHUNK)
        rows = pl.ds(r0, CHUNK)
        rows_prev = pl.ds(pl.multiple_of(jnp.maximum(r0 - HALO, 0), HALO), HALO)
        rows_next = pl.ds(pl.multiple_of(jnp.minimum(r0 + CHUNK, seq - HALO), HALO), HALO)

        def conv_tile(cs):
            prev = jnp.where(c > 0, xbc_ref[rows_prev, cs], 0.0)
            nxt = jnp.where(c < nc - 1, xbc_ref[rows_next, cs], 0.0)
            win = jnp.concatenate([prev, xbc_ref[rows, cs], nxt], axis=0)
            acc = jnp.broadcast_to(cb_ref[:, cs], (CHUNK, LANES))
            for k in range(CONV_W):
                lo = HALO - CONV_W // 2 + k
                acc = acc + cw_ref[k:k + 1, cs] * win[lo:lo + CHUNK, :]
            return _silu(acc)

        def x_tile(j, carry):
            cs = pl.ds(pl.multiple_of(j * LANES, LANES), LANES)
            u = conv_tile(cs)
            y_ref[rows, cs] = dsk_ref[:, cs] * u
            xs_scr[rows, cs] = u.astype(jnp.bfloat16)
            return carry

        lax.fori_loop(0, D_SSD // LANES, x_tile, 0)
        cum_t = cumt_ref[pl.ds(pl.multiple_of(c * nh2, nh2), nh2), :]
        cum_scr[rows, :] = jnp.concatenate(
            [cum_t, jnp.zeros((CHUNK - nh2, CHUNK), jnp.float32)], axis=0).T[:, :nh2]
        for g in range(SSD_GROUPS):
            b0 = pl.multiple_of(c * gs + g * SSD_STATE, SSD_STATE)
            bt_scr[pl.ds(b0, SSD_STATE), :] = conv_tile(slice(D_SSD + g * SSD_STATE, D_SSD + (g + 1) * SSD_STATE)).T
            c_scr[rows, g * SSD_STATE:(g + 1) * SSD_STATE] = conv_tile(
                slice(D_SSD + gs + g * SSD_STATE, D_SSD + gs + (g + 1) * SSD_STATE)).astype(jnp.bfloat16)
        return carry

    lax.fori_loop(0, nc, prep_chunk, 0)

    if zero_init:
        stf_scr[...] = jnp.zeros_like(stf_scr)
        stb_scr[...] = jnp.zeros_like(stb_scr)
    else:
        stf_scr[...] = h0f_ref[0].T
        stb_scr[...] = h0b_ref[0].T

    def scan_open(ci, st_scr):
        rows = pl.ds(pl.multiple_of(ci * CHUNK, CHUNK), CHUNK)
        c_b = c_scr[rows, :]
        st = st_scr[...]
        bts, cbms, zs = [], [], []
        for g in range(SSD_GROUPS):
            cg = c_b[:, g * SSD_STATE:(g + 1) * SSD_STATE]
            bt = bt_scr[pl.ds(pl.multiple_of(ci * gs + g * SSD_STATE, SSD_STATE), SSD_STATE), :]
            gcols = slice(g * SSD_HPG * SSD_HEAD_DIM, (g + 1) * SSD_HPG * SSD_HEAD_DIM)
            bts.append(bt)
            cbms.append(_dot(cg, bt.astype(jnp.bfloat16)))
            zs.append(_dot(cg, st[:, gcols].astype(jnp.bfloat16)))
        return st, bts, cbms, zs

    def scan_pairs(ci, st_scr, reverse, opened):
        st, bts, cbms, zs = opened
        lane0 = SSD_HEADS if reverse else 0
        causal = upper if reverse else lower
        last = 0 if reverse else CHUNK - 1
        rows = pl.ds(pl.multiple_of(ci * CHUNK, CHUNK), CHUNK)
        hrows = pl.ds(pl.multiple_of(ci * nh2, nh2), nh2)
        xs_b = xs_scr[rows, :]
        dts_t = dts_ref[hrows, :]
        cum_t = cumt_ref[hrows, :]
        w_t = wt_ref[hrows, :]
        cum = cum_scr[rows, :]
        for pair in range(SSD_HEADS // 2):
            g, jj = divmod(pair, SSD_HPG // 2)
            pcols = slice(pair * LANES, (pair + 1) * LANES)
            lhs_y, lhs_s, entry = [], [], []
            for hd in (2 * pair, 2 * pair + 1):
                ln = lane0 + hd
                cum_i = jnp.broadcast_to(cum[:, ln:ln + 1], (CHUNK, CHUNK))
                dec = jnp.exp2(jnp.where(causal, cum_i - cum_t[ln:ln + 1, :], -jnp.inf))
                lhs_y.append((cbms[g] * dec * dts_t[ln:ln + 1, :]).astype(jnp.bfloat16))
                lhs_s.append((bts[g] * w_t[ln:ln + 1, :]).astype(jnp.bfloat16))
                entry.append(jnp.exp2(cum_i))
            out = _dot(jnp.concatenate(lhs_y + lhs_s, axis=0), xs_b[:, pcols])
            ea = jnp.where(low_half, entry[0], entry[1])
            y_ref[rows, pcols] += (jnp.where(low_half, out[:CHUNK], out[CHUNK:2 * CHUNK])
                                   + zs[g][:, jj * LANES:(jj + 1) * LANES] * ea)
            st_scr[:, pcols] = (ea[last:last + 1, :] * st[:, pcols]
                                + jnp.where(low_half, out[2 * CHUNK:3 * CHUNK], out[3 * CHUNK:]))

    def both(c, carry):
        opened_f = scan_open(c, stf_scr)
        opened_b = scan_open(nc - 1 - c, stb_scr)
        scan_pairs(c, stf_scr, False, opened_f)
        scan_pairs(nc - 1 - c, stb_scr, True, opened_b)
        return carry

    lax.fori_loop(0, nc, both, 0)
    hf_ref[0] = stf_scr[...].T
    hb_ref[0] = stb_scr[...].T


def _ssd(i_ab, xbc, head_terms, h0, conv_w, vec_table, conv_b_row, d_skip, row_off, n_batch, seq):
    assert CONV_CH == D_MODEL
    hp = SSD_HEADS * SSD_HEAD_DIM
    gs = SSD_GROUPS * SSD_STATE
    nc = seq // CHUNK
    seq_blk = lambda b: (row_off // seq + b, 0)
    head_spec = pl.BlockSpec((nc * 2 * SSD_HEADS, CHUNK), seq_blk)
    st_spec = pl.BlockSpec((1, hp, SSD_STATE), lambda b: (b, 0, 0))
    st_shape = jax.ShapeDtypeStruct((n_batch, hp, SSD_STATE), jnp.float32)
    h0 = () if h0 is None else tuple(h0)
    return pl.pallas_call(
        functools.partial(_ssd_kernel, seq=seq, zero_init=not h0),
        grid=(n_batch,),
        in_specs=[pl.BlockSpec((seq, CONV_CH), seq_blk), head_spec, head_spec, head_spec] + [st_spec] * len(h0) + [
            pl.BlockSpec((None, CONV_W, CONV_CH), lambda b: (i_ab, 0, 0), pipeline_mode=pl.Buffered(1)),
            _vec_spec(conv_b_row + i_ab), _const_spec((1, D_SSD))],
        out_specs=[pl.BlockSpec((seq, D_SSD), lambda b: (b, 0)), st_spec, st_spec],
        out_shape=[jax.ShapeDtypeStruct((n_batch * seq, D_SSD), jnp.float32), st_shape, st_shape],
        scratch_shapes=[pltpu.VMEM((seq, D_SSD), jnp.bfloat16),
                        pltpu.VMEM((seq, gs), jnp.bfloat16),
                        pltpu.VMEM((nc * gs, CHUNK), jnp.float32),
                        pltpu.VMEM((seq, 2 * SSD_HEADS), jnp.float32),
                        pltpu.VMEM((SSD_STATE, hp), jnp.float32),
                        pltpu.VMEM((SSD_STATE, hp), jnp.float32)],
        compiler_params=pltpu.CompilerParams(dimension_semantics=("arbitrary",), vmem_limit_bytes=VMEM_LIMIT),
        name=f"ssd_{seq}",
    )(xbc, *head_terms, *h0, conv_w, vec_table, d_skip)


def _post_mix(x, mix, mod_ref, npost_ref, npre_ref):
    d = D_MODEL
    gate_mix = mod_ref[0, :, 2 * d:3 * d]
    shf = mod_ref[0, :, 3 * d:4 * d]
    scf = mod_ref[0, :, 4 * d:5 * d]
    x1 = x + gate_mix * _rms(mix, npost_ref[...])
    return x1, (_rms(x1, npre_ref[...]) * (1.0 + scf) + shf).astype(jnp.bfloat16)


def _subtile_pipeline(n_sub, mixer_pre, mixer_dots, mod_ref, nffn_ref, wg_ref, wu_ref, wd_ref, interleave):
    gate_ffn = mod_ref[0, :, 5 * D_MODEL:6 * D_MODEL]
    ffn_up = lambda h: (_silu(_dot(h, wg_ref[...])) * _dot(h, wu_ref[...])).astype(jnp.bfloat16)
    ffn_down = lambda x1, hid: x1 + gate_ffn * _rms(_dot(hid, wd_ref[...]), nffn_ref[...])
    if not interleave:
        staged = [mixer_dots(r, mixer_pre(r)) for r in range(n_sub)]
        return [ffn_down(x1, ffn_up(h)) for x1, h in staged]
    outs = []
    x1, h = mixer_dots(0, mixer_pre(0))
    for r in range(n_sub):
        nxt_pre = mixer_pre(r + 1) if r + 1 < n_sub else None
        hid = ffn_up(h)
        nxt = mixer_dots(r + 1, nxt_pre) if r + 1 < n_sub else None
        outs.append(ffn_down(x1, hid))
        if nxt is not None:
            x1, h = nxt
    return outs


def _ffn_specs(l, vec_rows):
    return [_vec_spec(vec_rows[name] + l) for name in ("norm_post_mix", "norm_pre_ffn", "norm_post_ffn")] + [
        _wchunk_spec(D_MODEL, D_FF, l), _wchunk_spec(D_MODEL, D_FF, l), _wchunk_spec(D_FF, D_MODEL, l)]


def _ffn_scratch():
    return [pltpu.VMEM((D_MODEL, D_FF), jnp.bfloat16), pltpu.VMEM((D_MODEL, D_FF), jnp.bfloat16),
            pltpu.VMEM((D_FF, D_MODEL), jnp.bfloat16)]


def _outproj_ffn_kernel(xp_ref, xs_ref, attp_ref, atts_ref, yp_ref, ys_ref, z_ref, mod_ref, sn_ref, wo_ref,
                        npost_ref, npre_ref, nffn_ref, wg_ref, wu_ref, wd_ref, o_ref,
                        wo_scr, wg_scr, wu_scr, wd_scr):
    step = pl.program_id(0)

    @pl.when(step < N_WCHUNK)
    def _stage():
        _stage_weight(step, wo_ref[0], wo_scr)
        _stage_weight(step, wg_ref[0], wg_scr)
        _stage_weight(step, wu_ref[0], wu_scr)
        _stage_weight(step, wd_ref[0], wd_scr)

    @pl.when(step == 0)
    def _():
        o_ref[...] = jnp.zeros_like(o_ref)

    @pl.when(step >= N_WCHUNK)
    def _tile_step():
        i = step - N_WCHUNK
        gw = D_SSD // SSD_GROUPS
        sub = lambda r: slice(r * SUB_ROWS, (r + 1) * SUB_ROWS)

        def mixer_pre(r):
            yg = _pick_group(i, yp_ref, ys_ref, sub(r)) * _silu(z_ref[sub(r), :])
            parts = [_pick_group(i, attp_ref, atts_ref, sub(r))]
            for g in range(SSD_GROUPS):
                parts.append(_rms(yg[:, g * gw:(g + 1) * gw], sn_ref[:, g * gw:(g + 1) * gw]).astype(jnp.bfloat16))
            return jnp.concatenate(parts, axis=1)

        def mixer_dots(r, cat):
            return _post_mix(_pick_group(i, xp_ref, xs_ref, sub(r)), _dot(cat, wo_scr[...]),
                             mod_ref, npost_ref, npre_ref)

        outs = _subtile_pipeline(TM // SUB_ROWS, mixer_pre, mixer_dots, mod_ref, nffn_ref, wg_scr, wu_scr, wd_scr,
                                 interleave=False)
        for r, res in enumerate(outs):
            o_ref[sub(r), :] = res


def _outproj_ffn(l, i_ab, vec_table, vec_rows, xp, xs, att_p, att_s, y_p, y_s, z, mod, w_out, wg, wu, wd):
    lead = N_WCHUNK
    d_cat = MLA_HEADS * V_DIM + D_SSD
    return pl.pallas_call(
        _outproj_ffn_kernel,
        grid=(lead + N_TOK // TM,),
        in_specs=(_group_specs(D_MODEL, lead) + _group_specs(MLA_HEADS * V_DIM, lead) + _group_specs(D_SSD, lead)
                  + [_row_spec(D_SSD, lead), _mod_spec(l, lead), _vec_spec(vec_rows["ssd_norm"] + i_ab),
                     _wchunk_spec(d_cat, D_MODEL, i_ab)] + _ffn_specs(l, vec_rows)),
        out_specs=_row_spec(D_MODEL, lead),
        out_shape=jax.ShapeDtypeStruct((N_TOK, D_MODEL), jnp.float32),
        scratch_shapes=[pltpu.VMEM((d_cat, D_MODEL), jnp.bfloat16)] + _ffn_scratch(),
        compiler_params=pltpu.CompilerParams(dimension_semantics=("arbitrary",), vmem_limit_bytes=VMEM_LIMIT),
        name="outproj_ffn",
    )(xp, xs, att_p, att_s, y_p, y_s, z, mod, vec_table, w_out, vec_table, vec_table, vec_table, wg, wu, wd)


def _pool_ffn_kernel(x_ref, xp_ref, xn_ref, mod_ref, nmix_ref, pw_ref, ps_ref,
                     npost_ref, npre_ref, nffn_ref, wg_ref, wu_ref, wd_ref, op_ref, os_ref,
                     wg_scr, wu_scr, wd_scr):
    step = pl.program_id(0)

    @pl.when(step < N_WCHUNK)
    def _stage():
        _stage_weight(step, wg_ref[0], wg_scr)
        _stage_weight(step, wu_ref[0], wu_scr)
        _stage_weight(step, wd_ref[0], wd_scr)

    @pl.when(step == 0)
    def _():
        op_ref[...] = jnp.zeros_like(op_ref)
        os_ref[...] = jnp.zeros_like(os_ref)

    @pl.when(step >= N_WCHUNK)
    def _tile_step():
        _pool_ffn_tile(step - N_WCHUNK, x_ref, xp_ref, xn_ref, mod_ref, nmix_ref, pw_ref, ps_ref,
                       npost_ref, npre_ref, nffn_ref, wg_scr, wu_scr, wd_scr, op_ref, os_ref)


def _pool_ffn_tile(i, x_ref, xp_ref, xn_ref, mod_ref, nmix_ref, pw_ref, ps_ref,
                   npost_ref, npre_ref, nffn_ref, wg_ref, wu_ref, wd_ref, op_ref, os_ref):
    seq = jnp.where(i < N_PROMPT // TM, SEQ, DEC_SEQ)
    pos0 = (i * TM) % seq
    sh = mod_ref[0, :, 0:D_MODEL]
    sc = mod_ref[0, :, D_MODEL:2 * D_MODEL]
    hmod = lambda v: _rms(v, nmix_ref[...]) * (1.0 + sc) + sh
    n_rows = POOL_SUB + 2 * HALO

    def shifted(v, s):
        return pltpu.roll(v, n_rows - s, 0)

    def mixer_pre(s):
        lo, hi = s * POOL_SUB, (s + 1) * POOL_SUB
        pos_s = (pos0 + lo) % seq
        h = hmod(x_ref[lo:hi, :])
        before = hmod(xp_ref[...] if s == 0 else x_ref[lo - HALO:lo, :])
        after = hmod(xn_ref[...] if hi == TM else x_ref[hi:hi + HALO, :])
        before = jnp.where(pos_s > 0, before, 0.0)
        after = jnp.where(pos_s + POOL_SUB < seq, after, 0.0)
        padded = jnp.concatenate([before, h, after], axis=0)
        pos = pos_s + lax.broadcasted_iota(jnp.int32, (POOL_SUB, 1), 0)
        pooled = []
        for gi, w in enumerate(POOL_WINDOWS):
            cols = slice(gi * POOL_GC, (gi + 1) * POOL_GC)
            t = padded[:, cols]
            span = 1
            while span < w:
                t = t + shifted(t, span)
                span *= 2
            lead = HALO - w // 2
            win_sum = (shifted(t, lead) if lead else t)[:POOL_SUB, :]
            cnt = (jnp.minimum(pos + w // 2, seq) - jnp.maximum(pos - w // 2, 0)).astype(jnp.float32)
            pooled.append((win_sum / cnt - h[:, cols]).astype(jnp.bfloat16))
        return pooled

    def mixer_dots(s, pooled):
        mix = jnp.concatenate([_dot(p, pw_ref[gi]) for gi, p in enumerate(pooled)], axis=1) * ps_ref[...]
        return _post_mix(x_ref[s * POOL_SUB:(s + 1) * POOL_SUB, :], mix, mod_ref, npost_ref, npre_ref)

    res = jnp.concatenate(_subtile_pipeline(TM // POOL_SUB, mixer_pre, mixer_dots, mod_ref, nffn_ref,
                                            wg_ref, wu_ref, wd_ref, interleave=True), axis=0)

    @pl.when(i < N_PT)
    def _():
        op_ref[...] = res

    @pl.when(i >= N_PT)
    def _():
        os_ref[...] = res


def _pool_ffn(l, j_c, vec_table, vec_rows, xa, mod, pool_w, wg, wu, wd):
    lead = N_WCHUNK
    hb = TM // HALO
    nh = N_TOK // HALO
    return pl.pallas_call(
        _pool_ffn_kernel,
        grid=(lead + N_TOK // TM,),
        in_specs=[_row_spec(D_MODEL, lead),
                  pl.BlockSpec((HALO, D_MODEL), lambda s: (jnp.maximum(_tile(s, lead) * hb - 1, 0), 0)),
                  pl.BlockSpec((HALO, D_MODEL), lambda s: (jnp.minimum((_tile(s, lead) + 1) * hb, nh - 1), 0)),
                  _mod_spec(l, lead),
                  _vec_spec(vec_rows["norm_pre_mix"] + l),
                  _const_spec((len(POOL_WINDOWS), POOL_GC, POOL_GC)),
                  _vec_spec(vec_rows["pool_scale"] + j_c)] + _ffn_specs(l, vec_rows),
        out_specs=_group_specs(D_MODEL, lead),
        out_shape=[jax.ShapeDtypeStruct((N_PROMPT, D_MODEL), jnp.float32),
                   jax.ShapeDtypeStruct((N_SAMPLE, D_MODEL), jnp.float32)],
        scratch_shapes=_ffn_scratch(),
        compiler_params=pltpu.CompilerParams(dimension_semantics=("arbitrary",), vmem_limit_bytes=VMEM_LIMIT),
        name="pool_ffn",
    )(xa, xa, xa, mod, vec_table, pool_w, vec_table, vec_table, vec_table, vec_table, wg, wu, wd)


def _rope_tables():
    f32 = np.float32
    rows = DEC_SEQ // GRID_W
    r = np.repeat(np.arange(rows, dtype=f32), GRID_W)
    c = np.tile(np.arange(GRID_W, dtype=f32), rows)
    half = ROPE_DIM // 2
    inv_freq = np.power(f32(ROPE_THETA), -np.arange(0, half, 2, dtype=f32) / f32(half)).astype(f32)
    ang = np.concatenate([r[:, None] * inv_freq, c[:, None] * inv_freq], axis=-1).astype(f32)
    cos, sin = np.cos(ang).astype(f32), np.sin(ang).astype(f32)
    ones = np.ones((DEC_SEQ, KPE_LANE0), f32)
    zl = np.zeros((DEC_SEQ, KPE_LANE0), f32)
    zr = np.zeros((DEC_SEQ, LANES - KPE_LANE0 - ROPE_DIM), f32)
    cos_t = np.concatenate([ones, cos, cos, zr], axis=1)
    sin_t = np.concatenate([zl, -sin, sin, zr], axis=1)
    keep = np.concatenate([np.ones((TM, KPE_LANE0 + ROPE_DIM), f32), np.zeros((TM, LANES - KPE_LANE0 - ROPE_DIM), f32)],
                          axis=1)
    return (jnp.asarray(np.concatenate([cos_t, keep], axis=0)),
            jnp.asarray(np.concatenate([sin_t, np.zeros((TM, LANES), f32)], axis=0)))


def _kpe_slab(k):
    pad = [(0, 0)] * (k.ndim - 1) + [(KPE_LANE0, LANES - KPE_LANE0 - ROPE_DIM)]
    return jnp.pad(k, pad)


def _with_swapped_pair(w_pe):
    half = ROPE_DIM // 2
    return jnp.concatenate([w_pe, w_pe[..., half:], w_pe[..., :half]], axis=-1)


def _layout_in_proj(w):
    o = np.cumsum((0, Q_RANK, KV_RANK, ROPE_DIM, D_SSD, CONV_CH, SSD_HEADS, SSD_HEADS))
    cq, ckv, kpe, z, xbc, dtf, dtb = (w[:, o[k]:o[k + 1]] for k in range(7))
    dt = jnp.pad(jnp.concatenate([dtf, dtb], axis=1), ((0, 0), (0, LANES - 2 * SSD_HEADS)))
    kpe_slab = jnp.pad(_with_swapped_pair(kpe), ((0, 0), (KPE_LANE0, LANES - KPE_LANE0 - 2 * ROPE_DIM)))
    return jnp.concatenate([cq, ckv, z, xbc, kpe_slab, dt], axis=1).astype(jnp.bfloat16)


def _layout_uq(w):
    w = w.reshape(Q_RANK, MLA_HEADS, NOPE_DIM + ROPE_DIM)
    w = jnp.concatenate([w[:, :, :NOPE_DIM], _with_swapped_pair(w[:, :, NOPE_DIM:])], axis=-1)
    assert w.shape[-1] == HEAD_SLAB
    return w.reshape(Q_RANK, MLA_HEADS * HEAD_SLAB).astype(jnp.bfloat16)


def _layout_ukv(w):
    w = w.reshape(KV_RANK, MLA_HEADS, NOPE_DIM + V_DIM)
    kn = jnp.pad(w[:, :, :NOPE_DIM], ((0, 0), (0, 0), (0, HEAD_SLAB - NOPE_DIM)))
    v = w[:, :, NOPE_DIM:]
    return kn.reshape(KV_RANK, -1).astype(jnp.bfloat16), v.reshape(KV_RANK, -1).T.astype(jnp.bfloat16)


def _lane_row(fwd, bwd):
    return jnp.pad(jnp.concatenate([fwd, bwd]), (0, LANES - 2 * SSD_HEADS)).reshape(1, LANES)


def kernel(x_prompt, x_sample, c, cache_mla_ckv, cache_mla_krope, state_ssd_fwd, state_ssd_bwd, c_ctx, w_mod, b_mod, norm_pre_mix, norm_post_mix, norm_pre_ffn, norm_post_ffn, w_in_ab, q_norm, w_uq, kv_norm, w_ukv, ssd_conv_w, ssd_conv_b, ssd_dt_bias_fwd, ssd_dt_bias_bwd, ssd_a_log_fwd, ssd_a_log_bwd, ssd_d, ssd_norm, w_out_ab, pool_w, pool_scale, ffn_w_gate, ffn_w_up, ffn_w_down):
    f32, bf16 = jnp.float32, jnp.bfloat16
    assert DEPTH == 2
    xp = x_prompt.reshape(N_PROMPT, D_MODEL)
    xs = x_sample.reshape(N_SAMPLE, D_MODEL)
    cvecs = jnp.concatenate([c_ctx[None, :], c, jnp.zeros((MOD_ROWS - N_MODVEC, D_MODEL), f32)], axis=0)
    mod = _modulation(cvecs, w_mod, b_mod).reshape(DEPTH, SUBLANES, 1, 6 * D_MODEL)
    tabs = _rope_tables()
    hp = SSD_HEADS * SSD_HEAD_DIM
    row = lambda v: v.reshape(1, -1)
    new_ckv, new_kpe, new_hf, new_hb = [], [], [], []

    vec_table, vec_rows = _pack_vectors(
        norm_pre_mix=norm_pre_mix, norm_post_mix=norm_post_mix, norm_pre_ffn=norm_pre_ffn, norm_post_ffn=norm_post_ffn,
        q_norm=q_norm, kv_norm=kv_norm, ssd_norm=ssd_norm, ssd_conv_b=ssd_conv_b, pool_scale=pool_scale)
    ffn_w = (ffn_w_gate, ffn_w_up, ffn_w_down)
    for l in range(DEPTH):
        if l % 2 == 0:
            i = l // 2
            q, ckv_n, kpe, z, xbc, dts_t, cum_t, w_t, ckv_prompt, kpe_prompt = _inproj(
                l, vec_table, vec_rows, xp, xs, mod, tabs, _layout_in_proj(w_in_ab[i]), _layout_uq(w_uq[i]),
                _lane_row(ssd_dt_bias_fwd[i], ssd_dt_bias_bwd[i]).T, _lane_row(ssd_a_log_fwd[i], ssd_a_log_bwd[i]).T)
            heads = (dts_t, cum_t, w_t)
            w_kv = _layout_ukv(w_ukv[i])
            att_p = _attention(q, ckv_n, kpe, w_kv, 0, BATCH, SEQ, n_pack=PROMPT_PACK)
            att_s = _attention(q, ckv_n, kpe, w_kv, N_PROMPT, DEC_BATCH, DEC_SEQ,
                               cache=(cache_mla_ckv[:, i], _kpe_slab(cache_mla_krope[:, i])))
            ssd_args = (ssd_conv_w, vec_table, vec_rows["ssd_conv_b"], row(jnp.repeat(ssd_d[i], SSD_HEAD_DIM)))
            y_p, hf, hb = _ssd(i, xbc, heads, None, *ssd_args, 0, BATCH, SEQ)
            y_s, _, _ = _ssd(i, xbc, heads, (state_ssd_fwd[:, i].reshape(DEC_BATCH, hp, SSD_STATE),
                                             state_ssd_bwd[:, i].reshape(DEC_BATCH, hp, SSD_STATE)),
                             *ssd_args, N_PROMPT, DEC_BATCH, DEC_SEQ)
            xa = _outproj_ffn(l, i, vec_table, vec_rows, xp, xs, att_p, att_s, y_p, y_s, z, mod, w_out_ab, *ffn_w)
            new_ckv.append(ckv_prompt.reshape(BATCH, SEQ, KV_RANK))
            new_kpe.append(kpe_prompt.reshape(BATCH, SEQ, ROPE_DIM))
            new_hf.append(hf.reshape(BATCH, SSD_HEADS, SSD_HEAD_DIM, SSD_STATE))
            new_hb.append(hb.reshape(BATCH, SSD_HEADS, SSD_HEAD_DIM, SSD_STATE))
        else:
            j = l // 2
            yp, ys = _pool_ffn(l, j, vec_table, vec_rows, xa, mod, pool_w[j].astype(bf16), *ffn_w)

    return (yp.reshape(BATCH, SEQ, D_MODEL), ys.reshape(DEC_BATCH, DEC_SEQ, D_MODEL),
            jnp.stack(new_ckv, axis=1), jnp.stack(new_kpe, axis=1),
            jnp.stack(new_hf, axis=1), jnp.stack(new_hb, axis=1))
```

```python
import functools

import numpy as np
import jax
import jax.numpy as jnp
from jax import lax
from jax.experimental import pallas as pl
from jax.experimental.pallas import tpu as pltpu

D_MODEL = 1024
BATCH = 16
SEQ = 256
DEPTH = 2
DEC_BATCH = 2
DEC_SEQ = 2048
PAST_LEN = 256
GRID_W = 64
EPS = 1e-6
MLA_HEADS = 8
Q_RANK = 256
KV_RANK = 256
NOPE_DIM = 64
ROPE_DIM = 32
V_DIM = 64
ROPE_THETA = 10000.0
SSD_HEADS = 8
SSD_GROUPS = 2
SSD_HPG = SSD_HEADS // SSD_GROUPS
SSD_HEAD_DIM = 64
SSD_STATE = 128
D_SSD = SSD_HEADS * SSD_HEAD_DIM
CONV_W = 5
CONV_CH = D_SSD + 2 * SSD_GROUPS * SSD_STATE
POOL_WINDOWS = (2, 4, 8, 16)
POOL_GC = D_MODEL // len(POOL_WINDOWS)
D_FF = ((8 * D_MODEL + 3 * 256 - 1) // (3 * 256)) * 256

SUBLANES = 8
LANES = 128

N_PROMPT = BATCH * SEQ
N_SAMPLE = DEC_BATCH * DEC_SEQ
N_TOK = N_PROMPT + N_SAMPLE
N_MODVEC = 1 + DEC_BATCH
TM = 512
TQ = 256
CHUNK = 128
HALO = SUBLANES
HEAD_SLAB = LANES
IN_COLS = Q_RANK + KV_RANK + D_SSD + CONV_CH + 2 * LANES
KPE_LANE0 = NOPE_DIM
VMEM_LIMIT = 56 * 1024 * 1024

POOL_SUB = min(SEQ, DEC_SEQ)

assert TM % POOL_SUB == 0 and SEQ % POOL_SUB == 0 and DEC_SEQ % TM == 0 and N_PROMPT % DEC_SEQ == 0


def _rms(x, g):
    return x * lax.rsqrt(jnp.mean(x * x, axis=-1, keepdims=True) + EPS) * g


def _silu(x):
    return x * jax.nn.sigmoid(x)


def _softplus(x):
    return jnp.maximum(x, 0.0) + jnp.log1p(jnp.exp(-jnp.abs(x)))


def _dot(a, b):
    return jnp.dot(a, b, preferred_element_type=jnp.float32)


def _dot_nt(a, b):
    return lax.dot_general(a, b, (((1,), (1,)), ((), ())), preferred_element_type=jnp.float32)


def _mod_row(i):
    return jnp.where(i < N_PROMPT // TM, 0, 1 + (i - N_PROMPT // TM) // (DEC_SEQ // TM))


def _const_spec(shape):
    nd = len(shape)
    return pl.BlockSpec(shape, lambda *_: (0,) * nd, pipeline_mode=pl.Buffered(1))


N_PT = N_PROMPT // TM
N_TILES = N_TOK // TM


def _row_spec(width):
    return pl.BlockSpec((TM, width), lambda i: (i, 0))


def _mod_spec(l):
    return pl.BlockSpec((None, 1, 1, 6 * D_MODEL), lambda i: (l, _mod_row(i), 0, 0))


def _vec_spec(row):
    return pl.BlockSpec((None, 1, D_MODEL), lambda *_: (row, 0, 0), pipeline_mode=pl.Buffered(1))


def _pack_vectors(**params):
    first_row, blocks, n = {}, [], 0
    for name, a in params.items():
        first_row[name] = n
        n += a.shape[0]
        blocks.append(jnp.pad(a, ((0, 0), (0, D_MODEL - a.shape[1]))))
    return jnp.concatenate(blocks, axis=0).reshape(n, 1, D_MODEL), first_row


def _group_specs(width):
    return [pl.BlockSpec((TM, width), lambda i: (jnp.minimum(i, N_PT - 1), 0)),
            pl.BlockSpec((TM, width), lambda i: (jnp.maximum(i - N_PT, 0), 0))]


def _cast_in_spec(w, layer):
    _, rows, cols = w.shape
    return pl.BlockSpec((1, rows // N_TILES, cols), lambda i: (layer, i, 0))


def _cast_out(w):
    _, rows, cols = w.shape
    return pl.BlockSpec((rows // N_TILES, cols), lambda i: (i, 0)), jax.ShapeDtypeStruct((rows, cols), jnp.bfloat16)


def _cast_chunks(src_refs, dst_refs):
    for src, dst in zip(src_refs, dst_refs):
        dst[...] = src[0].astype(jnp.bfloat16)


def _pick_group(i, p_ref, s_ref, rows=slice(None)):
    return jnp.where(i < N_PT, p_ref[rows, :], s_ref[rows, :])


MOD_TK = 256
MOD_STREAMS = 2
PROMPT_PACK = 4
MOD_ROWS = 2 * SUBLANES
SUB_ROWS = 256


def _split3(a):
    a_hi = a.astype(jnp.bfloat16)
    r1 = a - a_hi.astype(jnp.float32)
    a_mid = r1.astype(jnp.bfloat16)
    a_lo = (r1 - a_mid.astype(jnp.float32)).astype(jnp.bfloat16)
    return a_hi, a_mid, a_lo


def _mod_kernel(c_ref, *refs):
    w_refs, b_ref, o_ref = refs[:MOD_STREAMS], refs[MOD_STREAMS], refs[MOD_STREAMS + 1]
    k = pl.program_id(1)
    rows = MOD_TK // MOD_STREAMS
    s_all = _silu(c_ref[...])
    part = jnp.zeros((MOD_ROWS, 6 * D_MODEL), jnp.float32)
    for j, w_ref in enumerate(w_refs):
        s_hi, s_mid, s_lo = _split3(s_all[:, j * rows:(j + 1) * rows])
        w = w_ref[0]
        w_hi = w.astype(jnp.bfloat16)
        w_lo = (w - w_hi.astype(jnp.float32)).astype(jnp.bfloat16)
        top = _dot(jnp.concatenate([s_hi, s_mid, s_lo], axis=0), w_hi)
        low = _dot(jnp.concatenate([s_hi, s_mid], axis=0), w_lo)
        part = part + (top[:MOD_ROWS] + top[MOD_ROWS:2 * MOD_ROWS] + top[2 * MOD_ROWS:]
                       + low[:MOD_ROWS] + low[MOD_ROWS:])
    part = part[:SUBLANES]

    @pl.when(k == 0)
    def _():
        o_ref[0] = part + b_ref[0]

    @pl.when(k > 0)
    def _():
        o_ref[0] += part


def _modulation(cvecs, w_mod, b_mod):
    n = 6 * D_MODEL
    w_spec = lambda j: pl.BlockSpec((1, MOD_TK // MOD_STREAMS, n), lambda l, k: (l, k * MOD_STREAMS + j, 0))
    return pl.pallas_call(
        _mod_kernel,
        grid=(DEPTH, D_MODEL // MOD_TK),
        in_specs=[pl.BlockSpec((MOD_ROWS, MOD_TK), lambda l, k: (0, k))]
        + [w_spec(j) for j in range(MOD_STREAMS)]
        + [pl.BlockSpec((1, 1, n), lambda l, k: (l, 0, 0))],
        out_specs=pl.BlockSpec((1, SUBLANES, n), lambda l, k: (l, 0, 0)),
        out_shape=jax.ShapeDtypeStruct((DEPTH, SUBLANES, n), jnp.float32),
        compiler_params=pltpu.CompilerParams(dimension_semantics=("arbitrary", "arbitrary"),
                                             vmem_limit_bytes=VMEM_LIMIT),
        name="modulation",
    )(cvecs, *([w_mod] * MOD_STREAMS), b_mod.reshape(DEPTH, 1, n))


def _ssd_head_terms(dt_raw, dtb_c, a_neg_c):
    nh2 = 2 * SSD_HEADS
    row = lax.broadcasted_iota(jnp.int32, (CHUNK, CHUNK), 0)
    col = lax.broadcasted_iota(jnp.int32, (CHUNK, CHUNK), 1)
    upper_b = (row <= col).astype(jnp.bfloat16)
    lower_b = (row >= col).astype(jnp.bfloat16)
    fwd_rows = lax.broadcasted_iota(jnp.int32, (nh2, CHUNK), 0) < SSD_HEADS
    dts_t = _softplus(dt_raw.T[:nh2, :] + dtb_c)
    pieces = _split3(dts_t * a_neg_c)
    cum_t = jnp.where(fwd_rows, sum(_dot(p, upper_b) for p in pieces), sum(_dot(p, lower_b) for p in pieces))
    cum_t = cum_t * np.float32(np.log2(np.e))
    tot = jnp.where(fwd_rows[:, :1], cum_t[:, CHUNK - 1:], cum_t[:, :1])
    return dts_t, cum_t, dts_t * jnp.exp2(tot - cum_t)


def _inproj_kernel(*refs, n_cast):
    (xp_ref, xs_ref, mod_ref, cos_ref, sin_ref, w_in_ref, w_uq_ref, qn_ref, kvn_ref, npm_ref,
     dtbc_ref, alogc_ref) = refs[:12]
    cast_src = refs[12:12 + n_cast]
    (q_ref, ckv_ref, kpe_ref, z_ref, xbc_ref, dts_ref, cumt_ref, wt_ref,
     new_ckv_ref, new_kpe_ref) = refs[12 + n_cast:22 + n_cast]
    cast_dst = refs[22 + n_cast:]
    i = pl.program_id(0)
    _cast_chunks(cast_src, cast_dst)
    sh = mod_ref[0, :, 0:D_MODEL]
    sc = mod_ref[0, :, D_MODEL:2 * D_MODEL]
    scale = (NOPE_DIM + ROPE_DIM) ** -0.5 * np.log2(np.e)
    nh2 = 2 * SSD_HEADS
    dtb_c = dtbc_ref[:nh2, :]
    a_neg_c = -jnp.exp(alogc_ref[:nh2, :])
    dt_raw = []
    for r in range(TM // SUB_ROWS):
        rs = slice(r * SUB_ROWS, (r + 1) * SUB_ROWS)
        h = (_rms(_pick_group(i, xp_ref, xs_ref, rs), npm_ref[...]) * (1.0 + sc) + sh).astype(jnp.bfloat16)
        p = _dot(h, w_in_ref[...])
        o = 0
        cq = p[:, o:o + Q_RANK]; o += Q_RANK
        ckv = p[:, o:o + KV_RANK]; o += KV_RANK
        z_ref[rs, :] = p[:, o:o + D_SSD]; o += D_SSD
        xbc_ref[rs, :] = p[:, o:o + CONV_CH]; o += CONV_CH
        kpe = p[:, o:o + LANES]; o += LANES
        dt_raw.append(p[:, o:o + LANES])

        cos = cos_ref[rs, :]
        sin = sin_ref[rs, :]

        def rope(slab):
            return slab * cos + pltpu.roll(slab, LANES - ROPE_DIM, 1) * sin

        ckv_ref[rs, :] = _rms(ckv, kvn_ref[:, :KV_RANK])
        kpe_ref[rs, :] = rope(kpe)
        q = _dot(_rms(cq, qn_ref[:, :Q_RANK]).astype(jnp.bfloat16), w_uq_ref[...]) * scale
        for hd in range(MLA_HEADS):
            sl = slice(hd * HEAD_SLAB, (hd + 1) * HEAD_SLAB)
            q_ref[rs, sl] = rope(q[:, sl]).astype(jnp.bfloat16)

    dt_all = jnp.concatenate(dt_raw, axis=0)
    for ck in range(TM // CHUNK):
        terms = _ssd_head_terms(dt_all[ck * CHUNK:(ck + 1) * CHUNK, :], dtb_c, a_neg_c)
        for ref, val in zip((dts_ref, cumt_ref, wt_ref), terms):
            ref[ck * nh2:(ck + 1) * nh2, :] = val

    @pl.when(i < N_PT)
    def _():
        new_ckv_ref[...] = ckv_ref[...]
        new_kpe_ref[...] = kpe_ref[:, KPE_LANE0:KPE_LANE0 + ROPE_DIM]


def _inproj(l, vec_table, vec_rows, xp, xs, mod, tabs, w_in, w_uq, dt_bias_c, a_log_c, cast):
    nt = N_TILES
    cast_out = [_cast_out(w) for w, _ in cast]
    prompt_blk = lambda i: (jnp.minimum(i, N_PT - 1), 0)
    lat_tiles = DEC_SEQ // TM
    tab_spec = pl.BlockSpec((TM, LANES), lambda i: (jnp.where(i < N_PT, lat_tiles, (i - N_PT) % lat_tiles), 0))
    row = _row_spec
    hrows = TM // CHUNK * 2 * SSD_HEADS
    head_spec = pl.BlockSpec((hrows, CHUNK), lambda i: (i, 0))
    head_shape = jax.ShapeDtypeStruct((nt * hrows, CHUNK), jnp.float32)
    return pl.pallas_call(
        functools.partial(_inproj_kernel, n_cast=len(cast)),
        grid=(nt,),
        in_specs=_group_specs(D_MODEL) + [
            _mod_spec(l),
            tab_spec, tab_spec,
            _const_spec((D_MODEL, IN_COLS)),
            _const_spec((Q_RANK, MLA_HEADS * HEAD_SLAB)),
            _vec_spec(vec_rows["q_norm"] + l // 2),
            _vec_spec(vec_rows["kv_norm"] + l // 2),
            _vec_spec(vec_rows["norm_pre_mix"] + l),
            _const_spec((LANES, 1)), _const_spec((LANES, 1)),
        ] + [_cast_in_spec(w, layer) for w, layer in cast],
        out_specs=[row(MLA_HEADS * HEAD_SLAB), row(KV_RANK), row(LANES), row(D_SSD), row(CONV_CH),
                   head_spec, head_spec, head_spec,
                   pl.BlockSpec((TM, KV_RANK), prompt_blk), pl.BlockSpec((TM, ROPE_DIM), prompt_blk)]
        + [spec for spec, _ in cast_out],
        out_shape=[
            jax.ShapeDtypeStruct((N_TOK, MLA_HEADS * HEAD_SLAB), jnp.bfloat16),
            jax.ShapeDtypeStruct((N_TOK, KV_RANK), jnp.float32),
            jax.ShapeDtypeStruct((N_TOK, LANES), jnp.float32),
            jax.ShapeDtypeStruct((N_TOK, D_SSD), jnp.float32),
            jax.ShapeDtypeStruct((N_TOK, CONV_CH), jnp.float32),
            head_shape, head_shape, head_shape,
            jax.ShapeDtypeStruct((N_PROMPT, KV_RANK), jnp.float32),
            jax.ShapeDtypeStruct((N_PROMPT, ROPE_DIM), jnp.float32),
        ] + [shape for _, shape in cast_out],
        compiler_params=pltpu.CompilerParams(dimension_semantics=("arbitrary",), vmem_limit_bytes=VMEM_LIMIT),
        name="inproj",
    )(xp, xs, mod, *tabs, w_in, w_uq, vec_table, vec_table, vec_table, dt_bias_c, a_log_c, *[w for w, _ in cast])


def _attn_kernel(*refs, lk_cache, lk_new, n_pack):
    if lk_cache:
        q_ref, ckv_ref, kpe_ref, ckvc_ref, kpec_ref, wk_ref, wvt_ref, o_ref, k_scr, vt_scr = refs
    else:
        q_ref, ckv_ref, kpe_ref, wk_ref, wvt_ref, o_ref, k_scr, vt_scr = refs
    lk = lk_cache + lk_new

    @pl.when(pl.program_id(1) == 0)
    def _expand_kv():
        def expand(ckv, kpe, r0):
            ckv_b = ckv.astype(jnp.bfloat16)
            kn = _dot(ckv_b, wk_ref[...])
            rows = slice(r0, r0 + ckv.shape[0])
            for hd in range(MLA_HEADS):
                k_scr[hd, rows, :] = (kn[:, hd * HEAD_SLAB:(hd + 1) * HEAD_SLAB] + kpe).astype(jnp.bfloat16)
            vt_scr[:, rows] = _dot_nt(wvt_ref[...], ckv_b).astype(jnp.bfloat16)

        step = 256
        for r0 in range(0, lk_cache, step):
            expand(ckvc_ref[0, r0:r0 + step, :], kpec_ref[0, r0:r0 + step, :], r0)
        for r0 in range(0, n_pack * lk_new, step):
            expand(ckv_ref[r0:r0 + step, :], kpe_ref[r0:r0 + step, :], lk_cache + r0)

    work = [(s, hd) for s in range(n_pack) for hd in range(MLA_HEADS)]
    scores = [_dot_nt(k_scr[hd, s * lk:(s + 1) * lk, :], q_ref[s * TQ:(s + 1) * TQ, hd * HEAD_SLAB:(hd + 1) * HEAD_SLAB])
              for s, hd in work]
    outs = []
    for (s, hd), s_t in zip(work, scores):
        p_t = jnp.exp2(s_t - jnp.max(s_t, axis=0, keepdims=True))
        den = jnp.sum(p_t, axis=0, keepdims=True)
        v_t = vt_scr[hd * V_DIM:(hd + 1) * V_DIM, s * lk:(s + 1) * lk]
        outs.append(_dot(v_t, p_t.astype(jnp.bfloat16)) / den)
    for s in range(n_pack):
        o_ref[s * TQ:(s + 1) * TQ, :] = jnp.concatenate(
            outs[s * MLA_HEADS:(s + 1) * MLA_HEADS], axis=0).T.astype(jnp.bfloat16)


def _attention(q, ckv_n, kpe, w_ukv, row_off, n_batch, seq, cache=None, n_pack=1):
    nq = seq // TQ
    assert n_pack == 1 or (nq == 1 and cache is None and n_batch % n_pack == 0)
    n_batch //= n_pack
    lk_cache = 0 if cache is None else cache[0].shape[1]
    lk = n_pack * (lk_cache + seq)
    qblk = lambda b, qi: (row_off // (n_pack * TQ) + b * nq + qi, 0)
    sblk = lambda b, qi: (row_off // (n_pack * seq) + b, 0)
    in_specs = [
        pl.BlockSpec((n_pack * TQ, MLA_HEADS * HEAD_SLAB), qblk),
        pl.BlockSpec((n_pack * seq, KV_RANK), sblk),
        pl.BlockSpec((n_pack * seq, LANES), sblk),
    ]
    args = [q, ckv_n, kpe]
    if cache is not None:
        in_specs += [pl.BlockSpec((1, lk_cache, KV_RANK), lambda b, qi: (b, 0, 0)),
                     pl.BlockSpec((1, lk_cache, LANES), lambda b, qi: (b, 0, 0))]
        args += list(cache)
    in_specs += [_const_spec(w.shape) for w in w_ukv]
    args += list(w_ukv)
    return pl.pallas_call(
        functools.partial(_attn_kernel, lk_cache=lk_cache, lk_new=seq, n_pack=n_pack),
        grid=(n_batch, nq),
        in_specs=in_specs,
        out_specs=pl.BlockSpec((n_pack * TQ, MLA_HEADS * V_DIM), lambda b, qi: (b * nq + qi, 0)),
        out_shape=jax.ShapeDtypeStruct((n_batch * n_pack * seq, MLA_HEADS * V_DIM), jnp.bfloat16),
        scratch_shapes=[pltpu.VMEM((MLA_HEADS, lk, HEAD_SLAB), jnp.bfloat16),
                        pltpu.VMEM((MLA_HEADS * V_DIM, lk), jnp.bfloat16)],
        compiler_params=pltpu.CompilerParams(dimension_semantics=("arbitrary", "arbitrary"),
                                             vmem_limit_bytes=VMEM_LIMIT),
        name=f"attention_{seq}",
    )(*args)


def _ssd_kernel(*refs, seq, zero_init):
    if zero_init:
        (xbc_ref, dts_ref, cumt_ref, wt_ref, cw_ref, cb_ref, dsk_ref,
         y_ref, hf_ref, hb_ref, xs_scr, c_scr, bt_scr, cum_scr, stf_scr, stb_scr) = refs
    else:
        (xbc_ref, dts_ref, cumt_ref, wt_ref, h0f_ref, h0b_ref, cw_ref, cb_ref, dsk_ref,
         y_ref, hf_ref, hb_ref, xs_scr, c_scr, bt_scr, cum_scr, stf_scr, stb_scr) = refs
    nc = seq // CHUNK
    gs = SSD_GROUPS * SSD_STATE
    nh2 = 2 * SSD_HEADS

    row = lax.broadcasted_iota(jnp.int32, (CHUNK, CHUNK), 0)
    col = lax.broadcasted_iota(jnp.int32, (CHUNK, CHUNK), 1)
    low_half = col < SSD_HEAD_DIM
    lower = row >= col
    upper = row <= col

    def prep_chunk(c, carry):
        r0 = pl.multiple_of(c * CHUNK, CHUNK)
        rows = pl.ds(r0, CHUNK)
        rows_prev = pl.ds(pl.multiple_of(jnp.maximum(r0 - HALO, 0), HALO), HALO)
        rows_next = pl.ds(pl.multiple_of(jnp.minimum(r0 + CHUNK, seq - HALO), HALO), HALO)

        def conv_tile(cs):
            prev = jnp.where(c > 0, xbc_ref[rows_prev, cs], 0.0)
            nxt = jnp.where(c < nc - 1, xbc_ref[rows_next, cs], 0.0)
            win = jnp.concatenate([prev, xbc_ref[rows, cs], nxt], axis=0)
            acc = jnp.broadcast_to(cb_ref[:, cs], (CHUNK, LANES))
            for k in range(CONV_W):
                lo = HALO - CONV_W // 2 + k
                acc = acc + cw_ref[k:k + 1, cs] * win[lo:lo + CHUNK, :]
            return _silu(acc)

        def x_tile(j, carry):
            cs = pl.ds(pl.multiple_of(j * LANES, LANES), LANES)
            u = conv_tile(cs)
            y_ref[rows, cs] = dsk_ref[:, cs] * u
            xs_scr[rows, cs] = u.astype(jnp.bfloat16)
            return carry

        lax.fori_loop(0, D_SSD // LANES, x_tile, 0)
        cum_t = cumt_ref[pl.ds(pl.multiple_of(c * nh2, nh2), nh2), :]
        cum_scr[rows, :] = jnp.concatenate(
            [cum_t, jnp.zeros((CHUNK - nh2, CHUNK), jnp.float32)], axis=0).T[:, :nh2]
        for g in range(SSD_GROUPS):
            b0 = pl.multiple_of(c * gs + g * SSD_STATE, SSD_STATE)
            bt_scr[pl.ds(b0, SSD_STATE), :] = conv_tile(slice(D_SSD + g * SSD_STATE, D_SSD + (g + 1) * SSD_STATE)).T
            c_scr[rows, g * SSD_STATE:(g + 1) * SSD_STATE] = conv_tile(
                slice(D_SSD + gs + g * SSD_STATE, D_SSD + gs + (g + 1) * SSD_STATE)).astype(jnp.bfloat16)
        return carry

    lax.fori_loop(0, nc, prep_chunk, 0)

    if zero_init:
        stf_scr[...] = jnp.zeros_like(stf_scr)
        stb_scr[...] = jnp.zeros_like(stb_scr)
    else:
        stf_scr[...] = h0f_ref[0].T
        stb_scr[...] = h0b_ref[0].T

    def scan_open(ci, st_scr):
        rows = pl.ds(pl.multiple_of(ci * CHUNK, CHUNK), CHUNK)
        c_b = c_scr[rows, :]
        st = st_scr[...]
        bts, cbms, zs = [], [], []
        for g in range(SSD_GROUPS):
            cg = c_b[:, g * SSD_STATE:(g + 1) * SSD_STATE]
            bt = bt_scr[pl.ds(pl.multiple_of(ci * gs + g * SSD_STATE, SSD_STATE), SSD_STATE), :]
            gcols = slice(g * SSD_HPG * SSD_HEAD_DIM, (g + 1) * SSD_HPG * SSD_HEAD_DIM)
            bts.append(bt)
            cbms.append(_dot(cg, bt.astype(jnp.bfloat16)))
            zs.append(_dot(cg, st[:, gcols].astype(jnp.bfloat16)))
        return st, bts, cbms, zs

    def scan_pairs(ci, st_scr, reverse, opened):
        st, bts, cbms, zs = opened
        lane0 = SSD_HEADS if reverse else 0
        causal = upper if reverse else lower
        last = 0 if reverse else CHUNK - 1
        rows = pl.ds(pl.multiple_of(ci * CHUNK, CHUNK), CHUNK)
        hrows = pl.ds(pl.multiple_of(ci * nh2, nh2), nh2)
        xs_b = xs_scr[rows, :]
        dts_t = dts_ref[hrows, :]
        cum_t = cumt_ref[hrows, :]
        w_t = wt_ref[hrows, :]
        cum = cum_scr[rows, :]
        for pair in range(SSD_HEADS // 2):
            g, jj = divmod(pair, SSD_HPG // 2)
            pcols = slice(pair * LANES, (pair + 1) * LANES)
            lhs_y, lhs_s, entry = [], [], []
            for hd in (2 * pair, 2 * pair + 1):
                ln = lane0 + hd
                cum_i = jnp.broadcast_to(cum[:, ln:ln + 1], (CHUNK, CHUNK))
                dec = jnp.exp2(jnp.where(causal, cum_i - cum_t[ln:ln + 1, :], -jnp.inf))
                lhs_y.append((cbms[g] * dec * dts_t[ln:ln + 1, :]).astype(jnp.bfloat16))
                lhs_s.append((bts[g] * w_t[ln:ln + 1, :]).astype(jnp.bfloat16))
                entry.append(jnp.exp2(cum_i))
            out = _dot(jnp.concatenate(lhs_y + lhs_s, axis=0), xs_b[:, pcols])
            ea = jnp.where(low_half, entry[0], entry[1])
            y_ref[rows, pcols] += (jnp.where(low_half, out[:CHUNK], out[CHUNK:2 * CHUNK])
                                   + zs[g][:, jj * LANES:(jj + 1) * LANES] * ea)
            st_scr[:, pcols] = (ea[last:last + 1, :] * st[:, pcols]
                                + jnp.where(low_half, out[2 * CHUNK:3 * CHUNK], out[3 * CHUNK:]))

    def both(c, carry):
        opened_f = scan_open(c, stf_scr)
        opened_b = scan_open(nc - 1 - c, stb_scr)
        scan_pairs(c, stf_scr, False, opened_f)
        scan_pairs(nc - 1 - c, stb_scr, True, opened_b)
        return carry

    lax.fori_loop(0, nc, both, 0)
    hf_ref[0] = stf_scr[...].T
    hb_ref[0] = stb_scr[...].T


def _ssd(i_ab, xbc, head_terms, h0, conv_w, vec_table, conv_b_row, d_skip, row_off, n_batch, seq):
    assert CONV_CH == D_MODEL
    hp = SSD_HEADS * SSD_HEAD_DIM
    gs = SSD_GROUPS * SSD_STATE
    nc = seq // CHUNK
    seq_blk = lambda b: (row_off // seq + b, 0)
    head_spec = pl.BlockSpec((nc * 2 * SSD_HEADS, CHUNK), seq_blk)
    st_spec = pl.BlockSpec((1, hp, SSD_STATE), lambda b: (b, 0, 0))
    st_shape = jax.ShapeDtypeStruct((n_batch, hp, SSD_STATE), jnp.float32)
    h0 = () if h0 is None else tuple(h0)
    return pl.pallas_call(
        functools.partial(_ssd_kernel, seq=seq, zero_init=not h0),
        grid=(n_batch,),
        in_specs=[pl.BlockSpec((seq, CONV_CH), seq_blk), head_spec, head_spec, head_spec] + [st_spec] * len(h0) + [
            pl.BlockSpec((None, CONV_W, CONV_CH), lambda b: (i_ab, 0, 0), pipeline_mode=pl.Buffered(1)),
            _vec_spec(conv_b_row + i_ab), _const_spec((1, D_SSD))],
        out_specs=[pl.BlockSpec((seq, D_SSD), lambda b: (b, 0)), st_spec, st_spec],
        out_shape=[jax.ShapeDtypeStruct((n_batch * seq, D_SSD), jnp.float32), st_shape, st_shape],
        scratch_shapes=[pltpu.VMEM((seq, D_SSD), jnp.bfloat16),
                        pltpu.VMEM((seq, gs), jnp.bfloat16),
                        pltpu.VMEM((nc * gs, CHUNK), jnp.float32),
                        pltpu.VMEM((seq, 2 * SSD_HEADS), jnp.float32),
                        pltpu.VMEM((SSD_STATE, hp), jnp.float32),
                        pltpu.VMEM((SSD_STATE, hp), jnp.float32)],
        compiler_params=pltpu.CompilerParams(dimension_semantics=("arbitrary",), vmem_limit_bytes=VMEM_LIMIT),
        name=f"ssd_{seq}",
    )(xbc, *head_terms, *h0, conv_w, vec_table, d_skip)


def _post_mix(x, mix, mod_ref, npost_ref, npre_ref):
    d = D_MODEL
    gate_mix = mod_ref[0, :, 2 * d:3 * d]
    shf = mod_ref[0, :, 3 * d:4 * d]
    scf = mod_ref[0, :, 4 * d:5 * d]
    x1 = x + gate_mix * _rms(mix, npost_ref[...])
    return x1, (_rms(x1, npre_ref[...]) * (1.0 + scf) + shf).astype(jnp.bfloat16)


def _subtile_pipeline(n_sub, mixer_pre, mixer_dots, mod_ref, nffn_ref, wg_ref, wu_ref, wd_ref, interleave):
    gate_ffn = mod_ref[0, :, 5 * D_MODEL:6 * D_MODEL]
    ffn_up = lambda h: (_silu(_dot(h, wg_ref[...])) * _dot(h, wu_ref[...])).astype(jnp.bfloat16)
    ffn_down = lambda x1, hid: x1 + gate_ffn * _rms(_dot(hid, wd_ref[...]), nffn_ref[...])
    if not interleave:
        staged = [mixer_dots(r, mixer_pre(r)) for r in range(n_sub)]
        return [ffn_down(x1, ffn_up(h)) for x1, h in staged]
    outs = []
    x1, h = mixer_dots(0, mixer_pre(0))
    for r in range(n_sub):
        nxt_pre = mixer_pre(r + 1) if r + 1 < n_sub else None
        hid = ffn_up(h)
        nxt = mixer_dots(r + 1, nxt_pre) if r + 1 < n_sub else None
        outs.append(ffn_down(x1, hid))
        if nxt is not None:
            x1, h = nxt
    return outs


def _ffn_specs(l, vec_rows):
    return [_vec_spec(vec_rows[name] + l) for name in ("norm_post_mix", "norm_pre_ffn", "norm_post_ffn")] + [
        _const_spec((D_MODEL, D_FF)), _const_spec((D_MODEL, D_FF)), _const_spec((D_FF, D_MODEL))]


def _outproj_ffn_kernel(*refs, n_cast):
    (xp_ref, xs_ref, attp_ref, atts_ref, yp_ref, ys_ref, z_ref, mod_ref, sn_ref, wo_ref,
     npost_ref, npre_ref, nffn_ref, wg_ref, wu_ref, wd_ref) = refs[:16]
    cast_src, o_ref, cast_dst = refs[16:16 + n_cast], refs[16 + n_cast], refs[17 + n_cast:]
    i = pl.program_id(0)
    _cast_chunks(cast_src, cast_dst)
    gw = D_SSD // SSD_GROUPS
    sub = lambda r: slice(r * SUB_ROWS, (r + 1) * SUB_ROWS)

    def mixer_pre(r):
        yg = _pick_group(i, yp_ref, ys_ref, sub(r)) * _silu(z_ref[sub(r), :])
        parts = [_pick_group(i, attp_ref, atts_ref, sub(r))]
        for g in range(SSD_GROUPS):
            parts.append(_rms(yg[:, g * gw:(g + 1) * gw], sn_ref[:, g * gw:(g + 1) * gw]).astype(jnp.bfloat16))
        return jnp.concatenate(parts, axis=1)

    def mixer_dots(r, cat):
        return _post_mix(_pick_group(i, xp_ref, xs_ref, sub(r)), _dot(cat, wo_ref[...]),
                         mod_ref, npost_ref, npre_ref)

    outs = _subtile_pipeline(TM // SUB_ROWS, mixer_pre, mixer_dots, mod_ref, nffn_ref, wg_ref, wu_ref, wd_ref,
                             interleave=False)
    for r, res in enumerate(outs):
        o_ref[sub(r), :] = res


def _outproj_ffn(l, i_ab, vec_table, vec_rows, xp, xs, att_p, att_s, y_p, y_s, z, mod, w_out, wg, wu, wd, cast):
    d_cat = MLA_HEADS * V_DIM + D_SSD
    cast_out = [_cast_out(w) for w, _ in cast]
    return pl.pallas_call(
        functools.partial(_outproj_ffn_kernel, n_cast=len(cast)),
        grid=(N_TILES,),
        in_specs=(_group_specs(D_MODEL) + _group_specs(MLA_HEADS * V_DIM) + _group_specs(D_SSD)
                  + [_row_spec(D_SSD), _mod_spec(l), _vec_spec(vec_rows["ssd_norm"] + i_ab),
                     _const_spec((d_cat, D_MODEL))] + _ffn_specs(l, vec_rows)
                  + [_cast_in_spec(w, layer) for w, layer in cast]),
        out_specs=[_row_spec(D_MODEL)] + [spec for spec, _ in cast_out],
        out_shape=[jax.ShapeDtypeStruct((N_TOK, D_MODEL), jnp.float32)] + [shape for _, shape in cast_out],
        compiler_params=pltpu.CompilerParams(dimension_semantics=("arbitrary",), vmem_limit_bytes=VMEM_LIMIT),
        name="outproj_ffn",
    )(xp, xs, att_p, att_s, y_p, y_s, z, mod, vec_table, w_out, vec_table, vec_table, vec_table, wg, wu, wd,
      *[w for w, _ in cast])


def _pool_ffn_kernel(x_ref, xp_ref, xn_ref, mod_ref, nmix_ref, pw_ref, ps_ref,
                     npost_ref, npre_ref, nffn_ref, wg_ref, wu_ref, wd_ref, op_ref, os_ref):
    i = pl.program_id(0)

    @pl.when(i == 0)
    def _():
        os_ref[...] = jnp.zeros_like(os_ref)

    seq = jnp.where(i < N_PROMPT // TM, SEQ, DEC_SEQ)
    pos0 = (i * TM) % seq
    sh = mod_ref[0, :, 0:D_MODEL]
    sc = mod_ref[0, :, D_MODEL:2 * D_MODEL]
    hmod = lambda v: _rms(v, nmix_ref[...]) * (1.0 + sc) + sh
    n_rows = POOL_SUB + 2 * HALO

    def shifted(v, s):
        return pltpu.roll(v, n_rows - s, 0)

    def mixer_pre(s):
        lo, hi = s * POOL_SUB, (s + 1) * POOL_SUB
        pos_s = (pos0 + lo) % seq
        h = hmod(x_ref[lo:hi, :])
        before = hmod(xp_ref[...] if s == 0 else x_ref[lo - HALO:lo, :])
        after = hmod(xn_ref[...] if hi == TM else x_ref[hi:hi + HALO, :])
        before = jnp.where(pos_s > 0, before, 0.0)
        after = jnp.where(pos_s + POOL_SUB < seq, after, 0.0)
        padded = jnp.concatenate([before, h, after], axis=0)
        pos = pos_s + lax.broadcasted_iota(jnp.int32, (POOL_SUB, 1), 0)
        pooled = []
        for gi, w in enumerate(POOL_WINDOWS):
            cols = slice(gi * POOL_GC, (gi + 1) * POOL_GC)
            t = padded[:, cols]
            span = 1
            while span < w:
                t = t + shifted(t, span)
                span *= 2
            lead = HALO - w // 2
            win_sum = (shifted(t, lead) if lead else t)[:POOL_SUB, :]
            cnt = (jnp.minimum(pos + w // 2, seq) - jnp.maximum(pos - w // 2, 0)).astype(jnp.float32)
            pooled.append((win_sum / cnt - h[:, cols]).astype(jnp.bfloat16))
        return pooled

    def mixer_dots(s, pooled):
        mix = jnp.concatenate([_dot(p, pw_ref[gi]) for gi, p in enumerate(pooled)], axis=1) * ps_ref[...]
        return _post_mix(x_ref[s * POOL_SUB:(s + 1) * POOL_SUB, :], mix, mod_ref, npost_ref, npre_ref)

    res = jnp.concatenate(_subtile_pipeline(TM // POOL_SUB, mixer_pre, mixer_dots, mod_ref, nffn_ref,
                                            wg_ref, wu_ref, wd_ref, interleave=True), axis=0)

    @pl.when(i < N_PT)
    def _():
        op_ref[...] = res

    @pl.when(i >= N_PT)
    def _():
        os_ref[...] = res


def _pool_ffn(l, j_c, vec_table, vec_rows, xa, mod, pool_w, wg, wu, wd):
    hb = TM // HALO
    nh = N_TOK // HALO
    return pl.pallas_call(
        _pool_ffn_kernel,
        grid=(N_TILES,),
        in_specs=[_row_spec(D_MODEL),
                  pl.BlockSpec((HALO, D_MODEL), lambda i: (jnp.maximum(i * hb - 1, 0), 0)),
                  pl.BlockSpec((HALO, D_MODEL), lambda i: (jnp.minimum((i + 1) * hb, nh - 1), 0)),
                  _mod_spec(l),
                  _vec_spec(vec_rows["norm_pre_mix"] + l),
                  _const_spec((len(POOL_WINDOWS), POOL_GC, POOL_GC)),
                  _vec_spec(vec_rows["pool_scale"] + j_c)] + _ffn_specs(l, vec_rows),
        out_specs=_group_specs(D_MODEL),
        out_shape=[jax.ShapeDtypeStruct((N_PROMPT, D_MODEL), jnp.float32),
                   jax.ShapeDtypeStruct((N_SAMPLE, D_MODEL), jnp.float32)],
        compiler_params=pltpu.CompilerParams(dimension_semantics=("arbitrary",), vmem_limit_bytes=VMEM_LIMIT),
        name="pool_ffn",
    )(xa, xa, xa, mod, vec_table, pool_w, vec_table, vec_table, vec_table, vec_table, wg, wu, wd)


def _rope_tables():
    f32 = np.float32
    rows = DEC_SEQ // GRID_W
    r = np.repeat(np.arange(rows, dtype=f32), GRID_W)
    c = np.tile(np.arange(GRID_W, dtype=f32), rows)
    half = ROPE_DIM // 2
    inv_freq = np.power(f32(ROPE_THETA), -np.arange(0, half, 2, dtype=f32) / f32(half)).astype(f32)
    ang = np.concatenate([r[:, None] * inv_freq, c[:, None] * inv_freq], axis=-1).astype(f32)
    cos, sin = np.cos(ang).astype(f32), np.sin(ang).astype(f32)
    ones = np.ones((DEC_SEQ, KPE_LANE0), f32)
    zl = np.zeros((DEC_SEQ, KPE_LANE0), f32)
    zr = np.zeros((DEC_SEQ, LANES - KPE_LANE0 - ROPE_DIM), f32)
    cos_t = np.concatenate([ones, cos, cos, zr], axis=1)
    sin_t = np.concatenate([zl, -sin, sin, zr], axis=1)
    keep = np.concatenate([np.ones((TM, KPE_LANE0 + ROPE_DIM), f32), np.zeros((TM, LANES - KPE_LANE0 - ROPE_DIM), f32)],
                          axis=1)
    return (jnp.asarray(np.concatenate([cos_t, keep], axis=0)),
            jnp.asarray(np.concatenate([sin_t, np.zeros((TM, LANES), f32)], axis=0)))


def _kpe_slab(k):
    pad = [(0, 0)] * (k.ndim - 1) + [(KPE_LANE0, LANES - KPE_LANE0 - ROPE_DIM)]
    return jnp.pad(k, pad)


def _with_swapped_pair(w_pe):
    half = ROPE_DIM // 2
    return jnp.concatenate([w_pe, w_pe[..., half:], w_pe[..., :half]], axis=-1)


def _layout_in_proj(w):
    o = np.cumsum((0, Q_RANK, KV_RANK, ROPE_DIM, D_SSD, CONV_CH, SSD_HEADS, SSD_HEADS))
    cq, ckv, kpe, z, xbc, dtf, dtb = (w[:, o[k]:o[k + 1]] for k in range(7))
    dt = jnp.pad(jnp.concatenate([dtf, dtb], axis=1), ((0, 0), (0, LANES - 2 * SSD_HEADS)))
    kpe_slab = jnp.pad(_with_swapped_pair(kpe), ((0, 0), (KPE_LANE0, LANES - KPE_LANE0 - 2 * ROPE_DIM)))
    return jnp.concatenate([cq, ckv, z, xbc, kpe_slab, dt], axis=1).astype(jnp.bfloat16)


def _layout_uq(w):
    w = w.reshape(Q_RANK, MLA_HEADS, NOPE_DIM + ROPE_DIM)
    w = jnp.concatenate([w[:, :, :NOPE_DIM], _with_swapped_pair(w[:, :, NOPE_DIM:])], axis=-1)
    assert w.shape[-1] == HEAD_SLAB
    return w.reshape(Q_RANK, MLA_HEADS * HEAD_SLAB).astype(jnp.bfloat16)


def _layout_ukv(w):
    w = w.reshape(KV_RANK, MLA_HEADS, NOPE_DIM + V_DIM)
    kn = jnp.pad(w[:, :, :NOPE_DIM], ((0, 0), (0, 0), (0, HEAD_SLAB - NOPE_DIM)))
    v = w[:, :, NOPE_DIM:]
    return kn.reshape(KV_RANK, -1).astype(jnp.bfloat16), v.reshape(KV_RANK, -1).T.astype(jnp.bfloat16)


def _lane_row(fwd, bwd):
    return jnp.pad(jnp.concatenate([fwd, bwd]), (0, LANES - 2 * SSD_HEADS)).reshape(1, LANES)


def kernel(x_prompt, x_sample, c, cache_mla_ckv, cache_mla_krope, state_ssd_fwd, state_ssd_bwd, c_ctx, w_mod, b_mod, norm_pre_mix, norm_post_mix, norm_pre_ffn, norm_post_ffn, w_in_ab, q_norm, w_uq, kv_norm, w_ukv, ssd_conv_w, ssd_conv_b, ssd_dt_bias_fwd, ssd_dt_bias_bwd, ssd_a_log_fwd, ssd_a_log_bwd, ssd_d, ssd_norm, w_out_ab, pool_w, pool_scale, ffn_w_gate, ffn_w_up, ffn_w_down):
    f32, bf16 = jnp.float32, jnp.bfloat16
    assert DEPTH == 2
    xp = x_prompt.reshape(N_PROMPT, D_MODEL)
    xs = x_sample.reshape(N_SAMPLE, D_MODEL)
    cvecs = jnp.concatenate([c_ctx[None, :], c, jnp.zeros((MOD_ROWS - N_MODVEC, D_MODEL), f32)], axis=0)
    mod = _modulation(cvecs, w_mod, b_mod).reshape(DEPTH, SUBLANES, 1, 6 * D_MODEL)
    tabs = _rope_tables()
    hp = SSD_HEADS * SSD_HEAD_DIM
    row = lambda v: v.reshape(1, -1)
    new_ckv, new_kpe, new_hf, new_hb = [], [], [], []

    vec_table, vec_rows = _pack_vectors(
        norm_pre_mix=norm_pre_mix, norm_post_mix=norm_post_mix, norm_pre_ffn=norm_pre_ffn, norm_post_ffn=norm_post_ffn,
        q_norm=q_norm, kv_norm=kv_norm, ssd_norm=ssd_norm, ssd_conv_b=ssd_conv_b, pool_scale=pool_scale)
    ffn_w_f32 = (ffn_w_gate, ffn_w_up, ffn_w_down)
    for l in range(DEPTH):
        if l % 2 == 0:
            i = l // 2
            (q, ckv_n, kpe, z, xbc, dts_t, cum_t, w_t, ckv_prompt, kpe_prompt, w_out, *ffn_w) = _inproj(
                l, vec_table, vec_rows, xp, xs, mod, tabs, _layout_in_proj(w_in_ab[i]), _layout_uq(w_uq[i]),
                _lane_row(ssd_dt_bias_fwd[i], ssd_dt_bias_bwd[i]).T, _lane_row(ssd_a_log_fwd[i], ssd_a_log_bwd[i]).T,
                cast=[(w_out_ab, i)] + [(w, l) for w in ffn_w_f32])
            heads = (dts_t, cum_t, w_t)
            w_kv = _layout_ukv(w_ukv[i])
            att_p = _attention(q, ckv_n, kpe, w_kv, 0, BATCH, SEQ, n_pack=PROMPT_PACK)
            att_s = _attention(q, ckv_n, kpe, w_kv, N_PROMPT, DEC_BATCH, DEC_SEQ,
                               cache=(cache_mla_ckv[:, i], _kpe_slab(cache_mla_krope[:, i])))
            ssd_args = (ssd_conv_w, vec_table, vec_rows["ssd_conv_b"], row(jnp.repeat(ssd_d[i], SSD_HEAD_DIM)))
            y_p, hf, hb = _ssd(i, xbc, heads, None, *ssd_args, 0, BATCH, SEQ)
            y_s, _, _ = _ssd(i, xbc, heads, (state_ssd_fwd[:, i].reshape(DEC_BATCH, hp, SSD_STATE),
                                             state_ssd_bwd[:, i].reshape(DEC_BATCH, hp, SSD_STATE)),
                             *ssd_args, N_PROMPT, DEC_BATCH, DEC_SEQ)
            xa, *ffn_w = _outproj_ffn(l, i, vec_table, vec_rows, xp, xs, att_p, att_s, y_p, y_s, z, mod, w_out, *ffn_w,
                                      cast=[(w, l + 1) for w in ffn_w_f32])
            new_ckv.append(ckv_prompt.reshape(BATCH, SEQ, KV_RANK))
            new_kpe.append(kpe_prompt.reshape(BATCH, SEQ, ROPE_DIM))
            new_hf.append(hf.reshape(BATCH, SSD_HEADS, SSD_HEAD_DIM, SSD_STATE))
            new_hb.append(hb.reshape(BATCH, SSD_HEADS, SSD_HEAD_DIM, SSD_STATE))
        else:
            j = l // 2
            yp, ys = _pool_ffn(l, j, vec_table, vec_rows, xa, mod, pool_w[j].astype(bf16), *ffn_w)

    return (yp.reshape(BATCH, SEQ, D_MODEL), ys.reshape(DEC_BATCH, DEC_SEQ, D_MODEL),
            jnp.stack(new_ckv, axis=1), jnp.stack(new_kpe, axis=1),
            jnp.stack(new_hf, axis=1), jnp.stack(new_hb, axis=1))
```

```python
import functools

import numpy as np
import jax
import jax.numpy as jnp
from jax import lax
from jax.experimental import pallas as pl
from jax.experimental.pallas import tpu as pltpu

D_MODEL = 1024
BATCH = 16
SEQ = 256
DEPTH = 2
DEC_BATCH = 2
DEC_SEQ = 2048
PAST_LEN = 256
GRID_W = 64
EPS = 1e-6
MLA_HEADS = 8
Q_RANK = 256
KV_RANK = 256
NOPE_DIM = 64
ROPE_DIM = 32
V_DIM = 64
ROPE_THETA = 10000.0
SSD_HEADS = 8
SSD_GROUPS = 2
SSD_HPG = SSD_HEADS // SSD_GROUPS
SSD_HEAD_DIM = 64
SSD_STATE = 128
D_SSD = SSD_HEADS * SSD_HEAD_DIM
CONV_W = 5
CONV_CH = D_SSD + 2 * SSD_GROUPS * SSD_STATE
POOL_WINDOWS = (2, 4, 8, 16)
POOL_GC = D_MODEL // len(POOL_WINDOWS)
D_FF = ((8 * D_MODEL + 3 * 256 - 1) // (3 * 256)) * 256

SUBLANES = 8
LANES = 128

N_PROMPT = BATCH * SEQ
N_SAMPLE = DEC_BATCH * DEC_SEQ
N_TOK = N_PROMPT + N_SAMPLE
N_MODVEC = 1 + DEC_BATCH
TM = 512
TQ = 256
CHUNK = 128
HALO = SUBLANES
HEAD_SLAB = LANES
IN_COLS = Q_RANK + KV_RANK + D_SSD + CONV_CH + 2 * LANES
KPE_LANE0 = NOPE_DIM
VMEM_LIMIT = 56 * 1024 * 1024

POOL_SUB = min(SEQ, DEC_SEQ)

assert TM % POOL_SUB == 0 and SEQ % POOL_SUB == 0 and DEC_SEQ % TM == 0 and N_PROMPT % DEC_SEQ == 0


def _rms(x, g):
    return x * lax.rsqrt(jnp.mean(x * x, axis=-1, keepdims=True) + EPS) * g


def _silu(x):
    return x * jax.nn.sigmoid(x)


def _softplus(x):
    return jnp.maximum(x, 0.0) + jnp.log1p(jnp.exp(-jnp.abs(x)))


def _dot(a, b):
    return jnp.dot(a, b, preferred_element_type=jnp.float32)


def _dot_nt(a, b):
    return lax.dot_general(a, b, (((1,), (1,)), ((), ())), preferred_element_type=jnp.float32)


def _mod_row(i):
    return jnp.where(i < N_PROMPT // TM, 0, 1 + (i - N_PROMPT // TM) // (DEC_SEQ // TM))


def _const_spec(shape):
    nd = len(shape)
    return pl.BlockSpec(shape, lambda *_: (0,) * nd, pipeline_mode=pl.Buffered(1))


N_PT = N_PROMPT // TM
N_TILES = N_TOK // TM


def _row_spec(width):
    return pl.BlockSpec((TM, width), lambda i: (i, 0))


def _mod_spec(l):
    return pl.BlockSpec((None, 1, 1, 6 * D_MODEL), lambda i: (l, _mod_row(i), 0, 0))


def _vec_spec(row):
    return pl.BlockSpec((None, 1, D_MODEL), lambda *_: (row, 0, 0), pipeline_mode=pl.Buffered(1))


def _pack_vectors(**params):
    first_row, blocks, n = {}, [], 0
    for name, a in params.items():
        first_row[name] = n
        n += a.shape[0]
        blocks.append(jnp.pad(a, ((0, 0), (0, D_MODEL - a.shape[1]))))
    return jnp.concatenate(blocks, axis=0).reshape(n, 1, D_MODEL), first_row


def _group_specs(width):
    return [pl.BlockSpec((TM, width), lambda i: (jnp.minimum(i, N_PT - 1), 0)),
            pl.BlockSpec((TM, width), lambda i: (jnp.maximum(i - N_PT, 0), 0))]


N_CAST = 16


def _cast_in_spec(w, layer, step=lambda i: i):
    _, rows, cols = w.shape
    return pl.BlockSpec((1, rows // N_CAST, cols), lambda *g: (layer, step(*g), 0))


def _cast_out(w, step=lambda i: i):
    _, rows, cols = w.shape
    return (pl.BlockSpec((rows // N_CAST, cols), lambda *g: (step(*g), 0)),
            jax.ShapeDtypeStruct((rows, cols), jnp.bfloat16))


def _cast_chunks(src_refs, dst_refs):
    for src, dst in zip(src_refs, dst_refs):
        dst[...] = src[0].astype(jnp.bfloat16)


def _pick_group(i, p_ref, s_ref, rows=slice(None)):
    return jnp.where(i < N_PT, p_ref[rows, :], s_ref[rows, :])


MOD_TK = 256
MOD_STREAMS = 2
PROMPT_PACK = 4
MOD_ROWS = 2 * SUBLANES
SUB_ROWS = 256


def _split3(a):
    a_hi = a.astype(jnp.bfloat16)
    r1 = a - a_hi.astype(jnp.float32)
    a_mid = r1.astype(jnp.bfloat16)
    a_lo = (r1 - a_mid.astype(jnp.float32)).astype(jnp.bfloat16)
    return a_hi, a_mid, a_lo


def _mod_kernel(c_ref, *refs):
    w_refs, b_ref, o_ref = refs[:MOD_STREAMS], refs[MOD_STREAMS], refs[MOD_STREAMS + 1]
    k = pl.program_id(1)
    rows = MOD_TK // MOD_STREAMS
    s_all = _silu(c_ref[...])
    part = jnp.zeros((MOD_ROWS, 6 * D_MODEL), jnp.float32)
    for j, w_ref in enumerate(w_refs):
        s_hi, s_mid, s_lo = _split3(s_all[:, j * rows:(j + 1) * rows])
        w = w_ref[0]
        w_hi = w.astype(jnp.bfloat16)
        w_lo = (w - w_hi.astype(jnp.float32)).astype(jnp.bfloat16)
        top = _dot(jnp.concatenate([s_hi, s_mid, s_lo], axis=0), w_hi)
        low = _dot(jnp.concatenate([s_hi, s_mid], axis=0), w_lo)
        part = part + (top[:MOD_ROWS] + top[MOD_ROWS:2 * MOD_ROWS] + top[2 * MOD_ROWS:]
                       + low[:MOD_ROWS] + low[MOD_ROWS:])
    part = part[:SUBLANES]

    @pl.when(k == 0)
    def _():
        o_ref[0] = part + b_ref[0]

    @pl.when(k > 0)
    def _():
        o_ref[0] += part


def _modulation(cvecs, w_mod, b_mod):
    n = 6 * D_MODEL
    w_spec = lambda j: pl.BlockSpec((1, MOD_TK // MOD_STREAMS, n), lambda l, k: (l, k * MOD_STREAMS + j, 0))
    return pl.pallas_call(
        _mod_kernel,
        grid=(DEPTH, D_MODEL // MOD_TK),
        in_specs=[pl.BlockSpec((MOD_ROWS, MOD_TK), lambda l, k: (0, k))]
        + [w_spec(j) for j in range(MOD_STREAMS)]
        + [pl.BlockSpec((1, 1, n), lambda l, k: (l, 0, 0))],
        out_specs=pl.BlockSpec((1, SUBLANES, n), lambda l, k: (l, 0, 0)),
        out_shape=jax.ShapeDtypeStruct((DEPTH, SUBLANES, n), jnp.float32),
        compiler_params=pltpu.CompilerParams(dimension_semantics=("arbitrary", "arbitrary"),
                                             vmem_limit_bytes=VMEM_LIMIT),
        name="modulation",
    )(cvecs, *([w_mod] * MOD_STREAMS), b_mod.reshape(DEPTH, 1, n))


def _ssd_head_terms(dt_raw, dtb_c, a_neg_c):
    nh2 = 2 * SSD_HEADS
    row = lax.broadcasted_iota(jnp.int32, (CHUNK, CHUNK), 0)
    col = lax.broadcasted_iota(jnp.int32, (CHUNK, CHUNK), 1)
    upper_b = (row <= col).astype(jnp.bfloat16)
    lower_b = (row >= col).astype(jnp.bfloat16)
    fwd_rows = lax.broadcasted_iota(jnp.int32, (nh2, CHUNK), 0) < SSD_HEADS
    dts_t = _softplus(dt_raw.T[:nh2, :] + dtb_c)
    pieces = _split3(dts_t * a_neg_c)
    cum_t = jnp.where(fwd_rows, sum(_dot(p, upper_b) for p in pieces), sum(_dot(p, lower_b) for p in pieces))
    cum_t = cum_t * np.float32(np.log2(np.e))
    tot = jnp.where(fwd_rows[:, :1], cum_t[:, CHUNK - 1:], cum_t[:, :1])
    return dts_t, cum_t, dts_t * jnp.exp2(tot - cum_t)


def _inproj_kernel(xp_ref, xs_ref, mod_ref, cos_ref, sin_ref, w_in_ref, w_uq_ref, qn_ref, kvn_ref, npm_ref,
                   dtbc_ref, alogc_ref, q_ref, ckv_ref, kpe_ref, z_ref, xbc_ref, dts_ref, cumt_ref, wt_ref,
                   new_ckv_ref, new_kpe_ref):
    i = pl.program_id(0)
    sh = mod_ref[0, :, 0:D_MODEL]
    sc = mod_ref[0, :, D_MODEL:2 * D_MODEL]
    scale = (NOPE_DIM + ROPE_DIM) ** -0.5 * np.log2(np.e)
    nh2 = 2 * SSD_HEADS
    dtb_c = dtbc_ref[:nh2, :]
    a_neg_c = -jnp.exp(alogc_ref[:nh2, :])
    dt_raw = []
    for r in range(TM // SUB_ROWS):
        rs = slice(r * SUB_ROWS, (r + 1) * SUB_ROWS)
        h = (_rms(_pick_group(i, xp_ref, xs_ref, rs), npm_ref[...]) * (1.0 + sc) + sh).astype(jnp.bfloat16)
        p = _dot(h, w_in_ref[...])
        o = 0
        cq = p[:, o:o + Q_RANK]; o += Q_RANK
        ckv = p[:, o:o + KV_RANK]; o += KV_RANK
        z_ref[rs, :] = p[:, o:o + D_SSD]; o += D_SSD
        xbc_ref[rs, :] = p[:, o:o + CONV_CH]; o += CONV_CH
        kpe = p[:, o:o + LANES]; o += LANES
        dt_raw.append(p[:, o:o + LANES])

        cos = cos_ref[rs, :]
        sin = sin_ref[rs, :]

        def rope(slab):
            return slab * cos + pltpu.roll(slab, LANES - ROPE_DIM, 1) * sin

        ckv_ref[rs, :] = _rms(ckv, kvn_ref[:, :KV_RANK])
        kpe_ref[rs, :] = rope(kpe)
        q = _dot(_rms(cq, qn_ref[:, :Q_RANK]).astype(jnp.bfloat16), w_uq_ref[...]) * scale
        for hd in range(MLA_HEADS):
            sl = slice(hd * HEAD_SLAB, (hd + 1) * HEAD_SLAB)
            q_ref[rs, sl] = rope(q[:, sl]).astype(jnp.bfloat16)

    dt_all = jnp.concatenate(dt_raw, axis=0)
    for ck in range(TM // CHUNK):
        terms = _ssd_head_terms(dt_all[ck * CHUNK:(ck + 1) * CHUNK, :], dtb_c, a_neg_c)
        for ref, val in zip((dts_ref, cumt_ref, wt_ref), terms):
            ref[ck * nh2:(ck + 1) * nh2, :] = val

    @pl.when(i < N_PT)
    def _():
        new_ckv_ref[...] = ckv_ref[...]
        new_kpe_ref[...] = kpe_ref[:, KPE_LANE0:KPE_LANE0 + ROPE_DIM]


def _inproj(l, vec_table, vec_rows, xp, xs, mod, tabs, w_in, w_uq, dt_bias_c, a_log_c):
    nt = N_TILES
    prompt_blk = lambda i: (jnp.minimum(i, N_PT - 1), 0)
    lat_tiles = DEC_SEQ // TM
    tab_spec = pl.BlockSpec((TM, LANES), lambda i: (jnp.where(i < N_PT, lat_tiles, (i - N_PT) % lat_tiles), 0))
    row = _row_spec
    hrows = TM // CHUNK * 2 * SSD_HEADS
    head_spec = pl.BlockSpec((hrows, CHUNK), lambda i: (i, 0))
    head_shape = jax.ShapeDtypeStruct((nt * hrows, CHUNK), jnp.float32)
    return pl.pallas_call(
        _inproj_kernel,
        grid=(nt,),
        in_specs=_group_specs(D_MODEL) + [
            _mod_spec(l),
            tab_spec, tab_spec,
            _const_spec((D_MODEL, IN_COLS)),
            _const_spec((Q_RANK, MLA_HEADS * HEAD_SLAB)),
            _vec_spec(vec_rows["q_norm"] + l // 2),
            _vec_spec(vec_rows["kv_norm"] + l // 2),
            _vec_spec(vec_rows["norm_pre_mix"] + l),
            _const_spec((LANES, 1)), _const_spec((LANES, 1)),
        ],
        out_specs=[row(MLA_HEADS * HEAD_SLAB), row(KV_RANK), row(LANES), row(D_SSD), row(CONV_CH),
                   head_spec, head_spec, head_spec,
                   pl.BlockSpec((TM, KV_RANK), prompt_blk), pl.BlockSpec((TM, ROPE_DIM), prompt_blk)],
        out_shape=[
            jax.ShapeDtypeStruct((N_TOK, MLA_HEADS * HEAD_SLAB), jnp.bfloat16),
            jax.ShapeDtypeStruct((N_TOK, KV_RANK), jnp.float32),
            jax.ShapeDtypeStruct((N_TOK, LANES), jnp.float32),
            jax.ShapeDtypeStruct((N_TOK, D_SSD), jnp.float32),
            jax.ShapeDtypeStruct((N_TOK, CONV_CH), jnp.float32),
            head_shape, head_shape, head_shape,
            jax.ShapeDtypeStruct((N_PROMPT, KV_RANK), jnp.float32),
            jax.ShapeDtypeStruct((N_PROMPT, ROPE_DIM), jnp.float32),
        ],
        compiler_params=pltpu.CompilerParams(dimension_semantics=("arbitrary",), vmem_limit_bytes=VMEM_LIMIT),
        name="inproj",
    )(xp, xs, mod, *tabs, w_in, w_uq, vec_table, vec_table, vec_table, dt_bias_c, a_log_c)


def _attn_kernel(*refs, lk_cache, lk_new, n_pack, n_cast):
    n_in = 7 if lk_cache else 5
    if lk_cache:
        q_ref, ckv_ref, kpe_ref, ckvc_ref, kpec_ref, wk_ref, wvt_ref = refs[:n_in]
    else:
        q_ref, ckv_ref, kpe_ref, wk_ref, wvt_ref = refs[:n_in]
    cast_src, o_ref = refs[n_in:n_in + n_cast], refs[n_in + n_cast]
    cast_dst = refs[n_in + n_cast + 1:n_in + 2 * n_cast + 1]
    k_scr, vt_scr = refs[n_in + 2 * n_cast + 1:]
    _cast_chunks(cast_src, cast_dst)
    lk = lk_cache + lk_new

    @pl.when(pl.program_id(1) == 0)
    def _expand_kv():
        def expand(ckv, kpe, r0):
            ckv_b = ckv.astype(jnp.bfloat16)
            kn = _dot(ckv_b, wk_ref[...])
            rows = slice(r0, r0 + ckv.shape[0])
            for hd in range(MLA_HEADS):
                k_scr[hd, rows, :] = (kn[:, hd * HEAD_SLAB:(hd + 1) * HEAD_SLAB] + kpe).astype(jnp.bfloat16)
            vt_scr[:, rows] = _dot_nt(wvt_ref[...], ckv_b).astype(jnp.bfloat16)

        step = 256
        for r0 in range(0, lk_cache, step):
            expand(ckvc_ref[0, r0:r0 + step, :], kpec_ref[0, r0:r0 + step, :], r0)
        for r0 in range(0, n_pack * lk_new, step):
            expand(ckv_ref[r0:r0 + step, :], kpe_ref[r0:r0 + step, :], lk_cache + r0)

    work = [(s, hd) for s in range(n_pack) for hd in range(MLA_HEADS)]
    scores = [_dot_nt(k_scr[hd, s * lk:(s + 1) * lk, :], q_ref[s * TQ:(s + 1) * TQ, hd * HEAD_SLAB:(hd + 1) * HEAD_SLAB])
              for s, hd in work]
    outs = []
    for (s, hd), s_t in zip(work, scores):
        p_t = jnp.exp2(s_t - jnp.max(s_t, axis=0, keepdims=True))
        den = jnp.sum(p_t, axis=0, keepdims=True)
        v_t = vt_scr[hd * V_DIM:(hd + 1) * V_DIM, s * lk:(s + 1) * lk]
        outs.append(_dot(v_t, p_t.astype(jnp.bfloat16)) / den)
    for s in range(n_pack):
        o_ref[s * TQ:(s + 1) * TQ, :] = jnp.concatenate(
            outs[s * MLA_HEADS:(s + 1) * MLA_HEADS], axis=0).T.astype(jnp.bfloat16)


def _attention(q, ckv_n, kpe, w_ukv, row_off, n_batch, seq, cache=None, n_pack=1, cast=()):
    nq = seq // TQ
    assert n_pack == 1 or (nq == 1 and cache is None and n_batch % n_pack == 0)
    n_batch //= n_pack
    lk_cache = 0 if cache is None else cache[0].shape[1]
    lk = n_pack * (lk_cache + seq)
    qblk = lambda b, qi: (row_off // (n_pack * TQ) + b * nq + qi, 0)
    sblk = lambda b, qi: (row_off // (n_pack * seq) + b, 0)
    in_specs = [
        pl.BlockSpec((n_pack * TQ, MLA_HEADS * HEAD_SLAB), qblk),
        pl.BlockSpec((n_pack * seq, KV_RANK), sblk),
        pl.BlockSpec((n_pack * seq, LANES), sblk),
    ]
    args = [q, ckv_n, kpe]
    if cache is not None:
        in_specs += [pl.BlockSpec((1, lk_cache, KV_RANK), lambda b, qi: (b, 0, 0)),
                     pl.BlockSpec((1, lk_cache, LANES), lambda b, qi: (b, 0, 0))]
        args += list(cache)
    in_specs += [_const_spec(w.shape) for w in w_ukv]
    args += list(w_ukv)
    assert not cast or n_batch * nq == N_CAST
    step = lambda b, qi: b * nq + qi
    in_specs += [_cast_in_spec(w, layer, step) for w, layer in cast]
    args += [w for w, _ in cast]
    cast_out = [_cast_out(w, step) for w, _ in cast]
    return pl.pallas_call(
        functools.partial(_attn_kernel, lk_cache=lk_cache, lk_new=seq, n_pack=n_pack, n_cast=len(cast)),
        grid=(n_batch, nq),
        in_specs=in_specs,
        out_specs=[pl.BlockSpec((n_pack * TQ, MLA_HEADS * V_DIM), lambda b, qi: (b * nq + qi, 0))]
        + [spec for spec, _ in cast_out],
        out_shape=[jax.ShapeDtypeStruct((n_batch * n_pack * seq, MLA_HEADS * V_DIM), jnp.bfloat16)]
        + [shape for _, shape in cast_out],
        scratch_shapes=[pltpu.VMEM((MLA_HEADS, lk, HEAD_SLAB), jnp.bfloat16),
                        pltpu.VMEM((MLA_HEADS * V_DIM, lk), jnp.bfloat16)],
        compiler_params=pltpu.CompilerParams(dimension_semantics=("arbitrary", "arbitrary"),
                                             vmem_limit_bytes=VMEM_LIMIT),
        name=f"attention_{seq}",
    )(*args)


def _ssd_kernel(*refs, seq, zero_init):
    if zero_init:
        (xbc_ref, dts_ref, cumt_ref, wt_ref, cw_ref, cb_ref, dsk_ref,
         y_ref, hf_ref, hb_ref, xs_scr, c_scr, bt_scr, cum_scr, stf_scr, stb_scr) = refs
    else:
        (xbc_ref, dts_ref, cumt_ref, wt_ref, h0f_ref, h0b_ref, cw_ref, cb_ref, dsk_ref,
         y_ref, hf_ref, hb_ref, xs_scr, c_scr, bt_scr, cum_scr, stf_scr, stb_scr) = refs
    nc = seq // CHUNK
    gs = SSD_GROUPS * SSD_STATE
    nh2 = 2 * SSD_HEADS

    row = lax.broadcasted_iota(jnp.int32, (CHUNK, CHUNK), 0)
    col = lax.broadcasted_iota(jnp.int32, (CHUNK, CHUNK), 1)
    low_half = col < SSD_HEAD_DIM
    lower = row >= col
    upper = row <= col

    def prep_chunk(c, carry):
        r0 = pl.multiple_of(c * CHUNK, CHUNK)
        rows = pl.ds(r0, CHUNK)
        rows_prev = pl.ds(pl.multiple_of(jnp.maximum(r0 - HALO, 0), HALO), HALO)
        rows_next = pl.ds(pl.multiple_of(jnp.minimum(r0 + CHUNK, seq - HALO), HALO), HALO)

        def conv_tile(cs):
            prev = jnp.where(c > 0, xbc_ref[rows_prev, cs], 0.0)
            nxt = jnp.where(c < nc - 1, xbc_ref[rows_next, cs], 0.0)
            win = jnp.concatenate([prev, xbc_ref[rows, cs], nxt], axis=0)
            acc = jnp.broadcast_to(cb_ref[:, cs], (CHUNK, LANES))
            for k in range(CONV_W):
                lo = HALO - CONV_W // 2 + k
                acc = acc + cw_ref[k:k + 1, cs] * win[lo:lo + CHUNK, :]
            return _silu(acc)

        def x_tile(j, carry):
            cs = pl.ds(pl.multiple_of(j * LANES, LANES), LANES)
            u = conv_tile(cs)
            y_ref[rows, cs] = dsk_ref[:, cs] * u
            xs_scr[rows, cs] = u.astype(jnp.bfloat16)
            return carry

        lax.fori_loop(0, D_SSD // LANES, x_tile, 0)
        cum_t = cumt_ref[pl.ds(pl.multiple_of(c * nh2, nh2), nh2), :]
        cum_scr[rows, :] = jnp.concatenate(
            [cum_t, jnp.zeros((CHUNK - nh2, CHUNK), jnp.float32)], axis=0).T[:, :nh2]
        for g in range(SSD_GROUPS):
            b0 = pl.multiple_of(c * gs + g * SSD_STATE, SSD_STATE)
            bt_scr[pl.ds(b0, SSD_STATE), :] = conv_tile(slice(D_SSD + g * SSD_STATE, D_SSD + (g + 1) * SSD_STATE)).T
            c_scr[rows, g * SSD_STATE:(g + 1) * SSD_STATE] = conv_tile(
                slice(D_SSD + gs + g * SSD_STATE, D_SSD + gs + (g + 1) * SSD_STATE)).astype(jnp.bfloat16)
        return carry

    lax.fori_loop(0, nc, prep_chunk, 0)

    if zero_init:
        stf_scr[...] = jnp.zeros_like(stf_scr)
        stb_scr[...] = jnp.zeros_like(stb_scr)
    else:
        stf_scr[...] = h0f_ref[0].T
        stb_scr[...] = h0b_ref[0].T

    def scan_open(ci, st_scr):
        rows = pl.ds(pl.multiple_of(ci * CHUNK, CHUNK), CHUNK)
        c_b = c_scr[rows, :]
        st = st_scr[...]
        bts, cbms, zs = [], [], []
        for g in range(SSD_GROUPS):
            cg = c_b[:, g * SSD_STATE:(g + 1) * SSD_STATE]
            bt = bt_scr[pl.ds(pl.multiple_of(ci * gs + g * SSD_STATE, SSD_STATE), SSD_STATE), :]
            gcols = slice(g * SSD_HPG * SSD_HEAD_DIM, (g + 1) * SSD_HPG * SSD_HEAD_DIM)
            bts.append(bt)
            cbms.append(_dot(cg, bt.astype(jnp.bfloat16)))
            zs.append(_dot(cg, st[:, gcols].astype(jnp.bfloat16)))
        return st, bts, cbms, zs

    def scan_pairs(ci, st_scr, reverse, opened):
        st, bts, cbms, zs = opened
        lane0 = SSD_HEADS if reverse else 0
        causal = upper if reverse else lower
        last = 0 if reverse else CHUNK - 1
        rows = pl.ds(pl.multiple_of(ci * CHUNK, CHUNK), CHUNK)
        hrows = pl.ds(pl.multiple_of(ci * nh2, nh2), nh2)
        xs_b = xs_scr[rows, :]
        dts_t = dts_ref[hrows, :]
        cum_t = cumt_ref[hrows, :]
        w_t = wt_ref[hrows, :]
        cum = cum_scr[rows, :]
        for pair in range(SSD_HEADS // 2):
            g, jj = divmod(pair, SSD_HPG // 2)
            pcols = slice(pair * LANES, (pair + 1) * LANES)
            lhs_y, lhs_s, entry = [], [], []
            for hd in (2 * pair, 2 * pair + 1):
                ln = lane0 + hd
                cum_i = jnp.broadcast_to(cum[:, ln:ln + 1], (CHUNK, CHUNK))
                dec = jnp.exp2(jnp.where(causal, cum_i - cum_t[ln:ln + 1, :], -jnp.inf))
                lhs_y.append((cbms[g] * dec * dts_t[ln:ln + 1, :]).astype(jnp.bfloat16))
                lhs_s.append((bts[g] * w_t[ln:ln + 1, :]).astype(jnp.bfloat16))
                entry.append(jnp.exp2(cum_i))
            out = _dot(jnp.concatenate(lhs_y + lhs_s, axis=0), xs_b[:, pcols])
            ea = jnp.where(low_half, entry[0], entry[1])
            y_ref[rows, pcols] += (jnp.where(low_half, out[:CHUNK], out[CHUNK:2 * CHUNK])
                                   + zs[g][:, jj * LANES:(jj + 1) * LANES] * ea)
            st_scr[:, pcols] = (ea[last:last + 1, :] * st[:, pcols]
                                + jnp.where(low_half, out[2 * CHUNK:3 * CHUNK], out[3 * CHUNK:]))

    def both(c, carry):
        opened_f = scan_open(c, stf_scr)
        opened_b = scan_open(nc - 1 - c, stb_scr)
        scan_pairs(c, stf_scr, False, opened_f)
        scan_pairs(nc - 1 - c, stb_scr, True, opened_b)
        return carry

    lax.fori_loop(0, nc, both, 0)
    hf_ref[0] = stf_scr[...].T
    hb_ref[0] = stb_scr[...].T


def _ssd(i_ab, xbc, head_terms, h0, conv_w, vec_table, conv_b_row, d_skip, row_off, n_batch, seq):
    assert CONV_CH == D_MODEL
    hp = SSD_HEADS * SSD_HEAD_DIM
    gs = SSD_GROUPS * SSD_STATE
    nc = seq // CHUNK
    seq_blk = lambda b: (row_off // seq + b, 0)
    head_spec = pl.BlockSpec((nc * 2 * SSD_HEADS, CHUNK), seq_blk)
    st_spec = pl.BlockSpec((1, hp, SSD_STATE), lambda b: (b, 0, 0))
    st_shape = jax.ShapeDtypeStruct((n_batch, hp, SSD_STATE), jnp.float32)
    h0 = () if h0 is None else tuple(h0)
    return pl.pallas_call(
        functools.partial(_ssd_kernel, seq=seq, zero_init=not h0),
        grid=(n_batch,),
        in_specs=[pl.BlockSpec((seq, CONV_CH), seq_blk), head_spec, head_spec, head_spec] + [st_spec] * len(h0) + [
            pl.BlockSpec((None, CONV_W, CONV_CH), lambda b: (i_ab, 0, 0), pipeline_mode=pl.Buffered(1)),
            _vec_spec(conv_b_row + i_ab), _const_spec((1, D_SSD))],
        out_specs=[pl.BlockSpec((seq, D_SSD), lambda b: (b, 0)), st_spec, st_spec],
        out_shape=[jax.ShapeDtypeStruct((n_batch * seq, D_SSD), jnp.float32), st_shape, st_shape],
        scratch_shapes=[pltpu.VMEM((seq, D_SSD), jnp.bfloat16),
                        pltpu.VMEM((seq, gs), jnp.bfloat16),
                        pltpu.VMEM((nc * gs, CHUNK), jnp.float32),
                        pltpu.VMEM((seq, 2 * SSD_HEADS), jnp.float32),
                        pltpu.VMEM((SSD_STATE, hp), jnp.float32),
                        pltpu.VMEM((SSD_STATE, hp), jnp.float32)],
        compiler_params=pltpu.CompilerParams(dimension_semantics=("arbitrary",), vmem_limit_bytes=VMEM_LIMIT),
        name=f"ssd_{seq}",
    )(xbc, *head_terms, *h0, conv_w, vec_table, d_skip)


def _post_mix(x, mix, mod_ref, npost_ref, npre_ref):
    d = D_MODEL
    gate_mix = mod_ref[0, :, 2 * d:3 * d]
    shf = mod_ref[0, :, 3 * d:4 * d]
    scf = mod_ref[0, :, 4 * d:5 * d]
    x1 = x + gate_mix * _rms(mix, npost_ref[...])
    return x1, (_rms(x1, npre_ref[...]) * (1.0 + scf) + shf).astype(jnp.bfloat16)


def _subtile_pipeline(n_sub, mixer_pre, mixer_dots, mod_ref, nffn_ref, wg_ref, wu_ref, wd_ref, interleave):
    gate_ffn = mod_ref[0, :, 5 * D_MODEL:6 * D_MODEL]
    ffn_up = lambda h: (_silu(_dot(h, wg_ref[...])) * _dot(h, wu_ref[...])).astype(jnp.bfloat16)
    ffn_down = lambda x1, hid: x1 + gate_ffn * _rms(_dot(hid, wd_ref[...]), nffn_ref[...])
    if not interleave:
        staged = [mixer_dots(r, mixer_pre(r)) for r in range(n_sub)]
        return [ffn_down(x1, ffn_up(h)) for x1, h in staged]
    outs = []
    x1, h = mixer_dots(0, mixer_pre(0))
    for r in range(n_sub):
        nxt_pre = mixer_pre(r + 1) if r + 1 < n_sub else None
        hid = ffn_up(h)
        nxt = mixer_dots(r + 1, nxt_pre) if r + 1 < n_sub else None
        outs.append(ffn_down(x1, hid))
        if nxt is not None:
            x1, h = nxt
    return outs


def _ffn_specs(l, vec_rows):
    return [_vec_spec(vec_rows[name] + l) for name in ("norm_post_mix", "norm_pre_ffn", "norm_post_ffn")] + [
        _const_spec((D_MODEL, D_FF)), _const_spec((D_MODEL, D_FF)), _const_spec((D_FF, D_MODEL))]


def _outproj_ffn_kernel(*refs, n_cast):
    (xp_ref, xs_ref, attp_ref, atts_ref, yp_ref, ys_ref, z_ref, mod_ref, sn_ref, wo_ref,
     npost_ref, npre_ref, nffn_ref, wg_ref, wu_ref, wd_ref) = refs[:16]
    cast_src, o_ref, cast_dst = refs[16:16 + n_cast], refs[16 + n_cast], refs[17 + n_cast:]
    i = pl.program_id(0)
    _cast_chunks(cast_src, cast_dst)
    gw = D_SSD // SSD_GROUPS
    sub = lambda r: slice(r * SUB_ROWS, (r + 1) * SUB_ROWS)

    def mixer_pre(r):
        yg = _pick_group(i, yp_ref, ys_ref, sub(r)) * _silu(z_ref[sub(r), :])
        parts = [_pick_group(i, attp_ref, atts_ref, sub(r))]
        for g in range(SSD_GROUPS):
            parts.append(_rms(yg[:, g * gw:(g + 1) * gw], sn_ref[:, g * gw:(g + 1) * gw]).astype(jnp.bfloat16))
        return jnp.concatenate(parts, axis=1)

    def mixer_dots(r, cat):
        return _post_mix(_pick_group(i, xp_ref, xs_ref, sub(r)), _dot(cat, wo_ref[...]),
                         mod_ref, npost_ref, npre_ref)

    outs = _subtile_pipeline(TM // SUB_ROWS, mixer_pre, mixer_dots, mod_ref, nffn_ref, wg_ref, wu_ref, wd_ref,
                             interleave=False)
    for r, res in enumerate(outs):
        o_ref[sub(r), :] = res


def _outproj_ffn(l, i_ab, vec_table, vec_rows, xp, xs, att_p, att_s, y_p, y_s, z, mod, w_out, wg, wu, wd, cast):
    d_cat = MLA_HEADS * V_DIM + D_SSD
    assert not cast or N_TILES == N_CAST
    cast_out = [_cast_out(w) for w, _ in cast]
    return pl.pallas_call(
        functools.partial(_outproj_ffn_kernel, n_cast=len(cast)),
        grid=(N_TILES,),
        in_specs=(_group_specs(D_MODEL) + _group_specs(MLA_HEADS * V_DIM) + _group_specs(D_SSD)
                  + [_row_spec(D_SSD), _mod_spec(l), _vec_spec(vec_rows["ssd_norm"] + i_ab),
                     _const_spec((d_cat, D_MODEL))] + _ffn_specs(l, vec_rows)
                  + [_cast_in_spec(w, layer) for w, layer in cast]),
        out_specs=[_row_spec(D_MODEL)] + [spec for spec, _ in cast_out],
        out_shape=[jax.ShapeDtypeStruct((N_TOK, D_MODEL), jnp.float32)] + [shape for _, shape in cast_out],
        compiler_params=pltpu.CompilerParams(dimension_semantics=("arbitrary",), vmem_limit_bytes=VMEM_LIMIT),
        name="outproj_ffn",
    )(xp, xs, att_p, att_s, y_p, y_s, z, mod, vec_table, w_out, vec_table, vec_table, vec_table, wg, wu, wd,
      *[w for w, _ in cast])


def _pool_ffn_kernel(x_ref, xp_ref, xn_ref, mod_ref, nmix_ref, pw_ref, ps_ref,
                     npost_ref, npre_ref, nffn_ref, wg_ref, wu_ref, wd_ref, op_ref, os_ref):
    i = pl.program_id(0)

    @pl.when(i == 0)
    def _():
        os_ref[...] = jnp.zeros_like(os_ref)

    seq = jnp.where(i < N_PROMPT // TM, SEQ, DEC_SEQ)
    pos0 = (i * TM) % seq
    sh = mod_ref[0, :, 0:D_MODEL]
    sc = mod_ref[0, :, D_MODEL:2 * D_MODEL]
    hmod = lambda v: _rms(v, nmix_ref[...]) * (1.0 + sc) + sh
    n_rows = POOL_SUB + 2 * HALO

    def shifted(v, s):
        return pltpu.roll(v, n_rows - s, 0)

    def mixer_pre(s):
        lo, hi = s * POOL_SUB, (s + 1) * POOL_SUB
        pos_s = (pos0 + lo) % seq
        h = hmod(x_ref[lo:hi, :])
        before = hmod(xp_ref[...] if s == 0 else x_ref[lo - HALO:lo, :])
        after = hmod(xn_ref[...] if hi == TM else x_ref[hi:hi + HALO, :])
        before = jnp.where(pos_s > 0, before, 0.0)
        after = jnp.where(pos_s + POOL_SUB < seq, after, 0.0)
        padded = jnp.concatenate([before, h, after], axis=0)
        pos = pos_s + lax.broadcasted_iota(jnp.int32, (POOL_SUB, 1), 0)
        pooled = []
        for gi, w in enumerate(POOL_WINDOWS):
            cols = slice(gi * POOL_GC, (gi + 1) * POOL_GC)
            t = padded[:, cols]
            span = 1
            while span < w:
                t = t + shifted(t, span)
                span *= 2
            lead = HALO - w // 2
            win_sum = (shifted(t, lead) if lead else t)[:POOL_SUB, :]
            cnt = (jnp.minimum(pos + w // 2, seq) - jnp.maximum(pos - w // 2, 0)).astype(jnp.float32)
            pooled.append((win_sum / cnt - h[:, cols]).astype(jnp.bfloat16))
        return pooled

    def mixer_dots(s, pooled):
        mix = jnp.concatenate([_dot(p, pw_ref[gi]) for gi, p in enumerate(pooled)], axis=1) * ps_ref[...]
        return _post_mix(x_ref[s * POOL_SUB:(s + 1) * POOL_SUB, :], mix, mod_ref, npost_ref, npre_ref)

    res = jnp.concatenate(_subtile_pipeline(TM // POOL_SUB, mixer_pre, mixer_dots, mod_ref, nffn_ref,
                                            wg_ref, wu_ref, wd_ref, interleave=True), axis=0)

    @pl.when(i < N_PT)
    def _():
        op_ref[...] = res

    @pl.when(i >= N_PT)
    def _():
        os_ref[...] = res


def _pool_ffn(l, j_c, vec_table, vec_rows, xa, mod, pool_w, wg, wu, wd):
    hb = TM // HALO
    nh = N_TOK // HALO
    return pl.pallas_call(
        _pool_ffn_kernel,
        grid=(N_TILES,),
        in_specs=[_row_spec(D_MODEL),
                  pl.BlockSpec((HALO, D_MODEL), lambda i: (jnp.maximum(i * hb - 1, 0), 0)),
                  pl.BlockSpec((HALO, D_MODEL), lambda i: (jnp.minimum((i + 1) * hb, nh - 1), 0)),
                  _mod_spec(l),
                  _vec_spec(vec_rows["norm_pre_mix"] + l),
                  _const_spec((len(POOL_WINDOWS), POOL_GC, POOL_GC)),
                  _vec_spec(vec_rows["pool_scale"] + j_c)] + _ffn_specs(l, vec_rows),
        out_specs=_group_specs(D_MODEL),
        out_shape=[jax.ShapeDtypeStruct((N_PROMPT, D_MODEL), jnp.float32),
                   jax.ShapeDtypeStruct((N_SAMPLE, D_MODEL), jnp.float32)],
        compiler_params=pltpu.CompilerParams(dimension_semantics=("arbitrary",), vmem_limit_bytes=VMEM_LIMIT),
        name="pool_ffn",
    )(xa, xa, xa, mod, vec_table, pool_w, vec_table, vec_table, vec_table, vec_table, wg, wu, wd)


def _rope_tables():
    f32 = np.float32
    rows = DEC_SEQ // GRID_W
    r = np.repeat(np.arange(rows, dtype=f32), GRID_W)
    c = np.tile(np.arange(GRID_W, dtype=f32), rows)
    half = ROPE_DIM // 2
    inv_freq = np.power(f32(ROPE_THETA), -np.arange(0, half, 2, dtype=f32) / f32(half)).astype(f32)
    ang = np.concatenate([r[:, None] * inv_freq, c[:, None] * inv_freq], axis=-1).astype(f32)
    cos, sin = np.cos(ang).astype(f32), np.sin(ang).astype(f32)
    ones = np.ones((DEC_SEQ, KPE_LANE0), f32)
    zl = np.zeros((DEC_SEQ, KPE_LANE0), f32)
    zr = np.zeros((DEC_SEQ, LANES - KPE_LANE0 - ROPE_DIM), f32)
    cos_t = np.concatenate([ones, cos, cos, zr], axis=1)
    sin_t = np.concatenate([zl, -sin, sin, zr], axis=1)
    keep = np.concatenate([np.ones((TM, KPE_LANE0 + ROPE_DIM), f32), np.zeros((TM, LANES - KPE_LANE0 - ROPE_DIM), f32)],
                          axis=1)
    return (jnp.asarray(np.concatenate([cos_t, keep], axis=0)),
            jnp.asarray(np.concatenate([sin_t, np.zeros((TM, LANES), f32)], axis=0)))


def _kpe_slab(k):
    pad = [(0, 0)] * (k.ndim - 1) + [(KPE_LANE0, LANES - KPE_LANE0 - ROPE_DIM)]
    return jnp.pad(k, pad)


def _with_swapped_pair(w_pe):
    half = ROPE_DIM // 2
    return jnp.concatenate([w_pe, w_pe[..., half:], w_pe[..., :half]], axis=-1)


def _layout_in_proj(w):
    o = np.cumsum((0, Q_RANK, KV_RANK, ROPE_DIM, D_SSD, CONV_CH, SSD_HEADS, SSD_HEADS))
    cq, ckv, kpe, z, xbc, dtf, dtb = (w[:, o[k]:o[k + 1]] for k in range(7))
    dt = jnp.pad(jnp.concatenate([dtf, dtb], axis=1), ((0, 0), (0, LANES - 2 * SSD_HEADS)))
    kpe_slab = jnp.pad(_with_swapped_pair(kpe), ((0, 0), (KPE_LANE0, LANES - KPE_LANE0 - 2 * ROPE_DIM)))
    return jnp.concatenate([cq, ckv, z, xbc, kpe_slab, dt], axis=1).astype(jnp.bfloat16)


def _layout_uq(w):
    w = w.reshape(Q_RANK, MLA_HEADS, NOPE_DIM + ROPE_DIM)
    w = jnp.concatenate([w[:, :, :NOPE_DIM], _with_swapped_pair(w[:, :, NOPE_DIM:])], axis=-1)
    assert w.shape[-1] == HEAD_SLAB
    return w.reshape(Q_RANK, MLA_HEADS * HEAD_SLAB).astype(jnp.bfloat16)


def _layout_ukv(w):
    w = w.reshape(KV_RANK, MLA_HEADS, NOPE_DIM + V_DIM)
    kn = jnp.pad(w[:, :, :NOPE_DIM], ((0, 0), (0, 0), (0, HEAD_SLAB - NOPE_DIM)))
    v = w[:, :, NOPE_DIM:]
    return kn.reshape(KV_RANK, -1).astype(jnp.bfloat16), v.reshape(KV_RANK, -1).T.astype(jnp.bfloat16)


def _lane_row(fwd, bwd):
    return jnp.pad(jnp.concatenate([fwd, bwd]), (0, LANES - 2 * SSD_HEADS)).reshape(1, LANES)


def kernel(x_prompt, x_sample, c, cache_mla_ckv, cache_mla_krope, state_ssd_fwd, state_ssd_bwd, c_ctx, w_mod, b_mod, norm_pre_mix, norm_post_mix, norm_pre_ffn, norm_post_ffn, w_in_ab, q_norm, w_uq, kv_norm, w_ukv, ssd_conv_w, ssd_conv_b, ssd_dt_bias_fwd, ssd_dt_bias_bwd, ssd_a_log_fwd, ssd_a_log_bwd, ssd_d, ssd_norm, w_out_ab, pool_w, pool_scale, ffn_w_gate, ffn_w_up, ffn_w_down):
    f32, bf16 = jnp.float32, jnp.bfloat16
    assert DEPTH == 2
    xp = x_prompt.reshape(N_PROMPT, D_MODEL)
    xs = x_sample.reshape(N_SAMPLE, D_MODEL)
    cvecs = jnp.concatenate([c_ctx[None, :], c, jnp.zeros((MOD_ROWS - N_MODVEC, D_MODEL), f32)], axis=0)
    mod = _modulation(cvecs, w_mod, b_mod).reshape(DEPTH, SUBLANES, 1, 6 * D_MODEL)
    tabs = _rope_tables()
    hp = SSD_HEADS * SSD_HEAD_DIM
    row = lambda v: v.reshape(1, -1)
    new_ckv, new_kpe, new_hf, new_hb = [], [], [], []

    vec_table, vec_rows = _pack_vectors(
        norm_pre_mix=norm_pre_mix, norm_post_mix=norm_post_mix, norm_pre_ffn=norm_pre_ffn, norm_post_ffn=norm_post_ffn,
        q_norm=q_norm, kv_norm=kv_norm, ssd_norm=ssd_norm, ssd_conv_b=ssd_conv_b, pool_scale=pool_scale)
    ffn_w_f32 = (ffn_w_gate, ffn_w_up, ffn_w_down)
    for l in range(DEPTH):
        if l % 2 == 0:
            i = l // 2
            q, ckv_n, kpe, z, xbc, dts_t, cum_t, w_t, ckv_prompt, kpe_prompt = _inproj(
                l, vec_table, vec_rows, xp, xs, mod, tabs, _layout_in_proj(w_in_ab[i]), _layout_uq(w_uq[i]),
                _lane_row(ssd_dt_bias_fwd[i], ssd_dt_bias_bwd[i]).T, _lane_row(ssd_a_log_fwd[i], ssd_a_log_bwd[i]).T)
            heads = (dts_t, cum_t, w_t)
            w_kv = _layout_ukv(w_ukv[i])
            (att_p,) = _attention(q, ckv_n, kpe, w_kv, 0, BATCH, SEQ, n_pack=PROMPT_PACK)
            att_s, w_out, *ffn_w = _attention(q, ckv_n, kpe, w_kv, N_PROMPT, DEC_BATCH, DEC_SEQ,
                                              cache=(cache_mla_ckv[:, i], _kpe_slab(cache_mla_krope[:, i])),
                                              cast=[(w_out_ab, i)] + [(w, l) for w in ffn_w_f32])
            ssd_args = (ssd_conv_w, vec_table, vec_rows["ssd_conv_b"], row(jnp.repeat(ssd_d[i], SSD_HEAD_DIM)))
            y_p, hf, hb = _ssd(i, xbc, heads, None, *ssd_args, 0, BATCH, SEQ)
            y_s, _, _ = _ssd(i, xbc, heads, (state_ssd_fwd[:, i].reshape(DEC_BATCH, hp, SSD_STATE),
                                             state_ssd_bwd[:, i].reshape(DEC_BATCH, hp, SSD_STATE)),
                             *ssd_args, N_PROMPT, DEC_BATCH, DEC_SEQ)
            xa, *ffn_w = _outproj_ffn(l, i, vec_table, vec_rows, xp, xs, att_p, att_s, y_p, y_s, z, mod, w_out, *ffn_w,
                                      cast=[(w, l + 1) for w in ffn_w_f32])
            new_ckv.append(ckv_prompt.reshape(BATCH, SEQ, KV_RANK))
            new_kpe.append(kpe_prompt.reshape(BATCH, SEQ, ROPE_DIM))
            new_hf.append(hf.reshape(BATCH, SSD_HEADS, SSD_HEAD_DIM, SSD_STATE))
            new_hb.append(hb.reshape(BATCH, SSD_HEADS, SSD_HEAD_DIM, SSD_STATE))
        else:
            j = l // 2
            yp, ys = _pool_ffn(l, j, vec_table, vec_rows, xa, mod, pool_w[j].astype(bf16), *ffn_w)

    return (yp.reshape(BATCH, SEQ, D_MODEL), ys.reshape(DEC_BATCH, DEC_SEQ, D_MODEL),
            jnp.stack(new_ckv, axis=1), jnp.stack(new_kpe, axis=1),
            jnp.stack(new_hf, axis=1), jnp.stack(new_hb, axis=1))
```

```python
import functools

import numpy as np
import jax
import jax.numpy as jnp
from jax import lax
from jax.experimental import pallas as pl
from jax.experimental.pallas import tpu as pltpu

D_MODEL = 1024
BATCH = 16
SEQ = 256
DEPTH = 2
DEC_BATCH = 2
DEC_SEQ = 2048
PAST_LEN = 256
GRID_W = 64
EPS = 1e-6
MLA_HEADS = 8
Q_RANK = 256
KV_RANK = 256
NOPE_DIM = 64
ROPE_DIM = 32
V_DIM = 64
ROPE_THETA = 10000.0
SSD_HEADS = 8
SSD_GROUPS = 2
SSD_HPG = SSD_HEADS // SSD_GROUPS
SSD_HEAD_DIM = 64
SSD_STATE = 128
D_SSD = SSD_HEADS * SSD_HEAD_DIM
CONV_W = 5
CONV_CH = D_SSD + 2 * SSD_GROUPS * SSD_STATE
POOL_WINDOWS = (2, 4, 8, 16)
POOL_GC = D_MODEL // len(POOL_WINDOWS)
D_FF = ((8 * D_MODEL + 3 * 256 - 1) // (3 * 256)) * 256

SUBLANES = 8
LANES = 128

N_PROMPT = BATCH * SEQ
N_SAMPLE = DEC_BATCH * DEC_SEQ
N_TOK = N_PROMPT + N_SAMPLE
N_MODVEC = 1 + DEC_BATCH
TM = 512
TQ = 256
CHUNK = 128
HALO = SUBLANES
HEAD_SLAB = LANES
IN_COLS = Q_RANK + KV_RANK + D_SSD + CONV_CH + 2 * LANES
KPE_LANE0 = NOPE_DIM
VMEM_LIMIT = 56 * 1024 * 1024

POOL_SUB = min(SEQ, DEC_SEQ)

assert TM % POOL_SUB == 0 and SEQ % POOL_SUB == 0 and DEC_SEQ % TM == 0 and N_PROMPT % DEC_SEQ == 0


def _rms(x, g):
    return x * lax.rsqrt(jnp.mean(x * x, axis=-1, keepdims=True) + EPS) * g


def _silu(x):
    return x * jax.nn.sigmoid(x)


def _softplus(x):
    return jnp.maximum(x, 0.0) + jnp.log1p(jnp.exp(-jnp.abs(x)))


def _dot(a, b):
    return jnp.dot(a, b, preferred_element_type=jnp.float32)


def _dot_nt(a, b):
    return lax.dot_general(a, b, (((1,), (1,)), ((), ())), preferred_element_type=jnp.float32)


def _mod_row(i):
    return jnp.where(i < N_PROMPT // TM, 0, 1 + (i - N_PROMPT // TM) // (DEC_SEQ // TM))


def _const_spec(shape):
    nd = len(shape)
    return pl.BlockSpec(shape, lambda *_: (0,) * nd, pipeline_mode=pl.Buffered(1))


N_PT = N_PROMPT // TM
N_TILES = N_TOK // TM


def _row_spec(width):
    return pl.BlockSpec((TM, width), lambda i: (i, 0))


def _mod_spec(l):
    return pl.BlockSpec((None, SUBLANES, 6 * D_MODEL), lambda i: (l, 0, 0), pipeline_mode=pl.Buffered(1))


def _mod_part(mod_ref, k):
    return mod_ref[pl.ds(_mod_row(pl.program_id(0)), 1), k * D_MODEL:(k + 1) * D_MODEL]


def _vec_spec(row):
    return pl.BlockSpec((None, 1, D_MODEL), lambda *_: (row, 0, 0), pipeline_mode=pl.Buffered(1))


def _pack_vectors(**params):
    first_row, blocks, n = {}, [], 0
    for name, a in params.items():
        first_row[name] = n
        n += a.shape[0]
        blocks.append(jnp.pad(a, ((0, 0), (0, D_MODEL - a.shape[1]))))
    return jnp.concatenate(blocks, axis=0).reshape(n, 1, D_MODEL), first_row


def _group_specs(width):
    return [pl.BlockSpec((TM, width), lambda i: (jnp.minimum(i, N_PT - 1), 0)),
            pl.BlockSpec((TM, width), lambda i: (jnp.maximum(i - N_PT, 0), 0))]


N_CAST = 16


def _cast_in_spec(w, layer, step=lambda i: i):
    _, rows, cols = w.shape
    return pl.BlockSpec((1, rows // N_CAST, cols), lambda *g: (layer, step(*g), 0))


def _cast_out(w, step=lambda i: i):
    _, rows, cols = w.shape
    return (pl.BlockSpec((rows // N_CAST, cols), lambda *g: (step(*g), 0)),
            jax.ShapeDtypeStruct((rows, cols), jnp.bfloat16))


def _cast_chunks(src_refs, dst_refs):
    for src, dst in zip(src_refs, dst_refs):
        dst[...] = src[0].astype(jnp.bfloat16)


def _pick_group(i, p_ref, s_ref, rows=slice(None)):
    return jnp.where(i < N_PT, p_ref[rows, :], s_ref[rows, :])


MOD_TK = 256
MOD_STREAMS = 2
PROMPT_PACK = 4
MOD_ROWS = 2 * SUBLANES
SUB_ROWS = 256


def _split3(a):
    a_hi = a.astype(jnp.bfloat16)
    r1 = a - a_hi.astype(jnp.float32)
    a_mid = r1.astype(jnp.bfloat16)
    a_lo = (r1 - a_mid.astype(jnp.float32)).astype(jnp.bfloat16)
    return a_hi, a_mid, a_lo


def _mod_kernel(c_ref, *refs):
    w_refs, b_ref, o_ref = refs[:MOD_STREAMS], refs[MOD_STREAMS], refs[MOD_STREAMS + 1]
    k = pl.program_id(1)
    rows = MOD_TK // MOD_STREAMS
    s_all = _silu(c_ref[...])
    part = jnp.zeros((MOD_ROWS, 6 * D_MODEL), jnp.float32)
    for j, w_ref in enumerate(w_refs):
        s_hi, s_mid, s_lo = _split3(s_all[:, j * rows:(j + 1) * rows])
        w = w_ref[0]
        w_hi = w.astype(jnp.bfloat16)
        w_lo = (w - w_hi.astype(jnp.float32)).astype(jnp.bfloat16)
        top = _dot(jnp.concatenate([s_hi, s_mid, s_lo], axis=0), w_hi)
        low = _dot(jnp.concatenate([s_hi, s_mid], axis=0), w_lo)
        part = part + (top[:MOD_ROWS] + top[MOD_ROWS:2 * MOD_ROWS] + top[2 * MOD_ROWS:]
                       + low[:MOD_ROWS] + low[MOD_ROWS:])
    part = part[:SUBLANES]

    @pl.when(k == 0)
    def _():
        o_ref[0] = part + b_ref[pl.ds(pl.program_id(0), 1), :]

    @pl.when(k > 0)
    def _():
        o_ref[0] += part


def _modulation(cvecs, w_mod, b_mod):
    n = 6 * D_MODEL
    w_spec = lambda j: pl.BlockSpec((1, MOD_TK // MOD_STREAMS, n), lambda l, k: (l, k * MOD_STREAMS + j, 0))
    return pl.pallas_call(
        _mod_kernel,
        grid=(DEPTH, D_MODEL // MOD_TK),
        in_specs=[pl.BlockSpec((MOD_ROWS, MOD_TK), lambda l, k: (0, k))]
        + [w_spec(j) for j in range(MOD_STREAMS)]
        + [pl.BlockSpec((DEPTH, n), lambda l, k: (0, 0))],
        out_specs=pl.BlockSpec((1, SUBLANES, n), lambda l, k: (l, 0, 0)),
        out_shape=jax.ShapeDtypeStruct((DEPTH, SUBLANES, n), jnp.float32),
        compiler_params=pltpu.CompilerParams(dimension_semantics=("arbitrary", "arbitrary"),
                                             vmem_limit_bytes=VMEM_LIMIT),
        name="modulation",
    )(cvecs, *([w_mod] * MOD_STREAMS), b_mod)


def _ssd_head_terms(dt_raw, dtb_c, a_neg_c):
    nh2 = 2 * SSD_HEADS
    row = lax.broadcasted_iota(jnp.int32, (CHUNK, CHUNK), 0)
    col = lax.broadcasted_iota(jnp.int32, (CHUNK, CHUNK), 1)
    upper_b = (row <= col).astype(jnp.bfloat16)
    lower_b = (row >= col).astype(jnp.bfloat16)
    fwd_rows = lax.broadcasted_iota(jnp.int32, (nh2, CHUNK), 0) < SSD_HEADS
    dts_t = _softplus(dt_raw.T[:nh2, :] + dtb_c)
    pieces = _split3(dts_t * a_neg_c)
    cum_t = jnp.where(fwd_rows, sum(_dot(p, upper_b) for p in pieces), sum(_dot(p, lower_b) for p in pieces))
    cum_t = cum_t * np.float32(np.log2(np.e))
    tot = jnp.where(fwd_rows[:, :1], cum_t[:, CHUNK - 1:], cum_t[:, :1])
    return dts_t, cum_t, dts_t * jnp.exp2(tot - cum_t)


def _inproj_kernel(xp_ref, xs_ref, mod_ref, cos_ref, sin_ref, w_in_ref, w_uq_ref, qn_ref, kvn_ref, npm_ref,
                   dtb_ref, alog_ref, q_ref, ckv_ref, kpe_ref, z_ref, xbc_ref, dts_ref, cumt_ref, wt_ref,
                   new_ckv_ref, new_kpe_ref):
    i = pl.program_id(0)
    sh = _mod_part(mod_ref, 0)
    sc = _mod_part(mod_ref, 1)
    scale = (NOPE_DIM + ROPE_DIM) ** -0.5 * np.log2(np.e)
    nh2 = 2 * SSD_HEADS
    diag = (lax.broadcasted_iota(jnp.int32, (nh2, LANES), 0) == lax.broadcasted_iota(jnp.int32, (nh2, LANES), 1))
    to_col = lambda ref: jnp.sum(jnp.where(diag, ref[:, :LANES], 0.0), axis=1, keepdims=True)
    dtb_c = to_col(dtb_ref)
    a_neg_c = -jnp.exp(to_col(alog_ref))
    dt_raw = []
    for r in range(TM // SUB_ROWS):
        rs = slice(r * SUB_ROWS, (r + 1) * SUB_ROWS)
        h = (_rms(_pick_group(i, xp_ref, xs_ref, rs), npm_ref[...]) * (1.0 + sc) + sh).astype(jnp.bfloat16)
        p = _dot(h, w_in_ref[...])
        o = 0
        cq = p[:, o:o + Q_RANK]; o += Q_RANK
        ckv = p[:, o:o + KV_RANK]; o += KV_RANK
        z_ref[rs, :] = p[:, o:o + D_SSD]; o += D_SSD
        xbc_ref[rs, :] = p[:, o:o + CONV_CH]; o += CONV_CH
        kpe = p[:, o:o + LANES]; o += LANES
        dt_raw.append(p[:, o:o + LANES])

        cos = cos_ref[rs, :]
        sin = sin_ref[rs, :]

        def rope(slab):
            return slab * cos + pltpu.roll(slab, LANES - ROPE_DIM, 1) * sin

        ckv_ref[rs, :] = _rms(ckv, kvn_ref[:, :KV_RANK])
        kpe_ref[rs, :] = rope(kpe)
        q = _dot(_rms(cq, qn_ref[:, :Q_RANK]).astype(jnp.bfloat16), w_uq_ref[...]) * scale
        for hd in range(MLA_HEADS):
            sl = slice(hd * HEAD_SLAB, (hd + 1) * HEAD_SLAB)
            q_ref[rs, sl] = rope(q[:, sl]).astype(jnp.bfloat16)

    dt_all = jnp.concatenate(dt_raw, axis=0)
    for ck in range(TM // CHUNK):
        terms = _ssd_head_terms(dt_all[ck * CHUNK:(ck + 1) * CHUNK, :], dtb_c, a_neg_c)
        for ref, val in zip((dts_ref, cumt_ref, wt_ref), terms):
            ref[ck * nh2:(ck + 1) * nh2, :] = val

    @pl.when(i < N_PT)
    def _():
        new_ckv_ref[...] = ckv_ref[...]
        kpe_t = kpe_ref[...].T
        for b in range(TM // SEQ):
            new_kpe_ref[b] = kpe_t[KPE_LANE0:KPE_LANE0 + ROPE_DIM, b * SEQ:(b + 1) * SEQ]


def _inproj(l, vec_table, vec_rows, xp, xs, mod, tabs, w_in, w_uq):
    nt = N_TILES
    prompt_blk = lambda i: (jnp.minimum(i, N_PT - 1), 0)
    lat_tiles = DEC_SEQ // TM
    tab_spec = pl.BlockSpec((TM, LANES), lambda i: (jnp.where(i < N_PT, lat_tiles, (i - N_PT) % lat_tiles), 0))
    row = _row_spec
    hrows = TM // CHUNK * 2 * SSD_HEADS
    head_spec = pl.BlockSpec((hrows, CHUNK), lambda i: (i, 0))
    head_shape = jax.ShapeDtypeStruct((nt * hrows, CHUNK), jnp.float32)
    return pl.pallas_call(
        _inproj_kernel,
        grid=(nt,),
        in_specs=_group_specs(D_MODEL) + [
            _mod_spec(l),
            tab_spec, tab_spec,
            _const_spec((D_MODEL, IN_COLS)),
            _const_spec((Q_RANK, MLA_HEADS * HEAD_SLAB)),
            _vec_spec(vec_rows["q_norm"] + l // 2),
            _vec_spec(vec_rows["kv_norm"] + l // 2),
            _vec_spec(vec_rows["norm_pre_mix"] + l),
            _vec_spec(vec_rows["ssd_dt_bias"] + l // 2),
            _vec_spec(vec_rows["ssd_a_log"] + l // 2),
        ],
        out_specs=[row(MLA_HEADS * HEAD_SLAB), row(KV_RANK), row(LANES), row(D_SSD), row(CONV_CH),
                   head_spec, head_spec, head_spec,
                   pl.BlockSpec((TM, KV_RANK), prompt_blk),
                   pl.BlockSpec((TM // SEQ, ROPE_DIM, SEQ), lambda i: (jnp.minimum(i, N_PT - 1), 0, 0))],
        out_shape=[
            jax.ShapeDtypeStruct((N_TOK, MLA_HEADS * HEAD_SLAB), jnp.bfloat16),
            jax.ShapeDtypeStruct((N_TOK, KV_RANK), jnp.float32),
            jax.ShapeDtypeStruct((N_TOK, LANES), jnp.float32),
            jax.ShapeDtypeStruct((N_TOK, D_SSD), jnp.float32),
            jax.ShapeDtypeStruct((N_TOK, CONV_CH), jnp.float32),
            head_shape, head_shape, head_shape,
            jax.ShapeDtypeStruct((N_PROMPT, KV_RANK), jnp.float32),
            jax.ShapeDtypeStruct((BATCH, ROPE_DIM, SEQ), jnp.float32),
        ],
        compiler_params=pltpu.CompilerParams(dimension_semantics=("arbitrary",), vmem_limit_bytes=VMEM_LIMIT),
        name="inproj",
    )(xp, xs, mod, *tabs, w_in, w_uq, *([vec_table] * 5))


def _attn_kernel(*refs, lk_cache, lk_new, n_pack, n_cast):
    n_in = 7 if lk_cache else 5
    if lk_cache:
        q_ref, ckv_ref, kpe_ref, ckvc_ref, kpec_ref, wk_ref, wvt_ref = refs[:n_in]
    else:
        q_ref, ckv_ref, kpe_ref, wk_ref, wvt_ref = refs[:n_in]
    cast_src, o_ref = refs[n_in:n_in + n_cast], refs[n_in + n_cast]
    cast_dst = refs[n_in + n_cast + 1:n_in + 2 * n_cast + 1]
    k_scr, vt_scr = refs[n_in + 2 * n_cast + 1:]
    _cast_chunks(cast_src, cast_dst)
    lk = lk_cache + lk_new

    @pl.when(pl.program_id(1) == 0)
    def _expand_kv():
        def expand(ckv, kpe, r0):
            ckv_b = ckv.astype(jnp.bfloat16)
            kn = _dot(ckv_b, wk_ref[...])
            rows = slice(r0, r0 + ckv.shape[0])
            for hd in range(MLA_HEADS):
                k_scr[hd, rows, :] = (kn[:, hd * HEAD_SLAB:(hd + 1) * HEAD_SLAB] + kpe).astype(jnp.bfloat16)
            vt_scr[:, rows] = _dot_nt(wvt_ref[...], ckv_b).astype(jnp.bfloat16)

        step = 256
        for r0 in range(0, lk_cache, step):
            expand(ckvc_ref[0, r0:r0 + step, :], kpec_ref[0, r0:r0 + step, :], r0)
        for r0 in range(0, n_pack * lk_new, step):
            expand(ckv_ref[r0:r0 + step, :], kpe_ref[r0:r0 + step, :], lk_cache + r0)

    work = [(s, hd) for s in range(n_pack) for hd in range(MLA_HEADS)]
    scores = [_dot_nt(k_scr[hd, s * lk:(s + 1) * lk, :], q_ref[s * TQ:(s + 1) * TQ, hd * HEAD_SLAB:(hd + 1) * HEAD_SLAB])
              for s, hd in work]
    outs = []
    for (s, hd), s_t in zip(work, scores):
        p_t = jnp.exp2(s_t - jnp.max(s_t, axis=0, keepdims=True))
        den = jnp.sum(p_t, axis=0, keepdims=True)
        v_t = vt_scr[hd * V_DIM:(hd + 1) * V_DIM, s * lk:(s + 1) * lk]
        outs.append(_dot(v_t, p_t.astype(jnp.bfloat16)) / den)
    for s in range(n_pack):
        o_ref[s * TQ:(s + 1) * TQ, :] = jnp.concatenate(
            outs[s * MLA_HEADS:(s + 1) * MLA_HEADS], axis=0).T.astype(jnp.bfloat16)


def _attention(q, ckv_n, kpe, w_ukv, row_off, n_batch, seq, cache=None, n_pack=1, cast=()):
    nq = seq // TQ
    assert n_pack == 1 or (nq == 1 and cache is None and n_batch % n_pack == 0)
    n_batch //= n_pack
    lk_cache = 0 if cache is None else cache[0].shape[1]
    lk = n_pack * (lk_cache + seq)
    qblk = lambda b, qi: (row_off // (n_pack * TQ) + b * nq + qi, 0)
    sblk = lambda b, qi: (row_off // (n_pack * seq) + b, 0)
    in_specs = [
        pl.BlockSpec((n_pack * TQ, MLA_HEADS * HEAD_SLAB), qblk),
        pl.BlockSpec((n_pack * seq, KV_RANK), sblk),
        pl.BlockSpec((n_pack * seq, LANES), sblk),
    ]
    args = [q, ckv_n, kpe]
    if cache is not None:
        in_specs += [pl.BlockSpec((1, lk_cache, KV_RANK), lambda b, qi: (b, 0, 0)),
                     pl.BlockSpec((1, lk_cache, LANES), lambda b, qi: (b, 0, 0))]
        args += list(cache)
    in_specs += [_const_spec(w.shape) for w in w_ukv]
    args += list(w_ukv)
    assert not cast or n_batch * nq == N_CAST
    step = lambda b, qi: b * nq + qi
    in_specs += [_cast_in_spec(w, layer, step) for w, layer in cast]
    args += [w for w, _ in cast]
    cast_out = [_cast_out(w, step) for w, _ in cast]
    return pl.pallas_call(
        functools.partial(_attn_kernel, lk_cache=lk_cache, lk_new=seq, n_pack=n_pack, n_cast=len(cast)),
        grid=(n_batch, nq),
        in_specs=in_specs,
        out_specs=[pl.BlockSpec((n_pack * TQ, MLA_HEADS * V_DIM), lambda b, qi: (b * nq + qi, 0))]
        + [spec for spec, _ in cast_out],
        out_shape=[jax.ShapeDtypeStruct((n_batch * n_pack * seq, MLA_HEADS * V_DIM), jnp.bfloat16)]
        + [shape for _, shape in cast_out],
        scratch_shapes=[pltpu.VMEM((MLA_HEADS, lk, HEAD_SLAB), jnp.bfloat16),
                        pltpu.VMEM((MLA_HEADS * V_DIM, lk), jnp.bfloat16)],
        compiler_params=pltpu.CompilerParams(dimension_semantics=("arbitrary", "arbitrary"),
                                             vmem_limit_bytes=VMEM_LIMIT),
        name=f"attention_{seq}",
    )(*args)


def _ssd_kernel(*refs, seq, zero_init):
    if zero_init:
        (xbc_ref, dts_ref, cumt_ref, wt_ref, cw_ref, cb_ref, dsk_ref,
         y_ref, hf_ref, hb_ref, xs_scr, c_scr, bt_scr, cum_scr, stf_scr, stb_scr) = refs
    else:
        (xbc_ref, dts_ref, cumt_ref, wt_ref, h0f_ref, h0b_ref, cw_ref, cb_ref, dsk_ref,
         y_ref, hf_ref, hb_ref, xs_scr, c_scr, bt_scr, cum_scr, stf_scr, stb_scr) = refs
    nc = seq // CHUNK
    gs = SSD_GROUPS * SSD_STATE
    nh2 = 2 * SSD_HEADS

    row = lax.broadcasted_iota(jnp.int32, (CHUNK, CHUNK), 0)
    col = lax.broadcasted_iota(jnp.int32, (CHUNK, CHUNK), 1)
    low_half = col < SSD_HEAD_DIM
    lower = row >= col
    upper = row <= col

    def prep_chunk(c, carry):
        r0 = pl.multiple_of(c * CHUNK, CHUNK)
        rows = pl.ds(r0, CHUNK)
        rows_prev = pl.ds(pl.multiple_of(jnp.maximum(r0 - HALO, 0), HALO), HALO)
        rows_next = pl.ds(pl.multiple_of(jnp.minimum(r0 + CHUNK, seq - HALO), HALO), HALO)

        def conv_tile(cs):
            prev = jnp.where(c > 0, xbc_ref[rows_prev, cs], 0.0)
            nxt = jnp.where(c < nc - 1, xbc_ref[rows_next, cs], 0.0)
            win = jnp.concatenate([prev, xbc_ref[rows, cs], nxt], axis=0)
            acc = jnp.broadcast_to(cb_ref[:, cs], (CHUNK, LANES))
            for k in range(CONV_W):
                lo = HALO - CONV_W // 2 + k
                acc = acc + cw_ref[k:k + 1, cs] * win[lo:lo + CHUNK, :]
            return _silu(acc)

        def x_tile(j, carry):
            cs = pl.ds(pl.multiple_of(j * LANES, LANES), LANES)
            u = conv_tile(cs)
            y_ref[rows, cs] = dsk_ref[:, cs] * u
            xs_scr[0, rows, cs] = jnp.where(low_half, u, 0.0).astype(jnp.bfloat16)
            xs_scr[1, rows, cs] = jnp.where(low_half, 0.0, u).astype(jnp.bfloat16)
            return carry

        lax.fori_loop(0, D_SSD // LANES, x_tile, 0)
        cum_t = cumt_ref[pl.ds(pl.multiple_of(c * nh2, nh2), nh2), :]
        cum_scr[rows, :] = jnp.concatenate(
            [cum_t, jnp.zeros((CHUNK - nh2, CHUNK), jnp.float32)], axis=0).T[:, :nh2]
        for g in range(SSD_GROUPS):
            b0 = pl.multiple_of(c * gs + g * SSD_STATE, SSD_STATE)
            bt_scr[pl.ds(b0, SSD_STATE), :] = conv_tile(slice(D_SSD + g * SSD_STATE, D_SSD + (g + 1) * SSD_STATE)).T
            c_scr[rows, g * SSD_STATE:(g + 1) * SSD_STATE] = conv_tile(
                slice(D_SSD + gs + g * SSD_STATE, D_SSD + gs + (g + 1) * SSD_STATE)).astype(jnp.bfloat16)
        return carry

    lax.fori_loop(0, nc, prep_chunk, 0)

    if zero_init:
        stf_scr[...] = jnp.zeros_like(stf_scr)
        stb_scr[...] = jnp.zeros_like(stb_scr)
    else:
        stf_scr[...] = h0f_ref[0].T
        stb_scr[...] = h0b_ref[0].T

    def scan_open(ci, st_scr):
        rows = pl.ds(pl.multiple_of(ci * CHUNK, CHUNK), CHUNK)
        c_b = c_scr[rows, :]
        st = st_scr[...]
        bts, cbms, zs = [], [], []
        for g in range(SSD_GROUPS):
            cg = c_b[:, g * SSD_STATE:(g + 1) * SSD_STATE]
            bt = bt_scr[pl.ds(pl.multiple_of(ci * gs + g * SSD_STATE, SSD_STATE), SSD_STATE), :]
            gcols = slice(g * SSD_HPG * SSD_HEAD_DIM, (g + 1) * SSD_HPG * SSD_HEAD_DIM)
            bts.append(bt)
            cbms.append(_dot(cg, bt.astype(jnp.bfloat16)))
            zs.append(_dot(cg, st[:, gcols].astype(jnp.bfloat16)))
        return st, bts, cbms, zs

    def scan_pairs(ci, st_scr, reverse, opened):
        st, bts, cbms, zs = opened
        lane0 = SSD_HEADS if reverse else 0
        causal = upper if reverse else lower
        last = 0 if reverse else CHUNK - 1
        rows = pl.ds(pl.multiple_of(ci * CHUNK, CHUNK), CHUNK)
        hrows = pl.ds(pl.multiple_of(ci * nh2, nh2), nh2)
        xs_lo = xs_scr[0, rows, :]
        xs_hi = xs_scr[1, rows, :]
        dts_t = dts_ref[hrows, :]
        cum_t = cumt_ref[hrows, :]
        w_t = wt_ref[hrows, :]
        cum = cum_scr[rows, :]
        for pair in range(SSD_HEADS // 2):
            g, jj = divmod(pair, SSD_HPG // 2)
            pcols = slice(pair * LANES, (pair + 1) * LANES)
            lhs_y, lhs_s, entry = [], [], []
            for hd in (2 * pair, 2 * pair + 1):
                ln = lane0 + hd
                cum_i = jnp.broadcast_to(cum[:, ln:ln + 1], (CHUNK, CHUNK))
                dec = jnp.exp2(jnp.where(causal, cum_i - cum_t[ln:ln + 1, :], -jnp.inf))
                lhs_y.append((cbms[g] * dec * dts_t[ln:ln + 1, :]).astype(jnp.bfloat16))
                lhs_s.append((bts[g] * w_t[ln:ln + 1, :]).astype(jnp.bfloat16))
                entry.append(jnp.exp2(cum_i))
            lhs = jnp.concatenate([jnp.concatenate(lhs_y, axis=1), jnp.concatenate(lhs_s, axis=1)], axis=0)
            out = _dot(lhs, jnp.concatenate([xs_lo[:, pcols], xs_hi[:, pcols]], axis=0))
            ea = jnp.where(low_half, entry[0], entry[1])
            y_ref[rows, pcols] += out[:CHUNK] + zs[g][:, jj * LANES:(jj + 1) * LANES] * ea
            st_scr[:, pcols] = ea[last:last + 1, :] * st[:, pcols] + out[CHUNK:]

    def both(c, carry):
        opened_f = scan_open(c, stf_scr)
        opened_b = scan_open(nc - 1 - c, stb_scr)
        scan_pairs(c, stf_scr, False, opened_f)
        scan_pairs(nc - 1 - c, stb_scr, True, opened_b)
        return carry

    lax.fori_loop(0, nc, both, 0)
    hf_ref[0] = stf_scr[...].T
    hb_ref[0] = stb_scr[...].T


def _ssd(i_ab, xbc, head_terms, h0, conv_w, vec_table, conv_b_row, d_skip, row_off, n_batch, seq):
    assert CONV_CH == D_MODEL
    hp = SSD_HEADS * SSD_HEAD_DIM
    gs = SSD_GROUPS * SSD_STATE
    nc = seq // CHUNK
    seq_blk = lambda b: (row_off // seq + b, 0)
    head_spec = pl.BlockSpec((nc * 2 * SSD_HEADS, CHUNK), seq_blk)
    st_spec = pl.BlockSpec((1, hp, SSD_STATE), lambda b: (b, 0, 0))
    st_shape = jax.ShapeDtypeStruct((n_batch, hp, SSD_STATE), jnp.float32)
    h0 = () if h0 is None else tuple(h0)
    return pl.pallas_call(
        functools.partial(_ssd_kernel, seq=seq, zero_init=not h0),
        grid=(n_batch,),
        in_specs=[pl.BlockSpec((seq, CONV_CH), seq_blk), head_spec, head_spec, head_spec] + [st_spec] * len(h0) + [
            pl.BlockSpec((None, CONV_W, CONV_CH), lambda b: (i_ab, 0, 0), pipeline_mode=pl.Buffered(1)),
            _vec_spec(conv_b_row + i_ab), _const_spec((1, D_SSD))],
        out_specs=[pl.BlockSpec((seq, D_SSD), lambda b: (b, 0)), st_spec, st_spec],
        out_shape=[jax.ShapeDtypeStruct((n_batch * seq, D_SSD), jnp.float32), st_shape, st_shape],
        scratch_shapes=[pltpu.VMEM((2, seq, D_SSD), jnp.bfloat16),
                        pltpu.VMEM((seq, gs), jnp.bfloat16),
                        pltpu.VMEM((nc * gs, CHUNK), jnp.float32),
                        pltpu.VMEM((seq, 2 * SSD_HEADS), jnp.float32),
                        pltpu.VMEM((SSD_STATE, hp), jnp.float32),
                        pltpu.VMEM((SSD_STATE, hp), jnp.float32)],
        compiler_params=pltpu.CompilerParams(dimension_semantics=("arbitrary",), vmem_limit_bytes=VMEM_LIMIT),
        name=f"ssd_{seq}",
    )(xbc, *head_terms, *h0, conv_w, vec_table, d_skip)


def _post_mix(x, mix, mod_ref, npost_ref, npre_ref):
    d = D_MODEL
    gate_mix = _mod_part(mod_ref, 2)
    shf = _mod_part(mod_ref, 3)
    scf = _mod_part(mod_ref, 4)
    x1 = x + gate_mix * _rms(mix, npost_ref[...])
    return x1, (_rms(x1, npre_ref[...]) * (1.0 + scf) + shf).astype(jnp.bfloat16)


def _subtile_pipeline(n_sub, mixer_pre, mixer_dots, mod_ref, nffn_ref, wg_ref, wu_ref, wd_ref, interleave):
    gate_ffn = _mod_part(mod_ref, 5)
    ffn_up = lambda h: (_silu(_dot(h, wg_ref[...])) * _dot(h, wu_ref[...])).astype(jnp.bfloat16)
    ffn_down = lambda x1, hid: x1 + gate_ffn * _rms(_dot(hid, wd_ref[...]), nffn_ref[...])
    if not interleave:
        staged = [mixer_dots(r, mixer_pre(r)) for r in range(n_sub)]
        return [ffn_down(x1, ffn_up(h)) for x1, h in staged]
    outs = []
    x1, h = mixer_dots(0, mixer_pre(0))
    for r in range(n_sub):
        nxt_pre = mixer_pre(r + 1) if r + 1 < n_sub else None
        hid = ffn_up(h)
        nxt = mixer_dots(r + 1, nxt_pre) if r + 1 < n_sub else None
        outs.append(ffn_down(x1, hid))
        if nxt is not None:
            x1, h = nxt
    return outs


def _ffn_specs(l, vec_rows):
    return [_vec_spec(vec_rows[name] + l) for name in ("norm_post_mix", "norm_pre_ffn", "norm_post_ffn")] + [
        _const_spec((D_MODEL, D_FF)), _const_spec((D_MODEL, D_FF)), _const_spec((D_FF, D_MODEL))]


def _outproj_ffn_kernel(*refs, n_cast):
    (xp_ref, xs_ref, attp_ref, atts_ref, yp_ref, ys_ref, z_ref, mod_ref, sn_ref, wo_ref,
     npost_ref, npre_ref, nffn_ref, wg_ref, wu_ref, wd_ref) = refs[:16]
    cast_src, o_ref, cast_dst = refs[16:16 + n_cast], refs[16 + n_cast], refs[17 + n_cast:]
    i = pl.program_id(0)
    _cast_chunks(cast_src, cast_dst)
    gw = D_SSD // SSD_GROUPS
    sub = lambda r: slice(r * SUB_ROWS, (r + 1) * SUB_ROWS)

    def mixer_pre(r):
        yg = _pick_group(i, yp_ref, ys_ref, sub(r)) * _silu(z_ref[sub(r), :])
        parts = [_pick_group(i, attp_ref, atts_ref, sub(r))]
        for g in range(SSD_GROUPS):
            parts.append(_rms(yg[:, g * gw:(g + 1) * gw], sn_ref[:, g * gw:(g + 1) * gw]).astype(jnp.bfloat16))
        return jnp.concatenate(parts, axis=1)

    def mixer_dots(r, cat):
        return _post_mix(_pick_group(i, xp_ref, xs_ref, sub(r)), _dot(cat, wo_ref[...]),
                         mod_ref, npost_ref, npre_ref)

    outs = _subtile_pipeline(TM // SUB_ROWS, mixer_pre, mixer_dots, mod_ref, nffn_ref, wg_ref, wu_ref, wd_ref,
                             interleave=False)
    for r, res in enumerate(outs):
        o_ref[sub(r), :] = res


def _outproj_ffn(l, i_ab, vec_table, vec_rows, xp, xs, att_p, att_s, y_p, y_s, z, mod, w_out, wg, wu, wd, cast):
    d_cat = MLA_HEADS * V_DIM + D_SSD
    assert not cast or N_TILES == N_CAST
    cast_out = [_cast_out(w) for w, _ in cast]
    return pl.pallas_call(
        functools.partial(_outproj_ffn_kernel, n_cast=len(cast)),
        grid=(N_TILES,),
        in_specs=(_group_specs(D_MODEL) + _group_specs(MLA_HEADS * V_DIM) + _group_specs(D_SSD)
                  + [_row_spec(D_SSD), _mod_spec(l), _vec_spec(vec_rows["ssd_norm"] + i_ab),
                     _const_spec((d_cat, D_MODEL))] + _ffn_specs(l, vec_rows)
                  + [_cast_in_spec(w, layer) for w, layer in cast]),
        out_specs=[_row_spec(D_MODEL)] + [spec for spec, _ in cast_out],
        out_shape=[jax.ShapeDtypeStruct((N_TOK, D_MODEL), jnp.float32)] + [shape for _, shape in cast_out],
        compiler_params=pltpu.CompilerParams(dimension_semantics=("arbitrary",), vmem_limit_bytes=VMEM_LIMIT),
        name="outproj_ffn",
    )(xp, xs, att_p, att_s, y_p, y_s, z, mod, vec_table, w_out, vec_table, vec_table, vec_table, wg, wu, wd,
      *[w for w, _ in cast])


def _pool_ffn_kernel(x_ref, xp_ref, xn_ref, mod_ref, nmix_ref, pw_ref, ps_ref,
                     npost_ref, npre_ref, nffn_ref, wg_ref, wu_ref, wd_ref, op_ref, os_ref):
    i = pl.program_id(0)

    @pl.when(i == 0)
    def _():
        os_ref[...] = jnp.zeros_like(os_ref)

    seq = jnp.where(i < N_PROMPT // TM, SEQ, DEC_SEQ)
    pos0 = (i * TM) % seq
    sh = _mod_part(mod_ref, 0)
    sc = _mod_part(mod_ref, 1)
    hmod = lambda v: _rms(v, nmix_ref[...]) * (1.0 + sc) + sh
    n_rows = POOL_SUB + 2 * HALO

    def shifted(v, s):
        return pltpu.roll(v, n_rows - s, 0)

    def mixer_pre(s):
        lo, hi = s * POOL_SUB, (s + 1) * POOL_SUB
        pos_s = (pos0 + lo) % seq
        h = hmod(x_ref[lo:hi, :])
        before = hmod(xp_ref[...] if s == 0 else x_ref[lo - HALO:lo, :])
        after = hmod(xn_ref[...] if hi == TM else x_ref[hi:hi + HALO, :])
        before = jnp.where(pos_s > 0, before, 0.0)
        after = jnp.where(pos_s + POOL_SUB < seq, after, 0.0)
        padded = jnp.concatenate([before, h, after], axis=0)
        pos = pos_s + lax.broadcasted_iota(jnp.int32, (POOL_SUB, 1), 0)
        pooled = []
        for gi, w in enumerate(POOL_WINDOWS):
            cols = slice(gi * POOL_GC, (gi + 1) * POOL_GC)
            t = padded[:, cols]
            span = 1
            while span < w:
                t = t + shifted(t, span)
                span *= 2
            lead = HALO - w // 2
            win_sum = (shifted(t, lead) if lead else t)[:POOL_SUB, :]
            cnt = (jnp.minimum(pos + w // 2, seq) - jnp.maximum(pos - w // 2, 0)).astype(jnp.float32)
            pooled.append((win_sum / cnt - h[:, cols]).astype(jnp.bfloat16))
        return pooled

    def mixer_dots(s, pooled):
        mix = jnp.concatenate([_dot(p, pw_ref[gi]) for gi, p in enumerate(pooled)], axis=1) * ps_ref[...]
        return _post_mix(x_ref[s * POOL_SUB:(s + 1) * POOL_SUB, :], mix, mod_ref, npost_ref, npre_ref)

    res = jnp.concatenate(_subtile_pipeline(TM // POOL_SUB, mixer_pre, mixer_dots, mod_ref, nffn_ref,
                                            wg_ref, wu_ref, wd_ref, interleave=True), axis=0)

    @pl.when(i < N_PT)
    def _():
        op_ref[...] = res

    @pl.when(i >= N_PT)
    def _():
        os_ref[...] = res


def _pool_ffn(l, j_c, vec_table, vec_rows, xa, mod, pool_w, wg, wu, wd):
    hb = TM // HALO
    nh = N_TOK // HALO
    return pl.pallas_call(
        _pool_ffn_kernel,
        grid=(N_TILES,),
        in_specs=[_row_spec(D_MODEL),
                  pl.BlockSpec((HALO, D_MODEL), lambda i: (jnp.maximum(i * hb - 1, 0), 0)),
                  pl.BlockSpec((HALO, D_MODEL), lambda i: (jnp.minimum((i + 1) * hb, nh - 1), 0)),
                  _mod_spec(l),
                  _vec_spec(vec_rows["norm_pre_mix"] + l),
                  _const_spec((len(POOL_WINDOWS), POOL_GC, POOL_GC)),
                  _vec_spec(vec_rows["pool_scale"] + j_c)] + _ffn_specs(l, vec_rows),
        out_specs=_group_specs(D_MODEL),
        out_shape=[jax.ShapeDtypeStruct((N_PROMPT, D_MODEL), jnp.float32),
                   jax.ShapeDtypeStruct((N_SAMPLE, D_MODEL), jnp.float32)],
        compiler_params=pltpu.CompilerParams(dimension_semantics=("arbitrary",), vmem_limit_bytes=VMEM_LIMIT),
        name="pool_ffn",
    )(xa, xa, xa, mod, vec_table, pool_w, vec_table, vec_table, vec_table, vec_table, wg, wu, wd)


def _rope_tables():
    f32 = np.float32
    rows = DEC_SEQ // GRID_W
    r = np.repeat(np.arange(rows, dtype=f32), GRID_W)
    c = np.tile(np.arange(GRID_W, dtype=f32), rows)
    half = ROPE_DIM // 2
    inv_freq = np.power(f32(ROPE_THETA), -np.arange(0, half, 2, dtype=f32) / f32(half)).astype(f32)
    ang = np.concatenate([r[:, None] * inv_freq, c[:, None] * inv_freq], axis=-1).astype(f32)
    cos, sin = np.cos(ang).astype(f32), np.sin(ang).astype(f32)
    ones = np.ones((DEC_SEQ, KPE_LANE0), f32)
    zl = np.zeros((DEC_SEQ, KPE_LANE0), f32)
    zr = np.zeros((DEC_SEQ, LANES - KPE_LANE0 - ROPE_DIM), f32)
    cos_t = np.concatenate([ones, cos, cos, zr], axis=1)
    sin_t = np.concatenate([zl, -sin, sin, zr], axis=1)
    keep = np.concatenate([np.ones((TM, KPE_LANE0 + ROPE_DIM), f32), np.zeros((TM, LANES - KPE_LANE0 - ROPE_DIM), f32)],
                          axis=1)
    return (jnp.asarray(np.concatenate([cos_t, keep], axis=0)),
            jnp.asarray(np.concatenate([sin_t, np.zeros((TM, LANES), f32)], axis=0)))


def _kpe_slab(k):
    pad = [(0, 0)] * (k.ndim - 1) + [(KPE_LANE0, LANES - KPE_LANE0 - ROPE_DIM)]
    return jnp.pad(k, pad)


def _with_swapped_pair(w_pe):
    half = ROPE_DIM // 2
    return jnp.concatenate([w_pe, w_pe[..., half:], w_pe[..., :half]], axis=-1)


def _layout_in_proj(w):
    o = np.cumsum((0, Q_RANK, KV_RANK, ROPE_DIM, D_SSD, CONV_CH, SSD_HEADS, SSD_HEADS))
    cq, ckv, kpe, z, xbc, dtf, dtb = (w[:, o[k]:o[k + 1]] for k in range(7))
    dt = jnp.pad(jnp.concatenate([dtf, dtb], axis=1), ((0, 0), (0, LANES - 2 * SSD_HEADS)))
    kpe_slab = jnp.pad(_with_swapped_pair(kpe), ((0, 0), (KPE_LANE0, LANES - KPE_LANE0 - 2 * ROPE_DIM)))
    return jnp.concatenate([cq, ckv, z, xbc, kpe_slab, dt], axis=1).astype(jnp.bfloat16)


def _layout_uq(w):
    w = w.reshape(Q_RANK, MLA_HEADS, NOPE_DIM + ROPE_DIM)
    w = jnp.concatenate([w[:, :, :NOPE_DIM], _with_swapped_pair(w[:, :, NOPE_DIM:])], axis=-1)
    assert w.shape[-1] == HEAD_SLAB
    return w.reshape(Q_RANK, MLA_HEADS * HEAD_SLAB).astype(jnp.bfloat16)


def _layout_ukv(w):
    w = w.reshape(KV_RANK, MLA_HEADS, NOPE_DIM + V_DIM)
    kn = jnp.pad(w[:, :, :NOPE_DIM], ((0, 0), (0, 0), (0, HEAD_SLAB - NOPE_DIM)))
    v = w[:, :, NOPE_DIM:]
    return kn.reshape(KV_RANK, -1).astype(jnp.bfloat16), v.reshape(KV_RANK, -1).T.astype(jnp.bfloat16)


def kernel(x_prompt, x_sample, c, cache_mla_ckv, cache_mla_krope, state_ssd_fwd, state_ssd_bwd, c_ctx, w_mod, b_mod, norm_pre_mix, norm_post_mix, norm_pre_ffn, norm_post_ffn, w_in_ab, q_norm, w_uq, kv_norm, w_ukv, ssd_conv_w, ssd_conv_b, ssd_dt_bias_fwd, ssd_dt_bias_bwd, ssd_a_log_fwd, ssd_a_log_bwd, ssd_d, ssd_norm, w_out_ab, pool_w, pool_scale, ffn_w_gate, ffn_w_up, ffn_w_down):
    f32, bf16 = jnp.float32, jnp.bfloat16
    assert DEPTH == 2
    xp = x_prompt.reshape(N_PROMPT, D_MODEL)
    xs = x_sample.reshape(N_SAMPLE, D_MODEL)
    cvecs = jnp.concatenate([c_ctx[None, :], c, jnp.zeros((MOD_ROWS - N_MODVEC, D_MODEL), f32)], axis=0)
    mod = _modulation(cvecs, w_mod, b_mod)
    tabs = _rope_tables()
    hp = SSD_HEADS * SSD_HEAD_DIM
    row = lambda v: v.reshape(1, -1)
    new_ckv, new_kpe, new_hf, new_hb = [], [], [], []

    vec_table, vec_rows = _pack_vectors(
        norm_pre_mix=norm_pre_mix, norm_post_mix=norm_post_mix, norm_pre_ffn=norm_pre_ffn, norm_post_ffn=norm_post_ffn,
        q_norm=q_norm, kv_norm=kv_norm, ssd_norm=ssd_norm, ssd_conv_b=ssd_conv_b, pool_scale=pool_scale,
        ssd_dt_bias=jnp.concatenate([ssd_dt_bias_fwd, ssd_dt_bias_bwd], axis=1),
        ssd_a_log=jnp.concatenate([ssd_a_log_fwd, ssd_a_log_bwd], axis=1))
    ffn_w_f32 = (ffn_w_gate, ffn_w_up, ffn_w_down)
    for l in range(DEPTH):
        if l % 2 == 0:
            i = l // 2
            q, ckv_n, kpe, z, xbc, dts_t, cum_t, w_t, ckv_prompt, kpe_prompt = _inproj(
                l, vec_table, vec_rows, xp, xs, mod, tabs, _layout_in_proj(w_in_ab[i]), _layout_uq(w_uq[i]))
            heads = (dts_t, cum_t, w_t)
            w_kv = _layout_ukv(w_ukv[i])
            (att_p,) = _attention(q, ckv_n, kpe, w_kv, 0, BATCH, SEQ, n_pack=PROMPT_PACK)
            att_s, w_out, *ffn_w = _attention(q, ckv_n, kpe, w_kv, N_PROMPT, DEC_BATCH, DEC_SEQ,
                                              cache=(cache_mla_ckv[:, i], _kpe_slab(cache_mla_krope[:, i])),
                                              cast=[(w_out_ab, i)] + [(w, l) for w in ffn_w_f32])
            ssd_args = (ssd_conv_w, vec_table, vec_rows["ssd_conv_b"], row(jnp.repeat(ssd_d[i], SSD_HEAD_DIM)))
            y_p, hf, hb = _ssd(i, xbc, heads, None, *ssd_args, 0, BATCH, SEQ)
            y_s, _, _ = _ssd(i, xbc, heads, (state_ssd_fwd[:, i].reshape(DEC_BATCH, hp, SSD_STATE),
                                             state_ssd_bwd[:, i].reshape(DEC_BATCH, hp, SSD_STATE)),
                             *ssd_args, N_PROMPT, DEC_BATCH, DEC_SEQ)
            xa, *ffn_w = _outproj_ffn(l, i, vec_table, vec_rows, xp, xs, att_p, att_s, y_p, y_s, z, mod, w_out, *ffn_w,
                                      cast=[(w, l + 1) for w in ffn_w_f32])
            new_ckv.append(ckv_prompt.reshape(BATCH, SEQ, KV_RANK))
            new_kpe.append(jnp.swapaxes(kpe_prompt, 1, 2))
            new_hf.append(hf.reshape(BATCH, SSD_HEADS, SSD_HEAD_DIM, SSD_STATE))
            new_hb.append(hb.reshape(BATCH, SSD_HEADS, SSD_HEAD_DIM, SSD_STATE))
        else:
            j = l // 2
            yp, ys = _pool_ffn(l, j, vec_table, vec_rows, xa, mod, pool_w[j].astype(bf16), *ffn_w)

    return (yp.reshape(BATCH, SEQ, D_MODEL), ys.reshape(DEC_BATCH, DEC_SEQ, D_MODEL),
            jnp.stack(new_ckv, axis=1), jnp.stack(new_kpe, axis=1),
            jnp.stack(new_hf, axis=1), jnp.stack(new_hb, axis=1))
```

```python
import functools

import numpy as np
import jax
import jax.numpy as jnp
from jax import lax
from jax.experimental import pallas as pl
from jax.experimental.pallas import tpu as pltpu

D_MODEL = 1024
BATCH = 16
SEQ = 256
DEPTH = 2
DEC_BATCH = 2
DEC_SEQ = 2048
PAST_LEN = 256
GRID_W = 64
EPS = 1e-6
MLA_HEADS = 8
Q_RANK = 256
KV_RANK = 256
NOPE_DIM = 64
ROPE_DIM = 32
V_DIM = 64
ROPE_THETA = 10000.0
SSD_HEADS = 8
SSD_GROUPS = 2
SSD_HPG = SSD_HEADS // SSD_GROUPS
SSD_HEAD_DIM = 64
SSD_STATE = 128
D_SSD = SSD_HEADS * SSD_HEAD_DIM
CONV_W = 5
CONV_CH = D_SSD + 2 * SSD_GROUPS * SSD_STATE
POOL_WINDOWS = (2, 4, 8, 16)
POOL_GC = D_MODEL // len(POOL_WINDOWS)
D_FF = ((8 * D_MODEL + 3 * 256 - 1) // (3 * 256)) * 256

SUBLANES = 8
LANES = 128

N_PROMPT = BATCH * SEQ
N_SAMPLE = DEC_BATCH * DEC_SEQ
N_TOK = N_PROMPT + N_SAMPLE
N_MODVEC = 1 + DEC_BATCH
TM = 512
TQ = 256
CHUNK = 128
HALO = SUBLANES
HEAD_SLAB = LANES
IN_COLS = Q_RANK + KV_RANK + D_SSD + CONV_CH + 2 * LANES
KPE_LANE0 = NOPE_DIM
VMEM_LIMIT = 56 * 1024 * 1024

POOL_SUB = min(SEQ, DEC_SEQ)

assert TM % POOL_SUB == 0 and SEQ % POOL_SUB == 0 and DEC_SEQ % TM == 0 and N_PROMPT % DEC_SEQ == 0


def _rms(x, g):
    return x * lax.rsqrt(jnp.mean(x * x, axis=-1, keepdims=True) + EPS) * g


def _silu(x):
    return x * jax.nn.sigmoid(x)


def _softplus(x):
    return jnp.maximum(x, 0.0) + jnp.log1p(jnp.exp(-jnp.abs(x)))


def _dot(a, b):
    return jnp.dot(a, b, preferred_element_type=jnp.float32)


def _dot_nt(a, b):
    return lax.dot_general(a, b, (((1,), (1,)), ((), ())), preferred_element_type=jnp.float32)


def _mod_row(i):
    return jnp.where(i < N_PROMPT // TM, 0, 1 + (i - N_PROMPT // TM) // (DEC_SEQ // TM))


def _const_spec(shape):
    nd = len(shape)
    return pl.BlockSpec(shape, lambda *_: (0,) * nd, pipeline_mode=pl.Buffered(1))


N_PT = N_PROMPT // TM
N_TILES = N_TOK // TM


def _row_spec(width):
    return pl.BlockSpec((TM, width), lambda i: (i, 0))


def _mod_spec(l):
    return pl.BlockSpec((None, SUBLANES, 6 * D_MODEL), lambda i: (l, 0, 0), pipeline_mode=pl.Buffered(1))


def _mod_part(mod_ref, k):
    return mod_ref[pl.ds(_mod_row(pl.program_id(0)), 1), k * D_MODEL:(k + 1) * D_MODEL]


def _vec_spec(row):
    return pl.BlockSpec((None, 1, D_MODEL), lambda *_: (row, 0, 0), pipeline_mode=pl.Buffered(1))


def _pack_vectors(**params):
    first_row, blocks, n = {}, [], 0
    for name, a in params.items():
        first_row[name] = n
        n += a.shape[0]
        blocks.append(jnp.pad(a, ((0, 0), (0, D_MODEL - a.shape[1]))))
    return jnp.concatenate(blocks, axis=0).reshape(n, 1, D_MODEL), first_row


def _group_specs(width):
    return [pl.BlockSpec((TM, width), lambda i: (jnp.minimum(i, N_PT - 1), 0)),
            pl.BlockSpec((TM, width), lambda i: (jnp.maximum(i - N_PT, 0), 0))]


N_CAST = 16


def _cast_in_spec(w, layer, step=lambda i: i):
    _, rows, cols = w.shape
    return pl.BlockSpec((1, rows // N_CAST, cols), lambda *g: (layer, step(*g), 0))


def _cast_out(w, step=lambda i: i):
    _, rows, cols = w.shape
    return (pl.BlockSpec((rows // N_CAST, cols), lambda *g: (step(*g), 0)),
            jax.ShapeDtypeStruct((rows, cols), jnp.bfloat16))


def _cast_chunks(src_refs, dst_refs):
    for src, dst in zip(src_refs, dst_refs):
        dst[...] = src[0].astype(jnp.bfloat16)


def _pick_group(i, p_ref, s_ref, rows=slice(None)):
    return jnp.where(i < N_PT, p_ref[rows, :], s_ref[rows, :])


MOD_TK = 256
MOD_STREAMS = 2
PROMPT_PACK = 4
MOD_ROWS = 2 * SUBLANES
SUB_ROWS = 256


def _split3(a):
    a_hi = a.astype(jnp.bfloat16)
    r1 = a - a_hi.astype(jnp.float32)
    a_mid = r1.astype(jnp.bfloat16)
    a_lo = (r1 - a_mid.astype(jnp.float32)).astype(jnp.bfloat16)
    return a_hi, a_mid, a_lo


def _mod_kernel(c_ref, *refs):
    w_refs, (b_ref, win_ref, o_ref, wl_ref) = refs[:MOD_STREAMS], refs[MOD_STREAMS:]
    wl_ref[...] = _layout_in_proj(win_ref[0])
    k = pl.program_id(1)
    rows = MOD_TK // MOD_STREAMS
    s_all = _silu(c_ref[...])
    part = jnp.zeros((MOD_ROWS, 6 * D_MODEL), jnp.float32)
    for j, w_ref in enumerate(w_refs):
        s_hi, s_mid, s_lo = _split3(s_all[:, j * rows:(j + 1) * rows])
        w = w_ref[0]
        w_hi = w.astype(jnp.bfloat16)
        w_lo = (w - w_hi.astype(jnp.float32)).astype(jnp.bfloat16)
        top = _dot(jnp.concatenate([s_hi, s_mid, s_lo], axis=0), w_hi)
        low = _dot(jnp.concatenate([s_hi, s_mid], axis=0), w_lo)
        part = part + (top[:MOD_ROWS] + top[MOD_ROWS:2 * MOD_ROWS] + top[2 * MOD_ROWS:]
                       + low[:MOD_ROWS] + low[MOD_ROWS:])
    part = part[:SUBLANES]

    @pl.when(k == 0)
    def _():
        o_ref[0] = part + b_ref[pl.ds(pl.program_id(0), 1), :]

    @pl.when(k > 0)
    def _():
        o_ref[0] += part


def _modulation(cvecs, w_mod, b_mod, w_in_t):
    n = 6 * D_MODEL
    nk = D_MODEL // MOD_TK
    chunk = D_MODEL // (DEPTH * nk)
    assert w_in_t.shape[0] == 1
    w_spec = lambda j: pl.BlockSpec((1, MOD_TK // MOD_STREAMS, n), lambda l, k: (l, k * MOD_STREAMS + j, 0))
    return pl.pallas_call(
        _mod_kernel,
        grid=(DEPTH, D_MODEL // MOD_TK),
        in_specs=[pl.BlockSpec((MOD_ROWS, MOD_TK), lambda l, k: (0, k))]
        + [w_spec(j) for j in range(MOD_STREAMS)]
        + [pl.BlockSpec((DEPTH, n), lambda l, k: (0, 0)),
           pl.BlockSpec((1, w_in_t.shape[1], chunk), lambda l, k: (0, 0, l * nk + k))],
        out_specs=[pl.BlockSpec((1, SUBLANES, n), lambda l, k: (l, 0, 0)),
                   pl.BlockSpec((chunk, IN_COLS), lambda l, k: (l * nk + k, 0))],
        out_shape=[jax.ShapeDtypeStruct((DEPTH, SUBLANES, n), jnp.float32),
                   jax.ShapeDtypeStruct((D_MODEL, IN_COLS), jnp.bfloat16)],
        compiler_params=pltpu.CompilerParams(dimension_semantics=("arbitrary", "arbitrary"),
                                             vmem_limit_bytes=VMEM_LIMIT),
        name="modulation",
    )(cvecs, *([w_mod] * MOD_STREAMS), b_mod, w_in_t)


def _ssd_head_terms(dt_raw, dtb_c, a_neg_c):
    nh2 = 2 * SSD_HEADS
    row = lax.broadcasted_iota(jnp.int32, (CHUNK, CHUNK), 0)
    col = lax.broadcasted_iota(jnp.int32, (CHUNK, CHUNK), 1)
    upper_b = (row <= col).astype(jnp.bfloat16)
    lower_b = (row >= col).astype(jnp.bfloat16)
    fwd_rows = lax.broadcasted_iota(jnp.int32, (nh2, CHUNK), 0) < SSD_HEADS
    dts_t = _softplus(dt_raw.T[:nh2, :] + dtb_c)
    pieces = _split3(dts_t * a_neg_c)
    cum_t = jnp.where(fwd_rows, sum(_dot(p, upper_b) for p in pieces), sum(_dot(p, lower_b) for p in pieces))
    cum_t = cum_t * np.float32(np.log2(np.e))
    tot = jnp.where(fwd_rows[:, :1], cum_t[:, CHUNK - 1:], cum_t[:, :1])
    return dts_t, cum_t, dts_t * jnp.exp2(tot - cum_t)


def _inproj_kernel(xp_ref, xs_ref, mod_ref, cos_ref, sin_ref, w_in_ref, w_uq_ref, qn_ref, kvn_ref, npm_ref,
                   dtb_ref, alog_ref, q_ref, ckv_ref, kpe_ref, z_ref, xbc_ref, dts_ref, cumt_ref, wt_ref,
                   new_ckv_ref, new_kpe_ref):
    i = pl.program_id(0)
    sh = _mod_part(mod_ref, 0)
    sc = _mod_part(mod_ref, 1)
    scale = (NOPE_DIM + ROPE_DIM) ** -0.5 * np.log2(np.e)
    nh2 = 2 * SSD_HEADS
    diag = (lax.broadcasted_iota(jnp.int32, (nh2, LANES), 0) == lax.broadcasted_iota(jnp.int32, (nh2, LANES), 1))
    to_col = lambda ref: jnp.sum(jnp.where(diag, ref[:, :LANES], 0.0), axis=1, keepdims=True)
    dtb_c = to_col(dtb_ref)
    a_neg_c = -jnp.exp(to_col(alog_ref))
    dt_raw = []
    for r in range(TM // SUB_ROWS):
        rs = slice(r * SUB_ROWS, (r + 1) * SUB_ROWS)
        h = (_rms(_pick_group(i, xp_ref, xs_ref, rs), npm_ref[...]) * (1.0 + sc) + sh).astype(jnp.bfloat16)
        p = _dot(h, w_in_ref[...])
        o = 0
        cq = p[:, o:o + Q_RANK]; o += Q_RANK
        ckv = p[:, o:o + KV_RANK]; o += KV_RANK
        z_ref[rs, :] = p[:, o:o + D_SSD]; o += D_SSD
        xbc_ref[rs, :] = p[:, o:o + CONV_CH]; o += CONV_CH
        kpe = p[:, o:o + LANES]; o += LANES
        dt_raw.append(p[:, o:o + LANES])

        cos = cos_ref[rs, :]
        sin = sin_ref[rs, :]

        def rope(slab):
            return slab * cos + pltpu.roll(slab, LANES - ROPE_DIM, 1) * sin

        ckv_ref[rs, :] = _rms(ckv, kvn_ref[:, :KV_RANK])
        kpe_ref[rs, :] = rope(kpe)
        q = _dot(_rms(cq, qn_ref[:, :Q_RANK]).astype(jnp.bfloat16), w_uq_ref[...]) * scale
        for hd in range(MLA_HEADS):
            sl = slice(hd * HEAD_SLAB, (hd + 1) * HEAD_SLAB)
            q_ref[rs, sl] = rope(q[:, sl]).astype(jnp.bfloat16)

    dt_all = jnp.concatenate(dt_raw, axis=0)
    for ck in range(TM // CHUNK):
        terms = _ssd_head_terms(dt_all[ck * CHUNK:(ck + 1) * CHUNK, :], dtb_c, a_neg_c)
        for ref, val in zip((dts_ref, cumt_ref, wt_ref), terms):
            ref[ck * nh2:(ck + 1) * nh2, :] = val

    @pl.when(i < N_PT)
    def _():
        new_ckv_ref[...] = ckv_ref[...]
        kpe_t = kpe_ref[...].T
        for b in range(TM // SEQ):
            new_kpe_ref[b] = kpe_t[KPE_LANE0:KPE_LANE0 + ROPE_DIM, b * SEQ:(b + 1) * SEQ]


def _inproj(l, vec_table, vec_rows, xp, xs, mod, tabs, w_in, w_uq):
    nt = N_TILES
    prompt_blk = lambda i: (jnp.minimum(i, N_PT - 1), 0)
    lat_tiles = DEC_SEQ // TM
    tab_spec = pl.BlockSpec((TM, LANES), lambda i: (jnp.where(i < N_PT, lat_tiles, (i - N_PT) % lat_tiles), 0))
    row = _row_spec
    hrows = TM // CHUNK * 2 * SSD_HEADS
    head_spec = pl.BlockSpec((hrows, CHUNK), lambda i: (i, 0))
    head_shape = jax.ShapeDtypeStruct((nt * hrows, CHUNK), jnp.float32)
    return pl.pallas_call(
        _inproj_kernel,
        grid=(nt,),
        in_specs=_group_specs(D_MODEL) + [
            _mod_spec(l),
            tab_spec, tab_spec,
            _const_spec((D_MODEL, IN_COLS)),
            _const_spec((Q_RANK, MLA_HEADS * HEAD_SLAB)),
            _vec_spec(vec_rows["q_norm"] + l // 2),
            _vec_spec(vec_rows["kv_norm"] + l // 2),
            _vec_spec(vec_rows["norm_pre_mix"] + l),
            _vec_spec(vec_rows["ssd_dt_bias"] + l // 2),
            _vec_spec(vec_rows["ssd_a_log"] + l // 2),
        ],
        out_specs=[row(MLA_HEADS * HEAD_SLAB), row(KV_RANK), row(LANES), row(D_SSD), row(CONV_CH),
                   head_spec, head_spec, head_spec,
                   pl.BlockSpec((TM, KV_RANK), prompt_blk),
                   pl.BlockSpec((TM // SEQ, ROPE_DIM, SEQ), lambda i: (jnp.minimum(i, N_PT - 1), 0, 0))],
        out_shape=[
            jax.ShapeDtypeStruct((N_TOK, MLA_HEADS * HEAD_SLAB), jnp.bfloat16),
            jax.ShapeDtypeStruct((N_TOK, KV_RANK), jnp.float32),
            jax.ShapeDtypeStruct((N_TOK, LANES), jnp.float32),
            jax.ShapeDtypeStruct((N_TOK, D_SSD), jnp.float32),
            jax.ShapeDtypeStruct((N_TOK, CONV_CH), jnp.float32),
            head_shape, head_shape, head_shape,
            jax.ShapeDtypeStruct((N_PROMPT, KV_RANK), jnp.float32),
            jax.ShapeDtypeStruct((BATCH, ROPE_DIM, SEQ), jnp.float32),
        ],
        compiler_params=pltpu.CompilerParams(dimension_semantics=("arbitrary",), vmem_limit_bytes=VMEM_LIMIT),
        name="inproj",
    )(xp, xs, mod, *tabs, w_in, w_uq, *([vec_table] * 5))


def _attn_kernel(*refs, lk_cache, lk_new, n_pack, n_cast):
    n_in = 7 if lk_cache else 5
    if lk_cache:
        q_ref, ckv_ref, kpe_ref, ckvc_ref, kpec_ref, wk_ref, wvt_ref = refs[:n_in]
    else:
        q_ref, ckv_ref, kpe_ref, wk_ref, wvt_ref = refs[:n_in]
    cast_src, o_ref = refs[n_in:n_in + n_cast], refs[n_in + n_cast]
    cast_dst = refs[n_in + n_cast + 1:n_in + 2 * n_cast + 1]
    k_scr, vt_scr = refs[n_in + 2 * n_cast + 1:]
    _cast_chunks(cast_src, cast_dst)
    lk = lk_cache + lk_new

    @pl.when(pl.program_id(1) == 0)
    def _expand_kv():
        def expand(ckv, kpe, r0):
            ckv_b = ckv.astype(jnp.bfloat16)
            kn = _dot(ckv_b, wk_ref[...])
            rows = slice(r0, r0 + ckv.shape[0])
            for hd in range(MLA_HEADS):
                k_scr[hd, rows, :] = (kn[:, hd * HEAD_SLAB:(hd + 1) * HEAD_SLAB] + kpe).astype(jnp.bfloat16)
            vt_scr[:, rows] = _dot_nt(wvt_ref[...], ckv_b).astype(jnp.bfloat16)

        step = 256
        for r0 in range(0, lk_cache, step):
            expand(ckvc_ref[0, r0:r0 + step, :], kpec_ref[0, r0:r0 + step, :], r0)
        for r0 in range(0, n_pack * lk_new, step):
            expand(ckv_ref[r0:r0 + step, :], kpe_ref[r0:r0 + step, :], lk_cache + r0)

    work = [(s, hd) for s in range(n_pack) for hd in range(MLA_HEADS)]
    scores = [_dot_nt(k_scr[hd, s * lk:(s + 1) * lk, :], q_ref[s * TQ:(s + 1) * TQ, hd * HEAD_SLAB:(hd + 1) * HEAD_SLAB])
              for s, hd in work]
    outs = []
    for (s, hd), s_t in zip(work, scores):
        p_t = jnp.exp2(s_t - jnp.max(s_t, axis=0, keepdims=True))
        den = jnp.sum(p_t, axis=0, keepdims=True)
        v_t = vt_scr[hd * V_DIM:(hd + 1) * V_DIM, s * lk:(s + 1) * lk]
        outs.append(_dot(v_t, p_t.astype(jnp.bfloat16)) / den)
    for s in range(n_pack):
        o_ref[s * TQ:(s + 1) * TQ, :] = jnp.concatenate(
            outs[s * MLA_HEADS:(s + 1) * MLA_HEADS], axis=0).T.astype(jnp.bfloat16)


def _attention(q, ckv_n, kpe, w_ukv, row_off, n_batch, seq, cache=None, n_pack=1, cast=()):
    nq = seq // TQ
    assert n_pack == 1 or (nq == 1 and cache is None and n_batch % n_pack == 0)
    n_batch //= n_pack
    lk_cache = 0 if cache is None else cache[0].shape[1]
    lk = n_pack * (lk_cache + seq)
    qblk = lambda b, qi: (row_off // (n_pack * TQ) + b * nq + qi, 0)
    sblk = lambda b, qi: (row_off // (n_pack * seq) + b, 0)
    in_specs = [
        pl.BlockSpec((n_pack * TQ, MLA_HEADS * HEAD_SLAB), qblk),
        pl.BlockSpec((n_pack * seq, KV_RANK), sblk),
        pl.BlockSpec((n_pack * seq, LANES), sblk),
    ]
    args = [q, ckv_n, kpe]
    if cache is not None:
        in_specs += [pl.BlockSpec((1, lk_cache, KV_RANK), lambda b, qi: (b, 0, 0)),
                     pl.BlockSpec((1, lk_cache, LANES), lambda b, qi: (b, 0, 0))]
        args += list(cache)
    in_specs += [_const_spec(w.shape) for w in w_ukv]
    args += list(w_ukv)
    assert not cast or n_batch * nq == N_CAST
    step = lambda b, qi: b * nq + qi
    in_specs += [_cast_in_spec(w, layer, step) for w, layer in cast]
    args += [w for w, _ in cast]
    cast_out = [_cast_out(w, step) for w, _ in cast]
    return pl.pallas_call(
        functools.partial(_attn_kernel, lk_cache=lk_cache, lk_new=seq, n_pack=n_pack, n_cast=len(cast)),
        grid=(n_batch, nq),
        in_specs=in_specs,
        out_specs=[pl.BlockSpec((n_pack * TQ, MLA_HEADS * V_DIM), lambda b, qi: (b * nq + qi, 0))]
        + [spec for spec, _ in cast_out],
        out_shape=[jax.ShapeDtypeStruct((n_batch * n_pack * seq, MLA_HEADS * V_DIM), jnp.bfloat16)]
        + [shape for _, shape in cast_out],
        scratch_shapes=[pltpu.VMEM((MLA_HEADS, lk, HEAD_SLAB), jnp.bfloat16),
                        pltpu.VMEM((MLA_HEADS * V_DIM, lk), jnp.bfloat16)],
        compiler_params=pltpu.CompilerParams(dimension_semantics=("arbitrary", "arbitrary"),
                                             vmem_limit_bytes=VMEM_LIMIT),
        name=f"attention_{seq}",
    )(*args)


def _ssd_kernel(*refs, seq, zero_init):
    if zero_init:
        (xbc_ref, dts_ref, cumt_ref, wt_ref, cw_ref, cb_ref, dsk_ref,
         y_ref, hf_ref, hb_ref, xs_scr, c_scr, bt_scr, cum_scr, stf_scr, stb_scr) = refs
    else:
        (xbc_ref, dts_ref, cumt_ref, wt_ref, h0f_ref, h0b_ref, cw_ref, cb_ref, dsk_ref,
         y_ref, hf_ref, hb_ref, xs_scr, c_scr, bt_scr, cum_scr, stf_scr, stb_scr) = refs
    nc = seq // CHUNK
    gs = SSD_GROUPS * SSD_STATE
    nh2 = 2 * SSD_HEADS

    row = lax.broadcasted_iota(jnp.int32, (CHUNK, CHUNK), 0)
    col = lax.broadcasted_iota(jnp.int32, (CHUNK, CHUNK), 1)
    low_half = col < SSD_HEAD_DIM
    lower = row >= col
    upper = row <= col

    def prep_chunk(c, carry):
        r0 = pl.multiple_of(c * CHUNK, CHUNK)
        rows = pl.ds(r0, CHUNK)
        rows_prev = pl.ds(pl.multiple_of(jnp.maximum(r0 - HALO, 0), HALO), HALO)
        rows_next = pl.ds(pl.multiple_of(jnp.minimum(r0 + CHUNK, seq - HALO), HALO), HALO)

        def conv_tile(cs):
            prev = jnp.where(c > 0, xbc_ref[rows_prev, cs], 0.0)
            nxt = jnp.where(c < nc - 1, xbc_ref[rows_next, cs], 0.0)
            win = jnp.concatenate([prev, xbc_ref[rows, cs], nxt], axis=0)
            acc = jnp.broadcast_to(cb_ref[:, cs], (CHUNK, LANES))
            for k in range(CONV_W):
                lo = HALO - CONV_W // 2 + k
                acc = acc + cw_ref[k:k + 1, cs] * win[lo:lo + CHUNK, :]
            return _silu(acc)

        def x_tile(j, carry):
            cs = pl.ds(pl.multiple_of(j * LANES, LANES), LANES)
            u = conv_tile(cs)
            y_ref[rows, cs] = dsk_ref[:, cs] * u
            xs_scr[0, rows, cs] = jnp.where(low_half, u, 0.0).astype(jnp.bfloat16)
            xs_scr[1, rows, cs] = jnp.where(low_half, 0.0, u).astype(jnp.bfloat16)
            return carry

        lax.fori_loop(0, D_SSD // LANES, x_tile, 0)
        cum_t = cumt_ref[pl.ds(pl.multiple_of(c * nh2, nh2), nh2), :]
        cum_scr[rows, :] = jnp.concatenate(
            [cum_t, jnp.zeros((CHUNK - nh2, CHUNK), jnp.float32)], axis=0).T[:, :nh2]
        for g in range(SSD_GROUPS):
            b0 = pl.multiple_of(c * gs + g * SSD_STATE, SSD_STATE)
            bt_scr[pl.ds(b0, SSD_STATE), :] = conv_tile(slice(D_SSD + g * SSD_STATE, D_SSD + (g + 1) * SSD_STATE)).T
            c_scr[rows, g * SSD_STATE:(g + 1) * SSD_STATE] = conv_tile(
                slice(D_SSD + gs + g * SSD_STATE, D_SSD + gs + (g + 1) * SSD_STATE)).astype(jnp.bfloat16)
        return carry

    lax.fori_loop(0, nc, prep_chunk, 0)

    if zero_init:
        stf_scr[...] = jnp.zeros_like(stf_scr)
        stb_scr[...] = jnp.zeros_like(stb_scr)
    else:
        stf_scr[...] = h0f_ref[0].T
        stb_scr[...] = h0b_ref[0].T

    def scan_open(ci, st_scr):
        rows = pl.ds(pl.multiple_of(ci * CHUNK, CHUNK), CHUNK)
        c_b = c_scr[rows, :]
        st = st_scr[...]
        bts, cbms, zs = [], [], []
        for g in range(SSD_GROUPS):
            cg = c_b[:, g * SSD_STATE:(g + 1) * SSD_STATE]
            bt = bt_scr[pl.ds(pl.multiple_of(ci * gs + g * SSD_STATE, SSD_STATE), SSD_STATE), :]
            gcols = slice(g * SSD_HPG * SSD_HEAD_DIM, (g + 1) * SSD_HPG * SSD_HEAD_DIM)
            bts.append(bt)
            cbms.append(_dot(cg, bt.astype(jnp.bfloat16)))
            zs.append(_dot(cg, st[:, gcols].astype(jnp.bfloat16)))
        return st, bts, cbms, zs

    def scan_pairs(ci, st_scr, reverse, opened):
        st, bts, cbms, zs = opened
        lane0 = SSD_HEADS if reverse else 0
        causal = upper if reverse else lower
        last = 0 if reverse else CHUNK - 1
        rows = pl.ds(pl.multiple_of(ci * CHUNK, CHUNK), CHUNK)
        hrows = pl.ds(pl.multiple_of(ci * nh2, nh2), nh2)
        xs_lo = xs_scr[0, rows, :]
        xs_hi = xs_scr[1, rows, :]
        dts_t = dts_ref[hrows, :]
        cum_t = cumt_ref[hrows, :]
        w_t = wt_ref[hrows, :]
        cum = cum_scr[rows, :]
        for pair in range(SSD_HEADS // 2):
            g, jj = divmod(pair, SSD_HPG // 2)
            pcols = slice(pair * LANES, (pair + 1) * LANES)
            lhs_y, lhs_s, entry = [], [], []
            for hd in (2 * pair, 2 * pair + 1):
                ln = lane0 + hd
                cum_i = jnp.broadcast_to(cum[:, ln:ln + 1], (CHUNK, CHUNK))
                dec = jnp.exp2(jnp.where(causal, cum_i - cum_t[ln:ln + 1, :], -jnp.inf))
                lhs_y.append((cbms[g] * dec * dts_t[ln:ln + 1, :]).astype(jnp.bfloat16))
                lhs_s.append((bts[g] * w_t[ln:ln + 1, :]).astype(jnp.bfloat16))
                entry.append(jnp.exp2(cum_i))
            lhs = jnp.concatenate([jnp.concatenate(lhs_y, axis=1), jnp.concatenate(lhs_s, axis=1)], axis=0)
            out = _dot(lhs, jnp.concatenate([xs_lo[:, pcols], xs_hi[:, pcols]], axis=0))
            ea = jnp.where(low_half, entry[0], entry[1])
            y_ref[rows, pcols] += out[:CHUNK] + zs[g][:, jj * LANES:(jj + 1) * LANES] * ea
            st_scr[:, pcols] = ea[last:last + 1, :] * st[:, pcols] + out[CHUNK:]

    def both(c, carry):
        opened_f = scan_open(c, stf_scr)
        opened_b = scan_open(nc - 1 - c, stb_scr)
        scan_pairs(c, stf_scr, False, opened_f)
        scan_pairs(nc - 1 - c, stb_scr, True, opened_b)
        return carry

    lax.fori_loop(0, nc, both, 0)
    hf_ref[0] = stf_scr[...].T
    hb_ref[0] = stb_scr[...].T


def _ssd(i_ab, xbc, head_terms, h0, conv_w, vec_table, conv_b_row, d_skip, row_off, n_batch, seq):
    assert CONV_CH == D_MODEL
    hp = SSD_HEADS * SSD_HEAD_DIM
    gs = SSD_GROUPS * SSD_STATE
    nc = seq // CHUNK
    seq_blk = lambda b: (row_off // seq + b, 0)
    head_spec = pl.BlockSpec((nc * 2 * SSD_HEADS, CHUNK), seq_blk)
    st_spec = pl.BlockSpec((1, hp, SSD_STATE), lambda b: (b, 0, 0))
    st_shape = jax.ShapeDtypeStruct((n_batch, hp, SSD_STATE), jnp.float32)
    h0 = () if h0 is None else tuple(h0)
    return pl.pallas_call(
        functools.partial(_ssd_kernel, seq=seq, zero_init=not h0),
        grid=(n_batch,),
        in_specs=[pl.BlockSpec((seq, CONV_CH), seq_blk), head_spec, head_spec, head_spec] + [st_spec] * len(h0) + [
            pl.BlockSpec((None, CONV_W, CONV_CH), lambda b: (i_ab, 0, 0), pipeline_mode=pl.Buffered(1)),
            _vec_spec(conv_b_row + i_ab), _const_spec((1, D_SSD))],
        out_specs=[pl.BlockSpec((seq, D_SSD), lambda b: (b, 0)), st_spec, st_spec],
        out_shape=[jax.ShapeDtypeStruct((n_batch * seq, D_SSD), jnp.float32), st_shape, st_shape],
        scratch_shapes=[pltpu.VMEM((2, seq, D_SSD), jnp.bfloat16),
                        pltpu.VMEM((seq, gs), jnp.bfloat16),
                        pltpu.VMEM((nc * gs, CHUNK), jnp.float32),
                        pltpu.VMEM((seq, 2 * SSD_HEADS), jnp.float32),
                        pltpu.VMEM((SSD_STATE, hp), jnp.float32),
                        pltpu.VMEM((SSD_STATE, hp), jnp.float32)],
        compiler_params=pltpu.CompilerParams(dimension_semantics=("arbitrary",), vmem_limit_bytes=VMEM_LIMIT),
        name=f"ssd_{seq}",
    )(xbc, *head_terms, *h0, conv_w, vec_table, d_skip)


def _post_mix(x, mix, mod_ref, npost_ref, npre_ref):
    d = D_MODEL
    gate_mix = _mod_part(mod_ref, 2)
    shf = _mod_part(mod_ref, 3)
    scf = _mod_part(mod_ref, 4)
    x1 = x + gate_mix * _rms(mix, npost_ref[...])
    return x1, (_rms(x1, npre_ref[...]) * (1.0 + scf) + shf).astype(jnp.bfloat16)


def _subtile_pipeline(n_sub, mixer_pre, mixer_dots, mod_ref, nffn_ref, wg_ref, wu_ref, wd_ref, interleave):
    gate_ffn = _mod_part(mod_ref, 5)
    ffn_up = lambda h: (_silu(_dot(h, wg_ref[...])) * _dot(h, wu_ref[...])).astype(jnp.bfloat16)
    ffn_down = lambda x1, hid: x1 + gate_ffn * _rms(_dot(hid, wd_ref[...]), nffn_ref[...])
    if not interleave:
        staged = [mixer_dots(r, mixer_pre(r)) for r in range(n_sub)]
        return [ffn_down(x1, ffn_up(h)) for x1, h in staged]
    outs = []
    x1, h = mixer_dots(0, mixer_pre(0))
    for r in range(n_sub):
        nxt_pre = mixer_pre(r + 1) if r + 1 < n_sub else None
        hid = ffn_up(h)
        nxt = mixer_dots(r + 1, nxt_pre) if r + 1 < n_sub else None
        outs.append(ffn_down(x1, hid))
        if nxt is not None:
            x1, h = nxt
    return outs


def _ffn_specs(l, vec_rows):
    return [_vec_spec(vec_rows[name] + l) for name in ("norm_post_mix", "norm_pre_ffn", "norm_post_ffn")] + [
        _const_spec((D_MODEL, D_FF)), _const_spec((D_MODEL, D_FF)), _const_spec((D_FF, D_MODEL))]


def _outproj_ffn_kernel(*refs, n_cast):
    (xp_ref, xs_ref, attp_ref, atts_ref, yp_ref, ys_ref, z_ref, mod_ref, sn_ref, wo_ref,
     npost_ref, npre_ref, nffn_ref, wg_ref, wu_ref, wd_ref) = refs[:16]
    cast_src, o_ref, cast_dst = refs[16:16 + n_cast], refs[16 + n_cast], refs[17 + n_cast:]
    i = pl.program_id(0)
    _cast_chunks(cast_src, cast_dst)
    gw = D_SSD // SSD_GROUPS
    sub = lambda r: slice(r * SUB_ROWS, (r + 1) * SUB_ROWS)

    def mixer_pre(r):
        yg = _pick_group(i, yp_ref, ys_ref, sub(r)) * _silu(z_ref[sub(r), :])
        parts = [_pick_group(i, attp_ref, atts_ref, sub(r))]
        for g in range(SSD_GROUPS):
            parts.append(_rms(yg[:, g * gw:(g + 1) * gw], sn_ref[:, g * gw:(g + 1) * gw]).astype(jnp.bfloat16))
        return jnp.concatenate(parts, axis=1)

    def mixer_dots(r, cat):
        return _post_mix(_pick_group(i, xp_ref, xs_ref, sub(r)), _dot(cat, wo_ref[...]),
                         mod_ref, npost_ref, npre_ref)

    outs = _subtile_pipeline(TM // SUB_ROWS, mixer_pre, mixer_dots, mod_ref, nffn_ref, wg_ref, wu_ref, wd_ref,
                             interleave=False)
    for r, res in enumerate(outs):
        o_ref[sub(r), :] = res


def _outproj_ffn(l, i_ab, vec_table, vec_rows, xp, xs, att_p, att_s, y_p, y_s, z, mod, w_out, wg, wu, wd, cast):
    d_cat = MLA_HEADS * V_DIM + D_SSD
    assert not cast or N_TILES == N_CAST
    cast_out = [_cast_out(w) for w, _ in cast]
    return pl.pallas_call(
        functools.partial(_outproj_ffn_kernel, n_cast=len(cast)),
        grid=(N_TILES,),
        in_specs=(_group_specs(D_MODEL) + _group_specs(MLA_HEADS * V_DIM) + _group_specs(D_SSD)
                  + [_row_spec(D_SSD), _mod_spec(l), _vec_spec(vec_rows["ssd_norm"] + i_ab),
                     _const_spec((d_cat, D_MODEL))] + _ffn_specs(l, vec_rows)
                  + [_cast_in_spec(w, layer) for w, layer in cast]),
        out_specs=[_row_spec(D_MODEL)] + [spec for spec, _ in cast_out],
        out_shape=[jax.ShapeDtypeStruct((N_TOK, D_MODEL), jnp.float32)] + [shape for _, shape in cast_out],
        compiler_params=pltpu.CompilerParams(dimension_semantics=("arbitrary",), vmem_limit_bytes=VMEM_LIMIT),
        name="outproj_ffn",
    )(xp, xs, att_p, att_s, y_p, y_s, z, mod, vec_table, w_out, vec_table, vec_table, vec_table, wg, wu, wd,
      *[w for w, _ in cast])


def _pool_ffn_kernel(x_ref, xp_ref, xn_ref, mod_ref, nmix_ref, pw_ref, ps_ref,
                     npost_ref, npre_ref, nffn_ref, wg_ref, wu_ref, wd_ref, op_ref, os_ref):
    i = pl.program_id(0)

    @pl.when(i == 0)
    def _():
        os_ref[...] = jnp.zeros_like(os_ref)

    seq = jnp.where(i < N_PROMPT // TM, SEQ, DEC_SEQ)
    pos0 = (i * TM) % seq
    sh = _mod_part(mod_ref, 0)
    sc = _mod_part(mod_ref, 1)
    hmod = lambda v: _rms(v, nmix_ref[...]) * (1.0 + sc) + sh
    n_rows = POOL_SUB + 2 * HALO

    def shifted(v, s):
        return pltpu.roll(v, n_rows - s, 0)

    def mixer_pre(s):
        lo, hi = s * POOL_SUB, (s + 1) * POOL_SUB
        pos_s = (pos0 + lo) % seq
        h = hmod(x_ref[lo:hi, :])
        before = hmod(xp_ref[...] if s == 0 else x_ref[lo - HALO:lo, :])
        after = hmod(xn_ref[...] if hi == TM else x_ref[hi:hi + HALO, :])
        before = jnp.where(pos_s > 0, before, 0.0)
        after = jnp.where(pos_s + POOL_SUB < seq, after, 0.0)
        padded = jnp.concatenate([before, h, after], axis=0)
        pos = pos_s + lax.broadcasted_iota(jnp.int32, (POOL_SUB, 1), 0)
        pooled = []
        for gi, w in enumerate(POOL_WINDOWS):
            cols = slice(gi * POOL_GC, (gi + 1) * POOL_GC)
            t = padded[:, cols]
            span = 1
            while span < w:
                t = t + shifted(t, span)
                span *= 2
            lead = HALO - w // 2
            win_sum = (shifted(t, lead) if lead else t)[:POOL_SUB, :]
            cnt = (jnp.minimum(pos + w // 2, seq) - jnp.maximum(pos - w // 2, 0)).astype(jnp.float32)
            pooled.append((win_sum / cnt - h[:, cols]).astype(jnp.bfloat16))
        return pooled

    def mixer_dots(s, pooled):
        mix = jnp.concatenate([_dot(p, pw_ref[gi]) for gi, p in enumerate(pooled)], axis=1) * ps_ref[...]
        return _post_mix(x_ref[s * POOL_SUB:(s + 1) * POOL_SUB, :], mix, mod_ref, npost_ref, npre_ref)

    res = jnp.concatenate(_subtile_pipeline(TM // POOL_SUB, mixer_pre, mixer_dots, mod_ref, nffn_ref,
                                            wg_ref, wu_ref, wd_ref, interleave=True), axis=0)

    @pl.when(i < N_PT)
    def _():
        op_ref[...] = res

    @pl.when(i >= N_PT)
    def _():
        os_ref[...] = res


def _pool_ffn(l, j_c, vec_table, vec_rows, xa, mod, pool_w, wg, wu, wd):
    hb = TM // HALO
    nh = N_TOK // HALO
    return pl.pallas_call(
        _pool_ffn_kernel,
        grid=(N_TILES,),
        in_specs=[_row_spec(D_MODEL),
                  pl.BlockSpec((HALO, D_MODEL), lambda i: (jnp.maximum(i * hb - 1, 0), 0)),
                  pl.BlockSpec((HALO, D_MODEL), lambda i: (jnp.minimum((i + 1) * hb, nh - 1), 0)),
                  _mod_spec(l),
                  _vec_spec(vec_rows["norm_pre_mix"] + l),
                  _const_spec((len(POOL_WINDOWS), POOL_GC, POOL_GC)),
                  _vec_spec(vec_rows["pool_scale"] + j_c)] + _ffn_specs(l, vec_rows),
        out_specs=_group_specs(D_MODEL),
        out_shape=[jax.ShapeDtypeStruct((N_PROMPT, D_MODEL), jnp.float32),
                   jax.ShapeDtypeStruct((N_SAMPLE, D_MODEL), jnp.float32)],
        compiler_params=pltpu.CompilerParams(dimension_semantics=("arbitrary",), vmem_limit_bytes=VMEM_LIMIT),
        name="pool_ffn",
    )(xa, xa, xa, mod, vec_table, pool_w, vec_table, vec_table, vec_table, vec_table, wg, wu, wd)


def _rope_tables():
    f32 = np.float32
    rows = DEC_SEQ // GRID_W
    r = np.repeat(np.arange(rows, dtype=f32), GRID_W)
    c = np.tile(np.arange(GRID_W, dtype=f32), rows)
    half = ROPE_DIM // 2
    inv_freq = np.power(f32(ROPE_THETA), -np.arange(0, half, 2, dtype=f32) / f32(half)).astype(f32)
    ang = np.concatenate([r[:, None] * inv_freq, c[:, None] * inv_freq], axis=-1).astype(f32)
    cos, sin = np.cos(ang).astype(f32), np.sin(ang).astype(f32)
    ones = np.ones((DEC_SEQ, KPE_LANE0), f32)
    zl = np.zeros((DEC_SEQ, KPE_LANE0), f32)
    zr = np.zeros((DEC_SEQ, LANES - KPE_LANE0 - ROPE_DIM), f32)
    cos_t = np.concatenate([ones, cos, cos, zr], axis=1)
    sin_t = np.concatenate([zl, -sin, sin, zr], axis=1)
    keep = np.concatenate([np.ones((TM, KPE_LANE0 + ROPE_DIM), f32), np.zeros((TM, LANES - KPE_LANE0 - ROPE_DIM), f32)],
                          axis=1)
    return (jnp.asarray(np.concatenate([cos_t, keep], axis=0)),
            jnp.asarray(np.concatenate([sin_t, np.zeros((TM, LANES), f32)], axis=0)))


def _kpe_slab(k):
    pad = [(0, 0)] * (k.ndim - 1) + [(KPE_LANE0, LANES - KPE_LANE0 - ROPE_DIM)]
    return jnp.pad(k, pad)


def _with_swapped_pair(w_pe):
    half = ROPE_DIM // 2
    return jnp.concatenate([w_pe, w_pe[..., half:], w_pe[..., :half]], axis=-1)


def _layout_in_proj(wt):
    o = np.cumsum((0, Q_RANK, KV_RANK, ROPE_DIM, D_SSD, CONV_CH, SSD_HEADS, SSD_HEADS))
    cq, ckv, kpe, z, xbc, dtf, dtb = (wt[o[k]:o[k + 1], :] for k in range(7))
    zeros = lambda n: jnp.zeros((n, wt.shape[1]), wt.dtype)
    half = ROPE_DIM // 2
    assert KPE_LANE0 + 2 * ROPE_DIM == LANES
    stacked = jnp.concatenate([cq, ckv, z, xbc,
                               zeros(KPE_LANE0), kpe, kpe[half:], kpe[:half],
                               dtf, dtb, zeros(LANES - 2 * SSD_HEADS)], axis=0)
    return stacked.T.astype(jnp.bfloat16)


def _layout_uq(w):
    w = w.reshape(Q_RANK, MLA_HEADS, NOPE_DIM + ROPE_DIM)
    w = jnp.concatenate([w[:, :, :NOPE_DIM], _with_swapped_pair(w[:, :, NOPE_DIM:])], axis=-1)
    assert w.shape[-1] == HEAD_SLAB
    return w.reshape(Q_RANK, MLA_HEADS * HEAD_SLAB).astype(jnp.bfloat16)


def _layout_ukv(w):
    w = w.reshape(KV_RANK, MLA_HEADS, NOPE_DIM + V_DIM)
    kn = jnp.pad(w[:, :, :NOPE_DIM], ((0, 0), (0, 0), (0, HEAD_SLAB - NOPE_DIM)))
    v = w[:, :, NOPE_DIM:]
    return kn.reshape(KV_RANK, -1).astype(jnp.bfloat16), v.reshape(KV_RANK, -1).T.astype(jnp.bfloat16)


def kernel(x_prompt, x_sample, c, cache_mla_ckv, cache_mla_krope, state_ssd_fwd, state_ssd_bwd, c_ctx, w_mod, b_mod, norm_pre_mix, norm_post_mix, norm_pre_ffn, norm_post_ffn, w_in_ab, q_norm, w_uq, kv_norm, w_ukv, ssd_conv_w, ssd_conv_b, ssd_dt_bias_fwd, ssd_dt_bias_bwd, ssd_a_log_fwd, ssd_a_log_bwd, ssd_d, ssd_norm, w_out_ab, pool_w, pool_scale, ffn_w_gate, ffn_w_up, ffn_w_down):
    f32, bf16 = jnp.float32, jnp.bfloat16
    assert DEPTH == 2
    xp = x_prompt.reshape(N_PROMPT, D_MODEL)
    xs = x_sample.reshape(N_SAMPLE, D_MODEL)
    cvecs = jnp.concatenate([c_ctx[None, :], c, jnp.zeros((MOD_ROWS - N_MODVEC, D_MODEL), f32)], axis=0)
    mod, w_in = _modulation(cvecs, w_mod, b_mod, jnp.swapaxes(w_in_ab, 1, 2))
    tabs = _rope_tables()
    hp = SSD_HEADS * SSD_HEAD_DIM
    row = lambda v: v.reshape(1, -1)
    new_ckv, new_kpe, new_hf, new_hb = [], [], [], []

    vec_table, vec_rows = _pack_vectors(
        norm_pre_mix=norm_pre_mix, norm_post_mix=norm_post_mix, norm_pre_ffn=norm_pre_ffn, norm_post_ffn=norm_post_ffn,
        q_norm=q_norm, kv_norm=kv_norm, ssd_norm=ssd_norm, ssd_conv_b=ssd_conv_b, pool_scale=pool_scale,
        ssd_dt_bias=jnp.concatenate([ssd_dt_bias_fwd, ssd_dt_bias_bwd], axis=1),
        ssd_a_log=jnp.concatenate([ssd_a_log_fwd, ssd_a_log_bwd], axis=1))
    ffn_w_f32 = (ffn_w_gate, ffn_w_up, ffn_w_down)
    for l in range(DEPTH):
        if l % 2 == 0:
            i = l // 2
            q, ckv_n, kpe, z, xbc, dts_t, cum_t, w_t, ckv_prompt, kpe_prompt = _inproj(
                l, vec_table, vec_rows, xp, xs, mod, tabs, w_in, _layout_uq(w_uq[i]))
            heads = (dts_t, cum_t, w_t)
            w_kv = _layout_ukv(w_ukv[i])
            (att_p,) = _attention(q, ckv_n, kpe, w_kv, 0, BATCH, SEQ, n_pack=PROMPT_PACK)
            att_s, w_out, *ffn_w = _attention(q, ckv_n, kpe, w_kv, N_PROMPT, DEC_BATCH, DEC_SEQ,
                                              cache=(cache_mla_ckv[:, i], _kpe_slab(cache_mla_krope[:, i])),
                                              cast=[(w_out_ab, i)] + [(w, l) for w in ffn_w_f32])
            ssd_args = (ssd_conv_w, vec_table, vec_rows["ssd_conv_b"], row(jnp.repeat(ssd_d[i], SSD_HEAD_DIM)))
            y_p, hf, hb = _ssd(i, xbc, heads, None, *ssd_args, 0, BATCH, SEQ)
            y_s, _, _ = _ssd(i, xbc, heads, (state_ssd_fwd[:, i].reshape(DEC_BATCH, hp, SSD_STATE),
                                             state_ssd_bwd[:, i].reshape(DEC_BATCH, hp, SSD_STATE)),
                             *ssd_args, N_PROMPT, DEC_BATCH, DEC_SEQ)
            xa, *ffn_w = _outproj_ffn(l, i, vec_table, vec_rows, xp, xs, att_p, att_s, y_p, y_s, z, mod, w_out, *ffn_w,
                                      cast=[(w, l + 1) for w in ffn_w_f32])
            new_ckv.append(ckv_prompt.reshape(BATCH, SEQ, KV_RANK))
            new_kpe.append(jnp.swapaxes(kpe_prompt, 1, 2))
            new_hf.append(hf.reshape(BATCH, SSD_HEADS, SSD_HEAD_DIM, SSD_STATE))
            new_hb.append(hb.reshape(BATCH, SSD_HEADS, SSD_HEAD_DIM, SSD_STATE))
        else:
            j = l // 2
            yp, ys = _pool_ffn(l, j, vec_table, vec_rows, xa, mod, pool_w[j].astype(bf16), *ffn_w)

    return (yp.reshape(BATCH, SEQ, D_MODEL), ys.reshape(DEC_BATCH, DEC_SEQ, D_MODEL),
            jnp.stack(new_ckv, axis=1), jnp.stack(new_kpe, axis=1),
            jnp.stack(new_hf, axis=1), jnp.stack(new_hb, axis=1))
```

```python
import functools

import numpy as np
import jax
import jax.numpy as jnp
from jax import lax
from jax.experimental import pallas as pl
from jax.experimental.pallas import tpu as pltpu

D_MODEL = 1024
BATCH = 16
SEQ = 256
DEPTH = 2
DEC_BATCH = 2
DEC_SEQ = 2048
PAST_LEN = 256
GRID_W = 64
EPS = 1e-6
MLA_HEADS = 8
Q_RANK = 256
KV_RANK = 256
NOPE_DIM = 64
ROPE_DIM = 32
V_DIM = 64
ROPE_THETA = 10000.0
SSD_HEADS = 8
SSD_GROUPS = 2
SSD_HPG = SSD_HEADS // SSD_GROUPS
SSD_HEAD_DIM = 64
SSD_STATE = 128
D_SSD = SSD_HEADS * SSD_HEAD_DIM
CONV_W = 5
CONV_CH = D_SSD + 2 * SSD_GROUPS * SSD_STATE
POOL_WINDOWS = (2, 4, 8, 16)
POOL_GC = D_MODEL // len(POOL_WINDOWS)
D_FF = ((8 * D_MODEL + 3 * 256 - 1) // (3 * 256)) * 256

SUBLANES = 8
LANES = 128

N_PROMPT = BATCH * SEQ
N_SAMPLE = DEC_BATCH * DEC_SEQ
N_TOK = N_PROMPT + N_SAMPLE
N_MODVEC = 1 + DEC_BATCH
TM = 512
TQ = 256
CHUNK = 128
HALO = SUBLANES
HEAD_SLAB = LANES
IN_COLS = Q_RANK + KV_RANK + D_SSD + CONV_CH + 2 * LANES
KPE_LANE0 = NOPE_DIM
VMEM_LIMIT = 56 * 1024 * 1024

POOL_SUB = min(SEQ, DEC_SEQ)

assert TM % POOL_SUB == 0 and SEQ % POOL_SUB == 0 and DEC_SEQ % TM == 0 and N_PROMPT % DEC_SEQ == 0


def _rms(x, g):
    return x * lax.rsqrt(jnp.mean(x * x, axis=-1, keepdims=True) + EPS) * g


def _silu(x):
    return x * jax.nn.sigmoid(x)


def _softplus(x):
    return jnp.maximum(x, 0.0) + jnp.log1p(jnp.exp(-jnp.abs(x)))


def _dot(a, b):
    return jnp.dot(a, b, preferred_element_type=jnp.float32)


def _dot_nt(a, b):
    return lax.dot_general(a, b, (((1,), (1,)), ((), ())), preferred_element_type=jnp.float32)


def _mod_row(i):
    return jnp.where(i < N_PROMPT // TM, 0, 1 + (i - N_PROMPT // TM) // (DEC_SEQ // TM))


def _const_spec(shape):
    nd = len(shape)
    return pl.BlockSpec(shape, lambda *_: (0,) * nd, pipeline_mode=pl.Buffered(1))


N_PT = N_PROMPT // TM
N_TILES = N_TOK // TM


def _row_spec(width):
    return pl.BlockSpec((TM, width), lambda i: (i, 0))


def _mod_spec(l):
    return pl.BlockSpec((None, SUBLANES, 6 * D_MODEL), lambda i: (l, 0, 0), pipeline_mode=pl.Buffered(1))


def _mod_part(mod_ref, k):
    return mod_ref[pl.ds(_mod_row(pl.program_id(0)), 1), k * D_MODEL:(k + 1) * D_MODEL]


def _vec_spec(row):
    return pl.BlockSpec((None, 1, D_MODEL), lambda *_: (row, 0, 0), pipeline_mode=pl.Buffered(1))


def _narrow_spec(a, layer):
    assert a.shape[0] == 1
    return pl.BlockSpec((1, a.shape[1]), lambda *_: (layer, 0), pipeline_mode=pl.Buffered(1))


def _pack_vectors(**params):
    first_row, blocks, n = {}, [], 0
    for name, a in params.items():
        first_row[name] = n
        n += a.shape[0]
        blocks.append(jnp.pad(a, ((0, 0), (0, D_MODEL - a.shape[1]))))
    return jnp.concatenate(blocks, axis=0).reshape(n, 1, D_MODEL), first_row


def _group_specs(width):
    return [pl.BlockSpec((TM, width), lambda i: (jnp.minimum(i, N_PT - 1), 0)),
            pl.BlockSpec((TM, width), lambda i: (jnp.maximum(i - N_PT, 0), 0))]


N_CAST = 16


def _cast_in_spec(w, layer, step=lambda i: i):
    _, rows, cols = w.shape
    return pl.BlockSpec((1, rows // N_CAST, cols), lambda *g: (layer, step(*g), 0))


def _cast_out(w, step=lambda i: i):
    _, rows, cols = w.shape
    return (pl.BlockSpec((rows // N_CAST, cols), lambda *g: (step(*g), 0)),
            jax.ShapeDtypeStruct((rows, cols), jnp.bfloat16))


def _cast_chunks(src_refs, dst_refs):
    for src, dst in zip(src_refs, dst_refs):
        dst[...] = src[0].astype(jnp.bfloat16)


def _pick_group(i, p_ref, s_ref, rows=slice(None)):
    return jnp.where(i < N_PT, p_ref[rows, :], s_ref[rows, :])


MOD_TK = 256
MOD_STREAMS = 2
PROMPT_PACK = 4
MOD_ROWS = 2 * SUBLANES
SUB_ROWS = 256


def _split3(a):
    a_hi = a.astype(jnp.bfloat16)
    r1 = a - a_hi.astype(jnp.float32)
    a_mid = r1.astype(jnp.bfloat16)
    a_lo = (r1 - a_mid.astype(jnp.float32)).astype(jnp.bfloat16)
    return a_hi, a_mid, a_lo


def _mod_kernel(cctx_ref, c_ref, *refs):
    w_refs, (b_ref, win_ref, o_ref, wl_ref) = refs[:MOD_STREAMS], refs[MOD_STREAMS:]
    wl_ref[...] = _layout_in_proj(win_ref[0])
    k = pl.program_id(1)
    rows = MOD_TK // MOD_STREAMS
    rowid = lax.broadcasted_iota(jnp.int32, (MOD_ROWS, MOD_TK), 0)
    cv = jnp.where(rowid == 0, cctx_ref[...], 0.0)
    for b in range(c_ref.shape[0]):
        cv = jnp.where(rowid == 1 + b, c_ref[b:b + 1, :], cv)
    s_all = _silu(cv)
    part = jnp.zeros((MOD_ROWS, 6 * D_MODEL), jnp.float32)
    for j, w_ref in enumerate(w_refs):
        s_hi, s_mid, s_lo = _split3(s_all[:, j * rows:(j + 1) * rows])
        w = w_ref[0]
        w_hi = w.astype(jnp.bfloat16)
        w_lo = (w - w_hi.astype(jnp.float32)).astype(jnp.bfloat16)
        top = _dot(jnp.concatenate([s_hi, s_mid, s_lo], axis=0), w_hi)
        low = _dot(jnp.concatenate([s_hi, s_mid], axis=0), w_lo)
        part = part + (top[:MOD_ROWS] + top[MOD_ROWS:2 * MOD_ROWS] + top[2 * MOD_ROWS:]
                       + low[:MOD_ROWS] + low[MOD_ROWS:])
    part = part[:SUBLANES]

    @pl.when(k == 0)
    def _():
        o_ref[0] = part + b_ref[pl.ds(pl.program_id(0), 1), :]

    @pl.when(k > 0)
    def _():
        o_ref[0] += part


def _modulation(c_ctx, c, w_mod, b_mod, w_in_t):
    n = 6 * D_MODEL
    nk = D_MODEL // MOD_TK
    chunk = D_MODEL // (DEPTH * nk)
    assert w_in_t.shape[0] == 1
    w_spec = lambda j: pl.BlockSpec((1, MOD_TK // MOD_STREAMS, n), lambda l, k: (l, k * MOD_STREAMS + j, 0))
    return pl.pallas_call(
        _mod_kernel,
        grid=(DEPTH, D_MODEL // MOD_TK),
        in_specs=[pl.BlockSpec((1, MOD_TK), lambda l, k: (0, k)),
                  pl.BlockSpec((c.shape[0], MOD_TK), lambda l, k: (0, k))]
        + [w_spec(j) for j in range(MOD_STREAMS)]
        + [pl.BlockSpec((DEPTH, n), lambda l, k: (0, 0)),
           pl.BlockSpec((1, w_in_t.shape[1], chunk), lambda l, k: (0, 0, l * nk + k))],
        out_specs=[pl.BlockSpec((1, SUBLANES, n), lambda l, k: (l, 0, 0)),
                   pl.BlockSpec((chunk, IN_COLS), lambda l, k: (l * nk + k, 0))],
        out_shape=[jax.ShapeDtypeStruct((DEPTH, SUBLANES, n), jnp.float32),
                   jax.ShapeDtypeStruct((D_MODEL, IN_COLS), jnp.bfloat16)],
        compiler_params=pltpu.CompilerParams(dimension_semantics=("arbitrary", "arbitrary"),
                                             vmem_limit_bytes=VMEM_LIMIT),
        name="modulation",
    )(c_ctx.reshape(1, D_MODEL), c, *([w_mod] * MOD_STREAMS), b_mod, w_in_t)


def _ssd_head_terms(dt_raw, dtb_c, a_neg_c):
    nh2 = 2 * SSD_HEADS
    row = lax.broadcasted_iota(jnp.int32, (CHUNK, CHUNK), 0)
    col = lax.broadcasted_iota(jnp.int32, (CHUNK, CHUNK), 1)
    upper_b = (row <= col).astype(jnp.bfloat16)
    lower_b = (row >= col).astype(jnp.bfloat16)
    fwd_rows = lax.broadcasted_iota(jnp.int32, (nh2, CHUNK), 0) < SSD_HEADS
    dts_t = _softplus(dt_raw.T[:nh2, :] + dtb_c)
    pieces = _split3(dts_t * a_neg_c)
    cum_t = jnp.where(fwd_rows, sum(_dot(p, upper_b) for p in pieces), sum(_dot(p, lower_b) for p in pieces))
    cum_t = cum_t * np.float32(np.log2(np.e))
    tot = jnp.where(fwd_rows[:, :1], cum_t[:, CHUNK - 1:], cum_t[:, :1])
    return dts_t, cum_t, dts_t * jnp.exp2(tot - cum_t)


def _inproj_kernel(xp_ref, xs_ref, mod_ref, cos_ref, sin_ref, w_in_ref, w_uq_ref, qn_ref, kvn_ref, npm_ref,
                   dtb_ref, alog_ref, q_ref, ckv_ref, kpe_ref, z_ref, xbc_ref, dts_ref, cumt_ref, wt_ref,
                   new_ckv_ref, new_kpe_ref):
    i = pl.program_id(0)
    sh = _mod_part(mod_ref, 0)
    sc = _mod_part(mod_ref, 1)
    scale = (NOPE_DIM + ROPE_DIM) ** -0.5 * np.log2(np.e)
    nh2 = 2 * SSD_HEADS
    diag = (lax.broadcasted_iota(jnp.int32, (nh2, LANES), 0) == lax.broadcasted_iota(jnp.int32, (nh2, LANES), 1))
    to_col = lambda ref: jnp.sum(jnp.where(diag, ref[:, :LANES], 0.0), axis=1, keepdims=True)
    dtb_c = to_col(dtb_ref)
    a_neg_c = -jnp.exp(to_col(alog_ref))
    dt_raw = []
    for r in range(TM // SUB_ROWS):
        rs = slice(r * SUB_ROWS, (r + 1) * SUB_ROWS)
        h = (_rms(_pick_group(i, xp_ref, xs_ref, rs), npm_ref[...]) * (1.0 + sc) + sh).astype(jnp.bfloat16)
        p = _dot(h, w_in_ref[...])
        o = 0
        cq = p[:, o:o + Q_RANK]; o += Q_RANK
        ckv = p[:, o:o + KV_RANK]; o += KV_RANK
        z_ref[rs, :] = p[:, o:o + D_SSD]; o += D_SSD
        xbc_ref[rs, :] = p[:, o:o + CONV_CH]; o += CONV_CH
        kpe = p[:, o:o + LANES]; o += LANES
        dt_raw.append(p[:, o:o + LANES])

        cos = cos_ref[rs, :]
        sin = sin_ref[rs, :]

        def rope(slab):
            return slab * cos + pltpu.roll(slab, LANES - ROPE_DIM, 1) * sin

        ckv_ref[rs, :] = _rms(ckv, kvn_ref[:, :KV_RANK])
        kpe_ref[rs, :] = rope(kpe)
        q = _dot(_rms(cq, qn_ref[:, :Q_RANK]).astype(jnp.bfloat16), w_uq_ref[...]) * scale
        for hd in range(MLA_HEADS):
            sl = slice(hd * HEAD_SLAB, (hd + 1) * HEAD_SLAB)
            q_ref[rs, sl] = rope(q[:, sl]).astype(jnp.bfloat16)

    dt_all = jnp.concatenate(dt_raw, axis=0)
    for ck in range(TM // CHUNK):
        terms = _ssd_head_terms(dt_all[ck * CHUNK:(ck + 1) * CHUNK, :], dtb_c, a_neg_c)
        for ref, val in zip((dts_ref, cumt_ref, wt_ref), terms):
            ref[ck * nh2:(ck + 1) * nh2, :] = val

    @pl.when(i < N_PT)
    def _():
        new_ckv_ref[...] = ckv_ref[...]
        kpe_t = kpe_ref[...].T
        for b in range(TM // SEQ):
            new_kpe_ref[b] = kpe_t[KPE_LANE0:KPE_LANE0 + ROPE_DIM, b * SEQ:(b + 1) * SEQ]


def _inproj(l, vec_table, vec_rows, xp, xs, mod, tabs, w_in, w_uq, q_norm, kv_norm):
    nt = N_TILES
    prompt_blk = lambda i: (jnp.minimum(i, N_PT - 1), 0)
    lat_tiles = DEC_SEQ // TM
    tab_spec = pl.BlockSpec((TM, LANES), lambda i: (jnp.where(i < N_PT, lat_tiles, (i - N_PT) % lat_tiles), 0))
    row = _row_spec
    hrows = TM // CHUNK * 2 * SSD_HEADS
    head_spec = pl.BlockSpec((hrows, CHUNK), lambda i: (i, 0))
    head_shape = jax.ShapeDtypeStruct((nt * hrows, CHUNK), jnp.float32)
    return pl.pallas_call(
        _inproj_kernel,
        grid=(nt,),
        in_specs=_group_specs(D_MODEL) + [
            _mod_spec(l),
            tab_spec, tab_spec,
            _const_spec((D_MODEL, IN_COLS)),
            _const_spec((Q_RANK, MLA_HEADS * HEAD_SLAB)),
            _narrow_spec(q_norm, l // 2),
            _narrow_spec(kv_norm, l // 2),
            _vec_spec(vec_rows["norm_pre_mix"] + l),
            _vec_spec(vec_rows["ssd_dt_bias"] + l // 2),
            _vec_spec(vec_rows["ssd_a_log"] + l // 2),
        ],
        out_specs=[row(MLA_HEADS * HEAD_SLAB), row(KV_RANK), row(LANES), row(D_SSD), row(CONV_CH),
                   head_spec, head_spec, head_spec,
                   pl.BlockSpec((TM, KV_RANK), prompt_blk),
                   pl.BlockSpec((TM // SEQ, ROPE_DIM, SEQ), lambda i: (jnp.minimum(i, N_PT - 1), 0, 0))],
        out_shape=[
            jax.ShapeDtypeStruct((N_TOK, MLA_HEADS * HEAD_SLAB), jnp.bfloat16),
            jax.ShapeDtypeStruct((N_TOK, KV_RANK), jnp.float32),
            jax.ShapeDtypeStruct((N_TOK, LANES), jnp.float32),
            jax.ShapeDtypeStruct((N_TOK, D_SSD), jnp.float32),
            jax.ShapeDtypeStruct((N_TOK, CONV_CH), jnp.float32),
            head_shape, head_shape, head_shape,
            jax.ShapeDtypeStruct((N_PROMPT, KV_RANK), jnp.float32),
            jax.ShapeDtypeStruct((BATCH, ROPE_DIM, SEQ), jnp.float32),
        ],
        compiler_params=pltpu.CompilerParams(dimension_semantics=("arbitrary",), vmem_limit_bytes=VMEM_LIMIT),
        name="inproj",
    )(xp, xs, mod, *tabs, w_in, w_uq, q_norm, kv_norm, *([vec_table] * 3))


def _attn_kernel(*refs, lk_cache, lk_new, n_pack, n_cast):
    n_in = 7 if lk_cache else 5
    if lk_cache:
        q_ref, ckv_ref, kpe_ref, ckvc_ref, kpec_ref, wk_ref, wvt_ref = refs[:n_in]
    else:
        q_ref, ckv_ref, kpe_ref, wk_ref, wvt_ref = refs[:n_in]
    cast_src, o_ref = refs[n_in:n_in + n_cast], refs[n_in + n_cast]
    cast_dst = refs[n_in + n_cast + 1:n_in + 2 * n_cast + 1]
    k_scr, vt_scr = refs[n_in + 2 * n_cast + 1:]
    _cast_chunks(cast_src, cast_dst)
    lk = lk_cache + lk_new

    @pl.when(pl.program_id(1) == 0)
    def _expand_kv():
        def expand(ckv, kpe, r0):
            ckv_b = ckv.astype(jnp.bfloat16)
            kn = _dot(ckv_b, wk_ref[...])
            rows = slice(r0, r0 + ckv.shape[0])
            for hd in range(MLA_HEADS):
                k_scr[hd, rows, :] = (kn[:, hd * HEAD_SLAB:(hd + 1) * HEAD_SLAB] + kpe).astype(jnp.bfloat16)
            vt_scr[:, rows] = _dot_nt(wvt_ref[...], ckv_b).astype(jnp.bfloat16)

        step = 256
        for r0 in range(0, lk_cache, step):
            expand(ckvc_ref[0, r0:r0 + step, :], kpec_ref[0, r0:r0 + step, :], r0)
        for r0 in range(0, n_pack * lk_new, step):
            expand(ckv_ref[r0:r0 + step, :], kpe_ref[r0:r0 + step, :], lk_cache + r0)

    work = [(s, hd) for s in range(n_pack) for hd in range(MLA_HEADS)]
    scores = [_dot_nt(k_scr[hd, s * lk:(s + 1) * lk, :], q_ref[s * TQ:(s + 1) * TQ, hd * HEAD_SLAB:(hd + 1) * HEAD_SLAB])
              for s, hd in work]
    outs = []
    for (s, hd), s_t in zip(work, scores):
        p_t = jnp.exp2(s_t - jnp.max(s_t, axis=0, keepdims=True))
        den = jnp.sum(p_t, axis=0, keepdims=True)
        v_t = vt_scr[hd * V_DIM:(hd + 1) * V_DIM, s * lk:(s + 1) * lk]
        outs.append(_dot(v_t, p_t.astype(jnp.bfloat16)) / den)
    for s in range(n_pack):
        o_ref[s * TQ:(s + 1) * TQ, :] = jnp.concatenate(
            outs[s * MLA_HEADS:(s + 1) * MLA_HEADS], axis=0).T.astype(jnp.bfloat16)


def _attention(q, ckv_n, kpe, w_ukv, row_off, n_batch, seq, cache=None, n_pack=1, cast=()):
    nq = seq // TQ
    assert n_pack == 1 or (nq == 1 and cache is None and n_batch % n_pack == 0)
    n_batch //= n_pack
    lk_cache = 0 if cache is None else cache[0].shape[1]
    lk = n_pack * (lk_cache + seq)
    qblk = lambda b, qi: (row_off // (n_pack * TQ) + b * nq + qi, 0)
    sblk = lambda b, qi: (row_off // (n_pack * seq) + b, 0)
    in_specs = [
        pl.BlockSpec((n_pack * TQ, MLA_HEADS * HEAD_SLAB), qblk),
        pl.BlockSpec((n_pack * seq, KV_RANK), sblk),
        pl.BlockSpec((n_pack * seq, LANES), sblk),
    ]
    args = [q, ckv_n, kpe]
    if cache is not None:
        in_specs += [pl.BlockSpec((1, lk_cache, KV_RANK), lambda b, qi: (b, 0, 0)),
                     pl.BlockSpec((1, lk_cache, LANES), lambda b, qi: (b, 0, 0))]
        args += list(cache)
    in_specs += [_const_spec(w.shape) for w in w_ukv]
    args += list(w_ukv)
    assert not cast or n_batch * nq == N_CAST
    step = lambda b, qi: b * nq + qi
    in_specs += [_cast_in_spec(w, layer, step) for w, layer in cast]
    args += [w for w, _ in cast]
    cast_out = [_cast_out(w, step) for w, _ in cast]
    return pl.pallas_call(
        functools.partial(_attn_kernel, lk_cache=lk_cache, lk_new=seq, n_pack=n_pack, n_cast=len(cast)),
        grid=(n_batch, nq),
        in_specs=in_specs,
        out_specs=[pl.BlockSpec((n_pack * TQ, MLA_HEADS * V_DIM), lambda b, qi: (b * nq + qi, 0))]
        + [spec for spec, _ in cast_out],
        out_shape=[jax.ShapeDtypeStruct((n_batch * n_pack * seq, MLA_HEADS * V_DIM), jnp.bfloat16)]
        + [shape for _, shape in cast_out],
        scratch_shapes=[pltpu.VMEM((MLA_HEADS, lk, HEAD_SLAB), jnp.bfloat16),
                        pltpu.VMEM((MLA_HEADS * V_DIM, lk), jnp.bfloat16)],
        compiler_params=pltpu.CompilerParams(dimension_semantics=("arbitrary", "arbitrary"),
                                             vmem_limit_bytes=VMEM_LIMIT),
        name=f"attention_{seq}",
    )(*args)


def _ssd_kernel(*refs, seq, zero_init):
    if zero_init:
        (xbc_ref, dts_ref, cumt_ref, wt_ref, cw_ref, cb_ref, dsk_ref,
         y_ref, hf_ref, hb_ref, xs_scr, c_scr, bt_scr, cum_scr, stf_scr, stb_scr) = refs
    else:
        (xbc_ref, dts_ref, cumt_ref, wt_ref, h0f_ref, h0b_ref, cw_ref, cb_ref, dsk_ref,
         y_ref, hf_ref, hb_ref, xs_scr, c_scr, bt_scr, cum_scr, stf_scr, stb_scr) = refs
    nc = seq // CHUNK
    gs = SSD_GROUPS * SSD_STATE
    nh2 = 2 * SSD_HEADS

    row = lax.broadcasted_iota(jnp.int32, (CHUNK, CHUNK), 0)
    col = lax.broadcasted_iota(jnp.int32, (CHUNK, CHUNK), 1)
    low_half = col < SSD_HEAD_DIM
    lower = row >= col
    upper = row <= col

    def prep_chunk(c, carry):
        r0 = pl.multiple_of(c * CHUNK, CHUNK)
        rows = pl.ds(r0, CHUNK)
        rows_prev = pl.ds(pl.multiple_of(jnp.maximum(r0 - HALO, 0), HALO), HALO)
        rows_next = pl.ds(pl.multiple_of(jnp.minimum(r0 + CHUNK, seq - HALO), HALO), HALO)

        def conv_tile(cs):
            prev = jnp.where(c > 0, xbc_ref[rows_prev, cs], 0.0)
            nxt = jnp.where(c < nc - 1, xbc_ref[rows_next, cs], 0.0)
            win = jnp.concatenate([prev, xbc_ref[rows, cs], nxt], axis=0)
            acc = jnp.broadcast_to(cb_ref[:, cs], (CHUNK, LANES))
            for k in range(CONV_W):
                lo = HALO - CONV_W // 2 + k
                acc = acc + cw_ref[k:k + 1, cs] * win[lo:lo + CHUNK, :]
            return _silu(acc)

        def x_tile(j, carry):
            cs = pl.ds(pl.multiple_of(j * LANES, LANES), LANES)
            u = conv_tile(cs)
            y_ref[rows, cs] = dsk_ref[:, cs] * u
            xs_scr[0, rows, cs] = jnp.where(low_half, u, 0.0).astype(jnp.bfloat16)
            xs_scr[1, rows, cs] = jnp.where(low_half, 0.0, u).astype(jnp.bfloat16)
            return carry

        lax.fori_loop(0, D_SSD // LANES, x_tile, 0)
        cum_t = cumt_ref[pl.ds(pl.multiple_of(c * nh2, nh2), nh2), :]
        cum_scr[rows, :] = jnp.concatenate(
            [cum_t, jnp.zeros((CHUNK - nh2, CHUNK), jnp.float32)], axis=0).T[:, :nh2]
        for g in range(SSD_GROUPS):
            b0 = pl.multiple_of(c * gs + g * SSD_STATE, SSD_STATE)
            bt_scr[pl.ds(b0, SSD_STATE), :] = conv_tile(slice(D_SSD + g * SSD_STATE, D_SSD + (g + 1) * SSD_STATE)).T
            c_scr[rows, g * SSD_STATE:(g + 1) * SSD_STATE] = conv_tile(
                slice(D_SSD + gs + g * SSD_STATE, D_SSD + gs + (g + 1) * SSD_STATE)).astype(jnp.bfloat16)
        return carry

    lax.fori_loop(0, nc, prep_chunk, 0)

    if zero_init:
        stf_scr[...] = jnp.zeros_like(stf_scr)
        stb_scr[...] = jnp.zeros_like(stb_scr)
    else:
        stf_scr[...] = h0f_ref[0].T
        stb_scr[...] = h0b_ref[0].T

    def scan_open(ci, st_scr):
        rows = pl.ds(pl.multiple_of(ci * CHUNK, CHUNK), CHUNK)
        c_b = c_scr[rows, :]
        st = st_scr[...]
        bts, cbms, zs = [], [], []
        for g in range(SSD_GROUPS):
            cg = c_b[:, g * SSD_STATE:(g + 1) * SSD_STATE]
            bt = bt_scr[pl.ds(pl.multiple_of(ci * gs + g * SSD_STATE, SSD_STATE), SSD_STATE), :]
            gcols = slice(g * SSD_HPG * SSD_HEAD_DIM, (g + 1) * SSD_HPG * SSD_HEAD_DIM)
            bts.append(bt)
            cbms.append(_dot(cg, bt.astype(jnp.bfloat16)))
            zs.append(_dot(cg, st[:, gcols].astype(jnp.bfloat16)))
        return st, bts, cbms, zs

    def scan_pairs(ci, st_scr, reverse, opened):
        st, bts, cbms, zs = opened
        lane0 = SSD_HEADS if reverse else 0
        causal = upper if reverse else lower
        last = 0 if reverse else CHUNK - 1
        rows = pl.ds(pl.multiple_of(ci * CHUNK, CHUNK), CHUNK)
        hrows = pl.ds(pl.multiple_of(ci * nh2, nh2), nh2)
        xs_lo = xs_scr[0, rows, :]
        xs_hi = xs_scr[1, rows, :]
        dts_t = dts_ref[hrows, :]
        cum_t = cumt_ref[hrows, :]
        w_t = wt_ref[hrows, :]
        cum = cum_scr[rows, :]
        for pair in range(SSD_HEADS // 2):
            g, jj = divmod(pair, SSD_HPG // 2)
            pcols = slice(pair * LANES, (pair + 1) * LANES)
            lhs_y, lhs_s, entry = [], [], []
            for hd in (2 * pair, 2 * pair + 1):
                ln = lane0 + hd
                cum_i = jnp.broadcast_to(cum[:, ln:ln + 1], (CHUNK, CHUNK))
                dec = jnp.exp2(jnp.where(causal, cum_i - cum_t[ln:ln + 1, :], -jnp.inf))
                lhs_y.append((cbms[g] * dec * dts_t[ln:ln + 1, :]).astype(jnp.bfloat16))
                lhs_s.append((bts[g] * w_t[ln:ln + 1, :]).astype(jnp.bfloat16))
                entry.append(jnp.exp2(cum_i))
            lhs = jnp.concatenate([jnp.concatenate(lhs_y, axis=1), jnp.concatenate(lhs_s, axis=1)], axis=0)
            out = _dot(lhs, jnp.concatenate([xs_lo[:, pcols], xs_hi[:, pcols]], axis=0))
            ea = jnp.where(low_half, entry[0], entry[1])
            y_ref[rows, pcols] += out[:CHUNK] + zs[g][:, jj * LANES:(jj + 1) * LANES] * ea
            st_scr[:, pcols] = ea[last:last + 1, :] * st[:, pcols] + out[CHUNK:]

    def both(c, carry):
        opened_f = scan_open(c, stf_scr)
        opened_b = scan_open(nc - 1 - c, stb_scr)
        scan_pairs(c, stf_scr, False, opened_f)
        scan_pairs(nc - 1 - c, stb_scr, True, opened_b)
        return carry

    lax.fori_loop(0, nc, both, 0)
    hf_ref[0] = stf_scr[...].T
    hb_ref[0] = stb_scr[...].T


def _ssd(i_ab, xbc, head_terms, h0, conv_w, vec_table, conv_b_row, d_skip, row_off, n_batch, seq):
    assert CONV_CH == D_MODEL
    hp = SSD_HEADS * SSD_HEAD_DIM
    gs = SSD_GROUPS * SSD_STATE
    nc = seq // CHUNK
    seq_blk = lambda b: (row_off // seq + b, 0)
    head_spec = pl.BlockSpec((nc * 2 * SSD_HEADS, CHUNK), seq_blk)
    st_spec = pl.BlockSpec((1, hp, SSD_STATE), lambda b: (b, 0, 0))
    st_shape = jax.ShapeDtypeStruct((n_batch, hp, SSD_STATE), jnp.float32)
    h0 = () if h0 is None else tuple(h0)
    return pl.pallas_call(
        functools.partial(_ssd_kernel, seq=seq, zero_init=not h0),
        grid=(n_batch,),
        in_specs=[pl.BlockSpec((seq, CONV_CH), seq_blk), head_spec, head_spec, head_spec] + [st_spec] * len(h0) + [
            pl.BlockSpec((None, CONV_W, CONV_CH), lambda b: (i_ab, 0, 0), pipeline_mode=pl.Buffered(1)),
            _vec_spec(conv_b_row + i_ab), _const_spec((1, D_SSD))],
        out_specs=[pl.BlockSpec((seq, D_SSD), lambda b: (b, 0)), st_spec, st_spec],
        out_shape=[jax.ShapeDtypeStruct((n_batch * seq, D_SSD), jnp.float32), st_shape, st_shape],
        scratch_shapes=[pltpu.VMEM((2, seq, D_SSD), jnp.bfloat16),
                        pltpu.VMEM((seq, gs), jnp.bfloat16),
                        pltpu.VMEM((nc * gs, CHUNK), jnp.float32),
                        pltpu.VMEM((seq, 2 * SSD_HEADS), jnp.float32),
                        pltpu.VMEM((SSD_STATE, hp), jnp.float32),
                        pltpu.VMEM((SSD_STATE, hp), jnp.float32)],
        compiler_params=pltpu.CompilerParams(dimension_semantics=("arbitrary",), vmem_limit_bytes=VMEM_LIMIT),
        name=f"ssd_{seq}",
    )(xbc, *head_terms, *h0, conv_w, vec_table, d_skip)


def _post_mix(x, mix, mod_ref, npost_ref, npre_ref):
    d = D_MODEL
    gate_mix = _mod_part(mod_ref, 2)
    shf = _mod_part(mod_ref, 3)
    scf = _mod_part(mod_ref, 4)
    x1 = x + gate_mix * _rms(mix, npost_ref[...])
    return x1, (_rms(x1, npre_ref[...]) * (1.0 + scf) + shf).astype(jnp.bfloat16)


def _subtile_pipeline(n_sub, mixer_pre, mixer_dots, mod_ref, nffn_ref, wg_ref, wu_ref, wd_ref, interleave):
    gate_ffn = _mod_part(mod_ref, 5)
    ffn_up = lambda h: (_silu(_dot(h, wg_ref[...])) * _dot(h, wu_ref[...])).astype(jnp.bfloat16)
    ffn_down = lambda x1, hid: x1 + gate_ffn * _rms(_dot(hid, wd_ref[...]), nffn_ref[...])
    if not interleave:
        staged = [mixer_dots(r, mixer_pre(r)) for r in range(n_sub)]
        return [ffn_down(x1, ffn_up(h)) for x1, h in staged]
    outs = []
    x1, h = mixer_dots(0, mixer_pre(0))
    for r in range(n_sub):
        nxt_pre = mixer_pre(r + 1) if r + 1 < n_sub else None
        hid = ffn_up(h)
        nxt = mixer_dots(r + 1, nxt_pre) if r + 1 < n_sub else None
        outs.append(ffn_down(x1, hid))
        if nxt is not None:
            x1, h = nxt
    return outs


def _ffn_specs(l, vec_rows):
    return [_vec_spec(vec_rows[name] + l) for name in ("norm_post_mix", "norm_pre_ffn", "norm_post_ffn")] + [
        _const_spec((D_MODEL, D_FF)), _const_spec((D_MODEL, D_FF)), _const_spec((D_FF, D_MODEL))]


def _outproj_ffn_kernel(*refs, n_cast):
    (xp_ref, xs_ref, attp_ref, atts_ref, yp_ref, ys_ref, z_ref, mod_ref, sn_ref, wo_ref,
     npost_ref, npre_ref, nffn_ref, wg_ref, wu_ref, wd_ref) = refs[:16]
    cast_src, o_ref, cast_dst = refs[16:16 + n_cast], refs[16 + n_cast], refs[17 + n_cast:]
    i = pl.program_id(0)
    _cast_chunks(cast_src, cast_dst)
    gw = D_SSD // SSD_GROUPS
    sub = lambda r: slice(r * SUB_ROWS, (r + 1) * SUB_ROWS)

    def mixer_pre(r):
        yg = _pick_group(i, yp_ref, ys_ref, sub(r)) * _silu(z_ref[sub(r), :])
        parts = [_pick_group(i, attp_ref, atts_ref, sub(r))]
        for g in range(SSD_GROUPS):
            parts.append(_rms(yg[:, g * gw:(g + 1) * gw], sn_ref[:, g * gw:(g + 1) * gw]).astype(jnp.bfloat16))
        return jnp.concatenate(parts, axis=1)

    def mixer_dots(r, cat):
        return _post_mix(_pick_group(i, xp_ref, xs_ref, sub(r)), _dot(cat, wo_ref[...]),
                         mod_ref, npost_ref, npre_ref)

    outs = _subtile_pipeline(TM // SUB_ROWS, mixer_pre, mixer_dots, mod_ref, nffn_ref, wg_ref, wu_ref, wd_ref,
                             interleave=False)
    for r, res in enumerate(outs):
        o_ref[sub(r), :] = res


def _outproj_ffn(l, i_ab, vec_table, vec_rows, ssd_norm, xp, xs, att_p, att_s, y_p, y_s, z, mod, w_out, wg, wu, wd,
                 cast):
    d_cat = MLA_HEADS * V_DIM + D_SSD
    assert not cast or N_TILES == N_CAST
    cast_out = [_cast_out(w) for w, _ in cast]
    return pl.pallas_call(
        functools.partial(_outproj_ffn_kernel, n_cast=len(cast)),
        grid=(N_TILES,),
        in_specs=(_group_specs(D_MODEL) + _group_specs(MLA_HEADS * V_DIM) + _group_specs(D_SSD)
                  + [_row_spec(D_SSD), _mod_spec(l), _narrow_spec(ssd_norm, i_ab),
                     _const_spec((d_cat, D_MODEL))] + _ffn_specs(l, vec_rows)
                  + [_cast_in_spec(w, layer) for w, layer in cast]),
        out_specs=[_row_spec(D_MODEL)] + [spec for spec, _ in cast_out],
        out_shape=[jax.ShapeDtypeStruct((N_TOK, D_MODEL), jnp.float32)] + [shape for _, shape in cast_out],
        compiler_params=pltpu.CompilerParams(dimension_semantics=("arbitrary",), vmem_limit_bytes=VMEM_LIMIT),
        name="outproj_ffn",
    )(xp, xs, att_p, att_s, y_p, y_s, z, mod, ssd_norm, w_out, vec_table, vec_table, vec_table, wg, wu, wd,
      *[w for w, _ in cast])


def _pool_ffn_kernel(x_ref, xp_ref, xn_ref, mod_ref, nmix_ref, pw_ref, ps_ref,
                     npost_ref, npre_ref, nffn_ref, wg_ref, wu_ref, wd_ref, op_ref, os_ref):
    i = pl.program_id(0)

    @pl.when(i == 0)
    def _():
        os_ref[...] = jnp.zeros_like(os_ref)

    seq = jnp.where(i < N_PROMPT // TM, SEQ, DEC_SEQ)
    pos0 = (i * TM) % seq
    sh = _mod_part(mod_ref, 0)
    sc = _mod_part(mod_ref, 1)
    hmod = lambda v: _rms(v, nmix_ref[...]) * (1.0 + sc) + sh
    n_rows = POOL_SUB + 2 * HALO

    def shifted(v, s):
        return pltpu.roll(v, n_rows - s, 0)

    def mixer_pre(s):
        lo, hi = s * POOL_SUB, (s + 1) * POOL_SUB
        pos_s = (pos0 + lo) % seq
        h = hmod(x_ref[lo:hi, :])
        before = hmod(xp_ref[...] if s == 0 else x_ref[lo - HALO:lo, :])
        after = hmod(xn_ref[...] if hi == TM else x_ref[hi:hi + HALO, :])
        before = jnp.where(pos_s > 0, before, 0.0)
        after = jnp.where(pos_s + POOL_SUB < seq, after, 0.0)
        padded = jnp.concatenate([before, h, after], axis=0)
        pos = pos_s + lax.broadcasted_iota(jnp.int32, (POOL_SUB, 1), 0)
        pooled = []
        for gi, w in enumerate(POOL_WINDOWS):
            cols = slice(gi * POOL_GC, (gi + 1) * POOL_GC)
            t = padded[:, cols]
            span = 1
            while span < w:
                t = t + shifted(t, span)
                span *= 2
            lead = HALO - w // 2
            win_sum = (shifted(t, lead) if lead else t)[:POOL_SUB, :]
            cnt = (jnp.minimum(pos + w // 2, seq) - jnp.maximum(pos - w // 2, 0)).astype(jnp.float32)
            pooled.append((win_sum / cnt - h[:, cols]).astype(jnp.bfloat16))
        return pooled

    def mixer_dots(s, pooled):
        mix = jnp.concatenate([_dot(p, pw_ref[gi]) for gi, p in enumerate(pooled)], axis=1) * ps_ref[...]
        return _post_mix(x_ref[s * POOL_SUB:(s + 1) * POOL_SUB, :], mix, mod_ref, npost_ref, npre_ref)

    res = jnp.concatenate(_subtile_pipeline(TM // POOL_SUB, mixer_pre, mixer_dots, mod_ref, nffn_ref,
                                            wg_ref, wu_ref, wd_ref, interleave=True), axis=0)

    @pl.when(i < N_PT)
    def _():
        op_ref[...] = res

    @pl.when(i >= N_PT)
    def _():
        os_ref[...] = res


def _pool_ffn(l, j_c, vec_table, vec_rows, xa, mod, pool_w, wg, wu, wd):
    hb = TM // HALO
    nh = N_TOK // HALO
    return pl.pallas_call(
        _pool_ffn_kernel,
        grid=(N_TILES,),
        in_specs=[_row_spec(D_MODEL),
                  pl.BlockSpec((HALO, D_MODEL), lambda i: (jnp.maximum(i * hb - 1, 0), 0)),
                  pl.BlockSpec((HALO, D_MODEL), lambda i: (jnp.minimum((i + 1) * hb, nh - 1), 0)),
                  _mod_spec(l),
                  _vec_spec(vec_rows["norm_pre_mix"] + l),
                  _const_spec((len(POOL_WINDOWS), POOL_GC, POOL_GC)),
                  _vec_spec(vec_rows["pool_scale"] + j_c)] + _ffn_specs(l, vec_rows),
        out_specs=_group_specs(D_MODEL),
        out_shape=[jax.ShapeDtypeStruct((N_PROMPT, D_MODEL), jnp.float32),
                   jax.ShapeDtypeStruct((N_SAMPLE, D_MODEL), jnp.float32)],
        compiler_params=pltpu.CompilerParams(dimension_semantics=("arbitrary",), vmem_limit_bytes=VMEM_LIMIT),
        name="pool_ffn",
    )(xa, xa, xa, mod, vec_table, pool_w, vec_table, vec_table, vec_table, vec_table, wg, wu, wd)


def _rope_tables():
    f32 = np.float32
    rows = DEC_SEQ // GRID_W
    r = np.repeat(np.arange(rows, dtype=f32), GRID_W)
    c = np.tile(np.arange(GRID_W, dtype=f32), rows)
    half = ROPE_DIM // 2
    inv_freq = np.power(f32(ROPE_THETA), -np.arange(0, half, 2, dtype=f32) / f32(half)).astype(f32)
    ang = np.concatenate([r[:, None] * inv_freq, c[:, None] * inv_freq], axis=-1).astype(f32)
    cos, sin = np.cos(ang).astype(f32), np.sin(ang).astype(f32)
    ones = np.ones((DEC_SEQ, KPE_LANE0), f32)
    zl = np.zeros((DEC_SEQ, KPE_LANE0), f32)
    zr = np.zeros((DEC_SEQ, LANES - KPE_LANE0 - ROPE_DIM), f32)
    cos_t = np.concatenate([ones, cos, cos, zr], axis=1)
    sin_t = np.concatenate([zl, -sin, sin, zr], axis=1)
    keep = np.concatenate([np.ones((TM, KPE_LANE0 + ROPE_DIM), f32), np.zeros((TM, LANES - KPE_LANE0 - ROPE_DIM), f32)],
                          axis=1)
    return (jnp.asarray(np.concatenate([cos_t, keep], axis=0)),
            jnp.asarray(np.concatenate([sin_t, np.zeros((TM, LANES), f32)], axis=0)))


def _kpe_slab(k):
    pad = [(0, 0)] * (k.ndim - 1) + [(KPE_LANE0, LANES - KPE_LANE0 - ROPE_DIM)]
    return jnp.pad(k, pad)


def _with_swapped_pair(w_pe):
    half = ROPE_DIM // 2
    return jnp.concatenate([w_pe, w_pe[..., half:], w_pe[..., :half]], axis=-1)


def _layout_in_proj(wt):
    o = np.cumsum((0, Q_RANK, KV_RANK, ROPE_DIM, D_SSD, CONV_CH, SSD_HEADS, SSD_HEADS))
    cq, ckv, kpe, z, xbc, dtf, dtb = (wt[o[k]:o[k + 1], :] for k in range(7))
    zeros = lambda n: jnp.zeros((n, wt.shape[1]), wt.dtype)
    half = ROPE_DIM // 2
    assert KPE_LANE0 + 2 * ROPE_DIM == LANES
    stacked = jnp.concatenate([cq, ckv, z, xbc,
                               zeros(KPE_LANE0), kpe, kpe[half:], kpe[:half],
                               dtf, dtb, zeros(LANES - 2 * SSD_HEADS)], axis=0)
    return stacked.T.astype(jnp.bfloat16)


def _layout_uq(w):
    w = w.reshape(Q_RANK, MLA_HEADS, NOPE_DIM + ROPE_DIM)
    w = jnp.concatenate([w[:, :, :NOPE_DIM], _with_swapped_pair(w[:, :, NOPE_DIM:])], axis=-1)
    assert w.shape[-1] == HEAD_SLAB
    return w.reshape(Q_RANK, MLA_HEADS * HEAD_SLAB).astype(jnp.bfloat16)


def _layout_ukv(w):
    w = w.reshape(KV_RANK, MLA_HEADS, NOPE_DIM + V_DIM)
    kn = jnp.pad(w[:, :, :NOPE_DIM], ((0, 0), (0, 0), (0, HEAD_SLAB - NOPE_DIM)))
    v = w[:, :, NOPE_DIM:]
    return kn.reshape(KV_RANK, -1).astype(jnp.bfloat16), v.reshape(KV_RANK, -1).T.astype(jnp.bfloat16)


def kernel(x_prompt, x_sample, c, cache_mla_ckv, cache_mla_krope, state_ssd_fwd, state_ssd_bwd, c_ctx, w_mod, b_mod, norm_pre_mix, norm_post_mix, norm_pre_ffn, norm_post_ffn, w_in_ab, q_norm, w_uq, kv_norm, w_ukv, ssd_conv_w, ssd_conv_b, ssd_dt_bias_fwd, ssd_dt_bias_bwd, ssd_a_log_fwd, ssd_a_log_bwd, ssd_d, ssd_norm, w_out_ab, pool_w, pool_scale, ffn_w_gate, ffn_w_up, ffn_w_down):
    f32, bf16 = jnp.float32, jnp.bfloat16
    assert DEPTH == 2
    xp = x_prompt.reshape(N_PROMPT, D_MODEL)
    xs = x_sample.reshape(N_SAMPLE, D_MODEL)
    assert 1 + c.shape[0] == N_MODVEC
    mod, w_in = _modulation(c_ctx, c, w_mod, b_mod, jnp.swapaxes(w_in_ab, 1, 2))
    tabs = _rope_tables()
    hp = SSD_HEADS * SSD_HEAD_DIM
    row = lambda v: v.reshape(1, -1)
    new_ckv, new_kpe, new_hf, new_hb = [], [], [], []

    vec_table, vec_rows = _pack_vectors(
        norm_pre_mix=norm_pre_mix, norm_post_mix=norm_post_mix, norm_pre_ffn=norm_pre_ffn, norm_post_ffn=norm_post_ffn,
        ssd_conv_b=ssd_conv_b, pool_scale=pool_scale,
        ssd_dt_bias=jnp.concatenate([ssd_dt_bias_fwd, ssd_dt_bias_bwd], axis=1),
        ssd_a_log=jnp.concatenate([ssd_a_log_fwd, ssd_a_log_bwd], axis=1))
    ffn_w_f32 = (ffn_w_gate, ffn_w_up, ffn_w_down)
    for l in range(DEPTH):
        if l % 2 == 0:
            i = l // 2
            q, ckv_n, kpe, z, xbc, dts_t, cum_t, w_t, ckv_prompt, kpe_prompt = _inproj(
                l, vec_table, vec_rows, xp, xs, mod, tabs, w_in, _layout_uq(w_uq[i]), q_norm, kv_norm)
            heads = (dts_t, cum_t, w_t)
            w_kv = _layout_ukv(w_ukv[i])
            (att_p,) = _attention(q, ckv_n, kpe, w_kv, 0, BATCH, SEQ, n_pack=PROMPT_PACK)
            att_s, w_out, *ffn_w = _attention(q, ckv_n, kpe, w_kv, N_PROMPT, DEC_BATCH, DEC_SEQ,
                                              cache=(cache_mla_ckv[:, i], _kpe_slab(cache_mla_krope[:, i])),
                                              cast=[(w_out_ab, i)] + [(w, l) for w in ffn_w_f32])
            ssd_args = (ssd_conv_w, vec_table, vec_rows["ssd_conv_b"], row(jnp.repeat(ssd_d[i], SSD_HEAD_DIM)))
            y_p, hf, hb = _ssd(i, xbc, heads, None, *ssd_args, 0, BATCH, SEQ)
            y_s, _, _ = _ssd(i, xbc, heads, (state_ssd_fwd[:, i].reshape(DEC_BATCH, hp, SSD_STATE),
                                             state_ssd_bwd[:, i].reshape(DEC_BATCH, hp, SSD_STATE)),
                             *ssd_args, N_PROMPT, DEC_BATCH, DEC_SEQ)
            xa, *ffn_w = _outproj_ffn(l, i, vec_table, vec_rows, ssd_norm, xp, xs, att_p, att_s, y_p, y_s, z, mod, w_out, *ffn_w,
                                      cast=[(w, l + 1) for w in ffn_w_f32])
            new_ckv.append(ckv_prompt.reshape(BATCH, SEQ, KV_RANK))
            new_kpe.append(jnp.swapaxes(kpe_prompt, 1, 2))
            new_hf.append(hf.reshape(BATCH, SSD_HEADS, SSD_HEAD_DIM, SSD_STATE))
            new_hb.append(hb.reshape(BATCH, SSD_HEADS, SSD_HEAD_DIM, SSD_STATE))
        else:
            j = l // 2
            yp, ys = _pool_ffn(l, j, vec_table, vec_rows, xa, mod, pool_w[j].astype(bf16), *ffn_w)

    return (yp.reshape(BATCH, SEQ, D_MODEL), ys.reshape(DEC_BATCH, DEC_SEQ, D_MODEL),
            jnp.stack(new_ckv, axis=1), jnp.stack(new_kpe, axis=1),
            jnp.stack(new_hf, axis=1), jnp.stack(new_hb, axis=1))
```

```python
import functools

import numpy as np
import jax
import jax.numpy as jnp
from jax import lax
from jax.experimental import pallas as pl
from jax.experimental.pallas import tpu as pltpu

D_MODEL = 1024
BATCH = 16
SEQ = 256
DEPTH = 2
DEC_BATCH = 2
DEC_SEQ = 2048
PAST_LEN = 256
GRID_W = 64
EPS = 1e-6
MLA_HEADS = 8
Q_RANK = 256
KV_RANK = 256
NOPE_DIM = 64
ROPE_DIM = 32
V_DIM = 64
ROPE_THETA = 10000.0
SSD_HEADS = 8
SSD_GROUPS = 2
SSD_HPG = SSD_HEADS // SSD_GROUPS
SSD_HEAD_DIM = 64
SSD_STATE = 128
D_SSD = SSD_HEADS * SSD_HEAD_DIM
CONV_W = 5
CONV_CH = D_SSD + 2 * SSD_GROUPS * SSD_STATE
POOL_WINDOWS = (2, 4, 8, 16)
POOL_GC = D_MODEL // len(POOL_WINDOWS)
D_FF = ((8 * D_MODEL + 3 * 256 - 1) // (3 * 256)) * 256

SUBLANES = 8
LANES = 128

N_PROMPT = BATCH * SEQ
N_SAMPLE = DEC_BATCH * DEC_SEQ
N_TOK = N_PROMPT + N_SAMPLE
N_MODVEC = 1 + DEC_BATCH
TM = 512
TQ = 256
CHUNK = 128
HALO = SUBLANES
HEAD_SLAB = LANES
IN_COLS = Q_RANK + KV_RANK + D_SSD + CONV_CH + 2 * LANES
KPE_LANE0 = NOPE_DIM
VMEM_LIMIT = 56 * 1024 * 1024

POOL_SUB = min(SEQ, DEC_SEQ)

assert TM % POOL_SUB == 0 and SEQ % POOL_SUB == 0 and DEC_SEQ % TM == 0 and N_PROMPT % DEC_SEQ == 0


def _rms(x, g):
    return x * lax.rsqrt(jnp.mean(x * x, axis=-1, keepdims=True) + EPS) * g


def _silu(x):
    return x * jax.nn.sigmoid(x)


def _softplus(x):
    return jnp.maximum(x, 0.0) + jnp.log1p(jnp.exp(-jnp.abs(x)))


def _dot(a, b):
    return jnp.dot(a, b, preferred_element_type=jnp.float32)


def _dot_nt(a, b):
    return lax.dot_general(a, b, (((1,), (1,)), ((), ())), preferred_element_type=jnp.float32)


def _mod_row(i):
    return jnp.where(i < N_PROMPT // TM, 0, 1 + (i - N_PROMPT // TM) // (DEC_SEQ // TM))


def _const_spec(shape):
    nd = len(shape)
    return pl.BlockSpec(shape, lambda *_: (0,) * nd, pipeline_mode=pl.Buffered(1))


N_PT = N_PROMPT // TM
N_TILES = N_TOK // TM


def _row_spec(width):
    return pl.BlockSpec((TM, width), lambda i: (i, 0))


def _mod_spec(l):
    return pl.BlockSpec((None, SUBLANES, 6 * D_MODEL), lambda i: (l, 0, 0), pipeline_mode=pl.Buffered(1))


def _mod_part(mod_ref, k):
    return mod_ref[pl.ds(_mod_row(pl.program_id(0)), 1), k * D_MODEL:(k + 1) * D_MODEL]


def _vec_spec(row):
    return pl.BlockSpec((None, 1, D_MODEL), lambda *_: (row, 0, 0), pipeline_mode=pl.Buffered(1))


def _narrow_spec(a, layer):
    assert a.shape[0] == 1
    return pl.BlockSpec((1, a.shape[1]), lambda *_: (layer, 0), pipeline_mode=pl.Buffered(1))


def _pack_vectors(**params):
    first_row, blocks, n = {}, [], 0
    for name, a in params.items():
        first_row[name] = n
        n += a.shape[0]
        blocks.append(jnp.pad(a, ((0, 0), (0, D_MODEL - a.shape[1]))))
    return jnp.concatenate(blocks, axis=0).reshape(n, 1, D_MODEL), first_row


def _group_specs(width):
    return [pl.BlockSpec((TM, width), lambda i: (jnp.minimum(i, N_PT - 1), 0)),
            pl.BlockSpec((TM, width), lambda i: (jnp.maximum(i - N_PT, 0), 0))]


N_CAST = 16


def _cast_in_spec(w, layer, step=lambda i: i):
    _, rows, cols = w.shape
    return pl.BlockSpec((1, rows // N_CAST, cols), lambda *g: (layer, step(*g), 0))


def _cast_out(w, step=lambda i: i):
    _, rows, cols = w.shape
    return (pl.BlockSpec((rows // N_CAST, cols), lambda *g: (step(*g), 0)),
            jax.ShapeDtypeStruct((rows, cols), jnp.bfloat16))


def _cast_chunks(src_refs, dst_refs):
    for src, dst in zip(src_refs, dst_refs):
        dst[...] = src[0].astype(jnp.bfloat16)


def _pick_group(i, p_ref, s_ref, rows=slice(None)):
    return jnp.where(i < N_PT, p_ref[rows, :], s_ref[rows, :])


MOD_TK = 256
MOD_STREAMS = 2
PROMPT_PACK = 4
MOD_ROWS = 2 * SUBLANES
SUB_ROWS = 256


def _split3(a):
    a_hi = a.astype(jnp.bfloat16)
    r1 = a - a_hi.astype(jnp.float32)
    a_mid = r1.astype(jnp.bfloat16)
    a_lo = (r1 - a_mid.astype(jnp.float32)).astype(jnp.bfloat16)
    return a_hi, a_mid, a_lo


def _mod_kernel(cctx_ref, c_ref, *refs):
    w_refs, (b_ref, win_ref, wuq_ref, wukv_ref, o_ref, wl_ref, uq_ref, wk_ref, wvt_ref) = (
        refs[:MOD_STREAMS], refs[MOD_STREAMS:])
    wl_ref[...] = _layout_in_proj(win_ref[0])
    k = pl.program_id(1)

    @pl.when((pl.program_id(0) == 0) & (k == 0))
    def _():
        uq_ref[...] = _layout_uq(wuq_ref[0])
        wk_ref[...], wvt_ref[...] = _layout_ukv(wukv_ref[0])

    rows = MOD_TK // MOD_STREAMS
    rowid = lax.broadcasted_iota(jnp.int32, (MOD_ROWS, MOD_TK), 0)
    cv = jnp.where(rowid == 0, cctx_ref[...], 0.0)
    for b in range(c_ref.shape[0]):
        cv = jnp.where(rowid == 1 + b, c_ref[b:b + 1, :], cv)
    s_all = _silu(cv)
    part = jnp.zeros((MOD_ROWS, 6 * D_MODEL), jnp.float32)
    for j, w_ref in enumerate(w_refs):
        s_hi, s_mid, s_lo = _split3(s_all[:, j * rows:(j + 1) * rows])
        w = w_ref[0]
        w_hi = w.astype(jnp.bfloat16)
        w_lo = (w - w_hi.astype(jnp.float32)).astype(jnp.bfloat16)
        top = _dot(jnp.concatenate([s_hi, s_mid, s_lo], axis=0), w_hi)
        low = _dot(jnp.concatenate([s_hi, s_mid], axis=0), w_lo)
        part = part + (top[:MOD_ROWS] + top[MOD_ROWS:2 * MOD_ROWS] + top[2 * MOD_ROWS:]
                       + low[:MOD_ROWS] + low[MOD_ROWS:])
    part = part[:SUBLANES]

    @pl.when(k == 0)
    def _():
        o_ref[0] = part + b_ref[pl.ds(pl.program_id(0), 1), :]

    @pl.when(k > 0)
    def _():
        o_ref[0] += part


def _modulation(c_ctx, c, w_mod, b_mod, w_in_t, w_uq, w_ukv):
    n = 6 * D_MODEL
    nk = D_MODEL // MOD_TK
    chunk = D_MODEL // (DEPTH * nk)
    assert w_in_t.shape[0] == 1 and w_uq.shape[0] == 1 and w_ukv.shape[0] == 1
    whole = lambda shape: pl.BlockSpec(shape, lambda l, k: (0,) * len(shape))
    uq_shape = (Q_RANK, MLA_HEADS * HEAD_SLAB)
    wvt_shape = (MLA_HEADS * V_DIM, KV_RANK)
    assert Q_RANK == KV_RANK
    w_spec = lambda j: pl.BlockSpec((1, MOD_TK // MOD_STREAMS, n), lambda l, k: (l, k * MOD_STREAMS + j, 0))
    return pl.pallas_call(
        _mod_kernel,
        grid=(DEPTH, D_MODEL // MOD_TK),
        in_specs=[pl.BlockSpec((1, MOD_TK), lambda l, k: (0, k)),
                  pl.BlockSpec((c.shape[0], MOD_TK), lambda l, k: (0, k))]
        + [w_spec(j) for j in range(MOD_STREAMS)]
        + [pl.BlockSpec((DEPTH, n), lambda l, k: (0, 0)),
           pl.BlockSpec((1, w_in_t.shape[1], chunk), lambda l, k: (0, 0, l * nk + k)),
           whole(w_uq.shape), whole(w_ukv.shape)],
        out_specs=[pl.BlockSpec((1, SUBLANES, n), lambda l, k: (l, 0, 0)),
                   pl.BlockSpec((chunk, IN_COLS), lambda l, k: (l * nk + k, 0)),
                   whole(uq_shape), whole(uq_shape), whole(wvt_shape)],
        out_shape=[jax.ShapeDtypeStruct((DEPTH, SUBLANES, n), jnp.float32),
                   jax.ShapeDtypeStruct((D_MODEL, IN_COLS), jnp.bfloat16),
                   jax.ShapeDtypeStruct(uq_shape, jnp.bfloat16),
                   jax.ShapeDtypeStruct(uq_shape, jnp.bfloat16),
                   jax.ShapeDtypeStruct(wvt_shape, jnp.bfloat16)],
        compiler_params=pltpu.CompilerParams(dimension_semantics=("arbitrary", "arbitrary"),
                                             vmem_limit_bytes=VMEM_LIMIT),
        name="modulation",
    )(c_ctx.reshape(1, D_MODEL), c, *([w_mod] * MOD_STREAMS), b_mod, w_in_t, w_uq, w_ukv)


def _ssd_head_terms(dt_raw, dtb_c, a_neg_c):
    nh2 = 2 * SSD_HEADS
    row = lax.broadcasted_iota(jnp.int32, (CHUNK, CHUNK), 0)
    col = lax.broadcasted_iota(jnp.int32, (CHUNK, CHUNK), 1)
    upper_b = (row <= col).astype(jnp.bfloat16)
    lower_b = (row >= col).astype(jnp.bfloat16)
    fwd_rows = lax.broadcasted_iota(jnp.int32, (nh2, CHUNK), 0) < SSD_HEADS
    dts_t = _softplus(dt_raw.T[:nh2, :] + dtb_c)
    pieces = _split3(dts_t * a_neg_c)
    cum_t = jnp.where(fwd_rows, sum(_dot(p, upper_b) for p in pieces), sum(_dot(p, lower_b) for p in pieces))
    cum_t = cum_t * np.float32(np.log2(np.e))
    tot = jnp.where(fwd_rows[:, :1], cum_t[:, CHUNK - 1:], cum_t[:, :1])
    return dts_t, cum_t, dts_t * jnp.exp2(tot - cum_t)


def _inproj_kernel(xp_ref, xs_ref, mod_ref, cos_ref, sin_ref, w_in_ref, w_uq_ref, qn_ref, kvn_ref, npm_ref,
                   dtb_ref, alog_ref, q_ref, ckv_ref, kpe_ref, z_ref, xbc_ref, dts_ref, cumt_ref, wt_ref,
                   new_ckv_ref, new_kpe_ref):
    i = pl.program_id(0)
    sh = _mod_part(mod_ref, 0)
    sc = _mod_part(mod_ref, 1)
    scale = (NOPE_DIM + ROPE_DIM) ** -0.5 * np.log2(np.e)
    nh2 = 2 * SSD_HEADS
    diag = (lax.broadcasted_iota(jnp.int32, (nh2, LANES), 0) == lax.broadcasted_iota(jnp.int32, (nh2, LANES), 1))
    to_col = lambda ref: jnp.sum(jnp.where(diag, ref[:, :LANES], 0.0), axis=1, keepdims=True)
    dtb_c = to_col(dtb_ref)
    a_neg_c = -jnp.exp(to_col(alog_ref))
    dt_raw = []
    for r in range(TM // SUB_ROWS):
        rs = slice(r * SUB_ROWS, (r + 1) * SUB_ROWS)
        h = (_rms(_pick_group(i, xp_ref, xs_ref, rs), npm_ref[...]) * (1.0 + sc) + sh).astype(jnp.bfloat16)
        p = _dot(h, w_in_ref[...])
        o = 0
        cq = p[:, o:o + Q_RANK]; o += Q_RANK
        ckv = p[:, o:o + KV_RANK]; o += KV_RANK
        z_ref[rs, :] = p[:, o:o + D_SSD]; o += D_SSD
        xbc_ref[rs, :] = p[:, o:o + CONV_CH]; o += CONV_CH
        kpe = p[:, o:o + LANES]; o += LANES
        dt_raw.append(p[:, o:o + LANES])

        cos = cos_ref[rs, :]
        sin = sin_ref[rs, :]

        def rope(slab):
            return slab * cos + pltpu.roll(slab, LANES - ROPE_DIM, 1) * sin

        ckv_ref[rs, :] = _rms(ckv, kvn_ref[:, :KV_RANK])
        kpe_ref[rs, :] = rope(kpe)
        q = _dot(_rms(cq, qn_ref[:, :Q_RANK]).astype(jnp.bfloat16), w_uq_ref[...]) * scale
        for hd in range(MLA_HEADS):
            sl = slice(hd * HEAD_SLAB, (hd + 1) * HEAD_SLAB)
            q_ref[rs, sl] = rope(q[:, sl]).astype(jnp.bfloat16)

    dt_all = jnp.concatenate(dt_raw, axis=0)
    for ck in range(TM // CHUNK):
        terms = _ssd_head_terms(dt_all[ck * CHUNK:(ck + 1) * CHUNK, :], dtb_c, a_neg_c)
        for ref, val in zip((dts_ref, cumt_ref, wt_ref), terms):
            ref[ck * nh2:(ck + 1) * nh2, :] = val

    @pl.when(i < N_PT)
    def _():
        new_ckv_ref[...] = ckv_ref[...]
        kpe_t = kpe_ref[...].T
        for b in range(TM // SEQ):
            new_kpe_ref[b] = kpe_t[KPE_LANE0:KPE_LANE0 + ROPE_DIM, b * SEQ:(b + 1) * SEQ]


def _inproj(l, vec_table, vec_rows, xp, xs, mod, tabs, w_in, w_uq, q_norm, kv_norm):
    nt = N_TILES
    prompt_blk = lambda i: (jnp.minimum(i, N_PT - 1), 0)
    lat_tiles = DEC_SEQ // TM
    tab_spec = pl.BlockSpec((TM, LANES), lambda i: (jnp.where(i < N_PT, lat_tiles, (i - N_PT) % lat_tiles), 0))
    row = _row_spec
    hrows = TM // CHUNK * 2 * SSD_HEADS
    head_spec = pl.BlockSpec((hrows, CHUNK), lambda i: (i, 0))
    head_shape = jax.ShapeDtypeStruct((nt * hrows, CHUNK), jnp.float32)
    return pl.pallas_call(
        _inproj_kernel,
        grid=(nt,),
        in_specs=_group_specs(D_MODEL) + [
            _mod_spec(l),
            tab_spec, tab_spec,
            _const_spec((D_MODEL, IN_COLS)),
            _const_spec((Q_RANK, MLA_HEADS * HEAD_SLAB)),
            _narrow_spec(q_norm, l // 2),
            _narrow_spec(kv_norm, l // 2),
            _vec_spec(vec_rows["norm_pre_mix"] + l),
            _vec_spec(vec_rows["ssd_dt_bias"] + l // 2),
            _vec_spec(vec_rows["ssd_a_log"] + l // 2),
        ],
        out_specs=[row(MLA_HEADS * HEAD_SLAB), row(KV_RANK), row(LANES), row(D_SSD), row(CONV_CH),
                   head_spec, head_spec, head_spec,
                   pl.BlockSpec((TM, KV_RANK), prompt_blk),
                   pl.BlockSpec((TM // SEQ, ROPE_DIM, SEQ), lambda i: (jnp.minimum(i, N_PT - 1), 0, 0))],
        out_shape=[
            jax.ShapeDtypeStruct((N_TOK, MLA_HEADS * HEAD_SLAB), jnp.bfloat16),
            jax.ShapeDtypeStruct((N_TOK, KV_RANK), jnp.float32),
            jax.ShapeDtypeStruct((N_TOK, LANES), jnp.float32),
            jax.ShapeDtypeStruct((N_TOK, D_SSD), jnp.float32),
            jax.ShapeDtypeStruct((N_TOK, CONV_CH), jnp.float32),
            head_shape, head_shape, head_shape,
            jax.ShapeDtypeStruct((N_PROMPT, KV_RANK), jnp.float32),
            jax.ShapeDtypeStruct((BATCH, ROPE_DIM, SEQ), jnp.float32),
        ],
        compiler_params=pltpu.CompilerParams(dimension_semantics=("arbitrary",), vmem_limit_bytes=VMEM_LIMIT),
        name="inproj",
    )(xp, xs, mod, *tabs, w_in, w_uq, q_norm, kv_norm, *([vec_table] * 3))


def _attn_kernel(*refs, lk_cache, lk_new, n_pack, n_cast):
    n_in = 7 if lk_cache else 5
    if lk_cache:
        q_ref, ckv_ref, kpe_ref, ckvc_ref, kpec_ref, wk_ref, wvt_ref = refs[:n_in]
    else:
        q_ref, ckv_ref, kpe_ref, wk_ref, wvt_ref = refs[:n_in]
    cast_src, o_ref = refs[n_in:n_in + n_cast], refs[n_in + n_cast]
    cast_dst = refs[n_in + n_cast + 1:n_in + 2 * n_cast + 1]
    k_scr, vt_scr = refs[n_in + 2 * n_cast + 1:]
    _cast_chunks(cast_src, cast_dst)
    lk = lk_cache + lk_new

    @pl.when(pl.program_id(1) == 0)
    def _expand_kv():
        def expand(ckv, kpe, r0):
            ckv_b = ckv.astype(jnp.bfloat16)
            kn = _dot(ckv_b, wk_ref[...])
            rows = slice(r0, r0 + ckv.shape[0])
            for hd in range(MLA_HEADS):
                k_scr[hd, rows, :] = (kn[:, hd * HEAD_SLAB:(hd + 1) * HEAD_SLAB] + kpe).astype(jnp.bfloat16)
            vt_scr[:, rows] = _dot_nt(wvt_ref[...], ckv_b).astype(jnp.bfloat16)

        step = 256
        for r0 in range(0, lk_cache, step):
            expand(ckvc_ref[0, r0:r0 + step, :], kpec_ref[0, r0:r0 + step, :], r0)
        for r0 in range(0, n_pack * lk_new, step):
            expand(ckv_ref[r0:r0 + step, :], kpe_ref[r0:r0 + step, :], lk_cache + r0)

    work = [(s, hd) for s in range(n_pack) for hd in range(MLA_HEADS)]
    scores = [_dot_nt(k_scr[hd, s * lk:(s + 1) * lk, :], q_ref[s * TQ:(s + 1) * TQ, hd * HEAD_SLAB:(hd + 1) * HEAD_SLAB])
              for s, hd in work]
    outs = []
    for (s, hd), s_t in zip(work, scores):
        p_t = jnp.exp2(s_t - jnp.max(s_t, axis=0, keepdims=True))
        den = jnp.sum(p_t, axis=0, keepdims=True)
        v_t = vt_scr[hd * V_DIM:(hd + 1) * V_DIM, s * lk:(s + 1) * lk]
        outs.append(_dot(v_t, p_t.astype(jnp.bfloat16)) / den)
    for s in range(n_pack):
        o_ref[s * TQ:(s + 1) * TQ, :] = jnp.concatenate(
            outs[s * MLA_HEADS:(s + 1) * MLA_HEADS], axis=0).T.astype(jnp.bfloat16)


def _attention(q, ckv_n, kpe, w_ukv, row_off, n_batch, seq, cache=None, n_pack=1, cast=()):
    nq = seq // TQ
    assert n_pack == 1 or (nq == 1 and cache is None and n_batch % n_pack == 0)
    n_batch //= n_pack
    lk_cache = 0 if cache is None else cache[0].shape[1]
    lk = n_pack * (lk_cache + seq)
    qblk = lambda b, qi: (row_off // (n_pack * TQ) + b * nq + qi, 0)
    sblk = lambda b, qi: (row_off // (n_pack * seq) + b, 0)
    in_specs = [
        pl.BlockSpec((n_pack * TQ, MLA_HEADS * HEAD_SLAB), qblk),
        pl.BlockSpec((n_pack * seq, KV_RANK), sblk),
        pl.BlockSpec((n_pack * seq, LANES), sblk),
    ]
    args = [q, ckv_n, kpe]
    if cache is not None:
        in_specs += [pl.BlockSpec((1, lk_cache, KV_RANK), lambda b, qi: (b, 0, 0)),
                     pl.BlockSpec((1, lk_cache, LANES), lambda b, qi: (b, 0, 0))]
        args += list(cache)
    in_specs += [_const_spec(w.shape) for w in w_ukv]
    args += list(w_ukv)
    assert not cast or n_batch * nq == N_CAST
    step = lambda b, qi: b * nq + qi
    in_specs += [_cast_in_spec(w, layer, step) for w, layer in cast]
    args += [w for w, _ in cast]
    cast_out = [_cast_out(w, step) for w, _ in cast]
    return pl.pallas_call(
        functools.partial(_attn_kernel, lk_cache=lk_cache, lk_new=seq, n_pack=n_pack, n_cast=len(cast)),
        grid=(n_batch, nq),
        in_specs=in_specs,
        out_specs=[pl.BlockSpec((n_pack * TQ, MLA_HEADS * V_DIM), lambda b, qi: (b * nq + qi, 0))]
        + [spec for spec, _ in cast_out],
        out_shape=[jax.ShapeDtypeStruct((n_batch * n_pack * seq, MLA_HEADS * V_DIM), jnp.bfloat16)]
        + [shape for _, shape in cast_out],
        scratch_shapes=[pltpu.VMEM((MLA_HEADS, lk, HEAD_SLAB), jnp.bfloat16),
                        pltpu.VMEM((MLA_HEADS * V_DIM, lk), jnp.bfloat16)],
        compiler_params=pltpu.CompilerParams(dimension_semantics=("arbitrary", "arbitrary"),
                                             vmem_limit_bytes=VMEM_LIMIT),
        name=f"attention_{seq}",
    )(*args)


def _ssd_kernel(*refs, seq, zero_init):
    if zero_init:
        (xbc_ref, dts_ref, cumt_ref, wt_ref, cw_ref, cb_ref, dsk_ref,
         y_ref, hf_ref, hb_ref, xs_scr, c_scr, bt_scr, cum_scr, stf_scr, stb_scr) = refs
    else:
        (xbc_ref, dts_ref, cumt_ref, wt_ref, h0f_ref, h0b_ref, cw_ref, cb_ref, dsk_ref,
         y_ref, hf_ref, hb_ref, xs_scr, c_scr, bt_scr, cum_scr, stf_scr, stb_scr) = refs
    nc = seq // CHUNK
    gs = SSD_GROUPS * SSD_STATE
    nh2 = 2 * SSD_HEADS

    row = lax.broadcasted_iota(jnp.int32, (CHUNK, CHUNK), 0)
    col = lax.broadcasted_iota(jnp.int32, (CHUNK, CHUNK), 1)
    low_half = col < SSD_HEAD_DIM
    lower = row >= col
    upper = row <= col

    def prep_chunk(c, carry):
        r0 = pl.multiple_of(c * CHUNK, CHUNK)
        rows = pl.ds(r0, CHUNK)
        rows_prev = pl.ds(pl.multiple_of(jnp.maximum(r0 - HALO, 0), HALO), HALO)
        rows_next = pl.ds(pl.multiple_of(jnp.minimum(r0 + CHUNK, seq - HALO), HALO), HALO)

        def conv_tile(cs):
            prev = jnp.where(c > 0, xbc_ref[rows_prev, cs], 0.0)
            nxt = jnp.where(c < nc - 1, xbc_ref[rows_next, cs], 0.0)
            win = jnp.concatenate([prev, xbc_ref[rows, cs], nxt], axis=0)
            acc = jnp.broadcast_to(cb_ref[:, cs], (CHUNK, LANES))
            for k in range(CONV_W):
                lo = HALO - CONV_W // 2 + k
                acc = acc + cw_ref[k:k + 1, cs] * win[lo:lo + CHUNK, :]
            return _silu(acc)

        def x_tile(j, carry):
            cs = pl.ds(pl.multiple_of(j * LANES, LANES), LANES)
            u = conv_tile(cs)
            y_ref[rows, cs] = dsk_ref[:, cs] * u
            xs_scr[0, rows, cs] = jnp.where(low_half, u, 0.0).astype(jnp.bfloat16)
            xs_scr[1, rows, cs] = jnp.where(low_half, 0.0, u).astype(jnp.bfloat16)
            return carry

        lax.fori_loop(0, D_SSD // LANES, x_tile, 0)
        cum_t = cumt_ref[pl.ds(pl.multiple_of(c * nh2, nh2), nh2), :]
        cum_scr[rows, :] = jnp.concatenate(
            [cum_t, jnp.zeros((CHUNK - nh2, CHUNK), jnp.float32)], axis=0).T[:, :nh2]
        for g in range(SSD_GROUPS):
            b0 = pl.multiple_of(c * gs + g * SSD_STATE, SSD_STATE)
            bt_scr[pl.ds(b0, SSD_STATE), :] = conv_tile(slice(D_SSD + g * SSD_STATE, D_SSD + (g + 1) * SSD_STATE)).T
            c_scr[rows, g * SSD_STATE:(g + 1) * SSD_STATE] = conv_tile(
                slice(D_SSD + gs + g * SSD_STATE, D_SSD + gs + (g + 1) * SSD_STATE)).astype(jnp.bfloat16)
        return carry

    lax.fori_loop(0, nc, prep_chunk, 0)

    if zero_init:
        stf_scr[...] = jnp.zeros_like(stf_scr)
        stb_scr[...] = jnp.zeros_like(stb_scr)
    else:
        stf_scr[...] = h0f_ref[0].T
        stb_scr[...] = h0b_ref[0].T

    def scan_open(ci, st_scr):
        rows = pl.ds(pl.multiple_of(ci * CHUNK, CHUNK), CHUNK)
        c_b = c_scr[rows, :]
        st = st_scr[...]
        bts, cbms, zs = [], [], []
        for g in range(SSD_GROUPS):
            cg = c_b[:, g * SSD_STATE:(g + 1) * SSD_STATE]
            bt = bt_scr[pl.ds(pl.multiple_of(ci * gs + g * SSD_STATE, SSD_STATE), SSD_STATE), :]
            gcols = slice(g * SSD_HPG * SSD_HEAD_DIM, (g + 1) * SSD_HPG * SSD_HEAD_DIM)
            bts.append(bt)
            cbms.append(_dot(cg, bt.astype(jnp.bfloat16)))
            zs.append(_dot(cg, st[:, gcols].astype(jnp.bfloat16)))
        return st, bts, cbms, zs

    def scan_pairs(ci, st_scr, reverse, opened):
        st, bts, cbms, zs = opened
        lane0 = SSD_HEADS if reverse else 0
        causal = upper if reverse else lower
        last = 0 if reverse else CHUNK - 1
        rows = pl.ds(pl.multiple_of(ci * CHUNK, CHUNK), CHUNK)
        hrows = pl.ds(pl.multiple_of(ci * nh2, nh2), nh2)
        xs_lo = xs_scr[0, rows, :]
        xs_hi = xs_scr[1, rows, :]
        dts_t = dts_ref[hrows, :]
        cum_t = cumt_ref[hrows, :]
        w_t = wt_ref[hrows, :]
        cum = cum_scr[rows, :]
        for pair in range(SSD_HEADS // 2):
            g, jj = divmod(pair, SSD_HPG // 2)
            pcols = slice(pair * LANES, (pair + 1) * LANES)
            lhs_y, lhs_s, entry = [], [], []
            for hd in (2 * pair, 2 * pair + 1):
                ln = lane0 + hd
                cum_i = jnp.broadcast_to(cum[:, ln:ln + 1], (CHUNK, CHUNK))
                dec = jnp.exp2(jnp.where(causal, cum_i - cum_t[ln:ln + 1, :], -jnp.inf))
                lhs_y.append((cbms[g] * dec * dts_t[ln:ln + 1, :]).astype(jnp.bfloat16))
                lhs_s.append((bts[g] * w_t[ln:ln + 1, :]).astype(jnp.bfloat16))
                entry.append(jnp.exp2(cum_i))
            lhs = jnp.concatenate([jnp.concatenate(lhs_y, axis=1), jnp.concatenate(lhs_s, axis=1)], axis=0)
            out = _dot(lhs, jnp.concatenate([xs_lo[:, pcols], xs_hi[:, pcols]], axis=0))
            ea = jnp.where(low_half, entry[0], entry[1])
            y_ref[rows, pcols] += out[:CHUNK] + zs[g][:, jj * LANES:(jj + 1) * LANES] * ea
            st_scr[:, pcols] = ea[last:last + 1, :] * st[:, pcols] + out[CHUNK:]

    def both(c, carry):
        opened_f = scan_open(c, stf_scr)
        opened_b = scan_open(nc - 1 - c, stb_scr)
        scan_pairs(c, stf_scr, False, opened_f)
        scan_pairs(nc - 1 - c, stb_scr, True, opened_b)
        return carry

    lax.fori_loop(0, nc, both, 0)
    hf_ref[0] = stf_scr[...].T
    hb_ref[0] = stb_scr[...].T


def _ssd(i_ab, xbc, head_terms, h0, conv_w, vec_table, conv_b_row, d_skip, row_off, n_batch, seq):
    assert CONV_CH == D_MODEL
    hp = SSD_HEADS * SSD_HEAD_DIM
    gs = SSD_GROUPS * SSD_STATE
    nc = seq // CHUNK
    seq_blk = lambda b: (row_off // seq + b, 0)
    head_spec = pl.BlockSpec((nc * 2 * SSD_HEADS, CHUNK), seq_blk)
    st_spec = pl.BlockSpec((1, hp, SSD_STATE), lambda b: (b, 0, 0))
    st_shape = jax.ShapeDtypeStruct((n_batch, hp, SSD_STATE), jnp.float32)
    h0 = () if h0 is None else tuple(h0)
    return pl.pallas_call(
        functools.partial(_ssd_kernel, seq=seq, zero_init=not h0),
        grid=(n_batch,),
        in_specs=[pl.BlockSpec((seq, CONV_CH), seq_blk), head_spec, head_spec, head_spec] + [st_spec] * len(h0) + [
            pl.BlockSpec((None, CONV_W, CONV_CH), lambda b: (i_ab, 0, 0), pipeline_mode=pl.Buffered(1)),
            _vec_spec(conv_b_row + i_ab), _const_spec((1, D_SSD))],
        out_specs=[pl.BlockSpec((seq, D_SSD), lambda b: (b, 0)), st_spec, st_spec],
        out_shape=[jax.ShapeDtypeStruct((n_batch * seq, D_SSD), jnp.float32), st_shape, st_shape],
        scratch_shapes=[pltpu.VMEM((2, seq, D_SSD), jnp.bfloat16),
                        pltpu.VMEM((seq, gs), jnp.bfloat16),
                        pltpu.VMEM((nc * gs, CHUNK), jnp.float32),
                        pltpu.VMEM((seq, 2 * SSD_HEADS), jnp.float32),
                        pltpu.VMEM((SSD_STATE, hp), jnp.float32),
                        pltpu.VMEM((SSD_STATE, hp), jnp.float32)],
        compiler_params=pltpu.CompilerParams(dimension_semantics=("arbitrary",), vmem_limit_bytes=VMEM_LIMIT),
        name=f"ssd_{seq}",
    )(xbc, *head_terms, *h0, conv_w, vec_table, d_skip)


def _post_mix(x, mix, mod_ref, npost_ref, npre_ref):
    d = D_MODEL
    gate_mix = _mod_part(mod_ref, 2)
    shf = _mod_part(mod_ref, 3)
    scf = _mod_part(mod_ref, 4)
    x1 = x + gate_mix * _rms(mix, npost_ref[...])
    return x1, (_rms(x1, npre_ref[...]) * (1.0 + scf) + shf).astype(jnp.bfloat16)


def _subtile_pipeline(n_sub, mixer_pre, mixer_dots, mod_ref, nffn_ref, wg_ref, wu_ref, wd_ref, interleave):
    gate_ffn = _mod_part(mod_ref, 5)
    ffn_up = lambda h: (_silu(_dot(h, wg_ref[...])) * _dot(h, wu_ref[...])).astype(jnp.bfloat16)
    ffn_down = lambda x1, hid: x1 + gate_ffn * _rms(_dot(hid, wd_ref[...]), nffn_ref[...])
    if not interleave:
        staged = [mixer_dots(r, mixer_pre(r)) for r in range(n_sub)]
        return [ffn_down(x1, ffn_up(h)) for x1, h in staged]
    outs = []
    x1, h = mixer_dots(0, mixer_pre(0))
    for r in range(n_sub):
        nxt_pre = mixer_pre(r + 1) if r + 1 < n_sub else None
        hid = ffn_up(h)
        nxt = mixer_dots(r + 1, nxt_pre) if r + 1 < n_sub else None
        outs.append(ffn_down(x1, hid))
        if nxt is not None:
            x1, h = nxt
    return outs


def _ffn_specs(l, vec_rows):
    return [_vec_spec(vec_rows[name] + l) for name in ("norm_post_mix", "norm_pre_ffn", "norm_post_ffn")] + [
        _const_spec((D_MODEL, D_FF)), _const_spec((D_MODEL, D_FF)), _const_spec((D_FF, D_MODEL))]


def _outproj_ffn_kernel(*refs, n_cast):
    (xp_ref, xs_ref, attp_ref, atts_ref, yp_ref, ys_ref, z_ref, mod_ref, sn_ref, wo_ref,
     npost_ref, npre_ref, nffn_ref, wg_ref, wu_ref, wd_ref) = refs[:16]
    cast_src, o_ref, cast_dst = refs[16:16 + n_cast], refs[16 + n_cast], refs[17 + n_cast:]
    i = pl.program_id(0)
    _cast_chunks(cast_src, cast_dst)
    gw = D_SSD // SSD_GROUPS
    sub = lambda r: slice(r * SUB_ROWS, (r + 1) * SUB_ROWS)

    def mixer_pre(r):
        yg = _pick_group(i, yp_ref, ys_ref, sub(r)) * _silu(z_ref[sub(r), :])
        parts = [_pick_group(i, attp_ref, atts_ref, sub(r))]
        for g in range(SSD_GROUPS):
            parts.append(_rms(yg[:, g * gw:(g + 1) * gw], sn_ref[:, g * gw:(g + 1) * gw]).astype(jnp.bfloat16))
        return jnp.concatenate(parts, axis=1)

    def mixer_dots(r, cat):
        return _post_mix(_pick_group(i, xp_ref, xs_ref, sub(r)), _dot(cat, wo_ref[...]),
                         mod_ref, npost_ref, npre_ref)

    outs = _subtile_pipeline(TM // SUB_ROWS, mixer_pre, mixer_dots, mod_ref, nffn_ref, wg_ref, wu_ref, wd_ref,
                             interleave=False)
    for r, res in enumerate(outs):
        o_ref[sub(r), :] = res


def _outproj_ffn(l, i_ab, vec_table, vec_rows, ssd_norm, xp, xs, att_p, att_s, y_p, y_s, z, mod, w_out, wg, wu, wd,
                 cast):
    d_cat = MLA_HEADS * V_DIM + D_SSD
    assert not cast or N_TILES == N_CAST
    cast_out = [_cast_out(w) for w, _ in cast]
    return pl.pallas_call(
        functools.partial(_outproj_ffn_kernel, n_cast=len(cast)),
        grid=(N_TILES,),
        in_specs=(_group_specs(D_MODEL) + _group_specs(MLA_HEADS * V_DIM) + _group_specs(D_SSD)
                  + [_row_spec(D_SSD), _mod_spec(l), _narrow_spec(ssd_norm, i_ab),
                     _const_spec((d_cat, D_MODEL))] + _ffn_specs(l, vec_rows)
                  + [_cast_in_spec(w, layer) for w, layer in cast]),
        out_specs=[_row_spec(D_MODEL)] + [spec for spec, _ in cast_out],
        out_shape=[jax.ShapeDtypeStruct((N_TOK, D_MODEL), jnp.float32)] + [shape for _, shape in cast_out],
        compiler_params=pltpu.CompilerParams(dimension_semantics=("arbitrary",), vmem_limit_bytes=VMEM_LIMIT),
        name="outproj_ffn",
    )(xp, xs, att_p, att_s, y_p, y_s, z, mod, ssd_norm, w_out, vec_table, vec_table, vec_table, wg, wu, wd,
      *[w for w, _ in cast])


def _pool_ffn_kernel(x_ref, xp_ref, xn_ref, mod_ref, nmix_ref, pw_ref, ps_ref,
                     npost_ref, npre_ref, nffn_ref, wg_ref, wu_ref, wd_ref, op_ref, os_ref):
    i = pl.program_id(0)

    @pl.when(i == 0)
    def _():
        os_ref[...] = jnp.zeros_like(os_ref)

    seq = jnp.where(i < N_PROMPT // TM, SEQ, DEC_SEQ)
    pos0 = (i * TM) % seq
    sh = _mod_part(mod_ref, 0)
    sc = _mod_part(mod_ref, 1)
    hmod = lambda v: _rms(v, nmix_ref[...]) * (1.0 + sc) + sh
    n_rows = POOL_SUB + 2 * HALO

    def shifted(v, s):
        return pltpu.roll(v, n_rows - s, 0)

    def mixer_pre(s):
        lo, hi = s * POOL_SUB, (s + 1) * POOL_SUB
        pos_s = (pos0 + lo) % seq
        h = hmod(x_ref[lo:hi, :])
        before = hmod(xp_ref[...] if s == 0 else x_ref[lo - HALO:lo, :])
        after = hmod(xn_ref[...] if hi == TM else x_ref[hi:hi + HALO, :])
        before = jnp.where(pos_s > 0, before, 0.0)
        after = jnp.where(pos_s + POOL_SUB < seq, after, 0.0)
        padded = jnp.concatenate([before, h, after], axis=0)
        pos = pos_s + lax.broadcasted_iota(jnp.int32, (POOL_SUB, 1), 0)
        pooled = []
        for gi, w in enumerate(POOL_WINDOWS):
            cols = slice(gi * POOL_GC, (gi + 1) * POOL_GC)
            t = padded[:, cols]
            span = 1
            while span < w:
                t = t + shifted(t, span)
                span *= 2
            lead = HALO - w // 2
            win_sum = (shifted(t, lead) if lead else t)[:POOL_SUB, :]
            cnt = (jnp.minimum(pos + w // 2, seq) - jnp.maximum(pos - w // 2, 0)).astype(jnp.float32)
            pooled.append((win_sum / cnt - h[:, cols]).astype(jnp.bfloat16))
        return pooled

    def mixer_dots(s, pooled):
        mix = jnp.concatenate([_dot(p, pw_ref[gi]) for gi, p in enumerate(pooled)], axis=1) * ps_ref[...]
        return _post_mix(x_ref[s * POOL_SUB:(s + 1) * POOL_SUB, :], mix, mod_ref, npost_ref, npre_ref)

    res = jnp.concatenate(_subtile_pipeline(TM // POOL_SUB, mixer_pre, mixer_dots, mod_ref, nffn_ref,
                                            wg_ref, wu_ref, wd_ref, interleave=True), axis=0)

    @pl.when(i < N_PT)
    def _():
        op_ref[...] = res

    @pl.when(i >= N_PT)
    def _():
        os_ref[...] = res


def _pool_ffn(l, j_c, vec_table, vec_rows, xa, mod, pool_w, wg, wu, wd):
    hb = TM // HALO
    nh = N_TOK // HALO
    return pl.pallas_call(
        _pool_ffn_kernel,
        grid=(N_TILES,),
        in_specs=[_row_spec(D_MODEL),
                  pl.BlockSpec((HALO, D_MODEL), lambda i: (jnp.maximum(i * hb - 1, 0), 0)),
                  pl.BlockSpec((HALO, D_MODEL), lambda i: (jnp.minimum((i + 1) * hb, nh - 1), 0)),
                  _mod_spec(l),
                  _vec_spec(vec_rows["norm_pre_mix"] + l),
                  _const_spec((len(POOL_WINDOWS), POOL_GC, POOL_GC)),
                  _vec_spec(vec_rows["pool_scale"] + j_c)] + _ffn_specs(l, vec_rows),
        out_specs=_group_specs(D_MODEL),
        out_shape=[jax.ShapeDtypeStruct((N_PROMPT, D_MODEL), jnp.float32),
                   jax.ShapeDtypeStruct((N_SAMPLE, D_MODEL), jnp.float32)],
        compiler_params=pltpu.CompilerParams(dimension_semantics=("arbitrary",), vmem_limit_bytes=VMEM_LIMIT),
        name="pool_ffn",
    )(xa, xa, xa, mod, vec_table, pool_w, vec_table, vec_table, vec_table, vec_table, wg, wu, wd)


def _rope_tables():
    f32 = np.float32
    rows = DEC_SEQ // GRID_W
    r = np.repeat(np.arange(rows, dtype=f32), GRID_W)
    c = np.tile(np.arange(GRID_W, dtype=f32), rows)
    half = ROPE_DIM // 2
    inv_freq = np.power(f32(ROPE_THETA), -np.arange(0, half, 2, dtype=f32) / f32(half)).astype(f32)
    ang = np.concatenate([r[:, None] * inv_freq, c[:, None] * inv_freq], axis=-1).astype(f32)
    cos, sin = np.cos(ang).astype(f32), np.sin(ang).astype(f32)
    ones = np.ones((DEC_SEQ, KPE_LANE0), f32)
    zl = np.zeros((DEC_SEQ, KPE_LANE0), f32)
    zr = np.zeros((DEC_SEQ, LANES - KPE_LANE0 - ROPE_DIM), f32)
    cos_t = np.concatenate([ones, cos, cos, zr], axis=1)
    sin_t = np.concatenate([zl, -sin, sin, zr], axis=1)
    keep = np.concatenate([np.ones((TM, KPE_LANE0 + ROPE_DIM), f32), np.zeros((TM, LANES - KPE_LANE0 - ROPE_DIM), f32)],
                          axis=1)
    return (jnp.asarray(np.concatenate([cos_t, keep], axis=0)),
            jnp.asarray(np.concatenate([sin_t, np.zeros((TM, LANES), f32)], axis=0)))


def _kpe_slab(k):
    pad = [(0, 0)] * (k.ndim - 1) + [(KPE_LANE0, LANES - KPE_LANE0 - ROPE_DIM)]
    return jnp.pad(k, pad)


def _layout_in_proj(wt):
    o = np.cumsum((0, Q_RANK, KV_RANK, ROPE_DIM, D_SSD, CONV_CH, SSD_HEADS, SSD_HEADS))
    cq, ckv, kpe, z, xbc, dtf, dtb = (wt[o[k]:o[k + 1], :] for k in range(7))
    zeros = lambda n: jnp.zeros((n, wt.shape[1]), wt.dtype)
    half = ROPE_DIM // 2
    assert KPE_LANE0 + 2 * ROPE_DIM == LANES
    stacked = jnp.concatenate([cq, ckv, z, xbc,
                               zeros(KPE_LANE0), kpe, kpe[half:], kpe[:half],
                               dtf, dtb, zeros(LANES - 2 * SSD_HEADS)], axis=0)
    return stacked.T.astype(jnp.bfloat16)


def _layout_uq(w):
    hw, half = NOPE_DIM + ROPE_DIM, ROPE_DIM // 2
    assert hw + ROPE_DIM == HEAD_SLAB
    pieces = []
    for hd in range(MLA_HEADS):
        s = w[:, hd * hw:(hd + 1) * hw]
        pieces += [s, s[:, NOPE_DIM + half:], s[:, NOPE_DIM:NOPE_DIM + half]]
    return jnp.concatenate(pieces, axis=1).astype(jnp.bfloat16)


def _layout_ukv(w):
    hw = NOPE_DIM + V_DIM
    pad = jnp.zeros((w.shape[0], HEAD_SLAB - NOPE_DIM), w.dtype)
    kn, v = [], []
    for hd in range(MLA_HEADS):
        kn += [w[:, hd * hw:hd * hw + NOPE_DIM], pad]
        v.append(w[:, hd * hw + NOPE_DIM:(hd + 1) * hw])
    return (jnp.concatenate(kn, axis=1).astype(jnp.bfloat16),
            jnp.concatenate(v, axis=1).T.astype(jnp.bfloat16))


def kernel(x_prompt, x_sample, c, cache_mla_ckv, cache_mla_krope, state_ssd_fwd, state_ssd_bwd, c_ctx, w_mod, b_mod, norm_pre_mix, norm_post_mix, norm_pre_ffn, norm_post_ffn, w_in_ab, q_norm, w_uq, kv_norm, w_ukv, ssd_conv_w, ssd_conv_b, ssd_dt_bias_fwd, ssd_dt_bias_bwd, ssd_a_log_fwd, ssd_a_log_bwd, ssd_d, ssd_norm, w_out_ab, pool_w, pool_scale, ffn_w_gate, ffn_w_up, ffn_w_down):
    f32, bf16 = jnp.float32, jnp.bfloat16
    assert DEPTH == 2
    xp = x_prompt.reshape(N_PROMPT, D_MODEL)
    xs = x_sample.reshape(N_SAMPLE, D_MODEL)
    assert 1 + c.shape[0] == N_MODVEC
    mod, w_in, w_uq_slabs, *w_kv = _modulation(c_ctx, c, w_mod, b_mod, jnp.swapaxes(w_in_ab, 1, 2), w_uq, w_ukv)
    tabs = _rope_tables()
    hp = SSD_HEADS * SSD_HEAD_DIM
    row = lambda v: v.reshape(1, -1)
    new_ckv, new_kpe, new_hf, new_hb = [], [], [], []

    vec_table, vec_rows = _pack_vectors(
        norm_pre_mix=norm_pre_mix, norm_post_mix=norm_post_mix, norm_pre_ffn=norm_pre_ffn, norm_post_ffn=norm_post_ffn,
        ssd_conv_b=ssd_conv_b, pool_scale=pool_scale,
        ssd_dt_bias=jnp.concatenate([ssd_dt_bias_fwd, ssd_dt_bias_bwd], axis=1),
        ssd_a_log=jnp.concatenate([ssd_a_log_fwd, ssd_a_log_bwd], axis=1))
    ffn_w_f32 = (ffn_w_gate, ffn_w_up, ffn_w_down)
    for l in range(DEPTH):
        if l % 2 == 0:
            i = l // 2
            q, ckv_n, kpe, z, xbc, dts_t, cum_t, w_t, ckv_prompt, kpe_prompt = _inproj(
                l, vec_table, vec_rows, xp, xs, mod, tabs, w_in, w_uq_slabs, q_norm, kv_norm)
            heads = (dts_t, cum_t, w_t)
            (att_p,) = _attention(q, ckv_n, kpe, w_kv, 0, BATCH, SEQ, n_pack=PROMPT_PACK)
            att_s, w_out, *ffn_w = _attention(q, ckv_n, kpe, w_kv, N_PROMPT, DEC_BATCH, DEC_SEQ,
                                              cache=(cache_mla_ckv[:, i], _kpe_slab(cache_mla_krope[:, i])),
                                              cast=[(w_out_ab, i)] + [(w, l) for w in ffn_w_f32])
            ssd_args = (ssd_conv_w, vec_table, vec_rows["ssd_conv_b"], row(jnp.repeat(ssd_d[i], SSD_HEAD_DIM)))
            y_p, hf, hb = _ssd(i, xbc, heads, None, *ssd_args, 0, BATCH, SEQ)
            y_s, _, _ = _ssd(i, xbc, heads, (state_ssd_fwd[:, i].reshape(DEC_BATCH, hp, SSD_STATE),
                                             state_ssd_bwd[:, i].reshape(DEC_BATCH, hp, SSD_STATE)),
                             *ssd_args, N_PROMPT, DEC_BATCH, DEC_SEQ)
            xa, *ffn_w = _outproj_ffn(l, i, vec_table, vec_rows, ssd_norm, xp, xs, att_p, att_s, y_p, y_s, z, mod, w_out, *ffn_w,
                                      cast=[(w, l + 1) for w in ffn_w_f32])
            new_ckv.append(ckv_prompt.reshape(BATCH, SEQ, KV_RANK))
            new_kpe.append(jnp.swapaxes(kpe_prompt, 1, 2))
            new_hf.append(hf.reshape(BATCH, SSD_HEADS, SSD_HEAD_DIM, SSD_STATE))
            new_hb.append(hb.reshape(BATCH, SSD_HEADS, SSD_HEAD_DIM, SSD_STATE))
        else:
            j = l // 2
            yp, ys = _pool_ffn(l, j, vec_table, vec_rows, xa, mod, pool_w[j].astype(bf16), *ffn_w)

    return (yp.reshape(BATCH, SEQ, D_MODEL), ys.reshape(DEC_BATCH, DEC_SEQ, D_MODEL),
            jnp.stack(new_ckv, axis=1), jnp.stack(new_kpe, axis=1),
            jnp.stack(new_hf, axis=1), jnp.stack(new_hb, axis=1))
```

```python
import functools

import numpy as np
import jax
import jax.numpy as jnp
from jax import lax
from jax.experimental import pallas as pl
from jax.experimental.pallas import tpu as pltpu

D_MODEL = 1024
BATCH = 16
SEQ = 256
DEPTH = 2
DEC_BATCH = 2
DEC_SEQ = 2048
PAST_LEN = 256
GRID_W = 64
EPS = 1e-6
MLA_HEADS = 8
Q_RANK = 256
KV_RANK = 256
NOPE_DIM = 64
ROPE_DIM = 32
V_DIM = 64
ROPE_THETA = 10000.0
SSD_HEADS = 8
SSD_GROUPS = 2
SSD_HPG = SSD_HEADS // SSD_GROUPS
SSD_HEAD_DIM = 64
SSD_STATE = 128
D_SSD = SSD_HEADS * SSD_HEAD_DIM
CONV_W = 5
CONV_CH = D_SSD + 2 * SSD_GROUPS * SSD_STATE
POOL_WINDOWS = (2, 4, 8, 16)
POOL_GC = D_MODEL // len(POOL_WINDOWS)
D_FF = ((8 * D_MODEL + 3 * 256 - 1) // (3 * 256)) * 256

SUBLANES = 8
LANES = 128

N_PROMPT = BATCH * SEQ
N_SAMPLE = DEC_BATCH * DEC_SEQ
N_TOK = N_PROMPT + N_SAMPLE
N_MODVEC = 1 + DEC_BATCH
TM = 512
TQ = 256
CHUNK = 128
HALO = SUBLANES
HEAD_SLAB = LANES
IN_COLS = Q_RANK + KV_RANK + D_SSD + CONV_CH + 2 * LANES
KPE_LANE0 = NOPE_DIM
VMEM_LIMIT = 56 * 1024 * 1024

POOL_SUB = min(SEQ, DEC_SEQ)

assert TM % POOL_SUB == 0 and SEQ % POOL_SUB == 0 and DEC_SEQ % TM == 0 and N_PROMPT % DEC_SEQ == 0


def _rms(x, g):
    return x * lax.rsqrt(jnp.mean(x * x, axis=-1, keepdims=True) + EPS) * g


def _silu(x):
    return x * jax.nn.sigmoid(x)


def _softplus(x):
    return jnp.maximum(x, 0.0) + jnp.log1p(jnp.exp(-jnp.abs(x)))


def _dot(a, b):
    return jnp.dot(a, b, preferred_element_type=jnp.float32)


def _dot_nt(a, b):
    return lax.dot_general(a, b, (((1,), (1,)), ((), ())), preferred_element_type=jnp.float32)


def _mod_row(i):
    return jnp.where(i < N_PROMPT // TM, 0, 1 + (i - N_PROMPT // TM) // (DEC_SEQ // TM))


def _const_spec(shape):
    nd = len(shape)
    return pl.BlockSpec(shape, lambda *_: (0,) * nd, pipeline_mode=pl.Buffered(1))


N_PT = N_PROMPT // TM
N_TILES = N_TOK // TM


def _row_spec(width):
    return pl.BlockSpec((TM, width), lambda i: (i, 0))


def _mod_spec(l):
    return pl.BlockSpec((None, SUBLANES, 6 * D_MODEL), lambda i: (l, 0, 0), pipeline_mode=pl.Buffered(1))


def _mod_part(mod_ref, k):
    return mod_ref[pl.ds(_mod_row(pl.program_id(0)), 1), k * D_MODEL:(k + 1) * D_MODEL]


def _vec_spec(row):
    return pl.BlockSpec((None, 1, D_MODEL), lambda *_: (row, 0, 0), pipeline_mode=pl.Buffered(1))


def _narrow_spec(a, layer):
    assert a.shape[0] == 1
    return pl.BlockSpec((1, a.shape[1]), lambda *_: (layer, 0), pipeline_mode=pl.Buffered(1))


def _pack_vectors(**params):
    first_row, blocks, n = {}, [], 0
    for name, a in params.items():
        first_row[name] = n
        n += a.shape[0]
        blocks.append(jnp.pad(a, ((0, 0), (0, D_MODEL - a.shape[1]))))
    return jnp.concatenate(blocks, axis=0).reshape(n, 1, D_MODEL), first_row


def _group_specs(width):
    return [pl.BlockSpec((TM, width), lambda i: (jnp.minimum(i, N_PT - 1), 0)),
            pl.BlockSpec((TM, width), lambda i: (jnp.maximum(i - N_PT, 0), 0))]


N_CAST = 16


def _cast_in_spec(w, layer, step=lambda i: i):
    _, rows, cols = w.shape
    return pl.BlockSpec((1, rows // N_CAST, cols), lambda *g: (layer, step(*g), 0))


def _cast_out(w, step=lambda i: i):
    _, rows, cols = w.shape
    return (pl.BlockSpec((rows // N_CAST, cols), lambda *g: (step(*g), 0)),
            jax.ShapeDtypeStruct((rows, cols), jnp.bfloat16))


def _cast_chunks(src_refs, dst_refs):
    for src, dst in zip(src_refs, dst_refs):
        dst[...] = src[0].astype(jnp.bfloat16)


def _pick_group(i, p_ref, s_ref, rows=slice(None)):
    return jnp.where(i < N_PT, p_ref[rows, :], s_ref[rows, :])


MOD_TK = 256
MOD_STREAMS = 2
PROMPT_PACK = 4
MOD_ROWS = 2 * SUBLANES
SUB_ROWS = 256


def _split3(a):
    a_hi = a.astype(jnp.bfloat16)
    r1 = a - a_hi.astype(jnp.float32)
    a_mid = r1.astype(jnp.bfloat16)
    a_lo = (r1 - a_mid.astype(jnp.float32)).astype(jnp.bfloat16)
    return a_hi, a_mid, a_lo


def _mod_kernel(cctx_ref, c_ref, *refs):
    w_refs, (b_ref, win_ref, wuq_ref, wukv_ref, o_ref, wl_ref, uq_ref, wk_ref, wvt_ref) = (
        refs[:MOD_STREAMS], refs[MOD_STREAMS:])
    wl_ref[...] = _layout_in_proj(win_ref[0])
    k = pl.program_id(1)

    @pl.when((pl.program_id(0) == 0) & (k == 0))
    def _():
        uq_ref[...] = _layout_uq(wuq_ref[0])
        wk_ref[...], wvt_ref[...] = _layout_ukv(wukv_ref[0])

    rows = MOD_TK // MOD_STREAMS
    rowid = lax.broadcasted_iota(jnp.int32, (MOD_ROWS, MOD_TK), 0)
    cv = jnp.where(rowid == 0, cctx_ref[...], 0.0)
    for b in range(c_ref.shape[0]):
        cv = jnp.where(rowid == 1 + b, c_ref[b:b + 1, :], cv)
    s_all = _silu(cv)
    part = jnp.zeros((MOD_ROWS, 6 * D_MODEL), jnp.float32)
    for j, w_ref in enumerate(w_refs):
        s_hi, s_mid, s_lo = _split3(s_all[:, j * rows:(j + 1) * rows])
        w = w_ref[0]
        w_hi = w.astype(jnp.bfloat16)
        w_lo = (w - w_hi.astype(jnp.float32)).astype(jnp.bfloat16)
        top = _dot(jnp.concatenate([s_hi, s_mid, s_lo], axis=0), w_hi)
        low = _dot(jnp.concatenate([s_hi, s_mid], axis=0), w_lo)
        part = part + (top[:MOD_ROWS] + top[MOD_ROWS:2 * MOD_ROWS] + top[2 * MOD_ROWS:]
                       + low[:MOD_ROWS] + low[MOD_ROWS:])
    part = part[:SUBLANES]

    @pl.when(k == 0)
    def _():
        o_ref[0] = part + b_ref[pl.ds(pl.program_id(0), 1), :]

    @pl.when(k > 0)
    def _():
        o_ref[0] += part


def _modulation(c_ctx, c, w_mod, b_mod, w_in_t, w_uq, w_ukv):
    n = 6 * D_MODEL
    nk = D_MODEL // MOD_TK
    chunk = D_MODEL // (DEPTH * nk)
    assert w_in_t.shape[0] == 1 and w_uq.shape[0] == 1 and w_ukv.shape[0] == 1
    whole = lambda shape: pl.BlockSpec(shape, lambda l, k: (0,) * len(shape))
    uq_shape = (Q_RANK, MLA_HEADS * HEAD_SLAB)
    wvt_shape = (MLA_HEADS * V_DIM, KV_RANK)
    assert Q_RANK == KV_RANK
    w_spec = lambda j: pl.BlockSpec((1, MOD_TK // MOD_STREAMS, n), lambda l, k: (l, k * MOD_STREAMS + j, 0))
    return pl.pallas_call(
        _mod_kernel,
        grid=(DEPTH, D_MODEL // MOD_TK),
        in_specs=[pl.BlockSpec((1, MOD_TK), lambda l, k: (0, k)),
                  pl.BlockSpec((c.shape[0], MOD_TK), lambda l, k: (0, k))]
        + [w_spec(j) for j in range(MOD_STREAMS)]
        + [pl.BlockSpec((DEPTH, n), lambda l, k: (0, 0)),
           pl.BlockSpec((1, w_in_t.shape[1], chunk), lambda l, k: (0, 0, l * nk + k)),
           whole(w_uq.shape), whole(w_ukv.shape)],
        out_specs=[pl.BlockSpec((1, SUBLANES, n), lambda l, k: (l, 0, 0)),
                   pl.BlockSpec((chunk, IN_COLS), lambda l, k: (l * nk + k, 0)),
                   whole(uq_shape), whole(uq_shape), whole(wvt_shape)],
        out_shape=[jax.ShapeDtypeStruct((DEPTH, SUBLANES, n), jnp.float32),
                   jax.ShapeDtypeStruct((D_MODEL, IN_COLS), jnp.bfloat16),
                   jax.ShapeDtypeStruct(uq_shape, jnp.bfloat16),
                   jax.ShapeDtypeStruct(uq_shape, jnp.bfloat16),
                   jax.ShapeDtypeStruct(wvt_shape, jnp.bfloat16)],
        compiler_params=pltpu.CompilerParams(dimension_semantics=("arbitrary", "arbitrary"),
                                             vmem_limit_bytes=VMEM_LIMIT),
        name="modulation",
    )(c_ctx.reshape(1, D_MODEL), c, *([w_mod] * MOD_STREAMS), b_mod, w_in_t, w_uq, w_ukv)


def _ssd_head_terms(dt_raw, dtb_c, a_neg_c):
    nh2 = 2 * SSD_HEADS
    row = lax.broadcasted_iota(jnp.int32, (CHUNK, CHUNK), 0)
    col = lax.broadcasted_iota(jnp.int32, (CHUNK, CHUNK), 1)
    upper_b = (row <= col).astype(jnp.bfloat16)
    lower_b = (row >= col).astype(jnp.bfloat16)
    fwd_rows = lax.broadcasted_iota(jnp.int32, (nh2, CHUNK), 0) < SSD_HEADS
    dts_t = _softplus(dt_raw.T[:nh2, :] + dtb_c)
    pieces = _split3(dts_t * a_neg_c)
    cum_t = jnp.where(fwd_rows, sum(_dot(p, upper_b) for p in pieces), sum(_dot(p, lower_b) for p in pieces))
    cum_t = cum_t * np.float32(np.log2(np.e))
    tot = jnp.where(fwd_rows[:, :1], cum_t[:, CHUNK - 1:], cum_t[:, :1])
    return dts_t, cum_t, dts_t * jnp.exp2(tot - cum_t)


def _inproj_kernel(xp_ref, xs_ref, mod_ref, cos_ref, sin_ref, w_in_ref, w_uq_ref, qn_ref, kvn_ref, npm_ref,
                   dtb_ref, alog_ref, q_ref, ckv_ref, kpe_ref, z_ref, xbc_ref, dts_ref, cumt_ref, wt_ref,
                   new_ckv_ref, new_kpe_ref):
    i = pl.program_id(0)
    sh = _mod_part(mod_ref, 0)
    sc = _mod_part(mod_ref, 1)
    scale = (NOPE_DIM + ROPE_DIM) ** -0.5 * np.log2(np.e)
    nh2 = 2 * SSD_HEADS
    diag = (lax.broadcasted_iota(jnp.int32, (nh2, LANES), 0) == lax.broadcasted_iota(jnp.int32, (nh2, LANES), 1))
    to_col = lambda ref: jnp.sum(jnp.where(diag, ref[:, :LANES], 0.0), axis=1, keepdims=True)
    dtb_c = to_col(dtb_ref)
    a_neg_c = -jnp.exp(to_col(alog_ref))
    dt_raw = []
    for r in range(TM // SUB_ROWS):
        rs = slice(r * SUB_ROWS, (r + 1) * SUB_ROWS)
        h = (_rms(_pick_group(i, xp_ref, xs_ref, rs), npm_ref[...]) * (1.0 + sc) + sh).astype(jnp.bfloat16)
        p = _dot(h, w_in_ref[...])
        o = 0
        cq = p[:, o:o + Q_RANK]; o += Q_RANK
        ckv = p[:, o:o + KV_RANK]; o += KV_RANK
        z_ref[rs, :] = p[:, o:o + D_SSD]; o += D_SSD
        xbc_ref[rs, :] = p[:, o:o + CONV_CH]; o += CONV_CH
        kpe = p[:, o:o + LANES]; o += LANES
        dt_raw.append(p[:, o:o + LANES])

        cos = cos_ref[rs, :]
        sin = sin_ref[rs, :]

        def rope(slab):
            return slab * cos + pltpu.roll(slab, LANES - ROPE_DIM, 1) * sin

        ckv_ref[rs, :] = _rms(ckv, kvn_ref[:, :KV_RANK])
        kpe_ref[rs, :] = rope(kpe)
        q = _dot(_rms(cq, qn_ref[:, :Q_RANK]).astype(jnp.bfloat16), w_uq_ref[...]) * scale
        for hd in range(MLA_HEADS):
            sl = slice(hd * HEAD_SLAB, (hd + 1) * HEAD_SLAB)
            q_ref[rs, sl] = rope(q[:, sl]).astype(jnp.bfloat16)

    dt_all = jnp.concatenate(dt_raw, axis=0)
    for ck in range(TM // CHUNK):
        terms = _ssd_head_terms(dt_all[ck * CHUNK:(ck + 1) * CHUNK, :], dtb_c, a_neg_c)
        for ref, val in zip((dts_ref, cumt_ref, wt_ref), terms):
            ref[ck * nh2:(ck + 1) * nh2, :] = val

    @pl.when(i < N_PT)
    def _():
        new_ckv_ref[...] = ckv_ref[...]
        kpe_t = kpe_ref[...].T
        for b in range(TM // SEQ):
            new_kpe_ref[b] = kpe_t[KPE_LANE0:KPE_LANE0 + ROPE_DIM, b * SEQ:(b + 1) * SEQ]


def _inproj(l, vec_table, vec_rows, xp, xs, mod, tabs, w_in, w_uq, q_norm, kv_norm):
    nt = N_TILES
    prompt_blk = lambda i: (jnp.minimum(i, N_PT - 1), 0)
    lat_tiles = DEC_SEQ // TM
    tab_spec = pl.BlockSpec((TM, LANES), lambda i: (jnp.where(i < N_PT, lat_tiles, (i - N_PT) % lat_tiles), 0))
    row = _row_spec
    hrows = TM // CHUNK * 2 * SSD_HEADS
    head_spec = pl.BlockSpec((hrows, CHUNK), lambda i: (i, 0))
    head_shape = jax.ShapeDtypeStruct((nt * hrows, CHUNK), jnp.float32)
    return pl.pallas_call(
        _inproj_kernel,
        grid=(nt,),
        in_specs=_group_specs(D_MODEL) + [
            _mod_spec(l),
            tab_spec, tab_spec,
            _const_spec((D_MODEL, IN_COLS)),
            _const_spec((Q_RANK, MLA_HEADS * HEAD_SLAB)),
            _narrow_spec(q_norm, l // 2),
            _narrow_spec(kv_norm, l // 2),
            _vec_spec(vec_rows["norm_pre_mix"] + l),
            _vec_spec(vec_rows["ssd_dt_bias"] + l // 2),
            _vec_spec(vec_rows["ssd_a_log"] + l // 2),
        ],
        out_specs=[row(MLA_HEADS * HEAD_SLAB), row(KV_RANK), row(LANES), row(D_SSD), row(CONV_CH),
                   head_spec, head_spec, head_spec,
                   pl.BlockSpec((TM, KV_RANK), prompt_blk),
                   pl.BlockSpec((TM // SEQ, ROPE_DIM, SEQ), lambda i: (jnp.minimum(i, N_PT - 1), 0, 0))],
        out_shape=[
            jax.ShapeDtypeStruct((N_TOK, MLA_HEADS * HEAD_SLAB), jnp.bfloat16),
            jax.ShapeDtypeStruct((N_TOK, KV_RANK), jnp.float32),
            jax.ShapeDtypeStruct((N_TOK, LANES), jnp.float32),
            jax.ShapeDtypeStruct((N_TOK, D_SSD), jnp.float32),
            jax.ShapeDtypeStruct((N_TOK, CONV_CH), jnp.float32),
            head_shape, head_shape, head_shape,
            jax.ShapeDtypeStruct((N_PROMPT, KV_RANK), jnp.float32),
            jax.ShapeDtypeStruct((BATCH, ROPE_DIM, SEQ), jnp.float32),
        ],
        compiler_params=pltpu.CompilerParams(dimension_semantics=("arbitrary",), vmem_limit_bytes=VMEM_LIMIT),
        name="inproj",
    )(xp, xs, mod, *tabs, w_in, w_uq, q_norm, kv_norm, *([vec_table] * 3))


def _attn_kernel(*refs, lk_cache, lk_new, n_pack, n_cast):
    n_in = 7 if lk_cache else 5
    if lk_cache:
        q_ref, ckv_ref, kpe_ref, ckvc_ref, kpec_ref, wk_ref, wvt_ref = refs[:n_in]
    else:
        q_ref, ckv_ref, kpe_ref, wk_ref, wvt_ref = refs[:n_in]
    cast_src, o_ref = refs[n_in:n_in + n_cast], refs[n_in + n_cast]
    cast_dst = refs[n_in + n_cast + 1:n_in + 2 * n_cast + 1]
    k_scr, vt_scr = refs[n_in + 2 * n_cast + 1:]
    _cast_chunks(cast_src, cast_dst)
    lk = lk_cache + lk_new

    @pl.when(pl.program_id(1) == 0)
    def _expand_kv():
        def expand(ckv, kpe, r0):
            ckv_b = ckv.astype(jnp.bfloat16)
            kn = _dot(ckv_b, wk_ref[...])
            rows = slice(r0, r0 + ckv.shape[0])
            for hd in range(MLA_HEADS):
                k_scr[hd, rows, :] = (kn[:, hd * HEAD_SLAB:(hd + 1) * HEAD_SLAB] + kpe).astype(jnp.bfloat16)
            vt_scr[:, rows] = _dot_nt(wvt_ref[...], ckv_b).astype(jnp.bfloat16)

        step = 256
        for r0 in range(0, lk_cache, step):
            kpec = kpec_ref[0, r0:r0 + step, :]
            slab = jnp.concatenate([jnp.zeros((step, KPE_LANE0), jnp.float32), kpec,
                                    jnp.zeros((step, LANES - KPE_LANE0 - ROPE_DIM), jnp.float32)], axis=1)
            expand(ckvc_ref[0, r0:r0 + step, :], slab, r0)
        for r0 in range(0, n_pack * lk_new, step):
            expand(ckv_ref[r0:r0 + step, :], kpe_ref[r0:r0 + step, :], lk_cache + r0)

    work = [(s, hd) for s in range(n_pack) for hd in range(MLA_HEADS)]
    scores = [_dot_nt(k_scr[hd, s * lk:(s + 1) * lk, :], q_ref[s * TQ:(s + 1) * TQ, hd * HEAD_SLAB:(hd + 1) * HEAD_SLAB])
              for s, hd in work]
    outs = []
    for (s, hd), s_t in zip(work, scores):
        p_t = jnp.exp2(s_t - jnp.max(s_t, axis=0, keepdims=True))
        den = jnp.sum(p_t, axis=0, keepdims=True)
        v_t = vt_scr[hd * V_DIM:(hd + 1) * V_DIM, s * lk:(s + 1) * lk]
        outs.append(_dot(v_t, p_t.astype(jnp.bfloat16)) / den)
    for s in range(n_pack):
        o_ref[s * TQ:(s + 1) * TQ, :] = jnp.concatenate(
            outs[s * MLA_HEADS:(s + 1) * MLA_HEADS], axis=0).T.astype(jnp.bfloat16)


def _attention(q, ckv_n, kpe, w_ukv, row_off, n_batch, seq, cache=None, n_pack=1, cast=()):
    nq = seq // TQ
    assert n_pack == 1 or (nq == 1 and cache is None and n_batch % n_pack == 0)
    n_batch //= n_pack
    lk_cache = 0 if cache is None else cache[0].shape[1]
    lk = n_pack * (lk_cache + seq)
    qblk = lambda b, qi: (row_off // (n_pack * TQ) + b * nq + qi, 0)
    sblk = lambda b, qi: (row_off // (n_pack * seq) + b, 0)
    in_specs = [
        pl.BlockSpec((n_pack * TQ, MLA_HEADS * HEAD_SLAB), qblk),
        pl.BlockSpec((n_pack * seq, KV_RANK), sblk),
        pl.BlockSpec((n_pack * seq, LANES), sblk),
    ]
    args = [q, ckv_n, kpe]
    if cache is not None:
        in_specs += [pl.BlockSpec((1, lk_cache, KV_RANK), lambda b, qi: (b, 0, 0)),
                     pl.BlockSpec((1, lk_cache, ROPE_DIM), lambda b, qi: (b, 0, 0))]
        args += list(cache)
    in_specs += [_const_spec(w.shape) for w in w_ukv]
    args += list(w_ukv)
    assert not cast or n_batch * nq == N_CAST
    step = lambda b, qi: b * nq + qi
    in_specs += [_cast_in_spec(w, layer, step) for w, layer in cast]
    args += [w for w, _ in cast]
    cast_out = [_cast_out(w, step) for w, _ in cast]
    return pl.pallas_call(
        functools.partial(_attn_kernel, lk_cache=lk_cache, lk_new=seq, n_pack=n_pack, n_cast=len(cast)),
        grid=(n_batch, nq),
        in_specs=in_specs,
        out_specs=[pl.BlockSpec((n_pack * TQ, MLA_HEADS * V_DIM), lambda b, qi: (b * nq + qi, 0))]
        + [spec for spec, _ in cast_out],
        out_shape=[jax.ShapeDtypeStruct((n_batch * n_pack * seq, MLA_HEADS * V_DIM), jnp.bfloat16)]
        + [shape for _, shape in cast_out],
        scratch_shapes=[pltpu.VMEM((MLA_HEADS, lk, HEAD_SLAB), jnp.bfloat16),
                        pltpu.VMEM((MLA_HEADS * V_DIM, lk), jnp.bfloat16)],
        compiler_params=pltpu.CompilerParams(dimension_semantics=("arbitrary", "arbitrary"),
                                             vmem_limit_bytes=VMEM_LIMIT),
        name=f"attention_{seq}",
    )(*args)


def _ssd_kernel(*refs, seq, zero_init):
    if zero_init:
        (xbc_ref, dts_ref, cumt_ref, wt_ref, cw_ref, cb_ref, dsk_ref,
         y_ref, hf_ref, hb_ref, xs_scr, c_scr, bt_scr, cum_scr, stf_scr, stb_scr) = refs
    else:
        (xbc_ref, dts_ref, cumt_ref, wt_ref, h0f_ref, h0b_ref, cw_ref, cb_ref, dsk_ref,
         y_ref, hf_ref, hb_ref, xs_scr, c_scr, bt_scr, cum_scr, stf_scr, stb_scr) = refs
    nc = seq // CHUNK
    gs = SSD_GROUPS * SSD_STATE
    nh2 = 2 * SSD_HEADS

    row = lax.broadcasted_iota(jnp.int32, (CHUNK, CHUNK), 0)
    col = lax.broadcasted_iota(jnp.int32, (CHUNK, CHUNK), 1)
    low_half = col < SSD_HEAD_DIM
    lower = row >= col
    upper = row <= col

    def prep_chunk(c, carry):
        r0 = pl.multiple_of(c * CHUNK, CHUNK)
        rows = pl.ds(r0, CHUNK)
        rows_prev = pl.ds(pl.multiple_of(jnp.maximum(r0 - HALO, 0), HALO), HALO)
        rows_next = pl.ds(pl.multiple_of(jnp.minimum(r0 + CHUNK, seq - HALO), HALO), HALO)

        def conv_tile(cs):
            prev = jnp.where(c > 0, xbc_ref[rows_prev, cs], 0.0)
            nxt = jnp.where(c < nc - 1, xbc_ref[rows_next, cs], 0.0)
            win = jnp.concatenate([prev, xbc_ref[rows, cs], nxt], axis=0)
            acc = jnp.broadcast_to(cb_ref[:, cs], (CHUNK, LANES))
            for k in range(CONV_W):
                lo = HALO - CONV_W // 2 + k
                acc = acc + cw_ref[k:k + 1, cs] * win[lo:lo + CHUNK, :]
            return _silu(acc)

        def x_tile(j, carry):
            cs = pl.ds(pl.multiple_of(j * LANES, LANES), LANES)
            u = conv_tile(cs)
            y_ref[rows, cs] = dsk_ref[:, cs] * u
            xs_scr[0, rows, cs] = jnp.where(low_half, u, 0.0).astype(jnp.bfloat16)
            xs_scr[1, rows, cs] = jnp.where(low_half, 0.0, u).astype(jnp.bfloat16)
            return carry

        lax.fori_loop(0, D_SSD // LANES, x_tile, 0)
        cum_t = cumt_ref[pl.ds(pl.multiple_of(c * nh2, nh2), nh2), :]
        cum_scr[rows, :] = jnp.concatenate(
            [cum_t, jnp.zeros((CHUNK - nh2, CHUNK), jnp.float32)], axis=0).T[:, :nh2]
        for g in range(SSD_GROUPS):
            b0 = pl.multiple_of(c * gs + g * SSD_STATE, SSD_STATE)
            bt_scr[pl.ds(b0, SSD_STATE), :] = conv_tile(slice(D_SSD + g * SSD_STATE, D_SSD + (g + 1) * SSD_STATE)).T
            c_scr[rows, g * SSD_STATE:(g + 1) * SSD_STATE] = conv_tile(
                slice(D_SSD + gs + g * SSD_STATE, D_SSD + gs + (g + 1) * SSD_STATE)).astype(jnp.bfloat16)
        return carry

    lax.fori_loop(0, nc, prep_chunk, 0)

    if zero_init:
        stf_scr[...] = jnp.zeros_like(stf_scr)
        stb_scr[...] = jnp.zeros_like(stb_scr)
    else:
        stf_scr[...] = h0f_ref[0].T
        stb_scr[...] = h0b_ref[0].T

    def scan_open(ci, st_scr):
        rows = pl.ds(pl.multiple_of(ci * CHUNK, CHUNK), CHUNK)
        c_b = c_scr[rows, :]
        st = st_scr[...]
        bts, cbms, zs = [], [], []
        for g in range(SSD_GROUPS):
            cg = c_b[:, g * SSD_STATE:(g + 1) * SSD_STATE]
            bt = bt_scr[pl.ds(pl.multiple_of(ci * gs + g * SSD_STATE, SSD_STATE), SSD_STATE), :]
            gcols = slice(g * SSD_HPG * SSD_HEAD_DIM, (g + 1) * SSD_HPG * SSD_HEAD_DIM)
            bts.append(bt)
            cbms.append(_dot(cg, bt.astype(jnp.bfloat16)))
            zs.append(_dot(cg, st[:, gcols].astype(jnp.bfloat16)))
        return st, bts, cbms, zs

    def scan_pairs(ci, st_scr, reverse, opened):
        st, bts, cbms, zs = opened
        lane0 = SSD_HEADS if reverse else 0
        causal = upper if reverse else lower
        last = 0 if reverse else CHUNK - 1
        rows = pl.ds(pl.multiple_of(ci * CHUNK, CHUNK), CHUNK)
        hrows = pl.ds(pl.multiple_of(ci * nh2, nh2), nh2)
        xs_lo = xs_scr[0, rows, :]
        xs_hi = xs_scr[1, rows, :]
        dts_t = dts_ref[hrows, :]
        cum_t = cumt_ref[hrows, :]
        w_t = wt_ref[hrows, :]
        cum = cum_scr[rows, :]
        for pair in range(SSD_HEADS // 2):
            g, jj = divmod(pair, SSD_HPG // 2)
            pcols = slice(pair * LANES, (pair + 1) * LANES)
            lhs_y, lhs_s, entry = [], [], []
            for hd in (2 * pair, 2 * pair + 1):
                ln = lane0 + hd
                cum_i = jnp.broadcast_to(cum[:, ln:ln + 1], (CHUNK, CHUNK))
                dec = jnp.exp2(jnp.where(causal, cum_i - cum_t[ln:ln + 1, :], -jnp.inf))
                lhs_y.append((cbms[g] * dec * dts_t[ln:ln + 1, :]).astype(jnp.bfloat16))
                lhs_s.append((bts[g] * w_t[ln:ln + 1, :]).astype(jnp.bfloat16))
                entry.append(jnp.exp2(cum_i))
            lhs = jnp.concatenate([jnp.concatenate(lhs_y, axis=1), jnp.concatenate(lhs_s, axis=1)], axis=0)
            out = _dot(lhs, jnp.concatenate([xs_lo[:, pcols], xs_hi[:, pcols]], axis=0))
            ea = jnp.where(low_half, entry[0], entry[1])
            y_ref[rows, pcols] += out[:CHUNK] + zs[g][:, jj * LANES:(jj + 1) * LANES] * ea
            st_scr[:, pcols] = ea[last:last + 1, :] * st[:, pcols] + out[CHUNK:]

    def both(c, carry):
        opened_f = scan_open(c, stf_scr)
        opened_b = scan_open(nc - 1 - c, stb_scr)
        scan_pairs(c, stf_scr, False, opened_f)
        scan_pairs(nc - 1 - c, stb_scr, True, opened_b)
        return carry

    lax.fori_loop(0, nc, both, 0)
    hf_ref[0] = stf_scr[...].T
    hb_ref[0] = stb_scr[...].T


def _ssd(i_ab, xbc, head_terms, h0, conv_w, vec_table, conv_b_row, d_skip, row_off, n_batch, seq):
    assert CONV_CH == D_MODEL
    hp = SSD_HEADS * SSD_HEAD_DIM
    gs = SSD_GROUPS * SSD_STATE
    nc = seq // CHUNK
    seq_blk = lambda b: (row_off // seq + b, 0)
    head_spec = pl.BlockSpec((nc * 2 * SSD_HEADS, CHUNK), seq_blk)
    st_spec = pl.BlockSpec((1, hp, SSD_STATE), lambda b: (b, 0, 0))
    st_shape = jax.ShapeDtypeStruct((n_batch, hp, SSD_STATE), jnp.float32)
    h0 = () if h0 is None else tuple(h0)
    return pl.pallas_call(
        functools.partial(_ssd_kernel, seq=seq, zero_init=not h0),
        grid=(n_batch,),
        in_specs=[pl.BlockSpec((seq, CONV_CH), seq_blk), head_spec, head_spec, head_spec] + [st_spec] * len(h0) + [
            pl.BlockSpec((None, CONV_W, CONV_CH), lambda b: (i_ab, 0, 0), pipeline_mode=pl.Buffered(1)),
            _vec_spec(conv_b_row + i_ab), _const_spec((1, D_SSD))],
        out_specs=[pl.BlockSpec((seq, D_SSD), lambda b: (b, 0)), st_spec, st_spec],
        out_shape=[jax.ShapeDtypeStruct((n_batch * seq, D_SSD), jnp.float32), st_shape, st_shape],
        scratch_shapes=[pltpu.VMEM((2, seq, D_SSD), jnp.bfloat16),
                        pltpu.VMEM((seq, gs), jnp.bfloat16),
                        pltpu.VMEM((nc * gs, CHUNK), jnp.float32),
                        pltpu.VMEM((seq, 2 * SSD_HEADS), jnp.float32),
                        pltpu.VMEM((SSD_STATE, hp), jnp.float32),
                        pltpu.VMEM((SSD_STATE, hp), jnp.float32)],
        compiler_params=pltpu.CompilerParams(dimension_semantics=("arbitrary",), vmem_limit_bytes=VMEM_LIMIT),
        name=f"ssd_{seq}",
    )(xbc, *head_terms, *h0, conv_w, vec_table, d_skip)


def _post_mix(x, mix, mod_ref, npost_ref, npre_ref):
    d = D_MODEL
    gate_mix = _mod_part(mod_ref, 2)
    shf = _mod_part(mod_ref, 3)
    scf = _mod_part(mod_ref, 4)
    x1 = x + gate_mix * _rms(mix, npost_ref[...])
    return x1, (_rms(x1, npre_ref[...]) * (1.0 + scf) + shf).astype(jnp.bfloat16)


def _subtile_pipeline(n_sub, mixer_pre, mixer_dots, mod_ref, nffn_ref, wg_ref, wu_ref, wd_ref, interleave):
    gate_ffn = _mod_part(mod_ref, 5)
    ffn_up = lambda h: (_silu(_dot(h, wg_ref[...])) * _dot(h, wu_ref[...])).astype(jnp.bfloat16)
    ffn_down = lambda x1, hid: x1 + gate_ffn * _rms(_dot(hid, wd_ref[...]), nffn_ref[...])
    if not interleave:
        staged = [mixer_dots(r, mixer_pre(r)) for r in range(n_sub)]
        return [ffn_down(x1, ffn_up(h)) for x1, h in staged]
    outs = []
    x1, h = mixer_dots(0, mixer_pre(0))
    for r in range(n_sub):
        nxt_pre = mixer_pre(r + 1) if r + 1 < n_sub else None
        hid = ffn_up(h)
        nxt = mixer_dots(r + 1, nxt_pre) if r + 1 < n_sub else None
        outs.append(ffn_down(x1, hid))
        if nxt is not None:
            x1, h = nxt
    return outs


def _ffn_specs(l, vec_rows):
    return [_vec_spec(vec_rows[name] + l) for name in ("norm_post_mix", "norm_pre_ffn", "norm_post_ffn")] + [
        _const_spec((D_MODEL, D_FF)), _const_spec((D_MODEL, D_FF)), _const_spec((D_FF, D_MODEL))]


def _outproj_ffn_kernel(*refs, n_cast):
    (xp_ref, xs_ref, attp_ref, atts_ref, yp_ref, ys_ref, z_ref, mod_ref, sn_ref, wo_ref,
     npost_ref, npre_ref, nffn_ref, wg_ref, wu_ref, wd_ref) = refs[:16]
    cast_src, o_ref, cast_dst = refs[16:16 + n_cast], refs[16 + n_cast], refs[17 + n_cast:]
    i = pl.program_id(0)
    _cast_chunks(cast_src, cast_dst)
    gw = D_SSD // SSD_GROUPS
    sub = lambda r: slice(r * SUB_ROWS, (r + 1) * SUB_ROWS)

    def mixer_pre(r):
        yg = _pick_group(i, yp_ref, ys_ref, sub(r)) * _silu(z_ref[sub(r), :])
        parts = [_pick_group(i, attp_ref, atts_ref, sub(r))]
        for g in range(SSD_GROUPS):
            parts.append(_rms(yg[:, g * gw:(g + 1) * gw], sn_ref[:, g * gw:(g + 1) * gw]).astype(jnp.bfloat16))
        return jnp.concatenate(parts, axis=1)

    def mixer_dots(r, cat):
        return _post_mix(_pick_group(i, xp_ref, xs_ref, sub(r)), _dot(cat, wo_ref[...]),
                         mod_ref, npost_ref, npre_ref)

    outs = _subtile_pipeline(TM // SUB_ROWS, mixer_pre, mixer_dots, mod_ref, nffn_ref, wg_ref, wu_ref, wd_ref,
                             interleave=False)
    for r, res in enumerate(outs):
        o_ref[sub(r), :] = res


def _outproj_ffn(l, i_ab, vec_table, vec_rows, ssd_norm, xp, xs, att_p, att_s, y_p, y_s, z, mod, w_out, wg, wu, wd,
                 cast):
    d_cat = MLA_HEADS * V_DIM + D_SSD
    assert not cast or N_TILES == N_CAST
    cast_out = [_cast_out(w) for w, _ in cast]
    return pl.pallas_call(
        functools.partial(_outproj_ffn_kernel, n_cast=len(cast)),
        grid=(N_TILES,),
        in_specs=(_group_specs(D_MODEL) + _group_specs(MLA_HEADS * V_DIM) + _group_specs(D_SSD)
                  + [_row_spec(D_SSD), _mod_spec(l), _narrow_spec(ssd_norm, i_ab),
                     _const_spec((d_cat, D_MODEL))] + _ffn_specs(l, vec_rows)
                  + [_cast_in_spec(w, layer) for w, layer in cast]),
        out_specs=[_row_spec(D_MODEL)] + [spec for spec, _ in cast_out],
        out_shape=[jax.ShapeDtypeStruct((N_TOK, D_MODEL), jnp.float32)] + [shape for _, shape in cast_out],
        compiler_params=pltpu.CompilerParams(dimension_semantics=("arbitrary",), vmem_limit_bytes=VMEM_LIMIT),
        name="outproj_ffn",
    )(xp, xs, att_p, att_s, y_p, y_s, z, mod, ssd_norm, w_out, vec_table, vec_table, vec_table, wg, wu, wd,
      *[w for w, _ in cast])


def _pool_ffn_kernel(x_ref, xp_ref, xn_ref, mod_ref, nmix_ref, pw_ref, ps_ref,
                     npost_ref, npre_ref, nffn_ref, wg_ref, wu_ref, wd_ref, op_ref, os_ref):
    i = pl.program_id(0)

    @pl.when(i == 0)
    def _():
        os_ref[...] = jnp.zeros_like(os_ref)

    seq = jnp.where(i < N_PROMPT // TM, SEQ, DEC_SEQ)
    pos0 = (i * TM) % seq
    sh = _mod_part(mod_ref, 0)
    sc = _mod_part(mod_ref, 1)
    hmod = lambda v: _rms(v, nmix_ref[...]) * (1.0 + sc) + sh
    n_rows = POOL_SUB + 2 * HALO

    def shifted(v, s):
        return pltpu.roll(v, n_rows - s, 0)

    def mixer_pre(s):
        lo, hi = s * POOL_SUB, (s + 1) * POOL_SUB
        pos_s = (pos0 + lo) % seq
        h = hmod(x_ref[lo:hi, :])
        before = hmod(xp_ref[...] if s == 0 else x_ref[lo - HALO:lo, :])
        after = hmod(xn_ref[...] if hi == TM else x_ref[hi:hi + HALO, :])
        before = jnp.where(pos_s > 0, before, 0.0)
        after = jnp.where(pos_s + POOL_SUB < seq, after, 0.0)
        padded = jnp.concatenate([before, h, after], axis=0)
        pos = pos_s + lax.broadcasted_iota(jnp.int32, (POOL_SUB, 1), 0)
        pooled = []
        for gi, w in enumerate(POOL_WINDOWS):
            cols = slice(gi * POOL_GC, (gi + 1) * POOL_GC)
            t = padded[:, cols]
            span = 1
            while span < w:
                t = t + shifted(t, span)
                span *= 2
            lead = HALO - w // 2
            win_sum = (shifted(t, lead) if lead else t)[:POOL_SUB, :]
            cnt = (jnp.minimum(pos + w // 2, seq) - jnp.maximum(pos - w // 2, 0)).astype(jnp.float32)
            pooled.append((win_sum / cnt - h[:, cols]).astype(jnp.bfloat16))
        return pooled

    def mixer_dots(s, pooled):
        mix = jnp.concatenate([_dot(p, pw_ref[gi]) for gi, p in enumerate(pooled)], axis=1) * ps_ref[...]
        return _post_mix(x_ref[s * POOL_SUB:(s + 1) * POOL_SUB, :], mix, mod_ref, npost_ref, npre_ref)

    res = jnp.concatenate(_subtile_pipeline(TM // POOL_SUB, mixer_pre, mixer_dots, mod_ref, nffn_ref,
                                            wg_ref, wu_ref, wd_ref, interleave=True), axis=0)

    @pl.when(i < N_PT)
    def _():
        op_ref[...] = res

    @pl.when(i >= N_PT)
    def _():
        os_ref[...] = res


def _pool_ffn(l, j_c, vec_table, vec_rows, xa, mod, pool_w, wg, wu, wd):
    hb = TM // HALO
    nh = N_TOK // HALO
    return pl.pallas_call(
        _pool_ffn_kernel,
        grid=(N_TILES,),
        in_specs=[_row_spec(D_MODEL),
                  pl.BlockSpec((HALO, D_MODEL), lambda i: (jnp.maximum(i * hb - 1, 0), 0)),
                  pl.BlockSpec((HALO, D_MODEL), lambda i: (jnp.minimum((i + 1) * hb, nh - 1), 0)),
                  _mod_spec(l),
                  _vec_spec(vec_rows["norm_pre_mix"] + l),
                  _const_spec((len(POOL_WINDOWS), POOL_GC, POOL_GC)),
                  _vec_spec(vec_rows["pool_scale"] + j_c)] + _ffn_specs(l, vec_rows),
        out_specs=_group_specs(D_MODEL),
        out_shape=[jax.ShapeDtypeStruct((N_PROMPT, D_MODEL), jnp.float32),
                   jax.ShapeDtypeStruct((N_SAMPLE, D_MODEL), jnp.float32)],
        compiler_params=pltpu.CompilerParams(dimension_semantics=("arbitrary",), vmem_limit_bytes=VMEM_LIMIT),
        name="pool_ffn",
    )(xa, xa, xa, mod, vec_table, pool_w, vec_table, vec_table, vec_table, vec_table, wg, wu, wd)


def _rope_tables():
    f32 = np.float32
    rows = DEC_SEQ // GRID_W
    r = np.repeat(np.arange(rows, dtype=f32), GRID_W)
    c = np.tile(np.arange(GRID_W, dtype=f32), rows)
    half = ROPE_DIM // 2
    inv_freq = np.power(f32(ROPE_THETA), -np.arange(0, half, 2, dtype=f32) / f32(half)).astype(f32)
    ang = np.concatenate([r[:, None] * inv_freq, c[:, None] * inv_freq], axis=-1).astype(f32)
    cos, sin = np.cos(ang).astype(f32), np.sin(ang).astype(f32)
    ones = np.ones((DEC_SEQ, KPE_LANE0), f32)
    zl = np.zeros((DEC_SEQ, KPE_LANE0), f32)
    zr = np.zeros((DEC_SEQ, LANES - KPE_LANE0 - ROPE_DIM), f32)
    cos_t = np.concatenate([ones, cos, cos, zr], axis=1)
    sin_t = np.concatenate([zl, -sin, sin, zr], axis=1)
    keep = np.concatenate([np.ones((TM, KPE_LANE0 + ROPE_DIM), f32), np.zeros((TM, LANES - KPE_LANE0 - ROPE_DIM), f32)],
                          axis=1)
    return (jnp.asarray(np.concatenate([cos_t, keep], axis=0)),
            jnp.asarray(np.concatenate([sin_t, np.zeros((TM, LANES), f32)], axis=0)))


def _layout_in_proj(wt):
    o = np.cumsum((0, Q_RANK, KV_RANK, ROPE_DIM, D_SSD, CONV_CH, SSD_HEADS, SSD_HEADS))
    cq, ckv, kpe, z, xbc, dtf, dtb = (wt[o[k]:o[k + 1], :] for k in range(7))
    zeros = lambda n: jnp.zeros((n, wt.shape[1]), wt.dtype)
    half = ROPE_DIM // 2
    assert KPE_LANE0 + 2 * ROPE_DIM == LANES
    stacked = jnp.concatenate([cq, ckv, z, xbc,
                               zeros(KPE_LANE0), kpe, kpe[half:], kpe[:half],
                               dtf, dtb, zeros(LANES - 2 * SSD_HEADS)], axis=0)
    return stacked.T.astype(jnp.bfloat16)


def _layout_uq(w):
    hw, half = NOPE_DIM + ROPE_DIM, ROPE_DIM // 2
    assert hw + ROPE_DIM == HEAD_SLAB
    pieces = []
    for hd in range(MLA_HEADS):
        s = w[:, hd * hw:(hd + 1) * hw]
        pieces += [s, s[:, NOPE_DIM + half:], s[:, NOPE_DIM:NOPE_DIM + half]]
    return jnp.concatenate(pieces, axis=1).astype(jnp.bfloat16)


def _layout_ukv(w):
    hw = NOPE_DIM + V_DIM
    pad = jnp.zeros((w.shape[0], HEAD_SLAB - NOPE_DIM), w.dtype)
    kn, v = [], []
    for hd in range(MLA_HEADS):
        kn += [w[:, hd * hw:hd * hw + NOPE_DIM], pad]
        v.append(w[:, hd * hw + NOPE_DIM:(hd + 1) * hw])
    return (jnp.concatenate(kn, axis=1).astype(jnp.bfloat16),
            jnp.concatenate(v, axis=1).T.astype(jnp.bfloat16))


def kernel(x_prompt, x_sample, c, cache_mla_ckv, cache_mla_krope, state_ssd_fwd, state_ssd_bwd, c_ctx, w_mod, b_mod, norm_pre_mix, norm_post_mix, norm_pre_ffn, norm_post_ffn, w_in_ab, q_norm, w_uq, kv_norm, w_ukv, ssd_conv_w, ssd_conv_b, ssd_dt_bias_fwd, ssd_dt_bias_bwd, ssd_a_log_fwd, ssd_a_log_bwd, ssd_d, ssd_norm, w_out_ab, pool_w, pool_scale, ffn_w_gate, ffn_w_up, ffn_w_down):
    f32, bf16 = jnp.float32, jnp.bfloat16
    assert DEPTH == 2
    xp = x_prompt.reshape(N_PROMPT, D_MODEL)
    xs = x_sample.reshape(N_SAMPLE, D_MODEL)
    assert 1 + c.shape[0] == N_MODVEC
    mod, w_in, w_uq_slabs, *w_kv = _modulation(c_ctx, c, w_mod, b_mod, jnp.swapaxes(w_in_ab, 1, 2), w_uq, w_ukv)
    tabs = _rope_tables()
    hp = SSD_HEADS * SSD_HEAD_DIM
    row = lambda v: v.reshape(1, -1)
    new_ckv, new_kpe, new_hf, new_hb = [], [], [], []

    vec_table, vec_rows = _pack_vectors(
        norm_pre_mix=norm_pre_mix, norm_post_mix=norm_post_mix, norm_pre_ffn=norm_pre_ffn, norm_post_ffn=norm_post_ffn,
        ssd_conv_b=ssd_conv_b, pool_scale=pool_scale,
        ssd_dt_bias=jnp.concatenate([ssd_dt_bias_fwd, ssd_dt_bias_bwd], axis=1),
        ssd_a_log=jnp.concatenate([ssd_a_log_fwd, ssd_a_log_bwd], axis=1))
    ffn_w_f32 = (ffn_w_gate, ffn_w_up, ffn_w_down)
    for l in range(DEPTH):
        if l % 2 == 0:
            i = l // 2
            q, ckv_n, kpe, z, xbc, dts_t, cum_t, w_t, ckv_prompt, kpe_prompt = _inproj(
                l, vec_table, vec_rows, xp, xs, mod, tabs, w_in, w_uq_slabs, q_norm, kv_norm)
            heads = (dts_t, cum_t, w_t)
            (att_p,) = _attention(q, ckv_n, kpe, w_kv, 0, BATCH, SEQ, n_pack=PROMPT_PACK)
            att_s, w_out, *ffn_w = _attention(q, ckv_n, kpe, w_kv, N_PROMPT, DEC_BATCH, DEC_SEQ,
                                              cache=(cache_mla_ckv[:, i], cache_mla_krope[:, i]),
                                              cast=[(w_out_ab, i)] + [(w, l) for w in ffn_w_f32])
            ssd_args = (ssd_conv_w, vec_table, vec_rows["ssd_conv_b"], row(jnp.repeat(ssd_d[i], SSD_HEAD_DIM)))
            y_p, hf, hb = _ssd(i, xbc, heads, None, *ssd_args, 0, BATCH, SEQ)
            y_s, _, _ = _ssd(i, xbc, heads, (state_ssd_fwd[:, i].reshape(DEC_BATCH, hp, SSD_STATE),
                                             state_ssd_bwd[:, i].reshape(DEC_BATCH, hp, SSD_STATE)),
                             *ssd_args, N_PROMPT, DEC_BATCH, DEC_SEQ)
            xa, *ffn_w, pool_w_b = _outproj_ffn(
                l, i, vec_table, vec_rows, ssd_norm, xp, xs, att_p, att_s, y_p, y_s, z, mod, w_out, *ffn_w,
                cast=[(w, l + 1) for w in ffn_w_f32] + [(pool_w.reshape(pool_w.shape[0], -1, POOL_GC), (l + 1) // 2)])
            new_ckv.append(ckv_prompt.reshape(BATCH, SEQ, KV_RANK))
            new_kpe.append(jnp.swapaxes(kpe_prompt, 1, 2))
            new_hf.append(hf.reshape(BATCH, SSD_HEADS, SSD_HEAD_DIM, SSD_STATE))
            new_hb.append(hb.reshape(BATCH, SSD_HEADS, SSD_HEAD_DIM, SSD_STATE))
        else:
            j = l // 2
            yp, ys = _pool_ffn(l, j, vec_table, vec_rows, xa, mod,
                               pool_w_b.reshape(len(POOL_WINDOWS), POOL_GC, POOL_GC), *ffn_w)

    return (yp.reshape(BATCH, SEQ, D_MODEL), ys.reshape(DEC_BATCH, DEC_SEQ, D_MODEL),
            jnp.stack(new_ckv, axis=1), jnp.stack(new_kpe, axis=1),
            jnp.stack(new_hf, axis=1), jnp.stack(new_hb, axis=1))
```

```python
import functools

import numpy as np
import jax
import jax.numpy as jnp
from jax import lax
from jax.experimental import pallas as pl
from jax.experimental.pallas import tpu as pltpu

D_MODEL = 1024
BATCH = 16
SEQ = 256
DEPTH = 2
DEC_BATCH = 2
DEC_SEQ = 2048
PAST_LEN = 256
GRID_W = 64
EPS = 1e-6
MLA_HEADS = 8
Q_RANK = 256
KV_RANK = 256
NOPE_DIM = 64
ROPE_DIM = 32
V_DIM = 64
ROPE_THETA = 10000.0
SSD_HEADS = 8
SSD_GROUPS = 2
SSD_HPG = SSD_HEADS // SSD_GROUPS
SSD_HEAD_DIM = 64
SSD_STATE = 128
D_SSD = SSD_HEADS * SSD_HEAD_DIM
CONV_W = 5
CONV_CH = D_SSD + 2 * SSD_GROUPS * SSD_STATE
POOL_WINDOWS = (2, 4, 8, 16)
POOL_GC = D_MODEL // len(POOL_WINDOWS)
D_FF = ((8 * D_MODEL + 3 * 256 - 1) // (3 * 256)) * 256

SUBLANES = 8
LANES = 128

N_PROMPT = BATCH * SEQ
N_SAMPLE = DEC_BATCH * DEC_SEQ
N_TOK = N_PROMPT + N_SAMPLE
N_MODVEC = 1 + DEC_BATCH
TM = 512
TQ = 256
CHUNK = 128
HALO = SUBLANES
HEAD_SLAB = LANES
IN_COLS = Q_RANK + KV_RANK + D_SSD + CONV_CH + 2 * LANES
KPE_LANE0 = NOPE_DIM
VMEM_LIMIT = 56 * 1024 * 1024

POOL_SUB = min(SEQ, DEC_SEQ)

assert TM % POOL_SUB == 0 and SEQ % POOL_SUB == 0 and DEC_SEQ % TM == 0 and N_PROMPT % DEC_SEQ == 0


def _rms(x, g):
    return x * lax.rsqrt(jnp.mean(x * x, axis=-1, keepdims=True) + EPS) * g


def _silu(x):
    return x * jax.nn.sigmoid(x)


def _softplus(x):
    return jnp.maximum(x, 0.0) + jnp.log1p(jnp.exp(-jnp.abs(x)))


def _dot(a, b):
    return jnp.dot(a, b, preferred_element_type=jnp.float32)


def _dot_nt(a, b):
    return lax.dot_general(a, b, (((1,), (1,)), ((), ())), preferred_element_type=jnp.float32)


def _mod_row(i):
    return jnp.where(i < N_PROMPT // TM, 0, 1 + (i - N_PROMPT // TM) // (DEC_SEQ // TM))


def _const_spec(shape):
    nd = len(shape)
    return pl.BlockSpec(shape, lambda *_: (0,) * nd, pipeline_mode=pl.Buffered(1))


N_PT = N_PROMPT // TM
N_TILES = N_TOK // TM


def _row_spec(width):
    return pl.BlockSpec((TM, width), lambda i: (i, 0))


def _mod_spec(l):
    return pl.BlockSpec((None, SUBLANES, 6 * D_MODEL), lambda i: (l, 0, 0), pipeline_mode=pl.Buffered(1))


def _mod_part(mod_ref, k):
    return mod_ref[pl.ds(_mod_row(pl.program_id(0)), 1), k * D_MODEL:(k + 1) * D_MODEL]


def _vec_spec(row):
    return pl.BlockSpec((None, 1, D_MODEL), lambda *_: (row, 0, 0), pipeline_mode=pl.Buffered(1))


def _narrow_spec(a, layer):
    assert a.shape[0] == 1
    return pl.BlockSpec((1, a.shape[1]), lambda *_: (layer, 0), pipeline_mode=pl.Buffered(1))


def _pack_vectors(**params):
    first_row, blocks, n = {}, [], 0
    for name, a in params.items():
        first_row[name] = n
        n += a.shape[0]
        blocks.append(jnp.pad(a, ((0, 0), (0, D_MODEL - a.shape[1]))))
    return jnp.concatenate(blocks, axis=0).reshape(n, 1, D_MODEL), first_row


def _group_specs(width):
    return [pl.BlockSpec((TM, width), lambda i: (jnp.minimum(i, N_PT - 1), 0)),
            pl.BlockSpec((TM, width), lambda i: (jnp.maximum(i - N_PT, 0), 0))]


N_CAST = 16


def _cast_in_spec(w, layer, step=lambda i: i):
    _, rows, cols = w.shape
    return pl.BlockSpec((1, rows // N_CAST, cols), lambda *g: (layer, step(*g), 0))


def _cast_out(w, step=lambda i: i):
    _, rows, cols = w.shape
    return (pl.BlockSpec((rows // N_CAST, cols), lambda *g: (step(*g), 0)),
            jax.ShapeDtypeStruct((rows, cols), jnp.bfloat16))


def _cast_chunks(src_refs, dst_refs):
    for src, dst in zip(src_refs, dst_refs):
        dst[...] = src[0].astype(jnp.bfloat16)


def _pick_group(i, p_ref, s_ref, rows=slice(None)):
    return jnp.where(i < N_PT, p_ref[rows, :], s_ref[rows, :])


MOD_TK = 256
MOD_STREAMS = 2
PROMPT_PACK = 4
MOD_ROWS = 2 * SUBLANES
SUB_ROWS = 256


def _split3(a):
    a_hi = a.astype(jnp.bfloat16)
    r1 = a - a_hi.astype(jnp.float32)
    a_mid = r1.astype(jnp.bfloat16)
    a_lo = (r1 - a_mid.astype(jnp.float32)).astype(jnp.bfloat16)
    return a_hi, a_mid, a_lo


def _mod_kernel(cctx_ref, c_ref, *refs):
    w_refs, (b_ref, win_ref, wuq_ref, wukv_ref, o_ref, wl_ref, uq_ref, wk_ref, wvt_ref) = (
        refs[:MOD_STREAMS], refs[MOD_STREAMS:])
    wl_ref[...] = _layout_in_proj(win_ref[0])
    k = pl.program_id(1)

    @pl.when((pl.program_id(0) == 0) & (k == 0))
    def _():
        uq_ref[...] = _layout_uq(wuq_ref[0])
        wk_ref[...], wvt_ref[...] = _layout_ukv(wukv_ref[0])

    rows = MOD_TK // MOD_STREAMS
    rowid = lax.broadcasted_iota(jnp.int32, (MOD_ROWS, MOD_TK), 0)
    cv = jnp.where(rowid == 0, cctx_ref[...], 0.0)
    for b in range(c_ref.shape[0]):
        cv = jnp.where(rowid == 1 + b, c_ref[b:b + 1, :], cv)
    s_all = _silu(cv)
    part = jnp.zeros((MOD_ROWS, 6 * D_MODEL), jnp.float32)
    for j, w_ref in enumerate(w_refs):
        s_hi, s_mid, s_lo = _split3(s_all[:, j * rows:(j + 1) * rows])
        w = w_ref[0]
        w_hi = w.astype(jnp.bfloat16)
        w_lo = (w - w_hi.astype(jnp.float32)).astype(jnp.bfloat16)
        top = _dot(jnp.concatenate([s_hi, s_mid, s_lo], axis=0), w_hi)
        low = _dot(jnp.concatenate([s_hi, s_mid], axis=0), w_lo)
        part = part + (top[:MOD_ROWS] + top[MOD_ROWS:2 * MOD_ROWS] + top[2 * MOD_ROWS:]
                       + low[:MOD_ROWS] + low[MOD_ROWS:])
    part = part[:SUBLANES]

    @pl.when(k == 0)
    def _():
        o_ref[0] = part + b_ref[pl.ds(pl.program_id(0), 1), :]

    @pl.when(k > 0)
    def _():
        o_ref[0] += part


def _modulation(c_ctx, c, w_mod, b_mod, w_in_t, w_uq, w_ukv):
    n = 6 * D_MODEL
    nk = D_MODEL // MOD_TK
    chunk = D_MODEL // (DEPTH * nk)
    assert w_in_t.shape[0] == 1 and w_uq.shape[0] == 1 and w_ukv.shape[0] == 1
    whole = lambda shape: pl.BlockSpec(shape, lambda l, k: (0,) * len(shape))
    uq_shape = (Q_RANK, MLA_HEADS * HEAD_SLAB)
    wvt_shape = (MLA_HEADS * V_DIM, KV_RANK)
    assert Q_RANK == KV_RANK
    w_spec = lambda j: pl.BlockSpec((1, MOD_TK // MOD_STREAMS, n), lambda l, k: (l, k * MOD_STREAMS + j, 0))
    return pl.pallas_call(
        _mod_kernel,
        grid=(DEPTH, D_MODEL // MOD_TK),
        in_specs=[pl.BlockSpec((1, MOD_TK), lambda l, k: (0, k)),
                  pl.BlockSpec((c.shape[0], MOD_TK), lambda l, k: (0, k))]
        + [w_spec(j) for j in range(MOD_STREAMS)]
        + [pl.BlockSpec((DEPTH, n), lambda l, k: (0, 0)),
           pl.BlockSpec((1, w_in_t.shape[1], chunk), lambda l, k: (0, 0, l * nk + k)),
           whole(w_uq.shape), whole(w_ukv.shape)],
        out_specs=[pl.BlockSpec((1, SUBLANES, n), lambda l, k: (l, 0, 0)),
                   pl.BlockSpec((chunk, IN_COLS), lambda l, k: (l * nk + k, 0)),
                   whole(uq_shape), whole(uq_shape), whole(wvt_shape)],
        out_shape=[jax.ShapeDtypeStruct((DEPTH, SUBLANES, n), jnp.float32),
                   jax.ShapeDtypeStruct((D_MODEL, IN_COLS), jnp.bfloat16),
                   jax.ShapeDtypeStruct(uq_shape, jnp.bfloat16),
                   jax.ShapeDtypeStruct(uq_shape, jnp.bfloat16),
                   jax.ShapeDtypeStruct(wvt_shape, jnp.bfloat16)],
        compiler_params=pltpu.CompilerParams(dimension_semantics=("arbitrary", "arbitrary"),
                                             vmem_limit_bytes=VMEM_LIMIT),
        name="modulation",
    )(c_ctx.reshape(1, D_MODEL), c, *([w_mod] * MOD_STREAMS), b_mod, w_in_t, w_uq, w_ukv)


def _ssd_head_terms(dt_raw, dtb_c, a_neg_c):
    nh2 = 2 * SSD_HEADS
    row = lax.broadcasted_iota(jnp.int32, (CHUNK, CHUNK), 0)
    col = lax.broadcasted_iota(jnp.int32, (CHUNK, CHUNK), 1)
    upper_b = (row <= col).astype(jnp.bfloat16)
    lower_b = (row >= col).astype(jnp.bfloat16)
    fwd_rows = lax.broadcasted_iota(jnp.int32, (nh2, CHUNK), 0) < SSD_HEADS
    dts_t = _softplus(dt_raw.T[:nh2, :] + dtb_c)
    pieces = _split3(dts_t * a_neg_c)
    cum_t = jnp.where(fwd_rows, sum(_dot(p, upper_b) for p in pieces), sum(_dot(p, lower_b) for p in pieces))
    cum_t = cum_t * np.float32(np.log2(np.e))
    tot = jnp.where(fwd_rows[:, :1], cum_t[:, CHUNK - 1:], cum_t[:, :1])
    return dts_t, cum_t, dts_t * jnp.exp2(tot - cum_t)


def _inproj_kernel(xp_ref, xs_ref, mod_ref, cos_ref, sin_ref, w_in_ref, w_uq_ref, qn_ref, kvn_ref, npm_ref,
                   dtb_ref, alog_ref, q_ref, ckv_ref, kpe_ref, z_ref, xbc_ref, dts_ref, cumt_ref, wt_ref,
                   new_ckv_ref, new_kpe_ref):
    i = pl.program_id(0)
    sh = _mod_part(mod_ref, 0)
    sc = _mod_part(mod_ref, 1)
    scale = (NOPE_DIM + ROPE_DIM) ** -0.5 * np.log2(np.e)
    nh2 = 2 * SSD_HEADS
    diag = (lax.broadcasted_iota(jnp.int32, (nh2, LANES), 0) == lax.broadcasted_iota(jnp.int32, (nh2, LANES), 1))
    to_col = lambda ref: jnp.sum(jnp.where(diag, ref[:, :LANES], 0.0), axis=1, keepdims=True)
    dtb_c = to_col(dtb_ref)
    a_neg_c = -jnp.exp(to_col(alog_ref))
    dt_raw = []
    for r in range(TM // SUB_ROWS):
        rs = slice(r * SUB_ROWS, (r + 1) * SUB_ROWS)
        h = (_rms(_pick_group(i, xp_ref, xs_ref, rs), npm_ref[...]) * (1.0 + sc) + sh).astype(jnp.bfloat16)
        p = _dot(h, w_in_ref[...])
        o = 0
        cq = p[:, o:o + Q_RANK]; o += Q_RANK
        ckv = p[:, o:o + KV_RANK]; o += KV_RANK
        z_ref[rs, :] = p[:, o:o + D_SSD]; o += D_SSD
        xbc_ref[rs, :] = p[:, o:o + CONV_CH]; o += CONV_CH
        kpe = p[:, o:o + LANES]; o += LANES
        dt_raw.append(p[:, o:o + LANES])

        cos = cos_ref[rs, :]
        sin = sin_ref[rs, :]

        def rope(slab):
            return slab * cos + pltpu.roll(slab, LANES - ROPE_DIM, 1) * sin

        ckv_ref[rs, :] = _rms(ckv, kvn_ref[:, :KV_RANK])
        kpe_ref[rs, :] = rope(kpe)
        q = _dot(_rms(cq, qn_ref[:, :Q_RANK]).astype(jnp.bfloat16), w_uq_ref[...]) * scale
        for hd in range(MLA_HEADS):
            sl = slice(hd * HEAD_SLAB, (hd + 1) * HEAD_SLAB)
            q_ref[rs, sl] = rope(q[:, sl]).astype(jnp.bfloat16)

    dt_all = jnp.concatenate(dt_raw, axis=0)
    for ck in range(TM // CHUNK):
        terms = _ssd_head_terms(dt_all[ck * CHUNK:(ck + 1) * CHUNK, :], dtb_c, a_neg_c)
        for ref, val in zip((dts_ref, cumt_ref, wt_ref), terms):
            ref[ck * nh2:(ck + 1) * nh2, :] = val

    @pl.when(i < N_PT)
    def _():
        new_ckv_ref[...] = ckv_ref[...]
        kpe_t = kpe_ref[...].T
        for b in range(TM // SEQ):
            new_kpe_ref[b] = kpe_t[KPE_LANE0:KPE_LANE0 + ROPE_DIM, b * SEQ:(b + 1) * SEQ]


def _inproj(l, vec_table, vec_rows, xp, xs, mod, tabs, w_in, w_uq, q_norm, kv_norm):
    nt = N_TILES
    prompt_blk = lambda i: (jnp.minimum(i, N_PT - 1), 0)
    lat_tiles = DEC_SEQ // TM
    tab_spec = pl.BlockSpec((TM, LANES), lambda i: (jnp.where(i < N_PT, lat_tiles, (i - N_PT) % lat_tiles), 0))
    row = _row_spec
    hrows = TM // CHUNK * 2 * SSD_HEADS
    head_spec = pl.BlockSpec((hrows, CHUNK), lambda i: (i, 0))
    head_shape = jax.ShapeDtypeStruct((nt * hrows, CHUNK), jnp.float32)
    return pl.pallas_call(
        _inproj_kernel,
        grid=(nt,),
        in_specs=_group_specs(D_MODEL) + [
            _mod_spec(l),
            tab_spec, tab_spec,
            _const_spec((D_MODEL, IN_COLS)),
            _const_spec((Q_RANK, MLA_HEADS * HEAD_SLAB)),
            _narrow_spec(q_norm, l // 2),
            _narrow_spec(kv_norm, l // 2),
            _vec_spec(vec_rows["norm_pre_mix"] + l),
            _vec_spec(vec_rows["ssd_dt_bias"] + l // 2),
            _vec_spec(vec_rows["ssd_a_log"] + l // 2),
        ],
        out_specs=[row(MLA_HEADS * HEAD_SLAB), row(KV_RANK), row(LANES), row(D_SSD), row(CONV_CH),
                   head_spec, head_spec, head_spec,
                   pl.BlockSpec((TM, KV_RANK), prompt_blk),
                   pl.BlockSpec((TM // SEQ, ROPE_DIM, SEQ), lambda i: (jnp.minimum(i, N_PT - 1), 0, 0))],
        out_shape=[
            jax.ShapeDtypeStruct((N_TOK, MLA_HEADS * HEAD_SLAB), jnp.bfloat16),
            jax.ShapeDtypeStruct((N_TOK, KV_RANK), jnp.float32),
            jax.ShapeDtypeStruct((N_TOK, LANES), jnp.float32),
            jax.ShapeDtypeStruct((N_TOK, D_SSD), jnp.float32),
            jax.ShapeDtypeStruct((N_TOK, CONV_CH), jnp.float32),
            head_shape, head_shape, head_shape,
            jax.ShapeDtypeStruct((N_PROMPT, KV_RANK), jnp.float32),
            jax.ShapeDtypeStruct((BATCH, ROPE_DIM, SEQ), jnp.float32),
        ],
        compiler_params=pltpu.CompilerParams(dimension_semantics=("arbitrary",), vmem_limit_bytes=VMEM_LIMIT),
        name="inproj",
    )(xp, xs, mod, *tabs, w_in, w_uq, q_norm, kv_norm, *([vec_table] * 3))


def _attn_kernel(*refs, lk_cache, lk_new, n_pack, n_cast):
    n_in = 7 if lk_cache else 5
    if lk_cache:
        q_ref, ckv_ref, kpe_ref, ckvc_ref, kpec_ref, wk_ref, wvt_ref = refs[:n_in]
    else:
        q_ref, ckv_ref, kpe_ref, wk_ref, wvt_ref = refs[:n_in]
    cast_src, o_ref = refs[n_in:n_in + n_cast], refs[n_in + n_cast]
    cast_dst = refs[n_in + n_cast + 1:n_in + 2 * n_cast + 1]
    k_scr, vt_scr = refs[n_in + 2 * n_cast + 1:]
    _cast_chunks(cast_src, cast_dst)
    lk = lk_cache + lk_new

    @pl.when(pl.program_id(1) == 0)
    def _expand_kv():
        def expand(ckv, kpe, r0):
            ckv_b = ckv.astype(jnp.bfloat16)
            kn = _dot(ckv_b, wk_ref[...])
            rows = slice(r0, r0 + ckv.shape[0])
            for hd in range(MLA_HEADS):
                k_scr[hd, rows, :] = (kn[:, hd * HEAD_SLAB:(hd + 1) * HEAD_SLAB] + kpe).astype(jnp.bfloat16)
            vt_scr[:, rows] = _dot_nt(wvt_ref[...], ckv_b).astype(jnp.bfloat16)

        step = 256
        for r0 in range(0, lk_cache, step):
            expand(ckvc_ref[0, r0:r0 + step, :], kpec_ref[0, r0:r0 + step, :], r0)
        for r0 in range(0, n_pack * lk_new, step):
            expand(ckv_ref[r0:r0 + step, :], kpe_ref[r0:r0 + step, :], lk_cache + r0)

    work = [(s, hd) for s in range(n_pack) for hd in range(MLA_HEADS)]
    scores = [_dot_nt(k_scr[hd, s * lk:(s + 1) * lk, :], q_ref[s * TQ:(s + 1) * TQ, hd * HEAD_SLAB:(hd + 1) * HEAD_SLAB])
              for s, hd in work]
    outs = []
    for (s, hd), s_t in zip(work, scores):
        p_t = jnp.exp2(s_t - jnp.max(s_t, axis=0, keepdims=True))
        den = jnp.sum(p_t, axis=0, keepdims=True)
        v_t = vt_scr[hd * V_DIM:(hd + 1) * V_DIM, s * lk:(s + 1) * lk]
        outs.append(_dot(v_t, p_t.astype(jnp.bfloat16)) / den)
    for s in range(n_pack):
        o_ref[s * TQ:(s + 1) * TQ, :] = jnp.concatenate(
            outs[s * MLA_HEADS:(s + 1) * MLA_HEADS], axis=0).T.astype(jnp.bfloat16)


def _attention(q, ckv_n, kpe, w_ukv, row_off, n_batch, seq, cache=None, n_pack=1, cast=()):
    nq = seq // TQ
    assert n_pack == 1 or (nq == 1 and cache is None and n_batch % n_pack == 0)
    n_batch //= n_pack
    lk_cache = 0 if cache is None else cache[0].shape[1]
    lk = n_pack * (lk_cache + seq)
    qblk = lambda b, qi: (row_off // (n_pack * TQ) + b * nq + qi, 0)
    sblk = lambda b, qi: (row_off // (n_pack * seq) + b, 0)
    in_specs = [
        pl.BlockSpec((n_pack * TQ, MLA_HEADS * HEAD_SLAB), qblk),
        pl.BlockSpec((n_pack * seq, KV_RANK), sblk),
        pl.BlockSpec((n_pack * seq, LANES), sblk),
    ]
    args = [q, ckv_n, kpe]
    if cache is not None:
        in_specs += [pl.BlockSpec((1, lk_cache, KV_RANK), lambda b, qi: (b, 0, 0)),
                     pl.BlockSpec((1, lk_cache, LANES), lambda b, qi: (b, 0, 0))]
        args += list(cache)
    in_specs += [_const_spec(w.shape) for w in w_ukv]
    args += list(w_ukv)
    assert not cast or n_batch * nq == N_CAST
    step = lambda b, qi: b * nq + qi
    in_specs += [_cast_in_spec(w, layer, step) for w, layer in cast]
    args += [w for w, _ in cast]
    cast_out = [_cast_out(w, step) for w, _ in cast]
    return pl.pallas_call(
        functools.partial(_attn_kernel, lk_cache=lk_cache, lk_new=seq, n_pack=n_pack, n_cast=len(cast)),
        grid=(n_batch, nq),
        in_specs=in_specs,
        out_specs=[pl.BlockSpec((n_pack * TQ, MLA_HEADS * V_DIM), lambda b, qi: (b * nq + qi, 0))]
        + [spec for spec, _ in cast_out],
        out_shape=[jax.ShapeDtypeStruct((n_batch * n_pack * seq, MLA_HEADS * V_DIM), jnp.bfloat16)]
        + [shape for _, shape in cast_out],
        scratch_shapes=[pltpu.VMEM((MLA_HEADS, lk, HEAD_SLAB), jnp.bfloat16),
                        pltpu.VMEM((MLA_HEADS * V_DIM, lk), jnp.bfloat16)],
        compiler_params=pltpu.CompilerParams(dimension_semantics=("arbitrary", "arbitrary"),
                                             vmem_limit_bytes=VMEM_LIMIT),
        name=f"attention_{seq}",
    )(*args)


def _ssd_kernel(*refs, seq, zero_init, stream_in, row_off):
    if stream_in:
        *refs, xbuf, sems = refs
    if zero_init:
        (xbc_ref, dts_ref, cumt_ref, wt_ref, cw_ref, cb_ref, dsk_ref,
         y_ref, hf_ref, hb_ref, xs_scr, c_scr, bt_scr, cum_scr, stf_scr, stb_scr) = refs
    else:
        (xbc_ref, dts_ref, cumt_ref, wt_ref, h0f_ref, h0b_ref, cw_ref, cb_ref, dsk_ref,
         y_ref, hf_ref, hb_ref, xs_scr, c_scr, bt_scr, cum_scr, stf_scr, stb_scr) = refs
    nc = seq // CHUNK
    gs = SSD_GROUPS * SSD_STATE
    nh2 = 2 * SSD_HEADS

    if stream_in:
        xbc_hbm = xbc_ref
        b = pl.program_id(0)
        slot = b % 2

        def chunk_copy(seq_idx, sl, c):
            src = xbc_hbm.at[pl.ds(pl.multiple_of(row_off + seq_idx * seq + c * CHUNK, CHUNK), CHUNK), :]
            dst = xbuf.at[sl, pl.ds(pl.multiple_of(c * CHUNK, CHUNK), CHUNK), :]
            return pltpu.make_async_copy(src, dst, sems.at[sl, c])

        @pl.when(b == 0)
        def _():
            for c in range(nc):
                chunk_copy(0, 0, c).start()

        @pl.when(b + 1 < pl.num_programs(0))
        def _():
            for c in range(nc):
                chunk_copy(b + 1, 1 - slot, c).start()

        chunk_copy(b, slot, 0).wait()
        xbc_ref = xbuf.at[slot]

    row = lax.broadcasted_iota(jnp.int32, (CHUNK, CHUNK), 0)
    col = lax.broadcasted_iota(jnp.int32, (CHUNK, CHUNK), 1)
    low_half = col < SSD_HEAD_DIM
    lower = row >= col
    upper = row <= col

    def prep_chunk(c, carry):
        r0 = pl.multiple_of(c * CHUNK, CHUNK)
        rows = pl.ds(r0, CHUNK)
        rows_prev = pl.ds(pl.multiple_of(jnp.maximum(r0 - HALO, 0), HALO), HALO)
        rows_next = pl.ds(pl.multiple_of(jnp.minimum(r0 + CHUNK, seq - HALO), HALO), HALO)
        if stream_in:

            @pl.when(c + 1 < nc)
            def _():
                chunk_copy(b, slot, c + 1).wait()

        def conv_tile(cs):
            prev = jnp.where(c > 0, xbc_ref[rows_prev, cs], 0.0)
            nxt = jnp.where(c < nc - 1, xbc_ref[rows_next, cs], 0.0)
            win = jnp.concatenate([prev, xbc_ref[rows, cs], nxt], axis=0)
            acc = jnp.broadcast_to(cb_ref[:, cs], (CHUNK, LANES))
            for k in range(CONV_W):
                lo = HALO - CONV_W // 2 + k
                acc = acc + cw_ref[k:k + 1, cs] * win[lo:lo + CHUNK, :]
            return _silu(acc)

        def x_tile(j, carry):
            cs = pl.ds(pl.multiple_of(j * LANES, LANES), LANES)
            u = conv_tile(cs)
            y_ref[rows, cs] = dsk_ref[:, cs] * u
            xs_scr[0, rows, cs] = jnp.where(low_half, u, 0.0).astype(jnp.bfloat16)
            xs_scr[1, rows, cs] = jnp.where(low_half, 0.0, u).astype(jnp.bfloat16)
            return carry

        lax.fori_loop(0, D_SSD // LANES, x_tile, 0)
        cum_t = cumt_ref[pl.ds(pl.multiple_of(c * nh2, nh2), nh2), :]
        cum_scr[rows, :] = jnp.concatenate(
            [cum_t, jnp.zeros((CHUNK - nh2, CHUNK), jnp.float32)], axis=0).T[:, :nh2]
        for g in range(SSD_GROUPS):
            b0 = pl.multiple_of(c * gs + g * SSD_STATE, SSD_STATE)
            bt_scr[pl.ds(b0, SSD_STATE), :] = conv_tile(slice(D_SSD + g * SSD_STATE, D_SSD + (g + 1) * SSD_STATE)).T
            c_scr[rows, g * SSD_STATE:(g + 1) * SSD_STATE] = conv_tile(
                slice(D_SSD + gs + g * SSD_STATE, D_SSD + gs + (g + 1) * SSD_STATE)).astype(jnp.bfloat16)
        return carry

    lax.fori_loop(0, nc, prep_chunk, 0)

    if zero_init:
        stf_scr[...] = jnp.zeros_like(stf_scr)
        stb_scr[...] = jnp.zeros_like(stb_scr)
    else:
        stf_scr[...] = h0f_ref[0].T
        stb_scr[...] = h0b_ref[0].T

    def scan_open(ci, st_scr):
        rows = pl.ds(pl.multiple_of(ci * CHUNK, CHUNK), CHUNK)
        c_b = c_scr[rows, :]
        st = st_scr[...]
        bts, cbms, zs = [], [], []
        for g in range(SSD_GROUPS):
            cg = c_b[:, g * SSD_STATE:(g + 1) * SSD_STATE]
            bt = bt_scr[pl.ds(pl.multiple_of(ci * gs + g * SSD_STATE, SSD_STATE), SSD_STATE), :]
            gcols = slice(g * SSD_HPG * SSD_HEAD_DIM, (g + 1) * SSD_HPG * SSD_HEAD_DIM)
            bts.append(bt)
            cbms.append(_dot(cg, bt.astype(jnp.bfloat16)))
            zs.append(_dot(cg, st[:, gcols].astype(jnp.bfloat16)))
        return st, bts, cbms, zs

    def scan_pairs(ci, st_scr, reverse, opened):
        st, bts, cbms, zs = opened
        lane0 = SSD_HEADS if reverse else 0
        causal = upper if reverse else lower
        last = 0 if reverse else CHUNK - 1
        rows = pl.ds(pl.multiple_of(ci * CHUNK, CHUNK), CHUNK)
        hrows = pl.ds(pl.multiple_of(ci * nh2, nh2), nh2)
        xs_lo = xs_scr[0, rows, :]
        xs_hi = xs_scr[1, rows, :]
        dts_t = dts_ref[hrows, :]
        cum_t = cumt_ref[hrows, :]
        w_t = wt_ref[hrows, :]
        cum = cum_scr[rows, :]
        for pair in range(SSD_HEADS // 2):
            g, jj = divmod(pair, SSD_HPG // 2)
            pcols = slice(pair * LANES, (pair + 1) * LANES)
            lhs_y, lhs_s, entry = [], [], []
            for hd in (2 * pair, 2 * pair + 1):
                ln = lane0 + hd
                cum_i = jnp.broadcast_to(cum[:, ln:ln + 1], (CHUNK, CHUNK))
                dec = jnp.exp2(jnp.where(causal, cum_i - cum_t[ln:ln + 1, :], -jnp.inf))
                lhs_y.append((cbms[g] * dec * dts_t[ln:ln + 1, :]).astype(jnp.bfloat16))
                lhs_s.append((bts[g] * w_t[ln:ln + 1, :]).astype(jnp.bfloat16))
                entry.append(jnp.exp2(cum_i))
            lhs = jnp.concatenate([jnp.concatenate(lhs_y, axis=1), jnp.concatenate(lhs_s, axis=1)], axis=0)
            out = _dot(lhs, jnp.concatenate([xs_lo[:, pcols], xs_hi[:, pcols]], axis=0))
            ea = jnp.where(low_half, entry[0], entry[1])
            y_ref[rows, pcols] += out[:CHUNK] + zs[g][:, jj * LANES:(jj + 1) * LANES] * ea
            st_scr[:, pcols] = ea[last:last + 1, :] * st[:, pcols] + out[CHUNK:]

    def both(c, carry):
        opened_f = scan_open(c, stf_scr)
        opened_b = scan_open(nc - 1 - c, stb_scr)
        scan_pairs(c, stf_scr, False, opened_f)
        scan_pairs(nc - 1 - c, stb_scr, True, opened_b)
        return carry

    lax.fori_loop(0, nc, both, 0)
    hf_ref[0] = stf_scr[...].T
    hb_ref[0] = stb_scr[...].T


def _ssd(i_ab, xbc, head_terms, h0, conv_w, vec_table, conv_b_row, d_skip, row_off, n_batch, seq):
    assert CONV_CH == D_MODEL
    hp = SSD_HEADS * SSD_HEAD_DIM
    gs = SSD_GROUPS * SSD_STATE
    nc = seq // CHUNK
    seq_blk = lambda b: (row_off // seq + b, 0)
    head_spec = pl.BlockSpec((nc * 2 * SSD_HEADS, CHUNK), seq_blk)
    st_spec = pl.BlockSpec((1, hp, SSD_STATE), lambda b: (b, 0, 0))
    st_shape = jax.ShapeDtypeStruct((n_batch, hp, SSD_STATE), jnp.float32)
    h0 = () if h0 is None else tuple(h0)
    stream_in = nc > 2
    xbc_spec = pl.BlockSpec(memory_space=pl.ANY) if stream_in else pl.BlockSpec((seq, CONV_CH), seq_blk)
    stream_scratch = [pltpu.VMEM((2, seq, CONV_CH), jnp.float32), pltpu.SemaphoreType.DMA((2, nc))] if stream_in else []
    return pl.pallas_call(
        functools.partial(_ssd_kernel, seq=seq, zero_init=not h0, stream_in=stream_in, row_off=row_off),
        grid=(n_batch,),
        in_specs=[xbc_spec, head_spec, head_spec, head_spec] + [st_spec] * len(h0) + [
            pl.BlockSpec((None, CONV_W, CONV_CH), lambda b: (i_ab, 0, 0), pipeline_mode=pl.Buffered(1)),
            _vec_spec(conv_b_row + i_ab), _const_spec((1, D_SSD))],
        out_specs=[pl.BlockSpec((seq, D_SSD), lambda b: (b, 0)), st_spec, st_spec],
        out_shape=[jax.ShapeDtypeStruct((n_batch * seq, D_SSD), jnp.float32), st_shape, st_shape],
        scratch_shapes=[pltpu.VMEM((2, seq, D_SSD), jnp.bfloat16),
                        pltpu.VMEM((seq, gs), jnp.bfloat16),
                        pltpu.VMEM((nc * gs, CHUNK), jnp.float32),
                        pltpu.VMEM((seq, 2 * SSD_HEADS), jnp.float32),
                        pltpu.VMEM((SSD_STATE, hp), jnp.float32),
                        pltpu.VMEM((SSD_STATE, hp), jnp.float32)] + stream_scratch,
        compiler_params=pltpu.CompilerParams(dimension_semantics=("arbitrary",), vmem_limit_bytes=VMEM_LIMIT),
        name=f"ssd_{seq}",
    )(xbc, *head_terms, *h0, conv_w, vec_table, d_skip)


def _post_mix(x, mix, mod_ref, npost_ref, npre_ref):
    d = D_MODEL
    gate_mix = _mod_part(mod_ref, 2)
    shf = _mod_part(mod_ref, 3)
    scf = _mod_part(mod_ref, 4)
    x1 = x + gate_mix * _rms(mix, npost_ref[...])
    return x1, (_rms(x1, npre_ref[...]) * (1.0 + scf) + shf).astype(jnp.bfloat16)


def _subtile_pipeline(n_sub, mixer_pre, mixer_dots, mod_ref, nffn_ref, wg_ref, wu_ref, wd_ref, interleave):
    gate_ffn = _mod_part(mod_ref, 5)
    ffn_up = lambda h: (_silu(_dot(h, wg_ref[...])) * _dot(h, wu_ref[...])).astype(jnp.bfloat16)
    ffn_down = lambda x1, hid: x1 + gate_ffn * _rms(_dot(hid, wd_ref[...]), nffn_ref[...])
    if not interleave:
        staged = [mixer_dots(r, mixer_pre(r)) for r in range(n_sub)]
        return [ffn_down(x1, ffn_up(h)) for x1, h in staged]
    outs = []
    x1, h = mixer_dots(0, mixer_pre(0))
    for r in range(n_sub):
        nxt_pre = mixer_pre(r + 1) if r + 1 < n_sub else None
        hid = ffn_up(h)
        nxt = mixer_dots(r + 1, nxt_pre) if r + 1 < n_sub else None
        outs.append(ffn_down(x1, hid))
        if nxt is not None:
            x1, h = nxt
    return outs


def _ffn_specs(l, vec_rows):
    return [_vec_spec(vec_rows[name] + l) for name in ("norm_post_mix", "norm_pre_ffn", "norm_post_ffn")] + [
        _const_spec((D_MODEL, D_FF)), _const_spec((D_MODEL, D_FF)), _const_spec((D_FF, D_MODEL))]


def _outproj_ffn_kernel(*refs, n_cast):
    (xp_ref, xs_ref, attp_ref, atts_ref, yp_ref, ys_ref, z_ref, mod_ref, sn_ref, wo_ref,
     npost_ref, npre_ref, nffn_ref, wg_ref, wu_ref, wd_ref) = refs[:16]
    cast_src, o_ref, cast_dst = refs[16:16 + n_cast], refs[16 + n_cast], refs[17 + n_cast:]
    i = pl.program_id(0)
    _cast_chunks(cast_src, cast_dst)
    gw = D_SSD // SSD_GROUPS
    sub = lambda r: slice(r * SUB_ROWS, (r + 1) * SUB_ROWS)

    def mixer_pre(r):
        yg = _pick_group(i, yp_ref, ys_ref, sub(r)) * _silu(z_ref[sub(r), :])
        parts = [_pick_group(i, attp_ref, atts_ref, sub(r))]
        for g in range(SSD_GROUPS):
            parts.append(_rms(yg[:, g * gw:(g + 1) * gw], sn_ref[:, g * gw:(g + 1) * gw]).astype(jnp.bfloat16))
        return jnp.concatenate(parts, axis=1)

    def mixer_dots(r, cat):
        return _post_mix(_pick_group(i, xp_ref, xs_ref, sub(r)), _dot(cat, wo_ref[...]),
                         mod_ref, npost_ref, npre_ref)

    outs = _subtile_pipeline(TM // SUB_ROWS, mixer_pre, mixer_dots, mod_ref, nffn_ref, wg_ref, wu_ref, wd_ref,
                             interleave=False)
    for r, res in enumerate(outs):
        o_ref[sub(r), :] = res


def _outproj_ffn(l, i_ab, vec_table, vec_rows, ssd_norm, xp, xs, att_p, att_s, y_p, y_s, z, mod, w_out, wg, wu, wd,
                 cast):
    d_cat = MLA_HEADS * V_DIM + D_SSD
    assert not cast or N_TILES == N_CAST
    cast_out = [_cast_out(w) for w, _ in cast]
    return pl.pallas_call(
        functools.partial(_outproj_ffn_kernel, n_cast=len(cast)),
        grid=(N_TILES,),
        in_specs=(_group_specs(D_MODEL) + _group_specs(MLA_HEADS * V_DIM) + _group_specs(D_SSD)
                  + [_row_spec(D_SSD), _mod_spec(l), _narrow_spec(ssd_norm, i_ab),
                     _const_spec((d_cat, D_MODEL))] + _ffn_specs(l, vec_rows)
                  + [_cast_in_spec(w, layer) for w, layer in cast]),
        out_specs=[_row_spec(D_MODEL)] + [spec for spec, _ in cast_out],
        out_shape=[jax.ShapeDtypeStruct((N_TOK, D_MODEL), jnp.float32)] + [shape for _, shape in cast_out],
        compiler_params=pltpu.CompilerParams(dimension_semantics=("arbitrary",), vmem_limit_bytes=VMEM_LIMIT),
        name="outproj_ffn",
    )(xp, xs, att_p, att_s, y_p, y_s, z, mod, ssd_norm, w_out, vec_table, vec_table, vec_table, wg, wu, wd,
      *[w for w, _ in cast])


def _pool_ffn_kernel(x_ref, xp_ref, xn_ref, mod_ref, nmix_ref, pw_ref, ps_ref,
                     npost_ref, npre_ref, nffn_ref, wg_ref, wu_ref, wd_ref, op_ref, os_ref):
    i = pl.program_id(0)

    @pl.when(i == 0)
    def _():
        os_ref[...] = jnp.zeros_like(os_ref)

    seq = jnp.where(i < N_PROMPT // TM, SEQ, DEC_SEQ)
    pos0 = (i * TM) % seq
    sh = _mod_part(mod_ref, 0)
    sc = _mod_part(mod_ref, 1)
    hmod = lambda v: _rms(v, nmix_ref[...]) * (1.0 + sc) + sh
    n_rows = POOL_SUB + 2 * HALO

    def shifted(v, s):
        return pltpu.roll(v, n_rows - s, 0)

    def mixer_pre(s):
        lo, hi = s * POOL_SUB, (s + 1) * POOL_SUB
        pos_s = (pos0 + lo) % seq
        h = hmod(x_ref[lo:hi, :])
        before = hmod(xp_ref[...] if s == 0 else x_ref[lo - HALO:lo, :])
        after = hmod(xn_ref[...] if hi == TM else x_ref[hi:hi + HALO, :])
        before = jnp.where(pos_s > 0, before, 0.0)
        after = jnp.where(pos_s + POOL_SUB < seq, after, 0.0)
        padded = jnp.concatenate([before, h, after], axis=0)
        pos = pos_s + lax.broadcasted_iota(jnp.int32, (POOL_SUB, 1), 0)
        pooled = []
        for gi, w in enumerate(POOL_WINDOWS):
            cols = slice(gi * POOL_GC, (gi + 1) * POOL_GC)
            t = padded[:, cols]
            span = 1
            while span < w:
                t = t + shifted(t, span)
                span *= 2
            lead = HALO - w // 2
            win_sum = (shifted(t, lead) if lead else t)[:POOL_SUB, :]
            cnt = (jnp.minimum(pos + w // 2, seq) - jnp.maximum(pos - w // 2, 0)).astype(jnp.float32)
            pooled.append((win_sum / cnt - h[:, cols]).astype(jnp.bfloat16))
        return pooled

    def mixer_dots(s, pooled):
        mix = jnp.concatenate([_dot(p, pw_ref[gi]) for gi, p in enumerate(pooled)], axis=1) * ps_ref[...]
        return _post_mix(x_ref[s * POOL_SUB:(s + 1) * POOL_SUB, :], mix, mod_ref, npost_ref, npre_ref)

    res = jnp.concatenate(_subtile_pipeline(TM // POOL_SUB, mixer_pre, mixer_dots, mod_ref, nffn_ref,
                                            wg_ref, wu_ref, wd_ref, interleave=True), axis=0)

    @pl.when(i < N_PT)
    def _():
        op_ref[...] = res

    @pl.when(i >= N_PT)
    def _():
        os_ref[...] = res


def _pool_ffn(l, j_c, vec_table, vec_rows, xa, mod, pool_w, wg, wu, wd):
    hb = TM // HALO
    nh = N_TOK // HALO
    return pl.pallas_call(
        _pool_ffn_kernel,
        grid=(N_TILES,),
        in_specs=[_row_spec(D_MODEL),
                  pl.BlockSpec((HALO, D_MODEL), lambda i: (jnp.maximum(i * hb - 1, 0), 0)),
                  pl.BlockSpec((HALO, D_MODEL), lambda i: (jnp.minimum((i + 1) * hb, nh - 1), 0)),
                  _mod_spec(l),
                  _vec_spec(vec_rows["norm_pre_mix"] + l),
                  _const_spec((len(POOL_WINDOWS), POOL_GC, POOL_GC)),
                  _vec_spec(vec_rows["pool_scale"] + j_c)] + _ffn_specs(l, vec_rows),
        out_specs=_group_specs(D_MODEL),
        out_shape=[jax.ShapeDtypeStruct((N_PROMPT, D_MODEL), jnp.float32),
                   jax.ShapeDtypeStruct((N_SAMPLE, D_MODEL), jnp.float32)],
        compiler_params=pltpu.CompilerParams(dimension_semantics=("arbitrary",), vmem_limit_bytes=VMEM_LIMIT),
        name="pool_ffn",
    )(xa, xa, xa, mod, vec_table, pool_w, vec_table, vec_table, vec_table, vec_table, wg, wu, wd)


def _rope_tables():
    f32 = np.float32
    rows = DEC_SEQ // GRID_W
    r = np.repeat(np.arange(rows, dtype=f32), GRID_W)
    c = np.tile(np.arange(GRID_W, dtype=f32), rows)
    half = ROPE_DIM // 2
    inv_freq = np.power(f32(ROPE_THETA), -np.arange(0, half, 2, dtype=f32) / f32(half)).astype(f32)
    ang = np.concatenate([r[:, None] * inv_freq, c[:, None] * inv_freq], axis=-1).astype(f32)
    cos, sin = np.cos(ang).astype(f32), np.sin(ang).astype(f32)
    ones = np.ones((DEC_SEQ, KPE_LANE0), f32)
    zl = np.zeros((DEC_SEQ, KPE_LANE0), f32)
    zr = np.zeros((DEC_SEQ, LANES - KPE_LANE0 - ROPE_DIM), f32)
    cos_t = np.concatenate([ones, cos, cos, zr], axis=1)
    sin_t = np.concatenate([zl, -sin, sin, zr], axis=1)
    keep = np.concatenate([np.ones((TM, KPE_LANE0 + ROPE_DIM), f32), np.zeros((TM, LANES - KPE_LANE0 - ROPE_DIM), f32)],
                          axis=1)
    return (jnp.asarray(np.concatenate([cos_t, keep], axis=0)),
            jnp.asarray(np.concatenate([sin_t, np.zeros((TM, LANES), f32)], axis=0)))


def _kpe_slab(k):
    pad = [(0, 0)] * (k.ndim - 1) + [(KPE_LANE0, LANES - KPE_LANE0 - ROPE_DIM)]
    return jnp.pad(k, pad)


def _layout_in_proj(wt):
    o = np.cumsum((0, Q_RANK, KV_RANK, ROPE_DIM, D_SSD, CONV_CH, SSD_HEADS, SSD_HEADS))
    cq, ckv, kpe, z, xbc, dtf, dtb = (wt[o[k]:o[k + 1], :] for k in range(7))
    zeros = lambda n: jnp.zeros((n, wt.shape[1]), wt.dtype)
    half = ROPE_DIM // 2
    assert KPE_LANE0 + 2 * ROPE_DIM == LANES
    stacked = jnp.concatenate([cq, ckv, z, xbc,
                               zeros(KPE_LANE0), kpe, kpe[half:], kpe[:half],
                               dtf, dtb, zeros(LANES - 2 * SSD_HEADS)], axis=0)
    return stacked.T.astype(jnp.bfloat16)


def _layout_uq(w):
    hw, half = NOPE_DIM + ROPE_DIM, ROPE_DIM // 2
    assert hw + ROPE_DIM == HEAD_SLAB
    pieces = []
    for hd in range(MLA_HEADS):
        s = w[:, hd * hw:(hd + 1) * hw]
        pieces += [s, s[:, NOPE_DIM + half:], s[:, NOPE_DIM:NOPE_DIM + half]]
    return jnp.concatenate(pieces, axis=1).astype(jnp.bfloat16)


def _layout_ukv(w):
    hw = NOPE_DIM + V_DIM
    pad = jnp.zeros((w.shape[0], HEAD_SLAB - NOPE_DIM), w.dtype)
    kn, v = [], []
    for hd in range(MLA_HEADS):
        kn += [w[:, hd * hw:hd * hw + NOPE_DIM], pad]
        v.append(w[:, hd * hw + NOPE_DIM:(hd + 1) * hw])
    return (jnp.concatenate(kn, axis=1).astype(jnp.bfloat16),
            jnp.concatenate(v, axis=1).T.astype(jnp.bfloat16))


def kernel(x_prompt, x_sample, c, cache_mla_ckv, cache_mla_krope, state_ssd_fwd, state_ssd_bwd, c_ctx, w_mod, b_mod, norm_pre_mix, norm_post_mix, norm_pre_ffn, norm_post_ffn, w_in_ab, q_norm, w_uq, kv_norm, w_ukv, ssd_conv_w, ssd_conv_b, ssd_dt_bias_fwd, ssd_dt_bias_bwd, ssd_a_log_fwd, ssd_a_log_bwd, ssd_d, ssd_norm, w_out_ab, pool_w, pool_scale, ffn_w_gate, ffn_w_up, ffn_w_down):
    f32, bf16 = jnp.float32, jnp.bfloat16
    assert DEPTH == 2
    xp = x_prompt.reshape(N_PROMPT, D_MODEL)
    xs = x_sample.reshape(N_SAMPLE, D_MODEL)
    assert 1 + c.shape[0] == N_MODVEC
    mod, w_in, w_uq_slabs, *w_kv = _modulation(c_ctx, c, w_mod, b_mod, jnp.swapaxes(w_in_ab, 1, 2), w_uq, w_ukv)
    tabs = _rope_tables()
    hp = SSD_HEADS * SSD_HEAD_DIM
    row = lambda v: v.reshape(1, -1)
    new_ckv, new_kpe, new_hf, new_hb = [], [], [], []

    vec_table, vec_rows = _pack_vectors(
        norm_pre_mix=norm_pre_mix, norm_post_mix=norm_post_mix, norm_pre_ffn=norm_pre_ffn, norm_post_ffn=norm_post_ffn,
        ssd_conv_b=ssd_conv_b, pool_scale=pool_scale,
        ssd_dt_bias=jnp.concatenate([ssd_dt_bias_fwd, ssd_dt_bias_bwd], axis=1),
        ssd_a_log=jnp.concatenate([ssd_a_log_fwd, ssd_a_log_bwd], axis=1))
    ffn_w_f32 = (ffn_w_gate, ffn_w_up, ffn_w_down)
    for l in range(DEPTH):
        if l % 2 == 0:
            i = l // 2
            q, ckv_n, kpe, z, xbc, dts_t, cum_t, w_t, ckv_prompt, kpe_prompt = _inproj(
                l, vec_table, vec_rows, xp, xs, mod, tabs, w_in, w_uq_slabs, q_norm, kv_norm)
            heads = (dts_t, cum_t, w_t)
            (att_p,) = _attention(q, ckv_n, kpe, w_kv, 0, BATCH, SEQ, n_pack=PROMPT_PACK)
            att_s, w_out, *ffn_w = _attention(q, ckv_n, kpe, w_kv, N_PROMPT, DEC_BATCH, DEC_SEQ,
                                              cache=(cache_mla_ckv[:, i], _kpe_slab(cache_mla_krope[:, i])),
                                              cast=[(w_out_ab, i)] + [(w, l) for w in ffn_w_f32])
            ssd_args = (ssd_conv_w, vec_table, vec_rows["ssd_conv_b"], row(jnp.repeat(ssd_d[i], SSD_HEAD_DIM)))
            y_p, hf, hb = _ssd(i, xbc, heads, None, *ssd_args, 0, BATCH, SEQ)
            y_s, _, _ = _ssd(i, xbc, heads, (state_ssd_fwd[:, i].reshape(DEC_BATCH, hp, SSD_STATE),
                                             state_ssd_bwd[:, i].reshape(DEC_BATCH, hp, SSD_STATE)),
                             *ssd_args, N_PROMPT, DEC_BATCH, DEC_SEQ)
            xa, *ffn_w = _outproj_ffn(l, i, vec_table, vec_rows, ssd_norm, xp, xs, att_p, att_s, y_p, y_s, z, mod, w_out, *ffn_w,
                                      cast=[(w, l + 1) for w in ffn_w_f32])
            new_ckv.append(ckv_prompt.reshape(BATCH, SEQ, KV_RANK))
            new_kpe.append(jnp.swapaxes(kpe_prompt, 1, 2))
            new_hf.append(hf.reshape(BATCH, SSD_HEADS, SSD_HEAD_DIM, SSD_STATE))
            new_hb.append(hb.reshape(BATCH, SSD_HEADS, SSD_HEAD_DIM, SSD_STATE))
        else:
            j = l // 2
            yp, ys = _pool_ffn(l, j, vec_table, vec_rows, xa, mod, pool_w[j].astype(bf16), *ffn_w)

    return (yp.reshape(BATCH, SEQ, D_MODEL), ys.reshape(DEC_BATCH, DEC_SEQ, D_MODEL),
            jnp.stack(new_ckv, axis=1), jnp.stack(new_kpe, axis=1),
            jnp.stack(new_hf, axis=1), jnp.stack(new_hb, axis=1))
```

```python
import functools

import numpy as np
import jax
import jax.numpy as jnp
from jax import lax
from jax.experimental import pallas as pl
from jax.experimental.pallas import tpu as pltpu

D_MODEL = 1024
BATCH = 16
SEQ = 256
DEPTH = 2
DEC_BATCH = 2
DEC_SEQ = 2048
PAST_LEN = 256
GRID_W = 64
EPS = 1e-6
MLA_HEADS = 8
Q_RANK = 256
KV_RANK = 256
NOPE_DIM = 64
ROPE_DIM = 32
V_DIM = 64
ROPE_THETA = 10000.0
SSD_HEADS = 8
SSD_GROUPS = 2
SSD_HPG = SSD_HEADS // SSD_GROUPS
SSD_HEAD_DIM = 64
SSD_STATE = 128
D_SSD = SSD_HEADS * SSD_HEAD_DIM
CONV_W = 5
CONV_CH = D_SSD + 2 * SSD_GROUPS * SSD_STATE
POOL_WINDOWS = (2, 4, 8, 16)
POOL_GC = D_MODEL // len(POOL_WINDOWS)
D_FF = ((8 * D_MODEL + 3 * 256 - 1) // (3 * 256)) * 256

SUBLANES = 8
LANES = 128

N_PROMPT = BATCH * SEQ
N_SAMPLE = DEC_BATCH * DEC_SEQ
N_TOK = N_PROMPT + N_SAMPLE
N_MODVEC = 1 + DEC_BATCH
TM = 512
TQ = 256
CHUNK = 128
HALO = SUBLANES
HEAD_SLAB = LANES
IN_COLS = Q_RANK + KV_RANK + D_SSD + CONV_CH + 2 * LANES
KPE_LANE0 = NOPE_DIM
VMEM_LIMIT = 56 * 1024 * 1024

POOL_SUB = min(SEQ, DEC_SEQ)

assert TM % POOL_SUB == 0 and SEQ % POOL_SUB == 0 and DEC_SEQ % TM == 0 and N_PROMPT % DEC_SEQ == 0


def _rms(x, g):
    return x * lax.rsqrt(jnp.mean(x * x, axis=-1, keepdims=True) + EPS) * g


def _silu(x):
    return x * jax.nn.sigmoid(x)


def _softplus(x):
    return jnp.maximum(x, 0.0) + jnp.log1p(jnp.exp(-jnp.abs(x)))


def _dot(a, b):
    return jnp.dot(a, b, preferred_element_type=jnp.float32)


def _dot_nt(a, b):
    return lax.dot_general(a, b, (((1,), (1,)), ((), ())), preferred_element_type=jnp.float32)


def _mod_row(i):
    return jnp.where(i < N_PROMPT // TM, 0, 1 + (i - N_PROMPT // TM) // (DEC_SEQ // TM))


def _const_spec(shape):
    nd = len(shape)
    return pl.BlockSpec(shape, lambda *_: (0,) * nd, pipeline_mode=pl.Buffered(1))


N_PT = N_PROMPT // TM
N_TILES = N_TOK // TM


def _row_spec(width):
    return pl.BlockSpec((TM, width), lambda i: (i, 0))


def _mod_spec(l):
    return pl.BlockSpec((None, SUBLANES, 6 * D_MODEL), lambda i: (l, 0, 0), pipeline_mode=pl.Buffered(1))


def _mod_part(mod_ref, k):
    return mod_ref[pl.ds(_mod_row(pl.program_id(0)), 1), k * D_MODEL:(k + 1) * D_MODEL]


def _vec_spec(row):
    return pl.BlockSpec((None, 1, D_MODEL), lambda *_: (row, 0, 0), pipeline_mode=pl.Buffered(1))


def _narrow_spec(a, layer):
    assert a.shape[0] == 1
    return pl.BlockSpec((1, a.shape[1]), lambda *_: (layer, 0), pipeline_mode=pl.Buffered(1))


def _pack_vectors(**params):
    first_row, blocks, n = {}, [], 0
    for name, a in params.items():
        first_row[name] = n
        n += a.shape[0]
        blocks.append(jnp.pad(a, ((0, 0), (0, D_MODEL - a.shape[1]))))
    return jnp.concatenate(blocks, axis=0).reshape(n, 1, D_MODEL), first_row


def _group_specs(width):
    return [pl.BlockSpec((TM, width), lambda i: (jnp.minimum(i, N_PT - 1), 0)),
            pl.BlockSpec((TM, width), lambda i: (jnp.maximum(i - N_PT, 0), 0))]


N_CAST = 16


def _cast_in_spec(w, layer, step=lambda i: i):
    _, rows, cols = w.shape
    return pl.BlockSpec((1, rows // N_CAST, cols), lambda *g: (layer, step(*g), 0))


def _cast_out(w, step=lambda i: i):
    _, rows, cols = w.shape
    return (pl.BlockSpec((rows // N_CAST, cols), lambda *g: (step(*g), 0)),
            jax.ShapeDtypeStruct((rows, cols), jnp.bfloat16))


def _cast_chunks(src_refs, dst_refs):
    for src, dst in zip(src_refs, dst_refs):
        dst[...] = src[0].astype(jnp.bfloat16)


def _pick_group(i, p_ref, s_ref, rows=slice(None)):
    return jnp.where(i < N_PT, p_ref[rows, :], s_ref[rows, :])


MOD_TK = 256
MOD_STREAMS = 2
PROMPT_PACK = 4
MOD_ROWS = 2 * SUBLANES
SUB_ROWS = 256


def _split3(a):
    a_hi = a.astype(jnp.bfloat16)
    r1 = a - a_hi.astype(jnp.float32)
    a_mid = r1.astype(jnp.bfloat16)
    a_lo = (r1 - a_mid.astype(jnp.float32)).astype(jnp.bfloat16)
    return a_hi, a_mid, a_lo


def _mod_kernel(cctx_ref, c_ref, *refs):
    w_refs, (b_ref, win_ref, wuq_ref, wukv_ref, o_ref, wl_ref, uq_ref, wk_ref, wvt_ref) = (
        refs[:MOD_STREAMS], refs[MOD_STREAMS:])
    wl_ref[...] = _layout_in_proj(win_ref[0])
    k = pl.program_id(1)

    @pl.when((pl.program_id(0) == 0) & (k == 0))
    def _():
        uq_ref[...] = _layout_uq(wuq_ref[0])
        wk_ref[...], wvt_ref[...] = _layout_ukv(wukv_ref[0])

    rows = MOD_TK // MOD_STREAMS
    rowid = lax.broadcasted_iota(jnp.int32, (MOD_ROWS, MOD_TK), 0)
    cv = jnp.where(rowid == 0, cctx_ref[...], 0.0)
    for b in range(c_ref.shape[0]):
        cv = jnp.where(rowid == 1 + b, c_ref[b:b + 1, :], cv)
    s_all = _silu(cv)
    part = jnp.zeros((MOD_ROWS, 6 * D_MODEL), jnp.float32)
    for j, w_ref in enumerate(w_refs):
        s_hi, s_mid, s_lo = _split3(s_all[:, j * rows:(j + 1) * rows])
        w = w_ref[0]
        w_hi = w.astype(jnp.bfloat16)
        w_lo = (w - w_hi.astype(jnp.float32)).astype(jnp.bfloat16)
        top = _dot(jnp.concatenate([s_hi, s_mid, s_lo], axis=0), w_hi)
        low = _dot(jnp.concatenate([s_hi, s_mid], axis=0), w_lo)
        part = part + (top[:MOD_ROWS] + top[MOD_ROWS:2 * MOD_ROWS] + top[2 * MOD_ROWS:]
                       + low[:MOD_ROWS] + low[MOD_ROWS:])
    part = part[:SUBLANES]

    @pl.when(k == 0)
    def _():
        o_ref[0] = part + b_ref[pl.ds(pl.program_id(0), 1), :]

    @pl.when(k > 0)
    def _():
        o_ref[0] += part


def _modulation(c_ctx, c, w_mod, b_mod, w_in_t, w_uq, w_ukv):
    n = 6 * D_MODEL
    nk = D_MODEL // MOD_TK
    chunk = D_MODEL // (DEPTH * nk)
    assert w_in_t.shape[0] == 1 and w_uq.shape[0] == 1 and w_ukv.shape[0] == 1
    whole = lambda shape: pl.BlockSpec(shape, lambda l, k: (0,) * len(shape))
    uq_shape = (Q_RANK, MLA_HEADS * HEAD_SLAB)
    wvt_shape = (MLA_HEADS * V_DIM, KV_RANK)
    assert Q_RANK == KV_RANK
    w_spec = lambda j: pl.BlockSpec((1, MOD_TK // MOD_STREAMS, n), lambda l, k: (l, k * MOD_STREAMS + j, 0))
    return pl.pallas_call(
        _mod_kernel,
        grid=(DEPTH, D_MODEL // MOD_TK),
        in_specs=[pl.BlockSpec((1, MOD_TK), lambda l, k: (0, k)),
                  pl.BlockSpec((c.shape[0], MOD_TK), lambda l, k: (0, k))]
        + [w_spec(j) for j in range(MOD_STREAMS)]
        + [pl.BlockSpec((DEPTH, n), lambda l, k: (0, 0)),
           pl.BlockSpec((1, w_in_t.shape[1], chunk), lambda l, k: (0, 0, l * nk + k)),
           whole(w_uq.shape), whole(w_ukv.shape)],
        out_specs=[pl.BlockSpec((1, SUBLANES, n), lambda l, k: (l, 0, 0)),
                   pl.BlockSpec((chunk, IN_COLS), lambda l, k: (l * nk + k, 0)),
                   whole(uq_shape), whole(uq_shape), whole(wvt_shape)],
        out_shape=[jax.ShapeDtypeStruct((DEPTH, SUBLANES, n), jnp.float32),
                   jax.ShapeDtypeStruct((D_MODEL, IN_COLS), jnp.bfloat16),
                   jax.ShapeDtypeStruct(uq_shape, jnp.bfloat16),
                   jax.ShapeDtypeStruct(uq_shape, jnp.bfloat16),
                   jax.ShapeDtypeStruct(wvt_shape, jnp.bfloat16)],
        compiler_params=pltpu.CompilerParams(dimension_semantics=("arbitrary", "arbitrary"),
                                             vmem_limit_bytes=VMEM_LIMIT),
        name="modulation",
    )(c_ctx.reshape(1, D_MODEL), c, *([w_mod] * MOD_STREAMS), b_mod, w_in_t, w_uq, w_ukv)


def _ssd_head_terms(dt_raw, dtb_c, a_neg_c):
    nh2 = 2 * SSD_HEADS
    row = lax.broadcasted_iota(jnp.int32, (CHUNK, CHUNK), 0)
    col = lax.broadcasted_iota(jnp.int32, (CHUNK, CHUNK), 1)
    upper_b = (row <= col).astype(jnp.bfloat16)
    lower_b = (row >= col).astype(jnp.bfloat16)
    fwd_rows = lax.broadcasted_iota(jnp.int32, (nh2, CHUNK), 0) < SSD_HEADS
    dts_t = _softplus(dt_raw.T[:nh2, :] + dtb_c)
    pieces = _split3(dts_t * a_neg_c)
    cum_t = jnp.where(fwd_rows, sum(_dot(p, upper_b) for p in pieces), sum(_dot(p, lower_b) for p in pieces))
    cum_t = cum_t * np.float32(np.log2(np.e))
    tot = jnp.where(fwd_rows[:, :1], cum_t[:, CHUNK - 1:], cum_t[:, :1])
    return dts_t, cum_t, dts_t * jnp.exp2(tot - cum_t)


def _inproj_kernel(xp_ref, xs_ref, mod_ref, cos_ref, sin_ref, w_in_ref, w_uq_ref, qn_ref, kvn_ref, npm_ref,
                   dtb_ref, alog_ref, q_ref, ckv_ref, kpe_ref, z_ref, xbc_ref, dts_ref, cumt_ref, wt_ref,
                   new_ckv_ref, new_kpe_ref):
    i = pl.program_id(0)
    sh = _mod_part(mod_ref, 0)
    sc = _mod_part(mod_ref, 1)
    scale = (NOPE_DIM + ROPE_DIM) ** -0.5 * np.log2(np.e)
    nh2 = 2 * SSD_HEADS
    diag = (lax.broadcasted_iota(jnp.int32, (nh2, LANES), 0) == lax.broadcasted_iota(jnp.int32, (nh2, LANES), 1))
    to_col = lambda ref: jnp.sum(jnp.where(diag, ref[:, :LANES], 0.0), axis=1, keepdims=True)
    dtb_c = to_col(dtb_ref)
    a_neg_c = -jnp.exp(to_col(alog_ref))
    dt_raw = []
    for r in range(TM // SUB_ROWS):
        rs = slice(r * SUB_ROWS, (r + 1) * SUB_ROWS)
        h = (_rms(_pick_group(i, xp_ref, xs_ref, rs), npm_ref[...]) * (1.0 + sc) + sh).astype(jnp.bfloat16)
        p = _dot(h, w_in_ref[...])
        o = 0
        cq = p[:, o:o + Q_RANK]; o += Q_RANK
        ckv = p[:, o:o + KV_RANK]; o += KV_RANK
        z_ref[rs, :] = p[:, o:o + D_SSD]; o += D_SSD
        xbc_ref[rs, :] = p[:, o:o + CONV_CH]; o += CONV_CH
        kpe = p[:, o:o + LANES]; o += LANES
        dt_raw.append(p[:, o:o + LANES])

        cos = cos_ref[rs, :]
        sin = sin_ref[rs, :]

        def rope(slab):
            return slab * cos + pltpu.roll(slab, LANES - ROPE_DIM, 1) * sin

        ckv_ref[rs, :] = _rms(ckv, kvn_ref[:, :KV_RANK])
        kpe_ref[rs, :] = rope(kpe)
        q = _dot(_rms(cq, qn_ref[:, :Q_RANK]).astype(jnp.bfloat16), w_uq_ref[...]) * scale
        for hd in range(MLA_HEADS):
            sl = slice(hd * HEAD_SLAB, (hd + 1) * HEAD_SLAB)
            q_ref[rs, sl] = rope(q[:, sl]).astype(jnp.bfloat16)

    dt_all = jnp.concatenate(dt_raw, axis=0)
    for ck in range(TM // CHUNK):
        terms = _ssd_head_terms(dt_all[ck * CHUNK:(ck + 1) * CHUNK, :], dtb_c, a_neg_c)
        for ref, val in zip((dts_ref, cumt_ref, wt_ref), terms):
            ref[ck * nh2:(ck + 1) * nh2, :] = val

    @pl.when(i < N_PT)
    def _():
        new_ckv_ref[...] = ckv_ref[...]
        kpe_t = kpe_ref[...].T
        for b in range(TM // SEQ):
            new_kpe_ref[b] = kpe_t[KPE_LANE0:KPE_LANE0 + ROPE_DIM, b * SEQ:(b + 1) * SEQ]


def _inproj(l, vec_table, vec_rows, xp, xs, mod, tabs, w_in, w_uq, q_norm, kv_norm):
    nt = N_TILES
    prompt_blk = lambda i: (jnp.minimum(i, N_PT - 1), 0)
    lat_tiles = DEC_SEQ // TM
    tab_spec = pl.BlockSpec((TM, LANES), lambda i: (jnp.where(i < N_PT, lat_tiles, (i - N_PT) % lat_tiles), 0))
    row = _row_spec
    hrows = TM // CHUNK * 2 * SSD_HEADS
    head_spec = pl.BlockSpec((hrows, CHUNK), lambda i: (i, 0))
    head_shape = jax.ShapeDtypeStruct((nt * hrows, CHUNK), jnp.float32)
    return pl.pallas_call(
        _inproj_kernel,
        grid=(nt,),
        in_specs=_group_specs(D_MODEL) + [
            _mod_spec(l),
            tab_spec, tab_spec,
            _const_spec((D_MODEL, IN_COLS)),
            _const_spec((Q_RANK, MLA_HEADS * HEAD_SLAB)),
            _narrow_spec(q_norm, l // 2),
            _narrow_spec(kv_norm, l // 2),
            _vec_spec(vec_rows["norm_pre_mix"] + l),
            _vec_spec(vec_rows["ssd_dt_bias"] + l // 2),
            _vec_spec(vec_rows["ssd_a_log"] + l // 2),
        ],
        out_specs=[row(MLA_HEADS * HEAD_SLAB), row(KV_RANK), row(LANES), row(D_SSD), row(CONV_CH),
                   head_spec, head_spec, head_spec,
                   pl.BlockSpec((TM, KV_RANK), prompt_blk),
                   pl.BlockSpec((TM // SEQ, ROPE_DIM, SEQ), lambda i: (jnp.minimum(i, N_PT - 1), 0, 0))],
        out_shape=[
            jax.ShapeDtypeStruct((N_TOK, MLA_HEADS * HEAD_SLAB), jnp.bfloat16),
            jax.ShapeDtypeStruct((N_TOK, KV_RANK), jnp.float32),
            jax.ShapeDtypeStruct((N_TOK, LANES), jnp.float32),
            jax.ShapeDtypeStruct((N_TOK, D_SSD), jnp.float32),
            jax.ShapeDtypeStruct((N_TOK, CONV_CH), jnp.float32),
            head_shape, head_shape, head_shape,
            jax.ShapeDtypeStruct((N_PROMPT, KV_RANK), jnp.float32),
            jax.ShapeDtypeStruct((BATCH, ROPE_DIM, SEQ), jnp.float32),
        ],
        compiler_params=pltpu.CompilerParams(dimension_semantics=("arbitrary",), vmem_limit_bytes=VMEM_LIMIT),
        name="inproj",
    )(xp, xs, mod, *tabs, w_in, w_uq, q_norm, kv_norm, *([vec_table] * 3))


def _attn_kernel(*refs, lk_cache, lk_new, n_pack, n_cast):
    n_in = 7 if lk_cache else 5
    if lk_cache:
        q_ref, ckv_ref, kpe_ref, ckvc_ref, kpec_ref, wk_ref, wvt_ref = refs[:n_in]
    else:
        q_ref, ckv_ref, kpe_ref, wk_ref, wvt_ref = refs[:n_in]
    cast_src, o_ref = refs[n_in:n_in + n_cast], refs[n_in + n_cast]
    cast_dst = refs[n_in + n_cast + 1:n_in + 2 * n_cast + 1]
    k_scr, vt_scr = refs[n_in + 2 * n_cast + 1:]
    _cast_chunks(cast_src, cast_dst)
    lk = lk_cache + lk_new

    @pl.when(pl.program_id(1) == 0)
    def _expand_kv():
        def expand(ckv, kpe, r0):
            ckv_b = ckv.astype(jnp.bfloat16)
            kn = _dot(ckv_b, wk_ref[...])
            rows = slice(r0, r0 + ckv.shape[0])
            for hd in range(MLA_HEADS):
                k_scr[hd, rows, :] = (kn[:, hd * HEAD_SLAB:(hd + 1) * HEAD_SLAB] + kpe).astype(jnp.bfloat16)
            vt_scr[:, rows] = _dot_nt(wvt_ref[...], ckv_b).astype(jnp.bfloat16)

        step = 256
        for r0 in range(0, lk_cache, step):
            expand(ckvc_ref[0, r0:r0 + step, :], kpec_ref[0, r0:r0 + step, :], r0)
        for r0 in range(0, n_pack * lk_new, step):
            expand(ckv_ref[r0:r0 + step, :], kpe_ref[r0:r0 + step, :], lk_cache + r0)

    work = [(s, hd) for s in range(n_pack) for hd in range(MLA_HEADS)]
    scores = [_dot_nt(k_scr[hd, s * lk:(s + 1) * lk, :], q_ref[s * TQ:(s + 1) * TQ, hd * HEAD_SLAB:(hd + 1) * HEAD_SLAB])
              for s, hd in work]
    outs = []
    for (s, hd), s_t in zip(work, scores):
        p_t = jnp.exp2(s_t - jnp.max(s_t, axis=0, keepdims=True))
        den = jnp.sum(p_t, axis=0, keepdims=True)
        v_t = vt_scr[hd * V_DIM:(hd + 1) * V_DIM, s * lk:(s + 1) * lk]
        outs.append(_dot(v_t, p_t.astype(jnp.bfloat16)) / den)
    for s in range(n_pack):
        o_ref[s * TQ:(s + 1) * TQ, :] = jnp.concatenate(
            outs[s * MLA_HEADS:(s + 1) * MLA_HEADS], axis=0).T.astype(jnp.bfloat16)


def _attention(q, ckv_n, kpe, w_ukv, row_off, n_batch, seq, cache=None, n_pack=1, cast=()):
    nq = seq // TQ
    assert n_pack == 1 or (nq == 1 and cache is None and n_batch % n_pack == 0)
    n_batch //= n_pack
    lk_cache = 0 if cache is None else cache[0].shape[1]
    lk = n_pack * (lk_cache + seq)
    qblk = lambda b, qi: (row_off // (n_pack * TQ) + b * nq + qi, 0)
    sblk = lambda b, qi: (row_off // (n_pack * seq) + b, 0)
    in_specs = [
        pl.BlockSpec((n_pack * TQ, MLA_HEADS * HEAD_SLAB), qblk),
        pl.BlockSpec((n_pack * seq, KV_RANK), sblk),
        pl.BlockSpec((n_pack * seq, LANES), sblk),
    ]
    args = [q, ckv_n, kpe]
    if cache is not None:
        in_specs += [pl.BlockSpec((1, lk_cache, KV_RANK), lambda b, qi: (b, 0, 0)),
                     pl.BlockSpec((1, lk_cache, LANES), lambda b, qi: (b, 0, 0))]
        args += list(cache)
    in_specs += [_const_spec(w.shape) for w in w_ukv]
    args += list(w_ukv)
    assert not cast or n_batch * nq == N_CAST
    step = lambda b, qi: b * nq + qi
    in_specs += [_cast_in_spec(w, layer, step) for w, layer in cast]
    args += [w for w, _ in cast]
    cast_out = [_cast_out(w, step) for w, _ in cast]
    return pl.pallas_call(
        functools.partial(_attn_kernel, lk_cache=lk_cache, lk_new=seq, n_pack=n_pack, n_cast=len(cast)),
        grid=(n_batch, nq),
        in_specs=in_specs,
        out_specs=[pl.BlockSpec((n_pack * TQ, MLA_HEADS * V_DIM), lambda b, qi: (b * nq + qi, 0))]
        + [spec for spec, _ in cast_out],
        out_shape=[jax.ShapeDtypeStruct((n_batch * n_pack * seq, MLA_HEADS * V_DIM), jnp.bfloat16)]
        + [shape for _, shape in cast_out],
        scratch_shapes=[pltpu.VMEM((MLA_HEADS, lk, HEAD_SLAB), jnp.bfloat16),
                        pltpu.VMEM((MLA_HEADS * V_DIM, lk), jnp.bfloat16)],
        compiler_params=pltpu.CompilerParams(dimension_semantics=("arbitrary", "arbitrary"),
                                             vmem_limit_bytes=VMEM_LIMIT),
        name=f"attention_{seq}",
    )(*args)


def _ssd_kernel(*refs, seq, zero_init, stream_in, row_off):
    if stream_in:
        *refs, xbuf, sems = refs
    if zero_init:
        (xbc_ref, dts_ref, cumt_ref, wt_ref, cw_ref, cb_ref, dsk_ref,
         y_ref, hf_ref, hb_ref, xs_scr, c_scr, bt_scr, cum_scr, stf_scr, stb_scr) = refs
    else:
        (xbc_ref, dts_ref, cumt_ref, wt_ref, h0f_ref, h0b_ref, cw_ref, cb_ref, dsk_ref,
         y_ref, hf_ref, hb_ref, xs_scr, c_scr, bt_scr, cum_scr, stf_scr, stb_scr) = refs
    nc = seq // CHUNK
    gs = SSD_GROUPS * SSD_STATE
    nh2 = 2 * SSD_HEADS

    if stream_in:
        xbc_hbm = xbc_ref
        b = pl.program_id(0)
        slot = b % 2

        def chunk_copy(seq_idx, sl, c):
            src = xbc_hbm.at[pl.ds(pl.multiple_of(row_off + seq_idx * seq + c * CHUNK, CHUNK), CHUNK), :]
            dst = xbuf.at[sl, pl.ds(pl.multiple_of(c * CHUNK, CHUNK), CHUNK), :]
            return pltpu.make_async_copy(src, dst, sems.at[sl, c])

        @pl.when(b == 0)
        def _():
            for c in range(nc):
                chunk_copy(0, 0, c).start()

        @pl.when(b + 1 < pl.num_programs(0))
        def _():
            for c in range(nc):
                chunk_copy(b + 1, 1 - slot, c).start()

        chunk_copy(b, slot, 0).wait()
        xbc_ref = xbuf.at[slot]

    row = lax.broadcasted_iota(jnp.int32, (CHUNK, CHUNK), 0)
    col = lax.broadcasted_iota(jnp.int32, (CHUNK, CHUNK), 1)
    low_half = col < SSD_HEAD_DIM
    lower = row >= col
    upper = row <= col

    def prep_chunk(c, carry):
        r0 = pl.multiple_of(c * CHUNK, CHUNK)
        rows = pl.ds(r0, CHUNK)
        rows_prev = pl.ds(pl.multiple_of(jnp.maximum(r0 - HALO, 0), HALO), HALO)
        rows_next = pl.ds(pl.multiple_of(jnp.minimum(r0 + CHUNK, seq - HALO), HALO), HALO)
        if stream_in:

            @pl.when(c + 1 < nc)
            def _():
                chunk_copy(b, slot, c + 1).wait()

        def conv_tile(cs):
            prev = jnp.where(c > 0, xbc_ref[rows_prev, cs], 0.0)
            nxt = jnp.where(c < nc - 1, xbc_ref[rows_next, cs], 0.0)
            win = jnp.concatenate([prev, xbc_ref[rows, cs], nxt], axis=0)
            acc = jnp.broadcast_to(cb_ref[:, cs], (CHUNK, LANES))
            for k in range(CONV_W):
                lo = HALO - CONV_W // 2 + k
                acc = acc + cw_ref[k:k + 1, cs] * win[lo:lo + CHUNK, :]
            return _silu(acc)

        def x_tile(j, carry):
            cs = pl.ds(pl.multiple_of(j * LANES, LANES), LANES)
            u = conv_tile(cs)
            y_ref[rows, cs] = dsk_ref[:, cs] * u
            xs_scr[0, rows, cs] = jnp.where(low_half, u, 0.0).astype(jnp.bfloat16)
            xs_scr[1, rows, cs] = jnp.where(low_half, 0.0, u).astype(jnp.bfloat16)
            return carry

        lax.fori_loop(0, D_SSD // LANES, x_tile, 0)
        cum_t = cumt_ref[pl.ds(pl.multiple_of(c * nh2, nh2), nh2), :]
        cum_scr[rows, :] = jnp.concatenate(
            [cum_t, jnp.zeros((CHUNK - nh2, CHUNK), jnp.float32)], axis=0).T[:, :nh2]
        for g in range(SSD_GROUPS):
            b0 = pl.multiple_of(c * gs + g * SSD_STATE, SSD_STATE)
            bt_scr[pl.ds(b0, SSD_STATE), :] = conv_tile(slice(D_SSD + g * SSD_STATE, D_SSD + (g + 1) * SSD_STATE)).T
            c_scr[rows, g * SSD_STATE:(g + 1) * SSD_STATE] = conv_tile(
                slice(D_SSD + gs + g * SSD_STATE, D_SSD + gs + (g + 1) * SSD_STATE)).astype(jnp.bfloat16)
        return carry

    lax.fori_loop(0, nc, prep_chunk, 0)

    if zero_init:
        stf_scr[...] = jnp.zeros_like(stf_scr)
        stb_scr[...] = jnp.zeros_like(stb_scr)
    else:
        stf_scr[...] = h0f_ref[0].T
        stb_scr[...] = h0b_ref[0].T

    def scan_open(ci, st_scr):
        rows = pl.ds(pl.multiple_of(ci * CHUNK, CHUNK), CHUNK)
        c_b = c_scr[rows, :]
        st = st_scr[...]
        bts, cbms, zs = [], [], []
        for g in range(SSD_GROUPS):
            cg = c_b[:, g * SSD_STATE:(g + 1) * SSD_STATE]
            bt = bt_scr[pl.ds(pl.multiple_of(ci * gs + g * SSD_STATE, SSD_STATE), SSD_STATE), :]
            gcols = slice(g * SSD_HPG * SSD_HEAD_DIM, (g + 1) * SSD_HPG * SSD_HEAD_DIM)
            bts.append(bt)
            cbms.append(_dot(cg, bt.astype(jnp.bfloat16)))
            zs.append(_dot(cg, st[:, gcols].astype(jnp.bfloat16)))
        return st, bts, cbms, zs

    def scan_pairs(ci, st_scr, reverse, opened):
        st, bts, cbms, zs = opened
        lane0 = SSD_HEADS if reverse else 0
        causal = upper if reverse else lower
        last = 0 if reverse else CHUNK - 1
        rows = pl.ds(pl.multiple_of(ci * CHUNK, CHUNK), CHUNK)
        hrows = pl.ds(pl.multiple_of(ci * nh2, nh2), nh2)
        xs_lo = xs_scr[0, rows, :]
        xs_hi = xs_scr[1, rows, :]
        dts_t = dts_ref[hrows, :]
        cum_t = cumt_ref[hrows, :]
        w_t = wt_ref[hrows, :]
        cum = cum_scr[rows, :]
        for pair in range(SSD_HEADS // 2):
            g, jj = divmod(pair, SSD_HPG // 2)
            pcols = slice(pair * LANES, (pair + 1) * LANES)
            lhs_y, lhs_s, entry = [], [], []
            for hd in (2 * pair, 2 * pair + 1):
                ln = lane0 + hd
                cum_i = jnp.broadcast_to(cum[:, ln:ln + 1], (CHUNK, CHUNK))
                dec = jnp.exp2(jnp.where(causal, cum_i - cum_t[ln:ln + 1, :], -jnp.inf))
                lhs_y.append((cbms[g] * dec * dts_t[ln:ln + 1, :]).astype(jnp.bfloat16))
                lhs_s.append((bts[g] * w_t[ln:ln + 1, :]).astype(jnp.bfloat16))
                entry.append(jnp.exp2(cum_i))
            lhs = jnp.concatenate([jnp.concatenate(lhs_y, axis=1), jnp.concatenate(lhs_s, axis=1)], axis=0)
            out = _dot(lhs, jnp.concatenate([xs_lo[:, pcols], xs_hi[:, pcols]], axis=0))
            ea = jnp.where(low_half, entry[0], entry[1])
            y_ref[rows, pcols] += out[:CHUNK] + zs[g][:, jj * LANES:(jj + 1) * LANES] * ea
            st_scr[:, pcols] = ea[last:last + 1, :] * st[:, pcols] + out[CHUNK:]

    def both(c, carry):
        opened_f = scan_open(c, stf_scr)
        opened_b = scan_open(nc - 1 - c, stb_scr)
        scan_pairs(c, stf_scr, False, opened_f)
        scan_pairs(nc - 1 - c, stb_scr, True, opened_b)
        return carry

    lax.fori_loop(0, nc, both, 0)
    hf_ref[0] = stf_scr[...].T
    hb_ref[0] = stb_scr[...].T


def _ssd(i_ab, xbc, head_terms, h0, conv_w, vec_table, conv_b_row, d_skip, row_off, n_batch, seq):
    assert CONV_CH == D_MODEL
    hp = SSD_HEADS * SSD_HEAD_DIM
    gs = SSD_GROUPS * SSD_STATE
    nc = seq // CHUNK
    seq_blk = lambda b: (row_off // seq + b, 0)
    head_spec = pl.BlockSpec((nc * 2 * SSD_HEADS, CHUNK), seq_blk)
    st_spec = pl.BlockSpec((1, hp, SSD_STATE), lambda b: (b, 0, 0))
    st_shape = jax.ShapeDtypeStruct((n_batch, hp, SSD_STATE), jnp.float32)
    h0 = () if h0 is None else tuple(h0)
    stream_in = nc > 2
    xbc_spec = pl.BlockSpec(memory_space=pl.ANY) if stream_in else pl.BlockSpec((seq, CONV_CH), seq_blk)
    stream_scratch = [pltpu.VMEM((2, seq, CONV_CH), jnp.float32), pltpu.SemaphoreType.DMA((2, nc))] if stream_in else []
    return pl.pallas_call(
        functools.partial(_ssd_kernel, seq=seq, zero_init=not h0, stream_in=stream_in, row_off=row_off),
        grid=(n_batch,),
        in_specs=[xbc_spec, head_spec, head_spec, head_spec] + [st_spec] * len(h0) + [
            pl.BlockSpec((None, CONV_W, CONV_CH), lambda b: (i_ab, 0, 0), pipeline_mode=pl.Buffered(1)),
            _vec_spec(conv_b_row + i_ab), _const_spec((1, D_SSD))],
        out_specs=[pl.BlockSpec((seq, D_SSD), lambda b: (b, 0)), st_spec, st_spec],
        out_shape=[jax.ShapeDtypeStruct((n_batch * seq, D_SSD), jnp.float32), st_shape, st_shape],
        scratch_shapes=[pltpu.VMEM((2, seq, D_SSD), jnp.bfloat16),
                        pltpu.VMEM((seq, gs), jnp.bfloat16),
                        pltpu.VMEM((nc * gs, CHUNK), jnp.float32),
                        pltpu.VMEM((seq, 2 * SSD_HEADS), jnp.float32),
                        pltpu.VMEM((SSD_STATE, hp), jnp.float32),
                        pltpu.VMEM((SSD_STATE, hp), jnp.float32)] + stream_scratch,
        compiler_params=pltpu.CompilerParams(dimension_semantics=("arbitrary",), vmem_limit_bytes=VMEM_LIMIT),
        name=f"ssd_{seq}",
    )(xbc, *head_terms, *h0, conv_w, vec_table, d_skip)


def _post_mix(x, mix, mod_ref, npost_ref, npre_ref):
    d = D_MODEL
    gate_mix = _mod_part(mod_ref, 2)
    shf = _mod_part(mod_ref, 3)
    scf = _mod_part(mod_ref, 4)
    x1 = x + gate_mix * _rms(mix, npost_ref[...])
    return x1, (_rms(x1, npre_ref[...]) * (1.0 + scf) + shf).astype(jnp.bfloat16)


def _subtile_pipeline(n_sub, mixer_pre, mixer_dots, mod_ref, nffn_ref, wg_ref, wu_ref, wd_ref, interleave,
                      waits=None):
    gate_ffn = _mod_part(mod_ref, 5)
    ffn_up = lambda h: (_silu(_dot(h, wg_ref[...])) * _dot(h, wu_ref[...])).astype(jnp.bfloat16)
    ffn_down = lambda x1, hid: x1 + gate_ffn * _rms(_dot(hid, wd_ref[...]), nffn_ref[...])
    if not interleave:
        staged = [mixer_dots(r, mixer_pre(r)) for r in range(n_sub)]
        return [ffn_down(x1, ffn_up(h)) for x1, h in staged]
    outs = []
    x1, h = mixer_dots(0, mixer_pre(0))
    for r in range(n_sub):
        nxt_pre = mixer_pre(r + 1) if r + 1 < n_sub else None
        if waits and r == 0:
            waits[0]()
        hid = ffn_up(h)
        nxt = mixer_dots(r + 1, nxt_pre) if r + 1 < n_sub else None
        if waits and r == 0:
            waits[1]()
        outs.append(ffn_down(x1, hid))
        if nxt is not None:
            x1, h = nxt
    return outs


def _ffn_specs(l, vec_rows):
    return [_vec_spec(vec_rows[name] + l) for name in ("norm_post_mix", "norm_pre_ffn", "norm_post_ffn")] + [
        _const_spec((D_MODEL, D_FF)), _const_spec((D_MODEL, D_FF)), _const_spec((D_FF, D_MODEL))]


def _outproj_ffn_kernel(*refs, n_cast):
    (xp_ref, xs_ref, attp_ref, atts_ref, yp_ref, ys_ref, z_ref, mod_ref, sn_ref, wo_ref,
     npost_ref, npre_ref, nffn_ref, wg_ref, wu_ref, wd_ref) = refs[:16]
    cast_src, o_ref, cast_dst = refs[16:16 + n_cast], refs[16 + n_cast], refs[17 + n_cast:]
    i = pl.program_id(0)
    _cast_chunks(cast_src, cast_dst)
    gw = D_SSD // SSD_GROUPS
    sub = lambda r: slice(r * SUB_ROWS, (r + 1) * SUB_ROWS)

    def mixer_pre(r):
        yg = _pick_group(i, yp_ref, ys_ref, sub(r)) * _silu(z_ref[sub(r), :])
        parts = [_pick_group(i, attp_ref, atts_ref, sub(r))]
        for g in range(SSD_GROUPS):
            parts.append(_rms(yg[:, g * gw:(g + 1) * gw], sn_ref[:, g * gw:(g + 1) * gw]).astype(jnp.bfloat16))
        return jnp.concatenate(parts, axis=1)

    def mixer_dots(r, cat):
        return _post_mix(_pick_group(i, xp_ref, xs_ref, sub(r)), _dot(cat, wo_ref[...]),
                         mod_ref, npost_ref, npre_ref)

    outs = _subtile_pipeline(TM // SUB_ROWS, mixer_pre, mixer_dots, mod_ref, nffn_ref, wg_ref, wu_ref, wd_ref,
                             interleave=False)
    for r, res in enumerate(outs):
        o_ref[sub(r), :] = res


def _outproj_ffn(l, i_ab, vec_table, vec_rows, ssd_norm, xp, xs, att_p, att_s, y_p, y_s, z, mod, w_out, wg, wu, wd,
                 cast):
    d_cat = MLA_HEADS * V_DIM + D_SSD
    assert not cast or N_TILES == N_CAST
    cast_out = [_cast_out(w) for w, _ in cast]
    return pl.pallas_call(
        functools.partial(_outproj_ffn_kernel, n_cast=len(cast)),
        grid=(N_TILES,),
        in_specs=(_group_specs(D_MODEL) + _group_specs(MLA_HEADS * V_DIM) + _group_specs(D_SSD)
                  + [_row_spec(D_SSD), _mod_spec(l), _narrow_spec(ssd_norm, i_ab),
                     _const_spec((d_cat, D_MODEL))] + _ffn_specs(l, vec_rows)
                  + [_cast_in_spec(w, layer) for w, layer in cast]),
        out_specs=[_row_spec(D_MODEL)] + [spec for spec, _ in cast_out],
        out_shape=[jax.ShapeDtypeStruct((N_TOK, D_MODEL), jnp.float32)] + [shape for _, shape in cast_out],
        compiler_params=pltpu.CompilerParams(dimension_semantics=("arbitrary",), vmem_limit_bytes=VMEM_LIMIT),
        name="outproj_ffn",
    )(xp, xs, att_p, att_s, y_p, y_s, z, mod, ssd_norm, w_out, vec_table, vec_table, vec_table, wg, wu, wd,
      *[w for w, _ in cast])


def _pool_ffn_kernel(x_ref, xp_ref, xn_ref, mod_ref, nmix_ref, pw_ref, ps_ref,
                     npost_ref, npre_ref, nffn_ref, wg_hbm, wu_hbm, wd_hbm, op_ref, os_ref,
                     wg_ref, wu_ref, wd_ref, wsem):
    i = pl.program_id(0)
    w_copies = [pltpu.make_async_copy(src, dst, wsem.at[k])
                for k, (src, dst) in enumerate(((wg_hbm, wg_ref), (wu_hbm, wu_ref), (wd_hbm, wd_ref)))]

    @pl.when(i == 0)
    def _():
        for cp in w_copies:
            cp.start()

    @pl.when(i == 0)
    def _():
        os_ref[...] = jnp.zeros_like(os_ref)

    seq = jnp.where(i < N_PROMPT // TM, SEQ, DEC_SEQ)
    pos0 = (i * TM) % seq
    sh = _mod_part(mod_ref, 0)
    sc = _mod_part(mod_ref, 1)
    hmod = lambda v: _rms(v, nmix_ref[...]) * (1.0 + sc) + sh
    n_rows = POOL_SUB + 2 * HALO

    def shifted(v, s):
        return pltpu.roll(v, n_rows - s, 0)

    def mixer_pre(s):
        lo, hi = s * POOL_SUB, (s + 1) * POOL_SUB
        pos_s = (pos0 + lo) % seq
        h = hmod(x_ref[lo:hi, :])
        before = hmod(xp_ref[...] if s == 0 else x_ref[lo - HALO:lo, :])
        after = hmod(xn_ref[...] if hi == TM else x_ref[hi:hi + HALO, :])
        before = jnp.where(pos_s > 0, before, 0.0)
        after = jnp.where(pos_s + POOL_SUB < seq, after, 0.0)
        padded = jnp.concatenate([before, h, after], axis=0)
        pos = pos_s + lax.broadcasted_iota(jnp.int32, (POOL_SUB, 1), 0)
        pooled = []
        for gi, w in enumerate(POOL_WINDOWS):
            cols = slice(gi * POOL_GC, (gi + 1) * POOL_GC)
            t = padded[:, cols]
            span = 1
            while span < w:
                t = t + shifted(t, span)
                span *= 2
            lead = HALO - w // 2
            win_sum = (shifted(t, lead) if lead else t)[:POOL_SUB, :]
            cnt = (jnp.minimum(pos + w // 2, seq) - jnp.maximum(pos - w // 2, 0)).astype(jnp.float32)
            pooled.append((win_sum / cnt - h[:, cols]).astype(jnp.bfloat16))
        return pooled

    def mixer_dots(s, pooled):
        mix = jnp.concatenate([_dot(p, pw_ref[gi]) for gi, p in enumerate(pooled)], axis=1) * ps_ref[...]
        return _post_mix(x_ref[s * POOL_SUB:(s + 1) * POOL_SUB, :], mix, mod_ref, npost_ref, npre_ref)

    def tile(waits):
        return jnp.concatenate(_subtile_pipeline(TM // POOL_SUB, mixer_pre, mixer_dots, mod_ref, nffn_ref,
                                                 wg_ref, wu_ref, wd_ref, interleave=True, waits=waits), axis=0)

    @pl.when(i == 0)
    def _():
        op_ref[...] = tile((lambda: [cp.wait() for cp in w_copies[:2]], w_copies[2].wait))

    @pl.when((i > 0) & (i < N_PT))
    def _():
        op_ref[...] = tile(None)

    @pl.when(i >= N_PT)
    def _():
        os_ref[...] = tile(None)


def _pool_ffn(l, j_c, vec_table, vec_rows, xa, mod, pool_w, wg, wu, wd):
    hb = TM // HALO
    nh = N_TOK // HALO
    return pl.pallas_call(
        _pool_ffn_kernel,
        grid=(N_TILES,),
        in_specs=[_row_spec(D_MODEL),
                  pl.BlockSpec((HALO, D_MODEL), lambda i: (jnp.maximum(i * hb - 1, 0), 0)),
                  pl.BlockSpec((HALO, D_MODEL), lambda i: (jnp.minimum((i + 1) * hb, nh - 1), 0)),
                  _mod_spec(l),
                  _vec_spec(vec_rows["norm_pre_mix"] + l),
                  _const_spec((len(POOL_WINDOWS), POOL_GC, POOL_GC)),
                  _vec_spec(vec_rows["pool_scale"] + j_c)] + _ffn_specs(l, vec_rows)[:3]
        + [pl.BlockSpec(memory_space=pl.ANY)] * 3,
        scratch_shapes=[pltpu.VMEM((D_MODEL, D_FF), jnp.bfloat16), pltpu.VMEM((D_MODEL, D_FF), jnp.bfloat16),
                        pltpu.VMEM((D_FF, D_MODEL), jnp.bfloat16), pltpu.SemaphoreType.DMA((3,))],
        out_specs=_group_specs(D_MODEL),
        out_shape=[jax.ShapeDtypeStruct((N_PROMPT, D_MODEL), jnp.float32),
                   jax.ShapeDtypeStruct((N_SAMPLE, D_MODEL), jnp.float32)],
        compiler_params=pltpu.CompilerParams(dimension_semantics=("arbitrary",), vmem_limit_bytes=VMEM_LIMIT),
        name="pool_ffn",
    )(xa, xa, xa, mod, vec_table, pool_w, vec_table, vec_table, vec_table, vec_table, wg, wu, wd)


def _rope_tables():
    f32 = np.float32
    rows = DEC_SEQ // GRID_W
    r = np.repeat(np.arange(rows, dtype=f32), GRID_W)
    c = np.tile(np.arange(GRID_W, dtype=f32), rows)
    half = ROPE_DIM // 2
    inv_freq = np.power(f32(ROPE_THETA), -np.arange(0, half, 2, dtype=f32) / f32(half)).astype(f32)
    ang = np.concatenate([r[:, None] * inv_freq, c[:, None] * inv_freq], axis=-1).astype(f32)
    cos, sin = np.cos(ang).astype(f32), np.sin(ang).astype(f32)
    ones = np.ones((DEC_SEQ, KPE_LANE0), f32)
    zl = np.zeros((DEC_SEQ, KPE_LANE0), f32)
    zr = np.zeros((DEC_SEQ, LANES - KPE_LANE0 - ROPE_DIM), f32)
    cos_t = np.concatenate([ones, cos, cos, zr], axis=1)
    sin_t = np.concatenate([zl, -sin, sin, zr], axis=1)
    keep = np.concatenate([np.ones((TM, KPE_LANE0 + ROPE_DIM), f32), np.zeros((TM, LANES - KPE_LANE0 - ROPE_DIM), f32)],
                          axis=1)
    return (jnp.asarray(np.concatenate([cos_t, keep], axis=0)),
            jnp.asarray(np.concatenate([sin_t, np.zeros((TM, LANES), f32)], axis=0)))


def _kpe_slab(k):
    pad = [(0, 0)] * (k.ndim - 1) + [(KPE_LANE0, LANES - KPE_LANE0 - ROPE_DIM)]
    return jnp.pad(k, pad)


def _layout_in_proj(wt):
    o = np.cumsum((0, Q_RANK, KV_RANK, ROPE_DIM, D_SSD, CONV_CH, SSD_HEADS, SSD_HEADS))
    cq, ckv, kpe, z, xbc, dtf, dtb = (wt[o[k]:o[k + 1], :] for k in range(7))
    zeros = lambda n: jnp.zeros((n, wt.shape[1]), wt.dtype)
    half = ROPE_DIM // 2
    assert KPE_LANE0 + 2 * ROPE_DIM == LANES
    stacked = jnp.concatenate([cq, ckv, z, xbc,
                               zeros(KPE_LANE0), kpe, kpe[half:], kpe[:half],
                               dtf, dtb, zeros(LANES - 2 * SSD_HEADS)], axis=0)
    return stacked.T.astype(jnp.bfloat16)


def _layout_uq(w):
    hw, half = NOPE_DIM + ROPE_DIM, ROPE_DIM // 2
    assert hw + ROPE_DIM == HEAD_SLAB
    pieces = []
    for hd in range(MLA_HEADS):
        s = w[:, hd * hw:(hd + 1) * hw]
        pieces += [s, s[:, NOPE_DIM + half:], s[:, NOPE_DIM:NOPE_DIM + half]]
    return jnp.concatenate(pieces, axis=1).astype(jnp.bfloat16)


def _layout_ukv(w):
    hw = NOPE_DIM + V_DIM
    pad = jnp.zeros((w.shape[0], HEAD_SLAB - NOPE_DIM), w.dtype)
    kn, v = [], []
    for hd in range(MLA_HEADS):
        kn += [w[:, hd * hw:hd * hw + NOPE_DIM], pad]
        v.append(w[:, hd * hw + NOPE_DIM:(hd + 1) * hw])
    return (jnp.concatenate(kn, axis=1).astype(jnp.bfloat16),
            jnp.concatenate(v, axis=1).T.astype(jnp.bfloat16))


def kernel(x_prompt, x_sample, c, cache_mla_ckv, cache_mla_krope, state_ssd_fwd, state_ssd_bwd, c_ctx, w_mod, b_mod, norm_pre_mix, norm_post_mix, norm_pre_ffn, norm_post_ffn, w_in_ab, q_norm, w_uq, kv_norm, w_ukv, ssd_conv_w, ssd_conv_b, ssd_dt_bias_fwd, ssd_dt_bias_bwd, ssd_a_log_fwd, ssd_a_log_bwd, ssd_d, ssd_norm, w_out_ab, pool_w, pool_scale, ffn_w_gate, ffn_w_up, ffn_w_down):
    f32, bf16 = jnp.float32, jnp.bfloat16
    assert DEPTH == 2
    xp = x_prompt.reshape(N_PROMPT, D_MODEL)
    xs = x_sample.reshape(N_SAMPLE, D_MODEL)
    assert 1 + c.shape[0] == N_MODVEC
    mod, w_in, w_uq_slabs, *w_kv = _modulation(c_ctx, c, w_mod, b_mod, jnp.swapaxes(w_in_ab, 1, 2), w_uq, w_ukv)
    tabs = _rope_tables()
    hp = SSD_HEADS * SSD_HEAD_DIM
    row = lambda v: v.reshape(1, -1)
    new_ckv, new_kpe, new_hf, new_hb = [], [], [], []

    vec_table, vec_rows = _pack_vectors(
        norm_pre_mix=norm_pre_mix, norm_post_mix=norm_post_mix, norm_pre_ffn=norm_pre_ffn, norm_post_ffn=norm_post_ffn,
        ssd_conv_b=ssd_conv_b, pool_scale=pool_scale,
        ssd_dt_bias=jnp.concatenate([ssd_dt_bias_fwd, ssd_dt_bias_bwd], axis=1),
        ssd_a_log=jnp.concatenate([ssd_a_log_fwd, ssd_a_log_bwd], axis=1))
    ffn_w_f32 = (ffn_w_gate, ffn_w_up, ffn_w_down)
    for l in range(DEPTH):
        if l % 2 == 0:
            i = l // 2
            q, ckv_n, kpe, z, xbc, dts_t, cum_t, w_t, ckv_prompt, kpe_prompt = _inproj(
                l, vec_table, vec_rows, xp, xs, mod, tabs, w_in, w_uq_slabs, q_norm, kv_norm)
            heads = (dts_t, cum_t, w_t)
            (att_p,) = _attention(q, ckv_n, kpe, w_kv, 0, BATCH, SEQ, n_pack=PROMPT_PACK)
            att_s, w_out, *ffn_w = _attention(q, ckv_n, kpe, w_kv, N_PROMPT, DEC_BATCH, DEC_SEQ,
                                              cache=(cache_mla_ckv[:, i], _kpe_slab(cache_mla_krope[:, i])),
                                              cast=[(w_out_ab, i)] + [(w, l) for w in ffn_w_f32])
            ssd_args = (ssd_conv_w, vec_table, vec_rows["ssd_conv_b"], row(jnp.repeat(ssd_d[i], SSD_HEAD_DIM)))
            y_p, hf, hb = _ssd(i, xbc, heads, None, *ssd_args, 0, BATCH, SEQ)
            y_s, _, _ = _ssd(i, xbc, heads, (state_ssd_fwd[:, i].reshape(DEC_BATCH, hp, SSD_STATE),
                                             state_ssd_bwd[:, i].reshape(DEC_BATCH, hp, SSD_STATE)),
                             *ssd_args, N_PROMPT, DEC_BATCH, DEC_SEQ)
            xa, *ffn_w = _outproj_ffn(l, i, vec_table, vec_rows, ssd_norm, xp, xs, att_p, att_s, y_p, y_s, z, mod, w_out, *ffn_w,
                                      cast=[(w, l + 1) for w in ffn_w_f32])
            new_ckv.append(ckv_prompt.reshape(BATCH, SEQ, KV_RANK))
            new_kpe.append(jnp.swapaxes(kpe_prompt, 1, 2))
            new_hf.append(hf.reshape(BATCH, SSD_HEADS, SSD_HEAD_DIM, SSD_STATE))
            new_hb.append(hb.reshape(BATCH, SSD_HEADS, SSD_HEAD_DIM, SSD_STATE))
        else:
            j = l // 2
            yp, ys = _pool_ffn(l, j, vec_table, vec_rows, xa, mod, pool_w[j].astype(bf16), *ffn_w)

    return (yp.reshape(BATCH, SEQ, D_MODEL), ys.reshape(DEC_BATCH, DEC_SEQ, D_MODEL),
            jnp.stack(new_ckv, axis=1), jnp.stack(new_kpe, axis=1),
            jnp.stack(new_hf, axis=1), jnp.stack(new_hb, axis=1))
```
